```python
import jax, jax.numpy as jnp
from jax import lax
import numpy as np

D_MODEL = 1024
BATCH = 8
SEQ = 4096
DEPTH = 1

CHUNK = 64
RET_HEADS = 4
RET_QK_DIM = 256
RET_V_DIM = 256
RET_QK = RET_HEADS * RET_QK_DIM
RET_V = RET_HEADS * RET_V_DIM
POOL_WINDOWS = (2, 4, 8, 16)
POOL_GROUPS = 4
POOL_GROUP_DIM = 256
POOL_WIDTH = POOL_GROUPS * POOL_GROUP_DIM
N_BRANCH = 2
IN_WIDTH = 2 * RET_QK + 2 * RET_V + POOL_WIDTH + N_BRANCH * D_MODEL
D_FF = 2816
ROPE_BASE = 10000.0
NORM_EPS = 1e-6
FFN_RES_WEIGHT = 0.5

kernel_name = "hybrid_retention_pool_macaron"


def rmsnorm(x, g):
    xf = x.astype(jnp.float32)
    y = xf * lax.rsqrt(jnp.mean(xf * xf, axis=-1, keepdims=True) + NORM_EPS)
    return (y * g.astype(jnp.float32)).astype(x.dtype)


def swiglu_ffn(x, w_in, w_out):
    gate, up = jnp.split(x @ w_in, 2, axis=-1)
    return (jax.nn.silu(gate) * up) @ w_out


def rotary(x):
    s, d = x.shape[1], x.shape[-1]
    half = d // 2
    inv_freq = ROPE_BASE ** (-jnp.arange(half, dtype=jnp.float32) / half)
    ang = jnp.arange(s, dtype=jnp.float32)[:, None] * inv_freq[None, :]
    cos = jnp.cos(ang)[None, :, None, :].astype(x.dtype)
    sin = jnp.sin(ang)[None, :, None, :].astype(x.dtype)
    x1, x2 = x[..., :half], x[..., half:]
    return jnp.concatenate([x1 * cos - x2 * sin, x1 * sin + x2 * cos], axis=-1)


def retention(q, k, v):
    b, s, h, dk = q.shape
    dv = v.shape[-1]
    nc = s // CHUNK
    log_gamma = jnp.log(1.0 - 2.0 ** (-5.0 - jnp.arange(h, dtype=jnp.float32)))
    idx = jnp.arange(CHUNK, dtype=jnp.float32)
    inner_decay = jnp.exp(log_gamma[:, None, None] * jnp.abs(idx[:, None] - idx[None, :]))
    q_decay = jnp.exp(log_gamma[None, :] * (idx[:, None] + 1.0))
    k_decay = jnp.exp(log_gamma[None, :] * (CHUNK - 1.0 - idx[:, None]))
    chunk_decay = jnp.exp(log_gamma * CHUNK)

    qc = q.reshape(b, nc, CHUNK, h, dk)
    kc = k.reshape(b, nc, CHUNK, h, dk)
    vc = v.reshape(b, nc, CHUNK, h, dv)

    scores = jnp.einsum("bnchd,bnshd->bnhcs", qc, kc) * inner_decay[None, None]
    inner = jnp.einsum("bnhcs,bnshe->bnche", scores, vc)

    def step(state, inp):
        q_i, k_i, v_i = inp
        cross = jnp.einsum("bchd,bhde->bche", q_i * q_decay[None, :, :, None], state)
        new_state = state * chunk_decay[None, :, None, None] + jnp.einsum(
            "bchd,bche->bhde", k_i * k_decay[None, :, :, None], v_i)
        return new_state, cross

    state0 = jnp.zeros((b, h, dk, dv), jnp.float32)
    xs = (qc.transpose(1, 0, 2, 3, 4), kc.transpose(1, 0, 2, 3, 4), vc.transpose(1, 0, 2, 3, 4))
    _, cross = lax.scan(step, state0, xs)
    out = inner + cross.transpose(1, 0, 2, 3, 4)
    return out.reshape(b, s, h, dv)


def multiscale_pool(p, w_group, scale):
    b, s, _ = p.shape
    pf = p.astype(jnp.float32).reshape(b, s, POOL_GROUPS, POOL_GROUP_DIM)
    cs = jnp.concatenate([jnp.zeros((b, 1, POOL_GROUPS, POOL_GROUP_DIM), jnp.float32),
                          jnp.cumsum(pf, axis=1)], axis=1)
    t = jnp.arange(s, dtype=jnp.float32)
    outs = []
    for g, w in enumerate(POOL_WINDOWS):
        cs_g = cs[:, :, g]
        shifted = jnp.pad(cs_g[:, : s + 1 - w], ((0, 0), (w - 1, 0), (0, 0)))
        count = jnp.minimum(t + 1.0, float(w))[None, :, None]
        outs.append((cs_g[:, 1:] - shifted) / count - pf[:, :, g])
    pooled = jnp.stack(outs, axis=2)
    mixed = jnp.einsum("bsgc,gcd->bsgd", pooled, w_group.astype(jnp.float32))
    return (mixed.reshape(b, s, POOL_WIDTH) * scale.astype(jnp.float32)).astype(p.dtype)


def _fwd_setup_inputs(seed: int = 0) -> dict:
    key = jax.random.key(seed)
    ks = jax.random.split(key, 20)
    f32 = jnp.float32

    def nrm(k, shape, fan_in):
        return jax.random.normal(k, shape, f32) * (fan_in ** -0.5)

    def gain(k, shape):
        return 1.0 + 0.02 * jax.random.normal(k, shape, f32)

    return {
        "x": jax.random.normal(ks[0], (BATCH, SEQ, D_MODEL), f32),
        "norm_ffn1": gain(ks[1], (DEPTH, D_MODEL)),
        "ffn1_w_in": nrm(ks[2], (DEPTH, D_MODEL, 2 * D_FF), D_MODEL),
        "ffn1_w_out": nrm(ks[3], (DEPTH, D_FF, D_MODEL), D_FF),
        "norm_mix": gain(ks[4], (DEPTH, D_MODEL)),
        "w_in": nrm(ks[5], (DEPTH, D_MODEL, IN_WIDTH), D_MODEL),
        "gate_bias": 0.02 * jax.random.normal(ks[6], (DEPTH, N_BRANCH, D_MODEL), f32),
        "pool_w": nrm(ks[7], (DEPTH, POOL_GROUPS, POOL_GROUP_DIM, POOL_GROUP_DIM), POOL_GROUP_DIM),
        "pool_scale": gain(ks[8], (DEPTH, POOL_WIDTH)),
        "w_ret_up": nrm(ks[9], (DEPTH, RET_V, D_MODEL), RET_V),
        "w_pool_up": nrm(ks[10], (DEPTH, POOL_WIDTH, D_MODEL), POOL_WIDTH),
        "w_out": nrm(ks[11], (DEPTH, D_MODEL, D_MODEL), D_MODEL),
        "norm_ffn2": gain(ks[12], (DEPTH, D_MODEL)),
        "ffn2_w_in": nrm(ks[13], (DEPTH, D_MODEL, 2 * D_FF), D_MODEL),
        "ffn2_w_out": nrm(ks[14], (DEPTH, D_FF, D_MODEL), D_FF),
        "norm_final": gain(ks[15], (D_MODEL,)),
    }


def _fwd_reference(x, norm_ffn1, ffn1_w_in, ffn1_w_out, norm_mix, w_in, gate_bias, pool_w,
              pool_scale, w_ret_up, w_pool_up, w_out, norm_ffn2, ffn2_w_in, ffn2_w_out,
              norm_final):
    b, s, _ = x.shape
    split_points = [RET_QK, 2 * RET_QK, 2 * RET_QK + RET_V, 2 * RET_QK + 2 * RET_V,
                    2 * RET_QK + 2 * RET_V + POOL_WIDTH]
    h = x
    for l in range(DEPTH):
        h = h + FFN_RES_WEIGHT * swiglu_ffn(rmsnorm(h, norm_ffn1[l]), ffn1_w_in[l], ffn1_w_out[l])

        u = rmsnorm(h, norm_mix[l])
        proj = u @ w_in[l]
        q, k, v, g_ret, p, gates = jnp.split(proj, split_points, axis=-1)
        q = rotary(q.reshape(b, s, RET_HEADS, RET_QK_DIM))
        k = rotary(k.reshape(b, s, RET_HEADS, RET_QK_DIM)) * (RET_QK_DIM ** -0.5)
        v = v.reshape(b, s, RET_HEADS, RET_V_DIM)
        ret = retention(q.astype(jnp.float32), k.astype(jnp.float32), v.astype(jnp.float32))
        ret = ret * lax.rsqrt(jnp.mean(ret * ret, axis=-1, keepdims=True) + NORM_EPS)
        ret = ret.reshape(b, s, RET_V).astype(proj.dtype) * jax.nn.silu(g_ret)

        pool_out = multiscale_pool(p, pool_w[l], pool_scale[l])

        gate = jax.nn.sigmoid(gates.reshape(b, s, N_BRANCH, D_MODEL) + gate_bias[l])
        merged = gate[:, :, 0] * (ret @ w_ret_up[l]) + gate[:, :, 1] * (pool_out @ w_pool_up[l])
        h = h + merged @ w_out[l]

        h = h + FFN_RES_WEIGHT * swiglu_ffn(rmsnorm(h, norm_ffn2[l]), ffn2_w_in[l], ffn2_w_out[l])
    return rmsnorm(h, norm_final)


import jax as _jax
import jax.numpy as _jnp

TWIN_FORMAT = 'train_step'
FWD_PARAMS = ['x', 'norm_ffn1', 'ffn1_w_in', 'ffn1_w_out', 'norm_mix', 'w_in', 'gate_bias', 'pool_w', 'pool_scale', 'w_ret_up', 'w_pool_up', 'w_out', 'norm_ffn2', 'ffn2_w_in', 'ffn2_w_out', 'norm_final']
TWIN_WEIGHTS = ['norm_ffn1', 'ffn1_w_in', 'ffn1_w_out', 'norm_mix', 'w_in', 'gate_bias', 'pool_w', 'pool_scale', 'w_ret_up', 'w_pool_up', 'w_out', 'norm_ffn2', 'ffn2_w_in', 'ffn2_w_out', 'norm_final']
TWIN_DIFF_INPUT = 'x'
TWIN_INPUTS = ['x', 'norm_ffn1', 'ffn1_w_in', 'ffn1_w_out', 'norm_mix', 'w_in', 'gate_bias', 'pool_w', 'pool_scale', 'w_ret_up', 'w_pool_up', 'w_out', 'norm_ffn2', 'ffn2_w_in', 'ffn2_w_out', 'norm_final', 'loss_target', 'm_norm_ffn1', 'm_ffn1_w_in', 'm_ffn1_w_out', 'm_norm_mix', 'm_w_in', 'm_gate_bias', 'm_pool_w', 'm_pool_scale', 'm_w_ret_up', 'm_w_pool_up', 'm_w_out', 'm_norm_ffn2', 'm_ffn2_w_in', 'm_ffn2_w_out', 'm_norm_final', 'v_norm_ffn1', 'v_ffn1_w_in', 'v_ffn1_w_out', 'v_norm_mix', 'v_w_in', 'v_gate_bias', 'v_pool_w', 'v_pool_scale', 'v_w_ret_up', 'v_w_pool_up', 'v_w_out', 'v_norm_ffn2', 'v_ffn2_w_in', 'v_ffn2_w_out', 'v_norm_final']
TWIN_OUTPUTS = ['loss', 'grad_x', 'grad_norm_ffn1', 'grad_ffn1_w_in', 'grad_ffn1_w_out', 'grad_norm_mix', 'grad_w_in', 'grad_gate_bias', 'grad_pool_w', 'grad_pool_scale', 'grad_w_ret_up', 'grad_w_pool_up', 'grad_w_out', 'grad_norm_ffn2', 'grad_ffn2_w_in', 'grad_ffn2_w_out', 'grad_norm_final', 'delta_norm_ffn1', 'delta_ffn1_w_in', 'delta_ffn1_w_out', 'delta_norm_mix', 'delta_w_in', 'delta_gate_bias', 'delta_pool_w', 'delta_pool_scale', 'delta_w_ret_up', 'delta_w_pool_up', 'delta_w_out', 'delta_norm_ffn2', 'delta_ffn2_w_in', 'delta_ffn2_w_out', 'delta_norm_final', 'new_m_norm_ffn1', 'new_m_ffn1_w_in', 'new_m_ffn1_w_out', 'new_m_norm_mix', 'new_m_w_in', 'new_m_gate_bias', 'new_m_pool_w', 'new_m_pool_scale', 'new_m_w_ret_up', 'new_m_w_pool_up', 'new_m_w_out', 'new_m_norm_ffn2', 'new_m_ffn2_w_in', 'new_m_ffn2_w_out', 'new_m_norm_final', 'new_v_norm_ffn1', 'new_v_ffn1_w_in', 'new_v_ffn1_w_out', 'new_v_norm_mix', 'new_v_w_in', 'new_v_gate_bias', 'new_v_pool_w', 'new_v_pool_scale', 'new_v_w_ret_up', 'new_v_w_pool_up', 'new_v_w_out', 'new_v_norm_ffn2', 'new_v_ffn2_w_in', 'new_v_ffn2_w_out', 'new_v_norm_final']
TWIN_LEAF_KINDS = {'loss': 'loss', 'grad_x': 'grad_x', 'grad_norm_ffn1': 'grad_w', 'grad_ffn1_w_in': 'grad_w', 'grad_ffn1_w_out': 'grad_w', 'grad_norm_mix': 'grad_w', 'grad_w_in': 'grad_w', 'grad_gate_bias': 'grad_w', 'grad_pool_w': 'grad_w', 'grad_pool_scale': 'grad_w', 'grad_w_ret_up': 'grad_w', 'grad_w_pool_up': 'grad_w', 'grad_w_out': 'grad_w', 'grad_norm_ffn2': 'grad_w', 'grad_ffn2_w_in': 'grad_w', 'grad_ffn2_w_out': 'grad_w', 'grad_norm_final': 'grad_w', 'delta_norm_ffn1': 'delta_w', 'delta_ffn1_w_in': 'delta_w', 'delta_ffn1_w_out': 'delta_w', 'delta_norm_mix': 'delta_w', 'delta_w_in': 'delta_w', 'delta_gate_bias': 'delta_w', 'delta_pool_w': 'delta_w', 'delta_pool_scale': 'delta_w', 'delta_w_ret_up': 'delta_w', 'delta_w_pool_up': 'delta_w', 'delta_w_out': 'delta_w', 'delta_norm_ffn2': 'delta_w', 'delta_ffn2_w_in': 'delta_w', 'delta_ffn2_w_out': 'delta_w', 'delta_norm_final': 'delta_w', 'new_m_norm_ffn1': 'new_m', 'new_m_ffn1_w_in': 'new_m', 'new_m_ffn1_w_out': 'new_m', 'new_m_norm_mix': 'new_m', 'new_m_w_in': 'new_m', 'new_m_gate_bias': 'new_m', 'new_m_pool_w': 'new_m', 'new_m_pool_scale': 'new_m', 'new_m_w_ret_up': 'new_m', 'new_m_w_pool_up': 'new_m', 'new_m_w_out': 'new_m', 'new_m_norm_ffn2': 'new_m', 'new_m_ffn2_w_in': 'new_m', 'new_m_ffn2_w_out': 'new_m', 'new_m_norm_final': 'new_m', 'new_v_norm_ffn1': 'new_v', 'new_v_ffn1_w_in': 'new_v', 'new_v_ffn1_w_out': 'new_v', 'new_v_norm_mix': 'new_v', 'new_v_w_in': 'new_v', 'new_v_gate_bias': 'new_v', 'new_v_pool_w': 'new_v', 'new_v_pool_scale': 'new_v', 'new_v_w_ret_up': 'new_v', 'new_v_w_pool_up': 'new_v', 'new_v_w_out': 'new_v', 'new_v_norm_ffn2': 'new_v', 'new_v_ffn2_w_in': 'new_v', 'new_v_ffn2_w_out': 'new_v', 'new_v_norm_final': 'new_v'}


def _forward(args):
    return _fwd_reference(*[args[k] for k in FWD_PARAMS])


def _output_shape():
    out = _jax.eval_shape(lambda: _forward(_fwd_setup_inputs(0)))
    return out.shape, out.dtype

N_MICROBATCH = 1
ADAM_LR = 0.001
ADAM_B1 = 0.9
ADAM_B2 = 0.999
ADAM_EPS = 1e-08
ADAM_WD = 0.01
ADAM_STEP = 10
PER_EXAMPLE_BATCH_AXIS = {'x': 0, 'loss_target': 0}
SHARED_INPUTS = []
_WEIGHT_DTYPES = {'norm_ffn1': _jnp.float32, 'ffn1_w_in': _jnp.float32, 'ffn1_w_out': _jnp.float32, 'norm_mix': _jnp.float32, 'w_in': _jnp.float32, 'gate_bias': _jnp.float32, 'pool_w': _jnp.float32, 'pool_scale': _jnp.float32, 'w_ret_up': _jnp.float32, 'w_pool_up': _jnp.float32, 'w_out': _jnp.float32, 'norm_ffn2': _jnp.float32, 'ffn2_w_in': _jnp.float32, 'ffn2_w_out': _jnp.float32, 'norm_final': _jnp.float32}
MOMENT_SCALE = {'norm_ffn1': 8.712842e-02, 'ffn1_w_in': 3.590202e-02, 'ffn1_w_out': 5.852983e-02, 'norm_mix': 1.261111e-01, 'w_in': 4.871221e-02, 'gate_bias': 2.511580e-02, 'pool_w': 7.317282e-02, 'pool_scale': 7.526921e-02, 'w_ret_up': 4.944220e-02, 'w_pool_up': 7.312245e-02, 'w_out': 8.887394e-02, 'norm_ffn2': 6.505897e-02, 'ffn2_w_in': 2.649699e-02, 'ffn2_w_out': 4.319177e-02, 'norm_final': 3.194538e+01}


def _to_microbatches(a, axis):
    t = _jnp.moveaxis(a, axis, 0)
    t = t.reshape((N_MICROBATCH, t.shape[0] // N_MICROBATCH) + t.shape[1:])
    return _jnp.moveaxis(t, 1, axis + 1)


def setup_inputs(seed: int = 0) -> dict:
    inp = _fwd_setup_inputs(seed)
    key = _jax.random.fold_in(_jax.random.key(seed), 7919)
    shape, _ = _output_shape()
    out = dict(inp)
    out["loss_target"] = _jax.random.normal(_jax.random.fold_in(key, 0), shape, _jnp.float32)
    for i, name in enumerate(TWIN_WEIGHTS):
        w = inp[name].astype(_jnp.float32)
        if MOMENT_SCALE is None:
            s = _jnp.sqrt(_jnp.mean(_jnp.square(w)) + 1e-30)
        else:
            s = MOMENT_SCALE[name]
        km, kv = _jax.random.split(_jax.random.fold_in(key, i + 1))
        out[name] = w
        out["m_" + name] = s * _jax.random.normal(km, w.shape, _jnp.float32)
        out["v_" + name] = (s * s) * _jax.random.uniform(kv, w.shape, _jnp.float32, 0.5, 1.5)
    if N_MICROBATCH > 1:
        for name, axis in PER_EXAMPLE_BATCH_AXIS.items():
            out[name] = _to_microbatches(out[name], axis)
    return {'x': out['x'], 'norm_ffn1': out['norm_ffn1'], 'ffn1_w_in': out['ffn1_w_in'], 'ffn1_w_out': out['ffn1_w_out'], 'norm_mix': out['norm_mix'], 'w_in': out['w_in'], 'gate_bias': out['gate_bias'], 'pool_w': out['pool_w'], 'pool_scale': out['pool_scale'], 'w_ret_up': out['w_ret_up'], 'w_pool_up': out['w_pool_up'], 'w_out': out['w_out'], 'norm_ffn2': out['norm_ffn2'], 'ffn2_w_in': out['ffn2_w_in'], 'ffn2_w_out': out['ffn2_w_out'], 'norm_final': out['norm_final'], 'loss_target': out['loss_target'], 'm_norm_ffn1': out['m_norm_ffn1'], 'm_ffn1_w_in': out['m_ffn1_w_in'], 'm_ffn1_w_out': out['m_ffn1_w_out'], 'm_norm_mix': out['m_norm_mix'], 'm_w_in': out['m_w_in'], 'm_gate_bias': out['m_gate_bias'], 'm_pool_w': out['m_pool_w'], 'm_pool_scale': out['m_pool_scale'], 'm_w_ret_up': out['m_w_ret_up'], 'm_w_pool_up': out['m_w_pool_up'], 'm_w_out': out['m_w_out'], 'm_norm_ffn2': out['m_norm_ffn2'], 'm_ffn2_w_in': out['m_ffn2_w_in'], 'm_ffn2_w_out': out['m_ffn2_w_out'], 'm_norm_final': out['m_norm_final'], 'v_norm_ffn1': out['v_norm_ffn1'], 'v_ffn1_w_in': out['v_ffn1_w_in'], 'v_ffn1_w_out': out['v_ffn1_w_out'], 'v_norm_mix': out['v_norm_mix'], 'v_w_in': out['v_w_in'], 'v_gate_bias': out['v_gate_bias'], 'v_pool_w': out['v_pool_w'], 'v_pool_scale': out['v_pool_scale'], 'v_w_ret_up': out['v_w_ret_up'], 'v_w_pool_up': out['v_w_pool_up'], 'v_w_out': out['v_w_out'], 'v_norm_ffn2': out['v_norm_ffn2'], 'v_ffn2_w_in': out['v_ffn2_w_in'], 'v_ffn2_w_out': out['v_ffn2_w_out'], 'v_norm_final': out['v_norm_final']}


def _loss(weights, diff, rest, loss_target):
    with _jax.named_scope("forward"):
        args = {**rest, TWIN_DIFF_INPUT: diff, **{k: w.astype(_WEIGHT_DTYPES[k]) for k, w in weights.items()}}
        y = _forward(args)
    with _jax.named_scope("loss_head"):
        err = _jnp.square(y.astype(_jnp.float32) - loss_target)
        return 0.5 * _jnp.sum(_jnp.mean(err, axis=-1)) if err.ndim else 0.5 * err


def _adamw(w, g, m, v):
    m = ADAM_B1 * m + (1.0 - ADAM_B1) * g
    v = ADAM_B2 * v + (1.0 - ADAM_B2) * _jnp.square(g)
    m_hat = m / (1.0 - ADAM_B1 ** ADAM_STEP)
    v_hat = v / (1.0 - ADAM_B2 ** ADAM_STEP)
    delta = -ADAM_LR * (m_hat / (_jnp.sqrt(v_hat) + ADAM_EPS) + ADAM_WD * w)
    return delta, m, v


def reference(x, norm_ffn1, ffn1_w_in, ffn1_w_out, norm_mix, w_in, gate_bias, pool_w, pool_scale, w_ret_up, w_pool_up, w_out, norm_ffn2, ffn2_w_in, ffn2_w_out, norm_final, loss_target, m_norm_ffn1, m_ffn1_w_in, m_ffn1_w_out, m_norm_mix, m_w_in, m_gate_bias, m_pool_w, m_pool_scale, m_w_ret_up, m_w_pool_up, m_w_out, m_norm_ffn2, m_ffn2_w_in, m_ffn2_w_out, m_norm_final, v_norm_ffn1, v_ffn1_w_in, v_ffn1_w_out, v_norm_mix, v_w_in, v_gate_bias, v_pool_w, v_pool_scale, v_w_ret_up, v_w_pool_up, v_w_out, v_norm_ffn2, v_ffn2_w_in, v_ffn2_w_out, v_norm_final):
    given = dict(x=x, norm_ffn1=norm_ffn1, ffn1_w_in=ffn1_w_in, ffn1_w_out=ffn1_w_out, norm_mix=norm_mix, w_in=w_in, gate_bias=gate_bias, pool_w=pool_w, pool_scale=pool_scale, w_ret_up=w_ret_up, w_pool_up=w_pool_up, w_out=w_out, norm_ffn2=norm_ffn2, ffn2_w_in=ffn2_w_in, ffn2_w_out=ffn2_w_out, norm_final=norm_final, loss_target=loss_target, m_norm_ffn1=m_norm_ffn1, m_ffn1_w_in=m_ffn1_w_in, m_ffn1_w_out=m_ffn1_w_out, m_norm_mix=m_norm_mix, m_w_in=m_w_in, m_gate_bias=m_gate_bias, m_pool_w=m_pool_w, m_pool_scale=m_pool_scale, m_w_ret_up=m_w_ret_up, m_w_pool_up=m_w_pool_up, m_w_out=m_w_out, m_norm_ffn2=m_norm_ffn2, m_ffn2_w_in=m_ffn2_w_in, m_ffn2_w_out=m_ffn2_w_out, m_norm_final=m_norm_final, v_norm_ffn1=v_norm_ffn1, v_ffn1_w_in=v_ffn1_w_in, v_ffn1_w_out=v_ffn1_w_out, v_norm_mix=v_norm_mix, v_w_in=v_w_in, v_gate_bias=v_gate_bias, v_pool_w=v_pool_w, v_pool_scale=v_pool_scale, v_w_ret_up=v_w_ret_up, v_w_pool_up=v_w_pool_up, v_w_out=v_w_out, v_norm_ffn2=v_norm_ffn2, v_ffn2_w_in=v_ffn2_w_in, v_ffn2_w_out=v_ffn2_w_out, v_norm_final=v_norm_final)
    weights = {n: given[n] for n in TWIN_WEIGHTS}
    shared = {n: given[n] for n in SHARED_INPUTS}
    per_example = {n: given[n] for n in ['x']}
    grad_fn = _jax.value_and_grad(_loss, argnums=(0, 1))

    def one_microbatch(ex, loss_target):
        ex = dict(ex)
        diff = ex.pop(TWIN_DIFF_INPUT)
        return grad_fn(weights, diff, {**shared, **ex}, loss_target)

    if N_MICROBATCH == 1:
        loss, (grad_w, grad_x) = one_microbatch(per_example, given["loss_target"])
    else:
        def body(carry, xs):
            loss_sum, grad_sum = carry
            l_k, (gw_k, gx_k) = one_microbatch(xs[0], xs[1])
            with _jax.named_scope("update"):
                return (loss_sum + l_k, _jax.tree.map(_jnp.add, grad_sum, gw_k)), gx_k

        init = (_jnp.zeros((), _jnp.float32), _jax.tree.map(_jnp.zeros_like, weights))
        (loss, grad_w), grad_x = _jax.lax.scan(body, init, (per_example, given["loss_target"]))
    with _jax.named_scope("update"):
        delta_w, new_m, new_v = {}, {}, {}
        for n in TWIN_WEIGHTS:
            delta_w[n], new_m[n], new_v[n] = _adamw(weights[n], grad_w[n], given["m_" + n], given["v_" + n])
    return (loss, grad_x, *[grad_w[n] for n in TWIN_WEIGHTS], *[delta_w[n] for n in TWIN_WEIGHTS],
            *[new_m[n] for n in TWIN_WEIGHTS], *[new_v[n] for n in TWIN_WEIGHTS])
```

```python
import functools

import numpy as np
import jax
import jax.numpy as jnp
from jax import lax
from jax.experimental import pallas as pl
from jax.experimental.pallas import tpu as pltpu

F32 = jnp.float32
BF16 = jnp.bfloat16

N_DEV = 8
D_MODEL = 1024
SEQ = 4096
D_FF = 2816
FF_SHARD = 2 * D_FF // N_DEV
N_FF_GROUPS = N_DEV // 2
HEADS = 4
HEAD_DIM = 256
ROT_HALF = HEAD_DIM // 2
CHUNK = 64
RET_BLOCK = 256
POOL_WINDOWS = (2, 4, 8, 16)
POOL_GROUP_DIM = 256
HALO = 16
MIX_SHARD = 7 * D_MODEL // N_DEV
N_SEG = 7
ROPE_BASE = 10000.0
NORM_EPS = 1e-6
FFN_RES_WEIGHT = 0.5
ADAM_LR, ADAM_B1, ADAM_B2, ADAM_EPS, ADAM_WD, ADAM_STEP = 0.001, 0.9, 0.999, 1e-08, 0.01, 10

TOKEN_TILE = 256
VMEM_CAP_V7X = 64 * 1024 * 1024
MESH = pl.DeviceIdType.MESH
ANY = pl.BlockSpec(memory_space=pl.ANY)


def _vmem_limit(estimate_bytes):
    return int(min(estimate_bytes * 5 // 4 + (6 << 20), VMEM_CAP_V7X - (4 << 20)))


def _params(estimate_bytes, n_grid):
    return pltpu.CompilerParams(dimension_semantics=("arbitrary",) * n_grid,
                                vmem_limit_bytes=_vmem_limit(estimate_bytes))


def _dot(a, b):
    return jnp.dot(a, b, preferred_element_type=F32)


def _dot_nt(a, b):
    return lax.dot_general(a, b, (((1,), (1,)), ((), ())), preferred_element_type=F32)


def _dot_tn(a, b):
    return lax.dot_general(a, b, (((0,), (0,)), ((), ())), preferred_element_type=F32)


def _sig(x):
    return 1.0 / (1.0 + jnp.exp(-x))


def _rms(x, g):
    r = lax.rsqrt(jnp.mean(x * x, axis=-1, keepdims=True) + NORM_EPS)
    xhat = x * r
    return xhat * g, xhat, r


def _rms_bwd(dyg, xhat, r):
    return r * (dyg - xhat * jnp.mean(dyg * xhat, axis=-1, keepdims=True))


def _row_spec(tile, width, col=0):
    return pl.BlockSpec((tile, width), lambda i, c=col: (i, c))


def _full_spec(shape):
    return pl.BlockSpec(shape, lambda *_: (0,) * len(shape))


def _rotary_tables():
    inv_freq = (np.float32(ROPE_BASE) ** (-np.arange(ROT_HALF, dtype=np.float32) / np.float32(ROT_HALF))).astype(np.float32)
    ang = (np.arange(SEQ, dtype=np.float32)[:, None] * inv_freq[None, :]).astype(np.float32)
    return jnp.asarray(np.cos(ang.astype(np.float64)), F32), jnp.asarray(np.sin(ang.astype(np.float64)), F32)


def _retention_tables():
    log_gamma = np.log(1.0 - 2.0 ** (-5.0 - np.arange(HEADS, dtype=np.float64)))
    n = np.arange(RET_BLOCK)
    diff = (n[:, None] - n[None, :]).astype(np.float64)
    same = (n[:, None] // CHUNK) == (n[None, :] // CHUNK)
    earlier = (n[None, :] // CHUNK) < (n[:, None] // CHUNK)
    expo = np.where(same, np.abs(diff), diff)
    mask = np.where(same | earlier, np.exp(log_gamma[:, None, None] * expo[None]), 0.0)
    qdec = np.exp(log_gamma[:, None] * (n[None, :] + 1.0))[:, :, None]
    kdec = np.exp(log_gamma[:, None] * (RET_BLOCK - 1.0 - n[None, :]))[:, :, None]
    cdec = np.exp(log_gamma * RET_BLOCK)[:, None, None]
    return (jnp.asarray(mask, F32), jnp.asarray(qdec, F32), jnp.asarray(kdec, F32), jnp.asarray(cdec, F32))


def _my_position():
    return lax.axis_index("x"), lax.axis_index("y"), lax.axis_index("c")


def _linear_id(px, py, pc):
    return 4 * px + 2 * py + pc


def _all_gather(shards):
    n = len(shards)

    def body(*refs):
        src, out = refs[:n], refs[n:2 * n]
        send_sems, recv_sems, local_sem = refs[2 * n:]
        x, y, c = _my_position()
        me, sibling = (x, y, c), (x, y, 1 - c)
        chips = [(1 - x, y), (x, 1 - y), (1 - x, 1 - y)]

        def copy(t, k, block, to, from_src=False):
            rows = out[t].at[_linear_id(*block)]
            return pltpu.make_async_remote_copy(
                src_ref=src[t] if from_src else rows, dst_ref=rows,
                send_sem=send_sems.at[t, k], recv_sem=recv_sems.at[t, k],
                device_id=to, device_id_type=MESH)

        local = [pltpu.make_async_copy(src[t], out[t].at[_linear_id(*me)], local_sem.at[t]) for t in range(n)]
        for cp in local:
            cp.start()
        first = []
        for t in range(n):
            first.append(copy(t, 0, me, sibling, from_src=True))
            first += [copy(t, 1 + j, me, (*chip, c), from_src=True) for j, chip in enumerate(chips)]
        for cp in first:
            cp.start()
        passed = []
        for j, chip in enumerate(chips):
            for t in range(n):
                copy(t, 1 + j, (*chip, c), me).wait_recv()
                fwd = copy(t, 4 + j, (*chip, c), sibling)
                fwd.start()
                passed.append(fwd)
        for t in range(n):
            copy(t, 0, sibling, me).wait_recv()
            for j, chip in enumerate(chips):
                copy(t, 4 + j, (*chip, 1 - c), me).wait_recv()
        for cp in first + passed:
            cp.wait_send()
        for cp in local:
            cp.wait()

    return pl.pallas_call(
        body, name="weights_all_gather",
        out_shape=[jax.ShapeDtypeStruct((N_DEV,) + s.shape, s.dtype) for s in shards],
        in_specs=[ANY] * n, out_specs=[ANY] * n,
        scratch_shapes=[pltpu.SemaphoreType.DMA((n, 7)), pltpu.SemaphoreType.DMA((n, 7)), pltpu.SemaphoreType.DMA((n,))],
    )(*shards)


def _reduce_scatter_exchange(partials):
    n = len(partials)

    def body(*refs):
        src, out = refs[:n], refs[n:2 * n]
        send_sems, recv_sems, local_sem = refs[2 * n:]
        x, y, c = _my_position()

        def peer(k):
            return (x ^ (k >> 2), y ^ ((k >> 1) & 1), c ^ (k & 1))

        def copy(t, k):
            return pltpu.make_async_remote_copy(
                src_ref=src[t].at[_linear_id(*peer(k))], dst_ref=out[t].at[k],
                send_sem=send_sems.at[t, k - 1], recv_sem=recv_sems.at[t, k - 1],
                device_id=peer(k), device_id_type=MESH)

        local = [pltpu.make_async_copy(src[t].at[_linear_id(x, y, c)], out[t].at[0], local_sem.at[t]) for t in range(n)]
        for cp in local:
            cp.start()
        copies = [copy(t, k) for t in range(n) for k in range(1, N_DEV)]
        for cp in copies:
            cp.start()
        for cp in copies:
            cp.wait_recv()
        for cp in copies:
            cp.wait_send()
        for cp in local:
            cp.wait()

    return pl.pallas_call(
        body, name="grads_reduce_scatter",
        out_shape=[jax.ShapeDtypeStruct(p.shape, p.dtype) for p in partials],
        in_specs=[ANY] * n, out_specs=[ANY] * n,
        scratch_shapes=[pltpu.SemaphoreType.DMA((n, 7)), pltpu.SemaphoreType.DMA((n, 7)), pltpu.SemaphoreType.DMA((n,))],
    )(*partials)


def _all_reduce_rows(block):
    rows, width = block.shape

    def body(x_ref, sum_ref, gathered, send_sems, recv_sems, local_sem):
        x, y, c = _my_position()
        me, sibling = (x, y, c), (x, y, 1 - c)
        chips = [(1 - x, y), (x, 1 - y), (1 - x, 1 - y)]

        def slot(px, py, pc):
            return gathered.at[_linear_id(px, py, pc)]

        def copy(k, block_of, to, from_src=False):
            return pltpu.make_async_remote_copy(
                src_ref=x_ref if from_src else slot(*block_of), dst_ref=slot(*block_of),
                send_sem=send_sems.at[k], recv_sem=recv_sems.at[k], device_id=to, device_id_type=MESH)

        mine = pltpu.make_async_copy(x_ref, slot(*me), local_sem)
        mine.start()
        first = [copy(0, me, sibling, from_src=True)]
        first += [copy(1 + j, me, (*chip, c), from_src=True) for j, chip in enumerate(chips)]
        for cp in first:
            cp.start()
        passed = [copy(4 + j, (*chip, c), sibling) for j, chip in enumerate(chips)]
        for j, chip in enumerate(chips):
            copy(1 + j, (*chip, c), me).wait_recv()
            passed[j].start()
        copy(0, sibling, me).wait_recv()
        for j, chip in enumerate(chips):
            copy(4 + j, (*chip, 1 - c), me).wait_recv()
        for cp in first + passed:
            cp.wait_send()
        mine.wait()
        total = gathered[0]
        for d in range(1, N_DEV):
            total = total + gathered[d]
        sum_ref[...] = total

    return pl.pallas_call(
        body, name="small_grads_all_reduce",
        out_shape=jax.ShapeDtypeStruct((rows, width), F32),
        in_specs=[pl.BlockSpec(memory_space=pltpu.VMEM)],
        out_specs=pl.BlockSpec(memory_space=pltpu.VMEM),
        scratch_shapes=[pltpu.VMEM((N_DEV, rows, width), F32),
                        pltpu.SemaphoreType.DMA((7,)), pltpu.SemaphoreType.DMA((7,)), pltpu.SemaphoreType.DMA],
    )(block)


def _load_ffn_weights(win_hbm, wout_hbm, win, wout, sem):
    a = pltpu.make_async_copy(win_hbm, win, sem.at[0])
    b = pltpu.make_async_copy(wout_hbm, wout, sem.at[1])
    a.start()
    b.start()
    a.wait()
    b.wait()


def _ffn_forward(h_in, gain, win8, wout, name, head=None):
    tm, nt = TOKEN_TILE, SEQ // TOKEN_TILE

    def body(*refs):
        if head is None:
            x_ref, g_ref, win_hbm, wout_hbm, out_ref, gu_ref, win, wout, sem = refs
        else:
            x_ref, g_ref, win_hbm, wout_hbm, tgt_ref, gf_ref, out_ref, gu_ref, loss_ref, dgf_ref, win, wout, sem = refs
        i = pl.program_id(0)

        @pl.when(i == 0)
        def _():
            _load_ffn_weights(win_hbm, wout_hbm, win, wout, sem)
            if head is not None:
                loss_ref[...] = jnp.zeros_like(loss_ref)
                dgf_ref[...] = jnp.zeros_like(dgf_ref)

        x = x_ref[...]
        xn, _, _ = _rms(x, g_ref[...])
        xb = xn.astype(BF16)
        acc = jnp.zeros((tm, D_MODEL), F32)
        for j in range(N_FF_GROUPS):
            gate = _dot(xb, win[j])
            up = _dot(xb, win[j + N_FF_GROUPS])
            gu_ref[j] = gate.astype(BF16)
            gu_ref[j + N_FF_GROUPS] = up.astype(BF16)
            act = gate * _sig(gate) * up
            acc = acc + _dot(act.astype(BF16), wout[pl.ds(j * FF_SHARD, FF_SHARD), :])
        h = x + FFN_RES_WEIGHT * acc
        if head is None:
            out_ref[...] = h
        else:
            gf = gf_ref[...]
            y, hhat, r = _rms(h, gf)
            err = y - tgt_ref[...]
            loss_ref[...] += jnp.full(loss_ref.shape, 0.5 / D_MODEL * jnp.sum(err * err), F32)
            dy = err * (1.0 / D_MODEL)
            dgf_ref[...] += jnp.sum(dy * hhat, axis=0, keepdims=True)
            out_ref[...] = _rms_bwd(dy * gf, hhat, r)

    weights = 2 * D_MODEL * 2 * D_FF + 2 * D_FF * D_MODEL
    tiles = 2 * (2 * 4 * tm * D_MODEL + 2 * tm * 2 * D_FF) + (2 * 4 * tm * D_MODEL if head else 0)
    in_specs = [_row_spec(tm, D_MODEL), _full_spec((1, D_MODEL)), ANY, ANY]
    out_shape = [jax.ShapeDtypeStruct((SEQ, D_MODEL), F32), jax.ShapeDtypeStruct((N_DEV, SEQ, FF_SHARD), BF16)]
    out_specs = [_row_spec(tm, D_MODEL), pl.BlockSpec((N_DEV, tm, FF_SHARD), lambda i: (0, i, 0))]
    args = [h_in, gain, win8, wout]
    if head is not None:
        in_specs += [_row_spec(tm, D_MODEL), _full_spec((1, D_MODEL))]
        out_shape += [jax.ShapeDtypeStruct((1, 128), F32), jax.ShapeDtypeStruct((1, D_MODEL), F32)]
        out_specs += [_full_spec((1, 128)), _full_spec((1, D_MODEL))]
        args += list(head)
    return pl.pallas_call(
        body, name=name, grid=(nt,), in_specs=in_specs, out_specs=out_specs, out_shape=out_shape,
        scratch_shapes=[pltpu.VMEM((N_DEV, D_MODEL, FF_SHARD), BF16), pltpu.VMEM((D_FF, D_MODEL), BF16),
                        pltpu.SemaphoreType.DMA((2,))],
        compiler_params=_params(weights + tiles + 16 * tm * FF_SHARD * 4, 1),
    )(*args)


def _ffn_backward(dh_out, h_in, gain, gu, win8, wout, name):
    tm, nt = TOKEN_TILE, SEQ // TOKEN_TILE

    def body(dh_ref, x_ref, g_ref, gu_ref, win_hbm, wout_hbm,
             dhin_ref, dgu_ref, act_ref, xn_ref, df_ref, dg_ref, win, wout, sem):
        i = pl.program_id(0)

        @pl.when(i == 0)
        def _():
            _load_ffn_weights(win_hbm, wout_hbm, win, wout, sem)
            dg_ref[...] = jnp.zeros_like(dg_ref)

        dh = dh_ref[...]
        g = g_ref[...]
        xn, xhat, r = _rms(x_ref[...], g)
        df = (FFN_RES_WEIGHT * dh).astype(BF16)
        dxn = jnp.zeros((tm, D_MODEL), F32)
        for j in range(N_FF_GROUPS):
            gate = gu_ref[j].astype(F32)
            up = gu_ref[j + N_FF_GROUPS].astype(F32)
            dact = _dot_nt(df, wout[pl.ds(j * FF_SHARD, FF_SHARD), :])
            s = _sig(gate)
            silu = gate * s
            dgate = (dact * up * (s * (1.0 + gate * (1.0 - s)))).astype(BF16)
            dup = (dact * silu).astype(BF16)
            act_ref[j] = (silu * up).astype(BF16)
            dgu_ref[j] = dgate
            dgu_ref[j + N_FF_GROUPS] = dup
            dxn = dxn + _dot_nt(dgate, win[j]) + _dot_nt(dup, win[j + N_FF_GROUPS])
        dg_ref[...] += jnp.sum(dxn * xhat, axis=0, keepdims=True)
        dhin_ref[...] = dh + _rms_bwd(dxn * g, xhat, r)
        xn_ref[...] = xn.astype(BF16)
        df_ref[...] = df

    weights = 2 * D_MODEL * 2 * D_FF + 2 * D_FF * D_MODEL
    tiles = 2 * (3 * 4 * tm * D_MODEL + 2 * tm * (2 * 2 * D_FF + D_FF) + 2 * 2 * tm * D_MODEL)
    gu_spec = pl.BlockSpec((N_DEV, tm, FF_SHARD), lambda i: (0, i, 0))
    return pl.pallas_call(
        body, name=name, grid=(nt,),
        in_specs=[_row_spec(tm, D_MODEL), _row_spec(tm, D_MODEL), _full_spec((1, D_MODEL)), gu_spec, ANY, ANY],
        out_specs=[_row_spec(tm, D_MODEL), gu_spec, pl.BlockSpec((N_FF_GROUPS, tm, FF_SHARD), lambda i: (0, i, 0)),
                   _row_spec(tm, D_MODEL), _row_spec(tm, D_MODEL), _full_spec((1, D_MODEL))],
        out_shape=[jax.ShapeDtypeStruct((SEQ, D_MODEL), F32), jax.ShapeDtypeStruct((N_DEV, SEQ, FF_SHARD), BF16),
                   jax.ShapeDtypeStruct((N_FF_GROUPS, SEQ, FF_SHARD), BF16), jax.ShapeDtypeStruct((SEQ, D_MODEL), BF16),
                   jax.ShapeDtypeStruct((SEQ, D_MODEL), BF16), jax.ShapeDtypeStruct((1, D_MODEL), F32)],
        scratch_shapes=[pltpu.VMEM((N_DEV, D_MODEL, FF_SHARD), BF16), pltpu.VMEM((D_FF, D_MODEL), BF16),
                        pltpu.SemaphoreType.DMA((2,))],
        compiler_params=_params(weights + tiles + 20 * tm * FF_SHARD * 4, 1),
    )(dh_out, h_in, gain, gu, win8, wout)


def _weight_grad(x, g, n_out, x_spec, g_spec, k_dim, n_dim, name, tt=512):
    nt = SEQ // tt

    def body(x_ref, g_ref, out_ref, acc):
        t = pl.program_id(1)

        @pl.when(t == 0)
        def _():
            acc[...] = jnp.zeros_like(acc)

        acc[...] += _dot_tn(x_ref[...], g_ref[...])

        @pl.when(t == nt - 1)
        def _():
            out_ref[...] = acc[...].astype(BF16)

    return pl.pallas_call(
        body, name=name, grid=(n_out, nt), in_specs=[x_spec(tt), g_spec(tt)],
        out_specs=pl.BlockSpec((None, k_dim, n_dim), lambda b, t: (b, 0, 0)),
        out_shape=jax.ShapeDtypeStruct((n_out, k_dim, n_dim), BF16),
        scratch_shapes=[pltpu.VMEM((k_dim, n_dim), F32)],
        compiler_params=_params(2 * 2 * tt * (k_dim + n_dim) + 8 * k_dim * n_dim + 4 * k_dim * n_dim, 2),
    )(x, g)


def _ffn_weight_grads(act, df, xn, dgu, tag):
    d_wout = _weight_grad(
        act, df, N_FF_GROUPS,
        lambda tt: pl.BlockSpec((None, tt, FF_SHARD), lambda b, t: (b, t, 0)),
        lambda tt: pl.BlockSpec((tt, D_MODEL), lambda b, t: (t, 0)),
        FF_SHARD, D_MODEL, name=f"ffn{tag}_w_out_grad")
    d_win = _weight_grad(
        xn, dgu, N_DEV,
        lambda tt: pl.BlockSpec((tt, D_MODEL), lambda b, t: (t, 0)),
        lambda tt: pl.BlockSpec((None, tt, FF_SHARD), lambda b, t: (b, t, 0)),
        D_MODEL, FF_SHARD, name=f"ffn{tag}_w_in_grad")
    return d_win, d_wout.reshape(N_DEV, D_FF // N_DEV, D_MODEL)


def _load_mix_weight(wmix_hbm, wmix, sem):
    copies = [pltpu.make_async_copy(wmix_hbm.at[d], wmix.at[:, pl.ds(d * MIX_SHARD, MIX_SHARD)], sem.at[d])
              for d in range(N_DEV)]
    for cp in copies:
        cp.start()
    for cp in copies:
        cp.wait()


def _load_pool_weight(pw_hbm, pw, sem):
    rows = POOL_GROUP_DIM // N_DEV
    copies = [pltpu.make_async_copy(pw_hbm.at[d], pw.at[:, pl.ds(d * rows, rows), :], sem.at[d]) for d in range(N_DEV)]
    for cp in copies:
        cp.start()
    for cp in copies:
        cp.wait()


def _rotate(x1, x2, cos, sin):
    return x1 * cos - x2 * sin, x1 * sin + x2 * cos


def _mix_proj_forward(h1, gain, wmix8, cos, sin):
    tm, nt = TOKEN_TILE, SEQ // TOKEN_TILE
    k_scale = HEAD_DIM ** -0.5

    def body(h_ref, g_ref, wmix_hbm, cos_ref, sin_ref, u_ref, qkvg_ref, p_ref, gates_ref, wmix, sem):
        @pl.when(pl.program_id(0) == 0)
        def _():
            _load_mix_weight(wmix_hbm, wmix, sem)

        u = _rms(h_ref[...], g_ref[...])[0].astype(BF16)
        u_ref[...] = u
        cos_t, sin_t = cos_ref[...], sin_ref[...]
        for seg in range(N_SEG):
            pr = _dot(u, wmix[:, pl.ds(seg * D_MODEL, D_MODEL)])
            if seg < 2:
                scale = 1.0 if seg == 0 else k_scale
                for hd in range(HEADS):
                    lo = hd * HEAD_DIM
                    o1, o2 = _rotate(pr[:, lo:lo + ROT_HALF], pr[:, lo + ROT_HALF:lo + HEAD_DIM], cos_t, sin_t)
                    qkvg_ref[:, pl.ds(seg * D_MODEL + lo, ROT_HALF)] = (o1 * scale).astype(BF16)
                    qkvg_ref[:, pl.ds(seg * D_MODEL + lo + ROT_HALF, ROT_HALF)] = (o2 * scale).astype(BF16)
            elif seg < 4:
                qkvg_ref[:, pl.ds(seg * D_MODEL, D_MODEL)] = pr.astype(BF16)
            elif seg == 4:
                p_ref[...] = pr
            else:
                gates_ref[:, pl.ds((seg - 5) * D_MODEL, D_MODEL)] = pr.astype(BF16)

    est = 2 * D_MODEL * N_SEG * D_MODEL + 2 * tm * (4 * D_MODEL + 2 * D_MODEL + 2 * 4 * D_MODEL + 4 * D_MODEL + 2 * 2 * D_MODEL)
    return pl.pallas_call(
        body, name="mix_proj_fwd", grid=(nt,),
        in_specs=[_row_spec(tm, D_MODEL), _full_spec((1, D_MODEL)), ANY, _row_spec(tm, ROT_HALF), _row_spec(tm, ROT_HALF)],
        out_specs=[_row_spec(tm, D_MODEL), _row_spec(tm, 4 * D_MODEL), _row_spec(tm, D_MODEL), _row_spec(tm, 2 * D_MODEL)],
        out_shape=[jax.ShapeDtypeStruct((SEQ, D_MODEL), BF16), jax.ShapeDtypeStruct((SEQ, 4 * D_MODEL), BF16),
                   jax.ShapeDtypeStruct((SEQ, D_MODEL), F32), jax.ShapeDtypeStruct((SEQ, 2 * D_MODEL), BF16)],
        scratch_shapes=[pltpu.VMEM((D_MODEL, N_SEG * D_MODEL), BF16), pltpu.SemaphoreType.DMA((N_DEV,))],
        compiler_params=_params(est + 8 * tm * D_MODEL * 4, 1),
    )(h1, gain, wmix8, cos, sin)


def _head_block_spec(seg, reverse=False):
    nb = SEQ // RET_BLOCK
    if reverse:
        return pl.BlockSpec((RET_BLOCK, HEAD_DIM), lambda h, i, s=seg: (nb - 1 - i, s * HEADS + h))
    return pl.BlockSpec((RET_BLOCK, HEAD_DIM), lambda h, i, s=seg: (i, s * HEADS + h))


def _table_specs():
    return [pl.BlockSpec((None, RET_BLOCK, RET_BLOCK), lambda h, i: (h, 0, 0)),
            pl.BlockSpec((None, RET_BLOCK, 1), lambda h, i: (h, 0, 0)),
            pl.BlockSpec((None, RET_BLOCK, 1), lambda h, i: (h, 0, 0)),
            pl.BlockSpec((None, 1, 1), lambda h, i: (h, 0, 0))]


def _retention_forward(qkvg, tables):
    nb = SEQ // RET_BLOCK

    def body(q_ref, k_ref, v_ref, gr_ref, mask_ref, qdec_ref, kdec_ref, cdec_ref, ret_ref, o_ref, state):
        @pl.when(pl.program_id(1) == 0)
        def _():
            state[...] = jnp.zeros_like(state)

        q, k, v = q_ref[...], k_ref[...], v_ref[...]
        scores = _dot_nt(q, k) * mask_ref[...]
        inner = _dot(scores.astype(BF16), v)
        cross = _dot((q.astype(F32) * qdec_ref[...]).astype(BF16), state[...].astype(BF16))
        ret = inner + cross
        state[...] = state[...] * cdec_ref[...] + _dot_tn((k.astype(F32) * kdec_ref[...]).astype(BF16), v)
        ret_ref[...] = ret
        retn = ret * lax.rsqrt(jnp.mean(ret * ret, axis=-1, keepdims=True) + NORM_EPS)
        gr = gr_ref[...].astype(F32)
        o_ref[...] = (retn * (gr * _sig(gr))).astype(BF16)

    return pl.pallas_call(
        body, name="retention_fwd", grid=(HEADS, nb),
        in_specs=[_head_block_spec(0), _head_block_spec(1), _head_block_spec(2), _head_block_spec(3)] + _table_specs(),
        out_specs=[pl.BlockSpec((RET_BLOCK, HEAD_DIM), lambda h, i: (i, h))] * 2,
        out_shape=[jax.ShapeDtypeStruct((SEQ, D_MODEL), F32), jax.ShapeDtypeStruct((SEQ, D_MODEL), BF16)],
        scratch_shapes=[pltpu.VMEM((HEAD_DIM, HEAD_DIM), F32)],
        compiler_params=_params(16 * RET_BLOCK * HEAD_DIM * 4, 2),
    )(qkvg, qkvg, qkvg, qkvg, *tables)


def _retention_backward_q(qkvg, dret, tables):
    nb = SEQ // RET_BLOCK

    def body(k_ref, v_ref, do_ref, mask_ref, qdec_ref, kdec_ref, cdec_ref, dq_ref, state):
        @pl.when(pl.program_id(1) == 0)
        def _():
            state[...] = jnp.zeros_like(state)

        k, v, do = k_ref[...], v_ref[...], do_ref[...]
        dscores = _dot_nt(do, v) * mask_ref[...]
        dq_ref[...] = _dot(dscores.astype(BF16), k) + _dot_nt(do, state[...].astype(BF16)) * qdec_ref[...]
        state[...] = state[...] * cdec_ref[...] + _dot_tn((k.astype(F32) * kdec_ref[...]).astype(BF16), v)

    return pl.pallas_call(
        body, name="retention_bwd_q", grid=(HEADS, nb),
        in_specs=[_head_block_spec(1), _head_block_spec(2), pl.BlockSpec((RET_BLOCK, HEAD_DIM), lambda h, i: (i, h))] + _table_specs(),
        out_specs=pl.BlockSpec((RET_BLOCK, HEAD_DIM), lambda h, i: (i, h)),
        out_shape=jax.ShapeDtypeStruct((SEQ, D_MODEL), F32),
        scratch_shapes=[pltpu.VMEM((HEAD_DIM, HEAD_DIM), F32)],
        compiler_params=_params(16 * RET_BLOCK * HEAD_DIM * 4, 2),
    )(qkvg, qkvg, dret, *tables)


def _retention_backward_kv(qkvg, dret, tables):
    nb = SEQ // RET_BLOCK

    def body(q_ref, k_ref, v_ref, do_ref, mask_ref, qdec_ref, kdec_ref, cdec_ref, dk_ref, dv_ref, gstate):
        @pl.when(pl.program_id(1) == 0)
        def _():
            gstate[...] = jnp.zeros_like(gstate)

        q, k, v, do = q_ref[...], k_ref[...], v_ref[...], do_ref[...]
        mask = mask_ref[...]
        scores = (_dot_nt(q, k) * mask).astype(BF16)
        dscores = (_dot_nt(do, v) * mask).astype(BF16)
        gs = gstate[...].astype(BF16)
        dk_ref[...] = _dot_tn(dscores, q) + _dot_nt(v, gs) * kdec_ref[...]
        dv_ref[...] = _dot_tn(scores, do) + _dot((k.astype(F32) * kdec_ref[...]).astype(BF16), gs)
        gstate[...] = gstate[...] * cdec_ref[...] + _dot_tn((q.astype(F32) * qdec_ref[...]).astype(BF16), do)

    rev = lambda h, i: (nb - 1 - i, h)
    return pl.pallas_call(
        body, name="retention_bwd_kv", grid=(HEADS, nb),
        in_specs=[_head_block_spec(0, True), _head_block_spec(1, True), _head_block_spec(2, True),
                  pl.BlockSpec((RET_BLOCK, HEAD_DIM), rev)] + _table_specs(),
        out_specs=[pl.BlockSpec((RET_BLOCK, HEAD_DIM), rev)] * 2,
        out_shape=[jax.ShapeDtypeStruct((SEQ, D_MODEL), F32)] * 2,
        scratch_shapes=[pltpu.VMEM((HEAD_DIM, HEAD_DIM), F32)],
        compiler_params=_params(20 * RET_BLOCK * HEAD_DIM * 4, 2),
    )(qkvg, qkvg, qkvg, dret, *tables)


def _pooled(p_ext, first_row):
    rows = p_ext.shape[0]
    t = first_row + lax.broadcasted_iota(jnp.int32, (rows - HALO, 1), 0)
    outs = []
    for g, w in enumerate(POOL_WINDOWS):
        e = p_ext[:, g * POOL_GROUP_DIM:(g + 1) * POOL_GROUP_DIM]
        s, span = e, 1
        while span < w:
            s = s + pltpu.roll(s, span, 0)
            span *= 2
        count = jnp.minimum(t + 1, w).astype(F32)
        outs.append(s[HALO:] / count - e[HALO:])
    return outs


def _pooled_transpose(d_ext, first_row):
    rows = d_ext.shape[0]
    t = first_row + lax.broadcasted_iota(jnp.int32, (rows, 1), 0)
    outs = []
    for g, w in enumerate(POOL_WINDOWS):
        d = d_ext[:, g * POOL_GROUP_DIM:(g + 1) * POOL_GROUP_DIM]
        e = jnp.where(t < SEQ, d / jnp.minimum(t + 1, w).astype(F32), 0.0)
        s, span = e, 1
        while span < w:
            s = s + pltpu.roll(s, rows - span, 0)
            span *= 2
        outs.append(s[:rows - HALO] - d[:rows - HALO])
    return outs


def _mix_tail_specs(tm):
    halo_blocks = tm // HALO
    return [
        _row_spec(tm, D_MODEL),
        pl.BlockSpec((HALO, D_MODEL), lambda i: (jnp.maximum(i * halo_blocks - 1, 0), 0)),
        _row_spec(tm, 2 * D_MODEL),
        _row_spec(tm, D_MODEL),
        _full_spec((2, D_MODEL)), _full_spec((1, D_MODEL)), ANY,
        _full_spec((D_MODEL, D_MODEL)), _full_spec((D_MODEL, D_MODEL)), _full_spec((D_MODEL, D_MODEL)),
    ]


def _mix_tail_compute(i, tm, p_ref, halo_ref, gates_ref, oret_ref, bias_ref, scale_ref, pw, wru_ref, wpu_ref):
    halo = jnp.where(i > 0, halo_ref[...], 0.0)
    pooled = _pooled(jnp.concatenate([halo, p_ref[...]], axis=0), i * tm)
    pooled = [x.astype(BF16) for x in pooled]
    mixed = jnp.concatenate([_dot(pooled[g], pw[g]) for g in range(len(POOL_WINDOWS))], axis=-1)
    pool_out = (mixed * scale_ref[...]).astype(BF16)
    o_ret = oret_ref[...]
    a = _dot(o_ret, wru_ref[...])
    b = _dot(pool_out, wpu_ref[...])
    z = gates_ref[...].astype(F32)
    g0 = _sig(z[:, :D_MODEL] + bias_ref[0:1, :])
    g1 = _sig(z[:, D_MODEL:] + bias_ref[1:2, :])
    merged = (g0 * a + g1 * b).astype(BF16)
    return pooled, mixed, pool_out, o_ret, a, b, g0, g1, merged


def _mix_tail_forward(p, gates, o_ret, h1, bias, scale, pw8, wru, wpu, wo):
    tm, nt = TOKEN_TILE, SEQ // TOKEN_TILE

    def body(p_ref, halo_ref, gates_ref, oret_ref, bias_ref, scale_ref, pw_hbm, wru_ref, wpu_ref, wo_ref, h1_ref,
             h2_ref, pw, sem):
        i = pl.program_id(0)

        @pl.when(i == 0)
        def _():
            _load_pool_weight(pw_hbm, pw, sem)

        merged = _mix_tail_compute(i, tm, p_ref, halo_ref, gates_ref, oret_ref, bias_ref, scale_ref, pw, wru_ref, wpu_ref)[-1]
        h2_ref[...] = h1_ref[...] + _dot(merged, wo_ref[...])

    est = 3 * 2 * 2 * D_MODEL * D_MODEL + 2 * tm * D_MODEL * (4 + 4 + 2 + 4 + 4) + 16 * tm * D_MODEL * 4
    return pl.pallas_call(
        body, name="mix_tail_fwd", grid=(nt,),
        in_specs=_mix_tail_specs(tm) + [_row_spec(tm, D_MODEL)],
        out_specs=_row_spec(tm, D_MODEL), out_shape=jax.ShapeDtypeStruct((SEQ, D_MODEL), F32),
        scratch_shapes=[pltpu.VMEM((len(POOL_WINDOWS), POOL_GROUP_DIM, POOL_GROUP_DIM), BF16), pltpu.SemaphoreType.DMA((N_DEV,))],
        compiler_params=_params(est, 1),
    )(p, p, gates, o_ret, bias, scale, pw8, wru, wpu, wo, h1)


def _mix_tail_backward(dh2, p, gates, o_ret, ret, qkvg, bias, scale, pw8, wru, wpu, wo):
    tm, nt = TOKEN_TILE, SEQ // TOKEN_TILE
    n_groups = len(POOL_WINDOWS)
    rows_per_dev = POOL_GROUP_DIM // N_DEV

    def body(p_ref, halo_ref, gates_ref, oret_ref, bias_ref, scale_ref, pw_hbm, wru_ref, wpu_ref, wo_ref,
             dh2_ref, ret_ref, gr_ref,
             dret_ref, dgr_ref, dgates_ref, dpooled_ref, dwo_ref, dwru_ref, dwpu_ref, dpw_ref, dbias_ref, dscale_ref,
             pw, sem, acc_wo, acc_wru, acc_wpu, acc_pw):
        i = pl.program_id(0)

        @pl.when(i == 0)
        def _():
            _load_pool_weight(pw_hbm, pw, sem)
            for ref in (acc_wo, acc_wru, acc_wpu, acc_pw, dbias_ref, dscale_ref):
                ref[...] = jnp.zeros_like(ref)

        pooled, mixed, pool_out, o_ret, a, b, g0, g1, merged = _mix_tail_compute(
            i, tm, p_ref, halo_ref, gates_ref, oret_ref, bias_ref, scale_ref, pw, wru_ref, wpu_ref)
        dh2 = dh2_ref[...].astype(BF16)
        dm = _dot_nt(dh2, wo_ref[...])
        acc_wo[...] += _dot_tn(merged, dh2)
        da = (dm * g0).astype(BF16)
        db = (dm * g1).astype(BF16)
        dz0 = dm * a * g0 * (1.0 - g0)
        dz1 = dm * b * g1 * (1.0 - g1)
        dbias_ref[0:1, :] += jnp.sum(dz0, axis=0, keepdims=True)
        dbias_ref[1:2, :] += jnp.sum(dz1, axis=0, keepdims=True)
        dgates_ref[:, pl.ds(0, D_MODEL)] = dz0.astype(BF16)
        dgates_ref[:, pl.ds(D_MODEL, D_MODEL)] = dz1.astype(BF16)
        acc_wru[...] += _dot_tn(o_ret, da)
        acc_wpu[...] += _dot_tn(pool_out, db)
        d_oret = _dot_nt(da, wru_ref[...])
        d_pool_out = _dot_nt(db, wpu_ref[...])
        dscale_ref[...] += jnp.sum(d_pool_out * mixed, axis=0, keepdims=True)
        dmixed = (d_pool_out * scale_ref[...]).astype(BF16)
        for g in range(n_groups):
            dmg = dmixed[:, g * POOL_GROUP_DIM:(g + 1) * POOL_GROUP_DIM]
            acc_pw[g] += _dot_tn(pooled[g], dmg)
            dpooled_ref[:, pl.ds(g * POOL_GROUP_DIM, POOL_GROUP_DIM)] = _dot_nt(dmg, pw[g])
        gr = gr_ref[...].astype(F32)
        s = _sig(gr)
        silu = gr * s
        for hd in range(HEADS):
            cols = slice(hd * HEAD_DIM, (hd + 1) * HEAD_DIM)
            r_h = ret_ref[:, cols]
            rr = lax.rsqrt(jnp.mean(r_h * r_h, axis=-1, keepdims=True) + NORM_EPS)
            rhat = r_h * rr
            do_h = d_oret[:, cols]
            dgr_ref[:, cols] = (do_h * rhat * (s[:, cols] * (1.0 + gr[:, cols] * (1.0 - s[:, cols])))).astype(BF16)
            dret_ref[:, cols] = _rms_bwd(do_h * silu[:, cols], rhat, rr).astype(BF16)

        @pl.when(i == nt - 1)
        def _():
            rows = D_MODEL // N_DEV
            for d in range(N_DEV):
                dwo_ref[d] = acc_wo[pl.ds(d * rows, rows), :].astype(BF16)
                dwru_ref[d] = acc_wru[pl.ds(d * rows, rows), :].astype(BF16)
                dwpu_ref[d] = acc_wpu[pl.ds(d * rows, rows), :].astype(BF16)
                dpw_ref[d] = acc_pw[:, pl.ds(d * rows_per_dev, rows_per_dev), :].astype(BF16)

    sq = (N_DEV, D_MODEL // N_DEV, D_MODEL)
    pw_shape = (N_DEV, n_groups, rows_per_dev, POOL_GROUP_DIM)
    est = (3 * 2 * 2 * D_MODEL * D_MODEL + 3 * 4 * D_MODEL * D_MODEL + 3 * 2 * 2 * D_MODEL * D_MODEL
           + 2 * tm * D_MODEL * (4 + 4 + 2 + 4 + 4 + 2 + 2 + 2 + 4 + 4) + 24 * tm * D_MODEL * 4)
    return pl.pallas_call(
        body, name="mix_tail_bwd", grid=(nt,),
        in_specs=_mix_tail_specs(tm) + [_row_spec(tm, D_MODEL), _row_spec(tm, D_MODEL), _row_spec(tm, D_MODEL, 3)],
        out_specs=[_row_spec(tm, D_MODEL), _row_spec(tm, D_MODEL), _row_spec(tm, 2 * D_MODEL), _row_spec(tm, D_MODEL),
                   _full_spec(sq), _full_spec(sq), _full_spec(sq), _full_spec(pw_shape),
                   _full_spec((2, D_MODEL)), _full_spec((1, D_MODEL))],
        out_shape=[jax.ShapeDtypeStruct((SEQ, D_MODEL), BF16), jax.ShapeDtypeStruct((SEQ, D_MODEL), BF16),
                   jax.ShapeDtypeStruct((SEQ, 2 * D_MODEL), BF16), jax.ShapeDtypeStruct((SEQ, D_MODEL), F32),
                   jax.ShapeDtypeStruct(sq, BF16), jax.ShapeDtypeStruct(sq, BF16), jax.ShapeDtypeStruct(sq, BF16),
                   jax.ShapeDtypeStruct(pw_shape, BF16),
                   jax.ShapeDtypeStruct((2, D_MODEL), F32), jax.ShapeDtypeStruct((1, D_MODEL), F32)],
        scratch_shapes=[pltpu.VMEM((n_groups, POOL_GROUP_DIM, POOL_GROUP_DIM), BF16), pltpu.SemaphoreType.DMA((N_DEV,)),
                        pltpu.VMEM((D_MODEL, D_MODEL), F32), pltpu.VMEM((D_MODEL, D_MODEL), F32),
                        pltpu.VMEM((D_MODEL, D_MODEL), F32), pltpu.VMEM((n_groups, POOL_GROUP_DIM, POOL_GROUP_DIM), F32)],
        compiler_params=_params(est, 1),
    )(p, p, gates, o_ret, bias, scale, pw8, wru, wpu, wo, dh2, ret, qkvg)


def _mix_proj_backward(dq, dk, dv, dgr, dpooled, dgates, cos, sin, h1, gain, dh2, wmix8):
    tm, nt = TOKEN_TILE, SEQ // TOKEN_TILE
    halo_blocks = tm // HALO
    last_halo = SEQ // HALO - 1
    k_scale = HEAD_DIM ** -0.5

    def body(dq_ref, dk_ref, dv_ref, dgr_ref, dpool_ref, dhalo_ref, dgates_ref, cos_ref, sin_ref, h1_ref, g_ref,
             dh2_ref, wmix_hbm, dh1_ref, dproj_ref, dg_ref, wmix, sem):
        i = pl.program_id(0)

        @pl.when(i == 0)
        def _():
            _load_mix_weight(wmix_hbm, wmix, sem)
            dg_ref[...] = jnp.zeros_like(dg_ref)

        cos_t, sin_t = cos_ref[...], sin_ref[...]
        for seg, ref, scale in ((0, dq_ref, 1.0), (1, dk_ref, k_scale)):
            for hd in range(HEADS):
                lo = hd * HEAD_DIM
                d1, d2 = ref[:, lo:lo + ROT_HALF], ref[:, lo + ROT_HALF:lo + HEAD_DIM]
                dproj_ref[:, pl.ds(seg * D_MODEL + lo, ROT_HALF)] = ((d1 * cos_t + d2 * sin_t) * scale).astype(BF16)
                dproj_ref[:, pl.ds(seg * D_MODEL + lo + ROT_HALF, ROT_HALF)] = ((d2 * cos_t - d1 * sin_t) * scale).astype(BF16)
        dproj_ref[:, pl.ds(2 * D_MODEL, D_MODEL)] = dv_ref[...].astype(BF16)
        dproj_ref[:, pl.ds(3 * D_MODEL, D_MODEL)] = dgr_ref[...]
        dp = _pooled_transpose(jnp.concatenate([dpool_ref[...], dhalo_ref[...]], axis=0), i * tm)
        for g in range(len(POOL_WINDOWS)):
            dproj_ref[:, pl.ds(4 * D_MODEL + g * POOL_GROUP_DIM, POOL_GROUP_DIM)] = dp[g].astype(BF16)
        dproj_ref[:, pl.ds(5 * D_MODEL, 2 * D_MODEL)] = dgates_ref[...]
        du = jnp.zeros((tm, D_MODEL), F32)
        for seg in range(N_SEG):
            cols = pl.ds(seg * D_MODEL, D_MODEL)
            du = du + _dot_nt(dproj_ref[:, cols], wmix[:, cols])
        g = g_ref[...]
        _, xhat, r = _rms(h1_ref[...], g)
        dg_ref[...] += jnp.sum(du * xhat, axis=0, keepdims=True)
        dh1_ref[...] = dh2_ref[...] + _rms_bwd(du * g, xhat, r)

    est = 2 * D_MODEL * N_SEG * D_MODEL + 2 * tm * D_MODEL * (3 * 4 + 2 + 4 + 4 + 4 + 4 + 4 + 14) + 12 * tm * D_MODEL * 4
    return pl.pallas_call(
        body, name="mix_proj_bwd", grid=(nt,),
        in_specs=[_row_spec(tm, D_MODEL), _row_spec(tm, D_MODEL), _row_spec(tm, D_MODEL), _row_spec(tm, D_MODEL),
                  _row_spec(tm, D_MODEL),
                  pl.BlockSpec((HALO, D_MODEL), lambda i: (jnp.minimum((i + 1) * halo_blocks, last_halo), 0)),
                  _row_spec(tm, 2 * D_MODEL), _row_spec(tm, ROT_HALF), _row_spec(tm, ROT_HALF),
                  _row_spec(tm, D_MODEL), _full_spec((1, D_MODEL)), _row_spec(tm, D_MODEL), ANY],
        out_specs=[_row_spec(tm, D_MODEL), _row_spec(tm, N_SEG * D_MODEL), _full_spec((1, D_MODEL))],
        out_shape=[jax.ShapeDtypeStruct((SEQ, D_MODEL), F32), jax.ShapeDtypeStruct((SEQ, N_SEG * D_MODEL), BF16),
                   jax.ShapeDtypeStruct((1, D_MODEL), F32)],
        scratch_shapes=[pltpu.VMEM((D_MODEL, N_SEG * D_MODEL), BF16), pltpu.SemaphoreType.DMA((N_DEV,))],
        compiler_params=_params(est, 1),
    )(dq, dk, dv, dgr, dpooled, dpooled, dgates, cos, sin, h1, gain, dh2, wmix8)


def _adamw(w, parts, m, v, name):
    rows, cols = w.shape
    n_parts = parts.shape[0]
    tr = rows
    for cand in (256, 128, 64, 32, 16):
        if rows > cand and rows % cand == 0:
            tr = cand
            break
    c1 = 1.0 - ADAM_B1 ** ADAM_STEP
    c2 = 1.0 - ADAM_B2 ** ADAM_STEP

    def body(w_ref, p_ref, m_ref, v_ref, g_out, d_out, m_out, v_out):
        g = p_ref[0].astype(F32)
        for k in range(1, n_parts):
            g = g + p_ref[k].astype(F32)
        m_new = ADAM_B1 * m_ref[...] + (1.0 - ADAM_B1) * g
        v_new = ADAM_B2 * v_ref[...] + (1.0 - ADAM_B2) * (g * g)
        g_out[...] = g
        m_out[...] = m_new
        v_out[...] = v_new
        d_out[...] = -ADAM_LR * ((m_new / c1) / (jnp.sqrt(v_new / c2) + ADAM_EPS) + ADAM_WD * w_ref[...])

    spec = pl.BlockSpec((tr, cols), lambda i: (i, 0))
    out = jax.ShapeDtypeStruct((rows, cols), F32)
    return pl.pallas_call(
        body, name=name, grid=(rows // tr,),
        in_specs=[spec, pl.BlockSpec((n_parts, tr, cols), lambda i: (0, i, 0)), spec, spec],
        out_specs=[spec] * 4, out_shape=[out] * 4,
        compiler_params=_params(2 * tr * cols * (7 * 4 + n_parts * parts.dtype.itemsize) + 8 * tr * cols * 4, 1),
    )(w, parts, m, v)


def _forward_backward(x2, target, norm_ffn1, norm_mix, pool_scale, norm_ffn2, norm_final, bias, weights):
    win1, wout1, wmix8, pw8, wru, wpu, wo, win2, wout2 = weights
    cos, sin = _rotary_tables()
    tables = _retention_tables()

    h1, gu1 = _ffn_forward(x2, norm_ffn1, win1, wout1, "ffn1_fwd")
    u, qkvg, p, gates = _mix_proj_forward(h1, norm_mix, wmix8, cos, sin)
    ret, o_ret = _retention_forward(qkvg, tables)
    h2 = _mix_tail_forward(p, gates, o_ret, h1, bias, pool_scale, pw8, wru, wpu, wo)
    dh3, gu2, loss_part, d_norm_final = _ffn_forward(h2, norm_ffn2, win2, wout2, "ffn2_fwd_loss",
                                                     head=(target, norm_final.reshape(1, D_MODEL)))

    dh2, dgu2, act2, xn2, df2, d_norm_ffn2 = _ffn_backward(dh3, h2, norm_ffn2, gu2, win2, wout2, "ffn2_bwd")
    d_win2, d_wout2 = _ffn_weight_grads(act2, df2, xn2, dgu2, 2)
    (dret, dgr, dgates, dpooled, d_wo, d_wru, d_wpu, d_pw, d_bias, d_scale) = _mix_tail_backward(
        dh2, p, gates, o_ret, ret, qkvg, bias, pool_scale, pw8, wru, wpu, wo)
    dq = _retention_backward_q(qkvg, dret, tables)
    dk, dv = _retention_backward_kv(qkvg, dret, tables)
    dh1, dproj, d_norm_mix = _mix_proj_backward(dq, dk, dv, dgr, dpooled, dgates, cos, sin, h1, norm_mix, dh2, wmix8)
    d_wmix = _weight_grad(
        u, dproj, N_DEV,
        lambda tt: pl.BlockSpec((tt, D_MODEL), lambda b, t: (t, 0)),
        lambda tt: pl.BlockSpec((tt, MIX_SHARD), lambda b, t: (t, b)),
        D_MODEL, MIX_SHARD, name="w_in_grad")
    grad_x, dgu1, act1, xn1, df1, d_norm_ffn1 = _ffn_backward(dh1, x2, norm_ffn1, gu1, win1, wout1, "ffn1_bwd")
    d_win1, d_wout1 = _ffn_weight_grads(act1, df1, xn1, dgu1, 1)

    partials = [d_win1, d_wout1, d_wmix, d_pw, d_wru, d_wpu, d_wo, d_win2, d_wout2]
    small = jnp.concatenate([d_norm_ffn1, d_norm_mix, d_scale, d_norm_ffn2, d_norm_final, d_bias,
                             jnp.zeros((1, D_MODEL), F32)], axis=0)
    return loss_part[0, 0], grad_x, partials, small


def kernel(x, norm_ffn1, ffn1_w_in, ffn1_w_out, norm_mix, w_in, gate_bias, pool_w, pool_scale, w_ret_up, w_pool_up, w_out, norm_ffn2, ffn2_w_in, ffn2_w_out, norm_final, loss_target, m_norm_ffn1, m_ffn1_w_in, m_ffn1_w_out, m_norm_mix, m_w_in, m_gate_bias, m_pool_w, m_pool_scale, m_w_ret_up, m_w_pool_up, m_w_out, m_norm_ffn2, m_ffn2_w_in, m_ffn2_w_out, m_norm_final, v_norm_ffn1, v_ffn1_w_in, v_ffn1_w_out, v_norm_mix, v_w_in, v_gate_bias, v_pool_w, v_pool_scale, v_w_ret_up, v_w_pool_up, v_w_out, v_norm_ffn2, v_ffn2_w_in, v_ffn2_w_out, v_norm_final):
    assert x.shape == (1, SEQ, D_MODEL) and ffn1_w_in.shape == (1, D_MODEL, FF_SHARD) and w_in.shape == (1, D_MODEL, MIX_SHARD)
    x2, target = x[0], loss_target[0]

    big = [ffn1_w_in, ffn1_w_out, w_in, pool_w, w_ret_up, w_pool_up, w_out, ffn2_w_in, ffn2_w_out]
    gathered = _all_gather([w[0].astype(BF16) for w in big] + [gate_bias[0]])
    win1, wout1, wmix8, pw8, wru, wpu, wo, win2, wout2, bias8 = gathered
    wout1, wout2 = wout1.reshape(D_FF, D_MODEL), wout2.reshape(D_FF, D_MODEL)
    wru, wpu, wo = (w.reshape(D_MODEL, D_MODEL) for w in (wru, wpu, wo))
    bias = bias8.transpose(1, 0, 2).reshape(2, D_MODEL)

    loss_part, grad_x, partials, small_part = _forward_backward(
        x2, target, norm_ffn1, norm_mix, pool_scale, norm_ffn2, norm_final, bias,
        (win1, wout1, wmix8, pw8, wru, wpu, wo, win2, wout2))

    received = _reduce_scatter_exchange(partials)
    zero_row = jnp.zeros((1, D_MODEL), F32)
    small = _all_reduce_rows(small_part)
    loss = lax.psum(loss_part, ("x", "y", "c"))

    names = ["ffn1_w_in", "ffn1_w_out", "w_in", "pool_w", "w_ret_up", "w_pool_up", "w_out", "ffn2_w_in", "ffn2_w_out"]
    moments_m = [m_ffn1_w_in, m_ffn1_w_out, m_w_in, m_pool_w, m_w_ret_up, m_w_pool_up, m_w_out, m_ffn2_w_in, m_ffn2_w_out]
    moments_v = [v_ffn1_w_in, v_ffn1_w_out, v_w_in, v_pool_w, v_w_ret_up, v_w_pool_up, v_w_out, v_ffn2_w_in, v_ffn2_w_out]
    results = {}
    for nm, w, parts, m, v in zip(names, big, received, moments_m, moments_v):
        shape2 = (-1, w.shape[-1])
        outs = _adamw(w.reshape(shape2), parts.reshape((N_DEV,) + w.reshape(shape2).shape), m.reshape(shape2),
                      v.reshape(shape2), name=f"adamw_{nm}")
        results[nm] = [o.reshape(w.shape) for o in outs]

    my_id = _linear_id(*_my_position())
    bias_cols = gate_bias.shape[-1]
    pad = lambda a: jnp.pad(a[0], ((0, 0), (0, D_MODEL - bias_cols)))
    pack = lambda a, b, c, d, e, gb: jnp.concatenate([a, b, c, d, e.reshape(1, D_MODEL), pad(gb), zero_row], axis=0)
    d_bias_mine = lax.dynamic_slice_in_dim(small[5:7], my_id * bias_cols, bias_cols, axis=1)
    g_small = jnp.concatenate([small[0:5], jnp.pad(d_bias_mine, ((0, 0), (0, D_MODEL - bias_cols))), zero_row], axis=0)
    s_outs = _adamw(pack(norm_ffn1, norm_mix, pool_scale, norm_ffn2, norm_final, gate_bias), g_small[None],
                    pack(m_norm_ffn1, m_norm_mix, m_pool_scale, m_norm_ffn2, m_norm_final, m_gate_bias),
                    pack(v_norm_ffn1, v_norm_mix, v_pool_scale, v_norm_ffn2, v_norm_final, v_gate_bias), name="adamw_small")
    for row, nm in enumerate(["norm_ffn1", "norm_mix", "pool_scale", "norm_ffn2"]):
        results[nm] = [o[row:row + 1] for o in s_outs]
    results["norm_final"] = [o[4] for o in s_outs]
    results["gate_bias"] = [o[5:7, :bias_cols][None] for o in s_outs]

    order = ["norm_ffn1", "ffn1_w_in", "ffn1_w_out", "norm_mix", "w_in", "gate_bias", "pool_w", "pool_scale",
             "w_ret_up", "w_pool_up", "w_out", "norm_ffn2", "ffn2_w_in", "ffn2_w_out", "norm_final"]
    return (loss, grad_x[None], *[results[nm][0] for nm in order], *[results[nm][1] for nm in order],
            *[results[nm][2] for nm in order], *[results[nm][3] for nm in order])
```

```python
import functools

import numpy as np
import jax
import jax.numpy as jnp
from jax import lax
from jax.experimental import pallas as pl
from jax.experimental.pallas import tpu as pltpu

F32 = jnp.float32
BF16 = jnp.bfloat16

N_DEV = 8
D_MODEL = 1024
SEQ = 4096
D_FF = 2816
FF_SHARD = 2 * D_FF // N_DEV
N_FF_GROUPS = N_DEV // 2
HEADS = 4
HEAD_DIM = 256
ROT_HALF = HEAD_DIM // 2
CHUNK = 64
RET_BLOCK = 256
POOL_WINDOWS = (2, 4, 8, 16)
POOL_GROUP_DIM = 256
HALO = 16
MIX_SHARD = 7 * D_MODEL // N_DEV
N_SEG = 7
ROPE_BASE = 10000.0
NORM_EPS = 1e-6
FFN_RES_WEIGHT = 0.5
ADAM_LR, ADAM_B1, ADAM_B2, ADAM_EPS, ADAM_WD, ADAM_STEP = 0.001, 0.9, 0.999, 1e-08, 0.01, 10

TOKEN_TILE = 256
VMEM_CAP_V7X = 64 * 1024 * 1024
MESH = pl.DeviceIdType.MESH
ANY = pl.BlockSpec(memory_space=pl.ANY)


def _vmem_limit(estimate_bytes):
    return int(min(estimate_bytes * 5 // 4 + (6 << 20), VMEM_CAP_V7X - (4 << 20)))


def _params(estimate_bytes, n_grid):
    return pltpu.CompilerParams(dimension_semantics=("arbitrary",) * n_grid,
                                vmem_limit_bytes=_vmem_limit(estimate_bytes))


def _dot(a, b):
    return jnp.dot(a, b, preferred_element_type=F32)


def _dot_nt(a, b):
    return lax.dot_general(a, b, (((1,), (1,)), ((), ())), preferred_element_type=F32)


def _dot_tn(a, b):
    return lax.dot_general(a, b, (((0,), (0,)), ((), ())), preferred_element_type=F32)


def _sig(x):
    return 1.0 / (1.0 + jnp.exp(-x))


def _rms(x, g):
    r = lax.rsqrt(jnp.mean(x * x, axis=-1, keepdims=True) + NORM_EPS)
    xhat = x * r
    return xhat * g, xhat, r


def _rms_bwd(dyg, xhat, r):
    return r * (dyg - xhat * jnp.mean(dyg * xhat, axis=-1, keepdims=True))


def _row_spec(tile, width, col=0):
    return pl.BlockSpec((tile, width), lambda i, c=col: (i, c))


def _full_spec(shape):
    return pl.BlockSpec(shape, lambda *_: (0,) * len(shape))


def _rotary_tables():
    inv_freq = (np.float32(ROPE_BASE) ** (-np.arange(ROT_HALF, dtype=np.float32) / np.float32(ROT_HALF))).astype(np.float32)
    ang = (np.arange(SEQ, dtype=np.float32)[:, None] * inv_freq[None, :]).astype(np.float32)
    return jnp.asarray(np.cos(ang.astype(np.float64)), F32), jnp.asarray(np.sin(ang.astype(np.float64)), F32)


def _retention_tables():
    log_gamma = np.log(1.0 - 2.0 ** (-5.0 - np.arange(HEADS, dtype=np.float64)))
    n = np.arange(RET_BLOCK)
    diff = (n[:, None] - n[None, :]).astype(np.float64)
    same = (n[:, None] // CHUNK) == (n[None, :] // CHUNK)
    earlier = (n[None, :] // CHUNK) < (n[:, None] // CHUNK)
    expo = np.where(same, np.abs(diff), diff)
    mask = np.where(same | earlier, np.exp(log_gamma[:, None, None] * expo[None]), 0.0)
    qdec = np.exp(log_gamma[:, None] * (n[None, :] + 1.0))[:, :, None]
    kdec = np.exp(log_gamma[:, None] * (RET_BLOCK - 1.0 - n[None, :]))[:, :, None]
    cdec = np.exp(log_gamma * RET_BLOCK)[:, None, None]
    return (jnp.asarray(mask, F32), jnp.asarray(qdec, F32), jnp.asarray(kdec, F32), jnp.asarray(cdec, F32))


def _my_position():
    return lax.axis_index("x"), lax.axis_index("y"), lax.axis_index("c")


def _linear_id(px, py, pc):
    return 4 * px + 2 * py + pc


def _when(pred, fn):
    if isinstance(pred, bool):
        if pred:
            fn()
    else:
        pl.when(pred)(fn)


class _GatherRide:
    def __init__(self, shards):
        self.args = list(shards)
        n = self.n = len(shards)
        self.out_shape = [jax.ShapeDtypeStruct((N_DEV,) + s.shape, s.dtype) for s in shards]
        self.scratch = [pltpu.SemaphoreType.DMA((n, 7)), pltpu.SemaphoreType.DMA((n, 7)), pltpu.SemaphoreType.DMA((n,))]

    def _plan(self, src, out, sems):
        send_sems, recv_sems, local_sem = sems
        x, y, c = _my_position()
        me, sibling = (x, y, c), (x, y, 1 - c)
        chips = [(1 - x, y), (x, 1 - y), (1 - x, 1 - y)]

        def copy(t, k, block, to, from_src=False):
            rows = out[t].at[_linear_id(*block)]
            return pltpu.make_async_remote_copy(
                src_ref=src[t] if from_src else rows, dst_ref=rows,
                send_sem=send_sems.at[t, k], recv_sem=recv_sems.at[t, k],
                device_id=to, device_id_type=MESH)

        local = [pltpu.make_async_copy(src[t], out[t].at[_linear_id(*me)], local_sem.at[t]) for t in range(self.n)]
        return copy, local, me, sibling, chips, c

    def begin(self, first, src, out, sems):
        copy, local, me, sibling, chips, c = self._plan(src, out, sems)

        def start():
            for cp in local:
                cp.start()
            for t in range(self.n):
                copy(t, 0, me, sibling, from_src=True).start()
                for j, chip in enumerate(chips):
                    copy(t, 1 + j, me, (*chip, c), from_src=True).start()

        _when(first, start)

    def finish(self, mid, last, src, out, sems):
        copy, local, me, sibling, chips, c = self._plan(src, out, sems)

        def pass_on():
            for j, chip in enumerate(chips):
                for t in range(self.n):
                    copy(t, 1 + j, (*chip, c), me).wait_recv()
                    copy(t, 4 + j, (*chip, c), sibling).start()

        def drain():
            for t in range(self.n):
                copy(t, 0, sibling, me).wait_recv()
                for j, chip in enumerate(chips):
                    copy(t, 4 + j, (*chip, 1 - c), me).wait_recv()
            for t in range(self.n):
                copy(t, 0, me, sibling, from_src=True).wait_send()
                for j, chip in enumerate(chips):
                    copy(t, 1 + j, me, (*chip, c), from_src=True).wait_send()
                    copy(t, 4 + j, (*chip, c), sibling).wait_send()
            for cp in local:
                cp.wait()

        _when(mid, pass_on)
        _when(last, drain)


class _ScatterRide:
    def __init__(self, partials):
        self.args = list(partials)
        n = self.n = len(partials)
        self.out_shape = [jax.ShapeDtypeStruct(p.shape, p.dtype) for p in partials]
        self.scratch = [pltpu.SemaphoreType.DMA((n, 7)), pltpu.SemaphoreType.DMA((n, 7)), pltpu.SemaphoreType.DMA((n,))]

    def _plan(self, src, out, sems):
        send_sems, recv_sems, local_sem = sems
        x, y, c = _my_position()

        def peer(k):
            return (x ^ (k >> 2), y ^ ((k >> 1) & 1), c ^ (k & 1))

        copies = [pltpu.make_async_remote_copy(
            src_ref=src[t].at[_linear_id(*peer(k))], dst_ref=out[t].at[k],
            send_sem=send_sems.at[t, k - 1], recv_sem=recv_sems.at[t, k - 1],
            device_id=peer(k), device_id_type=MESH) for t in range(self.n) for k in range(1, N_DEV)]
        local = [pltpu.make_async_copy(src[t].at[_linear_id(x, y, c)], out[t].at[0], local_sem.at[t])
                 for t in range(self.n)]
        return copies, local

    def begin(self, first, src, out, sems):
        copies, local = self._plan(src, out, sems)

        def start():
            for cp in local + copies:
                cp.start()

        _when(first, start)

    def finish(self, mid, last, src, out, sems):
        copies, local = self._plan(src, out, sems)

        def drain():
            for cp in copies:
                cp.wait_recv()
            for cp in copies:
                cp.wait_send()
            for cp in local:
                cp.wait()

        _when(last, drain)


def _call(body, *, name, grid, in_specs, out_specs, out_shape, scratch_shapes, vmem_bytes, args, ride=None):
    n_in, n_out, n_s = len(in_specs), len(out_specs), len(scratch_shapes)
    params = _params(vmem_bytes, len(grid))
    if ride is None:
        outs = pl.pallas_call(body, name=name, grid=grid, in_specs=in_specs, out_specs=out_specs, out_shape=out_shape,
                              scratch_shapes=scratch_shapes, compiler_params=params)(*args)
        return list(outs), []
    total = int(np.prod(grid))

    def riding_body(*refs):
        a = n_in
        b = a + ride.n
        c = b + n_out
        d = c + ride.n
        e = d + n_s
        step = pl.program_id(0)
        for axis in range(1, len(grid)):
            step = step * grid[axis] + pl.program_id(axis)
        ride.begin(step == 0, refs[a:b], refs[c:d], refs[e:])
        body(*refs[:a], *refs[b:c], *refs[d:e])
        ride.finish(step == (3 * total) // 4, step == total - 1, refs[a:b], refs[c:d], refs[e:])

    outs = pl.pallas_call(
        riding_body, name=name, grid=grid, in_specs=list(in_specs) + [ANY] * ride.n,
        out_specs=list(out_specs) + [ANY] * ride.n, out_shape=list(out_shape) + ride.out_shape,
        scratch_shapes=list(scratch_shapes) + ride.scratch, compiler_params=params)(*args, *ride.args)
    return list(outs[:n_out]), list(outs[n_out:])


def _alone(ride, name):
    def body(*refs):
        src, out, sems = refs[:ride.n], refs[ride.n:2 * ride.n], refs[2 * ride.n:]
        ride.begin(True, src, out, sems)
        ride.finish(True, True, src, out, sems)

    return list(pl.pallas_call(body, name=name, out_shape=ride.out_shape, in_specs=[ANY] * ride.n,
                               out_specs=[ANY] * ride.n, scratch_shapes=ride.scratch)(*ride.args))


def _all_reduce_rows(block):
    rows, width = block.shape

    def body(x_ref, sum_ref, gathered, send_sems, recv_sems, local_sem):
        x, y, c = _my_position()
        me, sibling = (x, y, c), (x, y, 1 - c)
        chips = [(1 - x, y), (x, 1 - y), (1 - x, 1 - y)]

        def slot(px, py, pc):
            return gathered.at[_linear_id(px, py, pc)]

        def copy(k, block_of, to, from_src=False):
            return pltpu.make_async_remote_copy(
                src_ref=x_ref if from_src else slot(*block_of), dst_ref=slot(*block_of),
                send_sem=send_sems.at[k], recv_sem=recv_sems.at[k], device_id=to, device_id_type=MESH)

        mine = pltpu.make_async_copy(x_ref, slot(*me), local_sem)
        mine.start()
        first = [copy(0, me, sibling, from_src=True)]
        first += [copy(1 + j, me, (*chip, c), from_src=True) for j, chip in enumerate(chips)]
        for cp in first:
            cp.start()
        passed = [copy(4 + j, (*chip, c), sibling) for j, chip in enumerate(chips)]
        for j, chip in enumerate(chips):
            copy(1 + j, (*chip, c), me).wait_recv()
            passed[j].start()
        copy(0, sibling, me).wait_recv()
        for j, chip in enumerate(chips):
            copy(4 + j, (*chip, 1 - c), me).wait_recv()
        for cp in first + passed:
            cp.wait_send()
        mine.wait()
        total = gathered[0]
        for d in range(1, N_DEV):
            total = total + gathered[d]
        sum_ref[...] = total

    return pl.pallas_call(
        body, name="small_grads_all_reduce",
        out_shape=jax.ShapeDtypeStruct((rows, width), F32),
        in_specs=[pl.BlockSpec(memory_space=pltpu.VMEM)],
        out_specs=pl.BlockSpec(memory_space=pltpu.VMEM),
        scratch_shapes=[pltpu.VMEM((N_DEV, rows, width), F32),
                        pltpu.SemaphoreType.DMA((7,)), pltpu.SemaphoreType.DMA((7,)), pltpu.SemaphoreType.DMA],
    )(block)


def _load_ffn_weights(win_hbm, wout_hbm, win, wout, sem):
    a = pltpu.make_async_copy(win_hbm, win, sem.at[0])
    b = pltpu.make_async_copy(wout_hbm, wout, sem.at[1])
    a.start()
    b.start()
    a.wait()
    b.wait()


def _ffn_forward(h_in, gain, win8, wout, name, head=None, ride=None):
    tm, nt = TOKEN_TILE, SEQ // TOKEN_TILE

    def body(*refs):
        if head is None:
            x_ref, g_ref, win_hbm, wout_hbm, out_ref, gu_ref, win, wout, sem = refs
        else:
            x_ref, g_ref, win_hbm, wout_hbm, tgt_ref, gf_ref, out_ref, gu_ref, loss_ref, dgf_ref, win, wout, sem = refs
        i = pl.program_id(0)

        @pl.when(i == 0)
        def _():
            _load_ffn_weights(win_hbm, wout_hbm, win, wout, sem)
            if head is not None:
                loss_ref[...] = jnp.zeros_like(loss_ref)
                dgf_ref[...] = jnp.zeros_like(dgf_ref)

        x = x_ref[...]
        xn, _, _ = _rms(x, g_ref[...])
        xb = xn.astype(BF16)
        acc = jnp.zeros((tm, D_MODEL), F32)
        for j in range(N_FF_GROUPS):
            gate = _dot(xb, win[j])
            up = _dot(xb, win[j + N_FF_GROUPS])
            gu_ref[j] = gate.astype(BF16)
            gu_ref[j + N_FF_GROUPS] = up.astype(BF16)
            act = gate * _sig(gate) * up
            acc = acc + _dot(act.astype(BF16), wout[pl.ds(j * FF_SHARD, FF_SHARD), :])
        h = x + FFN_RES_WEIGHT * acc
        if head is None:
            out_ref[...] = h
        else:
            gf = gf_ref[...]
            y, hhat, r = _rms(h, gf)
            err = y - tgt_ref[...]
            loss_ref[...] += jnp.full(loss_ref.shape, 0.5 / D_MODEL * jnp.sum(err * err), F32)
            dy = err * (1.0 / D_MODEL)
            dgf_ref[...] += jnp.sum(dy * hhat, axis=0, keepdims=True)
            out_ref[...] = _rms_bwd(dy * gf, hhat, r)

    weights = 2 * D_MODEL * 2 * D_FF + 2 * D_FF * D_MODEL
    tiles = 2 * (2 * 4 * tm * D_MODEL + 2 * tm * 2 * D_FF) + (2 * 4 * tm * D_MODEL if head else 0)
    in_specs = [_row_spec(tm, D_MODEL), _full_spec((1, D_MODEL)), ANY, ANY]
    out_shape = [jax.ShapeDtypeStruct((SEQ, D_MODEL), F32), jax.ShapeDtypeStruct((N_DEV, SEQ, FF_SHARD), BF16)]
    out_specs = [_row_spec(tm, D_MODEL), pl.BlockSpec((N_DEV, tm, FF_SHARD), lambda i: (0, i, 0))]
    args = [h_in, gain, win8, wout]
    if head is not None:
        in_specs += [_row_spec(tm, D_MODEL), _full_spec((1, D_MODEL))]
        out_shape += [jax.ShapeDtypeStruct((1, 128), F32), jax.ShapeDtypeStruct((1, D_MODEL), F32)]
        out_specs += [_full_spec((1, 128)), _full_spec((1, D_MODEL))]
        args += list(head)
    return _call(
        body, name=name, grid=(nt,), in_specs=in_specs, out_specs=out_specs, out_shape=out_shape,
        scratch_shapes=[pltpu.VMEM((N_DEV, D_MODEL, FF_SHARD), BF16), pltpu.VMEM((D_FF, D_MODEL), BF16),
                        pltpu.SemaphoreType.DMA((2,))],
        vmem_bytes=weights + tiles + 16 * tm * FF_SHARD * 4, args=args, ride=ride)


def _ffn_backward(dh_out, h_in, gain, gu, win8, wout, name, ride=None):
    tm, nt = TOKEN_TILE, SEQ // TOKEN_TILE

    def body(dh_ref, x_ref, g_ref, gu_ref, win_hbm, wout_hbm,
             dhin_ref, dgu_ref, act_ref, xn_ref, df_ref, dg_ref, win, wout, sem):
        i = pl.program_id(0)

        @pl.when(i == 0)
        def _():
            _load_ffn_weights(win_hbm, wout_hbm, win, wout, sem)
            dg_ref[...] = jnp.zeros_like(dg_ref)

        dh = dh_ref[...]
        g = g_ref[...]
        xn, xhat, r = _rms(x_ref[...], g)
        df = (FFN_RES_WEIGHT * dh).astype(BF16)
        dxn = jnp.zeros((tm, D_MODEL), F32)
        for j in range(N_FF_GROUPS):
            gate = gu_ref[j].astype(F32)
            up = gu_ref[j + N_FF_GROUPS].astype(F32)
            dact = _dot_nt(df, wout[pl.ds(j * FF_SHARD, FF_SHARD), :])
            s = _sig(gate)
            silu = gate * s
            dgate = (dact * up * (s * (1.0 + gate * (1.0 - s)))).astype(BF16)
            dup = (dact * silu).astype(BF16)
            act_ref[j] = (silu * up).astype(BF16)
            dgu_ref[j] = dgate
            dgu_ref[j + N_FF_GROUPS] = dup
            dxn = dxn + _dot_nt(dgate, win[j]) + _dot_nt(dup, win[j + N_FF_GROUPS])
        dg_ref[...] += jnp.sum(dxn * xhat, axis=0, keepdims=True)
        dhin_ref[...] = dh + _rms_bwd(dxn * g, xhat, r)
        xn_ref[...] = xn.astype(BF16)
        df_ref[...] = df

    weights = 2 * D_MODEL * 2 * D_FF + 2 * D_FF * D_MODEL
    tiles = 2 * (3 * 4 * tm * D_MODEL + 2 * tm * (2 * 2 * D_FF + D_FF) + 2 * 2 * tm * D_MODEL)
    gu_spec = pl.BlockSpec((N_DEV, tm, FF_SHARD), lambda i: (0, i, 0))
    return _call(
        body, name=name, grid=(nt,),
        in_specs=[_row_spec(tm, D_MODEL), _row_spec(tm, D_MODEL), _full_spec((1, D_MODEL)), gu_spec, ANY, ANY],
        out_specs=[_row_spec(tm, D_MODEL), gu_spec, pl.BlockSpec((N_FF_GROUPS, tm, FF_SHARD), lambda i: (0, i, 0)),
                   _row_spec(tm, D_MODEL), _row_spec(tm, D_MODEL), _full_spec((1, D_MODEL))],
        out_shape=[jax.ShapeDtypeStruct((SEQ, D_MODEL), F32), jax.ShapeDtypeStruct((N_DEV, SEQ, FF_SHARD), BF16),
                   jax.ShapeDtypeStruct((N_FF_GROUPS, SEQ, FF_SHARD), BF16), jax.ShapeDtypeStruct((SEQ, D_MODEL), BF16),
                   jax.ShapeDtypeStruct((SEQ, D_MODEL), BF16), jax.ShapeDtypeStruct((1, D_MODEL), F32)],
        scratch_shapes=[pltpu.VMEM((N_DEV, D_MODEL, FF_SHARD), BF16), pltpu.VMEM((D_FF, D_MODEL), BF16),
                        pltpu.SemaphoreType.DMA((2,))],
        vmem_bytes=weights + tiles + 20 * tm * FF_SHARD * 4, args=[dh_out, h_in, gain, gu, win8, wout], ride=ride)


def _weight_grad(x, g, n_out, x_spec, g_spec, k_dim, n_dim, name, tt=512, ride=None):
    nt = SEQ // tt

    def body(x_ref, g_ref, out_ref, acc):
        t = pl.program_id(1)

        @pl.when(t == 0)
        def _():
            acc[...] = jnp.zeros_like(acc)

        acc[...] += _dot_tn(x_ref[...], g_ref[...])

        @pl.when(t == nt - 1)
        def _():
            out_ref[...] = acc[...].astype(BF16)

    outs, ride_outs = _call(
        body, name=name, grid=(n_out, nt), in_specs=[x_spec(tt), g_spec(tt)],
        out_specs=[pl.BlockSpec((None, k_dim, n_dim), lambda b, t: (b, 0, 0))],
        out_shape=[jax.ShapeDtypeStruct((n_out, k_dim, n_dim), BF16)],
        scratch_shapes=[pltpu.VMEM((k_dim, n_dim), F32)],
        vmem_bytes=2 * 2 * tt * (k_dim + n_dim) + 8 * k_dim * n_dim + 4 * k_dim * n_dim, args=[x, g], ride=ride)
    return outs[0], ride_outs


def _ffn_w_out_grad(act, df, tag, ride=None):
    d_wout, ride_outs = _weight_grad(
        act, df, N_FF_GROUPS,
        lambda tt: pl.BlockSpec((None, tt, FF_SHARD), lambda b, t: (b, t, 0)),
        lambda tt: pl.BlockSpec((tt, D_MODEL), lambda b, t: (t, 0)),
        FF_SHARD, D_MODEL, name=f"ffn{tag}_w_out_grad", ride=ride)
    return d_wout.reshape(N_DEV, D_FF // N_DEV, D_MODEL), ride_outs


def _ffn_w_in_grad(xn, dgu, tag, ride=None):
    return _weight_grad(
        xn, dgu, N_DEV,
        lambda tt: pl.BlockSpec((tt, D_MODEL), lambda b, t: (t, 0)),
        lambda tt: pl.BlockSpec((None, tt, FF_SHARD), lambda b, t: (b, t, 0)),
        D_MODEL, FF_SHARD, name=f"ffn{tag}_w_in_grad", ride=ride)


def _load_mix_weight(wmix_hbm, wmix, sem):
    copies = [pltpu.make_async_copy(wmix_hbm.at[d], wmix.at[:, pl.ds(d * MIX_SHARD, MIX_SHARD)], sem.at[d])
              for d in range(N_DEV)]
    for cp in copies:
        cp.start()
    for cp in copies:
        cp.wait()


def _load_pool_weight(pw_hbm, pw, sem):
    rows = POOL_GROUP_DIM // N_DEV
    copies = [pltpu.make_async_copy(pw_hbm.at[d], pw.at[:, pl.ds(d * rows, rows), :], sem.at[d]) for d in range(N_DEV)]
    for cp in copies:
        cp.start()
    for cp in copies:
        cp.wait()


def _rotate(x1, x2, cos, sin):
    return x1 * cos - x2 * sin, x1 * sin + x2 * cos


def _mix_proj_forward(h1, gain, wmix8, cos, sin, ride=None):
    tm, nt = TOKEN_TILE, SEQ // TOKEN_TILE
    k_scale = HEAD_DIM ** -0.5

    def body(h_ref, g_ref, wmix_hbm, cos_ref, sin_ref, u_ref, qkvg_ref, p_ref, gates_ref, wmix, sem):
        @pl.when(pl.program_id(0) == 0)
        def _():
            _load_mix_weight(wmix_hbm, wmix, sem)

        u = _rms(h_ref[...], g_ref[...])[0].astype(BF16)
        u_ref[...] = u
        cos_t, sin_t = cos_ref[...], sin_ref[...]
        for seg in range(N_SEG):
            pr = _dot(u, wmix[:, pl.ds(seg * D_MODEL, D_MODEL)])
            if seg < 2:
                scale = 1.0 if seg == 0 else k_scale
                for hd in range(HEADS):
                    lo = hd * HEAD_DIM
                    o1, o2 = _rotate(pr[:, lo:lo + ROT_HALF], pr[:, lo + ROT_HALF:lo + HEAD_DIM], cos_t, sin_t)
                    qkvg_ref[:, pl.ds(seg * D_MODEL + lo, ROT_HALF)] = (o1 * scale).astype(BF16)
                    qkvg_ref[:, pl.ds(seg * D_MODEL + lo + ROT_HALF, ROT_HALF)] = (o2 * scale).astype(BF16)
            elif seg < 4:
                qkvg_ref[:, pl.ds(seg * D_MODEL, D_MODEL)] = pr.astype(BF16)
            elif seg == 4:
                p_ref[...] = pr
            else:
                gates_ref[:, pl.ds((seg - 5) * D_MODEL, D_MODEL)] = pr.astype(BF16)

    est = 2 * D_MODEL * N_SEG * D_MODEL + 2 * tm * (4 * D_MODEL + 2 * D_MODEL + 2 * 4 * D_MODEL + 4 * D_MODEL + 2 * 2 * D_MODEL)
    return _call(
        body, name="mix_proj_fwd", grid=(nt,),
        in_specs=[_row_spec(tm, D_MODEL), _full_spec((1, D_MODEL)), ANY, _row_spec(tm, ROT_HALF), _row_spec(tm, ROT_HALF)],
        out_specs=[_row_spec(tm, D_MODEL), _row_spec(tm, 4 * D_MODEL), _row_spec(tm, D_MODEL), _row_spec(tm, 2 * D_MODEL)],
        out_shape=[jax.ShapeDtypeStruct((SEQ, D_MODEL), BF16), jax.ShapeDtypeStruct((SEQ, 4 * D_MODEL), BF16),
                   jax.ShapeDtypeStruct((SEQ, D_MODEL), F32), jax.ShapeDtypeStruct((SEQ, 2 * D_MODEL), BF16)],
        scratch_shapes=[pltpu.VMEM((D_MODEL, N_SEG * D_MODEL), BF16), pltpu.SemaphoreType.DMA((N_DEV,))],
        vmem_bytes=est + 8 * tm * D_MODEL * 4, args=[h1, gain, wmix8, cos, sin], ride=ride)


def _head_block_spec(seg, reverse=False):
    nb = SEQ // RET_BLOCK
    if reverse:
        return pl.BlockSpec((RET_BLOCK, HEAD_DIM), lambda h, i, s=seg: (nb - 1 - i, s * HEADS + h))
    return pl.BlockSpec((RET_BLOCK, HEAD_DIM), lambda h, i, s=seg: (i, s * HEADS + h))


def _table_specs():
    return [pl.BlockSpec((None, RET_BLOCK, RET_BLOCK), lambda h, i: (h, 0, 0)),
            pl.BlockSpec((None, RET_BLOCK, 1), lambda h, i: (h, 0, 0)),
            pl.BlockSpec((None, RET_BLOCK, 1), lambda h, i: (h, 0, 0)),
            pl.BlockSpec((None, 1, 1), lambda h, i: (h, 0, 0))]


def _retention_forward(qkvg, tables, ride=None):
    nb = SEQ // RET_BLOCK

    def body(q_ref, k_ref, v_ref, gr_ref, mask_ref, qdec_ref, kdec_ref, cdec_ref, ret_ref, o_ref, state):
        @pl.when(pl.program_id(1) == 0)
        def _():
            state[...] = jnp.zeros_like(state)

        q, k, v = q_ref[...], k_ref[...], v_ref[...]
        scores = _dot_nt(q, k) * mask_ref[...]
        inner = _dot(scores.astype(BF16), v)
        cross = _dot((q.astype(F32) * qdec_ref[...]).astype(BF16), state[...].astype(BF16))
        ret = inner + cross
        state[...] = state[...] * cdec_ref[...] + _dot_tn((k.astype(F32) * kdec_ref[...]).astype(BF16), v)
        ret_ref[...] = ret
        retn = ret * lax.rsqrt(jnp.mean(ret * ret, axis=-1, keepdims=True) + NORM_EPS)
        gr = gr_ref[...].astype(F32)
        o_ref[...] = (retn * (gr * _sig(gr))).astype(BF16)

    return _call(
        body, name="retention_fwd", grid=(HEADS, nb),
        in_specs=[_head_block_spec(0), _head_block_spec(1), _head_block_spec(2), _head_block_spec(3)] + _table_specs(),
        out_specs=[pl.BlockSpec((RET_BLOCK, HEAD_DIM), lambda h, i: (i, h))] * 2,
        out_shape=[jax.ShapeDtypeStruct((SEQ, D_MODEL), F32), jax.ShapeDtypeStruct((SEQ, D_MODEL), BF16)],
        scratch_shapes=[pltpu.VMEM((HEAD_DIM, HEAD_DIM), F32)],
        vmem_bytes=16 * RET_BLOCK * HEAD_DIM * 4, args=[qkvg, qkvg, qkvg, qkvg, *tables], ride=ride)


def _retention_backward_q(qkvg, dret, tables, ride=None):
    nb = SEQ // RET_BLOCK

    def body(k_ref, v_ref, do_ref, mask_ref, qdec_ref, kdec_ref, cdec_ref, dq_ref, state):
        @pl.when(pl.program_id(1) == 0)
        def _():
            state[...] = jnp.zeros_like(state)

        k, v, do = k_ref[...], v_ref[...], do_ref[...]
        dscores = _dot_nt(do, v) * mask_ref[...]
        dq_ref[...] = _dot(dscores.astype(BF16), k) + _dot_nt(do, state[...].astype(BF16)) * qdec_ref[...]
        state[...] = state[...] * cdec_ref[...] + _dot_tn((k.astype(F32) * kdec_ref[...]).astype(BF16), v)

    return _call(
        body, name="retention_bwd_q", grid=(HEADS, nb),
        in_specs=[_head_block_spec(1), _head_block_spec(2), pl.BlockSpec((RET_BLOCK, HEAD_DIM), lambda h, i: (i, h))] + _table_specs(),
        out_specs=[pl.BlockSpec((RET_BLOCK, HEAD_DIM), lambda h, i: (i, h))],
        out_shape=[jax.ShapeDtypeStruct((SEQ, D_MODEL), F32)],
        scratch_shapes=[pltpu.VMEM((HEAD_DIM, HEAD_DIM), F32)],
        vmem_bytes=16 * RET_BLOCK * HEAD_DIM * 4, args=[qkvg, qkvg, dret, *tables], ride=ride)


def _retention_backward_kv(qkvg, dret, tables, ride=None):
    nb = SEQ // RET_BLOCK

    def body(q_ref, k_ref, v_ref, do_ref, mask_ref, qdec_ref, kdec_ref, cdec_ref, dk_ref, dv_ref, gstate):
        @pl.when(pl.program_id(1) == 0)
        def _():
            gstate[...] = jnp.zeros_like(gstate)

        q, k, v, do = q_ref[...], k_ref[...], v_ref[...], do_ref[...]
        mask = mask_ref[...]
        scores = (_dot_nt(q, k) * mask).astype(BF16)
        dscores = (_dot_nt(do, v) * mask).astype(BF16)
        gs = gstate[...].astype(BF16)
        dk_ref[...] = _dot_tn(dscores, q) + _dot_nt(v, gs) * kdec_ref[...]
        dv_ref[...] = _dot_tn(scores, do) + _dot((k.astype(F32) * kdec_ref[...]).astype(BF16), gs)
        gstate[...] = gstate[...] * cdec_ref[...] + _dot_tn((q.astype(F32) * qdec_ref[...]).astype(BF16), do)

    rev = lambda h, i: (nb - 1 - i, h)
    return _call(
        body, name="retention_bwd_kv", grid=(HEADS, nb),
        in_specs=[_head_block_spec(0, True), _head_block_spec(1, True), _head_block_spec(2, True),
                  pl.BlockSpec((RET_BLOCK, HEAD_DIM), rev)] + _table_specs(),
        out_specs=[pl.BlockSpec((RET_BLOCK, HEAD_DIM), rev)] * 2,
        out_shape=[jax.ShapeDtypeStruct((SEQ, D_MODEL), F32)] * 2,
        scratch_shapes=[pltpu.VMEM((HEAD_DIM, HEAD_DIM), F32)],
        vmem_bytes=20 * RET_BLOCK * HEAD_DIM * 4, args=[qkvg, qkvg, qkvg, dret, *tables], ride=ride)


def _pooled(p_ext, first_row):
    rows = p_ext.shape[0]
    t = first_row + lax.broadcasted_iota(jnp.int32, (rows - HALO, 1), 0)
    outs = []
    for g, w in enumerate(POOL_WINDOWS):
        e = p_ext[:, g * POOL_GROUP_DIM:(g + 1) * POOL_GROUP_DIM]
        s, span = e, 1
        while span < w:
            s = s + pltpu.roll(s, span, 0)
            span *= 2
        count = jnp.minimum(t + 1, w).astype(F32)
        outs.append(s[HALO:] / count - e[HALO:])
    return outs


def _pooled_transpose(d_ext, first_row):
    rows = d_ext.shape[0]
    t = first_row + lax.broadcasted_iota(jnp.int32, (rows, 1), 0)
    outs = []
    for g, w in enumerate(POOL_WINDOWS):
        d = d_ext[:, g * POOL_GROUP_DIM:(g + 1) * POOL_GROUP_DIM]
        e = jnp.where(t < SEQ, d / jnp.minimum(t + 1, w).astype(F32), 0.0)
        s, span = e, 1
        while span < w:
            s = s + pltpu.roll(s, rows - span, 0)
            span *= 2
        outs.append(s[:rows - HALO] - d[:rows - HALO])
    return outs


def _mix_tail_specs(tm):
    halo_blocks = tm // HALO
    return [
        _row_spec(tm, D_MODEL),
        pl.BlockSpec((HALO, D_MODEL), lambda i: (jnp.maximum(i * halo_blocks - 1, 0), 0)),
        _row_spec(tm, 2 * D_MODEL),
        _row_spec(tm, D_MODEL),
        _full_spec((2, D_MODEL)), _full_spec((1, D_MODEL)), ANY,
        _full_spec((D_MODEL, D_MODEL)), _full_spec((D_MODEL, D_MODEL)), _full_spec((D_MODEL, D_MODEL)),
    ]


def _mix_tail_compute(i, tm, p_ref, halo_ref, gates_ref, oret_ref, bias_ref, scale_ref, pw, wru_ref, wpu_ref):
    halo = jnp.where(i > 0, halo_ref[...], 0.0)
    pooled = _pooled(jnp.concatenate([halo, p_ref[...]], axis=0), i * tm)
    pooled = [x.astype(BF16) for x in pooled]
    mixed = jnp.concatenate([_dot(pooled[g], pw[g]) for g in range(len(POOL_WINDOWS))], axis=-1)
    pool_out = (mixed * scale_ref[...]).astype(BF16)
    o_ret = oret_ref[...]
    a = _dot(o_ret, wru_ref[...])
    b = _dot(pool_out, wpu_ref[...])
    z = gates_ref[...].astype(F32)
    g0 = _sig(z[:, :D_MODEL] + bias_ref[0:1, :])
    g1 = _sig(z[:, D_MODEL:] + bias_ref[1:2, :])
    merged = (g0 * a + g1 * b).astype(BF16)
    return pooled, mixed, pool_out, o_ret, a, b, g0, g1, merged


def _mix_tail_forward(p, gates, o_ret, h1, bias, scale, pw8, wru, wpu, wo, ride=None):
    tm, nt = TOKEN_TILE, SEQ // TOKEN_TILE

    def body(p_ref, halo_ref, gates_ref, oret_ref, bias_ref, scale_ref, pw_hbm, wru_ref, wpu_ref, wo_ref, h1_ref,
             h2_ref, pw, sem):
        i = pl.program_id(0)

        @pl.when(i == 0)
        def _():
            _load_pool_weight(pw_hbm, pw, sem)

        merged = _mix_tail_compute(i, tm, p_ref, halo_ref, gates_ref, oret_ref, bias_ref, scale_ref, pw, wru_ref, wpu_ref)[-1]
        h2_ref[...] = h1_ref[...] + _dot(merged, wo_ref[...])

    est = 3 * 2 * 2 * D_MODEL * D_MODEL + 2 * tm * D_MODEL * (4 + 4 + 2 + 4 + 4) + 16 * tm * D_MODEL * 4
    return _call(
        body, name="mix_tail_fwd", grid=(nt,),
        in_specs=_mix_tail_specs(tm) + [_row_spec(tm, D_MODEL)],
        out_specs=[_row_spec(tm, D_MODEL)], out_shape=[jax.ShapeDtypeStruct((SEQ, D_MODEL), F32)],
        scratch_shapes=[pltpu.VMEM((len(POOL_WINDOWS), POOL_GROUP_DIM, POOL_GROUP_DIM), BF16), pltpu.SemaphoreType.DMA((N_DEV,))],
        vmem_bytes=est, args=[p, p, gates, o_ret, bias, scale, pw8, wru, wpu, wo, h1], ride=ride)


def _mix_tail_backward(dh2, p, gates, o_ret, ret, qkvg, bias, scale, pw8, wru, wpu, wo, ride=None):
    tm, nt = TOKEN_TILE, SEQ // TOKEN_TILE
    n_groups = len(POOL_WINDOWS)
    rows_per_dev = POOL_GROUP_DIM // N_DEV

    def body(p_ref, halo_ref, gates_ref, oret_ref, bias_ref, scale_ref, pw_hbm, wru_ref, wpu_ref, wo_ref,
             dh2_ref, ret_ref, gr_ref,
             dret_ref, dgr_ref, dgates_ref, dpooled_ref, dwo_ref, dwru_ref, dwpu_ref, dpw_ref, dbias_ref, dscale_ref,
             pw, sem, acc_wo, acc_wru, acc_wpu, acc_pw):
        i = pl.program_id(0)

        @pl.when(i == 0)
        def _():
            _load_pool_weight(pw_hbm, pw, sem)
            for ref in (acc_wo, acc_wru, acc_wpu, acc_pw, dbias_ref, dscale_ref):
                ref[...] = jnp.zeros_like(ref)

        pooled, mixed, pool_out, o_ret, a, b, g0, g1, merged = _mix_tail_compute(
            i, tm, p_ref, halo_ref, gates_ref, oret_ref, bias_ref, scale_ref, pw, wru_ref, wpu_ref)
        dh2 = dh2_ref[...].astype(BF16)
        dm = _dot_nt(dh2, wo_ref[...])
        acc_wo[...] += _dot_tn(merged, dh2)
        da = (dm * g0).astype(BF16)
        db = (dm * g1).astype(BF16)
        dz0 = dm * a * g0 * (1.0 - g0)
        dz1 = dm * b * g1 * (1.0 - g1)
        dbias_ref[0:1, :] += jnp.sum(dz0, axis=0, keepdims=True)
        dbias_ref[1:2, :] += jnp.sum(dz1, axis=0, keepdims=True)
        dgates_ref[:, pl.ds(0, D_MODEL)] = dz0.astype(BF16)
        dgates_ref[:, pl.ds(D_MODEL, D_MODEL)] = dz1.astype(BF16)
        acc_wru[...] += _dot_tn(o_ret, da)
        acc_wpu[...] += _dot_tn(pool_out, db)
        d_oret = _dot_nt(da, wru_ref[...])
        d_pool_out = _dot_nt(db, wpu_ref[...])
        dscale_ref[...] += jnp.sum(d_pool_out * mixed, axis=0, keepdims=True)
        dmixed = (d_pool_out * scale_ref[...]).astype(BF16)
        for g in range(n_groups):
            dmg = dmixed[:, g * POOL_GROUP_DIM:(g + 1) * POOL_GROUP_DIM]
            acc_pw[g] += _dot_tn(pooled[g], dmg)
            dpooled_ref[:, pl.ds(g * POOL_GROUP_DIM, POOL_GROUP_DIM)] = _dot_nt(dmg, pw[g])
        gr = gr_ref[...].astype(F32)
        s = _sig(gr)
        silu = gr * s
        for hd in range(HEADS):
            cols = slice(hd * HEAD_DIM, (hd + 1) * HEAD_DIM)
            r_h = ret_ref[:, cols]
            rr = lax.rsqrt(jnp.mean(r_h * r_h, axis=-1, keepdims=True) + NORM_EPS)
            rhat = r_h * rr
            do_h = d_oret[:, cols]
            dgr_ref[:, cols] = (do_h * rhat * (s[:, cols] * (1.0 + gr[:, cols] * (1.0 - s[:, cols])))).astype(BF16)
            dret_ref[:, cols] = _rms_bwd(do_h * silu[:, cols], rhat, rr).astype(BF16)

        @pl.when(i == nt - 1)
        def _():
            rows = D_MODEL // N_DEV
            for d in range(N_DEV):
                dwo_ref[d] = acc_wo[pl.ds(d * rows, rows), :].astype(BF16)
                dwru_ref[d] = acc_wru[pl.ds(d * rows, rows), :].astype(BF16)
                dwpu_ref[d] = acc_wpu[pl.ds(d * rows, rows), :].astype(BF16)
                dpw_ref[d] = acc_pw[:, pl.ds(d * rows_per_dev, rows_per_dev), :].astype(BF16)

    sq = (N_DEV, D_MODEL // N_DEV, D_MODEL)
    pw_shape = (N_DEV, n_groups, rows_per_dev, POOL_GROUP_DIM)
    est = (3 * 2 * 2 * D_MODEL * D_MODEL + 3 * 4 * D_MODEL * D_MODEL + 3 * 2 * 2 * D_MODEL * D_MODEL
           + 2 * tm * D_MODEL * (4 + 4 + 2 + 4 + 4 + 2 + 2 + 2 + 4 + 4) + 24 * tm * D_MODEL * 4)
    return _call(
        body, name="mix_tail_bwd", grid=(nt,),
        in_specs=_mix_tail_specs(tm) + [_row_spec(tm, D_MODEL), _row_spec(tm, D_MODEL), _row_spec(tm, D_MODEL, 3)],
        out_specs=[_row_spec(tm, D_MODEL), _row_spec(tm, D_MODEL), _row_spec(tm, 2 * D_MODEL), _row_spec(tm, D_MODEL),
                   _full_spec(sq), _full_spec(sq), _full_spec(sq), _full_spec(pw_shape),
                   _full_spec((2, D_MODEL)), _full_spec((1, D_MODEL))],
        out_shape=[jax.ShapeDtypeStruct((SEQ, D_MODEL), BF16), jax.ShapeDtypeStruct((SEQ, D_MODEL), BF16),
                   jax.ShapeDtypeStruct((SEQ, 2 * D_MODEL), BF16), jax.ShapeDtypeStruct((SEQ, D_MODEL), F32),
                   jax.ShapeDtypeStruct(sq, BF16), jax.ShapeDtypeStruct(sq, BF16), jax.ShapeDtypeStruct(sq, BF16),
                   jax.ShapeDtypeStruct(pw_shape, BF16),
                   jax.ShapeDtypeStruct((2, D_MODEL), F32), jax.ShapeDtypeStruct((1, D_MODEL), F32)],
        scratch_shapes=[pltpu.VMEM((n_groups, POOL_GROUP_DIM, POOL_GROUP_DIM), BF16), pltpu.SemaphoreType.DMA((N_DEV,)),
                        pltpu.VMEM((D_MODEL, D_MODEL), F32), pltpu.VMEM((D_MODEL, D_MODEL), F32),
                        pltpu.VMEM((D_MODEL, D_MODEL), F32), pltpu.VMEM((n_groups, POOL_GROUP_DIM, POOL_GROUP_DIM), F32)],
        vmem_bytes=est, args=[p, p, gates, o_ret, bias, scale, pw8, wru, wpu, wo, dh2, ret, qkvg], ride=ride)


def _mix_proj_backward(dq, dk, dv, dgr, dpooled, dgates, cos, sin, h1, gain, dh2, wmix8, ride=None):
    tm, nt = TOKEN_TILE, SEQ // TOKEN_TILE
    halo_blocks = tm // HALO
    last_halo = SEQ // HALO - 1
    k_scale = HEAD_DIM ** -0.5

    def body(dq_ref, dk_ref, dv_ref, dgr_ref, dpool_ref, dhalo_ref, dgates_ref, cos_ref, sin_ref, h1_ref, g_ref,
             dh2_ref, wmix_hbm, dh1_ref, dproj_ref, dg_ref, wmix, sem):
        i = pl.program_id(0)

        @pl.when(i == 0)
        def _():
            _load_mix_weight(wmix_hbm, wmix, sem)
            dg_ref[...] = jnp.zeros_like(dg_ref)

        cos_t, sin_t = cos_ref[...], sin_ref[...]
        for seg, ref, scale in ((0, dq_ref, 1.0), (1, dk_ref, k_scale)):
            for hd in range(HEADS):
                lo = hd * HEAD_DIM
                d1, d2 = ref[:, lo:lo + ROT_HALF], ref[:, lo + ROT_HALF:lo + HEAD_DIM]
                dproj_ref[:, pl.ds(seg * D_MODEL + lo, ROT_HALF)] = ((d1 * cos_t + d2 * sin_t) * scale).astype(BF16)
                dproj_ref[:, pl.ds(seg * D_MODEL + lo + ROT_HALF, ROT_HALF)] = ((d2 * cos_t - d1 * sin_t) * scale).astype(BF16)
        dproj_ref[:, pl.ds(2 * D_MODEL, D_MODEL)] = dv_ref[...].astype(BF16)
        dproj_ref[:, pl.ds(3 * D_MODEL, D_MODEL)] = dgr_ref[...]
        dp = _pooled_transpose(jnp.concatenate([dpool_ref[...], dhalo_ref[...]], axis=0), i * tm)
        for g in range(len(POOL_WINDOWS)):
            dproj_ref[:, pl.ds(4 * D_MODEL + g * POOL_GROUP_DIM, POOL_GROUP_DIM)] = dp[g].astype(BF16)
        dproj_ref[:, pl.ds(5 * D_MODEL, 2 * D_MODEL)] = dgates_ref[...]
        du = jnp.zeros((tm, D_MODEL), F32)
        for seg in range(N_SEG):
            cols = pl.ds(seg * D_MODEL, D_MODEL)
            du = du + _dot_nt(dproj_ref[:, cols], wmix[:, cols])
        g = g_ref[...]
        _, xhat, r = _rms(h1_ref[...], g)
        dg_ref[...] += jnp.sum(du * xhat, axis=0, keepdims=True)
        dh1_ref[...] = dh2_ref[...] + _rms_bwd(du * g, xhat, r)

    est = 2 * D_MODEL * N_SEG * D_MODEL + 2 * tm * D_MODEL * (3 * 4 + 2 + 4 + 4 + 4 + 4 + 4 + 14) + 12 * tm * D_MODEL * 4
    return _call(
        body, name="mix_proj_bwd", grid=(nt,),
        in_specs=[_row_spec(tm, D_MODEL), _row_spec(tm, D_MODEL), _row_spec(tm, D_MODEL), _row_spec(tm, D_MODEL),
                  _row_spec(tm, D_MODEL),
                  pl.BlockSpec((HALO, D_MODEL), lambda i: (jnp.minimum((i + 1) * halo_blocks, last_halo), 0)),
                  _row_spec(tm, 2 * D_MODEL), _row_spec(tm, ROT_HALF), _row_spec(tm, ROT_HALF),
                  _row_spec(tm, D_MODEL), _full_spec((1, D_MODEL)), _row_spec(tm, D_MODEL), ANY],
        out_specs=[_row_spec(tm, D_MODEL), _row_spec(tm, N_SEG * D_MODEL), _full_spec((1, D_MODEL))],
        out_shape=[jax.ShapeDtypeStruct((SEQ, D_MODEL), F32), jax.ShapeDtypeStruct((SEQ, N_SEG * D_MODEL), BF16),
                   jax.ShapeDtypeStruct((1, D_MODEL), F32)],
        scratch_shapes=[pltpu.VMEM((D_MODEL, N_SEG * D_MODEL), BF16), pltpu.SemaphoreType.DMA((N_DEV,))],
        vmem_bytes=est, args=[dq, dk, dv, dgr, dpooled, dpooled, dgates, cos, sin, h1, gain, dh2, wmix8], ride=ride)


def _adamw(w, parts, m, v, name):
    rows, cols = w.shape
    n_parts = parts.shape[0]
    tr = rows
    for cand in (256, 128, 64, 32, 16):
        if rows > cand and rows % cand == 0:
            tr = cand
            break
    c1 = 1.0 - ADAM_B1 ** ADAM_STEP
    c2 = 1.0 - ADAM_B2 ** ADAM_STEP

    def body(w_ref, p_ref, m_ref, v_ref, g_out, d_out, m_out, v_out):
        g = p_ref[0].astype(F32)
        for k in range(1, n_parts):
            g = g + p_ref[k].astype(F32)
        m_new = ADAM_B1 * m_ref[...] + (1.0 - ADAM_B1) * g
        v_new = ADAM_B2 * v_ref[...] + (1.0 - ADAM_B2) * (g * g)
        g_out[...] = g
        m_out[...] = m_new
        v_out[...] = v_new
        d_out[...] = -ADAM_LR * ((m_new / c1) / (jnp.sqrt(v_new / c2) + ADAM_EPS) + ADAM_WD * w_ref[...])

    spec = pl.BlockSpec((tr, cols), lambda i: (i, 0))
    out = jax.ShapeDtypeStruct((rows, cols), F32)
    return pl.pallas_call(
        body, name=name, grid=(rows // tr,),
        in_specs=[spec, pl.BlockSpec((n_parts, tr, cols), lambda i: (0, i, 0)), spec, spec],
        out_specs=[spec] * 4, out_shape=[out] * 4,
        compiler_params=_params(2 * tr * cols * (7 * 4 + n_parts * parts.dtype.itemsize) + 8 * tr * cols * 4, 1),
    )(w, parts, m, v)


def _mix_w_in_grad(u, dproj, ride=None):
    return _weight_grad(
        u, dproj, N_DEV,
        lambda tt: pl.BlockSpec((tt, D_MODEL), lambda b, t: (t, 0)),
        lambda tt: pl.BlockSpec((tt, MIX_SHARD), lambda b, t: (t, b)),
        D_MODEL, MIX_SHARD, name="w_in_grad", ride=ride)


def kernel(x, norm_ffn1, ffn1_w_in, ffn1_w_out, norm_mix, w_in, gate_bias, pool_w, pool_scale, w_ret_up, w_pool_up, w_out, norm_ffn2, ffn2_w_in, ffn2_w_out, norm_final, loss_target, m_norm_ffn1, m_ffn1_w_in, m_ffn1_w_out, m_norm_mix, m_w_in, m_gate_bias, m_pool_w, m_pool_scale, m_w_ret_up, m_w_pool_up, m_w_out, m_norm_ffn2, m_ffn2_w_in, m_ffn2_w_out, m_norm_final, v_norm_ffn1, v_ffn1_w_in, v_ffn1_w_out, v_norm_mix, v_w_in, v_gate_bias, v_pool_w, v_pool_scale, v_w_ret_up, v_w_pool_up, v_w_out, v_norm_ffn2, v_ffn2_w_in, v_ffn2_w_out, v_norm_final):
    assert x.shape == (1, SEQ, D_MODEL) and ffn1_w_in.shape == (1, D_MODEL, FF_SHARD) and w_in.shape == (1, D_MODEL, MIX_SHARD)
    x2, target = x[0], loss_target[0]

    cos, sin = _rotary_tables()
    tables = _retention_tables()
    big = [ffn1_w_in, ffn1_w_out, w_in, pool_w, w_ret_up, w_pool_up, w_out, ffn2_w_in, ffn2_w_out]
    bf = lambda w: w[0].astype(BF16)
    square = lambda w: w.reshape(D_MODEL, D_MODEL)

    win1, wout1, bias8 = _alone(_GatherRide([bf(ffn1_w_in), bf(ffn1_w_out), gate_bias[0]]), "ffn1_weights_all_gather")
    wout1 = wout1.reshape(D_FF, D_MODEL)
    bias = bias8.transpose(1, 0, 2).reshape(2, D_MODEL)

    (h1, gu1), (wmix8,) = _ffn_forward(x2, norm_ffn1, win1, wout1, "ffn1_fwd", ride=_GatherRide([bf(w_in)]))
    (u, qkvg, p, gates), (win2,) = _mix_proj_forward(h1, norm_mix, wmix8, cos, sin, ride=_GatherRide([bf(ffn2_w_in)]))
    (ret, o_ret), (pw8, wru, wpu, wo) = _retention_forward(
        qkvg, tables, ride=_GatherRide([bf(pool_w), bf(w_ret_up), bf(w_pool_up), bf(w_out)]))
    wru, wpu, wo = square(wru), square(wpu), square(wo)
    (h2,), (wout2,) = _mix_tail_forward(p, gates, o_ret, h1, bias, pool_scale, pw8, wru, wpu, wo,
                                        ride=_GatherRide([bf(ffn2_w_out)]))
    wout2 = wout2.reshape(D_FF, D_MODEL)
    (dh3, gu2, loss_part, d_norm_final), _ = _ffn_forward(h2, norm_ffn2, win2, wout2, "ffn2_fwd_loss",
                                                          head=(target, norm_final.reshape(1, D_MODEL)))

    (dh2, dgu2, act2, xn2, df2, d_norm_ffn2), _ = _ffn_backward(dh3, h2, norm_ffn2, gu2, win2, wout2, "ffn2_bwd")
    d_wout2, _ = _ffn_w_out_grad(act2, df2, 2)
    d_win2, (r_wout2,) = _ffn_w_in_grad(xn2, dgu2, 2, ride=_ScatterRide([d_wout2]))
    (dret, dgr, dgates, dpooled, d_wo, d_wru, d_wpu, d_pw, d_bias, d_scale), (r_win2,) = _mix_tail_backward(
        dh2, p, gates, o_ret, ret, qkvg, bias, pool_scale, pw8, wru, wpu, wo, ride=_ScatterRide([d_win2]))
    (dq,), _ = _retention_backward_q(qkvg, dret, tables)
    (dk, dv), (r_pw, r_wru, r_wpu, r_wo) = _retention_backward_kv(
        qkvg, dret, tables, ride=_ScatterRide([d_pw, d_wru, d_wpu, d_wo]))
    (dh1, dproj, d_norm_mix), _ = _mix_proj_backward(dq, dk, dv, dgr, dpooled, dgates, cos, sin, h1, norm_mix, dh2, wmix8)
    d_wmix, _ = _mix_w_in_grad(u, dproj)
    (grad_x, dgu1, act1, xn1, df1, d_norm_ffn1), (r_wmix,) = _ffn_backward(
        dh1, x2, norm_ffn1, gu1, win1, wout1, "ffn1_bwd", ride=_ScatterRide([d_wmix]))
    d_wout1, _ = _ffn_w_out_grad(act1, df1, 1)
    d_win1, (r_wout1,) = _ffn_w_in_grad(xn1, dgu1, 1, ride=_ScatterRide([d_wout1]))
    (r_win1,) = _alone(_ScatterRide([d_win1]), "ffn1_w_in_grad_reduce_scatter")
    received = [r_win1, r_wout1, r_wmix, r_pw, r_wru, r_wpu, r_wo, r_win2, r_wout2]
    zero_row = jnp.zeros((1, D_MODEL), F32)
    small = _all_reduce_rows(jnp.concatenate(
        [d_norm_ffn1, d_norm_mix, d_scale, d_norm_ffn2, d_norm_final, d_bias, zero_row], axis=0))
    loss = lax.psum(loss_part[0, 0], ("x", "y", "c"))

    names = ["ffn1_w_in", "ffn1_w_out", "w_in", "pool_w", "w_ret_up", "w_pool_up", "w_out", "ffn2_w_in", "ffn2_w_out"]
    moments_m = [m_ffn1_w_in, m_ffn1_w_out, m_w_in, m_pool_w, m_w_ret_up, m_w_pool_up, m_w_out, m_ffn2_w_in, m_ffn2_w_out]
    moments_v = [v_ffn1_w_in, v_ffn1_w_out, v_w_in, v_pool_w, v_w_ret_up, v_w_pool_up, v_w_out, v_ffn2_w_in, v_ffn2_w_out]
    results = {}
    for nm, w, parts, m, v in zip(names, big, received, moments_m, moments_v):
        shape2 = (-1, w.shape[-1])
        outs = _adamw(w.reshape(shape2), parts.reshape((N_DEV,) + w.reshape(shape2).shape), m.reshape(shape2),
                      v.reshape(shape2), name=f"adamw_{nm}")
        results[nm] = [o.reshape(w.shape) for o in outs]

    my_id = _linear_id(*_my_position())
    bias_cols = gate_bias.shape[-1]
    pad = lambda a: jnp.pad(a[0], ((0, 0), (0, D_MODEL - bias_cols)))
    pack = lambda a, b, c, d, e, gb: jnp.concatenate([a, b, c, d, e.reshape(1, D_MODEL), pad(gb), zero_row], axis=0)
    d_bias_mine = lax.dynamic_slice_in_dim(small[5:7], my_id * bias_cols, bias_cols, axis=1)
    g_small = jnp.concatenate([small[0:5], jnp.pad(d_bias_mine, ((0, 0), (0, D_MODEL - bias_cols))), zero_row], axis=0)
    s_outs = _adamw(pack(norm_ffn1, norm_mix, pool_scale, norm_ffn2, norm_final, gate_bias), g_small[None],
                    pack(m_norm_ffn1, m_norm_mix, m_pool_scale, m_norm_ffn2, m_norm_final, m_gate_bias),
                    pack(v_norm_ffn1, v_norm_mix, v_pool_scale, v_norm_ffn2, v_norm_final, v_gate_bias), name="adamw_small")
    for row, nm in enumerate(["norm_ffn1", "norm_mix", "pool_scale", "norm_ffn2"]):
        results[nm] = [o[row:row + 1] for o in s_outs]
    results["norm_final"] = [o[4] for o in s_outs]
    results["gate_bias"] = [o[5:7, :bias_cols][None] for o in s_outs]

    order = ["norm_ffn1", "ffn1_w_in", "ffn1_w_out", "norm_mix", "w_in", "gate_bias", "pool_w", "pool_scale",
             "w_ret_up", "w_pool_up", "w_out", "norm_ffn2", "ffn2_w_in", "ffn2_w_out", "norm_final"]
    return (loss, grad_x[None], *[results[nm][0] for nm in order], *[results[nm][1] for nm in order],
            *[results[nm][2] for nm in order], *[results[nm][3] for nm in order])
```

```python
import functools

import numpy as np
import jax
import jax.numpy as jnp
from jax import lax
from jax.experimental import pallas as pl
from jax.experimental.pallas import tpu as pltpu

F32 = jnp.float32
BF16 = jnp.bfloat16

N_DEV = 8
D_MODEL = 1024
SEQ = 4096
D_FF = 2816
FF_SHARD = 2 * D_FF // N_DEV
N_FF_GROUPS = N_DEV // 2
HEADS = 4
HEAD_DIM = 256
ROT_HALF = HEAD_DIM // 2
CHUNK = 64
RET_BLOCK = 256
POOL_WINDOWS = (2, 4, 8, 16)
POOL_GROUP_DIM = 256
HALO = 16
MIX_SHARD = 7 * D_MODEL // N_DEV
N_SEG = 7
ROPE_BASE = 10000.0
NORM_EPS = 1e-6
FFN_RES_WEIGHT = 0.5
ADAM_LR, ADAM_B1, ADAM_B2, ADAM_EPS, ADAM_WD, ADAM_STEP = 0.001, 0.9, 0.999, 1e-08, 0.01, 10

TOKEN_TILE = 256
WIDE_TILE = 512
VMEM_CAP_V7X = 64 * 1024 * 1024
MESH = pl.DeviceIdType.MESH
ANY = pl.BlockSpec(memory_space=pl.ANY)


def _vmem_limit(estimate_bytes):
    return int(min(estimate_bytes * 5 // 4 + (6 << 20), VMEM_CAP_V7X - (4 << 20)))


def _params(estimate_bytes, n_grid):
    return pltpu.CompilerParams(dimension_semantics=("arbitrary",) * n_grid,
                                vmem_limit_bytes=_vmem_limit(estimate_bytes))


def _dot(a, b):
    return jnp.dot(a, b, preferred_element_type=F32)


def _dot_nt(a, b):
    return lax.dot_general(a, b, (((1,), (1,)), ((), ())), preferred_element_type=F32)


def _dot_tn(a, b):
    return lax.dot_general(a, b, (((0,), (0,)), ((), ())), preferred_element_type=F32)


def _sig(x):
    return 1.0 / (1.0 + jnp.exp(-x))


def _rms(x, g):
    r = lax.rsqrt(jnp.mean(x * x, axis=-1, keepdims=True) + NORM_EPS)
    xhat = x * r
    return xhat * g, xhat, r


def _rms_bwd(dyg, xhat, r):
    return r * (dyg - xhat * jnp.mean(dyg * xhat, axis=-1, keepdims=True))


def _row_spec(tile, width, col=0):
    return pl.BlockSpec((tile, width), lambda i, c=col: (i, c))


def _full_spec(shape):
    return pl.BlockSpec(shape, lambda *_: (0,) * len(shape))


def _rotary_tables():
    inv_freq = (np.float32(ROPE_BASE) ** (-np.arange(ROT_HALF, dtype=np.float32) / np.float32(ROT_HALF))).astype(np.float32)
    ang = (np.arange(SEQ, dtype=np.float32)[:, None] * inv_freq[None, :]).astype(np.float32)
    return jnp.asarray(np.cos(ang.astype(np.float64)), F32), jnp.asarray(np.sin(ang.astype(np.float64)), F32)


def _retention_tables():
    log_gamma = np.log(1.0 - 2.0 ** (-5.0 - np.arange(HEADS, dtype=np.float64)))
    n = np.arange(RET_BLOCK)
    diff = (n[:, None] - n[None, :]).astype(np.float64)
    same = (n[:, None] // CHUNK) == (n[None, :] // CHUNK)
    earlier = (n[None, :] // CHUNK) < (n[:, None] // CHUNK)
    expo = np.where(same, np.abs(diff), diff)
    mask = np.where(same | earlier, np.exp(log_gamma[:, None, None] * expo[None]), 0.0)
    qdec = np.exp(log_gamma[:, None] * (n[None, :] + 1.0))[:, :, None]
    kdec = np.exp(log_gamma[:, None] * (RET_BLOCK - 1.0 - n[None, :]))[:, :, None]
    cdec = np.exp(log_gamma * RET_BLOCK)[:, None, None]
    return (jnp.asarray(mask, F32), jnp.asarray(qdec, F32), jnp.asarray(kdec, F32), jnp.asarray(cdec, F32))


def _my_position():
    return lax.axis_index("x"), lax.axis_index("y"), lax.axis_index("c")


def _linear_id(px, py, pc):
    return 4 * px + 2 * py + pc


def _when(pred, fn):
    if isinstance(pred, bool):
        if pred:
            fn()
    else:
        pl.when(pred)(fn)


class _GatherRide:
    def __init__(self, shards):
        self.args = list(shards)
        n = self.n = len(shards)
        self.out_shape = [pltpu.HBM((N_DEV,) + s.shape, s.dtype) for s in shards]
        self.scratch = [pltpu.SemaphoreType.DMA((n, 7)), pltpu.SemaphoreType.DMA((n, 7)), pltpu.SemaphoreType.DMA((n,))]

    def _plan(self, src, out, sems):
        send_sems, recv_sems, local_sem = sems
        x, y, c = _my_position()
        me, sibling = (x, y, c), (x, y, 1 - c)
        chips = [(1 - x, y), (x, 1 - y), (1 - x, 1 - y)]

        def copy(t, k, block, to, from_src=False):
            rows = out[t].at[_linear_id(*block)]
            return pltpu.make_async_remote_copy(
                src_ref=src[t] if from_src else rows, dst_ref=rows,
                send_sem=send_sems.at[t, k], recv_sem=recv_sems.at[t, k],
                device_id=to, device_id_type=MESH)

        local = [pltpu.make_async_copy(src[t], out[t].at[_linear_id(*me)], local_sem.at[t]) for t in range(self.n)]
        return copy, local, me, sibling, chips, c

    def begin(self, first, src, out, sems):
        copy, local, me, sibling, chips, c = self._plan(src, out, sems)

        def start():
            for cp in local:
                cp.start()
            for t in range(self.n):
                copy(t, 0, me, sibling, from_src=True).start()
                for j, chip in enumerate(chips):
                    copy(t, 1 + j, me, (*chip, c), from_src=True).start()

        _when(first, start)

    def finish(self, mid, last, src, out, sems):
        copy, local, me, sibling, chips, c = self._plan(src, out, sems)

        def pass_on():
            for j, chip in enumerate(chips):
                for t in range(self.n):
                    copy(t, 1 + j, (*chip, c), me).wait_recv()
                    copy(t, 4 + j, (*chip, c), sibling).start()

        def drain():
            for t in range(self.n):
                copy(t, 0, sibling, me).wait_recv()
                for j, chip in enumerate(chips):
                    copy(t, 4 + j, (*chip, 1 - c), me).wait_recv()
            for t in range(self.n):
                copy(t, 0, me, sibling, from_src=True).wait_send()
                for j, chip in enumerate(chips):
                    copy(t, 1 + j, me, (*chip, c), from_src=True).wait_send()
                    copy(t, 4 + j, (*chip, c), sibling).wait_send()
            for cp in local:
                cp.wait()

        _when(mid, pass_on)
        _when(last, drain)


class _ScatterRide:
    def __init__(self, partials):
        self.args = list(partials)
        n = self.n = len(partials)
        self.out_shape = [pltpu.HBM(p.shape, p.dtype) for p in partials]
        self.scratch = [pltpu.SemaphoreType.DMA((n, 7)), pltpu.SemaphoreType.DMA((n, 7)), pltpu.SemaphoreType.DMA((n,))]

    def _plan(self, src, out, sems):
        send_sems, recv_sems, local_sem = sems
        x, y, c = _my_position()

        def peer(k):
            return (x ^ (k >> 2), y ^ ((k >> 1) & 1), c ^ (k & 1))

        copies = [pltpu.make_async_remote_copy(
            src_ref=src[t].at[_linear_id(*peer(k))], dst_ref=out[t].at[k],
            send_sem=send_sems.at[t, k - 1], recv_sem=recv_sems.at[t, k - 1],
            device_id=peer(k), device_id_type=MESH) for t in range(self.n) for k in range(1, N_DEV)]
        local = [pltpu.make_async_copy(src[t].at[_linear_id(x, y, c)], out[t].at[0], local_sem.at[t])
                 for t in range(self.n)]
        return copies, local

    def begin(self, first, src, out, sems):
        copies, local = self._plan(src, out, sems)

        def start():
            for cp in local + copies:
                cp.start()

        _when(first, start)

    def finish(self, mid, last, src, out, sems):
        copies, local = self._plan(src, out, sems)

        def drain():
            for cp in copies:
                cp.wait_recv()
            for cp in copies:
                cp.wait_send()
            for cp in local:
                cp.wait()

        _when(last, drain)


def _in_hbm(a):
    return pltpu.with_memory_space_constraint(a, pltpu.HBM)


def _call(body, *, name, grid, in_specs, out_specs, out_shape, scratch_shapes, vmem_bytes, args, ride=None):
    n_in, n_out, n_s = len(in_specs), len(out_specs), len(scratch_shapes)
    params = _params(vmem_bytes, len(grid))
    args = [_in_hbm(a) for a in args]
    out_shape = [pltpu.HBM(s.shape, s.dtype) for s in out_shape]
    if ride is None:
        outs = pl.pallas_call(body, name=name, grid=grid, in_specs=in_specs, out_specs=out_specs, out_shape=out_shape,
                              scratch_shapes=scratch_shapes, compiler_params=params)(*args)
        return list(outs), []
    total = int(np.prod(grid))

    def riding_body(*refs):
        a = n_in
        b = a + ride.n
        c = b + n_out
        d = c + ride.n
        e = d + n_s
        step = pl.program_id(0)
        for axis in range(1, len(grid)):
            step = step * grid[axis] + pl.program_id(axis)
        ride.begin(step == 0, refs[a:b], refs[c:d], refs[e:])
        body(*refs[:a], *refs[b:c], *refs[d:e])
        ride.finish(step == (3 * total) // 4, step == total - 1, refs[a:b], refs[c:d], refs[e:])

    outs = pl.pallas_call(
        riding_body, name=name, grid=grid, in_specs=list(in_specs) + [ANY] * ride.n,
        out_specs=list(out_specs) + [ANY] * ride.n, out_shape=list(out_shape) + ride.out_shape,
        scratch_shapes=list(scratch_shapes) + ride.scratch, compiler_params=params)(*args, *[_in_hbm(a) for a in ride.args])
    return list(outs[:n_out]), list(outs[n_out:])


def _alone(ride, name):
    def body(*refs):
        src, out, sems = refs[:ride.n], refs[ride.n:2 * ride.n], refs[2 * ride.n:]
        ride.begin(True, src, out, sems)
        ride.finish(True, True, src, out, sems)

    return list(pl.pallas_call(body, name=name, out_shape=ride.out_shape, in_specs=[ANY] * ride.n,
                               out_specs=[ANY] * ride.n, scratch_shapes=ride.scratch)(*[_in_hbm(a) for a in ride.args]))


def _all_reduce_rows(block):
    rows, width = block.shape

    def body(x_ref, sum_ref, gathered, send_sems, recv_sems, local_sem):
        x, y, c = _my_position()
        me, sibling = (x, y, c), (x, y, 1 - c)
        chips = [(1 - x, y), (x, 1 - y), (1 - x, 1 - y)]

        def slot(px, py, pc):
            return gathered.at[_linear_id(px, py, pc)]

        def copy(k, block_of, to, from_src=False):
            return pltpu.make_async_remote_copy(
                src_ref=x_ref if from_src else slot(*block_of), dst_ref=slot(*block_of),
                send_sem=send_sems.at[k], recv_sem=recv_sems.at[k], device_id=to, device_id_type=MESH)

        mine = pltpu.make_async_copy(x_ref, slot(*me), local_sem)
        mine.start()
        first = [copy(0, me, sibling, from_src=True)]
        first += [copy(1 + j, me, (*chip, c), from_src=True) for j, chip in enumerate(chips)]
        for cp in first:
            cp.start()
        passed = [copy(4 + j, (*chip, c), sibling) for j, chip in enumerate(chips)]
        for j, chip in enumerate(chips):
            copy(1 + j, (*chip, c), me).wait_recv()
            passed[j].start()
        copy(0, sibling, me).wait_recv()
        for j, chip in enumerate(chips):
            copy(4 + j, (*chip, 1 - c), me).wait_recv()
        for cp in first + passed:
            cp.wait_send()
        mine.wait()
        total = gathered[0]
        for d in range(1, N_DEV):
            total = total + gathered[d]
        sum_ref[...] = total

    return pl.pallas_call(
        body, name="small_grads_all_reduce",
        out_shape=jax.ShapeDtypeStruct((rows, width), F32),
        in_specs=[pl.BlockSpec(memory_space=pltpu.VMEM)],
        out_specs=pl.BlockSpec(memory_space=pltpu.VMEM),
        scratch_shapes=[pltpu.VMEM((N_DEV, rows, width), F32),
                        pltpu.SemaphoreType.DMA((7,)), pltpu.SemaphoreType.DMA((7,)), pltpu.SemaphoreType.DMA],
    )(block)


def _load_ffn_weights(win_hbm, wout_hbm, win, wout, sem):
    a = pltpu.make_async_copy(win_hbm, win, sem.at[0])
    b = pltpu.make_async_copy(wout_hbm, wout, sem.at[1])
    a.start()
    b.start()
    a.wait()
    b.wait()


def _ffn_forward(h_in, gain, win8, wout, name, head=None, ride=None):
    tm, nt = WIDE_TILE, SEQ // WIDE_TILE

    def body(*refs):
        if head is None:
            x_ref, g_ref, win_hbm, wout_hbm, out_ref, gu_ref, win, wout, sem = refs
        else:
            x_ref, g_ref, win_hbm, wout_hbm, tgt_ref, gf_ref, out_ref, gu_ref, loss_ref, dgf_ref, win, wout, sem = refs
        i = pl.program_id(0)

        @pl.when(i == 0)
        def _():
            _load_ffn_weights(win_hbm, wout_hbm, win, wout, sem)
            if head is not None:
                loss_ref[...] = jnp.zeros_like(loss_ref)
                dgf_ref[...] = jnp.zeros_like(dgf_ref)

        x = x_ref[...]
        xn, _, _ = _rms(x, g_ref[...])
        xb = xn.astype(BF16)
        acc = jnp.zeros((tm, D_MODEL), F32)
        for j in range(N_FF_GROUPS):
            gate = _dot_nt(xb, win[j])
            up = _dot_nt(xb, win[j + N_FF_GROUPS])
            gu_ref[j] = gate.astype(BF16)
            gu_ref[j + N_FF_GROUPS] = up.astype(BF16)
            act = gate * _sig(gate) * up
            acc = acc + _dot(act.astype(BF16), wout[j])
        h = x + FFN_RES_WEIGHT * acc
        if head is None:
            out_ref[...] = h
        else:
            gf = gf_ref[...]
            y, hhat, r = _rms(h, gf)
            err = y - tgt_ref[...]
            loss_ref[...] += jnp.full(loss_ref.shape, 0.5 / D_MODEL * jnp.sum(err * err), F32)
            dy = err * (1.0 / D_MODEL)
            dgf_ref[...] += jnp.sum(dy * hhat, axis=0, keepdims=True)
            out_ref[...] = _rms_bwd(dy * gf, hhat, r)

    weights = 2 * D_MODEL * 2 * D_FF + 2 * D_FF * D_MODEL
    tiles = 2 * (2 * 4 * tm * D_MODEL + 2 * tm * 2 * D_FF) + (2 * 4 * tm * D_MODEL if head else 0)
    in_specs = [_row_spec(tm, D_MODEL), _full_spec((1, D_MODEL)), ANY, ANY]
    out_shape = [jax.ShapeDtypeStruct((SEQ, D_MODEL), F32), jax.ShapeDtypeStruct((N_DEV, SEQ, FF_SHARD), BF16)]
    out_specs = [_row_spec(tm, D_MODEL), pl.BlockSpec((N_DEV, tm, FF_SHARD), lambda i: (0, i, 0))]
    args = [h_in, gain, win8, wout]
    if head is not None:
        in_specs += [_row_spec(tm, D_MODEL), _full_spec((1, D_MODEL))]
        out_shape += [jax.ShapeDtypeStruct((1, 128), F32), jax.ShapeDtypeStruct((1, D_MODEL), F32)]
        out_specs += [_full_spec((1, 128)), _full_spec((1, D_MODEL))]
        args += list(head)
    return _call(
        body, name=name, grid=(nt,), in_specs=in_specs, out_specs=out_specs, out_shape=out_shape,
        scratch_shapes=[pltpu.VMEM((N_DEV, FF_SHARD, D_MODEL), BF16), pltpu.VMEM((N_FF_GROUPS, FF_SHARD, D_MODEL), BF16),
                        pltpu.SemaphoreType.DMA((2,))],
        vmem_bytes=weights + tiles + 16 * tm * FF_SHARD * 4, args=args, ride=ride)


def _ffn_backward(dh_out, h_in, gain, gu, win8, wout, name, ride=None):
    tm, nt = WIDE_TILE, SEQ // WIDE_TILE
    last = N_FF_GROUPS - 1

    def body(dh_ref, x_ref, g_ref, gate_ref, up_ref, win_hbm, wout_hbm,
             dhin_ref, dgu_ref, act_ref, xn_ref, df_ref, dg_ref, win, wout, sem, dxn):
        i, j = pl.program_id(0), pl.program_id(1)

        @pl.when((i == 0) & (j == 0))
        def _():
            _load_ffn_weights(win_hbm, wout_hbm, win, wout, sem)
            dg_ref[...] = jnp.zeros_like(dg_ref)

        @pl.when(j == 0)
        def _():
            dxn[...] = jnp.zeros_like(dxn)
            df_ref[...] = (FFN_RES_WEIGHT * dh_ref[...]).astype(BF16)

        gate = gate_ref[...].astype(F32)
        up = up_ref[...].astype(F32)
        dact = _dot_nt(df_ref[...], wout[j])
        s = _sig(gate)
        silu = gate * s
        dgate = (dact * up * (s * (1.0 + gate * (1.0 - s)))).astype(BF16)
        dup = (dact * silu).astype(BF16)
        act_ref[...] = (silu * up).astype(BF16)
        dgu_ref[0] = dgate
        dgu_ref[1] = dup
        dxn[...] += _dot(dgate, win[j]) + _dot(dup, win[j + N_FF_GROUPS])

        @pl.when(j == last)
        def _():
            g = g_ref[...]
            xn, xhat, r = _rms(x_ref[...], g)
            d = dxn[...]
            dg_ref[...] += jnp.sum(d * xhat, axis=0, keepdims=True)
            dhin_ref[...] = dh_ref[...] + _rms_bwd(d * g, xhat, r)
            xn_ref[...] = xn.astype(BF16)

    weights = 2 * D_MODEL * 2 * D_FF + 2 * D_FF * D_MODEL
    tiles = 2 * (3 * 4 * tm * D_MODEL + 2 * 2 * tm * D_MODEL + 5 * 2 * tm * FF_SHARD) + 4 * tm * D_MODEL
    row = lambda: pl.BlockSpec((tm, D_MODEL), lambda i, j: (i, 0))
    vec = lambda: pl.BlockSpec((1, D_MODEL), lambda i, j: (0, 0))
    return _call(
        body, name=name, grid=(nt, N_FF_GROUPS),
        in_specs=[row(), row(), vec(),
                  pl.BlockSpec((None, tm, FF_SHARD), lambda i, j: (j, i, 0)),
                  pl.BlockSpec((None, tm, FF_SHARD), lambda i, j: (j + N_FF_GROUPS, i, 0)), ANY, ANY],
        out_specs=[row(), pl.BlockSpec((None, 2, tm, FF_SHARD), lambda i, j: (j, 0, i, 0)),
                   pl.BlockSpec((None, tm, FF_SHARD), lambda i, j: (j, i, 0)), row(), row(), vec()],
        out_shape=[jax.ShapeDtypeStruct((SEQ, D_MODEL), F32), jax.ShapeDtypeStruct((N_FF_GROUPS, 2, SEQ, FF_SHARD), BF16),
                   jax.ShapeDtypeStruct((N_FF_GROUPS, SEQ, FF_SHARD), BF16), jax.ShapeDtypeStruct((SEQ, D_MODEL), BF16),
                   jax.ShapeDtypeStruct((SEQ, D_MODEL), BF16), jax.ShapeDtypeStruct((1, D_MODEL), F32)],
        scratch_shapes=[pltpu.VMEM((N_DEV, FF_SHARD, D_MODEL), BF16), pltpu.VMEM((N_FF_GROUPS, FF_SHARD, D_MODEL), BF16),
                        pltpu.SemaphoreType.DMA((2,)), pltpu.VMEM((tm, D_MODEL), F32)],
        vmem_bytes=weights + tiles + 12 * tm * FF_SHARD * 4, args=[dh_out, h_in, gain, gu, gu, win8, wout], ride=ride)


def _weight_grad(x, g, n_out, x_spec, g_spec, k_dim, n_dim, name, tt=1024, ride=None):
    nt = SEQ // tt

    def body(x_ref, g_ref, out_ref, acc):
        t = pl.program_id(1)

        @pl.when(t == 0)
        def _():
            acc[...] = jnp.zeros_like(acc)

        acc[...] += _dot_tn(x_ref[...], g_ref[...])

        @pl.when(t == nt - 1)
        def _():
            out_ref[...] = acc[...].astype(BF16)

    outs, ride_outs = _call(
        body, name=name, grid=(n_out, nt), in_specs=[x_spec(tt), g_spec(tt)],
        out_specs=[pl.BlockSpec((None, k_dim, n_dim), lambda b, t: (b, 0, 0))],
        out_shape=[jax.ShapeDtypeStruct((n_out, k_dim, n_dim), BF16)],
        scratch_shapes=[pltpu.VMEM((k_dim, n_dim), F32)],
        vmem_bytes=2 * 2 * tt * (k_dim + n_dim) + 8 * k_dim * n_dim + 4 * k_dim * n_dim, args=[x, g], ride=ride)
    return outs[0], ride_outs


def _ffn_w_out_grad(act, df, tag, ride=None):
    d_wout, ride_outs = _weight_grad(
        act, df, N_FF_GROUPS,
        lambda tt: pl.BlockSpec((None, tt, FF_SHARD), lambda b, t: (b, t, 0)),
        lambda tt: pl.BlockSpec((tt, D_MODEL), lambda b, t: (t, 0)),
        FF_SHARD, D_MODEL, name=f"ffn{tag}_w_out_grad", ride=ride)
    return d_wout.reshape(N_DEV, D_FF // N_DEV, D_MODEL), ride_outs


def _ffn_w_in_grad(xn, dgu, tag, ride=None):
    return _weight_grad(
        dgu, xn, N_DEV,
        lambda tt: pl.BlockSpec((None, None, tt, FF_SHARD), lambda b, t: (b % N_FF_GROUPS, b // N_FF_GROUPS, t, 0)),
        lambda tt: pl.BlockSpec((tt, D_MODEL), lambda b, t: (t, 0)),
        FF_SHARD, D_MODEL, name=f"ffn{tag}_w_in_grad", ride=ride)


def _load_mix_weight(wmix_hbm, wmix, sem):
    copies = [pltpu.make_async_copy(wmix_hbm.at[d], wmix.at[:, pl.ds(d * MIX_SHARD, MIX_SHARD)], sem.at[d])
              for d in range(N_DEV)]
    for cp in copies:
        cp.start()
    for cp in copies:
        cp.wait()


def _load_pool_weight(pw_hbm, pw, sem):
    rows = POOL_GROUP_DIM // N_DEV
    copies = [pltpu.make_async_copy(pw_hbm.at[d], pw.at[:, pl.ds(d * rows, rows), :], sem.at[d]) for d in range(N_DEV)]
    for cp in copies:
        cp.start()
    for cp in copies:
        cp.wait()


def _rotate(x1, x2, cos, sin):
    return x1 * cos - x2 * sin, x1 * sin + x2 * cos


def _mix_proj_forward(h1, gain, wmix8, cos, sin, ride=None):
    tm, nt = WIDE_TILE, SEQ // WIDE_TILE
    k_scale = HEAD_DIM ** -0.5

    def body(h_ref, g_ref, wmix_hbm, cos_ref, sin_ref, u_ref, qkvg_ref, p_ref, gates_ref, wmix, sem):
        @pl.when(pl.program_id(0) == 0)
        def _():
            _load_mix_weight(wmix_hbm, wmix, sem)

        u = _rms(h_ref[...], g_ref[...])[0].astype(BF16)
        u_ref[...] = u
        cos_t, sin_t = cos_ref[...], sin_ref[...]
        for seg in range(N_SEG):
            pr = _dot(u, wmix[:, pl.ds(seg * D_MODEL, D_MODEL)])
            if seg < 2:
                scale = 1.0 if seg == 0 else k_scale
                for hd in range(HEADS):
                    lo = hd * HEAD_DIM
                    o1, o2 = _rotate(pr[:, lo:lo + ROT_HALF], pr[:, lo + ROT_HALF:lo + HEAD_DIM], cos_t, sin_t)
                    qkvg_ref[:, pl.ds(seg * D_MODEL + lo, ROT_HALF)] = (o1 * scale).astype(BF16)
                    qkvg_ref[:, pl.ds(seg * D_MODEL + lo + ROT_HALF, ROT_HALF)] = (o2 * scale).astype(BF16)
            elif seg < 4:
                qkvg_ref[:, pl.ds(seg * D_MODEL, D_MODEL)] = pr.astype(BF16)
            elif seg == 4:
                p_ref[...] = pr
            else:
                gates_ref[:, pl.ds((seg - 5) * D_MODEL, D_MODEL)] = pr.astype(BF16)

    est = 2 * D_MODEL * N_SEG * D_MODEL + 2 * tm * (4 * D_MODEL + 2 * D_MODEL + 2 * 4 * D_MODEL + 4 * D_MODEL + 2 * 2 * D_MODEL)
    return _call(
        body, name="mix_proj_fwd", grid=(nt,),
        in_specs=[_row_spec(tm, D_MODEL), _full_spec((1, D_MODEL)), ANY, _row_spec(tm, ROT_HALF), _row_spec(tm, ROT_HALF)],
        out_specs=[_row_spec(tm, D_MODEL), _row_spec(tm, 4 * D_MODEL), _row_spec(tm, D_MODEL), _row_spec(tm, 2 * D_MODEL)],
        out_shape=[jax.ShapeDtypeStruct((SEQ, D_MODEL), BF16), jax.ShapeDtypeStruct((SEQ, 4 * D_MODEL), BF16),
                   jax.ShapeDtypeStruct((SEQ, D_MODEL), F32), jax.ShapeDtypeStruct((SEQ, 2 * D_MODEL), BF16)],
        scratch_shapes=[pltpu.VMEM((D_MODEL, N_SEG * D_MODEL), BF16), pltpu.SemaphoreType.DMA((N_DEV,))],
        vmem_bytes=est + 8 * tm * D_MODEL * 4, args=[h1, gain, wmix8, cos, sin], ride=ride)


def _head_block_spec(seg, reverse=False):
    nb = SEQ // RET_BLOCK
    if reverse:
        return pl.BlockSpec((RET_BLOCK, HEAD_DIM), lambda h, i, s=seg: (nb - 1 - i, s * HEADS + h))
    return pl.BlockSpec((RET_BLOCK, HEAD_DIM), lambda h, i, s=seg: (i, s * HEADS + h))


def _table_specs():
    return [pl.BlockSpec((None, RET_BLOCK, RET_BLOCK), lambda h, i: (h, 0, 0)),
            pl.BlockSpec((None, RET_BLOCK, 1), lambda h, i: (h, 0, 0)),
            pl.BlockSpec((None, RET_BLOCK, 1), lambda h, i: (h, 0, 0)),
            pl.BlockSpec((None, 1, 1), lambda h, i: (h, 0, 0))]


def _retention_forward(qkvg, tables, ride=None):
    nb = SEQ // RET_BLOCK

    def body(q_ref, k_ref, v_ref, gr_ref, mask_ref, qdec_ref, kdec_ref, cdec_ref, ret_ref, o_ref, state):
        @pl.when(pl.program_id(1) == 0)
        def _():
            state[...] = jnp.zeros_like(state)

        q, k, v = q_ref[...], k_ref[...], v_ref[...]
        scores = _dot_nt(q, k) * mask_ref[...]
        inner = _dot(scores.astype(BF16), v)
        cross = _dot((q.astype(F32) * qdec_ref[...]).astype(BF16), state[...].astype(BF16))
        ret = inner + cross
        state[...] = state[...] * cdec_ref[...] + _dot_tn((k.astype(F32) * kdec_ref[...]).astype(BF16), v)
        ret_ref[...] = ret
        retn = ret * lax.rsqrt(jnp.mean(ret * ret, axis=-1, keepdims=True) + NORM_EPS)
        gr = gr_ref[...].astype(F32)
        o_ref[...] = (retn * (gr * _sig(gr))).astype(BF16)

    return _call(
        body, name="retention_fwd", grid=(HEADS, nb),
        in_specs=[_head_block_spec(0), _head_block_spec(1), _head_block_spec(2), _head_block_spec(3)] + _table_specs(),
        out_specs=[pl.BlockSpec((RET_BLOCK, HEAD_DIM), lambda h, i: (i, h))] * 2,
        out_shape=[jax.ShapeDtypeStruct((SEQ, D_MODEL), F32), jax.ShapeDtypeStruct((SEQ, D_MODEL), BF16)],
        scratch_shapes=[pltpu.VMEM((HEAD_DIM, HEAD_DIM), F32)],
        vmem_bytes=16 * RET_BLOCK * HEAD_DIM * 4, args=[qkvg, qkvg, qkvg, qkvg, *tables], ride=ride)


def _retention_backward_q(qkvg, dret, tables, ride=None):
    nb = SEQ // RET_BLOCK

    def body(k_ref, v_ref, do_ref, mask_ref, qdec_ref, kdec_ref, cdec_ref, dq_ref, state):
        @pl.when(pl.program_id(1) == 0)
        def _():
            state[...] = jnp.zeros_like(state)

        k, v, do = k_ref[...], v_ref[...], do_ref[...]
        dscores = _dot_nt(do, v) * mask_ref[...]
        dq_ref[...] = _dot(dscores.astype(BF16), k) + _dot_nt(do, state[...].astype(BF16)) * qdec_ref[...]
        state[...] = state[...] * cdec_ref[...] + _dot_tn((k.astype(F32) * kdec_ref[...]).astype(BF16), v)

    return _call(
        body, name="retention_bwd_q", grid=(HEADS, nb),
        in_specs=[_head_block_spec(1), _head_block_spec(2), pl.BlockSpec((RET_BLOCK, HEAD_DIM), lambda h, i: (i, h))] + _table_specs(),
        out_specs=[pl.BlockSpec((RET_BLOCK, HEAD_DIM), lambda h, i: (i, h))],
        out_shape=[jax.ShapeDtypeStruct((SEQ, D_MODEL), F32)],
        scratch_shapes=[pltpu.VMEM((HEAD_DIM, HEAD_DIM), F32)],
        vmem_bytes=16 * RET_BLOCK * HEAD_DIM * 4, args=[qkvg, qkvg, dret, *tables], ride=ride)


def _retention_backward_kv(qkvg, dret, tables, ride=None):
    nb = SEQ // RET_BLOCK

    def body(q_ref, k_ref, v_ref, do_ref, mask_ref, qdec_ref, kdec_ref, cdec_ref, dk_ref, dv_ref, gstate):
        @pl.when(pl.program_id(1) == 0)
        def _():
            gstate[...] = jnp.zeros_like(gstate)

        q, k, v, do = q_ref[...], k_ref[...], v_ref[...], do_ref[...]
        mask = mask_ref[...]
        scores = (_dot_nt(q, k) * mask).astype(BF16)
        dscores = (_dot_nt(do, v) * mask).astype(BF16)
        gs = gstate[...].astype(BF16)
        dk_ref[...] = _dot_tn(dscores, q) + _dot_nt(v, gs) * kdec_ref[...]
        dv_ref[...] = _dot_tn(scores, do) + _dot((k.astype(F32) * kdec_ref[...]).astype(BF16), gs)
        gstate[...] = gstate[...] * cdec_ref[...] + _dot_tn((q.astype(F32) * qdec_ref[...]).astype(BF16), do)

    rev = lambda h, i: (nb - 1 - i, h)
    return _call(
        body, name="retention_bwd_kv", grid=(HEADS, nb),
        in_specs=[_head_block_spec(0, True), _head_block_spec(1, True), _head_block_spec(2, True),
                  pl.BlockSpec((RET_BLOCK, HEAD_DIM), rev)] + _table_specs(),
        out_specs=[pl.BlockSpec((RET_BLOCK, HEAD_DIM), rev)] * 2,
        out_shape=[jax.ShapeDtypeStruct((SEQ, D_MODEL), F32)] * 2,
        scratch_shapes=[pltpu.VMEM((HEAD_DIM, HEAD_DIM), F32)],
        vmem_bytes=20 * RET_BLOCK * HEAD_DIM * 4, args=[qkvg, qkvg, qkvg, dret, *tables], ride=ride)


def _pooled(p_ext, first_row):
    rows = p_ext.shape[0]
    t = first_row + lax.broadcasted_iota(jnp.int32, (rows - HALO, 1), 0)
    outs = []
    for g, w in enumerate(POOL_WINDOWS):
        e = p_ext[:, g * POOL_GROUP_DIM:(g + 1) * POOL_GROUP_DIM]
        s, span = e, 1
        while span < w:
            s = s + pltpu.roll(s, span, 0)
            span *= 2
        count = jnp.minimum(t + 1, w).astype(F32)
        outs.append(s[HALO:] / count - e[HALO:])
    return outs


def _pooled_transpose(d_ext, first_row):
    rows = d_ext.shape[0]
    t = first_row + lax.broadcasted_iota(jnp.int32, (rows, 1), 0)
    outs = []
    for g, w in enumerate(POOL_WINDOWS):
        d = d_ext[:, g * POOL_GROUP_DIM:(g + 1) * POOL_GROUP_DIM]
        e = jnp.where(t < SEQ, d / jnp.minimum(t + 1, w).astype(F32), 0.0)
        s, span = e, 1
        while span < w:
            s = s + pltpu.roll(s, rows - span, 0)
            span *= 2
        outs.append(s[:rows - HALO] - d[:rows - HALO])
    return outs


def _mix_tail_specs(tm):
    halo_blocks = tm // HALO
    return [
        _row_spec(tm, D_MODEL),
        pl.BlockSpec((HALO, D_MODEL), lambda i: (jnp.maximum(i * halo_blocks - 1, 0), 0)),
        _row_spec(tm, 2 * D_MODEL),
        _row_spec(tm, D_MODEL),
        _full_spec((2, D_MODEL)), _full_spec((1, D_MODEL)), ANY,
        _full_spec((D_MODEL, D_MODEL)), _full_spec((D_MODEL, D_MODEL)), _full_spec((D_MODEL, D_MODEL)),
    ]


def _mix_tail_compute(i, tm, p_ref, halo_ref, gates_ref, oret_ref, bias_ref, scale_ref, pw, wru_ref, wpu_ref):
    halo = jnp.where(i > 0, halo_ref[...], 0.0)
    pooled = _pooled(jnp.concatenate([halo, p_ref[...]], axis=0), i * tm)
    pooled = [x.astype(BF16) for x in pooled]
    mixed = jnp.concatenate([_dot(pooled[g], pw[g]) for g in range(len(POOL_WINDOWS))], axis=-1)
    pool_out = (mixed * scale_ref[...]).astype(BF16)
    o_ret = oret_ref[...]
    a = _dot(o_ret, wru_ref[...])
    b = _dot(pool_out, wpu_ref[...])
    z = gates_ref[...].astype(F32)
    g0 = _sig(z[:, :D_MODEL] + bias_ref[0:1, :])
    g1 = _sig(z[:, D_MODEL:] + bias_ref[1:2, :])
    merged = (g0 * a + g1 * b).astype(BF16)
    return pooled, mixed, pool_out, o_ret, a, b, g0, g1, merged


def _mix_tail_forward(p, gates, o_ret, h1, bias, scale, pw8, wru, wpu, wo, ride=None):
    tm, nt = TOKEN_TILE, SEQ // TOKEN_TILE

    def body(p_ref, halo_ref, gates_ref, oret_ref, bias_ref, scale_ref, pw_hbm, wru_ref, wpu_ref, wo_ref, h1_ref,
             h2_ref, pw, sem):
        i = pl.program_id(0)

        @pl.when(i == 0)
        def _():
            _load_pool_weight(pw_hbm, pw, sem)

        merged = _mix_tail_compute(i, tm, p_ref, halo_ref, gates_ref, oret_ref, bias_ref, scale_ref, pw, wru_ref, wpu_ref)[-1]
        h2_ref[...] = h1_ref[...] + _dot(merged, wo_ref[...])

    est = 3 * 2 * 2 * D_MODEL * D_MODEL + 2 * tm * D_MODEL * (4 + 4 + 2 + 4 + 4) + 16 * tm * D_MODEL * 4
    return _call(
        body, name="mix_tail_fwd", grid=(nt,),
        in_specs=_mix_tail_specs(tm) + [_row_spec(tm, D_MODEL)],
        out_specs=[_row_spec(tm, D_MODEL)], out_shape=[jax.ShapeDtypeStruct((SEQ, D_MODEL), F32)],
        scratch_shapes=[pltpu.VMEM((len(POOL_WINDOWS), POOL_GROUP_DIM, POOL_GROUP_DIM), BF16), pltpu.SemaphoreType.DMA((N_DEV,))],
        vmem_bytes=est, args=[p, p, gates, o_ret, bias, scale, pw8, wru, wpu, wo, h1], ride=ride)


def _mix_tail_backward(dh2, p, gates, o_ret, ret, qkvg, bias, scale, pw8, wru, wpu, wo, ride=None):
    tm, nt = TOKEN_TILE, SEQ // TOKEN_TILE
    n_groups = len(POOL_WINDOWS)
    rows_per_dev = POOL_GROUP_DIM // N_DEV

    def body(p_ref, halo_ref, gates_ref, oret_ref, bias_ref, scale_ref, pw_hbm, wru_ref, wpu_ref, wo_ref,
             dh2_ref, ret_ref, gr_ref,
             dret_ref, dgr_ref, dgates_ref, dpooled_ref, dwo_ref, dwru_ref, dwpu_ref, dpw_ref, dbias_ref, dscale_ref,
             pw, sem, acc_wo, acc_wru, acc_wpu, acc_pw):
        i = pl.program_id(0)

        @pl.when(i == 0)
        def _():
            _load_pool_weight(pw_hbm, pw, sem)
            for ref in (acc_wo, acc_wru, acc_wpu, acc_pw, dbias_ref, dscale_ref):
                ref[...] = jnp.zeros_like(ref)

        pooled, mixed, pool_out, o_ret, a, b, g0, g1, merged = _mix_tail_compute(
            i, tm, p_ref, halo_ref, gates_ref, oret_ref, bias_ref, scale_ref, pw, wru_ref, wpu_ref)
        dh2 = dh2_ref[...].astype(BF16)
        dm = _dot_nt(dh2, wo_ref[...])
        acc_wo[...] += _dot_tn(merged, dh2)
        da = (dm * g0).astype(BF16)
        db = (dm * g1).astype(BF16)
        dz0 = dm * a * g0 * (1.0 - g0)
        dz1 = dm * b * g1 * (1.0 - g1)
        dbias_ref[0:1, :] += jnp.sum(dz0, axis=0, keepdims=True)
        dbias_ref[1:2, :] += jnp.sum(dz1, axis=0, keepdims=True)
        dgates_ref[:, pl.ds(0, D_MODEL)] = dz0.astype(BF16)
        dgates_ref[:, pl.ds(D_MODEL, D_MODEL)] = dz1.astype(BF16)
        acc_wru[...] += _dot_tn(o_ret, da)
        acc_wpu[...] += _dot_tn(pool_out, db)
        d_oret = _dot_nt(da, wru_ref[...])
        d_pool_out = _dot_nt(db, wpu_ref[...])
        dscale_ref[...] += jnp.sum(d_pool_out * mixed, axis=0, keepdims=True)
        dmixed = (d_pool_out * scale_ref[...]).astype(BF16)
        for g in range(n_groups):
            dmg = dmixed[:, g * POOL_GROUP_DIM:(g + 1) * POOL_GROUP_DIM]
            acc_pw[g] += _dot_tn(pooled[g], dmg)
            dpooled_ref[:, pl.ds(g * POOL_GROUP_DIM, POOL_GROUP_DIM)] = _dot_nt(dmg, pw[g])
        gr = gr_ref[...].astype(F32)
        s = _sig(gr)
        silu = gr * s
        for hd in range(HEADS):
            cols = slice(hd * HEAD_DIM, (hd + 1) * HEAD_DIM)
            r_h = ret_ref[:, cols]
            rr = lax.rsqrt(jnp.mean(r_h * r_h, axis=-1, keepdims=True) + NORM_EPS)
            rhat = r_h * rr
            do_h = d_oret[:, cols]
            dgr_ref[:, cols] = (do_h * rhat * (s[:, cols] * (1.0 + gr[:, cols] * (1.0 - s[:, cols])))).astype(BF16)
            dret_ref[:, cols] = _rms_bwd(do_h * silu[:, cols], rhat, rr).astype(BF16)

        @pl.when(i == nt - 1)
        def _():
            rows = D_MODEL // N_DEV
            for d in range(N_DEV):
                dwo_ref[d] = acc_wo[pl.ds(d * rows, rows), :].astype(BF16)
                dwru_ref[d] = acc_wru[pl.ds(d * rows, rows), :].astype(BF16)
                dwpu_ref[d] = acc_wpu[pl.ds(d * rows, rows), :].astype(BF16)
                dpw_ref[d] = acc_pw[:, pl.ds(d * rows_per_dev, rows_per_dev), :].astype(BF16)

    sq = (N_DEV, D_MODEL // N_DEV, D_MODEL)
    pw_shape = (N_DEV, n_groups, rows_per_dev, POOL_GROUP_DIM)
    est = (3 * 2 * 2 * D_MODEL * D_MODEL + 3 * 4 * D_MODEL * D_MODEL + 3 * 2 * 2 * D_MODEL * D_MODEL
           + 2 * tm * D_MODEL * (4 + 4 + 2 + 4 + 4 + 2 + 2 + 2 + 4 + 4) + 24 * tm * D_MODEL * 4)
    return _call(
        body, name="mix_tail_bwd", grid=(nt,),
        in_specs=_mix_tail_specs(tm) + [_row_spec(tm, D_MODEL), _row_spec(tm, D_MODEL), _row_spec(tm, D_MODEL, 3)],
        out_specs=[_row_spec(tm, D_MODEL), _row_spec(tm, D_MODEL), _row_spec(tm, 2 * D_MODEL), _row_spec(tm, D_MODEL),
                   _full_spec(sq), _full_spec(sq), _full_spec(sq), _full_spec(pw_shape),
                   _full_spec((2, D_MODEL)), _full_spec((1, D_MODEL))],
        out_shape=[jax.ShapeDtypeStruct((SEQ, D_MODEL), BF16), jax.ShapeDtypeStruct((SEQ, D_MODEL), BF16),
                   jax.ShapeDtypeStruct((SEQ, 2 * D_MODEL), BF16), jax.ShapeDtypeStruct((SEQ, D_MODEL), F32),
                   jax.ShapeDtypeStruct(sq, BF16), jax.ShapeDtypeStruct(sq, BF16), jax.ShapeDtypeStruct(sq, BF16),
                   jax.ShapeDtypeStruct(pw_shape, BF16),
                   jax.ShapeDtypeStruct((2, D_MODEL), F32), jax.ShapeDtypeStruct((1, D_MODEL), F32)],
        scratch_shapes=[pltpu.VMEM((n_groups, POOL_GROUP_DIM, POOL_GROUP_DIM), BF16), pltpu.SemaphoreType.DMA((N_DEV,)),
                        pltpu.VMEM((D_MODEL, D_MODEL), F32), pltpu.VMEM((D_MODEL, D_MODEL), F32),
                        pltpu.VMEM((D_MODEL, D_MODEL), F32), pltpu.VMEM((n_groups, POOL_GROUP_DIM, POOL_GROUP_DIM), F32)],
        vmem_bytes=est, args=[p, p, gates, o_ret, bias, scale, pw8, wru, wpu, wo, dh2, ret, qkvg], ride=ride)


def _mix_proj_backward(dq, dk, dv, dgr, dpooled, dgates, cos, sin, h1, gain, dh2, wmix8, ride=None):
    tm, nt = TOKEN_TILE, SEQ // TOKEN_TILE
    halo_blocks = tm // HALO
    last_halo = SEQ // HALO - 1
    k_scale = HEAD_DIM ** -0.5

    def body(dq_ref, dk_ref, dv_ref, dgr_ref, dpool_ref, dhalo_ref, dgates_ref, cos_ref, sin_ref, h1_ref, g_ref,
             dh2_ref, wmix_hbm, dh1_ref, dproj_ref, dg_ref, wmix, sem):
        i = pl.program_id(0)

        @pl.when(i == 0)
        def _():
            _load_mix_weight(wmix_hbm, wmix, sem)
            dg_ref[...] = jnp.zeros_like(dg_ref)

        cos_t, sin_t = cos_ref[...], sin_ref[...]
        for seg, ref, scale in ((0, dq_ref, 1.0), (1, dk_ref, k_scale)):
            for hd in range(HEADS):
                lo = hd * HEAD_DIM
                d1, d2 = ref[:, lo:lo + ROT_HALF], ref[:, lo + ROT_HALF:lo + HEAD_DIM]
                dproj_ref[:, pl.ds(seg * D_MODEL + lo, ROT_HALF)] = ((d1 * cos_t + d2 * sin_t) * scale).astype(BF16)
                dproj_ref[:, pl.ds(seg * D_MODEL + lo + ROT_HALF, ROT_HALF)] = ((d2 * cos_t - d1 * sin_t) * scale).astype(BF16)
        dproj_ref[:, pl.ds(2 * D_MODEL, D_MODEL)] = dv_ref[...].astype(BF16)
        dproj_ref[:, pl.ds(3 * D_MODEL, D_MODEL)] = dgr_ref[...]
        dp = _pooled_transpose(jnp.concatenate([dpool_ref[...], dhalo_ref[...]], axis=0), i * tm)
        for g in range(len(POOL_WINDOWS)):
            dproj_ref[:, pl.ds(4 * D_MODEL + g * POOL_GROUP_DIM, POOL_GROUP_DIM)] = dp[g].astype(BF16)
        dproj_ref[:, pl.ds(5 * D_MODEL, 2 * D_MODEL)] = dgates_ref[...]
        du = jnp.zeros((tm, D_MODEL), F32)
        for seg in range(N_SEG):
            cols = pl.ds(seg * D_MODEL, D_MODEL)
            du = du + _dot_nt(dproj_ref[:, cols], wmix[:, cols])
        g = g_ref[...]
        _, xhat, r = _rms(h1_ref[...], g)
        dg_ref[...] += jnp.sum(du * xhat, axis=0, keepdims=True)
        dh1_ref[...] = dh2_ref[...] + _rms_bwd(du * g, xhat, r)

    est = 2 * D_MODEL * N_SEG * D_MODEL + 2 * tm * D_MODEL * (3 * 4 + 2 + 4 + 4 + 4 + 4 + 4 + 14) + 12 * tm * D_MODEL * 4
    return _call(
        body, name="mix_proj_bwd", grid=(nt,),
        in_specs=[_row_spec(tm, D_MODEL), _row_spec(tm, D_MODEL), _row_spec(tm, D_MODEL), _row_spec(tm, D_MODEL),
                  _row_spec(tm, D_MODEL),
                  pl.BlockSpec((HALO, D_MODEL), lambda i: (jnp.minimum((i + 1) * halo_blocks, last_halo), 0)),
                  _row_spec(tm, 2 * D_MODEL), _row_spec(tm, ROT_HALF), _row_spec(tm, ROT_HALF),
                  _row_spec(tm, D_MODEL), _full_spec((1, D_MODEL)), _row_spec(tm, D_MODEL), ANY],
        out_specs=[_row_spec(tm, D_MODEL), _row_spec(tm, N_SEG * D_MODEL), _full_spec((1, D_MODEL))],
        out_shape=[jax.ShapeDtypeStruct((SEQ, D_MODEL), F32), jax.ShapeDtypeStruct((SEQ, N_SEG * D_MODEL), BF16),
                   jax.ShapeDtypeStruct((1, D_MODEL), F32)],
        scratch_shapes=[pltpu.VMEM((D_MODEL, N_SEG * D_MODEL), BF16), pltpu.SemaphoreType.DMA((N_DEV,))],
        vmem_bytes=est, args=[dq, dk, dv, dgr, dpooled, dpooled, dgates, cos, sin, h1, gain, dh2, wmix8], ride=ride)


def _adamw(w, parts, m, v, name):
    rows, cols = w.shape
    n_parts = parts.shape[0]
    tr = max([t for t in range(16, 257, 16) if rows % t == 0], default=rows)
    c1 = 1.0 - ADAM_B1 ** ADAM_STEP
    c2 = 1.0 - ADAM_B2 ** ADAM_STEP

    def body(w_ref, p_ref, m_ref, v_ref, g_out, d_out, m_out, v_out):
        g = p_ref[0].astype(F32)
        for k in range(1, n_parts):
            g = g + p_ref[k].astype(F32)
        m_new = ADAM_B1 * m_ref[...] + (1.0 - ADAM_B1) * g
        v_new = ADAM_B2 * v_ref[...] + (1.0 - ADAM_B2) * (g * g)
        g_out[...] = g
        m_out[...] = m_new
        v_out[...] = v_new
        d_out[...] = -ADAM_LR * ((m_new / c1) / (jnp.sqrt(v_new / c2) + ADAM_EPS) + ADAM_WD * w_ref[...])

    spec = pl.BlockSpec((tr, cols), lambda i: (i, 0))
    out = jax.ShapeDtypeStruct((rows, cols), F32)
    return pl.pallas_call(
        body, name=name, grid=(rows // tr,),
        in_specs=[spec, pl.BlockSpec((n_parts, tr, cols), lambda i: (0, i, 0)), spec, spec],
        out_specs=[spec] * 4, out_shape=[out] * 4,
        compiler_params=_params(2 * tr * cols * (7 * 4 + n_parts * parts.dtype.itemsize) + 8 * tr * cols * 4, 1),
    )(_in_hbm(w), _in_hbm(parts), _in_hbm(m), _in_hbm(v))


def _mix_w_in_grad(u, dproj, ride=None):
    return _weight_grad(
        u, dproj, N_DEV,
        lambda tt: pl.BlockSpec((tt, D_MODEL), lambda b, t: (t, 0)),
        lambda tt: pl.BlockSpec((tt, MIX_SHARD), lambda b, t: (t, b)),
        D_MODEL, MIX_SHARD, name="w_in_grad", ride=ride)


def kernel(x, norm_ffn1, ffn1_w_in, ffn1_w_out, norm_mix, w_in, gate_bias, pool_w, pool_scale, w_ret_up, w_pool_up, w_out, norm_ffn2, ffn2_w_in, ffn2_w_out, norm_final, loss_target, m_norm_ffn1, m_ffn1_w_in, m_ffn1_w_out, m_norm_mix, m_w_in, m_gate_bias, m_pool_w, m_pool_scale, m_w_ret_up, m_w_pool_up, m_w_out, m_norm_ffn2, m_ffn2_w_in, m_ffn2_w_out, m_norm_final, v_norm_ffn1, v_ffn1_w_in, v_ffn1_w_out, v_norm_mix, v_w_in, v_gate_bias, v_pool_w, v_pool_scale, v_w_ret_up, v_w_pool_up, v_w_out, v_norm_ffn2, v_ffn2_w_in, v_ffn2_w_out, v_norm_final):
    assert x.shape == (1, SEQ, D_MODEL) and ffn1_w_in.shape == (1, D_MODEL, FF_SHARD) and w_in.shape == (1, D_MODEL, MIX_SHARD)
    x2, target = x[0], loss_target[0]

    cos, sin = _rotary_tables()
    tables = _retention_tables()
    big = [ffn1_w_in, ffn1_w_out, w_in, pool_w, w_ret_up, w_pool_up, w_out, ffn2_w_in, ffn2_w_out]
    bf = lambda w: w[0].astype(BF16)
    bf_t = lambda w: jnp.swapaxes(w[0], 0, 1).astype(BF16)
    square = lambda w: w.reshape(D_MODEL, D_MODEL)

    win1, wout1, bias8 = _alone(_GatherRide([bf_t(ffn1_w_in), bf(ffn1_w_out), gate_bias[0]]), "ffn1_weights_all_gather")
    wout1 = wout1.reshape(N_FF_GROUPS, FF_SHARD, D_MODEL)
    bias = bias8.transpose(1, 0, 2).reshape(2, D_MODEL)

    (h1, gu1), (wmix8,) = _ffn_forward(x2, norm_ffn1, win1, wout1, "ffn1_fwd", ride=_GatherRide([bf(w_in)]))
    (u, qkvg, p, gates), (win2,) = _mix_proj_forward(h1, norm_mix, wmix8, cos, sin, ride=_GatherRide([bf_t(ffn2_w_in)]))
    (ret, o_ret), (pw8, wru, wpu, wo) = _retention_forward(
        qkvg, tables, ride=_GatherRide([bf(pool_w), bf(w_ret_up), bf(w_pool_up), bf(w_out)]))
    wru, wpu, wo = square(wru), square(wpu), square(wo)
    (h2,), (wout2,) = _mix_tail_forward(p, gates, o_ret, h1, bias, pool_scale, pw8, wru, wpu, wo,
                                        ride=_GatherRide([bf(ffn2_w_out)]))
    wout2 = wout2.reshape(N_FF_GROUPS, FF_SHARD, D_MODEL)
    (dh3, gu2, loss_part, d_norm_final), _ = _ffn_forward(h2, norm_ffn2, win2, wout2, "ffn2_fwd_loss",
                                                          head=(target, norm_final.reshape(1, D_MODEL)))

    (dh2, dgu2, act2, xn2, df2, d_norm_ffn2), _ = _ffn_backward(dh3, h2, norm_ffn2, gu2, win2, wout2, "ffn2_bwd")
    d_wout2, _ = _ffn_w_out_grad(act2, df2, 2)
    d_win2, (r_wout2,) = _ffn_w_in_grad(xn2, dgu2, 2, ride=_ScatterRide([d_wout2]))
    (dret, dgr, dgates, dpooled, d_wo, d_wru, d_wpu, d_pw, d_bias, d_scale), (r_win2,) = _mix_tail_backward(
        dh2, p, gates, o_ret, ret, qkvg, bias, pool_scale, pw8, wru, wpu, wo, ride=_ScatterRide([d_win2]))
    (dq,), _ = _retention_backward_q(qkvg, dret, tables)
    (dk, dv), (r_pw, r_wru, r_wpu, r_wo) = _retention_backward_kv(
        qkvg, dret, tables, ride=_ScatterRide([d_pw, d_wru, d_wpu, d_wo]))
    (dh1, dproj, d_norm_mix), _ = _mix_proj_backward(dq, dk, dv, dgr, dpooled, dgates, cos, sin, h1, norm_mix, dh2, wmix8)
    d_wmix, _ = _mix_w_in_grad(u, dproj)
    (grad_x, dgu1, act1, xn1, df1, d_norm_ffn1), (r_wmix,) = _ffn_backward(
        dh1, x2, norm_ffn1, gu1, win1, wout1, "ffn1_bwd", ride=_ScatterRide([d_wmix]))
    d_wout1, _ = _ffn_w_out_grad(act1, df1, 1)
    d_win1, (r_wout1,) = _ffn_w_in_grad(xn1, dgu1, 1, ride=_ScatterRide([d_wout1]))
    (r_win1,) = _alone(_ScatterRide([d_win1]), "ffn1_w_in_grad_reduce_scatter")
    received = [r_win1, r_wout1, r_wmix, r_pw, r_wru, r_wpu, r_wo, r_win2, r_wout2]
    zero_row = jnp.zeros((1, D_MODEL), F32)
    small = _all_reduce_rows(jnp.concatenate(
        [d_norm_ffn1, d_norm_mix, d_scale, d_norm_ffn2, d_norm_final, d_bias, zero_row], axis=0))
    loss = lax.psum(loss_part[0, 0], ("x", "y", "c"))

    names = ["ffn1_w_in", "ffn1_w_out", "w_in", "pool_w", "w_ret_up", "w_pool_up", "w_out", "ffn2_w_in", "ffn2_w_out"]
    moments_m = [m_ffn1_w_in, m_ffn1_w_out, m_w_in, m_pool_w, m_w_ret_up, m_w_pool_up, m_w_out, m_ffn2_w_in, m_ffn2_w_out]
    moments_v = [v_ffn1_w_in, v_ffn1_w_out, v_w_in, v_pool_w, v_w_ret_up, v_w_pool_up, v_w_out, v_ffn2_w_in, v_ffn2_w_out]
    results = {}
    for nm, w, parts, m, v in zip(names, big, received, moments_m, moments_v):
        if nm in ("ffn1_w_in", "ffn2_w_in"):
            flat, back = (lambda a: jnp.swapaxes(a[0], 0, 1)), (lambda o: jnp.swapaxes(o, 0, 1)[None])
        else:
            flat, back = (lambda a, w=w: a.reshape(-1, w.shape[-1])), (lambda o, w=w: o.reshape(w.shape))
        outs = _adamw(flat(w), parts.reshape((N_DEV,) + flat(w).shape), flat(m), flat(v), name=f"adamw_{nm}")
        results[nm] = [back(o) for o in outs]

    my_id = _linear_id(*_my_position())
    bias_cols = gate_bias.shape[-1]
    pad = lambda a: jnp.pad(a[0], ((0, 0), (0, D_MODEL - bias_cols)))
    pack = lambda a, b, c, d, e, gb: jnp.concatenate([a, b, c, d, e.reshape(1, D_MODEL), pad(gb), zero_row], axis=0)
    d_bias_mine = lax.dynamic_slice_in_dim(small[5:7], my_id * bias_cols, bias_cols, axis=1)
    g_small = jnp.concatenate([small[0:5], jnp.pad(d_bias_mine, ((0, 0), (0, D_MODEL - bias_cols))), zero_row], axis=0)
    s_outs = _adamw(pack(norm_ffn1, norm_mix, pool_scale, norm_ffn2, norm_final, gate_bias), g_small[None],
                    pack(m_norm_ffn1, m_norm_mix, m_pool_scale, m_norm_ffn2, m_norm_final, m_gate_bias),
                    pack(v_norm_ffn1, v_norm_mix, v_pool_scale, v_norm_ffn2, v_norm_final, v_gate_bias), name="adamw_small")
    for row, nm in enumerate(["norm_ffn1", "norm_mix", "pool_scale", "norm_ffn2"]):
        results[nm] = [o[row:row + 1] for o in s_outs]
    results["norm_final"] = [o[4] for o in s_outs]
    results["gate_bias"] = [o[5:7, :bias_cols][None] for o in s_outs]

    order = ["norm_ffn1", "ffn1_w_in", "ffn1_w_out", "norm_mix", "w_in", "gate_bias", "pool_w", "pool_scale",
             "w_ret_up", "w_pool_up", "w_out", "norm_ffn2", "ffn2_w_in", "ffn2_w_out", "norm_final"]
    return (loss, grad_x[None], *[results[nm][0] for nm in order], *[results[nm][1] for nm in order],
            *[results[nm][2] for nm in order], *[results[nm][3] for nm in order])
```

```python
import functools

import numpy as np
import jax
import jax.numpy as jnp
from jax import lax
from jax.experimental import pallas as pl
from jax.experimental.pallas import tpu as pltpu

F32 = jnp.float32
BF16 = jnp.bfloat16

N_DEV = 8
D_MODEL = 1024
SEQ = 4096
D_FF = 2816
FF_SHARD = 2 * D_FF // N_DEV
N_FF_GROUPS = N_DEV // 2
HEADS = 4
HEAD_DIM = 256
ROT_HALF = HEAD_DIM // 2
CHUNK = 64
RET_BLOCK = 256
POOL_WINDOWS = (2, 4, 8, 16)
POOL_GROUP_DIM = 256
HALO = 16
MIX_SHARD = 7 * D_MODEL // N_DEV
N_SEG = 7
ROPE_BASE = 10000.0
NORM_EPS = 1e-6
FFN_RES_WEIGHT = 0.5
ADAM_LR, ADAM_B1, ADAM_B2, ADAM_EPS, ADAM_WD, ADAM_STEP = 0.001, 0.9, 0.999, 1e-08, 0.01, 10

TOKEN_TILE = 256
WIDE_TILE = 512
VMEM_CAP_V7X = 64 * 1024 * 1024
MESH = pl.DeviceIdType.MESH
ANY = pl.BlockSpec(memory_space=pl.ANY)


def _vmem_limit(estimate_bytes):
    return int(min(estimate_bytes * 5 // 4 + (6 << 20), VMEM_CAP_V7X - (4 << 20)))


def _params(estimate_bytes, n_grid):
    return pltpu.CompilerParams(dimension_semantics=("arbitrary",) * n_grid,
                                vmem_limit_bytes=_vmem_limit(estimate_bytes))


def _dot(a, b):
    return jnp.dot(a, b, preferred_element_type=F32)


def _dot_nt(a, b):
    return lax.dot_general(a, b, (((1,), (1,)), ((), ())), preferred_element_type=F32)


def _dot_tn(a, b):
    return lax.dot_general(a, b, (((0,), (0,)), ((), ())), preferred_element_type=F32)


def _sig(x):
    return 1.0 / (1.0 + jnp.exp(-x))


def _rms(x, g):
    r = lax.rsqrt(jnp.mean(x * x, axis=-1, keepdims=True) + NORM_EPS)
    xhat = x * r
    return xhat * g, xhat, r


def _rms_bwd(dyg, xhat, r):
    return r * (dyg - xhat * jnp.mean(dyg * xhat, axis=-1, keepdims=True))


def _row_spec(tile, width, col=0):
    return pl.BlockSpec((tile, width), lambda i, c=col: (i, c))


def _full_spec(shape):
    return pl.BlockSpec(shape, lambda *_: (0,) * len(shape))


def _rotary_tables():
    inv_freq = (np.float32(ROPE_BASE) ** (-np.arange(ROT_HALF, dtype=np.float32) / np.float32(ROT_HALF))).astype(np.float32)
    ang = (np.arange(SEQ, dtype=np.float32)[:, None] * inv_freq[None, :]).astype(np.float32)
    return jnp.asarray(np.cos(ang.astype(np.float64)), F32), jnp.asarray(np.sin(ang.astype(np.float64)), F32)


def _retention_tables():
    log_gamma = np.log(1.0 - 2.0 ** (-5.0 - np.arange(HEADS, dtype=np.float64)))
    n = np.arange(RET_BLOCK)
    diff = (n[:, None] - n[None, :]).astype(np.float64)
    same = (n[:, None] // CHUNK) == (n[None, :] // CHUNK)
    earlier = (n[None, :] // CHUNK) < (n[:, None] // CHUNK)
    expo = np.where(same, np.abs(diff), diff)
    mask = np.where(same | earlier, np.exp(log_gamma[:, None, None] * expo[None]), 0.0)
    qdec = np.exp(log_gamma[:, None] * (n[None, :] + 1.0))[:, :, None]
    kdec = np.exp(log_gamma[:, None] * (RET_BLOCK - 1.0 - n[None, :]))[:, :, None]
    cdec = np.exp(log_gamma * RET_BLOCK)[:, None, None]
    return (jnp.asarray(mask, F32), jnp.asarray(qdec, F32), jnp.asarray(kdec, F32), jnp.asarray(cdec, F32))


def _my_position():
    return lax.axis_index("x"), lax.axis_index("y"), lax.axis_index("c")


def _linear_id(px, py, pc):
    return 4 * px + 2 * py + pc


def _when(pred, fn):
    if isinstance(pred, bool):
        if pred:
            fn()
    else:
        pl.when(pred)(fn)


class _GatherRide:
    def __init__(self, shards):
        self.args = list(shards)
        n = self.n = len(shards)
        self.out_shape = [pltpu.HBM((N_DEV,) + s.shape, s.dtype) for s in shards]
        self.scratch = [pltpu.SemaphoreType.DMA((n, 7)), pltpu.SemaphoreType.DMA((n, 7)), pltpu.SemaphoreType.DMA((n,))]

    def _plan(self, src, out, sems):
        send_sems, recv_sems, local_sem = sems
        x, y, c = _my_position()
        me, sibling = (x, y, c), (x, y, 1 - c)
        chips = [(1 - x, y), (x, 1 - y), (1 - x, 1 - y)]

        def copy(t, k, block, to, from_src=False):
            rows = out[t].at[_linear_id(*block)]
            return pltpu.make_async_remote_copy(
                src_ref=src[t] if from_src else rows, dst_ref=rows,
                send_sem=send_sems.at[t, k], recv_sem=recv_sems.at[t, k],
                device_id=to, device_id_type=MESH)

        local = [pltpu.make_async_copy(src[t], out[t].at[_linear_id(*me)], local_sem.at[t]) for t in range(self.n)]
        return copy, local, me, sibling, chips, c

    def begin(self, first, src, out, sems):
        copy, local, me, sibling, chips, c = self._plan(src, out, sems)

        def start():
            for cp in local:
                cp.start()
            for t in range(self.n):
                copy(t, 0, me, sibling, from_src=True).start()
                for j, chip in enumerate(chips):
                    copy(t, 1 + j, me, (*chip, c), from_src=True).start()

        _when(first, start)

    def finish(self, mid, last, src, out, sems):
        copy, local, me, sibling, chips, c = self._plan(src, out, sems)

        def pass_on():
            for j, chip in enumerate(chips):
                for t in range(self.n):
                    copy(t, 1 + j, (*chip, c), me).wait_recv()
                    copy(t, 4 + j, (*chip, c), sibling).start()

        def drain():
            for t in range(self.n):
                copy(t, 0, sibling, me).wait_recv()
                for j, chip in enumerate(chips):
                    copy(t, 4 + j, (*chip, 1 - c), me).wait_recv()
            for t in range(self.n):
                copy(t, 0, me, sibling, from_src=True).wait_send()
                for j, chip in enumerate(chips):
                    copy(t, 1 + j, me, (*chip, c), from_src=True).wait_send()
                    copy(t, 4 + j, (*chip, c), sibling).wait_send()
            for cp in local:
                cp.wait()

        _when(mid, pass_on)
        _when(last, drain)


class _ScatterRide:
    def __init__(self, partials):
        self.args = list(partials)
        n = self.n = len(partials)
        self.out_shape = [pltpu.HBM(p.shape, p.dtype) for p in partials]
        self.scratch = [pltpu.SemaphoreType.DMA((n, 7)), pltpu.SemaphoreType.DMA((n, 7)), pltpu.SemaphoreType.DMA((n,))]

    def _plan(self, src, out, sems):
        send_sems, recv_sems, local_sem = sems
        x, y, c = _my_position()

        def peer(k):
            return (x ^ (k >> 2), y ^ ((k >> 1) & 1), c ^ (k & 1))

        copies = [pltpu.make_async_remote_copy(
            src_ref=src[t].at[_linear_id(*peer(k))], dst_ref=out[t].at[k],
            send_sem=send_sems.at[t, k - 1], recv_sem=recv_sems.at[t, k - 1],
            device_id=peer(k), device_id_type=MESH) for t in range(self.n) for k in range(1, N_DEV)]
        local = [pltpu.make_async_copy(src[t].at[_linear_id(x, y, c)], out[t].at[0], local_sem.at[t])
                 for t in range(self.n)]
        return copies, local

    def begin(self, first, src, out, sems):
        copies, local = self._plan(src, out, sems)

        def start():
            for cp in local + copies:
                cp.start()

        _when(first, start)

    def finish(self, mid, last, src, out, sems):
        copies, local = self._plan(src, out, sems)

        def drain():
            for cp in copies:
                cp.wait_recv()
            for cp in copies:
                cp.wait_send()
            for cp in local:
                cp.wait()

        _when(last, drain)


def _in_hbm(a):
    return pltpu.with_memory_space_constraint(a, pltpu.HBM)


def _call(body, *, name, grid, in_specs, out_specs, out_shape, scratch_shapes, vmem_bytes, args, ride=None):
    n_in, n_out, n_s = len(in_specs), len(out_specs), len(scratch_shapes)
    params = _params(vmem_bytes, len(grid))
    args = [_in_hbm(a) for a in args]
    out_shape = [pltpu.HBM(s.shape, s.dtype) for s in out_shape]
    if ride is None:
        outs = pl.pallas_call(body, name=name, grid=grid, in_specs=in_specs, out_specs=out_specs, out_shape=out_shape,
                              scratch_shapes=scratch_shapes, compiler_params=params)(*args)
        return list(outs), []
    total = int(np.prod(grid))

    def riding_body(*refs):
        a = n_in
        b = a + ride.n
        c = b + n_out
        d = c + ride.n
        e = d + n_s
        step = pl.program_id(0)
        for axis in range(1, len(grid)):
            step = step * grid[axis] + pl.program_id(axis)
        ride.begin(step == 0, refs[a:b], refs[c:d], refs[e:])
        body(*refs[:a], *refs[b:c], *refs[d:e])
        ride.finish(step == (3 * total) // 4, step == total - 1, refs[a:b], refs[c:d], refs[e:])

    outs = pl.pallas_call(
        riding_body, name=name, grid=grid, in_specs=list(in_specs) + [ANY] * ride.n,
        out_specs=list(out_specs) + [ANY] * ride.n, out_shape=list(out_shape) + ride.out_shape,
        scratch_shapes=list(scratch_shapes) + ride.scratch, compiler_params=params)(*args, *[_in_hbm(a) for a in ride.args])
    return list(outs[:n_out]), list(outs[n_out:])


def _alone(ride, name):
    def body(*refs):
        src, out, sems = refs[:ride.n], refs[ride.n:2 * ride.n], refs[2 * ride.n:]
        ride.begin(True, src, out, sems)
        ride.finish(True, True, src, out, sems)

    return list(pl.pallas_call(body, name=name, out_shape=ride.out_shape, in_specs=[ANY] * ride.n,
                               out_specs=[ANY] * ride.n, scratch_shapes=ride.scratch)(*[_in_hbm(a) for a in ride.args]))


def _all_reduce_rows(block):
    rows, width = block.shape

    def body(x_ref, sum_ref, gathered, send_sems, recv_sems, local_sem):
        x, y, c = _my_position()
        me, sibling = (x, y, c), (x, y, 1 - c)
        chips = [(1 - x, y), (x, 1 - y), (1 - x, 1 - y)]

        def slot(px, py, pc):
            return gathered.at[_linear_id(px, py, pc)]

        def copy(k, block_of, to, from_src=False):
            return pltpu.make_async_remote_copy(
                src_ref=x_ref if from_src else slot(*block_of), dst_ref=slot(*block_of),
                send_sem=send_sems.at[k], recv_sem=recv_sems.at[k], device_id=to, device_id_type=MESH)

        mine = pltpu.make_async_copy(x_ref, slot(*me), local_sem)
        mine.start()
        first = [copy(0, me, sibling, from_src=True)]
        first += [copy(1 + j, me, (*chip, c), from_src=True) for j, chip in enumerate(chips)]
        for cp in first:
            cp.start()
        passed = [copy(4 + j, (*chip, c), sibling) for j, chip in enumerate(chips)]
        for j, chip in enumerate(chips):
            copy(1 + j, (*chip, c), me).wait_recv()
            passed[j].start()
        copy(0, sibling, me).wait_recv()
        for j, chip in enumerate(chips):
            copy(4 + j, (*chip, 1 - c), me).wait_recv()
        for cp in first + passed:
            cp.wait_send()
        mine.wait()
        total = gathered[0]
        for d in range(1, N_DEV):
            total = total + gathered[d]
        sum_ref[...] = total

    return pl.pallas_call(
        body, name="small_grads_all_reduce",
        out_shape=jax.ShapeDtypeStruct((rows, width), F32),
        in_specs=[pl.BlockSpec(memory_space=pltpu.VMEM)],
        out_specs=pl.BlockSpec(memory_space=pltpu.VMEM),
        scratch_shapes=[pltpu.VMEM((N_DEV, rows, width), F32),
                        pltpu.SemaphoreType.DMA((7,)), pltpu.SemaphoreType.DMA((7,)), pltpu.SemaphoreType.DMA],
    )(block)


def _load_ffn_weights(win_hbm, wout_hbm, win, wout, sem):
    a = pltpu.make_async_copy(win_hbm, win, sem.at[0])
    b = pltpu.make_async_copy(wout_hbm, wout, sem.at[1])
    a.start()
    b.start()
    a.wait()
    b.wait()


def _ffn_forward(h_in, gain, win8, wout, name, head=None, ride=None):
    tm, nt = WIDE_TILE, SEQ // WIDE_TILE

    def body(*refs):
        if head is None:
            x_ref, g_ref, win_hbm, wout_hbm, out_ref, gu_ref, win, wout, sem = refs
        else:
            x_ref, g_ref, win_hbm, wout_hbm, tgt_ref, gf_ref, out_ref, gu_ref, loss_ref, dgf_ref, win, wout, sem = refs
        i = pl.program_id(0)

        @pl.when(i == 0)
        def _():
            _load_ffn_weights(win_hbm, wout_hbm, win, wout, sem)
            if head is not None:
                loss_ref[...] = jnp.zeros_like(loss_ref)
                dgf_ref[...] = jnp.zeros_like(dgf_ref)

        x = x_ref[...]
        xn, _, _ = _rms(x, g_ref[...])
        xb = xn.astype(BF16)
        acc = jnp.zeros((tm, D_MODEL), F32)
        for j in range(N_FF_GROUPS):
            gate = _dot_nt(xb, win[j])
            up = _dot_nt(xb, win[j + N_FF_GROUPS])
            gu_ref[j] = gate.astype(BF16)
            gu_ref[j + N_FF_GROUPS] = up.astype(BF16)
            act = gate * _sig(gate) * up
            acc = acc + _dot(act.astype(BF16), wout[j])
        h = x + FFN_RES_WEIGHT * acc
        if head is None:
            out_ref[...] = h
        else:
            gf = gf_ref[...]
            y, hhat, r = _rms(h, gf)
            err = y - tgt_ref[...]
            loss_ref[...] += jnp.full(loss_ref.shape, 0.5 / D_MODEL * jnp.sum(err * err), F32)
            dy = err * (1.0 / D_MODEL)
            dgf_ref[...] += jnp.sum(dy * hhat, axis=0, keepdims=True)
            out_ref[...] = _rms_bwd(dy * gf, hhat, r)

    weights = 2 * D_MODEL * 2 * D_FF + 2 * D_FF * D_MODEL
    tiles = 2 * (2 * 4 * tm * D_MODEL + 2 * tm * 2 * D_FF) + (2 * 4 * tm * D_MODEL if head else 0)
    in_specs = [_row_spec(tm, D_MODEL), _full_spec((1, D_MODEL)), ANY, ANY]
    out_shape = [jax.ShapeDtypeStruct((SEQ, D_MODEL), F32), jax.ShapeDtypeStruct((N_DEV, SEQ, FF_SHARD), BF16)]
    out_specs = [_row_spec(tm, D_MODEL), pl.BlockSpec((N_DEV, tm, FF_SHARD), lambda i: (0, i, 0))]
    args = [h_in, gain, win8, wout]
    if head is not None:
        in_specs += [_row_spec(tm, D_MODEL), _full_spec((1, D_MODEL))]
        out_shape += [jax.ShapeDtypeStruct((1, 128), F32), jax.ShapeDtypeStruct((1, D_MODEL), F32)]
        out_specs += [_full_spec((1, 128)), _full_spec((1, D_MODEL))]
        args += list(head)
    return _call(
        body, name=name, grid=(nt,), in_specs=in_specs, out_specs=out_specs, out_shape=out_shape,
        scratch_shapes=[pltpu.VMEM((N_DEV, FF_SHARD, D_MODEL), BF16), pltpu.VMEM((N_FF_GROUPS, FF_SHARD, D_MODEL), BF16),
                        pltpu.SemaphoreType.DMA((2,))],
        vmem_bytes=weights + tiles + 16 * tm * FF_SHARD * 4, args=args, ride=ride)


def _ffn_backward(dh_out, h_in, gain, gu, win8, wout, name, ride=None):
    tm, nt = TOKEN_TILE, SEQ // TOKEN_TILE

    def body(dh_ref, x_ref, g_ref, gu_ref, win_hbm, wout_hbm,
             dhin_ref, dgu_ref, act_ref, xn_ref, df_ref, dg_ref, win, wout, sem):
        i = pl.program_id(0)

        @pl.when(i == 0)
        def _():
            _load_ffn_weights(win_hbm, wout_hbm, win, wout, sem)
            dg_ref[...] = jnp.zeros_like(dg_ref)

        dh = dh_ref[...]
        g = g_ref[...]
        xn, xhat, r = _rms(x_ref[...], g)
        df = (FFN_RES_WEIGHT * dh).astype(BF16)
        dxn = jnp.zeros((tm, D_MODEL), F32)
        for j in range(N_FF_GROUPS):
            gate = gu_ref[j].astype(F32)
            up = gu_ref[j + N_FF_GROUPS].astype(F32)
            dact = _dot_nt(df, wout[j])
            s = _sig(gate)
            silu = gate * s
            dgate = (dact * up * (s * (1.0 + gate * (1.0 - s)))).astype(BF16)
            dup = (dact * silu).astype(BF16)
            act_ref[j] = (silu * up).astype(BF16)
            dgu_ref[j] = dgate
            dgu_ref[j + N_FF_GROUPS] = dup
            dxn = dxn + _dot(dgate, win[j]) + _dot(dup, win[j + N_FF_GROUPS])
        dg_ref[...] += jnp.sum(dxn * xhat, axis=0, keepdims=True)
        dhin_ref[...] = dh + _rms_bwd(dxn * g, xhat, r)
        xn_ref[...] = xn.astype(BF16)
        df_ref[...] = df

    weights = 2 * D_MODEL * 2 * D_FF + 2 * D_FF * D_MODEL
    tiles = 2 * (3 * 4 * tm * D_MODEL + 2 * tm * (2 * 2 * D_FF + D_FF) + 2 * 2 * tm * D_MODEL)
    gu_spec = pl.BlockSpec((N_DEV, tm, FF_SHARD), lambda i: (0, i, 0))
    return _call(
        body, name=name, grid=(nt,),
        in_specs=[_row_spec(tm, D_MODEL), _row_spec(tm, D_MODEL), _full_spec((1, D_MODEL)), gu_spec, ANY, ANY],
        out_specs=[_row_spec(tm, D_MODEL), gu_spec, pl.BlockSpec((N_FF_GROUPS, tm, FF_SHARD), lambda i: (0, i, 0)),
                   _row_spec(tm, D_MODEL), _row_spec(tm, D_MODEL), _full_spec((1, D_MODEL))],
        out_shape=[jax.ShapeDtypeStruct((SEQ, D_MODEL), F32), jax.ShapeDtypeStruct((N_DEV, SEQ, FF_SHARD), BF16),
                   jax.ShapeDtypeStruct((N_FF_GROUPS, SEQ, FF_SHARD), BF16), jax.ShapeDtypeStruct((SEQ, D_MODEL), BF16),
                   jax.ShapeDtypeStruct((SEQ, D_MODEL), BF16), jax.ShapeDtypeStruct((1, D_MODEL), F32)],
        scratch_shapes=[pltpu.VMEM((N_DEV, FF_SHARD, D_MODEL), BF16), pltpu.VMEM((N_FF_GROUPS, FF_SHARD, D_MODEL), BF16),
                        pltpu.SemaphoreType.DMA((2,))],
        vmem_bytes=weights + tiles + 20 * tm * FF_SHARD * 4, args=[dh_out, h_in, gain, gu, win8, wout], ride=ride)


def _weight_grad(x, g, n_out, x_spec, g_spec, k_dim, n_dim, name, tt=1024, ride=None):
    nt = SEQ // tt

    def body(x_ref, g_ref, out_ref, acc):
        t = pl.program_id(1)

        @pl.when(t == 0)
        def _():
            acc[...] = jnp.zeros_like(acc)

        acc[...] += _dot_tn(x_ref[...], g_ref[...])

        @pl.when(t == nt - 1)
        def _():
            out_ref[...] = acc[...].astype(BF16)

    outs, ride_outs = _call(
        body, name=name, grid=(n_out, nt), in_specs=[x_spec(tt), g_spec(tt)],
        out_specs=[pl.BlockSpec((None, k_dim, n_dim), lambda b, t: (b, 0, 0))],
        out_shape=[jax.ShapeDtypeStruct((n_out, k_dim, n_dim), BF16)],
        scratch_shapes=[pltpu.VMEM((k_dim, n_dim), F32)],
        vmem_bytes=2 * 2 * tt * (k_dim + n_dim) + 8 * k_dim * n_dim + 4 * k_dim * n_dim, args=[x, g], ride=ride)
    return outs[0], ride_outs


def _ffn_w_out_grad(act, df, tag, ride=None):
    d_wout, ride_outs = _weight_grad(
        act, df, N_FF_GROUPS,
        lambda tt: pl.BlockSpec((None, tt, FF_SHARD), lambda b, t: (b, t, 0)),
        lambda tt: pl.BlockSpec((tt, D_MODEL), lambda b, t: (t, 0)),
        FF_SHARD, D_MODEL, name=f"ffn{tag}_w_out_grad", ride=ride)
    return d_wout.reshape(N_DEV, D_FF // N_DEV, D_MODEL), ride_outs


def _ffn_w_in_grad(xn, dgu, tag, ride=None):
    return _weight_grad(
        dgu, xn, N_DEV,
        lambda tt: pl.BlockSpec((None, tt, FF_SHARD), lambda b, t: (b, t, 0)),
        lambda tt: pl.BlockSpec((tt, D_MODEL), lambda b, t: (t, 0)),
        FF_SHARD, D_MODEL, name=f"ffn{tag}_w_in_grad", ride=ride)


def _load_mix_weight(wmix_hbm, wmix, sem):
    copies = [pltpu.make_async_copy(wmix_hbm.at[d], wmix.at[:, pl.ds(d * MIX_SHARD, MIX_SHARD)], sem.at[d])
              for d in range(N_DEV)]
    for cp in copies:
        cp.start()
    for cp in copies:
        cp.wait()


def _load_pool_weight(pw_hbm, pw, sem):
    rows = POOL_GROUP_DIM // N_DEV
    copies = [pltpu.make_async_copy(pw_hbm.at[d], pw.at[:, pl.ds(d * rows, rows), :], sem.at[d]) for d in range(N_DEV)]
    for cp in copies:
        cp.start()
    for cp in copies:
        cp.wait()


def _rotate(x1, x2, cos, sin):
    return x1 * cos - x2 * sin, x1 * sin + x2 * cos


def _mix_proj_forward(h1, gain, wmix8, cos, sin, ride=None):
    tm, nt = WIDE_TILE, SEQ // WIDE_TILE
    k_scale = HEAD_DIM ** -0.5

    def body(h_ref, g_ref, wmix_hbm, cos_ref, sin_ref, u_ref, qkvg_ref, p_ref, gates_ref, wmix, sem):
        @pl.when(pl.program_id(0) == 0)
        def _():
            _load_mix_weight(wmix_hbm, wmix, sem)

        u = _rms(h_ref[...], g_ref[...])[0].astype(BF16)
        u_ref[...] = u
        cos_t, sin_t = cos_ref[...], sin_ref[...]
        for seg in range(N_SEG):
            pr = _dot(u, wmix[:, pl.ds(seg * D_MODEL, D_MODEL)])
            if seg < 2:
                scale = 1.0 if seg == 0 else k_scale
                for hd in range(HEADS):
                    lo = hd * HEAD_DIM
                    o1, o2 = _rotate(pr[:, lo:lo + ROT_HALF], pr[:, lo + ROT_HALF:lo + HEAD_DIM], cos_t, sin_t)
                    qkvg_ref[:, pl.ds(seg * D_MODEL + lo, ROT_HALF)] = (o1 * scale).astype(BF16)
                    qkvg_ref[:, pl.ds(seg * D_MODEL + lo + ROT_HALF, ROT_HALF)] = (o2 * scale).astype(BF16)
            elif seg < 4:
                qkvg_ref[:, pl.ds(seg * D_MODEL, D_MODEL)] = pr.astype(BF16)
            elif seg == 4:
                p_ref[...] = pr
            else:
                gates_ref[:, pl.ds((seg - 5) * D_MODEL, D_MODEL)] = pr.astype(BF16)

    est = 2 * D_MODEL * N_SEG * D_MODEL + 2 * tm * (4 * D_MODEL + 2 * D_MODEL + 2 * 4 * D_MODEL + 4 * D_MODEL + 2 * 2 * D_MODEL)
    return _call(
        body, name="mix_proj_fwd", grid=(nt,),
        in_specs=[_row_spec(tm, D_MODEL), _full_spec((1, D_MODEL)), ANY, _row_spec(tm, ROT_HALF), _row_spec(tm, ROT_HALF)],
        out_specs=[_row_spec(tm, D_MODEL), _row_spec(tm, 4 * D_MODEL), _row_spec(tm, D_MODEL), _row_spec(tm, 2 * D_MODEL)],
        out_shape=[jax.ShapeDtypeStruct((SEQ, D_MODEL), BF16), jax.ShapeDtypeStruct((SEQ, 4 * D_MODEL), BF16),
                   jax.ShapeDtypeStruct((SEQ, D_MODEL), F32), jax.ShapeDtypeStruct((SEQ, 2 * D_MODEL), BF16)],
        scratch_shapes=[pltpu.VMEM((D_MODEL, N_SEG * D_MODEL), BF16), pltpu.SemaphoreType.DMA((N_DEV,))],
        vmem_bytes=est + 8 * tm * D_MODEL * 4, args=[h1, gain, wmix8, cos, sin], ride=ride)


def _seg_block_spec(seg, reverse=False):
    nb = SEQ // RET_BLOCK
    if reverse:
        return pl.BlockSpec((RET_BLOCK, D_MODEL), lambda i, s=seg: (nb - 1 - i, s))
    return pl.BlockSpec((RET_BLOCK, D_MODEL), lambda i, s=seg: (i, s))


def _table_specs():
    return [_full_spec((HEADS, RET_BLOCK, RET_BLOCK)), _full_spec((HEADS, RET_BLOCK, 1)),
            _full_spec((HEADS, RET_BLOCK, 1)), _full_spec((HEADS, 1, 1))]


def _head_cols(h):
    return pl.ds(h * HEAD_DIM, HEAD_DIM)


def _retention_forward(qkvg, tables, ride=None):
    nb = SEQ // RET_BLOCK

    def body(q_ref, k_ref, v_ref, gr_ref, mask_ref, qdec_ref, kdec_ref, cdec_ref, ret_ref, o_ref, state):
        @pl.when(pl.program_id(0) == 0)
        def _():
            state[...] = jnp.zeros_like(state)

        for h in range(HEADS):
            cols = _head_cols(h)
            q, k, v = q_ref[:, cols], k_ref[:, cols], v_ref[:, cols]
            scores = _dot_nt(q, k) * mask_ref[h]
            inner = _dot(scores.astype(BF16), v)
            cross = _dot((q.astype(F32) * qdec_ref[h]).astype(BF16), state[h].astype(BF16))
            ret = inner + cross
            state[h] = state[h] * cdec_ref[h] + _dot_tn((k.astype(F32) * kdec_ref[h]).astype(BF16), v)
            ret_ref[:, cols] = ret
            retn = ret * lax.rsqrt(jnp.mean(ret * ret, axis=-1, keepdims=True) + NORM_EPS)
            gr = gr_ref[:, cols].astype(F32)
            o_ref[:, cols] = (retn * (gr * _sig(gr))).astype(BF16)

    return _call(
        body, name="retention_fwd", grid=(nb,),
        in_specs=[_seg_block_spec(0), _seg_block_spec(1), _seg_block_spec(2), _seg_block_spec(3)] + _table_specs(),
        out_specs=[_row_spec(RET_BLOCK, D_MODEL)] * 2,
        out_shape=[jax.ShapeDtypeStruct((SEQ, D_MODEL), F32), jax.ShapeDtypeStruct((SEQ, D_MODEL), BF16)],
        scratch_shapes=[pltpu.VMEM((HEADS, HEAD_DIM, HEAD_DIM), F32)],
        vmem_bytes=24 * RET_BLOCK * D_MODEL * 4, args=[qkvg, qkvg, qkvg, qkvg, *tables], ride=ride)


def _retention_backward_q(qkvg, dret, tables, ride=None):
    nb = SEQ // RET_BLOCK

    def body(k_ref, v_ref, do_ref, mask_ref, qdec_ref, kdec_ref, cdec_ref, dq_ref, state):
        @pl.when(pl.program_id(0) == 0)
        def _():
            state[...] = jnp.zeros_like(state)

        for h in range(HEADS):
            cols = _head_cols(h)
            k, v, do = k_ref[:, cols], v_ref[:, cols], do_ref[:, cols]
            dscores = _dot_nt(do, v) * mask_ref[h]
            dq_ref[:, cols] = _dot(dscores.astype(BF16), k) + _dot_nt(do, state[h].astype(BF16)) * qdec_ref[h]
            state[h] = state[h] * cdec_ref[h] + _dot_tn((k.astype(F32) * kdec_ref[h]).astype(BF16), v)

    return _call(
        body, name="retention_bwd_q", grid=(nb,),
        in_specs=[_seg_block_spec(1), _seg_block_spec(2), _row_spec(RET_BLOCK, D_MODEL)] + _table_specs(),
        out_specs=[_row_spec(RET_BLOCK, D_MODEL)],
        out_shape=[jax.ShapeDtypeStruct((SEQ, D_MODEL), F32)],
        scratch_shapes=[pltpu.VMEM((HEADS, HEAD_DIM, HEAD_DIM), F32)],
        vmem_bytes=24 * RET_BLOCK * D_MODEL * 4, args=[qkvg, qkvg, dret, *tables], ride=ride)


def _retention_backward_kv(qkvg, dret, tables, ride=None):
    nb = SEQ // RET_BLOCK

    def body(q_ref, k_ref, v_ref, do_ref, mask_ref, qdec_ref, kdec_ref, cdec_ref, dk_ref, dv_ref, gstate):
        @pl.when(pl.program_id(0) == 0)
        def _():
            gstate[...] = jnp.zeros_like(gstate)

        for h in range(HEADS):
            cols = _head_cols(h)
            q, k, v, do = q_ref[:, cols], k_ref[:, cols], v_ref[:, cols], do_ref[:, cols]
            mask = mask_ref[h]
            scores = (_dot_nt(q, k) * mask).astype(BF16)
            dscores = (_dot_nt(do, v) * mask).astype(BF16)
            gs = gstate[h].astype(BF16)
            dk_ref[:, cols] = _dot_tn(dscores, q) + _dot_nt(v, gs) * kdec_ref[h]
            dv_ref[:, cols] = _dot_tn(scores, do) + _dot((k.astype(F32) * kdec_ref[h]).astype(BF16), gs)
            gstate[h] = gstate[h] * cdec_ref[h] + _dot_tn((q.astype(F32) * qdec_ref[h]).astype(BF16), do)

    rev = lambda: pl.BlockSpec((RET_BLOCK, D_MODEL), lambda i: (nb - 1 - i, 0))
    return _call(
        body, name="retention_bwd_kv", grid=(nb,),
        in_specs=[_seg_block_spec(0, True), _seg_block_spec(1, True), _seg_block_spec(2, True), rev()] + _table_specs(),
        out_specs=[rev(), rev()],
        out_shape=[jax.ShapeDtypeStruct((SEQ, D_MODEL), F32)] * 2,
        scratch_shapes=[pltpu.VMEM((HEADS, HEAD_DIM, HEAD_DIM), F32)],
        vmem_bytes=32 * RET_BLOCK * D_MODEL * 4, args=[qkvg, qkvg, qkvg, dret, *tables], ride=ride)


def _pooled(p_ext, first_row):
    rows = p_ext.shape[0]
    t = first_row + lax.broadcasted_iota(jnp.int32, (rows - HALO, 1), 0)
    outs = []
    for g, w in enumerate(POOL_WINDOWS):
        e = p_ext[:, g * POOL_GROUP_DIM:(g + 1) * POOL_GROUP_DIM]
        s, span = e, 1
        while span < w:
            s = s + pltpu.roll(s, span, 0)
            span *= 2
        count = jnp.minimum(t + 1, w).astype(F32)
        outs.append(s[HALO:] / count - e[HALO:])
    return outs


def _pooled_transpose(d_ext, first_row):
    rows = d_ext.shape[0]
    t = first_row + lax.broadcasted_iota(jnp.int32, (rows, 1), 0)
    outs = []
    for g, w in enumerate(POOL_WINDOWS):
        d = d_ext[:, g * POOL_GROUP_DIM:(g + 1) * POOL_GROUP_DIM]
        e = jnp.where(t < SEQ, d / jnp.minimum(t + 1, w).astype(F32), 0.0)
        s, span = e, 1
        while span < w:
            s = s + pltpu.roll(s, rows - span, 0)
            span *= 2
        outs.append(s[:rows - HALO] - d[:rows - HALO])
    return outs


def _mix_tail_specs(tm):
    halo_blocks = tm // HALO
    return [
        _row_spec(tm, D_MODEL),
        pl.BlockSpec((HALO, D_MODEL), lambda i: (jnp.maximum(i * halo_blocks - 1, 0), 0)),
        _row_spec(tm, 2 * D_MODEL),
        _row_spec(tm, D_MODEL),
        _full_spec((2, D_MODEL)), _full_spec((1, D_MODEL)), ANY,
        _full_spec((D_MODEL, D_MODEL)), _full_spec((D_MODEL, D_MODEL)), _full_spec((D_MODEL, D_MODEL)),
    ]


def _mix_tail_compute(i, tm, p_ref, halo_ref, gates_ref, oret_ref, bias_ref, scale_ref, pw, wru_ref, wpu_ref):
    halo = jnp.where(i > 0, halo_ref[...], 0.0)
    pooled = _pooled(jnp.concatenate([halo, p_ref[...]], axis=0), i * tm)
    pooled = [x.astype(BF16) for x in pooled]
    mixed = jnp.concatenate([_dot(pooled[g], pw[g]) for g in range(len(POOL_WINDOWS))], axis=-1)
    pool_out = (mixed * scale_ref[...]).astype(BF16)
    o_ret = oret_ref[...]
    a = _dot(o_ret, wru_ref[...])
    b = _dot(pool_out, wpu_ref[...])
    z = gates_ref[...].astype(F32)
    g0 = _sig(z[:, :D_MODEL] + bias_ref[0:1, :])
    g1 = _sig(z[:, D_MODEL:] + bias_ref[1:2, :])
    merged = (g0 * a + g1 * b).astype(BF16)
    return pooled, mixed, pool_out, o_ret, a, b, g0, g1, merged


def _mix_tail_forward(p, gates, o_ret, h1, bias, scale, pw8, wru, wpu, wo, ride=None):
    tm, nt = TOKEN_TILE, SEQ // TOKEN_TILE

    def body(p_ref, halo_ref, gates_ref, oret_ref, bias_ref, scale_ref, pw_hbm, wru_ref, wpu_ref, wo_ref, h1_ref,
             h2_ref, pw, sem):
        i = pl.program_id(0)

        @pl.when(i == 0)
        def _():
            _load_pool_weight(pw_hbm, pw, sem)

        merged = _mix_tail_compute(i, tm, p_ref, halo_ref, gates_ref, oret_ref, bias_ref, scale_ref, pw, wru_ref, wpu_ref)[-1]
        h2_ref[...] = h1_ref[...] + _dot(merged, wo_ref[...])

    est = 3 * 2 * 2 * D_MODEL * D_MODEL + 2 * tm * D_MODEL * (4 + 4 + 2 + 4 + 4) + 16 * tm * D_MODEL * 4
    return _call(
        body, name="mix_tail_fwd", grid=(nt,),
        in_specs=_mix_tail_specs(tm) + [_row_spec(tm, D_MODEL)],
        out_specs=[_row_spec(tm, D_MODEL)], out_shape=[jax.ShapeDtypeStruct((SEQ, D_MODEL), F32)],
        scratch_shapes=[pltpu.VMEM((len(POOL_WINDOWS), POOL_GROUP_DIM, POOL_GROUP_DIM), BF16), pltpu.SemaphoreType.DMA((N_DEV,))],
        vmem_bytes=est, args=[p, p, gates, o_ret, bias, scale, pw8, wru, wpu, wo, h1], ride=ride)


def _mix_tail_backward(dh2, p, gates, o_ret, ret, qkvg, bias, scale, pw8, wru, wpu, wo, ride=None):
    tm, nt = TOKEN_TILE, SEQ // TOKEN_TILE
    n_groups = len(POOL_WINDOWS)
    rows_per_dev = POOL_GROUP_DIM // N_DEV

    def body(p_ref, halo_ref, gates_ref, oret_ref, bias_ref, scale_ref, pw_hbm, wru_ref, wpu_ref, wo_ref,
             dh2_ref, ret_ref, gr_ref,
             dret_ref, dgr_ref, dgates_ref, dpooled_ref, dwo_ref, dwru_ref, dwpu_ref, dpw_ref, dbias_ref, dscale_ref,
             pw, sem, acc_wo, acc_wru, acc_wpu, acc_pw):
        i = pl.program_id(0)

        @pl.when(i == 0)
        def _():
            _load_pool_weight(pw_hbm, pw, sem)
            for ref in (acc_wo, acc_wru, acc_wpu, acc_pw, dbias_ref, dscale_ref):
                ref[...] = jnp.zeros_like(ref)

        pooled, mixed, pool_out, o_ret, a, b, g0, g1, merged = _mix_tail_compute(
            i, tm, p_ref, halo_ref, gates_ref, oret_ref, bias_ref, scale_ref, pw, wru_ref, wpu_ref)
        dh2 = dh2_ref[...].astype(BF16)
        dm = _dot_nt(dh2, wo_ref[...])
        acc_wo[...] += _dot_tn(merged, dh2)
        da = (dm * g0).astype(BF16)
        db = (dm * g1).astype(BF16)
        dz0 = dm * a * g0 * (1.0 - g0)
        dz1 = dm * b * g1 * (1.0 - g1)
        dbias_ref[0:1, :] += jnp.sum(dz0, axis=0, keepdims=True)
        dbias_ref[1:2, :] += jnp.sum(dz1, axis=0, keepdims=True)
        dgates_ref[:, pl.ds(0, D_MODEL)] = dz0.astype(BF16)
        dgates_ref[:, pl.ds(D_MODEL, D_MODEL)] = dz1.astype(BF16)
        acc_wru[...] += _dot_tn(o_ret, da)
        acc_wpu[...] += _dot_tn(pool_out, db)
        d_oret = _dot_nt(da, wru_ref[...])
        d_pool_out = _dot_nt(db, wpu_ref[...])
        dscale_ref[...] += jnp.sum(d_pool_out * mixed, axis=0, keepdims=True)
        dmixed = (d_pool_out * scale_ref[...]).astype(BF16)
        for g in range(n_groups):
            dmg = dmixed[:, g * POOL_GROUP_DIM:(g + 1) * POOL_GROUP_DIM]
            acc_pw[g] += _dot_tn(pooled[g], dmg)
            dpooled_ref[:, pl.ds(g * POOL_GROUP_DIM, POOL_GROUP_DIM)] = _dot_nt(dmg, pw[g])
        gr = gr_ref[...].astype(F32)
        s = _sig(gr)
        silu = gr * s
        for hd in range(HEADS):
            cols = slice(hd * HEAD_DIM, (hd + 1) * HEAD_DIM)
            r_h = ret_ref[:, cols]
            rr = lax.rsqrt(jnp.mean(r_h * r_h, axis=-1, keepdims=True) + NORM_EPS)
            rhat = r_h * rr
            do_h = d_oret[:, cols]
            dgr_ref[:, cols] = (do_h * rhat * (s[:, cols] * (1.0 + gr[:, cols] * (1.0 - s[:, cols])))).astype(BF16)
            dret_ref[:, cols] = _rms_bwd(do_h * silu[:, cols], rhat, rr).astype(BF16)

        @pl.when(i == nt - 1)
        def _():
            rows = D_MODEL // N_DEV
            for d in range(N_DEV):
                dwo_ref[d] = acc_wo[pl.ds(d * rows, rows), :].astype(BF16)
                dwru_ref[d] = acc_wru[pl.ds(d * rows, rows), :].astype(BF16)
                dwpu_ref[d] = acc_wpu[pl.ds(d * rows, rows), :].astype(BF16)
                dpw_ref[d] = acc_pw[:, pl.ds(d * rows_per_dev, rows_per_dev), :].astype(BF16)

    sq = (N_DEV, D_MODEL // N_DEV, D_MODEL)
    pw_shape = (N_DEV, n_groups, rows_per_dev, POOL_GROUP_DIM)
    est = (3 * 2 * 2 * D_MODEL * D_MODEL + 3 * 4 * D_MODEL * D_MODEL + 3 * 2 * 2 * D_MODEL * D_MODEL
           + 2 * tm * D_MODEL * (4 + 4 + 2 + 4 + 4 + 2 + 2 + 2 + 4 + 4) + 24 * tm * D_MODEL * 4)
    return _call(
        body, name="mix_tail_bwd", grid=(nt,),
        in_specs=_mix_tail_specs(tm) + [_row_spec(tm, D_MODEL), _row_spec(tm, D_MODEL), _row_spec(tm, D_MODEL, 3)],
        out_specs=[_row_spec(tm, D_MODEL), _row_spec(tm, D_MODEL), _row_spec(tm, 2 * D_MODEL), _row_spec(tm, D_MODEL),
                   _full_spec(sq), _full_spec(sq), _full_spec(sq), _full_spec(pw_shape),
                   _full_spec((2, D_MODEL)), _full_spec((1, D_MODEL))],
        out_shape=[jax.ShapeDtypeStruct((SEQ, D_MODEL), BF16), jax.ShapeDtypeStruct((SEQ, D_MODEL), BF16),
                   jax.ShapeDtypeStruct((SEQ, 2 * D_MODEL), BF16), jax.ShapeDtypeStruct((SEQ, D_MODEL), F32),
                   jax.ShapeDtypeStruct(sq, BF16), jax.ShapeDtypeStruct(sq, BF16), jax.ShapeDtypeStruct(sq, BF16),
                   jax.ShapeDtypeStruct(pw_shape, BF16),
                   jax.ShapeDtypeStruct((2, D_MODEL), F32), jax.ShapeDtypeStruct((1, D_MODEL), F32)],
        scratch_shapes=[pltpu.VMEM((n_groups, POOL_GROUP_DIM, POOL_GROUP_DIM), BF16), pltpu.SemaphoreType.DMA((N_DEV,)),
                        pltpu.VMEM((D_MODEL, D_MODEL), F32), pltpu.VMEM((D_MODEL, D_MODEL), F32),
                        pltpu.VMEM((D_MODEL, D_MODEL), F32), pltpu.VMEM((n_groups, POOL_GROUP_DIM, POOL_GROUP_DIM), F32)],
        vmem_bytes=est, args=[p, p, gates, o_ret, bias, scale, pw8, wru, wpu, wo, dh2, ret, qkvg], ride=ride)


def _mix_proj_backward(dq, dk, dv, dgr, dpooled, dgates, cos, sin, h1, gain, dh2, wmix8, ride=None):
    tm, nt = TOKEN_TILE, SEQ // TOKEN_TILE
    halo_blocks = tm // HALO
    last_halo = SEQ // HALO - 1
    k_scale = HEAD_DIM ** -0.5

    def body(dq_ref, dk_ref, dv_ref, dgr_ref, dpool_ref, dhalo_ref, dgates_ref, cos_ref, sin_ref, h1_ref, g_ref,
             dh2_ref, wmix_hbm, dh1_ref, dproj_ref, dg_ref, wmix, sem):
        i = pl.program_id(0)

        @pl.when(i == 0)
        def _():
            _load_mix_weight(wmix_hbm, wmix, sem)
            dg_ref[...] = jnp.zeros_like(dg_ref)

        cos_t, sin_t = cos_ref[...], sin_ref[...]
        for seg, ref, scale in ((0, dq_ref, 1.0), (1, dk_ref, k_scale)):
            for hd in range(HEADS):
                lo = hd * HEAD_DIM
                d1, d2 = ref[:, lo:lo + ROT_HALF], ref[:, lo + ROT_HALF:lo + HEAD_DIM]
                dproj_ref[:, pl.ds(seg * D_MODEL + lo, ROT_HALF)] = ((d1 * cos_t + d2 * sin_t) * scale).astype(BF16)
                dproj_ref[:, pl.ds(seg * D_MODEL + lo + ROT_HALF, ROT_HALF)] = ((d2 * cos_t - d1 * sin_t) * scale).astype(BF16)
        dproj_ref[:, pl.ds(2 * D_MODEL, D_MODEL)] = dv_ref[...].astype(BF16)
        dproj_ref[:, pl.ds(3 * D_MODEL, D_MODEL)] = dgr_ref[...]
        dp = _pooled_transpose(jnp.concatenate([dpool_ref[...], dhalo_ref[...]], axis=0), i * tm)
        for g in range(len(POOL_WINDOWS)):
            dproj_ref[:, pl.ds(4 * D_MODEL + g * POOL_GROUP_DIM, POOL_GROUP_DIM)] = dp[g].astype(BF16)
        dproj_ref[:, pl.ds(5 * D_MODEL, 2 * D_MODEL)] = dgates_ref[...]
        du = jnp.zeros((tm, D_MODEL), F32)
        for seg in range(N_SEG):
            cols = pl.ds(seg * D_MODEL, D_MODEL)
            du = du + _dot_nt(dproj_ref[:, cols], wmix[:, cols])
        g = g_ref[...]
        _, xhat, r = _rms(h1_ref[...], g)
        dg_ref[...] += jnp.sum(du * xhat, axis=0, keepdims=True)
        dh1_ref[...] = dh2_ref[...] + _rms_bwd(du * g, xhat, r)

    est = 2 * D_MODEL * N_SEG * D_MODEL + 2 * tm * D_MODEL * (3 * 4 + 2 + 4 + 4 + 4 + 4 + 4 + 14) + 12 * tm * D_MODEL * 4
    return _call(
        body, name="mix_proj_bwd", grid=(nt,),
        in_specs=[_row_spec(tm, D_MODEL), _row_spec(tm, D_MODEL), _row_spec(tm, D_MODEL), _row_spec(tm, D_MODEL),
                  _row_spec(tm, D_MODEL),
                  pl.BlockSpec((HALO, D_MODEL), lambda i: (jnp.minimum((i + 1) * halo_blocks, last_halo), 0)),
                  _row_spec(tm, 2 * D_MODEL), _row_spec(tm, ROT_HALF), _row_spec(tm, ROT_HALF),
                  _row_spec(tm, D_MODEL), _full_spec((1, D_MODEL)), _row_spec(tm, D_MODEL), ANY],
        out_specs=[_row_spec(tm, D_MODEL), _row_spec(tm, N_SEG * D_MODEL), _full_spec((1, D_MODEL))],
        out_shape=[jax.ShapeDtypeStruct((SEQ, D_MODEL), F32), jax.ShapeDtypeStruct((SEQ, N_SEG * D_MODEL), BF16),
                   jax.ShapeDtypeStruct((1, D_MODEL), F32)],
        scratch_shapes=[pltpu.VMEM((D_MODEL, N_SEG * D_MODEL), BF16), pltpu.SemaphoreType.DMA((N_DEV,))],
        vmem_bytes=est, args=[dq, dk, dv, dgr, dpooled, dpooled, dgates, cos, sin, h1, gain, dh2, wmix8], ride=ride)


def _adamw(w, parts, m, v, name):
    rows, cols = w.shape
    n_parts = parts.shape[0]
    tr = max([t for t in range(16, 257, 16) if rows % t == 0], default=rows)
    c1 = 1.0 - ADAM_B1 ** ADAM_STEP
    c2 = 1.0 - ADAM_B2 ** ADAM_STEP

    def body(w_ref, p_ref, m_ref, v_ref, g_out, d_out, m_out, v_out):
        g = p_ref[0].astype(F32)
        for k in range(1, n_parts):
            g = g + p_ref[k].astype(F32)
        m_new = ADAM_B1 * m_ref[...] + (1.0 - ADAM_B1) * g
        v_new = ADAM_B2 * v_ref[...] + (1.0 - ADAM_B2) * (g * g)
        g_out[...] = g
        m_out[...] = m_new
        v_out[...] = v_new
        d_out[...] = -ADAM_LR * ((m_new / c1) / (jnp.sqrt(v_new / c2) + ADAM_EPS) + ADAM_WD * w_ref[...])

    spec = pl.BlockSpec((tr, cols), lambda i: (i, 0))
    out = jax.ShapeDtypeStruct((rows, cols), F32)
    return pl.pallas_call(
        body, name=name, grid=(rows // tr,),
        in_specs=[spec, pl.BlockSpec((n_parts, tr, cols), lambda i: (0, i, 0)), spec, spec],
        out_specs=[spec] * 4, out_shape=[out] * 4,
        compiler_params=_params(2 * tr * cols * (7 * 4 + n_parts * parts.dtype.itemsize) + 8 * tr * cols * 4, 1),
    )(_in_hbm(w), _in_hbm(parts), _in_hbm(m), _in_hbm(v))


def _mix_w_in_grad(u, dproj, ride=None):
    return _weight_grad(
        u, dproj, N_DEV,
        lambda tt: pl.BlockSpec((tt, D_MODEL), lambda b, t: (t, 0)),
        lambda tt: pl.BlockSpec((tt, MIX_SHARD), lambda b, t: (t, b)),
        D_MODEL, MIX_SHARD, name="w_in_grad", ride=ride)


def kernel(x, norm_ffn1, ffn1_w_in, ffn1_w_out, norm_mix, w_in, gate_bias, pool_w, pool_scale, w_ret_up, w_pool_up, w_out, norm_ffn2, ffn2_w_in, ffn2_w_out, norm_final, loss_target, m_norm_ffn1, m_ffn1_w_in, m_ffn1_w_out, m_norm_mix, m_w_in, m_gate_bias, m_pool_w, m_pool_scale, m_w_ret_up, m_w_pool_up, m_w_out, m_norm_ffn2, m_ffn2_w_in, m_ffn2_w_out, m_norm_final, v_norm_ffn1, v_ffn1_w_in, v_ffn1_w_out, v_norm_mix, v_w_in, v_gate_bias, v_pool_w, v_pool_scale, v_w_ret_up, v_w_pool_up, v_w_out, v_norm_ffn2, v_ffn2_w_in, v_ffn2_w_out, v_norm_final):
    assert x.shape == (1, SEQ, D_MODEL) and ffn1_w_in.shape == (1, D_MODEL, FF_SHARD) and w_in.shape == (1, D_MODEL, MIX_SHARD)
    x2, target = x[0], loss_target[0]

    cos, sin = _rotary_tables()
    tables = _retention_tables()
    big = [ffn1_w_in, ffn1_w_out, w_in, pool_w, w_ret_up, w_pool_up, w_out, ffn2_w_in, ffn2_w_out]
    bf = lambda w: w[0].astype(BF16)
    bf_t = lambda w: jnp.swapaxes(w[0], 0, 1).astype(BF16)
    square = lambda w: w.reshape(D_MODEL, D_MODEL)

    win1, wout1, bias8 = _alone(_GatherRide([bf_t(ffn1_w_in), bf(ffn1_w_out), gate_bias[0]]), "ffn1_weights_all_gather")
    wout1 = wout1.reshape(N_FF_GROUPS, FF_SHARD, D_MODEL)
    bias = bias8.transpose(1, 0, 2).reshape(2, D_MODEL)

    (h1, gu1), (wmix8,) = _ffn_forward(x2, norm_ffn1, win1, wout1, "ffn1_fwd", ride=_GatherRide([bf(w_in)]))
    (u, qkvg, p, gates), (win2,) = _mix_proj_forward(h1, norm_mix, wmix8, cos, sin, ride=_GatherRide([bf_t(ffn2_w_in)]))
    (ret, o_ret), (pw8, wru, wpu, wo) = _retention_forward(
        qkvg, tables, ride=_GatherRide([bf(pool_w), bf(w_ret_up), bf(w_pool_up), bf(w_out)]))
    wru, wpu, wo = square(wru), square(wpu), square(wo)
    (h2,), (wout2,) = _mix_tail_forward(p, gates, o_ret, h1, bias, pool_scale, pw8, wru, wpu, wo,
                                        ride=_GatherRide([bf(ffn2_w_out)]))
    wout2 = wout2.reshape(N_FF_GROUPS, FF_SHARD, D_MODEL)
    (dh3, gu2, loss_part, d_norm_final), _ = _ffn_forward(h2, norm_ffn2, win2, wout2, "ffn2_fwd_loss",
                                                          head=(target, norm_final.reshape(1, D_MODEL)))

    (dh2, dgu2, act2, xn2, df2, d_norm_ffn2), _ = _ffn_backward(dh3, h2, norm_ffn2, gu2, win2, wout2, "ffn2_bwd")
    d_wout2, _ = _ffn_w_out_grad(act2, df2, 2)
    d_win2, (r_wout2,) = _ffn_w_in_grad(xn2, dgu2, 2, ride=_ScatterRide([d_wout2]))
    (dret, dgr, dgates, dpooled, d_wo, d_wru, d_wpu, d_pw, d_bias, d_scale), (r_win2,) = _mix_tail_backward(
        dh2, p, gates, o_ret, ret, qkvg, bias, pool_scale, pw8, wru, wpu, wo, ride=_ScatterRide([d_win2]))
    (dq,), _ = _retention_backward_q(qkvg, dret, tables)
    (dk, dv), (r_pw, r_wru, r_wpu, r_wo) = _retention_backward_kv(
        qkvg, dret, tables, ride=_ScatterRide([d_pw, d_wru, d_wpu, d_wo]))
    (dh1, dproj, d_norm_mix), _ = _mix_proj_backward(dq, dk, dv, dgr, dpooled, dgates, cos, sin, h1, norm_mix, dh2, wmix8)
    d_wmix, _ = _mix_w_in_grad(u, dproj)
    (grad_x, dgu1, act1, xn1, df1, d_norm_ffn1), (r_wmix,) = _ffn_backward(
        dh1, x2, norm_ffn1, gu1, win1, wout1, "ffn1_bwd", ride=_ScatterRide([d_wmix]))
    d_wout1, _ = _ffn_w_out_grad(act1, df1, 1)
    d_win1, (r_wout1,) = _ffn_w_in_grad(xn1, dgu1, 1, ride=_ScatterRide([d_wout1]))
    (r_win1,) = _alone(_ScatterRide([d_win1]), "ffn1_w_in_grad_reduce_scatter")
    received = [r_win1, r_wout1, r_wmix, r_pw, r_wru, r_wpu, r_wo, r_win2, r_wout2]
    zero_row = jnp.zeros((1, D_MODEL), F32)
    small = _all_reduce_rows(jnp.concatenate(
        [d_norm_ffn1, d_norm_mix, d_scale, d_norm_ffn2, d_norm_final, d_bias, zero_row], axis=0))
    loss = lax.psum(loss_part[0, 0], ("x", "y", "c"))

    names = ["ffn1_w_in", "ffn1_w_out", "w_in", "pool_w", "w_ret_up", "w_pool_up", "w_out", "ffn2_w_in", "ffn2_w_out"]
    moments_m = [m_ffn1_w_in, m_ffn1_w_out, m_w_in, m_pool_w, m_w_ret_up, m_w_pool_up, m_w_out, m_ffn2_w_in, m_ffn2_w_out]
    moments_v = [v_ffn1_w_in, v_ffn1_w_out, v_w_in, v_pool_w, v_w_ret_up, v_w_pool_up, v_w_out, v_ffn2_w_in, v_ffn2_w_out]
    results = {}
    for nm, w, parts, m, v in zip(names, big, received, moments_m, moments_v):
        if nm in ("ffn1_w_in", "ffn2_w_in"):
            flat, back = (lambda a: jnp.swapaxes(a[0], 0, 1)), (lambda o: jnp.swapaxes(o, 0, 1)[None])
        else:
            flat, back = (lambda a, w=w: a.reshape(-1, w.shape[-1])), (lambda o, w=w: o.reshape(w.shape))
        outs = _adamw(flat(w), parts.reshape((N_DEV,) + flat(w).shape), flat(m), flat(v), name=f"adamw_{nm}")
        results[nm] = [back(o) for o in outs]

    my_id = _linear_id(*_my_position())
    bias_cols = gate_bias.shape[-1]
    pad = lambda a: jnp.pad(a[0], ((0, 0), (0, D_MODEL - bias_cols)))
    pack = lambda a, b, c, d, e, gb: jnp.concatenate([a, b, c, d, e.reshape(1, D_MODEL), pad(gb), zero_row], axis=0)
    d_bias_mine = lax.dynamic_slice_in_dim(small[5:7], my_id * bias_cols, bias_cols, axis=1)
    g_small = jnp.concatenate([small[0:5], jnp.pad(d_bias_mine, ((0, 0), (0, D_MODEL - bias_cols))), zero_row], axis=0)
    s_outs = _adamw(pack(norm_ffn1, norm_mix, pool_scale, norm_ffn2, norm_final, gate_bias), g_small[None],
                    pack(m_norm_ffn1, m_norm_mix, m_pool_scale, m_norm_ffn2, m_norm_final, m_gate_bias),
                    pack(v_norm_ffn1, v_norm_mix, v_pool_scale, v_norm_ffn2, v_norm_final, v_gate_bias), name="adamw_small")
    for row, nm in enumerate(["norm_ffn1", "norm_mix", "pool_scale", "norm_ffn2"]):
        results[nm] = [o[row:row + 1] for o in s_outs]
    results["norm_final"] = [o[4] for o in s_outs]
    results["gate_bias"] = [o[5:7, :bias_cols][None] for o in s_outs]

    order = ["norm_ffn1", "ffn1_w_in", "ffn1_w_out", "norm_mix", "w_in", "gate_bias", "pool_w", "pool_scale",
             "w_ret_up", "w_pool_up", "w_out", "norm_ffn2", "ffn2_w_in", "ffn2_w_out", "norm_final"]
    return (loss, grad_x[None], *[results[nm][0] for nm in order], *[results[nm][1] for nm in order],
            *[results[nm][2] for nm in order], *[results[nm][3] for nm in order])
```

```python
import functools

import numpy as np
import jax
import jax.numpy as jnp
from jax import lax
from jax.experimental import pallas as pl
from jax.experimental.pallas import tpu as pltpu

F32 = jnp.float32
BF16 = jnp.bfloat16

N_DEV = 8
D_MODEL = 1024
SEQ = 4096
D_FF = 2816
FF_SHARD = 2 * D_FF // N_DEV
N_FF_GROUPS = N_DEV // 2
HEADS = 4
HEAD_DIM = 256
ROT_HALF = HEAD_DIM // 2
CHUNK = 64
RET_BLOCK = 256
POOL_WINDOWS = (2, 4, 8, 16)
POOL_GROUP_DIM = 256
HALO = 16
MIX_SHARD = 7 * D_MODEL // N_DEV
N_SEG = 7
ROPE_BASE = 10000.0
NORM_EPS = 1e-6
FFN_RES_WEIGHT = 0.5
ADAM_LR, ADAM_B1, ADAM_B2, ADAM_EPS, ADAM_WD, ADAM_STEP = 0.001, 0.9, 0.999, 1e-08, 0.01, 10

TOKEN_TILE = 256
WIDE_TILE = 512
VMEM_CAP_V7X = 64 * 1024 * 1024
MESH = pl.DeviceIdType.MESH
ANY = pl.BlockSpec(memory_space=pl.ANY)


def _vmem_limit(estimate_bytes):
    return int(min(estimate_bytes * 5 // 4 + (6 << 20), VMEM_CAP_V7X - (4 << 20)))


def _params(estimate_bytes, n_grid):
    return pltpu.CompilerParams(dimension_semantics=("arbitrary",) * n_grid,
                                vmem_limit_bytes=_vmem_limit(estimate_bytes))


def _dot(a, b):
    return jnp.dot(a, b, preferred_element_type=F32)


def _dot_nt(a, b):
    return lax.dot_general(a, b, (((1,), (1,)), ((), ())), preferred_element_type=F32)


def _dot_tn(a, b):
    return lax.dot_general(a, b, (((0,), (0,)), ((), ())), preferred_element_type=F32)


def _sig(x):
    return 1.0 / (1.0 + jnp.exp(-x))


def _rms(x, g):
    r = lax.rsqrt(jnp.mean(x * x, axis=-1, keepdims=True) + NORM_EPS)
    xhat = x * r
    return xhat * g, xhat, r


def _rms_bwd(dyg, xhat, r):
    return r * (dyg - xhat * jnp.mean(dyg * xhat, axis=-1, keepdims=True))


def _row_spec(tile, width, col=0):
    return pl.BlockSpec((tile, width), lambda i, c=col: (i, c))


def _full_spec(shape):
    return pl.BlockSpec(shape, lambda *_: (0,) * len(shape))


def _rotary_tables():
    inv_freq = (np.float32(ROPE_BASE) ** (-np.arange(ROT_HALF, dtype=np.float32) / np.float32(ROT_HALF))).astype(np.float32)
    ang = (np.arange(SEQ, dtype=np.float32)[:, None] * inv_freq[None, :]).astype(np.float32)
    return jnp.asarray(np.cos(ang.astype(np.float64)), F32), jnp.asarray(np.sin(ang.astype(np.float64)), F32)


def _retention_tables():
    log_gamma = np.log(1.0 - 2.0 ** (-5.0 - np.arange(HEADS, dtype=np.float64)))
    n = np.arange(RET_BLOCK)
    diff = (n[:, None] - n[None, :]).astype(np.float64)
    same = (n[:, None] // CHUNK) == (n[None, :] // CHUNK)
    earlier = (n[None, :] // CHUNK) < (n[:, None] // CHUNK)
    expo = np.where(same, np.abs(diff), diff)
    mask = np.where(same | earlier, np.exp(log_gamma[:, None, None] * expo[None]), 0.0)
    qdec = np.exp(log_gamma[:, None] * (n[None, :] + 1.0))[:, :, None]
    kdec = np.exp(log_gamma[:, None] * (RET_BLOCK - 1.0 - n[None, :]))[:, :, None]
    cdec = np.exp(log_gamma * RET_BLOCK)[:, None, None]
    return (jnp.asarray(mask, F32), jnp.asarray(qdec, F32), jnp.asarray(kdec, F32), jnp.asarray(cdec, F32))


def _my_position():
    return lax.axis_index("x"), lax.axis_index("y"), lax.axis_index("c")


def _linear_id(px, py, pc):
    return 4 * px + 2 * py + pc


def _when(pred, fn):
    if isinstance(pred, bool):
        if pred:
            fn()
    else:
        pl.when(pred)(fn)


class _GatherRide:
    def __init__(self, shards):
        self.args = list(shards)
        n = self.n = len(shards)
        self.out_shape = [pltpu.HBM((N_DEV,) + s.shape, s.dtype) for s in shards]
        self.scratch = [pltpu.SemaphoreType.DMA((n, 7)), pltpu.SemaphoreType.DMA((n, 7)), pltpu.SemaphoreType.DMA((n,))]

    def _plan(self, src, out, sems):
        send_sems, recv_sems, local_sem = sems
        x, y, c = _my_position()
        me, sibling = (x, y, c), (x, y, 1 - c)
        chips = [(1 - x, y), (x, 1 - y), (1 - x, 1 - y)]

        def copy(t, k, block, to, from_src=False):
            rows = out[t].at[_linear_id(*block)]
            return pltpu.make_async_remote_copy(
                src_ref=src[t] if from_src else rows, dst_ref=rows,
                send_sem=send_sems.at[t, k], recv_sem=recv_sems.at[t, k],
                device_id=to, device_id_type=MESH)

        local = [pltpu.make_async_copy(src[t], out[t].at[_linear_id(*me)], local_sem.at[t]) for t in range(self.n)]
        return copy, local, me, sibling, chips, c

    def begin(self, first, src, out, sems):
        copy, local, me, sibling, chips, c = self._plan(src, out, sems)

        def start():
            for cp in local:
                cp.start()
            for t in range(self.n):
                copy(t, 0, me, sibling, from_src=True).start()
                for j, chip in enumerate(chips):
                    copy(t, 1 + j, me, (*chip, c), from_src=True).start()

        _when(first, start)

    def finish(self, mid, last, src, out, sems):
        copy, local, me, sibling, chips, c = self._plan(src, out, sems)

        def pass_on():
            for j, chip in enumerate(chips):
                for t in range(self.n):
                    copy(t, 1 + j, (*chip, c), me).wait_recv()
                    copy(t, 4 + j, (*chip, c), sibling).start()

        def drain():
            for t in range(self.n):
                copy(t, 0, sibling, me).wait_recv()
                for j, chip in enumerate(chips):
                    copy(t, 4 + j, (*chip, 1 - c), me).wait_recv()
            for t in range(self.n):
                copy(t, 0, me, sibling, from_src=True).wait_send()
                for j, chip in enumerate(chips):
                    copy(t, 1 + j, me, (*chip, c), from_src=True).wait_send()
                    copy(t, 4 + j, (*chip, c), sibling).wait_send()
            for cp in local:
                cp.wait()

        _when(mid, pass_on)
        _when(last, drain)


class _ScatterRide:
    def __init__(self, chip_sums):
        self.args = list(chip_sums)
        n = self.n = len(chip_sums)
        self.out_shape = [pltpu.HBM(p.shape, p.dtype) for p in chip_sums]
        self.scratch = [pltpu.SemaphoreType.DMA((n, 3)), pltpu.SemaphoreType.DMA((n, 3)), pltpu.SemaphoreType.DMA((n,))]

    def _plan(self, src, out, sems):
        send_sems, recv_sems, local_sem = sems
        x, y, c = _my_position()

        def peer(k):
            return (x ^ (k >> 1), y ^ (k & 1))

        copies = [pltpu.make_async_remote_copy(
            src_ref=src[t].at[2 * peer(k)[0] + peer(k)[1]], dst_ref=out[t].at[k],
            send_sem=send_sems.at[t, k - 1], recv_sem=recv_sems.at[t, k - 1],
            device_id=(*peer(k), c), device_id_type=MESH) for t in range(self.n) for k in range(1, N_DEV // 2)]
        local = [pltpu.make_async_copy(src[t].at[2 * x + y], out[t].at[0], local_sem.at[t]) for t in range(self.n)]
        return copies, local

    def begin(self, first, src, out, sems):
        copies, local = self._plan(src, out, sems)

        def start():
            for cp in local + copies:
                cp.start()

        _when(first, start)

    def finish(self, mid, last, src, out, sems):
        copies, local = self._plan(src, out, sems)

        def drain():
            for cp in copies:
                cp.wait_recv()
            for cp in copies:
                cp.wait_send()
            for cp in local:
                cp.wait()

        _when(last, drain)


def _in_hbm(a):
    return pltpu.with_memory_space_constraint(a, pltpu.HBM)


def _call(body, *, name, grid, in_specs, out_specs, out_shape, scratch_shapes, vmem_bytes, args, ride=None):
    n_in, n_out, n_s = len(in_specs), len(out_specs), len(scratch_shapes)
    params = _params(vmem_bytes, len(grid))
    args = [_in_hbm(a) for a in args]
    out_shape = [pltpu.HBM(s.shape, s.dtype) for s in out_shape]
    if ride is None:
        outs = pl.pallas_call(body, name=name, grid=grid, in_specs=in_specs, out_specs=out_specs, out_shape=out_shape,
                              scratch_shapes=scratch_shapes, compiler_params=params)(*args)
        return list(outs), []
    total = int(np.prod(grid))

    def riding_body(*refs):
        a = n_in
        b = a + ride.n
        c = b + n_out
        d = c + ride.n
        e = d + n_s
        step = pl.program_id(0)
        for axis in range(1, len(grid)):
            step = step * grid[axis] + pl.program_id(axis)
        ride.begin(step == 0, refs[a:b], refs[c:d], refs[e:])
        body(*refs[:a], *refs[b:c], *refs[d:e])
        ride.finish(step == (3 * total) // 4, step == total - 1, refs[a:b], refs[c:d], refs[e:])

    outs = pl.pallas_call(
        riding_body, name=name, grid=grid, in_specs=list(in_specs) + [ANY] * ride.n,
        out_specs=list(out_specs) + [ANY] * ride.n, out_shape=list(out_shape) + ride.out_shape,
        scratch_shapes=list(scratch_shapes) + ride.scratch, compiler_params=params)(*args, *[_in_hbm(a) for a in ride.args])
    return list(outs[:n_out]), list(outs[n_out:])


def _alone(ride, name):
    def body(*refs):
        src, out, sems = refs[:ride.n], refs[ride.n:2 * ride.n], refs[2 * ride.n:]
        ride.begin(True, src, out, sems)
        ride.finish(True, True, src, out, sems)

    return list(pl.pallas_call(body, name=name, out_shape=ride.out_shape, in_specs=[ANY] * ride.n,
                               out_specs=[ANY] * ride.n, scratch_shapes=ride.scratch)(*[_in_hbm(a) for a in ride.args]))


def _all_reduce_rows(block):
    rows, width = block.shape

    def body(x_ref, sum_ref, gathered, send_sems, recv_sems, local_sem):
        x, y, c = _my_position()
        me, sibling = (x, y, c), (x, y, 1 - c)
        chips = [(1 - x, y), (x, 1 - y), (1 - x, 1 - y)]

        def slot(px, py, pc):
            return gathered.at[_linear_id(px, py, pc)]

        def copy(k, block_of, to, from_src=False):
            return pltpu.make_async_remote_copy(
                src_ref=x_ref if from_src else slot(*block_of), dst_ref=slot(*block_of),
                send_sem=send_sems.at[k], recv_sem=recv_sems.at[k], device_id=to, device_id_type=MESH)

        mine = pltpu.make_async_copy(x_ref, slot(*me), local_sem)
        mine.start()
        first = [copy(0, me, sibling, from_src=True)]
        first += [copy(1 + j, me, (*chip, c), from_src=True) for j, chip in enumerate(chips)]
        for cp in first:
            cp.start()
        passed = [copy(4 + j, (*chip, c), sibling) for j, chip in enumerate(chips)]
        for j, chip in enumerate(chips):
            copy(1 + j, (*chip, c), me).wait_recv()
            passed[j].start()
        copy(0, sibling, me).wait_recv()
        for j, chip in enumerate(chips):
            copy(4 + j, (*chip, 1 - c), me).wait_recv()
        for cp in first + passed:
            cp.wait_send()
        mine.wait()
        total = gathered[0]
        for d in range(1, N_DEV):
            total = total + gathered[d]
        sum_ref[...] = total

    return pl.pallas_call(
        body, name="small_grads_all_reduce",
        out_shape=jax.ShapeDtypeStruct((rows, width), F32),
        in_specs=[pl.BlockSpec(memory_space=pltpu.VMEM)],
        out_specs=pl.BlockSpec(memory_space=pltpu.VMEM),
        scratch_shapes=[pltpu.VMEM((N_DEV, rows, width), F32),
                        pltpu.SemaphoreType.DMA((7,)), pltpu.SemaphoreType.DMA((7,)), pltpu.SemaphoreType.DMA],
    )(block)


def _load_ffn_weights(win_hbm, wout_hbm, win, wout, sem):
    a = pltpu.make_async_copy(win_hbm, win, sem.at[0])
    b = pltpu.make_async_copy(wout_hbm, wout, sem.at[1])
    a.start()
    b.start()
    a.wait()
    b.wait()


def _ffn_forward(h_in, gain, win8, wout, name, head=None, ride=None):
    tm, nt = WIDE_TILE, SEQ // WIDE_TILE

    def body(*refs):
        if head is None:
            x_ref, g_ref, win_hbm, wout_hbm, out_ref, gu_ref, win, wout, sem = refs
        else:
            x_ref, g_ref, win_hbm, wout_hbm, tgt_ref, gf_ref, out_ref, gu_ref, loss_ref, dgf_ref, win, wout, sem = refs
        i = pl.program_id(0)

        @pl.when(i == 0)
        def _():
            _load_ffn_weights(win_hbm, wout_hbm, win, wout, sem)
            if head is not None:
                loss_ref[...] = jnp.zeros_like(loss_ref)
                dgf_ref[...] = jnp.zeros_like(dgf_ref)

        x = x_ref[...]
        xn, _, _ = _rms(x, g_ref[...])
        xb = xn.astype(BF16)
        acc = jnp.zeros((tm, D_MODEL), F32)
        for j in range(N_FF_GROUPS):
            gate = _dot_nt(xb, win[j])
            up = _dot_nt(xb, win[j + N_FF_GROUPS])
            gu_ref[j] = gate.astype(BF16)
            gu_ref[j + N_FF_GROUPS] = up.astype(BF16)
            act = gate * _sig(gate) * up
            acc = acc + _dot(act.astype(BF16), wout[j])
        h = x + FFN_RES_WEIGHT * acc
        if head is None:
            out_ref[...] = h
        else:
            gf = gf_ref[...]
            y, hhat, r = _rms(h, gf)
            err = y - tgt_ref[...]
            loss_ref[...] += jnp.full(loss_ref.shape, 0.5 / D_MODEL * jnp.sum(err * err), F32)
            dy = err * (1.0 / D_MODEL)
            dgf_ref[...] += jnp.sum(dy * hhat, axis=0, keepdims=True)
            out_ref[...] = _rms_bwd(dy * gf, hhat, r)

    weights = 2 * D_MODEL * 2 * D_FF + 2 * D_FF * D_MODEL
    tiles = 2 * (2 * 4 * tm * D_MODEL + 2 * tm * 2 * D_FF) + (2 * 4 * tm * D_MODEL if head else 0)
    in_specs = [_row_spec(tm, D_MODEL), _full_spec((1, D_MODEL)), ANY, ANY]
    out_shape = [jax.ShapeDtypeStruct((SEQ, D_MODEL), F32), jax.ShapeDtypeStruct((N_DEV, SEQ, FF_SHARD), BF16)]
    out_specs = [_row_spec(tm, D_MODEL), pl.BlockSpec((N_DEV, tm, FF_SHARD), lambda i: (0, i, 0))]
    args = [h_in, gain, win8, wout]
    if head is not None:
        in_specs += [_row_spec(tm, D_MODEL), _full_spec((1, D_MODEL))]
        out_shape += [jax.ShapeDtypeStruct((1, 128), F32), jax.ShapeDtypeStruct((1, D_MODEL), F32)]
        out_specs += [_full_spec((1, 128)), _full_spec((1, D_MODEL))]
        args += list(head)
    return _call(
        body, name=name, grid=(nt,), in_specs=in_specs, out_specs=out_specs, out_shape=out_shape,
        scratch_shapes=[pltpu.VMEM((N_DEV, FF_SHARD, D_MODEL), BF16), pltpu.VMEM((N_FF_GROUPS, FF_SHARD, D_MODEL), BF16),
                        pltpu.SemaphoreType.DMA((2,))],
        vmem_bytes=weights + tiles + 16 * tm * FF_SHARD * 4, args=args, ride=ride)


def _ffn_backward(dh_out, h_in, gain, gu, win8, wout, name, ride=None):
    tm, nt = TOKEN_TILE, SEQ // TOKEN_TILE

    def body(dh_ref, x_ref, g_ref, gu_ref, win_hbm, wout_hbm,
             dhin_ref, dgu_ref, act_ref, xn_ref, df_ref, dg_ref, win, wout, sem):
        i = pl.program_id(0)

        @pl.when(i == 0)
        def _():
            _load_ffn_weights(win_hbm, wout_hbm, win, wout, sem)
            dg_ref[...] = jnp.zeros_like(dg_ref)

        dh = dh_ref[...]
        g = g_ref[...]
        xn, xhat, r = _rms(x_ref[...], g)
        df = (FFN_RES_WEIGHT * dh).astype(BF16)
        dxn = jnp.zeros((tm, D_MODEL), F32)
        for j in range(N_FF_GROUPS):
            gate = gu_ref[j].astype(F32)
            up = gu_ref[j + N_FF_GROUPS].astype(F32)
            dact = _dot_nt(df, wout[j])
            s = _sig(gate)
            silu = gate * s
            dgate = (dact * up * (s * (1.0 + gate * (1.0 - s)))).astype(BF16)
            dup = (dact * silu).astype(BF16)
            act_ref[j] = (silu * up).astype(BF16)
            dgu_ref[j] = dgate
            dgu_ref[j + N_FF_GROUPS] = dup
            dxn = dxn + _dot(dgate, win[j]) + _dot(dup, win[j + N_FF_GROUPS])
        dg_ref[...] += jnp.sum(dxn * xhat, axis=0, keepdims=True)
        dhin_ref[...] = dh + _rms_bwd(dxn * g, xhat, r)
        xn_ref[...] = xn.astype(BF16)
        df_ref[...] = df

    weights = 2 * D_MODEL * 2 * D_FF + 2 * D_FF * D_MODEL
    tiles = 2 * (3 * 4 * tm * D_MODEL + 2 * tm * (2 * 2 * D_FF + D_FF) + 2 * 2 * tm * D_MODEL)
    gu_spec = pl.BlockSpec((N_DEV, tm, FF_SHARD), lambda i: (0, i, 0))
    return _call(
        body, name=name, grid=(nt,),
        in_specs=[_row_spec(tm, D_MODEL), _row_spec(tm, D_MODEL), _full_spec((1, D_MODEL)), gu_spec, ANY, ANY],
        out_specs=[_row_spec(tm, D_MODEL), gu_spec, pl.BlockSpec((N_FF_GROUPS, tm, FF_SHARD), lambda i: (0, i, 0)),
                   _row_spec(tm, D_MODEL), _row_spec(tm, D_MODEL), _full_spec((1, D_MODEL))],
        out_shape=[jax.ShapeDtypeStruct((SEQ, D_MODEL), F32), jax.ShapeDtypeStruct((N_DEV, SEQ, FF_SHARD), BF16),
                   jax.ShapeDtypeStruct((N_FF_GROUPS, SEQ, FF_SHARD), BF16), jax.ShapeDtypeStruct((SEQ, D_MODEL), BF16),
                   jax.ShapeDtypeStruct((SEQ, D_MODEL), BF16), jax.ShapeDtypeStruct((1, D_MODEL), F32)],
        scratch_shapes=[pltpu.VMEM((N_DEV, FF_SHARD, D_MODEL), BF16), pltpu.VMEM((N_FF_GROUPS, FF_SHARD, D_MODEL), BF16),
                        pltpu.SemaphoreType.DMA((2,))],
        vmem_bytes=weights + tiles + 20 * tm * FF_SHARD * 4, args=[dh_out, h_in, gain, gu, win8, wout], ride=ride)


def _to_sibling(src, dst, send_sem, recv_sem):
    x, y, c = _my_position()
    return pltpu.make_async_remote_copy(src_ref=src, dst_ref=dst, send_sem=send_sem, recv_sem=recv_sem,
                                        device_id=(x, y, 1 - c), device_id_type=MESH)


def _weight_grad(x, g, n_out, x_spec, g_spec, k_dim, n_dim, name, halves=False, tt=1024, ride=None):
    nt = SEQ // tt
    n_chips = N_DEV // 2
    rows = k_dim // 2 if halves else k_dim

    def body(x_ref, g_ref, out_ref, acc, sendbuf, recvbuf, send_sems, recv_sems):
        b, t = pl.program_id(0), pl.program_id(1)
        c = lax.axis_index("c")

        def push(q):
            return _to_sibling(sendbuf.at[q], recvbuf.at[q], send_sems.at[q], recv_sems.at[q])

        @pl.when(t == 0)
        def _():
            acc[...] = jnp.zeros_like(acc)

        acc[...] += _dot_tn(x_ref[...], g_ref[...])

        @pl.when(t == nt - 1)
        def _():
            if halves:
                for mine, other in ((0, 1), (1, 0)):
                    @pl.when(c == mine)
                    def _():
                        out_ref[b] = acc[pl.ds(mine * rows, rows), :].astype(BF16)
                        sendbuf[b] = acc[pl.ds(other * rows, rows), :].astype(BF16)
                push(b).start()
            else:
                q = b // 2

                @pl.when(b % 2 == c)
                def _():
                    out_ref[q] = acc[...].astype(BF16)

                @pl.when(b % 2 != c)
                def _():
                    sendbuf[q] = acc[...].astype(BF16)
                    push(q).start()

        @pl.when((b == n_out - 1) & (t == nt - 1))
        def _():
            for q in range(n_chips):
                push(q).wait_recv()
                out_ref[q] = (out_ref[q].astype(F32) + recvbuf[q].astype(F32)).astype(BF16)
            for q in range(n_chips):
                push(q).wait_send()

    piece = (n_chips, rows, n_dim)
    outs, ride_outs = _call(
        body, name=name, grid=(n_out, nt), in_specs=[x_spec(tt), g_spec(tt)],
        out_specs=[pl.BlockSpec(piece, lambda b, t: (0, 0, 0))],
        out_shape=[jax.ShapeDtypeStruct(piece, BF16)],
        scratch_shapes=[pltpu.VMEM((k_dim, n_dim), F32), pltpu.VMEM(piece, BF16), pltpu.VMEM(piece, BF16),
                        pltpu.SemaphoreType.DMA((n_chips,)), pltpu.SemaphoreType.DMA((n_chips,))],
        vmem_bytes=2 * 2 * tt * (k_dim + n_dim) + 8 * k_dim * n_dim + 4 * 2 * n_chips * rows * n_dim, args=[x, g], ride=ride)
    return outs[0], ride_outs


def _ffn_w_out_grad(act, df, tag, ride=None):
    return _weight_grad(
        act, df, N_FF_GROUPS,
        lambda tt: pl.BlockSpec((None, tt, FF_SHARD), lambda b, t: (b, t, 0)),
        lambda tt: pl.BlockSpec((tt, D_MODEL), lambda b, t: (t, 0)),
        FF_SHARD, D_MODEL, name=f"ffn{tag}_w_out_grad", halves=True, ride=ride)


def _ffn_w_in_grad(xn, dgu, tag, ride=None):
    return _weight_grad(
        dgu, xn, N_DEV,
        lambda tt: pl.BlockSpec((None, tt, FF_SHARD), lambda b, t: (b, t, 0)),
        lambda tt: pl.BlockSpec((tt, D_MODEL), lambda b, t: (t, 0)),
        FF_SHARD, D_MODEL, name=f"ffn{tag}_w_in_grad", ride=ride)


def _load_mix_weight(wmix_hbm, wmix, sem):
    copies = [pltpu.make_async_copy(wmix_hbm.at[d], wmix.at[:, pl.ds(d * MIX_SHARD, MIX_SHARD)], sem.at[d])
              for d in range(N_DEV)]
    for cp in copies:
        cp.start()
    for cp in copies:
        cp.wait()


def _load_pool_weight(pw_hbm, pw, sem):
    rows = POOL_GROUP_DIM // N_DEV
    copies = [pltpu.make_async_copy(pw_hbm.at[d], pw.at[:, pl.ds(d * rows, rows), :], sem.at[d]) for d in range(N_DEV)]
    for cp in copies:
        cp.start()
    for cp in copies:
        cp.wait()


def _rotate(x1, x2, cos, sin):
    return x1 * cos - x2 * sin, x1 * sin + x2 * cos


def _mix_proj_forward(h1, gain, wmix8, cos, sin, ride=None):
    tm, nt = WIDE_TILE, SEQ // WIDE_TILE
    k_scale = HEAD_DIM ** -0.5

    def body(h_ref, g_ref, wmix_hbm, cos_ref, sin_ref, u_ref, qkvg_ref, p_ref, gates_ref, wmix, sem):
        @pl.when(pl.program_id(0) == 0)
        def _():
            _load_mix_weight(wmix_hbm, wmix, sem)

        u = _rms(h_ref[...], g_ref[...])[0].astype(BF16)
        u_ref[...] = u
        cos_t, sin_t = cos_ref[...], sin_ref[...]
        for seg in range(N_SEG):
            pr = _dot(u, wmix[:, pl.ds(seg * D_MODEL, D_MODEL)])
            if seg < 2:
                scale = 1.0 if seg == 0 else k_scale
                for hd in range(HEADS):
                    lo = hd * HEAD_DIM
                    o1, o2 = _rotate(pr[:, lo:lo + ROT_HALF], pr[:, lo + ROT_HALF:lo + HEAD_DIM], cos_t, sin_t)
                    qkvg_ref[:, pl.ds(seg * D_MODEL + lo, ROT_HALF)] = (o1 * scale).astype(BF16)
                    qkvg_ref[:, pl.ds(seg * D_MODEL + lo + ROT_HALF, ROT_HALF)] = (o2 * scale).astype(BF16)
            elif seg < 4:
                qkvg_ref[:, pl.ds(seg * D_MODEL, D_MODEL)] = pr.astype(BF16)
            elif seg == 4:
                p_ref[...] = pr
            else:
                gates_ref[:, pl.ds((seg - 5) * D_MODEL, D_MODEL)] = pr.astype(BF16)

    est = 2 * D_MODEL * N_SEG * D_MODEL + 2 * tm * (4 * D_MODEL + 2 * D_MODEL + 2 * 4 * D_MODEL + 4 * D_MODEL + 2 * 2 * D_MODEL)
    return _call(
        body, name="mix_proj_fwd", grid=(nt,),
        in_specs=[_row_spec(tm, D_MODEL), _full_spec((1, D_MODEL)), ANY, _row_spec(tm, ROT_HALF), _row_spec(tm, ROT_HALF)],
        out_specs=[_row_spec(tm, D_MODEL), _row_spec(tm, 4 * D_MODEL), _row_spec(tm, D_MODEL), _row_spec(tm, 2 * D_MODEL)],
        out_shape=[jax.ShapeDtypeStruct((SEQ, D_MODEL), BF16), jax.ShapeDtypeStruct((SEQ, 4 * D_MODEL), BF16),
                   jax.ShapeDtypeStruct((SEQ, D_MODEL), F32), jax.ShapeDtypeStruct((SEQ, 2 * D_MODEL), BF16)],
        scratch_shapes=[pltpu.VMEM((D_MODEL, N_SEG * D_MODEL), BF16), pltpu.SemaphoreType.DMA((N_DEV,))],
        vmem_bytes=est + 8 * tm * D_MODEL * 4, args=[h1, gain, wmix8, cos, sin], ride=ride)


def _seg_block_spec(seg, reverse=False):
    nb = SEQ // RET_BLOCK
    if reverse:
        return pl.BlockSpec((RET_BLOCK, D_MODEL), lambda i, s=seg: (nb - 1 - i, s))
    return pl.BlockSpec((RET_BLOCK, D_MODEL), lambda i, s=seg: (i, s))


def _table_specs():
    return [_full_spec((HEADS, RET_BLOCK, RET_BLOCK)), _full_spec((HEADS, RET_BLOCK, 1)),
            _full_spec((HEADS, RET_BLOCK, 1)), _full_spec((HEADS, 1, 1))]


def _head_cols(h):
    return pl.ds(h * HEAD_DIM, HEAD_DIM)


def _retention_forward(qkvg, tables, ride=None):
    nb = SEQ // RET_BLOCK

    def body(q_ref, k_ref, v_ref, gr_ref, mask_ref, qdec_ref, kdec_ref, cdec_ref, ret_ref, o_ref, state):
        @pl.when(pl.program_id(0) == 0)
        def _():
            state[...] = jnp.zeros_like(state)

        for h in range(HEADS):
            cols = _head_cols(h)
            q, k, v = q_ref[:, cols], k_ref[:, cols], v_ref[:, cols]
            scores = _dot_nt(q, k) * mask_ref[h]
            inner = _dot(scores.astype(BF16), v)
            cross = _dot((q.astype(F32) * qdec_ref[h]).astype(BF16), state[h].astype(BF16))
            ret = inner + cross
            state[h] = state[h] * cdec_ref[h] + _dot_tn((k.astype(F32) * kdec_ref[h]).astype(BF16), v)
            ret_ref[:, cols] = ret
            retn = ret * lax.rsqrt(jnp.mean(ret * ret, axis=-1, keepdims=True) + NORM_EPS)
            gr = gr_ref[:, cols].astype(F32)
            o_ref[:, cols] = (retn * (gr * _sig(gr))).astype(BF16)

    return _call(
        body, name="retention_fwd", grid=(nb,),
        in_specs=[_seg_block_spec(0), _seg_block_spec(1), _seg_block_spec(2), _seg_block_spec(3)] + _table_specs(),
        out_specs=[_row_spec(RET_BLOCK, D_MODEL)] * 2,
        out_shape=[jax.ShapeDtypeStruct((SEQ, D_MODEL), F32), jax.ShapeDtypeStruct((SEQ, D_MODEL), BF16)],
        scratch_shapes=[pltpu.VMEM((HEADS, HEAD_DIM, HEAD_DIM), F32)],
        vmem_bytes=24 * RET_BLOCK * D_MODEL * 4, args=[qkvg, qkvg, qkvg, qkvg, *tables], ride=ride)


def _retention_backward_q(qkvg, dret, tables, ride=None):
    nb = SEQ // RET_BLOCK

    def body(k_ref, v_ref, do_ref, mask_ref, qdec_ref, kdec_ref, cdec_ref, dq_ref, state):
        @pl.when(pl.program_id(0) == 0)
        def _():
            state[...] = jnp.zeros_like(state)

        for h in range(HEADS):
            cols = _head_cols(h)
            k, v, do = k_ref[:, cols], v_ref[:, cols], do_ref[:, cols]
            dscores = _dot_nt(do, v) * mask_ref[h]
            dq_ref[:, cols] = _dot(dscores.astype(BF16), k) + _dot_nt(do, state[h].astype(BF16)) * qdec_ref[h]
            state[h] = state[h] * cdec_ref[h] + _dot_tn((k.astype(F32) * kdec_ref[h]).astype(BF16), v)

    return _call(
        body, name="retention_bwd_q", grid=(nb,),
        in_specs=[_seg_block_spec(1), _seg_block_spec(2), _row_spec(RET_BLOCK, D_MODEL)] + _table_specs(),
        out_specs=[_row_spec(RET_BLOCK, D_MODEL)],
        out_shape=[jax.ShapeDtypeStruct((SEQ, D_MODEL), F32)],
        scratch_shapes=[pltpu.VMEM((HEADS, HEAD_DIM, HEAD_DIM), F32)],
        vmem_bytes=24 * RET_BLOCK * D_MODEL * 4, args=[qkvg, qkvg, dret, *tables], ride=ride)


def _retention_backward_kv(qkvg, dret, tables, ride=None):
    nb = SEQ // RET_BLOCK

    def body(q_ref, k_ref, v_ref, do_ref, mask_ref, qdec_ref, kdec_ref, cdec_ref, dk_ref, dv_ref, gstate):
        @pl.when(pl.program_id(0) == 0)
        def _():
            gstate[...] = jnp.zeros_like(gstate)

        for h in range(HEADS):
            cols = _head_cols(h)
            q, k, v, do = q_ref[:, cols], k_ref[:, cols], v_ref[:, cols], do_ref[:, cols]
            mask = mask_ref[h]
            scores = (_dot_nt(q, k) * mask).astype(BF16)
            dscores = (_dot_nt(do, v) * mask).astype(BF16)
            gs = gstate[h].astype(BF16)
            dk_ref[:, cols] = _dot_tn(dscores, q) + _dot_nt(v, gs) * kdec_ref[h]
            dv_ref[:, cols] = _dot_tn(scores, do) + _dot((k.astype(F32) * kdec_ref[h]).astype(BF16), gs)
            gstate[h] = gstate[h] * cdec_ref[h] + _dot_tn((q.astype(F32) * qdec_ref[h]).astype(BF16), do)

    rev = lambda: pl.BlockSpec((RET_BLOCK, D_MODEL), lambda i: (nb - 1 - i, 0))
    return _call(
        body, name="retention_bwd_kv", grid=(nb,),
        in_specs=[_seg_block_spec(0, True), _seg_block_spec(1, True), _seg_block_spec(2, True), rev()] + _table_specs(),
        out_specs=[rev(), rev()],
        out_shape=[jax.ShapeDtypeStruct((SEQ, D_MODEL), F32)] * 2,
        scratch_shapes=[pltpu.VMEM((HEADS, HEAD_DIM, HEAD_DIM), F32)],
        vmem_bytes=32 * RET_BLOCK * D_MODEL * 4, args=[qkvg, qkvg, qkvg, dret, *tables], ride=ride)


def _pooled(p_ext, first_row):
    rows = p_ext.shape[0]
    t = first_row + lax.broadcasted_iota(jnp.int32, (rows - HALO, 1), 0)
    outs = []
    for g, w in enumerate(POOL_WINDOWS):
        e = p_ext[:, g * POOL_GROUP_DIM:(g + 1) * POOL_GROUP_DIM]
        s, span = e, 1
        while span < w:
            s = s + pltpu.roll(s, span, 0)
            span *= 2
        count = jnp.minimum(t + 1, w).astype(F32)
        outs.append(s[HALO:] / count - e[HALO:])
    return outs


def _pooled_transpose(d_ext, first_row):
    rows = d_ext.shape[0]
    t = first_row + lax.broadcasted_iota(jnp.int32, (rows, 1), 0)
    outs = []
    for g, w in enumerate(POOL_WINDOWS):
        d = d_ext[:, g * POOL_GROUP_DIM:(g + 1) * POOL_GROUP_DIM]
        e = jnp.where(t < SEQ, d / jnp.minimum(t + 1, w).astype(F32), 0.0)
        s, span = e, 1
        while span < w:
            s = s + pltpu.roll(s, rows - span, 0)
            span *= 2
        outs.append(s[:rows - HALO] - d[:rows - HALO])
    return outs


def _mix_tail_specs(tm):
    halo_blocks = tm // HALO
    return [
        _row_spec(tm, D_MODEL),
        pl.BlockSpec((HALO, D_MODEL), lambda i: (jnp.maximum(i * halo_blocks - 1, 0), 0)),
        _row_spec(tm, 2 * D_MODEL),
        _row_spec(tm, D_MODEL),
        _full_spec((2, D_MODEL)), _full_spec((1, D_MODEL)), ANY,
        _full_spec((D_MODEL, D_MODEL)), _full_spec((D_MODEL, D_MODEL)), _full_spec((D_MODEL, D_MODEL)),
    ]


def _mix_tail_compute(i, tm, p_ref, halo_ref, gates_ref, oret_ref, bias_ref, scale_ref, pw, wru_ref, wpu_ref):
    halo = jnp.where(i > 0, halo_ref[...], 0.0)
    pooled = _pooled(jnp.concatenate([halo, p_ref[...]], axis=0), i * tm)
    pooled = [x.astype(BF16) for x in pooled]
    mixed = jnp.concatenate([_dot(pooled[g], pw[g]) for g in range(len(POOL_WINDOWS))], axis=-1)
    pool_out = (mixed * scale_ref[...]).astype(BF16)
    o_ret = oret_ref[...]
    a = _dot(o_ret, wru_ref[...])
    b = _dot(pool_out, wpu_ref[...])
    z = gates_ref[...].astype(F32)
    g0 = _sig(z[:, :D_MODEL] + bias_ref[0:1, :])
    g1 = _sig(z[:, D_MODEL:] + bias_ref[1:2, :])
    merged = (g0 * a + g1 * b).astype(BF16)
    return pooled, mixed, pool_out, o_ret, a, b, g0, g1, merged


def _mix_tail_forward(p, gates, o_ret, h1, bias, scale, pw8, wru, wpu, wo, ride=None):
    tm, nt = TOKEN_TILE, SEQ // TOKEN_TILE

    def body(p_ref, halo_ref, gates_ref, oret_ref, bias_ref, scale_ref, pw_hbm, wru_ref, wpu_ref, wo_ref, h1_ref,
             h2_ref, pw, sem):
        i = pl.program_id(0)

        @pl.when(i == 0)
        def _():
            _load_pool_weight(pw_hbm, pw, sem)

        merged = _mix_tail_compute(i, tm, p_ref, halo_ref, gates_ref, oret_ref, bias_ref, scale_ref, pw, wru_ref, wpu_ref)[-1]
        h2_ref[...] = h1_ref[...] + _dot(merged, wo_ref[...])

    est = 3 * 2 * 2 * D_MODEL * D_MODEL + 2 * tm * D_MODEL * (4 + 4 + 2 + 4 + 4) + 16 * tm * D_MODEL * 4
    return _call(
        body, name="mix_tail_fwd", grid=(nt,),
        in_specs=_mix_tail_specs(tm) + [_row_spec(tm, D_MODEL)],
        out_specs=[_row_spec(tm, D_MODEL)], out_shape=[jax.ShapeDtypeStruct((SEQ, D_MODEL), F32)],
        scratch_shapes=[pltpu.VMEM((len(POOL_WINDOWS), POOL_GROUP_DIM, POOL_GROUP_DIM), BF16), pltpu.SemaphoreType.DMA((N_DEV,))],
        vmem_bytes=est, args=[p, p, gates, o_ret, bias, scale, pw8, wru, wpu, wo, h1], ride=ride)


def _mix_tail_backward(dh2, p, gates, o_ret, ret, qkvg, bias, scale, pw8, wru, wpu, wo, ride=None):
    tm, nt = TOKEN_TILE, SEQ // TOKEN_TILE
    n_groups = len(POOL_WINDOWS)
    rows_per_dev = POOL_GROUP_DIM // N_DEV

    def body(p_ref, halo_ref, gates_ref, oret_ref, bias_ref, scale_ref, pw_hbm, wru_ref, wpu_ref, wo_ref,
             dh2_ref, ret_ref, gr_ref,
             dret_ref, dgr_ref, dgates_ref, dpooled_ref, dwo_ref, dwru_ref, dwpu_ref, dpw_ref, dbias_ref, dscale_ref,
             pw, sem, acc_wo, acc_wru, acc_wpu, acc_pw, send_sq, recv_sq, send_pw, recv_pw, send_sems, recv_sems):
        i = pl.program_id(0)

        @pl.when(i == 0)
        def _():
            _load_pool_weight(pw_hbm, pw, sem)
            for ref in (acc_wo, acc_wru, acc_wpu, acc_pw, dbias_ref, dscale_ref):
                ref[...] = jnp.zeros_like(ref)

        pooled, mixed, pool_out, o_ret, a, b, g0, g1, merged = _mix_tail_compute(
            i, tm, p_ref, halo_ref, gates_ref, oret_ref, bias_ref, scale_ref, pw, wru_ref, wpu_ref)
        dh2 = dh2_ref[...].astype(BF16)
        dm = _dot_nt(dh2, wo_ref[...])
        acc_wo[...] += _dot_tn(merged, dh2)
        da = (dm * g0).astype(BF16)
        db = (dm * g1).astype(BF16)
        dz0 = dm * a * g0 * (1.0 - g0)
        dz1 = dm * b * g1 * (1.0 - g1)
        dbias_ref[0:1, :] += jnp.sum(dz0, axis=0, keepdims=True)
        dbias_ref[1:2, :] += jnp.sum(dz1, axis=0, keepdims=True)
        dgates_ref[:, pl.ds(0, D_MODEL)] = dz0.astype(BF16)
        dgates_ref[:, pl.ds(D_MODEL, D_MODEL)] = dz1.astype(BF16)
        acc_wru[...] += _dot_tn(o_ret, da)
        acc_wpu[...] += _dot_tn(pool_out, db)
        d_oret = _dot_nt(da, wru_ref[...])
        d_pool_out = _dot_nt(db, wpu_ref[...])
        dscale_ref[...] += jnp.sum(d_pool_out * mixed, axis=0, keepdims=True)
        dmixed = (d_pool_out * scale_ref[...]).astype(BF16)
        for g in range(n_groups):
            dmg = dmixed[:, g * POOL_GROUP_DIM:(g + 1) * POOL_GROUP_DIM]
            acc_pw[g] += _dot_tn(pooled[g], dmg)
            dpooled_ref[:, pl.ds(g * POOL_GROUP_DIM, POOL_GROUP_DIM)] = _dot_nt(dmg, pw[g])
        gr = gr_ref[...].astype(F32)
        s = _sig(gr)
        silu = gr * s
        for hd in range(HEADS):
            cols = slice(hd * HEAD_DIM, (hd + 1) * HEAD_DIM)
            r_h = ret_ref[:, cols]
            rr = lax.rsqrt(jnp.mean(r_h * r_h, axis=-1, keepdims=True) + NORM_EPS)
            rhat = r_h * rr
            do_h = d_oret[:, cols]
            dgr_ref[:, cols] = (do_h * rhat * (s[:, cols] * (1.0 + gr[:, cols] * (1.0 - s[:, cols])))).astype(BF16)
            dret_ref[:, cols] = _rms_bwd(do_h * silu[:, cols], rhat, rr).astype(BF16)

        @pl.when(i == nt - 1)
        def _():
            c = lax.axis_index("c")
            rows = D_MODEL // N_DEV
            squares = ((acc_wo, dwo_ref), (acc_wru, dwru_ref), (acc_wpu, dwpu_ref))
            for q in range(n_chips):
                own = pl.multiple_of((2 * q + c) * rows, rows)
                other = pl.multiple_of((2 * q + 1 - c) * rows, rows)
                for t, (acc, out) in enumerate(squares):
                    out[q] = acc[pl.ds(own, rows), :].astype(BF16)
                    send_sq[t, q] = acc[pl.ds(other, rows), :].astype(BF16)
                own_pw = pl.multiple_of((2 * q + c) * rows_per_dev, rows_per_dev)
                other_pw = pl.multiple_of((2 * q + 1 - c) * rows_per_dev, rows_per_dev)
                dpw_ref[q] = acc_pw[:, pl.ds(own_pw, rows_per_dev), :].astype(BF16)
                send_pw[q] = acc_pw[:, pl.ds(other_pw, rows_per_dev), :].astype(BF16)
            pushes = [_to_sibling(send_sq, recv_sq, send_sems.at[0], recv_sems.at[0]),
                      _to_sibling(send_pw, recv_pw, send_sems.at[1], recv_sems.at[1])]
            for cp in pushes:
                cp.start()
            for cp in pushes:
                cp.wait_recv()
            for t, (acc, out) in enumerate(squares):
                out[...] = (out[...].astype(F32) + recv_sq[t].astype(F32)).astype(BF16)
            dpw_ref[...] = (dpw_ref[...].astype(F32) + recv_pw[...].astype(F32)).astype(BF16)
            for cp in pushes:
                cp.wait_send()

    n_chips = N_DEV // 2
    sq = (n_chips, D_MODEL // N_DEV, D_MODEL)
    pw_shape = (n_chips, n_groups, rows_per_dev, POOL_GROUP_DIM)
    est = (3 * 2 * 2 * D_MODEL * D_MODEL + 3 * 4 * D_MODEL * D_MODEL + 3 * 2 * 2 * D_MODEL * D_MODEL
           + 2 * tm * D_MODEL * (4 + 4 + 2 + 4 + 4 + 2 + 2 + 2 + 4 + 4) + 24 * tm * D_MODEL * 4)
    return _call(
        body, name="mix_tail_bwd", grid=(nt,),
        in_specs=_mix_tail_specs(tm) + [_row_spec(tm, D_MODEL), _row_spec(tm, D_MODEL), _row_spec(tm, D_MODEL, 3)],
        out_specs=[_row_spec(tm, D_MODEL), _row_spec(tm, D_MODEL), _row_spec(tm, 2 * D_MODEL), _row_spec(tm, D_MODEL),
                   _full_spec(sq), _full_spec(sq), _full_spec(sq), _full_spec(pw_shape),
                   _full_spec((2, D_MODEL)), _full_spec((1, D_MODEL))],
        out_shape=[jax.ShapeDtypeStruct((SEQ, D_MODEL), BF16), jax.ShapeDtypeStruct((SEQ, D_MODEL), BF16),
                   jax.ShapeDtypeStruct((SEQ, 2 * D_MODEL), BF16), jax.ShapeDtypeStruct((SEQ, D_MODEL), F32),
                   jax.ShapeDtypeStruct(sq, BF16), jax.ShapeDtypeStruct(sq, BF16), jax.ShapeDtypeStruct(sq, BF16),
                   jax.ShapeDtypeStruct(pw_shape, BF16),
                   jax.ShapeDtypeStruct((2, D_MODEL), F32), jax.ShapeDtypeStruct((1, D_MODEL), F32)],
        scratch_shapes=[pltpu.VMEM((n_groups, POOL_GROUP_DIM, POOL_GROUP_DIM), BF16), pltpu.SemaphoreType.DMA((N_DEV,)),
                        pltpu.VMEM((D_MODEL, D_MODEL), F32), pltpu.VMEM((D_MODEL, D_MODEL), F32),
                        pltpu.VMEM((D_MODEL, D_MODEL), F32), pltpu.VMEM((n_groups, POOL_GROUP_DIM, POOL_GROUP_DIM), F32),
                        pltpu.VMEM((3,) + sq, BF16), pltpu.VMEM((3,) + sq, BF16), pltpu.VMEM(pw_shape, BF16),
                        pltpu.VMEM(pw_shape, BF16), pltpu.SemaphoreType.DMA((2,)), pltpu.SemaphoreType.DMA((2,))],
        vmem_bytes=est, args=[p, p, gates, o_ret, bias, scale, pw8, wru, wpu, wo, dh2, ret, qkvg], ride=ride)


def _mix_proj_backward(dq, dk, dv, dgr, dpooled, dgates, cos, sin, h1, gain, dh2, wmix8, ride=None):
    tm, nt = TOKEN_TILE, SEQ // TOKEN_TILE
    halo_blocks = tm // HALO
    last_halo = SEQ // HALO - 1
    k_scale = HEAD_DIM ** -0.5

    def body(dq_ref, dk_ref, dv_ref, dgr_ref, dpool_ref, dhalo_ref, dgates_ref, cos_ref, sin_ref, h1_ref, g_ref,
             dh2_ref, wmix_hbm, dh1_ref, dproj_ref, dg_ref, wmix, sem):
        i = pl.program_id(0)

        @pl.when(i == 0)
        def _():
            _load_mix_weight(wmix_hbm, wmix, sem)
            dg_ref[...] = jnp.zeros_like(dg_ref)

        cos_t, sin_t = cos_ref[...], sin_ref[...]
        for seg, ref, scale in ((0, dq_ref, 1.0), (1, dk_ref, k_scale)):
            for hd in range(HEADS):
                lo = hd * HEAD_DIM
                d1, d2 = ref[:, lo:lo + ROT_HALF], ref[:, lo + ROT_HALF:lo + HEAD_DIM]
                dproj_ref[:, pl.ds(seg * D_MODEL + lo, ROT_HALF)] = ((d1 * cos_t + d2 * sin_t) * scale).astype(BF16)
                dproj_ref[:, pl.ds(seg * D_MODEL + lo + ROT_HALF, ROT_HALF)] = ((d2 * cos_t - d1 * sin_t) * scale).astype(BF16)
        dproj_ref[:, pl.ds(2 * D_MODEL, D_MODEL)] = dv_ref[...].astype(BF16)
        dproj_ref[:, pl.ds(3 * D_MODEL, D_MODEL)] = dgr_ref[...]
        dp = _pooled_transpose(jnp.concatenate([dpool_ref[...], dhalo_ref[...]], axis=0), i * tm)
        for g in range(len(POOL_WINDOWS)):
            dproj_ref[:, pl.ds(4 * D_MODEL + g * POOL_GROUP_DIM, POOL_GROUP_DIM)] = dp[g].astype(BF16)
        dproj_ref[:, pl.ds(5 * D_MODEL, 2 * D_MODEL)] = dgates_ref[...]
        du = jnp.zeros((tm, D_MODEL), F32)
        for seg in range(N_SEG):
            cols = pl.ds(seg * D_MODEL, D_MODEL)
            du = du + _dot_nt(dproj_ref[:, cols], wmix[:, cols])
        g = g_ref[...]
        _, xhat, r = _rms(h1_ref[...], g)
        dg_ref[...] += jnp.sum(du * xhat, axis=0, keepdims=True)
        dh1_ref[...] = dh2_ref[...] + _rms_bwd(du * g, xhat, r)

    est = 2 * D_MODEL * N_SEG * D_MODEL + 2 * tm * D_MODEL * (3 * 4 + 2 + 4 + 4 + 4 + 4 + 4 + 14) + 12 * tm * D_MODEL * 4
    return _call(
        body, name="mix_proj_bwd", grid=(nt,),
        in_specs=[_row_spec(tm, D_MODEL), _row_spec(tm, D_MODEL), _row_spec(tm, D_MODEL), _row_spec(tm, D_MODEL),
                  _row_spec(tm, D_MODEL),
                  pl.BlockSpec((HALO, D_MODEL), lambda i: (jnp.minimum((i + 1) * halo_blocks, last_halo), 0)),
                  _row_spec(tm, 2 * D_MODEL), _row_spec(tm, ROT_HALF), _row_spec(tm, ROT_HALF),
                  _row_spec(tm, D_MODEL), _full_spec((1, D_MODEL)), _row_spec(tm, D_MODEL), ANY],
        out_specs=[_row_spec(tm, D_MODEL), _row_spec(tm, N_SEG * D_MODEL), _full_spec((1, D_MODEL))],
        out_shape=[jax.ShapeDtypeStruct((SEQ, D_MODEL), F32), jax.ShapeDtypeStruct((SEQ, N_SEG * D_MODEL), BF16),
                   jax.ShapeDtypeStruct((1, D_MODEL), F32)],
        scratch_shapes=[pltpu.VMEM((D_MODEL, N_SEG * D_MODEL), BF16), pltpu.SemaphoreType.DMA((N_DEV,))],
        vmem_bytes=est, args=[dq, dk, dv, dgr, dpooled, dpooled, dgates, cos, sin, h1, gain, dh2, wmix8], ride=ride)


def _adamw(w, parts, m, v, name):
    rows, cols = w.shape
    n_parts = parts.shape[0]
    tr = max([t for t in range(16, 257, 16) if rows % t == 0], default=rows)
    c1 = 1.0 - ADAM_B1 ** ADAM_STEP
    c2 = 1.0 - ADAM_B2 ** ADAM_STEP

    def body(w_ref, p_ref, m_ref, v_ref, g_out, d_out, m_out, v_out):
        g = p_ref[0].astype(F32)
        for k in range(1, n_parts):
            g = g + p_ref[k].astype(F32)
        m_new = ADAM_B1 * m_ref[...] + (1.0 - ADAM_B1) * g
        v_new = ADAM_B2 * v_ref[...] + (1.0 - ADAM_B2) * (g * g)
        g_out[...] = g
        m_out[...] = m_new
        v_out[...] = v_new
        d_out[...] = -ADAM_LR * ((m_new / c1) / (jnp.sqrt(v_new / c2) + ADAM_EPS) + ADAM_WD * w_ref[...])

    spec = pl.BlockSpec((tr, cols), lambda i: (i, 0))
    out = jax.ShapeDtypeStruct((rows, cols), F32)
    return pl.pallas_call(
        body, name=name, grid=(rows // tr,),
        in_specs=[spec, pl.BlockSpec((n_parts, tr, cols), lambda i: (0, i, 0)), spec, spec],
        out_specs=[spec] * 4, out_shape=[out] * 4,
        compiler_params=_params(2 * tr * cols * (7 * 4 + n_parts * parts.dtype.itemsize) + 8 * tr * cols * 4, 1),
    )(_in_hbm(w), _in_hbm(parts), _in_hbm(m), _in_hbm(v))


def _mix_w_in_grad(u, dproj, ride=None):
    return _weight_grad(
        u, dproj, N_DEV,
        lambda tt: pl.BlockSpec((tt, D_MODEL), lambda b, t: (t, 0)),
        lambda tt: pl.BlockSpec((tt, MIX_SHARD), lambda b, t: (t, b)),
        D_MODEL, MIX_SHARD, name="w_in_grad", ride=ride)


def kernel(x, norm_ffn1, ffn1_w_in, ffn1_w_out, norm_mix, w_in, gate_bias, pool_w, pool_scale, w_ret_up, w_pool_up, w_out, norm_ffn2, ffn2_w_in, ffn2_w_out, norm_final, loss_target, m_norm_ffn1, m_ffn1_w_in, m_ffn1_w_out, m_norm_mix, m_w_in, m_gate_bias, m_pool_w, m_pool_scale, m_w_ret_up, m_w_pool_up, m_w_out, m_norm_ffn2, m_ffn2_w_in, m_ffn2_w_out, m_norm_final, v_norm_ffn1, v_ffn1_w_in, v_ffn1_w_out, v_norm_mix, v_w_in, v_gate_bias, v_pool_w, v_pool_scale, v_w_ret_up, v_w_pool_up, v_w_out, v_norm_ffn2, v_ffn2_w_in, v_ffn2_w_out, v_norm_final):
    assert x.shape == (1, SEQ, D_MODEL) and ffn1_w_in.shape == (1, D_MODEL, FF_SHARD) and w_in.shape == (1, D_MODEL, MIX_SHARD)
    x2, target = x[0], loss_target[0]

    cos, sin = _rotary_tables()
    tables = _retention_tables()
    big = [ffn1_w_in, ffn1_w_out, w_in, pool_w, w_ret_up, w_pool_up, w_out, ffn2_w_in, ffn2_w_out]
    bf = lambda w: w[0].astype(BF16)
    bf_t = lambda w: jnp.swapaxes(w[0], 0, 1).astype(BF16)
    square = lambda w: w.reshape(D_MODEL, D_MODEL)

    win1, wout1, bias8 = _alone(_GatherRide([bf_t(ffn1_w_in), bf(ffn1_w_out), gate_bias[0]]), "ffn1_weights_all_gather")
    wout1 = wout1.reshape(N_FF_GROUPS, FF_SHARD, D_MODEL)
    bias = bias8.transpose(1, 0, 2).reshape(2, D_MODEL)

    (h1, gu1), (wmix8,) = _ffn_forward(x2, norm_ffn1, win1, wout1, "ffn1_fwd", ride=_GatherRide([bf(w_in)]))
    (u, qkvg, p, gates), (win2,) = _mix_proj_forward(h1, norm_mix, wmix8, cos, sin, ride=_GatherRide([bf_t(ffn2_w_in)]))
    (ret, o_ret), (pw8, wru, wpu, wo) = _retention_forward(
        qkvg, tables, ride=_GatherRide([bf(pool_w), bf(w_ret_up), bf(w_pool_up), bf(w_out)]))
    wru, wpu, wo = square(wru), square(wpu), square(wo)
    (h2,), (wout2,) = _mix_tail_forward(p, gates, o_ret, h1, bias, pool_scale, pw8, wru, wpu, wo,
                                        ride=_GatherRide([bf(ffn2_w_out)]))
    wout2 = wout2.reshape(N_FF_GROUPS, FF_SHARD, D_MODEL)
    (dh3, gu2, loss_part, d_norm_final), _ = _ffn_forward(h2, norm_ffn2, win2, wout2, "ffn2_fwd_loss",
                                                          head=(target, norm_final.reshape(1, D_MODEL)))

    (dh2, dgu2, act2, xn2, df2, d_norm_ffn2), _ = _ffn_backward(dh3, h2, norm_ffn2, gu2, win2, wout2, "ffn2_bwd")
    d_wout2, _ = _ffn_w_out_grad(act2, df2, 2)
    d_win2, (r_wout2,) = _ffn_w_in_grad(xn2, dgu2, 2, ride=_ScatterRide([d_wout2]))
    (dret, dgr, dgates, dpooled, d_wo, d_wru, d_wpu, d_pw, d_bias, d_scale), (r_win2,) = _mix_tail_backward(
        dh2, p, gates, o_ret, ret, qkvg, bias, pool_scale, pw8, wru, wpu, wo, ride=_ScatterRide([d_win2]))
    (dq,), _ = _retention_backward_q(qkvg, dret, tables)
    (dk, dv), (r_pw, r_wru, r_wpu, r_wo) = _retention_backward_kv(
        qkvg, dret, tables, ride=_ScatterRide([d_pw, d_wru, d_wpu, d_wo]))
    (dh1, dproj, d_norm_mix), _ = _mix_proj_backward(dq, dk, dv, dgr, dpooled, dgates, cos, sin, h1, norm_mix, dh2, wmix8)
    d_wmix, _ = _mix_w_in_grad(u, dproj)
    (grad_x, dgu1, act1, xn1, df1, d_norm_ffn1), (r_wmix,) = _ffn_backward(
        dh1, x2, norm_ffn1, gu1, win1, wout1, "ffn1_bwd", ride=_ScatterRide([d_wmix]))
    d_wout1, _ = _ffn_w_out_grad(act1, df1, 1)
    d_win1, (r_wout1,) = _ffn_w_in_grad(xn1, dgu1, 1, ride=_ScatterRide([d_wout1]))
    (r_win1,) = _alone(_ScatterRide([d_win1]), "ffn1_w_in_grad_reduce_scatter")
    received = [r_win1, r_wout1, r_wmix, r_pw, r_wru, r_wpu, r_wo, r_win2, r_wout2]
    zero_row = jnp.zeros((1, D_MODEL), F32)
    small = _all_reduce_rows(jnp.concatenate(
        [d_norm_ffn1, d_norm_mix, d_scale, d_norm_ffn2, d_norm_final, d_bias, zero_row], axis=0))
    loss = lax.psum(loss_part[0, 0], ("x", "y", "c"))

    names = ["ffn1_w_in", "ffn1_w_out", "w_in", "pool_w", "w_ret_up", "w_pool_up", "w_out", "ffn2_w_in", "ffn2_w_out"]
    moments_m = [m_ffn1_w_in, m_ffn1_w_out, m_w_in, m_pool_w, m_w_ret_up, m_w_pool_up, m_w_out, m_ffn2_w_in, m_ffn2_w_out]
    moments_v = [v_ffn1_w_in, v_ffn1_w_out, v_w_in, v_pool_w, v_w_ret_up, v_w_pool_up, v_w_out, v_ffn2_w_in, v_ffn2_w_out]
    results = {}
    for nm, w, parts, m, v in zip(names, big, received, moments_m, moments_v):
        if nm in ("ffn1_w_in", "ffn2_w_in"):
            flat, back = (lambda a: jnp.swapaxes(a[0], 0, 1)), (lambda o: jnp.swapaxes(o, 0, 1)[None])
        else:
            flat, back = (lambda a, w=w: a.reshape(-1, w.shape[-1])), (lambda o, w=w: o.reshape(w.shape))
        outs = _adamw(flat(w), parts.reshape(parts.shape[:1] + flat(w).shape), flat(m), flat(v), name=f"adamw_{nm}")
        results[nm] = [back(o) for o in outs]

    my_id = _linear_id(*_my_position())
    bias_cols = gate_bias.shape[-1]
    pad = lambda a: jnp.pad(a[0], ((0, 0), (0, D_MODEL - bias_cols)))
    pack = lambda a, b, c, d, e, gb: jnp.concatenate([a, b, c, d, e.reshape(1, D_MODEL), pad(gb), zero_row], axis=0)
    d_bias_mine = lax.dynamic_slice_in_dim(small[5:7], my_id * bias_cols, bias_cols, axis=1)
    g_small = jnp.concatenate([small[0:5], jnp.pad(d_bias_mine, ((0, 0), (0, D_MODEL - bias_cols))), zero_row], axis=0)
    s_outs = _adamw(pack(norm_ffn1, norm_mix, pool_scale, norm_ffn2, norm_final, gate_bias), g_small[None],
                    pack(m_norm_ffn1, m_norm_mix, m_pool_scale, m_norm_ffn2, m_norm_final, m_gate_bias),
                    pack(v_norm_ffn1, v_norm_mix, v_pool_scale, v_norm_ffn2, v_norm_final, v_gate_bias), name="adamw_small")
    for row, nm in enumerate(["norm_ffn1", "norm_mix", "pool_scale", "norm_ffn2"]):
        results[nm] = [o[row:row + 1] for o in s_outs]
    results["norm_final"] = [o[4] for o in s_outs]
    results["gate_bias"] = [o[5:7, :bias_cols][None] for o in s_outs]

    order = ["norm_ffn1", "ffn1_w_in", "ffn1_w_out", "norm_mix", "w_in", "gate_bias", "pool_w", "pool_scale",
             "w_ret_up", "w_pool_up", "w_out", "norm_ffn2", "ffn2_w_in", "ffn2_w_out", "norm_final"]
    return (loss, grad_x[None], *[results[nm][0] for nm in order], *[results[nm][1] for nm in order],
            *[results[nm][2] for nm in order], *[results[nm][3] for nm in order])
```

```python
import functools

import numpy as np
import jax
import jax.numpy as jnp
from jax import lax
from jax.experimental import pallas as pl
from jax.experimental.pallas import tpu as pltpu

F32 = jnp.float32
BF16 = jnp.bfloat16

N_DEV = 8
D_MODEL = 1024
SEQ = 4096
D_FF = 2816
FF_SHARD = 2 * D_FF // N_DEV
N_FF_GROUPS = N_DEV // 2
HEADS = 4
HEAD_DIM = 256
ROT_HALF = HEAD_DIM // 2
CHUNK = 64
RET_BLOCK = 256
POOL_WINDOWS = (2, 4, 8, 16)
POOL_GROUP_DIM = 256
HALO = 16
MIX_SHARD = 7 * D_MODEL // N_DEV
N_SEG = 7
ROPE_BASE = 10000.0
NORM_EPS = 1e-6
FFN_RES_WEIGHT = 0.5
ADAM_LR, ADAM_B1, ADAM_B2, ADAM_EPS, ADAM_WD, ADAM_STEP = 0.001, 0.9, 0.999, 1e-08, 0.01, 10

TOKEN_TILE = 256
WIDE_TILE = 512
VMEM_CAP_V7X = 64 * 1024 * 1024
MESH = pl.DeviceIdType.MESH
ANY = pl.BlockSpec(memory_space=pl.ANY)


def _vmem_limit(estimate_bytes):
    return int(min(estimate_bytes * 5 // 4 + (6 << 20), VMEM_CAP_V7X - (4 << 20)))


def _params(estimate_bytes, n_grid):
    return pltpu.CompilerParams(dimension_semantics=("arbitrary",) * n_grid,
                                vmem_limit_bytes=_vmem_limit(estimate_bytes))


def _dot(a, b):
    return jnp.dot(a, b, preferred_element_type=F32)


def _dot_nt(a, b):
    return lax.dot_general(a, b, (((1,), (1,)), ((), ())), preferred_element_type=F32)


def _dot_tn(a, b):
    return lax.dot_general(a, b, (((0,), (0,)), ((), ())), preferred_element_type=F32)


def _sig(x):
    return 1.0 / (1.0 + jnp.exp(-x))


def _rms(x, g):
    r = lax.rsqrt(jnp.mean(x * x, axis=-1, keepdims=True) + NORM_EPS)
    xhat = x * r
    return xhat * g, xhat, r


def _rms_bwd(dyg, xhat, r):
    return r * (dyg - xhat * jnp.mean(dyg * xhat, axis=-1, keepdims=True))


def _row_spec(tile, width, col=0):
    return pl.BlockSpec((tile, width), lambda i, c=col: (i, c))


def _full_spec(shape):
    return pl.BlockSpec(shape, lambda *_: (0,) * len(shape))


def _rotary_tables():
    inv_freq = (np.float32(ROPE_BASE) ** (-np.arange(ROT_HALF, dtype=np.float32) / np.float32(ROT_HALF))).astype(np.float32)
    ang = (np.arange(SEQ, dtype=np.float32)[:, None] * inv_freq[None, :]).astype(np.float32)
    return jnp.asarray(np.cos(ang.astype(np.float64)), F32), jnp.asarray(np.sin(ang.astype(np.float64)), F32)


def _retention_tables():
    log_gamma = np.log(1.0 - 2.0 ** (-5.0 - np.arange(HEADS, dtype=np.float64)))
    n = np.arange(RET_BLOCK)
    diff = (n[:, None] - n[None, :]).astype(np.float64)
    same = (n[:, None] // CHUNK) == (n[None, :] // CHUNK)
    earlier = (n[None, :] // CHUNK) < (n[:, None] // CHUNK)
    expo = np.where(same, np.abs(diff), diff)
    mask = np.where(same | earlier, np.exp(log_gamma[:, None, None] * expo[None]), 0.0)
    qdec = np.exp(log_gamma[:, None] * (n[None, :] + 1.0))[:, :, None]
    kdec = np.exp(log_gamma[:, None] * (RET_BLOCK - 1.0 - n[None, :]))[:, :, None]
    cdec = np.exp(log_gamma * RET_BLOCK)[:, None, None]
    return (jnp.asarray(mask, F32), jnp.asarray(qdec, F32), jnp.asarray(kdec, F32), jnp.asarray(cdec, F32))


def _my_position():
    return lax.axis_index("x"), lax.axis_index("y"), lax.axis_index("c")


def _linear_id(px, py, pc):
    return 4 * px + 2 * py + pc


def _when(pred, fn):
    if isinstance(pred, bool):
        if pred:
            fn()
    else:
        pl.when(pred)(fn)


class _GatherRide:
    def __init__(self, shards):
        self.args = list(shards)
        n = self.n = len(shards)
        self.out_shape = [pltpu.HBM((N_DEV,) + s.shape, s.dtype) for s in shards]
        self.scratch = [pltpu.SemaphoreType.DMA((n, 7)), pltpu.SemaphoreType.DMA((n, 7)), pltpu.SemaphoreType.DMA((n,))]

    def _plan(self, src, out, sems):
        send_sems, recv_sems, local_sem = sems
        x, y, c = _my_position()
        me, sibling = (x, y, c), (x, y, 1 - c)
        chips = [(1 - x, y), (x, 1 - y), (1 - x, 1 - y)]

        def copy(t, k, block, to, from_src=False):
            rows = out[t].at[_linear_id(*block)]
            return pltpu.make_async_remote_copy(
                src_ref=src[t] if from_src else rows, dst_ref=rows,
                send_sem=send_sems.at[t, k], recv_sem=recv_sems.at[t, k],
                device_id=to, device_id_type=MESH)

        local = [pltpu.make_async_copy(src[t], out[t].at[_linear_id(*me)], local_sem.at[t]) for t in range(self.n)]
        return copy, local, me, sibling, chips, c

    def begin(self, first, src, out, sems):
        copy, local, me, sibling, chips, c = self._plan(src, out, sems)

        def start():
            for cp in local:
                cp.start()
            for t in range(self.n):
                copy(t, 0, me, sibling, from_src=True).start()
                for j, chip in enumerate(chips):
                    copy(t, 1 + j, me, (*chip, c), from_src=True).start()

        _when(first, start)

    def finish(self, mid, last, src, out, sems):
        copy, local, me, sibling, chips, c = self._plan(src, out, sems)

        def pass_on():
            for j, chip in enumerate(chips):
                for t in range(self.n):
                    copy(t, 1 + j, (*chip, c), me).wait_recv()
                    copy(t, 4 + j, (*chip, c), sibling).start()

        def drain():
            for t in range(self.n):
                copy(t, 0, sibling, me).wait_recv()
                for j, chip in enumerate(chips):
                    copy(t, 4 + j, (*chip, 1 - c), me).wait_recv()
            for t in range(self.n):
                copy(t, 0, me, sibling, from_src=True).wait_send()
                for j, chip in enumerate(chips):
                    copy(t, 1 + j, me, (*chip, c), from_src=True).wait_send()
                    copy(t, 4 + j, (*chip, c), sibling).wait_send()
            for cp in local:
                cp.wait()

        _when(mid, pass_on)
        _when(last, drain)


class _RelayGather(_GatherRide):
    def begin(self, first, src, out, sems):
        copy, local, me, sibling, chips, c = self._plan(src, out, sems)
        for cp in local:
            cp.start()
        for t in range(self.n):
            copy(t, 0, me, sibling, from_src=True).start()
            for j in range(2):
                copy(t, 1 + j, me, (*chips[j], c), from_src=True).start()

    def finish(self, mid, last, src, out, sems):
        copy, local, me, sibling, chips, c = self._plan(src, out, sems)
        x, y, _ = me
        relayed_from = (x ^ (1 - c), y ^ c, c)
        relayed_to = (x ^ c, y ^ (1 - c), c)
        for t in range(self.n):
            for j in range(2):
                copy(t, 1 + j, (*chips[j], c), me).wait_recv()
            copy(t, 3, relayed_from, relayed_to).start()
            for j in range(2):
                copy(t, 4 + j, (*chips[j], c), sibling).start()
        for t in range(self.n):
            copy(t, 3, (*chips[2], c), me).wait_recv()
            copy(t, 6, (*chips[2], c), sibling).start()
        for t in range(self.n):
            copy(t, 0, sibling, me).wait_recv()
            for j in range(3):
                copy(t, 4 + j, (*chips[j], 1 - c), me).wait_recv()
        for t in range(self.n):
            copy(t, 0, me, sibling, from_src=True).wait_send()
            for j in range(2):
                copy(t, 1 + j, me, (*chips[j], c), from_src=True).wait_send()
            copy(t, 3, relayed_from, relayed_to).wait_send()
            for j in range(3):
                copy(t, 4 + j, (*chips[j], c), sibling).wait_send()
        for cp in local:
            cp.wait()


class _ScatterRide:
    def __init__(self, chip_sums):
        self.args = list(chip_sums)
        n = self.n = len(chip_sums)
        self.out_shape = [pltpu.HBM(p.shape, p.dtype) for p in chip_sums]
        self.scratch = [pltpu.SemaphoreType.DMA((n, 3)), pltpu.SemaphoreType.DMA((n, 3)), pltpu.SemaphoreType.DMA((n,))]

    def _plan(self, src, out, sems):
        send_sems, recv_sems, local_sem = sems
        x, y, c = _my_position()

        def peer(k):
            return (x ^ (k >> 1), y ^ (k & 1))

        copies = [pltpu.make_async_remote_copy(
            src_ref=src[t].at[2 * peer(k)[0] + peer(k)[1]], dst_ref=out[t].at[k],
            send_sem=send_sems.at[t, k - 1], recv_sem=recv_sems.at[t, k - 1],
            device_id=(*peer(k), c), device_id_type=MESH) for t in range(self.n) for k in range(1, N_DEV // 2)]
        local = [pltpu.make_async_copy(src[t].at[2 * x + y], out[t].at[0], local_sem.at[t]) for t in range(self.n)]
        return copies, local

    def begin(self, first, src, out, sems):
        copies, local = self._plan(src, out, sems)

        def start():
            for cp in local + copies:
                cp.start()

        _when(first, start)

    def finish(self, mid, last, src, out, sems):
        copies, local = self._plan(src, out, sems)

        def drain():
            for cp in copies:
                cp.wait_recv()
            for cp in copies:
                cp.wait_send()
            for cp in local:
                cp.wait()

        _when(last, drain)


def _in_hbm(a):
    return pltpu.with_memory_space_constraint(a, pltpu.HBM)


def _call(body, *, name, grid, in_specs, out_specs, out_shape, scratch_shapes, vmem_bytes, args, ride=None):
    n_in, n_out, n_s = len(in_specs), len(out_specs), len(scratch_shapes)
    params = _params(vmem_bytes, len(grid))
    args = [_in_hbm(a) for a in args]
    out_shape = [pltpu.HBM(s.shape, s.dtype) for s in out_shape]
    if ride is None:
        outs = pl.pallas_call(body, name=name, grid=grid, in_specs=in_specs, out_specs=out_specs, out_shape=out_shape,
                              scratch_shapes=scratch_shapes, compiler_params=params)(*args)
        return list(outs), []
    total = int(np.prod(grid))

    def riding_body(*refs):
        a = n_in
        b = a + ride.n
        c = b + n_out
        d = c + ride.n
        e = d + n_s
        step = pl.program_id(0)
        for axis in range(1, len(grid)):
            step = step * grid[axis] + pl.program_id(axis)
        ride.begin(step == 0, refs[a:b], refs[c:d], refs[e:])
        body(*refs[:a], *refs[b:c], *refs[d:e])
        ride.finish(step == (3 * total) // 4, step == total - 1, refs[a:b], refs[c:d], refs[e:])

    outs = pl.pallas_call(
        riding_body, name=name, grid=grid, in_specs=list(in_specs) + [ANY] * ride.n,
        out_specs=list(out_specs) + [ANY] * ride.n, out_shape=list(out_shape) + ride.out_shape,
        scratch_shapes=list(scratch_shapes) + ride.scratch, compiler_params=params)(*args, *[_in_hbm(a) for a in ride.args])
    return list(outs[:n_out]), list(outs[n_out:])


def _alone(ride, name):
    def body(*refs):
        src, out, sems = refs[:ride.n], refs[ride.n:2 * ride.n], refs[2 * ride.n:]
        ride.begin(True, src, out, sems)
        ride.finish(True, True, src, out, sems)

    return list(pl.pallas_call(body, name=name, out_shape=ride.out_shape, in_specs=[ANY] * ride.n,
                               out_specs=[ANY] * ride.n, scratch_shapes=ride.scratch)(*[_in_hbm(a) for a in ride.args]))


def _all_reduce_rows(block):
    rows, width = block.shape

    def body(x_ref, sum_ref, gathered, send_sems, recv_sems, local_sem):
        x, y, c = _my_position()
        me, sibling = (x, y, c), (x, y, 1 - c)
        chips = [(1 - x, y), (x, 1 - y), (1 - x, 1 - y)]

        def slot(px, py, pc):
            return gathered.at[_linear_id(px, py, pc)]

        def copy(k, block_of, to, from_src=False):
            return pltpu.make_async_remote_copy(
                src_ref=x_ref if from_src else slot(*block_of), dst_ref=slot(*block_of),
                send_sem=send_sems.at[k], recv_sem=recv_sems.at[k], device_id=to, device_id_type=MESH)

        mine = pltpu.make_async_copy(x_ref, slot(*me), local_sem)
        mine.start()
        first = [copy(0, me, sibling, from_src=True)]
        first += [copy(1 + j, me, (*chip, c), from_src=True) for j, chip in enumerate(chips)]
        for cp in first:
            cp.start()
        passed = [copy(4 + j, (*chip, c), sibling) for j, chip in enumerate(chips)]
        for j, chip in enumerate(chips):
            copy(1 + j, (*chip, c), me).wait_recv()
            passed[j].start()
        copy(0, sibling, me).wait_recv()
        for j, chip in enumerate(chips):
            copy(4 + j, (*chip, 1 - c), me).wait_recv()
        for cp in first + passed:
            cp.wait_send()
        mine.wait()
        total = gathered[0]
        for d in range(1, N_DEV):
            total = total + gathered[d]
        sum_ref[...] = total

    return pl.pallas_call(
        body, name="small_grads_all_reduce",
        out_shape=jax.ShapeDtypeStruct((rows, width), F32),
        in_specs=[pl.BlockSpec(memory_space=pltpu.VMEM)],
        out_specs=pl.BlockSpec(memory_space=pltpu.VMEM),
        scratch_shapes=[pltpu.VMEM((N_DEV, rows, width), F32),
                        pltpu.SemaphoreType.DMA((7,)), pltpu.SemaphoreType.DMA((7,)), pltpu.SemaphoreType.DMA],
    )(block)


def _load_ffn_weights(win_hbm, wout_hbm, win, wout, sem):
    a = pltpu.make_async_copy(win_hbm, win, sem.at[0])
    b = pltpu.make_async_copy(wout_hbm, wout, sem.at[1])
    a.start()
    b.start()
    a.wait()
    b.wait()


def _ffn_forward(h_in, gain, win8, wout, name, head=None, ride=None):
    tm, nt = WIDE_TILE, SEQ // WIDE_TILE

    def body(*refs):
        if head is None:
            x_ref, g_ref, win_hbm, wout_hbm, out_ref, gu_ref, win, wout, sem = refs
        else:
            x_ref, g_ref, win_hbm, wout_hbm, tgt_ref, gf_ref, out_ref, gu_ref, loss_ref, dgf_ref, win, wout, sem = refs
        i = pl.program_id(0)

        @pl.when(i == 0)
        def _():
            _load_ffn_weights(win_hbm, wout_hbm, win, wout, sem)
            if head is not None:
                loss_ref[...] = jnp.zeros_like(loss_ref)
                dgf_ref[...] = jnp.zeros_like(dgf_ref)

        x = x_ref[...]
        xn, _, _ = _rms(x, g_ref[...])
        xb = xn.astype(BF16)
        acc = jnp.zeros((tm, D_MODEL), F32)
        for j in range(N_FF_GROUPS):
            gate = _dot_nt(xb, win[j])
            up = _dot_nt(xb, win[j + N_FF_GROUPS])
            gu_ref[j] = gate.astype(BF16)
            gu_ref[j + N_FF_GROUPS] = up.astype(BF16)
            act = gate * _sig(gate) * up
            acc = acc + _dot(act.astype(BF16), wout[j])
        h = x + FFN_RES_WEIGHT * acc
        if head is None:
            out_ref[...] = h
        else:
            gf = gf_ref[...]
            y, hhat, r = _rms(h, gf)
            err = y - tgt_ref[...]
            loss_ref[...] += jnp.full(loss_ref.shape, 0.5 / D_MODEL * jnp.sum(err * err), F32)
            dy = err * (1.0 / D_MODEL)
            dgf_ref[...] += jnp.sum(dy * hhat, axis=0, keepdims=True)
            out_ref[...] = _rms_bwd(dy * gf, hhat, r)

    weights = 2 * D_MODEL * 2 * D_FF + 2 * D_FF * D_MODEL
    tiles = 2 * (2 * 4 * tm * D_MODEL + 2 * tm * 2 * D_FF) + (2 * 4 * tm * D_MODEL if head else 0)
    in_specs = [_row_spec(tm, D_MODEL), _full_spec((1, D_MODEL)), ANY, ANY]
    out_shape = [jax.ShapeDtypeStruct((SEQ, D_MODEL), F32), jax.ShapeDtypeStruct((N_DEV, SEQ, FF_SHARD), BF16)]
    out_specs = [_row_spec(tm, D_MODEL), pl.BlockSpec((N_DEV, tm, FF_SHARD), lambda i: (0, i, 0))]
    args = [h_in, gain, win8, wout]
    if head is not None:
        in_specs += [_row_spec(tm, D_MODEL), _full_spec((1, D_MODEL))]
        out_shape += [jax.ShapeDtypeStruct((1, 128), F32), jax.ShapeDtypeStruct((1, D_MODEL), F32)]
        out_specs += [_full_spec((1, 128)), _full_spec((1, D_MODEL))]
        args += list(head)
    return _call(
        body, name=name, grid=(nt,), in_specs=in_specs, out_specs=out_specs, out_shape=out_shape,
        scratch_shapes=[pltpu.VMEM((N_DEV, FF_SHARD, D_MODEL), BF16), pltpu.VMEM((N_FF_GROUPS, FF_SHARD, D_MODEL), BF16),
                        pltpu.SemaphoreType.DMA((2,))],
        vmem_bytes=weights + tiles + 16 * tm * FF_SHARD * 4, args=args, ride=ride)


def _ffn_backward(dh_out, h_in, gain, gu, win8, wout, name, ride=None):
    tm, nt = TOKEN_TILE, SEQ // TOKEN_TILE

    def body(dh_ref, x_ref, g_ref, gu_ref, win_hbm, wout_hbm,
             dhin_ref, dgu_ref, act_ref, xn_ref, df_ref, dg_ref, win, wout, sem):
        i = pl.program_id(0)

        @pl.when(i == 0)
        def _():
            _load_ffn_weights(win_hbm, wout_hbm, win, wout, sem)
            dg_ref[...] = jnp.zeros_like(dg_ref)

        dh = dh_ref[...]
        g = g_ref[...]
        xn, xhat, r = _rms(x_ref[...], g)
        df = (FFN_RES_WEIGHT * dh).astype(BF16)
        dxn = jnp.zeros((tm, D_MODEL), F32)
        for j in range(N_FF_GROUPS):
            gate = gu_ref[j].astype(F32)
            up = gu_ref[j + N_FF_GROUPS].astype(F32)
            dact = _dot_nt(df, wout[j])
            s = _sig(gate)
            silu = gate * s
            dgate = (dact * up * (s * (1.0 + gate * (1.0 - s)))).astype(BF16)
            dup = (dact * silu).astype(BF16)
            act_ref[j] = (silu * up).astype(BF16)
            dgu_ref[j] = dgate
            dgu_ref[j + N_FF_GROUPS] = dup
            dxn = dxn + _dot(dgate, win[j]) + _dot(dup, win[j + N_FF_GROUPS])
        dg_ref[...] += jnp.sum(dxn * xhat, axis=0, keepdims=True)
        dhin_ref[...] = dh + _rms_bwd(dxn * g, xhat, r)
        xn_ref[...] = xn.astype(BF16)
        df_ref[...] = df

    weights = 2 * D_MODEL * 2 * D_FF + 2 * D_FF * D_MODEL
    tiles = 2 * (3 * 4 * tm * D_MODEL + 2 * tm * (2 * 2 * D_FF + D_FF) + 2 * 2 * tm * D_MODEL)
    gu_spec = pl.BlockSpec((N_DEV, tm, FF_SHARD), lambda i: (0, i, 0))
    return _call(
        body, name=name, grid=(nt,),
        in_specs=[_row_spec(tm, D_MODEL), _row_spec(tm, D_MODEL), _full_spec((1, D_MODEL)), gu_spec, ANY, ANY],
        out_specs=[_row_spec(tm, D_MODEL), gu_spec, pl.BlockSpec((N_FF_GROUPS, tm, FF_SHARD), lambda i: (0, i, 0)),
                   _row_spec(tm, D_MODEL), _row_spec(tm, D_MODEL), _full_spec((1, D_MODEL))],
        out_shape=[jax.ShapeDtypeStruct((SEQ, D_MODEL), F32), jax.ShapeDtypeStruct((N_DEV, SEQ, FF_SHARD), BF16),
                   jax.ShapeDtypeStruct((N_FF_GROUPS, SEQ, FF_SHARD), BF16), jax.ShapeDtypeStruct((SEQ, D_MODEL), BF16),
                   jax.ShapeDtypeStruct((SEQ, D_MODEL), BF16), jax.ShapeDtypeStruct((1, D_MODEL), F32)],
        scratch_shapes=[pltpu.VMEM((N_DEV, FF_SHARD, D_MODEL), BF16), pltpu.VMEM((N_FF_GROUPS, FF_SHARD, D_MODEL), BF16),
                        pltpu.SemaphoreType.DMA((2,))],
        vmem_bytes=weights + tiles + 20 * tm * FF_SHARD * 4, args=[dh_out, h_in, gain, gu, win8, wout], ride=ride)


def _to_sibling(src, dst, send_sem, recv_sem):
    x, y, c = _my_position()
    return pltpu.make_async_remote_copy(src_ref=src, dst_ref=dst, send_sem=send_sem, recv_sem=recv_sem,
                                        device_id=(x, y, 1 - c), device_id_type=MESH)


def _weight_grad(x, g, n_out, x_spec, g_spec, k_dim, n_dim, name, halves=False, tt=1024, ride=None):
    nt = SEQ // tt
    n_chips = N_DEV // 2
    rows = k_dim // 2 if halves else k_dim

    def body(x_ref, g_ref, out_ref, acc, sendbuf, recvbuf, send_sems, recv_sems):
        b, t = pl.program_id(0), pl.program_id(1)
        c = lax.axis_index("c")

        def push(q):
            return _to_sibling(sendbuf.at[q], recvbuf.at[q], send_sems.at[q], recv_sems.at[q])

        @pl.when(t == 0)
        def _():
            acc[...] = jnp.zeros_like(acc)

        acc[...] += _dot_tn(x_ref[...], g_ref[...])

        @pl.when(t == nt - 1)
        def _():
            if halves:
                for mine, other in ((0, 1), (1, 0)):
                    @pl.when(c == mine)
                    def _():
                        out_ref[b] = acc[pl.ds(mine * rows, rows), :].astype(BF16)
                        sendbuf[b] = acc[pl.ds(other * rows, rows), :].astype(BF16)
                push(b).start()
            else:
                q = b // 2

                @pl.when(b % 2 == c)
                def _():
                    out_ref[q] = acc[...].astype(BF16)

                @pl.when(b % 2 != c)
                def _():
                    sendbuf[q] = acc[...].astype(BF16)
                    push(q).start()

        @pl.when((b == n_out - 1) & (t == nt - 1))
        def _():
            for q in range(n_chips):
                push(q).wait_recv()
                out_ref[q] = (out_ref[q].astype(F32) + recvbuf[q].astype(F32)).astype(BF16)
            for q in range(n_chips):
                push(q).wait_send()

    piece = (n_chips, rows, n_dim)
    outs, ride_outs = _call(
        body, name=name, grid=(n_out, nt), in_specs=[x_spec(tt), g_spec(tt)],
        out_specs=[pl.BlockSpec(piece, lambda b, t: (0, 0, 0))],
        out_shape=[jax.ShapeDtypeStruct(piece, BF16)],
        scratch_shapes=[pltpu.VMEM((k_dim, n_dim), F32), pltpu.VMEM(piece, BF16), pltpu.VMEM(piece, BF16),
                        pltpu.SemaphoreType.DMA((n_chips,)), pltpu.SemaphoreType.DMA((n_chips,))],
        vmem_bytes=2 * 2 * tt * (k_dim + n_dim) + 8 * k_dim * n_dim + 4 * 2 * n_chips * rows * n_dim, args=[x, g], ride=ride)
    return outs[0], ride_outs


def _ffn_w_out_grad(act, df, tag, ride=None):
    return _weight_grad(
        act, df, N_FF_GROUPS,
        lambda tt: pl.BlockSpec((None, tt, FF_SHARD), lambda b, t: (b, t, 0)),
        lambda tt: pl.BlockSpec((tt, D_MODEL), lambda b, t: (t, 0)),
        FF_SHARD, D_MODEL, name=f"ffn{tag}_w_out_grad", halves=True, ride=ride)


def _ffn_w_in_grad(xn, dgu, tag, ride=None):
    return _weight_grad(
        dgu, xn, N_DEV,
        lambda tt: pl.BlockSpec((None, tt, FF_SHARD), lambda b, t: (b, t, 0)),
        lambda tt: pl.BlockSpec((tt, D_MODEL), lambda b, t: (t, 0)),
        FF_SHARD, D_MODEL, name=f"ffn{tag}_w_in_grad", ride=ride)


def _load_mix_weight(wmix_hbm, wmix, sem):
    copies = [pltpu.make_async_copy(wmix_hbm.at[d], wmix.at[:, pl.ds(d * MIX_SHARD, MIX_SHARD)], sem.at[d])
              for d in range(N_DEV)]
    for cp in copies:
        cp.start()
    for cp in copies:
        cp.wait()


def _load_pool_weight(pw_hbm, pw, sem):
    rows = POOL_GROUP_DIM // N_DEV
    copies = [pltpu.make_async_copy(pw_hbm.at[d], pw.at[:, pl.ds(d * rows, rows), :], sem.at[d]) for d in range(N_DEV)]
    for cp in copies:
        cp.start()
    for cp in copies:
        cp.wait()


def _rotate(x1, x2, cos, sin):
    return x1 * cos - x2 * sin, x1 * sin + x2 * cos


def _mix_proj_forward(h1, gain, wmix8, cos, sin, ride=None):
    tm, nt = WIDE_TILE, SEQ // WIDE_TILE
    k_scale = HEAD_DIM ** -0.5

    def body(h_ref, g_ref, wmix_hbm, cos_ref, sin_ref, u_ref, qkvg_ref, p_ref, gates_ref, wmix, sem):
        @pl.when(pl.program_id(0) == 0)
        def _():
            _load_mix_weight(wmix_hbm, wmix, sem)

        u = _rms(h_ref[...], g_ref[...])[0].astype(BF16)
        u_ref[...] = u
        cos_t, sin_t = cos_ref[...], sin_ref[...]
        for seg in range(N_SEG):
            pr = _dot(u, wmix[:, pl.ds(seg * D_MODEL, D_MODEL)])
            if seg < 2:
                scale = 1.0 if seg == 0 else k_scale
                for hd in range(HEADS):
                    lo = hd * HEAD_DIM
                    o1, o2 = _rotate(pr[:, lo:lo + ROT_HALF], pr[:, lo + ROT_HALF:lo + HEAD_DIM], cos_t, sin_t)
                    qkvg_ref[:, pl.ds(seg * D_MODEL + lo, ROT_HALF)] = (o1 * scale).astype(BF16)
                    qkvg_ref[:, pl.ds(seg * D_MODEL + lo + ROT_HALF, ROT_HALF)] = (o2 * scale).astype(BF16)
            elif seg < 4:
                qkvg_ref[:, pl.ds(seg * D_MODEL, D_MODEL)] = pr.astype(BF16)
            elif seg == 4:
                p_ref[...] = pr
            else:
                gates_ref[:, pl.ds((seg - 5) * D_MODEL, D_MODEL)] = pr.astype(BF16)

    est = 2 * D_MODEL * N_SEG * D_MODEL + 2 * tm * (4 * D_MODEL + 2 * D_MODEL + 2 * 4 * D_MODEL + 4 * D_MODEL + 2 * 2 * D_MODEL)
    return _call(
        body, name="mix_proj_fwd", grid=(nt,),
        in_specs=[_row_spec(tm, D_MODEL), _full_spec((1, D_MODEL)), ANY, _row_spec(tm, ROT_HALF), _row_spec(tm, ROT_HALF)],
        out_specs=[_row_spec(tm, D_MODEL), _row_spec(tm, 4 * D_MODEL), _row_spec(tm, D_MODEL), _row_spec(tm, 2 * D_MODEL)],
        out_shape=[jax.ShapeDtypeStruct((SEQ, D_MODEL), BF16), jax.ShapeDtypeStruct((SEQ, 4 * D_MODEL), BF16),
                   jax.ShapeDtypeStruct((SEQ, D_MODEL), F32), jax.ShapeDtypeStruct((SEQ, 2 * D_MODEL), BF16)],
        scratch_shapes=[pltpu.VMEM((D_MODEL, N_SEG * D_MODEL), BF16), pltpu.SemaphoreType.DMA((N_DEV,))],
        vmem_bytes=est + 8 * tm * D_MODEL * 4, args=[h1, gain, wmix8, cos, sin], ride=ride)


def _seg_block_spec(seg, reverse=False):
    nb = SEQ // RET_BLOCK
    if reverse:
        return pl.BlockSpec((RET_BLOCK, D_MODEL), lambda i, s=seg: (nb - 1 - i, s))
    return pl.BlockSpec((RET_BLOCK, D_MODEL), lambda i, s=seg: (i, s))


def _table_specs():
    return [_full_spec((HEADS, RET_BLOCK, RET_BLOCK)), _full_spec((HEADS, RET_BLOCK, 1)),
            _full_spec((HEADS, RET_BLOCK, 1)), _full_spec((HEADS, 1, 1))]


def _head_cols(h):
    return pl.ds(h * HEAD_DIM, HEAD_DIM)


def _retention_forward(qkvg, tables, ride=None):
    nb = SEQ // RET_BLOCK

    def body(q_ref, k_ref, v_ref, gr_ref, mask_ref, qdec_ref, kdec_ref, cdec_ref, ret_ref, o_ref, state):
        @pl.when(pl.program_id(0) == 0)
        def _():
            state[...] = jnp.zeros_like(state)

        for h in range(HEADS):
            cols = _head_cols(h)
            q, k, v = q_ref[:, cols], k_ref[:, cols], v_ref[:, cols]
            scores = _dot_nt(q, k) * mask_ref[h]
            inner = _dot(scores.astype(BF16), v)
            cross = _dot((q.astype(F32) * qdec_ref[h]).astype(BF16), state[h].astype(BF16))
            ret = inner + cross
            state[h] = state[h] * cdec_ref[h] + _dot_tn((k.astype(F32) * kdec_ref[h]).astype(BF16), v)
            ret_ref[:, cols] = ret
            retn = ret * lax.rsqrt(jnp.mean(ret * ret, axis=-1, keepdims=True) + NORM_EPS)
            gr = gr_ref[:, cols].astype(F32)
            o_ref[:, cols] = (retn * (gr * _sig(gr))).astype(BF16)

    return _call(
        body, name="retention_fwd", grid=(nb,),
        in_specs=[_seg_block_spec(0), _seg_block_spec(1), _seg_block_spec(2), _seg_block_spec(3)] + _table_specs(),
        out_specs=[_row_spec(RET_BLOCK, D_MODEL)] * 2,
        out_shape=[jax.ShapeDtypeStruct((SEQ, D_MODEL), F32), jax.ShapeDtypeStruct((SEQ, D_MODEL), BF16)],
        scratch_shapes=[pltpu.VMEM((HEADS, HEAD_DIM, HEAD_DIM), F32)],
        vmem_bytes=24 * RET_BLOCK * D_MODEL * 4, args=[qkvg, qkvg, qkvg, qkvg, *tables], ride=ride)


def _retention_backward_q(qkvg, dret, tables, ride=None):
    nb = SEQ // RET_BLOCK

    def body(k_ref, v_ref, do_ref, mask_ref, qdec_ref, kdec_ref, cdec_ref, dq_ref, state):
        @pl.when(pl.program_id(0) == 0)
        def _():
            state[...] = jnp.zeros_like(state)

        for h in range(HEADS):
            cols = _head_cols(h)
            k, v, do = k_ref[:, cols], v_ref[:, cols], do_ref[:, cols]
            dscores = _dot_nt(do, v) * mask_ref[h]
            dq_ref[:, cols] = _dot(dscores.astype(BF16), k) + _dot_nt(do, state[h].astype(BF16)) * qdec_ref[h]
            state[h] = state[h] * cdec_ref[h] + _dot_tn((k.astype(F32) * kdec_ref[h]).astype(BF16), v)

    return _call(
        body, name="retention_bwd_q", grid=(nb,),
        in_specs=[_seg_block_spec(1), _seg_block_spec(2), _row_spec(RET_BLOCK, D_MODEL)] + _table_specs(),
        out_specs=[_row_spec(RET_BLOCK, D_MODEL)],
        out_shape=[jax.ShapeDtypeStruct((SEQ, D_MODEL), F32)],
        scratch_shapes=[pltpu.VMEM((HEADS, HEAD_DIM, HEAD_DIM), F32)],
        vmem_bytes=24 * RET_BLOCK * D_MODEL * 4, args=[qkvg, qkvg, dret, *tables], ride=ride)


def _retention_backward_kv(qkvg, dret, tables, ride=None):
    nb = SEQ // RET_BLOCK

    def body(q_ref, k_ref, v_ref, do_ref, mask_ref, qdec_ref, kdec_ref, cdec_ref, dk_ref, dv_ref, gstate):
        @pl.when(pl.program_id(0) == 0)
        def _():
            gstate[...] = jnp.zeros_like(gstate)

        for h in range(HEADS):
            cols = _head_cols(h)
            q, k, v, do = q_ref[:, cols], k_ref[:, cols], v_ref[:, cols], do_ref[:, cols]
            mask = mask_ref[h]
            scores = (_dot_nt(q, k) * mask).astype(BF16)
            dscores = (_dot_nt(do, v) * mask).astype(BF16)
            gs = gstate[h].astype(BF16)
            dk_ref[:, cols] = _dot_tn(dscores, q) + _dot_nt(v, gs) * kdec_ref[h]
            dv_ref[:, cols] = _dot_tn(scores, do) + _dot((k.astype(F32) * kdec_ref[h]).astype(BF16), gs)
            gstate[h] = gstate[h] * cdec_ref[h] + _dot_tn((q.astype(F32) * qdec_ref[h]).astype(BF16), do)

    rev = lambda: pl.BlockSpec((RET_BLOCK, D_MODEL), lambda i: (nb - 1 - i, 0))
    return _call(
        body, name="retention_bwd_kv", grid=(nb,),
        in_specs=[_seg_block_spec(0, True), _seg_block_spec(1, True), _seg_block_spec(2, True), rev()] + _table_specs(),
        out_specs=[rev(), rev()],
        out_shape=[jax.ShapeDtypeStruct((SEQ, D_MODEL), F32)] * 2,
        scratch_shapes=[pltpu.VMEM((HEADS, HEAD_DIM, HEAD_DIM), F32)],
        vmem_bytes=32 * RET_BLOCK * D_MODEL * 4, args=[qkvg, qkvg, qkvg, dret, *tables], ride=ride)


def _pooled(p_ext, first_row):
    rows = p_ext.shape[0]
    t = first_row + lax.broadcasted_iota(jnp.int32, (rows - HALO, 1), 0)
    outs = []
    for g, w in enumerate(POOL_WINDOWS):
        e = p_ext[:, g * POOL_GROUP_DIM:(g + 1) * POOL_GROUP_DIM]
        s, span = e, 1
        while span < w:
            s = s + pltpu.roll(s, span, 0)
            span *= 2
        count = jnp.minimum(t + 1, w).astype(F32)
        outs.append(s[HALO:] / count - e[HALO:])
    return outs


def _pooled_transpose(d_ext, first_row):
    rows = d_ext.shape[0]
    t = first_row + lax.broadcasted_iota(jnp.int32, (rows, 1), 0)
    outs = []
    for g, w in enumerate(POOL_WINDOWS):
        d = d_ext[:, g * POOL_GROUP_DIM:(g + 1) * POOL_GROUP_DIM]
        e = jnp.where(t < SEQ, d / jnp.minimum(t + 1, w).astype(F32), 0.0)
        s, span = e, 1
        while span < w:
            s = s + pltpu.roll(s, rows - span, 0)
            span *= 2
        outs.append(s[:rows - HALO] - d[:rows - HALO])
    return outs


def _mix_tail_specs(tm):
    halo_blocks = tm // HALO
    return [
        _row_spec(tm, D_MODEL),
        pl.BlockSpec((HALO, D_MODEL), lambda i: (jnp.maximum(i * halo_blocks - 1, 0), 0)),
        _row_spec(tm, 2 * D_MODEL),
        _row_spec(tm, D_MODEL),
        _full_spec((2, D_MODEL)), _full_spec((1, D_MODEL)), ANY,
        _full_spec((D_MODEL, D_MODEL)), _full_spec((D_MODEL, D_MODEL)), _full_spec((D_MODEL, D_MODEL)),
    ]


def _mix_tail_compute(i, tm, p_ref, halo_ref, gates_ref, oret_ref, bias_ref, scale_ref, pw, wru_ref, wpu_ref, saved=None):
    halo = jnp.where(i > 0, halo_ref[...], 0.0)
    pooled = _pooled(jnp.concatenate([halo, p_ref[...]], axis=0), i * tm)
    pooled = [x.astype(BF16) for x in pooled]
    mixed = jnp.concatenate([_dot(pooled[g], pw[g]) for g in range(len(POOL_WINDOWS))], axis=-1)
    pool_out = (mixed * scale_ref[...]).astype(BF16)
    o_ret = oret_ref[...]
    if saved is None:
        a = _dot(o_ret, wru_ref[...])
        b = _dot(pool_out, wpu_ref[...])
    else:
        a, b = saved[0][...].astype(F32), saved[1][...].astype(F32)
    z = gates_ref[...].astype(F32)
    g0 = _sig(z[:, :D_MODEL] + bias_ref[0:1, :])
    g1 = _sig(z[:, D_MODEL:] + bias_ref[1:2, :])
    merged = (g0 * a + g1 * b).astype(BF16)
    return pooled, mixed, pool_out, o_ret, a, b, g0, g1, merged


def _mix_tail_forward(p, gates, o_ret, h1, bias, scale, pw8, wru, wpu, wo, ride=None):
    tm, nt = TOKEN_TILE, SEQ // TOKEN_TILE

    def body(p_ref, halo_ref, gates_ref, oret_ref, bias_ref, scale_ref, pw_hbm, wru_ref, wpu_ref, wo_ref, h1_ref,
             h2_ref, a_ref, b_ref, pw, sem):
        i = pl.program_id(0)

        @pl.when(i == 0)
        def _():
            _load_pool_weight(pw_hbm, pw, sem)

        out = _mix_tail_compute(i, tm, p_ref, halo_ref, gates_ref, oret_ref, bias_ref, scale_ref, pw, wru_ref, wpu_ref)
        a_ref[...] = out[4].astype(BF16)
        b_ref[...] = out[5].astype(BF16)
        h2_ref[...] = h1_ref[...] + _dot(out[-1], wo_ref[...])

    est = 3 * 2 * 2 * D_MODEL * D_MODEL + 2 * tm * D_MODEL * (4 + 4 + 2 + 4 + 4) + 16 * tm * D_MODEL * 4
    return _call(
        body, name="mix_tail_fwd", grid=(nt,),
        in_specs=_mix_tail_specs(tm) + [_row_spec(tm, D_MODEL)],
        out_specs=[_row_spec(tm, D_MODEL)] * 3,
        out_shape=[jax.ShapeDtypeStruct((SEQ, D_MODEL), F32)] + [jax.ShapeDtypeStruct((SEQ, D_MODEL), BF16)] * 2,
        scratch_shapes=[pltpu.VMEM((len(POOL_WINDOWS), POOL_GROUP_DIM, POOL_GROUP_DIM), BF16), pltpu.SemaphoreType.DMA((N_DEV,))],
        vmem_bytes=est, args=[p, p, gates, o_ret, bias, scale, pw8, wru, wpu, wo, h1], ride=ride)


def _mix_tail_backward(dh2, p, gates, o_ret, ret, qkvg, a_saved, b_saved, bias, scale, pw8, wru, wpu, wo, ride=None):
    tm, nt = TOKEN_TILE, SEQ // TOKEN_TILE
    n_groups = len(POOL_WINDOWS)
    rows_per_dev = POOL_GROUP_DIM // N_DEV

    def body(p_ref, halo_ref, gates_ref, oret_ref, bias_ref, scale_ref, pw_hbm, wru_ref, wpu_ref, wo_ref,
             dh2_ref, ret_ref, gr_ref, a_ref, b_ref,
             dret_ref, dgr_ref, dgates_ref, dpooled_ref, dwo_ref, dwru_ref, dwpu_ref, dpw_ref, dbias_ref, dscale_ref,
             pw, sem, acc_wo, acc_wru, acc_wpu, acc_pw, send_sq, recv_sq, send_pw, recv_pw, send_sems, recv_sems):
        i = pl.program_id(0)

        @pl.when(i == 0)
        def _():
            _load_pool_weight(pw_hbm, pw, sem)
            for ref in (acc_wo, acc_wru, acc_wpu, acc_pw, dbias_ref, dscale_ref):
                ref[...] = jnp.zeros_like(ref)

        pooled, mixed, pool_out, o_ret, a, b, g0, g1, merged = _mix_tail_compute(
            i, tm, p_ref, halo_ref, gates_ref, oret_ref, bias_ref, scale_ref, pw, wru_ref, wpu_ref, saved=(a_ref, b_ref))
        dh2 = dh2_ref[...].astype(BF16)
        dm = _dot_nt(dh2, wo_ref[...])
        acc_wo[...] += _dot_tn(merged, dh2)
        da = (dm * g0).astype(BF16)
        db = (dm * g1).astype(BF16)
        dz0 = dm * a * g0 * (1.0 - g0)
        dz1 = dm * b * g1 * (1.0 - g1)
        dbias_ref[0:1, :] += jnp.sum(dz0, axis=0, keepdims=True)
        dbias_ref[1:2, :] += jnp.sum(dz1, axis=0, keepdims=True)
        dgates_ref[:, pl.ds(0, D_MODEL)] = dz0.astype(BF16)
        dgates_ref[:, pl.ds(D_MODEL, D_MODEL)] = dz1.astype(BF16)
        acc_wru[...] += _dot_tn(o_ret, da)
        acc_wpu[...] += _dot_tn(pool_out, db)
        d_oret = _dot_nt(da, wru_ref[...])
        d_pool_out = _dot_nt(db, wpu_ref[...])
        dscale_ref[...] += jnp.sum(d_pool_out * mixed, axis=0, keepdims=True)
        dmixed = (d_pool_out * scale_ref[...]).astype(BF16)
        for g in range(n_groups):
            dmg = dmixed[:, g * POOL_GROUP_DIM:(g + 1) * POOL_GROUP_DIM]
            acc_pw[g] += _dot_tn(pooled[g], dmg)
            dpooled_ref[:, pl.ds(g * POOL_GROUP_DIM, POOL_GROUP_DIM)] = _dot_nt(dmg, pw[g])
        gr = gr_ref[...].astype(F32)
        s = _sig(gr)
        silu = gr * s
        for hd in range(HEADS):
            cols = slice(hd * HEAD_DIM, (hd + 1) * HEAD_DIM)
            r_h = ret_ref[:, cols]
            rr = lax.rsqrt(jnp.mean(r_h * r_h, axis=-1, keepdims=True) + NORM_EPS)
            rhat = r_h * rr
            do_h = d_oret[:, cols]
            dgr_ref[:, cols] = (do_h * rhat * (s[:, cols] * (1.0 + gr[:, cols] * (1.0 - s[:, cols])))).astype(BF16)
            dret_ref[:, cols] = _rms_bwd(do_h * silu[:, cols], rhat, rr).astype(BF16)

        @pl.when(i == nt - 1)
        def _():
            c = lax.axis_index("c")
            rows = D_MODEL // N_DEV
            squares = ((acc_wo, dwo_ref), (acc_wru, dwru_ref), (acc_wpu, dwpu_ref))
            for q in range(n_chips):
                own = pl.multiple_of((2 * q + c) * rows, rows)
                other = pl.multiple_of((2 * q + 1 - c) * rows, rows)
                for t, (acc, out) in enumerate(squares):
                    out[q] = acc[pl.ds(own, rows), :].astype(BF16)
                    send_sq[t, q] = acc[pl.ds(other, rows), :].astype(BF16)
                own_pw = pl.multiple_of((2 * q + c) * rows_per_dev, rows_per_dev)
                other_pw = pl.multiple_of((2 * q + 1 - c) * rows_per_dev, rows_per_dev)
                dpw_ref[q] = acc_pw[:, pl.ds(own_pw, rows_per_dev), :].astype(BF16)
                send_pw[q] = acc_pw[:, pl.ds(other_pw, rows_per_dev), :].astype(BF16)
            pushes = [_to_sibling(send_sq, recv_sq, send_sems.at[0], recv_sems.at[0]),
                      _to_sibling(send_pw, recv_pw, send_sems.at[1], recv_sems.at[1])]
            for cp in pushes:
                cp.start()
            for cp in pushes:
                cp.wait_recv()
            for t, (acc, out) in enumerate(squares):
                out[...] = (out[...].astype(F32) + recv_sq[t].astype(F32)).astype(BF16)
            dpw_ref[...] = (dpw_ref[...].astype(F32) + recv_pw[...].astype(F32)).astype(BF16)
            for cp in pushes:
                cp.wait_send()

    n_chips = N_DEV // 2
    sq = (n_chips, D_MODEL // N_DEV, D_MODEL)
    pw_shape = (n_chips, n_groups, rows_per_dev, POOL_GROUP_DIM)
    est = (3 * 2 * 2 * D_MODEL * D_MODEL + 3 * 4 * D_MODEL * D_MODEL + 3 * 2 * 2 * D_MODEL * D_MODEL
           + 2 * tm * D_MODEL * (4 + 4 + 2 + 4 + 4 + 2 + 2 + 2 + 4 + 4) + 24 * tm * D_MODEL * 4)
    return _call(
        body, name="mix_tail_bwd", grid=(nt,),
        in_specs=_mix_tail_specs(tm) + [_row_spec(tm, D_MODEL), _row_spec(tm, D_MODEL), _row_spec(tm, D_MODEL, 3),
                                        _row_spec(tm, D_MODEL), _row_spec(tm, D_MODEL)],
        out_specs=[_row_spec(tm, D_MODEL), _row_spec(tm, D_MODEL), _row_spec(tm, 2 * D_MODEL), _row_spec(tm, D_MODEL),
                   _full_spec(sq), _full_spec(sq), _full_spec(sq), _full_spec(pw_shape),
                   _full_spec((2, D_MODEL)), _full_spec((1, D_MODEL))],
        out_shape=[jax.ShapeDtypeStruct((SEQ, D_MODEL), BF16), jax.ShapeDtypeStruct((SEQ, D_MODEL), BF16),
                   jax.ShapeDtypeStruct((SEQ, 2 * D_MODEL), BF16), jax.ShapeDtypeStruct((SEQ, D_MODEL), F32),
                   jax.ShapeDtypeStruct(sq, BF16), jax.ShapeDtypeStruct(sq, BF16), jax.ShapeDtypeStruct(sq, BF16),
                   jax.ShapeDtypeStruct(pw_shape, BF16),
                   jax.ShapeDtypeStruct((2, D_MODEL), F32), jax.ShapeDtypeStruct((1, D_MODEL), F32)],
        scratch_shapes=[pltpu.VMEM((n_groups, POOL_GROUP_DIM, POOL_GROUP_DIM), BF16), pltpu.SemaphoreType.DMA((N_DEV,)),
                        pltpu.VMEM((D_MODEL, D_MODEL), F32), pltpu.VMEM((D_MODEL, D_MODEL), F32),
                        pltpu.VMEM((D_MODEL, D_MODEL), F32), pltpu.VMEM((n_groups, POOL_GROUP_DIM, POOL_GROUP_DIM), F32),
                        pltpu.VMEM((3,) + sq, BF16), pltpu.VMEM((3,) + sq, BF16), pltpu.VMEM(pw_shape, BF16),
                        pltpu.VMEM(pw_shape, BF16), pltpu.SemaphoreType.DMA((2,)), pltpu.SemaphoreType.DMA((2,))],
        vmem_bytes=est, args=[p, p, gates, o_ret, bias, scale, pw8, wru, wpu, wo, dh2, ret, qkvg, a_saved, b_saved], ride=ride)


def _mix_proj_backward(dq, dk, dv, dgr, dpooled, dgates, cos, sin, h1, gain, dh2, wmix8, ride=None):
    tm, nt = TOKEN_TILE, SEQ // TOKEN_TILE
    halo_blocks = tm // HALO
    last_halo = SEQ // HALO - 1
    k_scale = HEAD_DIM ** -0.5

    def body(dq_ref, dk_ref, dv_ref, dgr_ref, dpool_ref, dhalo_ref, dgates_ref, cos_ref, sin_ref, h1_ref, g_ref,
             dh2_ref, wmix_hbm, dh1_ref, dproj_ref, dg_ref, wmix, sem):
        i = pl.program_id(0)

        @pl.when(i == 0)
        def _():
            _load_mix_weight(wmix_hbm, wmix, sem)
            dg_ref[...] = jnp.zeros_like(dg_ref)

        cos_t, sin_t = cos_ref[...], sin_ref[...]
        for seg, ref, scale in ((0, dq_ref, 1.0), (1, dk_ref, k_scale)):
            for hd in range(HEADS):
                lo = hd * HEAD_DIM
                d1, d2 = ref[:, lo:lo + ROT_HALF], ref[:, lo + ROT_HALF:lo + HEAD_DIM]
                dproj_ref[:, pl.ds(seg * D_MODEL + lo, ROT_HALF)] = ((d1 * cos_t + d2 * sin_t) * scale).astype(BF16)
                dproj_ref[:, pl.ds(seg * D_MODEL + lo + ROT_HALF, ROT_HALF)] = ((d2 * cos_t - d1 * sin_t) * scale).astype(BF16)
        dproj_ref[:, pl.ds(2 * D_MODEL, D_MODEL)] = dv_ref[...].astype(BF16)
        dproj_ref[:, pl.ds(3 * D_MODEL, D_MODEL)] = dgr_ref[...]
        dp = _pooled_transpose(jnp.concatenate([dpool_ref[...], dhalo_ref[...]], axis=0), i * tm)
        for g in range(len(POOL_WINDOWS)):
            dproj_ref[:, pl.ds(4 * D_MODEL + g * POOL_GROUP_DIM, POOL_GROUP_DIM)] = dp[g].astype(BF16)
        dproj_ref[:, pl.ds(5 * D_MODEL, 2 * D_MODEL)] = dgates_ref[...]
        du = jnp.zeros((tm, D_MODEL), F32)
        for seg in range(N_SEG):
            cols = pl.ds(seg * D_MODEL, D_MODEL)
            du = du + _dot_nt(dproj_ref[:, cols], wmix[:, cols])
        g = g_ref[...]
        _, xhat, r = _rms(h1_ref[...], g)
        dg_ref[...] += jnp.sum(du * xhat, axis=0, keepdims=True)
        dh1_ref[...] = dh2_ref[...] + _rms_bwd(du * g, xhat, r)

    est = 2 * D_MODEL * N_SEG * D_MODEL + 2 * tm * D_MODEL * (3 * 4 + 2 + 4 + 4 + 4 + 4 + 4 + 14) + 12 * tm * D_MODEL * 4
    return _call(
        body, name="mix_proj_bwd", grid=(nt,),
        in_specs=[_row_spec(tm, D_MODEL), _row_spec(tm, D_MODEL), _row_spec(tm, D_MODEL), _row_spec(tm, D_MODEL),
                  _row_spec(tm, D_MODEL),
                  pl.BlockSpec((HALO, D_MODEL), lambda i: (jnp.minimum((i + 1) * halo_blocks, last_halo), 0)),
                  _row_spec(tm, 2 * D_MODEL), _row_spec(tm, ROT_HALF), _row_spec(tm, ROT_HALF),
                  _row_spec(tm, D_MODEL), _full_spec((1, D_MODEL)), _row_spec(tm, D_MODEL), ANY],
        out_specs=[_row_spec(tm, D_MODEL), _row_spec(tm, N_SEG * D_MODEL), _full_spec((1, D_MODEL))],
        out_shape=[jax.ShapeDtypeStruct((SEQ, D_MODEL), F32), jax.ShapeDtypeStruct((SEQ, N_SEG * D_MODEL), BF16),
                   jax.ShapeDtypeStruct((1, D_MODEL), F32)],
        scratch_shapes=[pltpu.VMEM((D_MODEL, N_SEG * D_MODEL), BF16), pltpu.SemaphoreType.DMA((N_DEV,))],
        vmem_bytes=est, args=[dq, dk, dv, dgr, dpooled, dpooled, dgates, cos, sin, h1, gain, dh2, wmix8], ride=ride)


def _adamw(w, parts, m, v, name):
    rows, cols = w.shape
    n_parts = parts.shape[0]
    tr = max([t for t in range(16, 257, 16) if rows % t == 0], default=rows)
    c1 = 1.0 - ADAM_B1 ** ADAM_STEP
    c2 = 1.0 - ADAM_B2 ** ADAM_STEP

    def body(w_ref, p_ref, m_ref, v_ref, g_out, d_out, m_out, v_out):
        g = p_ref[0].astype(F32)
        for k in range(1, n_parts):
            g = g + p_ref[k].astype(F32)
        m_new = ADAM_B1 * m_ref[...] + (1.0 - ADAM_B1) * g
        v_new = ADAM_B2 * v_ref[...] + (1.0 - ADAM_B2) * (g * g)
        g_out[...] = g
        m_out[...] = m_new
        v_out[...] = v_new
        d_out[...] = -ADAM_LR * ((m_new / c1) / (jnp.sqrt(v_new / c2) + ADAM_EPS) + ADAM_WD * w_ref[...])

    spec = pl.BlockSpec((tr, cols), lambda i: (i, 0))
    out = jax.ShapeDtypeStruct((rows, cols), F32)
    return pl.pallas_call(
        body, name=name, grid=(rows // tr,),
        in_specs=[spec, pl.BlockSpec((n_parts, tr, cols), lambda i: (0, i, 0)), spec, spec],
        out_specs=[spec] * 4, out_shape=[out] * 4,
        compiler_params=_params(2 * tr * cols * (7 * 4 + n_parts * parts.dtype.itemsize) + 8 * tr * cols * 4, 1),
    )(_in_hbm(w), _in_hbm(parts), _in_hbm(m), _in_hbm(v))


def _mix_w_in_grad(u, dproj, ride=None):
    return _weight_grad(
        u, dproj, N_DEV,
        lambda tt: pl.BlockSpec((tt, D_MODEL), lambda b, t: (t, 0)),
        lambda tt: pl.BlockSpec((tt, MIX_SHARD), lambda b, t: (t, b)),
        D_MODEL, MIX_SHARD, name="w_in_grad", ride=ride)


def kernel(x, norm_ffn1, ffn1_w_in, ffn1_w_out, norm_mix, w_in, gate_bias, pool_w, pool_scale, w_ret_up, w_pool_up, w_out, norm_ffn2, ffn2_w_in, ffn2_w_out, norm_final, loss_target, m_norm_ffn1, m_ffn1_w_in, m_ffn1_w_out, m_norm_mix, m_w_in, m_gate_bias, m_pool_w, m_pool_scale, m_w_ret_up, m_w_pool_up, m_w_out, m_norm_ffn2, m_ffn2_w_in, m_ffn2_w_out, m_norm_final, v_norm_ffn1, v_ffn1_w_in, v_ffn1_w_out, v_norm_mix, v_w_in, v_gate_bias, v_pool_w, v_pool_scale, v_w_ret_up, v_w_pool_up, v_w_out, v_norm_ffn2, v_ffn2_w_in, v_ffn2_w_out, v_norm_final):
    assert x.shape == (1, SEQ, D_MODEL) and ffn1_w_in.shape == (1, D_MODEL, FF_SHARD) and w_in.shape == (1, D_MODEL, MIX_SHARD)
    x2, target = x[0], loss_target[0]

    cos, sin = _rotary_tables()
    tables = _retention_tables()
    big = [ffn1_w_in, ffn1_w_out, w_in, pool_w, w_ret_up, w_pool_up, w_out, ffn2_w_in, ffn2_w_out]
    bf = lambda w: w[0].astype(BF16)
    bf_t = lambda w: jnp.swapaxes(w[0], 0, 1).astype(BF16)
    square = lambda w: w.reshape(D_MODEL, D_MODEL)

    win1, wout1, bias8 = _alone(_RelayGather([bf_t(ffn1_w_in), bf(ffn1_w_out), gate_bias[0]]), "ffn1_weights_all_gather")
    wout1 = wout1.reshape(N_FF_GROUPS, FF_SHARD, D_MODEL)
    bias = bias8.transpose(1, 0, 2).reshape(2, D_MODEL)

    (h1, gu1), (wmix8,) = _ffn_forward(x2, norm_ffn1, win1, wout1, "ffn1_fwd", ride=_GatherRide([bf(w_in)]))
    (u, qkvg, p, gates), (win2,) = _mix_proj_forward(h1, norm_mix, wmix8, cos, sin, ride=_GatherRide([bf_t(ffn2_w_in)]))
    (ret, o_ret), (pw8, wru, wpu, wo) = _retention_forward(
        qkvg, tables, ride=_GatherRide([bf(pool_w), bf(w_ret_up), bf(w_pool_up), bf(w_out)]))
    wru, wpu, wo = square(wru), square(wpu), square(wo)
    (h2, a_saved, b_saved), (wout2,) = _mix_tail_forward(p, gates, o_ret, h1, bias, pool_scale, pw8, wru, wpu, wo,
                                        ride=_GatherRide([bf(ffn2_w_out)]))
    wout2 = wout2.reshape(N_FF_GROUPS, FF_SHARD, D_MODEL)
    (dh3, gu2, loss_part, d_norm_final), _ = _ffn_forward(h2, norm_ffn2, win2, wout2, "ffn2_fwd_loss",
                                                          head=(target, norm_final.reshape(1, D_MODEL)))

    (dh2, dgu2, act2, xn2, df2, d_norm_ffn2), _ = _ffn_backward(dh3, h2, norm_ffn2, gu2, win2, wout2, "ffn2_bwd")
    d_wout2, _ = _ffn_w_out_grad(act2, df2, 2)
    d_win2, (r_wout2,) = _ffn_w_in_grad(xn2, dgu2, 2, ride=_ScatterRide([d_wout2]))
    (dret, dgr, dgates, dpooled, d_wo, d_wru, d_wpu, d_pw, d_bias, d_scale), (r_win2,) = _mix_tail_backward(
        dh2, p, gates, o_ret, ret, qkvg, a_saved, b_saved, bias, pool_scale, pw8, wru, wpu, wo, ride=_ScatterRide([d_win2]))
    (dq,), _ = _retention_backward_q(qkvg, dret, tables)
    (dk, dv), (r_pw, r_wru, r_wpu, r_wo) = _retention_backward_kv(
        qkvg, dret, tables, ride=_ScatterRide([d_pw, d_wru, d_wpu, d_wo]))
    (dh1, dproj, d_norm_mix), _ = _mix_proj_backward(dq, dk, dv, dgr, dpooled, dgates, cos, sin, h1, norm_mix, dh2, wmix8)
    d_wmix, _ = _mix_w_in_grad(u, dproj)
    (grad_x, dgu1, act1, xn1, df1, d_norm_ffn1), (r_wmix,) = _ffn_backward(
        dh1, x2, norm_ffn1, gu1, win1, wout1, "ffn1_bwd", ride=_ScatterRide([d_wmix]))
    d_win1, _ = _ffn_w_in_grad(xn1, dgu1, 1)
    d_wout1, (r_win1,) = _ffn_w_out_grad(act1, df1, 1, ride=_ScatterRide([d_win1]))
    (r_wout1,) = _alone(_ScatterRide([d_wout1]), "ffn1_w_out_grad_reduce_scatter")
    received = [r_win1, r_wout1, r_wmix, r_pw, r_wru, r_wpu, r_wo, r_win2, r_wout2]
    zero_row = jnp.zeros((1, D_MODEL), F32)
    small = _all_reduce_rows(jnp.concatenate(
        [d_norm_ffn1, d_norm_mix, d_scale, d_norm_ffn2, d_norm_final, d_bias, jnp.tile(loss_part, (1, D_MODEL // 128))],
        axis=0))
    loss = small[7, 0]

    names = ["ffn1_w_in", "ffn1_w_out", "w_in", "pool_w", "w_ret_up", "w_pool_up", "w_out", "ffn2_w_in", "ffn2_w_out"]
    moments_m = [m_ffn1_w_in, m_ffn1_w_out, m_w_in, m_pool_w, m_w_ret_up, m_w_pool_up, m_w_out, m_ffn2_w_in, m_ffn2_w_out]
    moments_v = [v_ffn1_w_in, v_ffn1_w_out, v_w_in, v_pool_w, v_w_ret_up, v_w_pool_up, v_w_out, v_ffn2_w_in, v_ffn2_w_out]
    results = {}
    for nm, w, parts, m, v in zip(names, big, received, moments_m, moments_v):
        if nm in ("ffn1_w_in", "ffn2_w_in"):
            flat, back = (lambda a: jnp.swapaxes(a[0], 0, 1)), (lambda o: jnp.swapaxes(o, 0, 1)[None])
        else:
            flat, back = (lambda a, w=w: a.reshape(-1, w.shape[-1])), (lambda o, w=w: o.reshape(w.shape))
        outs = _adamw(flat(w), parts.reshape(parts.shape[:1] + flat(w).shape), flat(m), flat(v), name=f"adamw_{nm}")
        results[nm] = [back(o) for o in outs]

    my_id = _linear_id(*_my_position())
    bias_cols = gate_bias.shape[-1]
    pad = lambda a: jnp.pad(a[0], ((0, 0), (0, D_MODEL - bias_cols)))
    pack = lambda a, b, c, d, e, gb: jnp.concatenate([a, b, c, d, e.reshape(1, D_MODEL), pad(gb), zero_row], axis=0)
    d_bias_mine = lax.dynamic_slice_in_dim(small[5:7], my_id * bias_cols, bias_cols, axis=1)
    g_small = jnp.concatenate([small[0:5], jnp.pad(d_bias_mine, ((0, 0), (0, D_MODEL - bias_cols))), zero_row], axis=0)
    s_outs = _adamw(pack(norm_ffn1, norm_mix, pool_scale, norm_ffn2, norm_final, gate_bias), g_small[None],
                    pack(m_norm_ffn1, m_norm_mix, m_pool_scale, m_norm_ffn2, m_norm_final, m_gate_bias),
                    pack(v_norm_ffn1, v_norm_mix, v_pool_scale, v_norm_ffn2, v_norm_final, v_gate_bias), name="adamw_small")
    for row, nm in enumerate(["norm_ffn1", "norm_mix", "pool_scale", "norm_ffn2"]):
        results[nm] = [o[row:row + 1] for o in s_outs]
    results["norm_final"] = [o[4] for o in s_outs]
    results["gate_bias"] = [o[5:7, :bias_cols][None] for o in s_outs]

    order = ["norm_ffn1", "ffn1_w_in", "ffn1_w_out", "norm_mix", "w_in", "gate_bias", "pool_w", "pool_scale",
             "w_ret_up", "w_pool_up", "w_out", "norm_ffn2", "ffn2_w_in", "ffn2_w_out", "norm_final"]
    return (loss, grad_x[None], *[results[nm][0] for nm in order], *[results[nm][1] for nm in order],
            *[results[nm][2] for nm in order], *[results[nm][3] for nm in order])
```

```python
import functools

import numpy as np
import jax
import jax.numpy as jnp
from jax import lax
from jax.experimental import pallas as pl
from jax.experimental.pallas import tpu as pltpu

F32 = jnp.float32
BF16 = jnp.bfloat16

N_DEV = 8
D_MODEL = 1024
SEQ = 4096
D_FF = 2816
FF_SHARD = 2 * D_FF // N_DEV
N_FF_GROUPS = N_DEV // 2
HEADS = 4
HEAD_DIM = 256
ROT_HALF = HEAD_DIM // 2
CHUNK = 64
RET_BLOCK = 256
POOL_WINDOWS = (2, 4, 8, 16)
POOL_GROUP_DIM = 256
HALO = 16
MIX_SHARD = 7 * D_MODEL // N_DEV
N_SEG = 7
ROPE_BASE = 10000.0
NORM_EPS = 1e-6
FFN_RES_WEIGHT = 0.5
ADAM_LR, ADAM_B1, ADAM_B2, ADAM_EPS, ADAM_WD, ADAM_STEP = 0.001, 0.9, 0.999, 1e-08, 0.01, 10

TOKEN_TILE = 256
WIDE_TILE = 512
VMEM_CAP_V7X = 64 * 1024 * 1024
MESH = pl.DeviceIdType.MESH
ANY = pl.BlockSpec(memory_space=pl.ANY)


def _vmem_limit(estimate_bytes):
    return int(min(estimate_bytes * 5 // 4 + (6 << 20), VMEM_CAP_V7X - (4 << 20)))


def _params(estimate_bytes, n_grid):
    return pltpu.CompilerParams(dimension_semantics=("arbitrary",) * n_grid,
                                vmem_limit_bytes=_vmem_limit(estimate_bytes))


def _dot(a, b):
    return jnp.dot(a, b, preferred_element_type=F32)


def _dot_nt(a, b):
    return lax.dot_general(a, b, (((1,), (1,)), ((), ())), preferred_element_type=F32)


def _dot_tn(a, b):
    return lax.dot_general(a, b, (((0,), (0,)), ((), ())), preferred_element_type=F32)


def _sig(x):
    return 1.0 / (1.0 + jnp.exp(-x))


def _rms(x, g):
    r = lax.rsqrt(jnp.mean(x * x, axis=-1, keepdims=True) + NORM_EPS)
    xhat = x * r
    return xhat * g, xhat, r


def _rms_bwd(dyg, xhat, r):
    return r * (dyg - xhat * jnp.mean(dyg * xhat, axis=-1, keepdims=True))


def _row_spec(tile, width, col=0):
    return pl.BlockSpec((tile, width), lambda i, c=col: (i, c))


def _full_spec(shape):
    return pl.BlockSpec(shape, lambda *_: (0,) * len(shape))


def _rotary_tables():
    inv_freq = (np.float32(ROPE_BASE) ** (-np.arange(ROT_HALF, dtype=np.float32) / np.float32(ROT_HALF))).astype(np.float32)
    ang = (np.arange(SEQ, dtype=np.float32)[:, None] * inv_freq[None, :]).astype(np.float32)
    return jnp.asarray(np.cos(ang.astype(np.float64)), F32), jnp.asarray(np.sin(ang.astype(np.float64)), F32)


def _retention_tables():
    log_gamma = np.log(1.0 - 2.0 ** (-5.0 - np.arange(HEADS, dtype=np.float64)))
    n = np.arange(RET_BLOCK)
    diff = (n[:, None] - n[None, :]).astype(np.float64)
    same = (n[:, None] // CHUNK) == (n[None, :] // CHUNK)
    earlier = (n[None, :] // CHUNK) < (n[:, None] // CHUNK)
    expo = np.where(same, np.abs(diff), diff)
    mask = np.where(same | earlier, np.exp(log_gamma[:, None, None] * expo[None]), 0.0)
    qdec = np.exp(log_gamma[:, None] * (n[None, :] + 1.0))[:, :, None]
    kdec = np.exp(log_gamma[:, None] * (RET_BLOCK - 1.0 - n[None, :]))[:, :, None]
    cdec = np.exp(log_gamma * RET_BLOCK)[:, None, None]
    return (jnp.asarray(mask, F32), jnp.asarray(qdec, F32), jnp.asarray(kdec, F32), jnp.asarray(cdec, F32))


def _my_position():
    return lax.axis_index("x"), lax.axis_index("y"), lax.axis_index("c")


def _linear_id(px, py, pc):
    return 4 * px + 2 * py + pc


def _when(pred, fn):
    if isinstance(pred, bool):
        if pred:
            fn()
    else:
        pl.when(pred)(fn)


class _GatherRide:
    def __init__(self, shards):
        self.args = list(shards)
        n = self.n = len(shards)
        self.out_shape = [pltpu.HBM((N_DEV,) + s.shape, s.dtype) for s in shards]
        self.scratch = [pltpu.SemaphoreType.DMA((n, 7)), pltpu.SemaphoreType.DMA((n, 7)), pltpu.SemaphoreType.DMA((n,))]

    def _plan(self, src, out, sems):
        send_sems, recv_sems, local_sem = sems
        x, y, c = _my_position()
        me, sibling = (x, y, c), (x, y, 1 - c)
        chips = [(1 - x, y), (x, 1 - y), (1 - x, 1 - y)]

        def copy(t, k, block, to, from_src=False):
            rows = out[t].at[_linear_id(*block)]
            return pltpu.make_async_remote_copy(
                src_ref=src[t] if from_src else rows, dst_ref=rows,
                send_sem=send_sems.at[t, k], recv_sem=recv_sems.at[t, k],
                device_id=to, device_id_type=MESH)

        local = [pltpu.make_async_copy(src[t], out[t].at[_linear_id(*me)], local_sem.at[t]) for t in range(self.n)]
        return copy, local, me, sibling, chips, c

    def begin(self, first, src, out, sems):
        copy, local, me, sibling, chips, c = self._plan(src, out, sems)

        def start():
            for cp in local:
                cp.start()
            for t in range(self.n):
                copy(t, 0, me, sibling, from_src=True).start()
                for j, chip in enumerate(chips):
                    copy(t, 1 + j, me, (*chip, c), from_src=True).start()

        _when(first, start)

    def finish(self, mid, last, src, out, sems):
        copy, local, me, sibling, chips, c = self._plan(src, out, sems)

        def pass_on():
            for j, chip in enumerate(chips):
                for t in range(self.n):
                    copy(t, 1 + j, (*chip, c), me).wait_recv()
                    copy(t, 4 + j, (*chip, c), sibling).start()

        def drain():
            for t in range(self.n):
                copy(t, 0, sibling, me).wait_recv()
                for j, chip in enumerate(chips):
                    copy(t, 4 + j, (*chip, 1 - c), me).wait_recv()
            for t in range(self.n):
                copy(t, 0, me, sibling, from_src=True).wait_send()
                for j, chip in enumerate(chips):
                    copy(t, 1 + j, me, (*chip, c), from_src=True).wait_send()
                    copy(t, 4 + j, (*chip, c), sibling).wait_send()
            for cp in local:
                cp.wait()

        _when(mid, pass_on)
        _when(last, drain)


class _RelayGather(_GatherRide):
    def begin(self, first, src, out, sems):
        copy, local, me, sibling, chips, c = self._plan(src, out, sems)
        for cp in local:
            cp.start()
        for t in range(self.n):
            copy(t, 0, me, sibling, from_src=True).start()
            for j in range(2):
                copy(t, 1 + j, me, (*chips[j], c), from_src=True).start()

    def finish(self, mid, last, src, out, sems):
        copy, local, me, sibling, chips, c = self._plan(src, out, sems)
        x, y, _ = me
        relayed_from = (x ^ (1 - c), y ^ c, c)
        relayed_to = (x ^ c, y ^ (1 - c), c)
        for t in range(self.n):
            for j in range(2):
                copy(t, 1 + j, (*chips[j], c), me).wait_recv()
            copy(t, 3, relayed_from, relayed_to).start()
            for j in range(2):
                copy(t, 4 + j, (*chips[j], c), sibling).start()
        for t in range(self.n):
            copy(t, 3, (*chips[2], c), me).wait_recv()
            copy(t, 6, (*chips[2], c), sibling).start()
        for t in range(self.n):
            copy(t, 0, sibling, me).wait_recv()
            for j in range(3):
                copy(t, 4 + j, (*chips[j], 1 - c), me).wait_recv()
        for t in range(self.n):
            copy(t, 0, me, sibling, from_src=True).wait_send()
            for j in range(2):
                copy(t, 1 + j, me, (*chips[j], c), from_src=True).wait_send()
            copy(t, 3, relayed_from, relayed_to).wait_send()
            for j in range(3):
                copy(t, 4 + j, (*chips[j], c), sibling).wait_send()
        for cp in local:
            cp.wait()


class _ScatterRide:
    def __init__(self, chip_sums):
        self.args = list(chip_sums)
        n = self.n = len(chip_sums)
        self.out_shape = [pltpu.HBM(p.shape, p.dtype) for p in chip_sums]
        self.scratch = [pltpu.SemaphoreType.DMA((n, 3)), pltpu.SemaphoreType.DMA((n, 3)), pltpu.SemaphoreType.DMA((n,))]

    def _plan(self, src, out, sems):
        send_sems, recv_sems, local_sem = sems
        x, y, c = _my_position()

        def peer(k):
            return (x ^ (k >> 1), y ^ (k & 1))

        copies = [pltpu.make_async_remote_copy(
            src_ref=src[t].at[2 * peer(k)[0] + peer(k)[1]], dst_ref=out[t].at[k],
            send_sem=send_sems.at[t, k - 1], recv_sem=recv_sems.at[t, k - 1],
            device_id=(*peer(k), c), device_id_type=MESH) for t in range(self.n) for k in range(1, N_DEV // 2)]
        local = [pltpu.make_async_copy(src[t].at[2 * x + y], out[t].at[0], local_sem.at[t]) for t in range(self.n)]
        return copies, local

    def begin(self, first, src, out, sems):
        copies, local = self._plan(src, out, sems)

        def start():
            for cp in local + copies:
                cp.start()

        _when(first, start)

    def finish(self, mid, last, src, out, sems):
        copies, local = self._plan(src, out, sems)

        def drain():
            for cp in copies:
                cp.wait_recv()
            for cp in copies:
                cp.wait_send()
            for cp in local:
                cp.wait()

        _when(last, drain)


def _in_hbm(a):
    return pltpu.with_memory_space_constraint(a, pltpu.HBM)


def _call(body, *, name, grid, in_specs, out_specs, out_shape, scratch_shapes, vmem_bytes, args, ride=None, after=None):
    n_in, n_out, n_s = len(in_specs), len(out_specs), len(scratch_shapes)
    params = _params(vmem_bytes, len(grid))
    args = [_in_hbm(a) for a in args]
    out_shape = [pltpu.HBM(s.shape, s.dtype) for s in out_shape]
    if ride is None:
        if after is not None:
            def ordered_body(*refs):
                body(*refs[:n_in], *refs[n_in + 1:])
            outs = pl.pallas_call(ordered_body, name=name, grid=grid, in_specs=list(in_specs) + [ANY], out_specs=out_specs,
                                  out_shape=out_shape, scratch_shapes=scratch_shapes, compiler_params=params)(*args, after)
            return list(outs), []
        outs = pl.pallas_call(body, name=name, grid=grid, in_specs=in_specs, out_specs=out_specs, out_shape=out_shape,
                              scratch_shapes=scratch_shapes, compiler_params=params)(*args)
        return list(outs), []
    total = int(np.prod(grid))

    def riding_body(*refs):
        a = n_in
        b = a + ride.n
        c = b + n_out
        d = c + ride.n
        e = d + n_s
        step = pl.program_id(0)
        for axis in range(1, len(grid)):
            step = step * grid[axis] + pl.program_id(axis)
        ride.begin(step == 0, refs[a:b], refs[c:d], refs[e:])
        body(*refs[:a], *refs[b:c], *refs[d:e])
        ride.finish(step == (3 * total) // 4, step == total - 1, refs[a:b], refs[c:d], refs[e:])

    outs = pl.pallas_call(
        riding_body, name=name, grid=grid, in_specs=list(in_specs) + [ANY] * ride.n,
        out_specs=list(out_specs) + [ANY] * ride.n, out_shape=list(out_shape) + ride.out_shape,
        scratch_shapes=list(scratch_shapes) + ride.scratch, compiler_params=params)(*args, *[_in_hbm(a) for a in ride.args])
    return list(outs[:n_out]), list(outs[n_out:])


def _alone(ride, name):
    def body(*refs):
        src, out, sems = refs[:ride.n], refs[ride.n:2 * ride.n], refs[2 * ride.n:]
        ride.begin(True, src, out, sems)
        ride.finish(True, True, src, out, sems)

    return list(pl.pallas_call(body, name=name, out_shape=ride.out_shape, in_specs=[ANY] * ride.n,
                               out_specs=[ANY] * ride.n, scratch_shapes=ride.scratch)(*[_in_hbm(a) for a in ride.args]))


def _scatter_copies(src, land, send_sems, recv_sems):
    x, y, c = _my_position()
    copies = []
    for k in range(1, N_DEV // 2):
        px, py = x ^ (k >> 1), y ^ (k & 1)
        copies.append(pltpu.make_async_remote_copy(
            src_ref=src.at[2 * px + py], dst_ref=land.at[k - 1], send_sem=send_sems.at[k - 1], recv_sem=recv_sems.at[k - 1],
            device_id=(px, py, c), device_id_type=MESH))
    return copies


def _scatter_start(chip_sums, name):
    n_peers = N_DEV // 2 - 1
    land_shape = (n_peers,) + chip_sums.shape[1:]
    hbm = pl.BlockSpec(memory_space=pltpu.HBM)
    sem = pl.BlockSpec(memory_space=pltpu.SEMAPHORE)

    def body(src_ref, land_ref, send_sems, recv_sems, src_thru, land_thru, token):
        for cp in _scatter_copies(src_ref, land_ref, send_sems, recv_sems):
            cp.start()
        token[...] = jnp.zeros_like(token)

    send_sems, recv_sems, src_thru, land_thru, token = pl.pallas_call(
        body, name=name,
        out_shape=(pltpu.SemaphoreType.DMA((n_peers,)), pltpu.SemaphoreType.DMA((n_peers,)),
                   pltpu.HBM(chip_sums.shape, chip_sums.dtype), pltpu.HBM(land_shape, chip_sums.dtype),
                   jax.ShapeDtypeStruct((8, 128), F32)),
        in_specs=(hbm, hbm), out_specs=(sem, sem, hbm, hbm, pl.BlockSpec(memory_space=pltpu.VMEM)),
        input_output_aliases={0: 2, 1: 3},
        compiler_params=pltpu.CompilerParams(has_side_effects=pltpu.SideEffectType.DATAFLOW_SIDE_EFFECTING),
    )(_in_hbm(chip_sums), _in_hbm(lax.empty(land_shape, chip_sums.dtype)))
    return (send_sems, recv_sems, src_thru, land_thru), token


def _scatter_wait(state, after, name):
    send_sems, recv_sems, src_thru, land_thru = state
    hbm = pl.BlockSpec(memory_space=pltpu.HBM)
    sem = pl.BlockSpec(memory_space=pltpu.SEMAPHORE)

    def body(src_ref, land_ref, send_sems, recv_sems, after_ref, src_out, land_out):
        for cp in _scatter_copies(src_ref, land_ref, send_sems, recv_sems):
            cp.wait_send()
            cp.wait_recv()

    src_done, land_done = pl.pallas_call(
        body, name=name,
        out_shape=(pltpu.HBM(src_thru.shape, src_thru.dtype), pltpu.HBM(land_thru.shape, land_thru.dtype)),
        in_specs=(hbm, hbm, sem, sem, ANY), out_specs=(hbm, hbm), input_output_aliases={0: 0, 1: 1},
        compiler_params=pltpu.CompilerParams(has_side_effects=pltpu.SideEffectType.DATAFLOW_SIDE_EFFECTING),
    )(src_thru, land_thru, send_sems, recv_sems, after)
    x, y, _ = _my_position()
    return lax.dynamic_slice_in_dim(src_done, 2 * x + y, 1, axis=0), land_done


def _all_reduce_rows(block):
    rows, width = block.shape

    def body(x_ref, sum_ref, gathered, send_sems, recv_sems, local_sem):
        x, y, c = _my_position()
        me, sibling = (x, y, c), (x, y, 1 - c)
        chips = [(1 - x, y), (x, 1 - y), (1 - x, 1 - y)]

        def slot(px, py, pc):
            return gathered.at[_linear_id(px, py, pc)]

        def copy(k, block_of, to, from_src=False):
            return pltpu.make_async_remote_copy(
                src_ref=x_ref if from_src else slot(*block_of), dst_ref=slot(*block_of),
                send_sem=send_sems.at[k], recv_sem=recv_sems.at[k], device_id=to, device_id_type=MESH)

        mine = pltpu.make_async_copy(x_ref, slot(*me), local_sem)
        mine.start()
        first = [copy(0, me, sibling, from_src=True)]
        first += [copy(1 + j, me, (*chip, c), from_src=True) for j, chip in enumerate(chips)]
        for cp in first:
            cp.start()
        passed = [copy(4 + j, (*chip, c), sibling) for j, chip in enumerate(chips)]
        for j, chip in enumerate(chips):
            copy(1 + j, (*chip, c), me).wait_recv()
            passed[j].start()
        copy(0, sibling, me).wait_recv()
        for j, chip in enumerate(chips):
            copy(4 + j, (*chip, 1 - c), me).wait_recv()
        for cp in first + passed:
            cp.wait_send()
        mine.wait()
        total = gathered[0]
        for d in range(1, N_DEV):
            total = total + gathered[d]
        sum_ref[...] = total

    return pl.pallas_call(
        body, name="small_grads_all_reduce",
        out_shape=jax.ShapeDtypeStruct((rows, width), F32),
        in_specs=[pl.BlockSpec(memory_space=pltpu.VMEM)],
        out_specs=pl.BlockSpec(memory_space=pltpu.VMEM),
        scratch_shapes=[pltpu.VMEM((N_DEV, rows, width), F32),
                        pltpu.SemaphoreType.DMA((7,)), pltpu.SemaphoreType.DMA((7,)), pltpu.SemaphoreType.DMA],
    )(block)


def _load_ffn_weights(win_hbm, wout_hbm, win, wout, sem):
    a = pltpu.make_async_copy(win_hbm, win, sem.at[0])
    b = pltpu.make_async_copy(wout_hbm, wout, sem.at[1])
    a.start()
    b.start()
    a.wait()
    b.wait()


def _ffn_forward(h_in, gain, win8, wout, name, head=None, ride=None):
    tm, nt = WIDE_TILE, SEQ // WIDE_TILE

    def body(*refs):
        if head is None:
            x_ref, g_ref, win_hbm, wout_hbm, out_ref, gu_ref, win, wout, sem = refs
        else:
            x_ref, g_ref, win_hbm, wout_hbm, tgt_ref, gf_ref, out_ref, gu_ref, loss_ref, dgf_ref, win, wout, sem = refs
        i = pl.program_id(0)

        @pl.when(i == 0)
        def _():
            _load_ffn_weights(win_hbm, wout_hbm, win, wout, sem)
            if head is not None:
                loss_ref[...] = jnp.zeros_like(loss_ref)
                dgf_ref[...] = jnp.zeros_like(dgf_ref)

        x = x_ref[...]
        xn, _, _ = _rms(x, g_ref[...])
        xb = xn.astype(BF16)
        acc = jnp.zeros((tm, D_MODEL), F32)
        for j in range(N_FF_GROUPS):
            gate = _dot_nt(xb, win[j])
            up = _dot_nt(xb, win[j + N_FF_GROUPS])
            gu_ref[j] = gate.astype(BF16)
            gu_ref[j + N_FF_GROUPS] = up.astype(BF16)
            act = gate * _sig(gate) * up
            acc = acc + _dot(act.astype(BF16), wout[j])
        h = x + FFN_RES_WEIGHT * acc
        if head is None:
            out_ref[...] = h
        else:
            gf = gf_ref[...]
            y, hhat, r = _rms(h, gf)
            err = y - tgt_ref[...]
            loss_ref[...] += jnp.full(loss_ref.shape, 0.5 / D_MODEL * jnp.sum(err * err), F32)
            dy = err * (1.0 / D_MODEL)
            dgf_ref[...] += jnp.sum(dy * hhat, axis=0, keepdims=True)
            out_ref[...] = _rms_bwd(dy * gf, hhat, r)

    weights = 2 * D_MODEL * 2 * D_FF + 2 * D_FF * D_MODEL
    tiles = 2 * (2 * 4 * tm * D_MODEL + 2 * tm * 2 * D_FF) + (2 * 4 * tm * D_MODEL if head else 0)
    in_specs = [_row_spec(tm, D_MODEL), _full_spec((1, D_MODEL)), ANY, ANY]
    out_shape = [jax.ShapeDtypeStruct((SEQ, D_MODEL), F32), jax.ShapeDtypeStruct((N_DEV, SEQ, FF_SHARD), BF16)]
    out_specs = [_row_spec(tm, D_MODEL), pl.BlockSpec((N_DEV, tm, FF_SHARD), lambda i: (0, i, 0))]
    args = [h_in, gain, win8, wout]
    if head is not None:
        in_specs += [_row_spec(tm, D_MODEL), _full_spec((1, D_MODEL))]
        out_shape += [jax.ShapeDtypeStruct((1, 128), F32), jax.ShapeDtypeStruct((1, D_MODEL), F32)]
        out_specs += [_full_spec((1, 128)), _full_spec((1, D_MODEL))]
        args += list(head)
    return _call(
        body, name=name, grid=(nt,), in_specs=in_specs, out_specs=out_specs, out_shape=out_shape,
        scratch_shapes=[pltpu.VMEM((N_DEV, FF_SHARD, D_MODEL), BF16), pltpu.VMEM((N_FF_GROUPS, FF_SHARD, D_MODEL), BF16),
                        pltpu.SemaphoreType.DMA((2,))],
        vmem_bytes=weights + tiles + 16 * tm * FF_SHARD * 4, args=args, ride=ride)


def _ffn_backward(dh_out, h_in, gain, gu, win8, wout, name, ride=None):
    tm, nt = TOKEN_TILE, SEQ // TOKEN_TILE

    def body(dh_ref, x_ref, g_ref, gu_ref, win_hbm, wout_hbm,
             dhin_ref, dgu_ref, act_ref, xn_ref, df_ref, dg_ref, win, wout, sem):
        i = pl.program_id(0)

        @pl.when(i == 0)
        def _():
            _load_ffn_weights(win_hbm, wout_hbm, win, wout, sem)
            dg_ref[...] = jnp.zeros_like(dg_ref)

        dh = dh_ref[...]
        g = g_ref[...]
        xn, xhat, r = _rms(x_ref[...], g)
        df = (FFN_RES_WEIGHT * dh).astype(BF16)
        dxn = jnp.zeros((tm, D_MODEL), F32)
        for j in range(N_FF_GROUPS):
            gate = gu_ref[j].astype(F32)
            up = gu_ref[j + N_FF_GROUPS].astype(F32)
            dact = _dot_nt(df, wout[j])
            s = _sig(gate)
            silu = gate * s
            dgate = (dact * up * (s * (1.0 + gate * (1.0 - s)))).astype(BF16)
            dup = (dact * silu).astype(BF16)
            act_ref[j] = (silu * up).astype(BF16)
            dgu_ref[j] = dgate
            dgu_ref[j + N_FF_GROUPS] = dup
            dxn = dxn + _dot(dgate, win[j]) + _dot(dup, win[j + N_FF_GROUPS])
        dg_ref[...] += jnp.sum(dxn * xhat, axis=0, keepdims=True)
        dhin_ref[...] = dh + _rms_bwd(dxn * g, xhat, r)
        xn_ref[...] = xn.astype(BF16)
        df_ref[...] = df

    weights = 2 * D_MODEL * 2 * D_FF + 2 * D_FF * D_MODEL
    tiles = 2 * (3 * 4 * tm * D_MODEL + 2 * tm * (2 * 2 * D_FF + D_FF) + 2 * 2 * tm * D_MODEL)
    gu_spec = pl.BlockSpec((N_DEV, tm, FF_SHARD), lambda i: (0, i, 0))
    return _call(
        body, name=name, grid=(nt,),
        in_specs=[_row_spec(tm, D_MODEL), _row_spec(tm, D_MODEL), _full_spec((1, D_MODEL)), gu_spec, ANY, ANY],
        out_specs=[_row_spec(tm, D_MODEL), gu_spec, pl.BlockSpec((N_FF_GROUPS, tm, FF_SHARD), lambda i: (0, i, 0)),
                   _row_spec(tm, D_MODEL), _row_spec(tm, D_MODEL), _full_spec((1, D_MODEL))],
        out_shape=[jax.ShapeDtypeStruct((SEQ, D_MODEL), F32), jax.ShapeDtypeStruct((N_DEV, SEQ, FF_SHARD), BF16),
                   jax.ShapeDtypeStruct((N_FF_GROUPS, SEQ, FF_SHARD), BF16), jax.ShapeDtypeStruct((SEQ, D_MODEL), BF16),
                   jax.ShapeDtypeStruct((SEQ, D_MODEL), BF16), jax.ShapeDtypeStruct((1, D_MODEL), F32)],
        scratch_shapes=[pltpu.VMEM((N_DEV, FF_SHARD, D_MODEL), BF16), pltpu.VMEM((N_FF_GROUPS, FF_SHARD, D_MODEL), BF16),
                        pltpu.SemaphoreType.DMA((2,))],
        vmem_bytes=weights + tiles + 20 * tm * FF_SHARD * 4, args=[dh_out, h_in, gain, gu, win8, wout], ride=ride)


def _to_sibling(src, dst, send_sem, recv_sem):
    x, y, c = _my_position()
    return pltpu.make_async_remote_copy(src_ref=src, dst_ref=dst, send_sem=send_sem, recv_sem=recv_sem,
                                        device_id=(x, y, 1 - c), device_id_type=MESH)


def _weight_grad(x, g, n_out, x_spec, g_spec, k_dim, n_dim, name, halves=False, tt=1024, ride=None, after=None):
    nt = SEQ // tt
    n_chips = N_DEV // 2
    rows = k_dim // 2 if halves else k_dim

    def body(x_ref, g_ref, out_ref, acc, sendbuf, recvbuf, send_sems, recv_sems):
        b, t = pl.program_id(0), pl.program_id(1)
        c = lax.axis_index("c")

        def push(q):
            return _to_sibling(sendbuf.at[q], recvbuf.at[q], send_sems.at[q], recv_sems.at[q])

        @pl.when(t == 0)
        def _():
            acc[...] = jnp.zeros_like(acc)

        acc[...] += _dot_tn(x_ref[...], g_ref[...])

        @pl.when(t == nt - 1)
        def _():
            if halves:
                for mine, other in ((0, 1), (1, 0)):
                    @pl.when(c == mine)
                    def _():
                        out_ref[b] = acc[pl.ds(mine * rows, rows), :].astype(BF16)
                        sendbuf[b] = acc[pl.ds(other * rows, rows), :].astype(BF16)
                push(b).start()
            else:
                q = b // 2

                @pl.when(b % 2 == c)
                def _():
                    out_ref[q] = acc[...].astype(BF16)

                @pl.when(b % 2 != c)
                def _():
                    sendbuf[q] = acc[...].astype(BF16)
                    push(q).start()

        @pl.when((b == n_out - 1) & (t == nt - 1))
        def _():
            for q in range(n_chips):
                push(q).wait_recv()
                out_ref[q] = (out_ref[q].astype(F32) + recvbuf[q].astype(F32)).astype(BF16)
            for q in range(n_chips):
                push(q).wait_send()

    piece = (n_chips, rows, n_dim)
    outs, ride_outs = _call(
        body, name=name, grid=(n_out, nt), in_specs=[x_spec(tt), g_spec(tt)],
        out_specs=[pl.BlockSpec(piece, lambda b, t: (0, 0, 0))],
        out_shape=[jax.ShapeDtypeStruct(piece, BF16)],
        scratch_shapes=[pltpu.VMEM((k_dim, n_dim), F32), pltpu.VMEM(piece, BF16), pltpu.VMEM(piece, BF16),
                        pltpu.SemaphoreType.DMA((n_chips,)), pltpu.SemaphoreType.DMA((n_chips,))],
        vmem_bytes=2 * 2 * tt * (k_dim + n_dim) + 8 * k_dim * n_dim + 4 * 2 * n_chips * rows * n_dim, args=[x, g], ride=ride,
        after=after)
    return outs[0], ride_outs


def _ffn_w_out_grad(act, df, tag, ride=None, after=None):
    return _weight_grad(
        act, df, N_FF_GROUPS,
        lambda tt: pl.BlockSpec((None, tt, FF_SHARD), lambda b, t: (b, t, 0)),
        lambda tt: pl.BlockSpec((tt, D_MODEL), lambda b, t: (t, 0)),
        FF_SHARD, D_MODEL, name=f"ffn{tag}_w_out_grad", halves=True, ride=ride, after=after)


def _ffn_w_in_grad(xn, dgu, tag, ride=None):
    return _weight_grad(
        dgu, xn, N_DEV,
        lambda tt: pl.BlockSpec((None, tt, FF_SHARD), lambda b, t: (b, t, 0)),
        lambda tt: pl.BlockSpec((tt, D_MODEL), lambda b, t: (t, 0)),
        FF_SHARD, D_MODEL, name=f"ffn{tag}_w_in_grad", ride=ride)


def _load_mix_weight(wmix_hbm, wmix, sem):
    copies = [pltpu.make_async_copy(wmix_hbm.at[d], wmix.at[:, pl.ds(d * MIX_SHARD, MIX_SHARD)], sem.at[d])
              for d in range(N_DEV)]
    for cp in copies:
        cp.start()
    for cp in copies:
        cp.wait()


def _load_pool_weight(pw_hbm, pw, sem):
    rows = POOL_GROUP_DIM // N_DEV
    copies = [pltpu.make_async_copy(pw_hbm.at[d], pw.at[:, pl.ds(d * rows, rows), :], sem.at[d]) for d in range(N_DEV)]
    for cp in copies:
        cp.start()
    for cp in copies:
        cp.wait()


def _rotate(x1, x2, cos, sin):
    return x1 * cos - x2 * sin, x1 * sin + x2 * cos


def _mix_proj_forward(h1, gain, wmix8, cos, sin, ride=None):
    tm, nt = WIDE_TILE, SEQ // WIDE_TILE
    k_scale = HEAD_DIM ** -0.5

    def body(h_ref, g_ref, wmix_hbm, cos_ref, sin_ref, u_ref, qkvg_ref, p_ref, gates_ref, wmix, sem):
        @pl.when(pl.program_id(0) == 0)
        def _():
            _load_mix_weight(wmix_hbm, wmix, sem)

        u = _rms(h_ref[...], g_ref[...])[0].astype(BF16)
        u_ref[...] = u
        cos_t, sin_t = cos_ref[...], sin_ref[...]
        for seg in range(N_SEG):
            pr = _dot(u, wmix[:, pl.ds(seg * D_MODEL, D_MODEL)])
            if seg < 2:
                scale = 1.0 if seg == 0 else k_scale
                for hd in range(HEADS):
                    lo = hd * HEAD_DIM
                    o1, o2 = _rotate(pr[:, lo:lo + ROT_HALF], pr[:, lo + ROT_HALF:lo + HEAD_DIM], cos_t, sin_t)
                    qkvg_ref[:, pl.ds(seg * D_MODEL + lo, ROT_HALF)] = (o1 * scale).astype(BF16)
                    qkvg_ref[:, pl.ds(seg * D_MODEL + lo + ROT_HALF, ROT_HALF)] = (o2 * scale).astype(BF16)
            elif seg < 4:
                qkvg_ref[:, pl.ds(seg * D_MODEL, D_MODEL)] = pr.astype(BF16)
            elif seg == 4:
                p_ref[...] = pr
            else:
                gates_ref[:, pl.ds((seg - 5) * D_MODEL, D_MODEL)] = pr.astype(BF16)

    est = 2 * D_MODEL * N_SEG * D_MODEL + 2 * tm * (4 * D_MODEL + 2 * D_MODEL + 2 * 4 * D_MODEL + 4 * D_MODEL + 2 * 2 * D_MODEL)
    return _call(
        body, name="mix_proj_fwd", grid=(nt,),
        in_specs=[_row_spec(tm, D_MODEL), _full_spec((1, D_MODEL)), ANY, _row_spec(tm, ROT_HALF), _row_spec(tm, ROT_HALF)],
        out_specs=[_row_spec(tm, D_MODEL), _row_spec(tm, 4 * D_MODEL), _row_spec(tm, D_MODEL), _row_spec(tm, 2 * D_MODEL)],
        out_shape=[jax.ShapeDtypeStruct((SEQ, D_MODEL), BF16), jax.ShapeDtypeStruct((SEQ, 4 * D_MODEL), BF16),
                   jax.ShapeDtypeStruct((SEQ, D_MODEL), F32), jax.ShapeDtypeStruct((SEQ, 2 * D_MODEL), BF16)],
        scratch_shapes=[pltpu.VMEM((D_MODEL, N_SEG * D_MODEL), BF16), pltpu.SemaphoreType.DMA((N_DEV,))],
        vmem_bytes=est + 8 * tm * D_MODEL * 4, args=[h1, gain, wmix8, cos, sin], ride=ride)


def _seg_block_spec(seg, reverse=False):
    nb = SEQ // RET_BLOCK
    if reverse:
        return pl.BlockSpec((RET_BLOCK, D_MODEL), lambda i, s=seg: (nb - 1 - i, s))
    return pl.BlockSpec((RET_BLOCK, D_MODEL), lambda i, s=seg: (i, s))


def _table_specs():
    return [_full_spec((HEADS, RET_BLOCK, RET_BLOCK)), _full_spec((HEADS, RET_BLOCK, 1)),
            _full_spec((HEADS, RET_BLOCK, 1)), _full_spec((HEADS, 1, 1))]


def _head_cols(h):
    return pl.ds(h * HEAD_DIM, HEAD_DIM)


def _retention_forward(qkvg, tables, ride=None):
    nb = SEQ // RET_BLOCK

    def body(q_ref, k_ref, v_ref, gr_ref, mask_ref, qdec_ref, kdec_ref, cdec_ref, ret_ref, o_ref, state):
        @pl.when(pl.program_id(0) == 0)
        def _():
            state[...] = jnp.zeros_like(state)

        for h in range(HEADS):
            cols = _head_cols(h)
            q, k, v = q_ref[:, cols], k_ref[:, cols], v_ref[:, cols]
            scores = _dot_nt(q, k) * mask_ref[h]
            inner = _dot(scores.astype(BF16), v)
            cross = _dot((q.astype(F32) * qdec_ref[h]).astype(BF16), state[h].astype(BF16))
            ret = inner + cross
            state[h] = state[h] * cdec_ref[h] + _dot_tn((k.astype(F32) * kdec_ref[h]).astype(BF16), v)
            ret_ref[:, cols] = ret
            retn = ret * lax.rsqrt(jnp.mean(ret * ret, axis=-1, keepdims=True) + NORM_EPS)
            gr = gr_ref[:, cols].astype(F32)
            o_ref[:, cols] = (retn * (gr * _sig(gr))).astype(BF16)

    return _call(
        body, name="retention_fwd", grid=(nb,),
        in_specs=[_seg_block_spec(0), _seg_block_spec(1), _seg_block_spec(2), _seg_block_spec(3)] + _table_specs(),
        out_specs=[_row_spec(RET_BLOCK, D_MODEL)] * 2,
        out_shape=[jax.ShapeDtypeStruct((SEQ, D_MODEL), F32), jax.ShapeDtypeStruct((SEQ, D_MODEL), BF16)],
        scratch_shapes=[pltpu.VMEM((HEADS, HEAD_DIM, HEAD_DIM), F32)],
        vmem_bytes=24 * RET_BLOCK * D_MODEL * 4, args=[qkvg, qkvg, qkvg, qkvg, *tables], ride=ride)


def _retention_backward_q(qkvg, dret, tables, ride=None):
    nb = SEQ // RET_BLOCK

    def body(k_ref, v_ref, do_ref, mask_ref, qdec_ref, kdec_ref, cdec_ref, dq_ref, state):
        @pl.when(pl.program_id(0) == 0)
        def _():
            state[...] = jnp.zeros_like(state)

        for h in range(HEADS):
            cols = _head_cols(h)
            k, v, do = k_ref[:, cols], v_ref[:, cols], do_ref[:, cols]
            dscores = _dot_nt(do, v) * mask_ref[h]
            dq_ref[:, cols] = _dot(dscores.astype(BF16), k) + _dot_nt(do, state[h].astype(BF16)) * qdec_ref[h]
            state[h] = state[h] * cdec_ref[h] + _dot_tn((k.astype(F32) * kdec_ref[h]).astype(BF16), v)

    return _call(
        body, name="retention_bwd_q", grid=(nb,),
        in_specs=[_seg_block_spec(1), _seg_block_spec(2), _row_spec(RET_BLOCK, D_MODEL)] + _table_specs(),
        out_specs=[_row_spec(RET_BLOCK, D_MODEL)],
        out_shape=[jax.ShapeDtypeStruct((SEQ, D_MODEL), F32)],
        scratch_shapes=[pltpu.VMEM((HEADS, HEAD_DIM, HEAD_DIM), F32)],
        vmem_bytes=24 * RET_BLOCK * D_MODEL * 4, args=[qkvg, qkvg, dret, *tables], ride=ride)


def _retention_backward_kv(qkvg, dret, tables, ride=None):
    nb = SEQ // RET_BLOCK

    def body(q_ref, k_ref, v_ref, do_ref, mask_ref, qdec_ref, kdec_ref, cdec_ref, dk_ref, dv_ref, gstate):
        @pl.when(pl.program_id(0) == 0)
        def _():
            gstate[...] = jnp.zeros_like(gstate)

        for h in range(HEADS):
            cols = _head_cols(h)
            q, k, v, do = q_ref[:, cols], k_ref[:, cols], v_ref[:, cols], do_ref[:, cols]
            mask = mask_ref[h]
            scores = (_dot_nt(q, k) * mask).astype(BF16)
            dscores = (_dot_nt(do, v) * mask).astype(BF16)
            gs = gstate[h].astype(BF16)
            dk_ref[:, cols] = _dot_tn(dscores, q) + _dot_nt(v, gs) * kdec_ref[h]
            dv_ref[:, cols] = _dot_tn(scores, do) + _dot((k.astype(F32) * kdec_ref[h]).astype(BF16), gs)
            gstate[h] = gstate[h] * cdec_ref[h] + _dot_tn((q.astype(F32) * qdec_ref[h]).astype(BF16), do)

    rev = lambda: pl.BlockSpec((RET_BLOCK, D_MODEL), lambda i: (nb - 1 - i, 0))
    return _call(
        body, name="retention_bwd_kv", grid=(nb,),
        in_specs=[_seg_block_spec(0, True), _seg_block_spec(1, True), _seg_block_spec(2, True), rev()] + _table_specs(),
        out_specs=[rev(), rev()],
        out_shape=[jax.ShapeDtypeStruct((SEQ, D_MODEL), F32)] * 2,
        scratch_shapes=[pltpu.VMEM((HEADS, HEAD_DIM, HEAD_DIM), F32)],
        vmem_bytes=32 * RET_BLOCK * D_MODEL * 4, args=[qkvg, qkvg, qkvg, dret, *tables], ride=ride)


def _pooled(p_ext, first_row):
    rows = p_ext.shape[0]
    t = first_row + lax.broadcasted_iota(jnp.int32, (rows - HALO, 1), 0)
    outs = []
    for g, w in enumerate(POOL_WINDOWS):
        e = p_ext[:, g * POOL_GROUP_DIM:(g + 1) * POOL_GROUP_DIM]
        s, span = e, 1
        while span < w:
            s = s + pltpu.roll(s, span, 0)
            span *= 2
        count = jnp.minimum(t + 1, w).astype(F32)
        outs.append(s[HALO:] / count - e[HALO:])
    return outs


def _pooled_transpose(d_ext, first_row):
    rows = d_ext.shape[0]
    t = first_row + lax.broadcasted_iota(jnp.int32, (rows, 1), 0)
    outs = []
    for g, w in enumerate(POOL_WINDOWS):
        d = d_ext[:, g * POOL_GROUP_DIM:(g + 1) * POOL_GROUP_DIM]
        e = jnp.where(t < SEQ, d / jnp.minimum(t + 1, w).astype(F32), 0.0)
        s, span = e, 1
        while span < w:
            s = s + pltpu.roll(s, rows - span, 0)
            span *= 2
        outs.append(s[:rows - HALO] - d[:rows - HALO])
    return outs


def _mix_tail_specs(tm):
    halo_blocks = tm // HALO
    return [
        _row_spec(tm, D_MODEL),
        pl.BlockSpec((HALO, D_MODEL), lambda i: (jnp.maximum(i * halo_blocks - 1, 0), 0)),
        _row_spec(tm, 2 * D_MODEL),
        _row_spec(tm, D_MODEL),
        _full_spec((2, D_MODEL)), _full_spec((1, D_MODEL)), ANY,
        _full_spec((D_MODEL, D_MODEL)), _full_spec((D_MODEL, D_MODEL)), _full_spec((D_MODEL, D_MODEL)),
    ]


def _mix_tail_compute(i, tm, p_ref, halo_ref, gates_ref, oret_ref, bias_ref, scale_ref, pw, wru_ref, wpu_ref, saved=None):
    halo = jnp.where(i > 0, halo_ref[...], 0.0)
    pooled = _pooled(jnp.concatenate([halo, p_ref[...]], axis=0), i * tm)
    pooled = [x.astype(BF16) for x in pooled]
    mixed = jnp.concatenate([_dot(pooled[g], pw[g]) for g in range(len(POOL_WINDOWS))], axis=-1)
    pool_out = (mixed * scale_ref[...]).astype(BF16)
    o_ret = oret_ref[...]
    if saved is None:
        a = _dot(o_ret, wru_ref[...])
        b = _dot(pool_out, wpu_ref[...])
    else:
        a, b = saved[0][...].astype(F32), saved[1][...].astype(F32)
    z = gates_ref[...].astype(F32)
    g0 = _sig(z[:, :D_MODEL] + bias_ref[0:1, :])
    g1 = _sig(z[:, D_MODEL:] + bias_ref[1:2, :])
    merged = (g0 * a + g1 * b).astype(BF16)
    return pooled, mixed, pool_out, o_ret, a, b, g0, g1, merged


def _mix_tail_forward(p, gates, o_ret, h1, bias, scale, pw8, wru, wpu, wo, ride=None):
    tm, nt = TOKEN_TILE, SEQ // TOKEN_TILE

    def body(p_ref, halo_ref, gates_ref, oret_ref, bias_ref, scale_ref, pw_hbm, wru_ref, wpu_ref, wo_ref, h1_ref,
             h2_ref, a_ref, b_ref, pw, sem):
        i = pl.program_id(0)

        @pl.when(i == 0)
        def _():
            _load_pool_weight(pw_hbm, pw, sem)

        out = _mix_tail_compute(i, tm, p_ref, halo_ref, gates_ref, oret_ref, bias_ref, scale_ref, pw, wru_ref, wpu_ref)
        a_ref[...] = out[4].astype(BF16)
        b_ref[...] = out[5].astype(BF16)
        h2_ref[...] = h1_ref[...] + _dot(out[-1], wo_ref[...])

    est = 3 * 2 * 2 * D_MODEL * D_MODEL + 2 * tm * D_MODEL * (4 + 4 + 2 + 4 + 4) + 16 * tm * D_MODEL * 4
    return _call(
        body, name="mix_tail_fwd", grid=(nt,),
        in_specs=_mix_tail_specs(tm) + [_row_spec(tm, D_MODEL)],
        out_specs=[_row_spec(tm, D_MODEL)] * 3,
        out_shape=[jax.ShapeDtypeStruct((SEQ, D_MODEL), F32)] + [jax.ShapeDtypeStruct((SEQ, D_MODEL), BF16)] * 2,
        scratch_shapes=[pltpu.VMEM((len(POOL_WINDOWS), POOL_GROUP_DIM, POOL_GROUP_DIM), BF16), pltpu.SemaphoreType.DMA((N_DEV,))],
        vmem_bytes=est, args=[p, p, gates, o_ret, bias, scale, pw8, wru, wpu, wo, h1], ride=ride)


def _mix_tail_backward(dh2, p, gates, o_ret, ret, qkvg, a_saved, b_saved, bias, scale, pw8, wru, wpu, wo, ride=None):
    tm, nt = TOKEN_TILE, SEQ // TOKEN_TILE
    n_groups = len(POOL_WINDOWS)
    rows_per_dev = POOL_GROUP_DIM // N_DEV

    def body(p_ref, halo_ref, gates_ref, oret_ref, bias_ref, scale_ref, pw_hbm, wru_ref, wpu_ref, wo_ref,
             dh2_ref, ret_ref, gr_ref, a_ref, b_ref,
             dret_ref, dgr_ref, dgates_ref, dpooled_ref, dwo_ref, dwru_ref, dwpu_ref, dpw_ref, dbias_ref, dscale_ref,
             pw, sem, acc_wo, acc_wru, acc_wpu, acc_pw, send_sq, recv_sq, send_pw, recv_pw, send_sems, recv_sems):
        i = pl.program_id(0)

        @pl.when(i == 0)
        def _():
            _load_pool_weight(pw_hbm, pw, sem)
            for ref in (acc_wo, acc_wru, acc_wpu, acc_pw, dbias_ref, dscale_ref):
                ref[...] = jnp.zeros_like(ref)

        pooled, mixed, pool_out, o_ret, a, b, g0, g1, merged = _mix_tail_compute(
            i, tm, p_ref, halo_ref, gates_ref, oret_ref, bias_ref, scale_ref, pw, wru_ref, wpu_ref, saved=(a_ref, b_ref))
        dh2 = dh2_ref[...].astype(BF16)
        dm = _dot_nt(dh2, wo_ref[...])
        acc_wo[...] += _dot_tn(merged, dh2)
        da = (dm * g0).astype(BF16)
        db = (dm * g1).astype(BF16)
        dz0 = dm * a * g0 * (1.0 - g0)
        dz1 = dm * b * g1 * (1.0 - g1)
        dbias_ref[0:1, :] += jnp.sum(dz0, axis=0, keepdims=True)
        dbias_ref[1:2, :] += jnp.sum(dz1, axis=0, keepdims=True)
        dgates_ref[:, pl.ds(0, D_MODEL)] = dz0.astype(BF16)
        dgates_ref[:, pl.ds(D_MODEL, D_MODEL)] = dz1.astype(BF16)
        acc_wru[...] += _dot_tn(o_ret, da)
        acc_wpu[...] += _dot_tn(pool_out, db)
        d_oret = _dot_nt(da, wru_ref[...])
        d_pool_out = _dot_nt(db, wpu_ref[...])
        dscale_ref[...] += jnp.sum(d_pool_out * mixed, axis=0, keepdims=True)
        dmixed = (d_pool_out * scale_ref[...]).astype(BF16)
        for g in range(n_groups):
            dmg = dmixed[:, g * POOL_GROUP_DIM:(g + 1) * POOL_GROUP_DIM]
            acc_pw[g] += _dot_tn(pooled[g], dmg)
            dpooled_ref[:, pl.ds(g * POOL_GROUP_DIM, POOL_GROUP_DIM)] = _dot_nt(dmg, pw[g])
        gr = gr_ref[...].astype(F32)
        s = _sig(gr)
        silu = gr * s
        for hd in range(HEADS):
            cols = slice(hd * HEAD_DIM, (hd + 1) * HEAD_DIM)
            r_h = ret_ref[:, cols]
            rr = lax.rsqrt(jnp.mean(r_h * r_h, axis=-1, keepdims=True) + NORM_EPS)
            rhat = r_h * rr
            do_h = d_oret[:, cols]
            dgr_ref[:, cols] = (do_h * rhat * (s[:, cols] * (1.0 + gr[:, cols] * (1.0 - s[:, cols])))).astype(BF16)
            dret_ref[:, cols] = _rms_bwd(do_h * silu[:, cols], rhat, rr).astype(BF16)

        @pl.when(i == nt - 1)
        def _():
            c = lax.axis_index("c")
            rows = D_MODEL // N_DEV
            squares = ((acc_wo, dwo_ref), (acc_wru, dwru_ref), (acc_wpu, dwpu_ref))
            for q in range(n_chips):
                own = pl.multiple_of((2 * q + c) * rows, rows)
                other = pl.multiple_of((2 * q + 1 - c) * rows, rows)
                for t, (acc, out) in enumerate(squares):
                    out[q] = acc[pl.ds(own, rows), :].astype(BF16)
                    send_sq[t, q] = acc[pl.ds(other, rows), :].astype(BF16)
                own_pw = pl.multiple_of((2 * q + c) * rows_per_dev, rows_per_dev)
                other_pw = pl.multiple_of((2 * q + 1 - c) * rows_per_dev, rows_per_dev)
                dpw_ref[q] = acc_pw[:, pl.ds(own_pw, rows_per_dev), :].astype(BF16)
                send_pw[q] = acc_pw[:, pl.ds(other_pw, rows_per_dev), :].astype(BF16)
            pushes = [_to_sibling(send_sq, recv_sq, send_sems.at[0], recv_sems.at[0]),
                      _to_sibling(send_pw, recv_pw, send_sems.at[1], recv_sems.at[1])]
            for cp in pushes:
                cp.start()
            for cp in pushes:
                cp.wait_recv()
            for t, (acc, out) in enumerate(squares):
                out[...] = (out[...].astype(F32) + recv_sq[t].astype(F32)).astype(BF16)
            dpw_ref[...] = (dpw_ref[...].astype(F32) + recv_pw[...].astype(F32)).astype(BF16)
            for cp in pushes:
                cp.wait_send()

    n_chips = N_DEV // 2
    sq = (n_chips, D_MODEL // N_DEV, D_MODEL)
    pw_shape = (n_chips, n_groups, rows_per_dev, POOL_GROUP_DIM)
    est = (3 * 2 * 2 * D_MODEL * D_MODEL + 3 * 4 * D_MODEL * D_MODEL + 3 * 2 * 2 * D_MODEL * D_MODEL
           + 2 * tm * D_MODEL * (4 + 4 + 2 + 4 + 4 + 2 + 2 + 2 + 4 + 4) + 24 * tm * D_MODEL * 4)
    return _call(
        body, name="mix_tail_bwd", grid=(nt,),
        in_specs=_mix_tail_specs(tm) + [_row_spec(tm, D_MODEL), _row_spec(tm, D_MODEL), _row_spec(tm, D_MODEL, 3),
                                        _row_spec(tm, D_MODEL), _row_spec(tm, D_MODEL)],
        out_specs=[_row_spec(tm, D_MODEL), _row_spec(tm, D_MODEL), _row_spec(tm, 2 * D_MODEL), _row_spec(tm, D_MODEL),
                   _full_spec(sq), _full_spec(sq), _full_spec(sq), _full_spec(pw_shape),
                   _full_spec((2, D_MODEL)), _full_spec((1, D_MODEL))],
        out_shape=[jax.ShapeDtypeStruct((SEQ, D_MODEL), BF16), jax.ShapeDtypeStruct((SEQ, D_MODEL), BF16),
                   jax.ShapeDtypeStruct((SEQ, 2 * D_MODEL), BF16), jax.ShapeDtypeStruct((SEQ, D_MODEL), F32),
                   jax.ShapeDtypeStruct(sq, BF16), jax.ShapeDtypeStruct(sq, BF16), jax.ShapeDtypeStruct(sq, BF16),
                   jax.ShapeDtypeStruct(pw_shape, BF16),
                   jax.ShapeDtypeStruct((2, D_MODEL), F32), jax.ShapeDtypeStruct((1, D_MODEL), F32)],
        scratch_shapes=[pltpu.VMEM((n_groups, POOL_GROUP_DIM, POOL_GROUP_DIM), BF16), pltpu.SemaphoreType.DMA((N_DEV,)),
                        pltpu.VMEM((D_MODEL, D_MODEL), F32), pltpu.VMEM((D_MODEL, D_MODEL), F32),
                        pltpu.VMEM((D_MODEL, D_MODEL), F32), pltpu.VMEM((n_groups, POOL_GROUP_DIM, POOL_GROUP_DIM), F32),
                        pltpu.VMEM((3,) + sq, BF16), pltpu.VMEM((3,) + sq, BF16), pltpu.VMEM(pw_shape, BF16),
                        pltpu.VMEM(pw_shape, BF16), pltpu.SemaphoreType.DMA((2,)), pltpu.SemaphoreType.DMA((2,))],
        vmem_bytes=est, args=[p, p, gates, o_ret, bias, scale, pw8, wru, wpu, wo, dh2, ret, qkvg, a_saved, b_saved], ride=ride)


def _mix_proj_backward(dq, dk, dv, dgr, dpooled, dgates, cos, sin, h1, gain, dh2, wmix8, ride=None):
    tm, nt = TOKEN_TILE, SEQ // TOKEN_TILE
    halo_blocks = tm // HALO
    last_halo = SEQ // HALO - 1
    k_scale = HEAD_DIM ** -0.5

    def body(dq_ref, dk_ref, dv_ref, dgr_ref, dpool_ref, dhalo_ref, dgates_ref, cos_ref, sin_ref, h1_ref, g_ref,
             dh2_ref, wmix_hbm, dh1_ref, dproj_ref, dg_ref, wmix, sem):
        i = pl.program_id(0)

        @pl.when(i == 0)
        def _():
            _load_mix_weight(wmix_hbm, wmix, sem)
            dg_ref[...] = jnp.zeros_like(dg_ref)

        cos_t, sin_t = cos_ref[...], sin_ref[...]
        for seg, ref, scale in ((0, dq_ref, 1.0), (1, dk_ref, k_scale)):
            for hd in range(HEADS):
                lo = hd * HEAD_DIM
                d1, d2 = ref[:, lo:lo + ROT_HALF], ref[:, lo + ROT_HALF:lo + HEAD_DIM]
                dproj_ref[:, pl.ds(seg * D_MODEL + lo, ROT_HALF)] = ((d1 * cos_t + d2 * sin_t) * scale).astype(BF16)
                dproj_ref[:, pl.ds(seg * D_MODEL + lo + ROT_HALF, ROT_HALF)] = ((d2 * cos_t - d1 * sin_t) * scale).astype(BF16)
        dproj_ref[:, pl.ds(2 * D_MODEL, D_MODEL)] = dv_ref[...].astype(BF16)
        dproj_ref[:, pl.ds(3 * D_MODEL, D_MODEL)] = dgr_ref[...]
        dp = _pooled_transpose(jnp.concatenate([dpool_ref[...], dhalo_ref[...]], axis=0), i * tm)
        for g in range(len(POOL_WINDOWS)):
            dproj_ref[:, pl.ds(4 * D_MODEL + g * POOL_GROUP_DIM, POOL_GROUP_DIM)] = dp[g].astype(BF16)
        dproj_ref[:, pl.ds(5 * D_MODEL, 2 * D_MODEL)] = dgates_ref[...]
        du = jnp.zeros((tm, D_MODEL), F32)
        for seg in range(N_SEG):
            cols = pl.ds(seg * D_MODEL, D_MODEL)
            du = du + _dot_nt(dproj_ref[:, cols], wmix[:, cols])
        g = g_ref[...]
        _, xhat, r = _rms(h1_ref[...], g)
        dg_ref[...] += jnp.sum(du * xhat, axis=0, keepdims=True)
        dh1_ref[...] = dh2_ref[...] + _rms_bwd(du * g, xhat, r)

    est = 2 * D_MODEL * N_SEG * D_MODEL + 2 * tm * D_MODEL * (3 * 4 + 2 + 4 + 4 + 4 + 4 + 4 + 14) + 12 * tm * D_MODEL * 4
    return _call(
        body, name="mix_proj_bwd", grid=(nt,),
        in_specs=[_row_spec(tm, D_MODEL), _row_spec(tm, D_MODEL), _row_spec(tm, D_MODEL), _row_spec(tm, D_MODEL),
                  _row_spec(tm, D_MODEL),
                  pl.BlockSpec((HALO, D_MODEL), lambda i: (jnp.minimum((i + 1) * halo_blocks, last_halo), 0)),
                  _row_spec(tm, 2 * D_MODEL), _row_spec(tm, ROT_HALF), _row_spec(tm, ROT_HALF),
                  _row_spec(tm, D_MODEL), _full_spec((1, D_MODEL)), _row_spec(tm, D_MODEL), ANY],
        out_specs=[_row_spec(tm, D_MODEL), _row_spec(tm, N_SEG * D_MODEL), _full_spec((1, D_MODEL))],
        out_shape=[jax.ShapeDtypeStruct((SEQ, D_MODEL), F32), jax.ShapeDtypeStruct((SEQ, N_SEG * D_MODEL), BF16),
                   jax.ShapeDtypeStruct((1, D_MODEL), F32)],
        scratch_shapes=[pltpu.VMEM((D_MODEL, N_SEG * D_MODEL), BF16), pltpu.SemaphoreType.DMA((N_DEV,))],
        vmem_bytes=est, args=[dq, dk, dv, dgr, dpooled, dpooled, dgates, cos, sin, h1, gain, dh2, wmix8], ride=ride)


def _adamw(w, parts, m, v, name, after=None):
    rows, cols = w.shape
    n_lists = len(parts)
    tr = max([t for t in range(16, 257, 16) if rows % t == 0], default=rows)
    c1 = 1.0 - ADAM_B1 ** ADAM_STEP
    c2 = 1.0 - ADAM_B2 ** ADAM_STEP

    def body(*refs):
        w_ref, m_ref, v_ref = refs[:3]
        part_refs = refs[3:3 + n_lists]
        g_out, d_out, m_out, v_out = refs[-4:]
        g = None
        for p_ref in part_refs:
            for k in range(p_ref.shape[0]):
                term = p_ref[k].astype(F32)
                g = term if g is None else g + term
        m_new = ADAM_B1 * m_ref[...] + (1.0 - ADAM_B1) * g
        v_new = ADAM_B2 * v_ref[...] + (1.0 - ADAM_B2) * (g * g)
        g_out[...] = g
        m_out[...] = m_new
        v_out[...] = v_new
        d_out[...] = -ADAM_LR * ((m_new / c1) / (jnp.sqrt(v_new / c2) + ADAM_EPS) + ADAM_WD * w_ref[...])

    spec = pl.BlockSpec((tr, cols), lambda i: (i, 0))
    out = jax.ShapeDtypeStruct((rows, cols), F32)
    part_specs = [pl.BlockSpec((p.shape[0], tr, cols), lambda i: (0, i, 0)) for p in parts]
    part_bytes = sum(p.shape[0] * p.dtype.itemsize for p in parts)
    extra = [] if after is None else [after]
    return pl.pallas_call(
        body, name=name, grid=(rows // tr,),
        in_specs=[spec, spec, spec] + part_specs + [ANY] * len(extra),
        out_specs=[spec] * 4, out_shape=[out] * 4,
        compiler_params=_params(2 * tr * cols * (7 * 4 + part_bytes) + 8 * tr * cols * 4, 1),
    )(_in_hbm(w), _in_hbm(m), _in_hbm(v), *[_in_hbm(p) for p in parts], *extra)


def _mix_w_in_grad(u, dproj, ride=None):
    return _weight_grad(
        u, dproj, N_DEV,
        lambda tt: pl.BlockSpec((tt, D_MODEL), lambda b, t: (t, 0)),
        lambda tt: pl.BlockSpec((tt, MIX_SHARD), lambda b, t: (t, b)),
        D_MODEL, MIX_SHARD, name="w_in_grad", ride=ride)


def kernel(x, norm_ffn1, ffn1_w_in, ffn1_w_out, norm_mix, w_in, gate_bias, pool_w, pool_scale, w_ret_up, w_pool_up, w_out, norm_ffn2, ffn2_w_in, ffn2_w_out, norm_final, loss_target, m_norm_ffn1, m_ffn1_w_in, m_ffn1_w_out, m_norm_mix, m_w_in, m_gate_bias, m_pool_w, m_pool_scale, m_w_ret_up, m_w_pool_up, m_w_out, m_norm_ffn2, m_ffn2_w_in, m_ffn2_w_out, m_norm_final, v_norm_ffn1, v_ffn1_w_in, v_ffn1_w_out, v_norm_mix, v_w_in, v_gate_bias, v_pool_w, v_pool_scale, v_w_ret_up, v_w_pool_up, v_w_out, v_norm_ffn2, v_ffn2_w_in, v_ffn2_w_out, v_norm_final):
    assert x.shape == (1, SEQ, D_MODEL) and ffn1_w_in.shape == (1, D_MODEL, FF_SHARD) and w_in.shape == (1, D_MODEL, MIX_SHARD)
    x2, target = x[0], loss_target[0]

    cos, sin = _rotary_tables()
    tables = _retention_tables()
    bf = lambda w: w[0].astype(BF16)
    bf_t = lambda w: jnp.swapaxes(w[0], 0, 1).astype(BF16)
    square = lambda w: w.reshape(D_MODEL, D_MODEL)

    win1, wout1, bias8 = _alone(_RelayGather([bf_t(ffn1_w_in), bf(ffn1_w_out), gate_bias[0]]), "ffn1_weights_all_gather")
    wout1 = wout1.reshape(N_FF_GROUPS, FF_SHARD, D_MODEL)
    bias = bias8.transpose(1, 0, 2).reshape(2, D_MODEL)

    (h1, gu1), (wmix8,) = _ffn_forward(x2, norm_ffn1, win1, wout1, "ffn1_fwd", ride=_GatherRide([bf(w_in)]))
    (u, qkvg, p, gates), (win2,) = _mix_proj_forward(h1, norm_mix, wmix8, cos, sin, ride=_GatherRide([bf_t(ffn2_w_in)]))
    (ret, o_ret), (pw8, wru, wpu, wo) = _retention_forward(
        qkvg, tables, ride=_GatherRide([bf(pool_w), bf(w_ret_up), bf(w_pool_up), bf(w_out)]))
    wru, wpu, wo = square(wru), square(wpu), square(wo)
    (h2, a_saved, b_saved), (wout2,) = _mix_tail_forward(p, gates, o_ret, h1, bias, pool_scale, pw8, wru, wpu, wo,
                                        ride=_GatherRide([bf(ffn2_w_out)]))
    wout2 = wout2.reshape(N_FF_GROUPS, FF_SHARD, D_MODEL)
    (dh3, gu2, loss_part, d_norm_final), _ = _ffn_forward(h2, norm_ffn2, win2, wout2, "ffn2_fwd_loss",
                                                          head=(target, norm_final.reshape(1, D_MODEL)))

    (dh2, dgu2, act2, xn2, df2, d_norm_ffn2), _ = _ffn_backward(dh3, h2, norm_ffn2, gu2, win2, wout2, "ffn2_bwd")
    d_wout2, _ = _ffn_w_out_grad(act2, df2, 2)
    d_win2, (r_wout2,) = _ffn_w_in_grad(xn2, dgu2, 2, ride=_ScatterRide([d_wout2]))
    (dret, dgr, dgates, dpooled, d_wo, d_wru, d_wpu, d_pw, d_bias, d_scale), (r_win2,) = _mix_tail_backward(
        dh2, p, gates, o_ret, ret, qkvg, a_saved, b_saved, bias, pool_scale, pw8, wru, wpu, wo, ride=_ScatterRide([d_win2]))
    (dq,), _ = _retention_backward_q(qkvg, dret, tables)
    (dk, dv), (r_pw, r_wru, r_wpu, r_wo) = _retention_backward_kv(
        qkvg, dret, tables, ride=_ScatterRide([d_pw, d_wru, d_wpu, d_wo]))
    (dh1, dproj, d_norm_mix), _ = _mix_proj_backward(dq, dk, dv, dgr, dpooled, dgates, cos, sin, h1, norm_mix, dh2, wmix8)
    d_wmix, _ = _mix_w_in_grad(u, dproj)
    (grad_x, dgu1, act1, xn1, df1, d_norm_ffn1), (r_wmix,) = _ffn_backward(
        dh1, x2, norm_ffn1, gu1, win1, wout1, "ffn1_bwd", ride=_ScatterRide([d_wmix]))
    d_win1, _ = _ffn_w_in_grad(xn1, dgu1, 1)
    win1_state, win1_started = _scatter_start(d_win1, "ffn1_w_in_grad_exchange_start")
    d_wout1, _ = _ffn_w_out_grad(act1, df1, 1, after=win1_started)
    wout1_state, started = _scatter_start(d_wout1, "ffn1_w_out_grad_exchange_start")
    zero_row = jnp.zeros((1, D_MODEL), F32)
    small = _all_reduce_rows(jnp.concatenate(
        [d_norm_ffn1, d_norm_mix, d_scale, d_norm_ffn2, d_norm_final, d_bias, jnp.tile(loss_part, (1, D_MODEL // 128))],
        axis=0))
    loss = small[7, 0]

    results = {}

    def update(nm, w, parts, m, v, after):
        if nm in ("ffn1_w_in", "ffn2_w_in"):
            flat, back = (lambda a: jnp.swapaxes(a[0], 0, 1)), (lambda o: jnp.swapaxes(o, 0, 1)[None])
        else:
            flat, back = (lambda a: a.reshape(-1, w.shape[-1])), (lambda o: o.reshape(w.shape))
        parts = [p.reshape(p.shape[:1] + flat(w).shape) for p in parts]
        outs = _adamw(flat(w), parts, flat(m), flat(v), name=f"adamw_{nm}", after=after)
        results[nm] = [back(o) for o in outs]
        return outs[0]

    done = started
    for nm, w, parts, m, v in (
            ("w_in", w_in, r_wmix, m_w_in, v_w_in), ("ffn2_w_in", ffn2_w_in, r_win2, m_ffn2_w_in, v_ffn2_w_in),
            ("ffn2_w_out", ffn2_w_out, r_wout2, m_ffn2_w_out, v_ffn2_w_out), ("w_ret_up", w_ret_up, r_wru, m_w_ret_up, v_w_ret_up),
            ("w_pool_up", w_pool_up, r_wpu, m_w_pool_up, v_w_pool_up), ("w_out", w_out, r_wo, m_w_out, v_w_out),
            ("pool_w", pool_w, r_pw, m_pool_w, v_pool_w)):
        done = update(nm, w, [parts], m, v, done)
    done = update("ffn1_w_in", ffn1_w_in, _scatter_wait(win1_state, done, "ffn1_w_in_grad_exchange_wait"),
                  m_ffn1_w_in, v_ffn1_w_in, None)
    update("ffn1_w_out", ffn1_w_out, _scatter_wait(wout1_state, done, "ffn1_w_out_grad_exchange_wait"),
           m_ffn1_w_out, v_ffn1_w_out, None)

    my_id = _linear_id(*_my_position())
    bias_cols = gate_bias.shape[-1]
    pad = lambda a: jnp.pad(a[0], ((0, 0), (0, D_MODEL - bias_cols)))
    pack = lambda a, b, c, d, e, gb: jnp.concatenate([a, b, c, d, e.reshape(1, D_MODEL), pad(gb), zero_row], axis=0)
    d_bias_mine = lax.dynamic_slice_in_dim(small[5:7], my_id * bias_cols, bias_cols, axis=1)
    g_small = jnp.concatenate([small[0:5], jnp.pad(d_bias_mine, ((0, 0), (0, D_MODEL - bias_cols))), zero_row], axis=0)
    s_outs = _adamw(pack(norm_ffn1, norm_mix, pool_scale, norm_ffn2, norm_final, gate_bias), [g_small[None]],
                    pack(m_norm_ffn1, m_norm_mix, m_pool_scale, m_norm_ffn2, m_norm_final, m_gate_bias),
                    pack(v_norm_ffn1, v_norm_mix, v_pool_scale, v_norm_ffn2, v_norm_final, v_gate_bias), name="adamw_small")
    for row, nm in enumerate(["norm_ffn1", "norm_mix", "pool_scale", "norm_ffn2"]):
        results[nm] = [o[row:row + 1] for o in s_outs]
    results["norm_final"] = [o[4] for o in s_outs]
    results["gate_bias"] = [o[5:7, :bias_cols][None] for o in s_outs]

    order = ["norm_ffn1", "ffn1_w_in", "ffn1_w_out", "norm_mix", "w_in", "gate_bias", "pool_w", "pool_scale",
             "w_ret_up", "w_pool_up", "w_out", "norm_ffn2", "ffn2_w_in", "ffn2_w_out", "norm_final"]
    return (loss, grad_x[None], *[results[nm][0] for nm in order], *[results[nm][1] for nm in order],
            *[results[nm][2] for nm in order], *[results[nm][3] for nm in order])
```

```python
import functools

import numpy as np
import jax
import jax.numpy as jnp
from jax import lax
from jax.experimental import pallas as pl
from jax.experimental.pallas import tpu as pltpu

F32 = jnp.float32
BF16 = jnp.bfloat16

N_DEV = 8
D_MODEL = 1024
SEQ = 4096
D_FF = 2816
FF_SHARD = 2 * D_FF // N_DEV
N_FF_GROUPS = N_DEV // 2
HEADS = 4
HEAD_DIM = 256
ROT_HALF = HEAD_DIM // 2
CHUNK = 64
RET_BLOCK = 256
POOL_WINDOWS = (2, 4, 8, 16)
POOL_GROUP_DIM = 256
HALO = 16
MIX_SHARD = 7 * D_MODEL // N_DEV
N_SEG = 7
ROPE_BASE = 10000.0
NORM_EPS = 1e-6
FFN_RES_WEIGHT = 0.5
ADAM_LR, ADAM_B1, ADAM_B2, ADAM_EPS, ADAM_WD, ADAM_STEP = 0.001, 0.9, 0.999, 1e-08, 0.01, 10

TOKEN_TILE = 256
WIDE_TILE = 512
VMEM_CAP_V7X = 64 * 1024 * 1024
MESH = pl.DeviceIdType.MESH
ANY = pl.BlockSpec(memory_space=pl.ANY)


def _vmem_limit(estimate_bytes):
    return int(min(estimate_bytes * 5 // 4 + (6 << 20), VMEM_CAP_V7X - (4 << 20)))


def _params(estimate_bytes, n_grid):
    return pltpu.CompilerParams(dimension_semantics=("arbitrary",) * n_grid,
                                vmem_limit_bytes=_vmem_limit(estimate_bytes))


def _dot(a, b):
    return jnp.dot(a, b, preferred_element_type=F32)


def _dot_nt(a, b):
    return lax.dot_general(a, b, (((1,), (1,)), ((), ())), preferred_element_type=F32)


def _dot_tn(a, b):
    return lax.dot_general(a, b, (((0,), (0,)), ((), ())), preferred_element_type=F32)


def _sig(x):
    return 1.0 / (1.0 + jnp.exp(-x))


def _rms(x, g):
    r = lax.rsqrt(jnp.mean(x * x, axis=-1, keepdims=True) + NORM_EPS)
    xhat = x * r
    return xhat * g, xhat, r


def _rms_bwd(dyg, xhat, r):
    return r * (dyg - xhat * jnp.mean(dyg * xhat, axis=-1, keepdims=True))


def _row_spec(tile, width, col=0):
    return pl.BlockSpec((tile, width), lambda i, c=col: (i, c))


def _full_spec(shape):
    return pl.BlockSpec(shape, lambda *_: (0,) * len(shape))


def _rotary_tables():
    inv_freq = (np.float32(ROPE_BASE) ** (-np.arange(ROT_HALF, dtype=np.float32) / np.float32(ROT_HALF))).astype(np.float32)
    ang = (np.arange(SEQ, dtype=np.float32)[:, None] * inv_freq[None, :]).astype(np.float32)
    return jnp.asarray(np.cos(ang.astype(np.float64)), F32), jnp.asarray(np.sin(ang.astype(np.float64)), F32)


def _retention_tables():
    log_gamma = np.log(1.0 - 2.0 ** (-5.0 - np.arange(HEADS, dtype=np.float64)))
    n = np.arange(RET_BLOCK)
    diff = (n[:, None] - n[None, :]).astype(np.float64)
    same = (n[:, None] // CHUNK) == (n[None, :] // CHUNK)
    earlier = (n[None, :] // CHUNK) < (n[:, None] // CHUNK)
    expo = np.where(same, np.abs(diff), diff)
    mask = np.where(same | earlier, np.exp(log_gamma[:, None, None] * expo[None]), 0.0)
    qdec = np.exp(log_gamma[:, None] * (n[None, :] + 1.0))[:, :, None]
    kdec = np.exp(log_gamma[:, None] * (RET_BLOCK - 1.0 - n[None, :]))[:, :, None]
    cdec = np.exp(log_gamma * RET_BLOCK)[:, None, None]
    return (jnp.asarray(mask, F32), jnp.asarray(qdec, F32), jnp.asarray(kdec, F32), jnp.asarray(cdec, F32))


def _my_position():
    return lax.axis_index("x"), lax.axis_index("y"), lax.axis_index("c")


def _linear_id(px, py, pc):
    return 4 * px + 2 * py + pc


def _when(pred, fn):
    if isinstance(pred, bool):
        if pred:
            fn()
    else:
        pl.when(pred)(fn)


class _GatherRide:
    def __init__(self, shards):
        self.args = list(shards)
        n = self.n = len(shards)
        self.out_shape = [pltpu.HBM((N_DEV,) + s.shape, s.dtype) for s in shards]
        self.scratch = [pltpu.SemaphoreType.DMA((n, 7)), pltpu.SemaphoreType.DMA((n, 7)), pltpu.SemaphoreType.DMA((n,))]

    def _plan(self, src, out, sems):
        send_sems, recv_sems, local_sem = sems
        x, y, c = _my_position()
        me, sibling = (x, y, c), (x, y, 1 - c)
        chips = [(1 - x, y), (x, 1 - y), (1 - x, 1 - y)]

        def copy(t, k, block, to, from_src=False):
            rows = out[t].at[_linear_id(*block)]
            return pltpu.make_async_remote_copy(
                src_ref=src[t] if from_src else rows, dst_ref=rows,
                send_sem=send_sems.at[t, k], recv_sem=recv_sems.at[t, k],
                device_id=to, device_id_type=MESH)

        local = [pltpu.make_async_copy(src[t], out[t].at[_linear_id(*me)], local_sem.at[t]) for t in range(self.n)]
        return copy, local, me, sibling, chips, c

    def begin(self, first, src, out, sems):
        copy, local, me, sibling, chips, c = self._plan(src, out, sems)

        def start():
            for cp in local:
                cp.start()
            for t in range(self.n):
                copy(t, 0, me, sibling, from_src=True).start()
                for j, chip in enumerate(chips):
                    copy(t, 1 + j, me, (*chip, c), from_src=True).start()

        _when(first, start)

    def finish(self, mid, last, src, out, sems):
        copy, local, me, sibling, chips, c = self._plan(src, out, sems)

        def pass_on():
            for j, chip in enumerate(chips):
                for t in range(self.n):
                    copy(t, 1 + j, (*chip, c), me).wait_recv()
                    copy(t, 4 + j, (*chip, c), sibling).start()

        def drain():
            for t in range(self.n):
                copy(t, 0, sibling, me).wait_recv()
                for j, chip in enumerate(chips):
                    copy(t, 4 + j, (*chip, 1 - c), me).wait_recv()
            for t in range(self.n):
                copy(t, 0, me, sibling, from_src=True).wait_send()
                for j, chip in enumerate(chips):
                    copy(t, 1 + j, me, (*chip, c), from_src=True).wait_send()
                    copy(t, 4 + j, (*chip, c), sibling).wait_send()
            for cp in local:
                cp.wait()

        _when(mid, pass_on)
        _when(last, drain)


class _RelayGather(_GatherRide):
    def begin(self, first, src, out, sems):
        copy, local, me, sibling, chips, c = self._plan(src, out, sems)
        for cp in local:
            cp.start()
        for t in range(self.n):
            copy(t, 0, me, sibling, from_src=True).start()
            for j in range(2):
                copy(t, 1 + j, me, (*chips[j], c), from_src=True).start()

    def finish(self, mid, last, src, out, sems):
        copy, local, me, sibling, chips, c = self._plan(src, out, sems)
        x, y, _ = me
        relayed_from = (x ^ (1 - c), y ^ c, c)
        relayed_to = (x ^ c, y ^ (1 - c), c)
        for t in range(self.n):
            for j in range(2):
                copy(t, 1 + j, (*chips[j], c), me).wait_recv()
            copy(t, 3, relayed_from, relayed_to).start()
            for j in range(2):
                copy(t, 4 + j, (*chips[j], c), sibling).start()
        for t in range(self.n):
            copy(t, 3, (*chips[2], c), me).wait_recv()
            copy(t, 6, (*chips[2], c), sibling).start()
        for t in range(self.n):
            copy(t, 0, sibling, me).wait_recv()
            for j in range(3):
                copy(t, 4 + j, (*chips[j], 1 - c), me).wait_recv()
        for t in range(self.n):
            copy(t, 0, me, sibling, from_src=True).wait_send()
            for j in range(2):
                copy(t, 1 + j, me, (*chips[j], c), from_src=True).wait_send()
            copy(t, 3, relayed_from, relayed_to).wait_send()
            for j in range(3):
                copy(t, 4 + j, (*chips[j], c), sibling).wait_send()
        for cp in local:
            cp.wait()


class _ScatterRide:
    def __init__(self, chip_sums):
        self.args = list(chip_sums)
        n = self.n = len(chip_sums)
        self.out_shape = [pltpu.HBM(p.shape, p.dtype) for p in chip_sums]
        self.scratch = [pltpu.SemaphoreType.DMA((n, 3)), pltpu.SemaphoreType.DMA((n, 3)), pltpu.SemaphoreType.DMA((n,))]

    def _plan(self, src, out, sems):
        send_sems, recv_sems, local_sem = sems
        x, y, c = _my_position()

        def peer(k):
            return (x ^ (k >> 1), y ^ (k & 1))

        copies = [pltpu.make_async_remote_copy(
            src_ref=src[t].at[2 * peer(k)[0] + peer(k)[1]], dst_ref=out[t].at[k],
            send_sem=send_sems.at[t, k - 1], recv_sem=recv_sems.at[t, k - 1],
            device_id=(*peer(k), c), device_id_type=MESH) for t in range(self.n) for k in range(1, N_DEV // 2)]
        local = [pltpu.make_async_copy(src[t].at[2 * x + y], out[t].at[0], local_sem.at[t]) for t in range(self.n)]
        return copies, local

    def begin(self, first, src, out, sems):
        copies, local = self._plan(src, out, sems)

        def start():
            for cp in local + copies:
                cp.start()

        _when(first, start)

    def finish(self, mid, last, src, out, sems):
        copies, local = self._plan(src, out, sems)

        def drain():
            for cp in copies:
                cp.wait_recv()
            for cp in copies:
                cp.wait_send()
            for cp in local:
                cp.wait()

        _when(last, drain)


def _in_hbm(a):
    return pltpu.with_memory_space_constraint(a, pltpu.HBM)


def _call(body, *, name, grid, in_specs, out_specs, out_shape, scratch_shapes, vmem_bytes, args, ride=None, after=None):
    n_in, n_out, n_s = len(in_specs), len(out_specs), len(scratch_shapes)
    params = _params(vmem_bytes, len(grid))
    args = [_in_hbm(a) for a in args]
    out_shape = [pltpu.HBM(s.shape, s.dtype) for s in out_shape]
    if ride is None:
        if after is not None:
            def ordered_body(*refs):
                body(*refs[:n_in], *refs[n_in + 1:])
            outs = pl.pallas_call(ordered_body, name=name, grid=grid, in_specs=list(in_specs) + [ANY], out_specs=out_specs,
                                  out_shape=out_shape, scratch_shapes=scratch_shapes, compiler_params=params)(*args, after)
            return list(outs), []
        outs = pl.pallas_call(body, name=name, grid=grid, in_specs=in_specs, out_specs=out_specs, out_shape=out_shape,
                              scratch_shapes=scratch_shapes, compiler_params=params)(*args)
        return list(outs), []
    total = int(np.prod(grid))

    def riding_body(*refs):
        a = n_in
        b = a + ride.n
        c = b + n_out
        d = c + ride.n
        e = d + n_s
        step = pl.program_id(0)
        for axis in range(1, len(grid)):
            step = step * grid[axis] + pl.program_id(axis)
        ride.begin(step == 0, refs[a:b], refs[c:d], refs[e:])
        body(*refs[:a], *refs[b:c], *refs[d:e])
        ride.finish(step == (3 * total) // 4, step == total - 1, refs[a:b], refs[c:d], refs[e:])

    outs = pl.pallas_call(
        riding_body, name=name, grid=grid, in_specs=list(in_specs) + [ANY] * ride.n,
        out_specs=list(out_specs) + [ANY] * ride.n, out_shape=list(out_shape) + ride.out_shape,
        scratch_shapes=list(scratch_shapes) + ride.scratch, compiler_params=params)(*args, *[_in_hbm(a) for a in ride.args])
    return list(outs[:n_out]), list(outs[n_out:])


def _alone(ride, name):
    def body(*refs):
        src, out, sems = refs[:ride.n], refs[ride.n:2 * ride.n], refs[2 * ride.n:]
        ride.begin(True, src, out, sems)
        ride.finish(True, True, src, out, sems)

    return list(pl.pallas_call(body, name=name, out_shape=ride.out_shape, in_specs=[ANY] * ride.n,
                               out_specs=[ANY] * ride.n, scratch_shapes=ride.scratch)(*[_in_hbm(a) for a in ride.args]))


def _scatter_copies(src, land, send_sems, recv_sems):
    x, y, c = _my_position()
    copies = []
    for k in range(1, N_DEV // 2):
        px, py = x ^ (k >> 1), y ^ (k & 1)
        copies.append(pltpu.make_async_remote_copy(
            src_ref=src.at[2 * px + py], dst_ref=land.at[k - 1], send_sem=send_sems.at[k - 1], recv_sem=recv_sems.at[k - 1],
            device_id=(px, py, c), device_id_type=MESH))
    return copies


def _scatter_start(chip_sums, name):
    n_peers = N_DEV // 2 - 1
    land_shape = (n_peers,) + chip_sums.shape[1:]
    hbm = pl.BlockSpec(memory_space=pltpu.HBM)
    sem = pl.BlockSpec(memory_space=pltpu.SEMAPHORE)

    def body(src_ref, land_ref, send_sems, recv_sems, src_thru, land_thru, token):
        for cp in _scatter_copies(src_ref, land_ref, send_sems, recv_sems):
            cp.start()
        token[...] = jnp.zeros_like(token)

    send_sems, recv_sems, src_thru, land_thru, token = pl.pallas_call(
        body, name=name,
        out_shape=(pltpu.SemaphoreType.DMA((n_peers,)), pltpu.SemaphoreType.DMA((n_peers,)),
                   pltpu.HBM(chip_sums.shape, chip_sums.dtype), pltpu.HBM(land_shape, chip_sums.dtype),
                   jax.ShapeDtypeStruct((8, 128), F32)),
        in_specs=(hbm, hbm), out_specs=(sem, sem, hbm, hbm, pl.BlockSpec(memory_space=pltpu.VMEM)),
        input_output_aliases={0: 2, 1: 3},
        compiler_params=pltpu.CompilerParams(has_side_effects=pltpu.SideEffectType.DATAFLOW_SIDE_EFFECTING),
    )(_in_hbm(chip_sums), _in_hbm(lax.empty(land_shape, chip_sums.dtype)))
    return (send_sems, recv_sems, src_thru, land_thru), token


def _scatter_wait(state, after, name):
    send_sems, recv_sems, src_thru, land_thru = state
    hbm = pl.BlockSpec(memory_space=pltpu.HBM)
    sem = pl.BlockSpec(memory_space=pltpu.SEMAPHORE)

    def body(src_ref, land_ref, send_sems, recv_sems, after_ref, src_out, land_out):
        for cp in _scatter_copies(src_ref, land_ref, send_sems, recv_sems):
            cp.wait_send()
            cp.wait_recv()

    src_done, land_done = pl.pallas_call(
        body, name=name,
        out_shape=(pltpu.HBM(src_thru.shape, src_thru.dtype), pltpu.HBM(land_thru.shape, land_thru.dtype)),
        in_specs=(hbm, hbm, sem, sem, ANY), out_specs=(hbm, hbm), input_output_aliases={0: 0, 1: 1},
        compiler_params=pltpu.CompilerParams(has_side_effects=pltpu.SideEffectType.DATAFLOW_SIDE_EFFECTING),
    )(src_thru, land_thru, send_sems, recv_sems, after)
    x, y, _ = _my_position()
    return lax.dynamic_slice_in_dim(src_done, 2 * x + y, 1, axis=0), land_done


def _all_reduce_rows(block):
    rows, width = block.shape

    def body(x_ref, sum_ref, gathered, send_sems, recv_sems, local_sem):
        x, y, c = _my_position()
        me, sibling = (x, y, c), (x, y, 1 - c)
        chips = [(1 - x, y), (x, 1 - y), (1 - x, 1 - y)]

        def slot(px, py, pc):
            return gathered.at[_linear_id(px, py, pc)]

        def copy(k, block_of, to, from_src=False):
            return pltpu.make_async_remote_copy(
                src_ref=x_ref if from_src else slot(*block_of), dst_ref=slot(*block_of),
                send_sem=send_sems.at[k], recv_sem=recv_sems.at[k], device_id=to, device_id_type=MESH)

        mine = pltpu.make_async_copy(x_ref, slot(*me), local_sem)
        mine.start()
        first = [copy(0, me, sibling, from_src=True)]
        first += [copy(1 + j, me, (*chip, c), from_src=True) for j, chip in enumerate(chips)]
        for cp in first:
            cp.start()
        passed = [copy(4 + j, (*chip, c), sibling) for j, chip in enumerate(chips)]
        for j, chip in enumerate(chips):
            copy(1 + j, (*chip, c), me).wait_recv()
            passed[j].start()
        copy(0, sibling, me).wait_recv()
        for j, chip in enumerate(chips):
            copy(4 + j, (*chip, 1 - c), me).wait_recv()
        for cp in first + passed:
            cp.wait_send()
        mine.wait()
        total = gathered[0]
        for d in range(1, N_DEV):
            total = total + gathered[d]
        sum_ref[...] = total

    return pl.pallas_call(
        body, name="small_grads_all_reduce",
        out_shape=jax.ShapeDtypeStruct((rows, width), F32),
        in_specs=[pl.BlockSpec(memory_space=pltpu.VMEM)],
        out_specs=pl.BlockSpec(memory_space=pltpu.VMEM),
        scratch_shapes=[pltpu.VMEM((N_DEV, rows, width), F32),
                        pltpu.SemaphoreType.DMA((7,)), pltpu.SemaphoreType.DMA((7,)), pltpu.SemaphoreType.DMA],
    )(block)


def _load_ffn_weights(win_hbm, wout_hbm, win, wout, sem):
    a = pltpu.make_async_copy(win_hbm, win, sem.at[0])
    b = pltpu.make_async_copy(wout_hbm, wout, sem.at[1])
    a.start()
    b.start()
    a.wait()
    b.wait()


def _ffn_forward(h_in, gain, win8, wout, name, head=None, ride=None):
    tm, nt = WIDE_TILE, SEQ // WIDE_TILE

    def body(*refs):
        if head is None:
            x_ref, g_ref, win_hbm, wout_hbm, out_ref, gu_ref, win, wout, sem = refs
        else:
            x_ref, g_ref, win_hbm, wout_hbm, tgt_ref, gf_ref, out_ref, gu_ref, loss_ref, dgf_ref, win, wout, sem = refs
        i = pl.program_id(0)

        @pl.when(i == 0)
        def _():
            _load_ffn_weights(win_hbm, wout_hbm, win, wout, sem)
            if head is not None:
                loss_ref[...] = jnp.zeros_like(loss_ref)
                dgf_ref[...] = jnp.zeros_like(dgf_ref)

        x = x_ref[...]
        xn, _, _ = _rms(x, g_ref[...])
        xb = xn.astype(BF16)
        acc = jnp.zeros((tm, D_MODEL), F32)
        for j in range(N_FF_GROUPS):
            gate = _dot_nt(xb, win[j])
            up = _dot_nt(xb, win[j + N_FF_GROUPS])
            gu_ref[j] = gate.astype(BF16)
            gu_ref[j + N_FF_GROUPS] = up.astype(BF16)
            act = gate * _sig(gate) * up
            acc = acc + _dot(act.astype(BF16), wout[j])
        h = x + FFN_RES_WEIGHT * acc
        if head is None:
            out_ref[...] = h
        else:
            gf = gf_ref[...]
            y, hhat, r = _rms(h, gf)
            err = y - tgt_ref[...]
            loss_ref[...] += jnp.full(loss_ref.shape, 0.5 / D_MODEL * jnp.sum(err * err), F32)
            dy = err * (1.0 / D_MODEL)
            dgf_ref[...] += jnp.sum(dy * hhat, axis=0, keepdims=True)
            out_ref[...] = _rms_bwd(dy * gf, hhat, r)

    weights = 2 * D_MODEL * 2 * D_FF + 2 * D_FF * D_MODEL
    tiles = 2 * (2 * 4 * tm * D_MODEL + 2 * tm * 2 * D_FF) + (2 * 4 * tm * D_MODEL if head else 0)
    in_specs = [_row_spec(tm, D_MODEL), _full_spec((1, D_MODEL)), ANY, ANY]
    out_shape = [jax.ShapeDtypeStruct((SEQ, D_MODEL), F32), jax.ShapeDtypeStruct((N_DEV, SEQ, FF_SHARD), BF16)]
    out_specs = [_row_spec(tm, D_MODEL), pl.BlockSpec((N_DEV, tm, FF_SHARD), lambda i: (0, i, 0))]
    args = [h_in, gain, win8, wout]
    if head is not None:
        in_specs += [_row_spec(tm, D_MODEL), _full_spec((1, D_MODEL))]
        out_shape += [jax.ShapeDtypeStruct((1, 128), F32), jax.ShapeDtypeStruct((1, D_MODEL), F32)]
        out_specs += [_full_spec((1, 128)), _full_spec((1, D_MODEL))]
        args += list(head)
    return _call(
        body, name=name, grid=(nt,), in_specs=in_specs, out_specs=out_specs, out_shape=out_shape,
        scratch_shapes=[pltpu.VMEM((N_DEV, FF_SHARD, D_MODEL), BF16), pltpu.VMEM((N_FF_GROUPS, FF_SHARD, D_MODEL), BF16),
                        pltpu.SemaphoreType.DMA((2,))],
        vmem_bytes=weights + tiles + 16 * tm * FF_SHARD * 4, args=args, ride=ride)


def _ffn_backward(dh_out, h_in, gain, gu, win8, wout, name, after=None):
    tm, nt = TOKEN_TILE, SEQ // TOKEN_TILE

    def body(dh_ref, x_ref, g_ref, gu_ref, win_hbm, wout_hbm,
             dhin_ref, dgu_ref, act_ref, xn_ref, df_ref, dg_ref, win, wout, sem):
        i = pl.program_id(0)

        @pl.when(i == 0)
        def _():
            _load_ffn_weights(win_hbm, wout_hbm, win, wout, sem)
            dg_ref[...] = jnp.zeros_like(dg_ref)

        dh = dh_ref[...]
        g = g_ref[...]
        xn, xhat, r = _rms(x_ref[...], g)
        df = (FFN_RES_WEIGHT * dh).astype(BF16)
        dxn = jnp.zeros((tm, D_MODEL), F32)
        for j in range(N_FF_GROUPS):
            gate = gu_ref[j].astype(F32)
            up = gu_ref[j + N_FF_GROUPS].astype(F32)
            dact = _dot_nt(df, wout[j])
            s = _sig(gate)
            silu = gate * s
            dgate = (dact * up * (s * (1.0 + gate * (1.0 - s)))).astype(BF16)
            dup = (dact * silu).astype(BF16)
            act_ref[j] = (silu * up).astype(BF16)
            dgu_ref[j] = dgate
            dgu_ref[j + N_FF_GROUPS] = dup
            dxn = dxn + _dot(dgate, win[j]) + _dot(dup, win[j + N_FF_GROUPS])
        dg_ref[...] += jnp.sum(dxn * xhat, axis=0, keepdims=True)
        dhin_ref[...] = dh + _rms_bwd(dxn * g, xhat, r)
        xn_ref[...] = xn.astype(BF16)
        df_ref[...] = df

    weights = 2 * D_MODEL * 2 * D_FF + 2 * D_FF * D_MODEL
    tiles = 2 * (3 * 4 * tm * D_MODEL + 2 * tm * (2 * 2 * D_FF + D_FF) + 2 * 2 * tm * D_MODEL)
    gu_spec = pl.BlockSpec((N_DEV, tm, FF_SHARD), lambda i: (0, i, 0))
    return _call(
        body, name=name, grid=(nt,),
        in_specs=[_row_spec(tm, D_MODEL), _row_spec(tm, D_MODEL), _full_spec((1, D_MODEL)), gu_spec, ANY, ANY],
        out_specs=[_row_spec(tm, D_MODEL), gu_spec, pl.BlockSpec((N_FF_GROUPS, tm, FF_SHARD), lambda i: (0, i, 0)),
                   _row_spec(tm, D_MODEL), _row_spec(tm, D_MODEL), _full_spec((1, D_MODEL))],
        out_shape=[jax.ShapeDtypeStruct((SEQ, D_MODEL), F32), jax.ShapeDtypeStruct((N_DEV, SEQ, FF_SHARD), BF16),
                   jax.ShapeDtypeStruct((N_FF_GROUPS, SEQ, FF_SHARD), BF16), jax.ShapeDtypeStruct((SEQ, D_MODEL), BF16),
                   jax.ShapeDtypeStruct((SEQ, D_MODEL), BF16), jax.ShapeDtypeStruct((1, D_MODEL), F32)],
        scratch_shapes=[pltpu.VMEM((N_DEV, FF_SHARD, D_MODEL), BF16), pltpu.VMEM((N_FF_GROUPS, FF_SHARD, D_MODEL), BF16),
                        pltpu.SemaphoreType.DMA((2,))],
        vmem_bytes=weights + tiles + 20 * tm * FF_SHARD * 4, args=[dh_out, h_in, gain, gu, win8, wout], after=after)[0]


def _to_sibling(src, dst, send_sem, recv_sem):
    x, y, c = _my_position()
    return pltpu.make_async_remote_copy(src_ref=src, dst_ref=dst, send_sem=send_sem, recv_sem=recv_sem,
                                        device_id=(x, y, 1 - c), device_id_type=MESH)


def _weight_grad(x, g, n_out, x_spec, g_spec, k_dim, n_dim, name, halves=False, tt=2048, ride=None, after=None):
    nt = SEQ // tt
    n_chips = N_DEV // 2
    rows = k_dim // 2 if halves else k_dim

    def body(x_ref, g_ref, out_ref, acc, sendbuf, recvbuf, send_sems, recv_sems):
        b, t = pl.program_id(0), pl.program_id(1)
        c = lax.axis_index("c")

        def push(q):
            return _to_sibling(sendbuf.at[q], recvbuf.at[q], send_sems.at[q], recv_sems.at[q])

        @pl.when(t == 0)
        def _():
            acc[...] = jnp.zeros_like(acc)

        acc[...] += _dot_tn(x_ref[...], g_ref[...])

        @pl.when(t == nt - 1)
        def _():
            if halves:
                for mine, other in ((0, 1), (1, 0)):
                    @pl.when(c == mine)
                    def _():
                        out_ref[b] = acc[pl.ds(mine * rows, rows), :].astype(BF16)
                        sendbuf[b] = acc[pl.ds(other * rows, rows), :].astype(BF16)
                push(b).start()
            else:
                q = b // 2

                @pl.when(b % 2 == c)
                def _():
                    out_ref[q] = acc[...].astype(BF16)

                @pl.when(b % 2 != c)
                def _():
                    sendbuf[q] = acc[...].astype(BF16)
                    push(q).start()

        @pl.when((b == n_out - 1) & (t == nt - 1))
        def _():
            for q in range(n_chips):
                push(q).wait_recv()
                out_ref[q] = (out_ref[q].astype(F32) + recvbuf[q].astype(F32)).astype(BF16)
            for q in range(n_chips):
                push(q).wait_send()

    piece = (n_chips, rows, n_dim)
    outs, ride_outs = _call(
        body, name=name, grid=(n_out, nt), in_specs=[x_spec(tt), g_spec(tt)],
        out_specs=[pl.BlockSpec(piece, lambda b, t: (0, 0, 0))],
        out_shape=[jax.ShapeDtypeStruct(piece, BF16)],
        scratch_shapes=[pltpu.VMEM((k_dim, n_dim), F32), pltpu.VMEM(piece, BF16), pltpu.VMEM(piece, BF16),
                        pltpu.SemaphoreType.DMA((n_chips,)), pltpu.SemaphoreType.DMA((n_chips,))],
        vmem_bytes=2 * 2 * tt * (k_dim + n_dim) + 8 * k_dim * n_dim + 4 * 2 * n_chips * rows * n_dim, args=[x, g], ride=ride,
        after=after)
    return outs[0], ride_outs


def _ffn_w_out_grad(act, df, tag, ride=None, after=None):
    return _weight_grad(
        act, df, N_FF_GROUPS,
        lambda tt: pl.BlockSpec((None, tt, FF_SHARD), lambda b, t: (b, t, 0)),
        lambda tt: pl.BlockSpec((tt, D_MODEL), lambda b, t: (t, 0)),
        FF_SHARD, D_MODEL, name=f"ffn{tag}_w_out_grad", halves=True, ride=ride, after=after)


def _ffn_w_in_grad(xn, dgu, tag, ride=None):
    return _weight_grad(
        dgu, xn, N_DEV,
        lambda tt: pl.BlockSpec((None, tt, FF_SHARD), lambda b, t: (b, t, 0)),
        lambda tt: pl.BlockSpec((tt, D_MODEL), lambda b, t: (t, 0)),
        FF_SHARD, D_MODEL, name=f"ffn{tag}_w_in_grad", ride=ride)


def _load_mix_weight(wmix_hbm, wmix, sem):
    copies = [pltpu.make_async_copy(wmix_hbm.at[d], wmix.at[:, pl.ds(d * MIX_SHARD, MIX_SHARD)], sem.at[d])
              for d in range(N_DEV)]
    for cp in copies:
        cp.start()
    for cp in copies:
        cp.wait()


def _load_pool_weight(pw_hbm, pw, sem):
    rows = POOL_GROUP_DIM // N_DEV
    copies = [pltpu.make_async_copy(pw_hbm.at[d], pw.at[:, pl.ds(d * rows, rows), :], sem.at[d]) for d in range(N_DEV)]
    for cp in copies:
        cp.start()
    for cp in copies:
        cp.wait()


def _rotate(x1, x2, cos, sin):
    return x1 * cos - x2 * sin, x1 * sin + x2 * cos


def _mix_proj_forward(h1, gain, wmix8, cos, sin, ride=None):
    tm, nt = WIDE_TILE, SEQ // WIDE_TILE
    k_scale = HEAD_DIM ** -0.5

    def body(h_ref, g_ref, wmix_hbm, cos_ref, sin_ref, u_ref, qkvg_ref, p_ref, gates_ref, wmix, sem):
        @pl.when(pl.program_id(0) == 0)
        def _():
            _load_mix_weight(wmix_hbm, wmix, sem)

        u = _rms(h_ref[...], g_ref[...])[0].astype(BF16)
        u_ref[...] = u
        cos_t, sin_t = cos_ref[...], sin_ref[...]
        for seg in range(N_SEG):
            pr = _dot(u, wmix[:, pl.ds(seg * D_MODEL, D_MODEL)])
            if seg < 2:
                scale = 1.0 if seg == 0 else k_scale
                for hd in range(HEADS):
                    lo = hd * HEAD_DIM
                    o1, o2 = _rotate(pr[:, lo:lo + ROT_HALF], pr[:, lo + ROT_HALF:lo + HEAD_DIM], cos_t, sin_t)
                    qkvg_ref[:, pl.ds(seg * D_MODEL + lo, ROT_HALF)] = (o1 * scale).astype(BF16)
                    qkvg_ref[:, pl.ds(seg * D_MODEL + lo + ROT_HALF, ROT_HALF)] = (o2 * scale).astype(BF16)
            elif seg < 4:
                qkvg_ref[:, pl.ds(seg * D_MODEL, D_MODEL)] = pr.astype(BF16)
            elif seg == 4:
                p_ref[...] = pr
            else:
                gates_ref[:, pl.ds((seg - 5) * D_MODEL, D_MODEL)] = pr.astype(BF16)

    est = 2 * D_MODEL * N_SEG * D_MODEL + 2 * tm * (4 * D_MODEL + 2 * D_MODEL + 2 * 4 * D_MODEL + 4 * D_MODEL + 2 * 2 * D_MODEL)
    return _call(
        body, name="mix_proj_fwd", grid=(nt,),
        in_specs=[_row_spec(tm, D_MODEL), _full_spec((1, D_MODEL)), ANY, _row_spec(tm, ROT_HALF), _row_spec(tm, ROT_HALF)],
        out_specs=[_row_spec(tm, D_MODEL), _row_spec(tm, 4 * D_MODEL), _row_spec(tm, D_MODEL), _row_spec(tm, 2 * D_MODEL)],
        out_shape=[jax.ShapeDtypeStruct((SEQ, D_MODEL), BF16), jax.ShapeDtypeStruct((SEQ, 4 * D_MODEL), BF16),
                   jax.ShapeDtypeStruct((SEQ, D_MODEL), F32), jax.ShapeDtypeStruct((SEQ, 2 * D_MODEL), BF16)],
        scratch_shapes=[pltpu.VMEM((D_MODEL, N_SEG * D_MODEL), BF16), pltpu.SemaphoreType.DMA((N_DEV,))],
        vmem_bytes=est + 8 * tm * D_MODEL * 4, args=[h1, gain, wmix8, cos, sin], ride=ride)


def _seg_block_spec(seg, reverse=False):
    nb = SEQ // RET_BLOCK
    if reverse:
        return pl.BlockSpec((RET_BLOCK, D_MODEL), lambda i, s=seg: (nb - 1 - i, s))
    return pl.BlockSpec((RET_BLOCK, D_MODEL), lambda i, s=seg: (i, s))


def _table_specs():
    return [_full_spec((HEADS, RET_BLOCK, RET_BLOCK)), _full_spec((HEADS, RET_BLOCK, 1)),
            _full_spec((HEADS, RET_BLOCK, 1)), _full_spec((HEADS, 1, 1))]


def _head_cols(h):
    return pl.ds(h * HEAD_DIM, HEAD_DIM)


def _retention_forward(qkvg, tables, ride=None):
    nb = SEQ // RET_BLOCK

    def body(q_ref, k_ref, v_ref, gr_ref, mask_ref, qdec_ref, kdec_ref, cdec_ref, ret_ref, o_ref, state):
        @pl.when(pl.program_id(0) == 0)
        def _():
            state[...] = jnp.zeros_like(state)

        for h in range(HEADS):
            cols = _head_cols(h)
            q, k, v = q_ref[:, cols], k_ref[:, cols], v_ref[:, cols]
            scores = _dot_nt(q, k) * mask_ref[h]
            inner = _dot(scores.astype(BF16), v)
            cross = _dot((q.astype(F32) * qdec_ref[h]).astype(BF16), state[h].astype(BF16))
            ret = inner + cross
            state[h] = state[h] * cdec_ref[h] + _dot_tn((k.astype(F32) * kdec_ref[h]).astype(BF16), v)
            ret_ref[:, cols] = ret
            retn = ret * lax.rsqrt(jnp.mean(ret * ret, axis=-1, keepdims=True) + NORM_EPS)
            gr = gr_ref[:, cols].astype(F32)
            o_ref[:, cols] = (retn * (gr * _sig(gr))).astype(BF16)

    return _call(
        body, name="retention_fwd", grid=(nb,),
        in_specs=[_seg_block_spec(0), _seg_block_spec(1), _seg_block_spec(2), _seg_block_spec(3)] + _table_specs(),
        out_specs=[_row_spec(RET_BLOCK, D_MODEL)] * 2,
        out_shape=[jax.ShapeDtypeStruct((SEQ, D_MODEL), F32), jax.ShapeDtypeStruct((SEQ, D_MODEL), BF16)],
        scratch_shapes=[pltpu.VMEM((HEADS, HEAD_DIM, HEAD_DIM), F32)],
        vmem_bytes=24 * RET_BLOCK * D_MODEL * 4, args=[qkvg, qkvg, qkvg, qkvg, *tables], ride=ride)


def _retention_backward_q(qkvg, dret, tables, ride=None):
    nb = SEQ // RET_BLOCK

    def body(k_ref, v_ref, do_ref, mask_ref, qdec_ref, kdec_ref, cdec_ref, dq_ref, state):
        @pl.when(pl.program_id(0) == 0)
        def _():
            state[...] = jnp.zeros_like(state)

        for h in range(HEADS):
            cols = _head_cols(h)
            k, v, do = k_ref[:, cols], v_ref[:, cols], do_ref[:, cols]
            dscores = _dot_nt(do, v) * mask_ref[h]
            dq_ref[:, cols] = _dot(dscores.astype(BF16), k) + _dot_nt(do, state[h].astype(BF16)) * qdec_ref[h]
            state[h] = state[h] * cdec_ref[h] + _dot_tn((k.astype(F32) * kdec_ref[h]).astype(BF16), v)

    return _call(
        body, name="retention_bwd_q", grid=(nb,),
        in_specs=[_seg_block_spec(1), _seg_block_spec(2), _row_spec(RET_BLOCK, D_MODEL)] + _table_specs(),
        out_specs=[_row_spec(RET_BLOCK, D_MODEL)],
        out_shape=[jax.ShapeDtypeStruct((SEQ, D_MODEL), F32)],
        scratch_shapes=[pltpu.VMEM((HEADS, HEAD_DIM, HEAD_DIM), F32)],
        vmem_bytes=24 * RET_BLOCK * D_MODEL * 4, args=[qkvg, qkvg, dret, *tables], ride=ride)


def _retention_backward_kv(qkvg, dret, tables, ride=None):
    nb = SEQ // RET_BLOCK

    def body(q_ref, k_ref, v_ref, do_ref, mask_ref, qdec_ref, kdec_ref, cdec_ref, dk_ref, dv_ref, gstate):
        @pl.when(pl.program_id(0) == 0)
        def _():
            gstate[...] = jnp.zeros_like(gstate)

        for h in range(HEADS):
            cols = _head_cols(h)
            q, k, v, do = q_ref[:, cols], k_ref[:, cols], v_ref[:, cols], do_ref[:, cols]
            mask = mask_ref[h]
            scores = (_dot_nt(q, k) * mask).astype(BF16)
            dscores = (_dot_nt(do, v) * mask).astype(BF16)
            gs = gstate[h].astype(BF16)
            dk_ref[:, cols] = _dot_tn(dscores, q) + _dot_nt(v, gs) * kdec_ref[h]
            dv_ref[:, cols] = _dot_tn(scores, do) + _dot((k.astype(F32) * kdec_ref[h]).astype(BF16), gs)
            gstate[h] = gstate[h] * cdec_ref[h] + _dot_tn((q.astype(F32) * qdec_ref[h]).astype(BF16), do)

    rev = lambda: pl.BlockSpec((RET_BLOCK, D_MODEL), lambda i: (nb - 1 - i, 0))
    return _call(
        body, name="retention_bwd_kv", grid=(nb,),
        in_specs=[_seg_block_spec(0, True), _seg_block_spec(1, True), _seg_block_spec(2, True), rev()] + _table_specs(),
        out_specs=[rev(), rev()],
        out_shape=[jax.ShapeDtypeStruct((SEQ, D_MODEL), F32)] * 2,
        scratch_shapes=[pltpu.VMEM((HEADS, HEAD_DIM, HEAD_DIM), F32)],
        vmem_bytes=32 * RET_BLOCK * D_MODEL * 4, args=[qkvg, qkvg, qkvg, dret, *tables], ride=ride)


def _pooled(p_ext, first_row):
    rows = p_ext.shape[0]
    t = first_row + lax.broadcasted_iota(jnp.int32, (rows - HALO, 1), 0)
    outs = []
    for g, w in enumerate(POOL_WINDOWS):
        e = p_ext[:, g * POOL_GROUP_DIM:(g + 1) * POOL_GROUP_DIM]
        s, span = e, 1
        while span < w:
            s = s + pltpu.roll(s, span, 0)
            span *= 2
        count = jnp.minimum(t + 1, w).astype(F32)
        outs.append(s[HALO:] / count - e[HALO:])
    return outs


def _pooled_transpose(d_ext, first_row):
    rows = d_ext.shape[0]
    t = first_row + lax.broadcasted_iota(jnp.int32, (rows, 1), 0)
    outs = []
    for g, w in enumerate(POOL_WINDOWS):
        d = d_ext[:, g * POOL_GROUP_DIM:(g + 1) * POOL_GROUP_DIM]
        e = jnp.where(t < SEQ, d / jnp.minimum(t + 1, w).astype(F32), 0.0)
        s, span = e, 1
        while span < w:
            s = s + pltpu.roll(s, rows - span, 0)
            span *= 2
        outs.append(s[:rows - HALO] - d[:rows - HALO])
    return outs


def _mix_tail_specs(tm):
    halo_blocks = tm // HALO
    return [
        _row_spec(tm, D_MODEL),
        pl.BlockSpec((HALO, D_MODEL), lambda i: (jnp.maximum(i * halo_blocks - 1, 0), 0)),
        _row_spec(tm, 2 * D_MODEL),
        _row_spec(tm, D_MODEL),
        _full_spec((2, D_MODEL)), _full_spec((1, D_MODEL)), ANY,
        _full_spec((D_MODEL, D_MODEL)), _full_spec((D_MODEL, D_MODEL)), _full_spec((D_MODEL, D_MODEL)),
    ]


def _mix_tail_compute(i, tm, p_ref, halo_ref, gates_ref, oret_ref, bias_ref, scale_ref, pw, wru_ref, wpu_ref, saved=None):
    halo = jnp.where(i > 0, halo_ref[...], 0.0)
    pooled = _pooled(jnp.concatenate([halo, p_ref[...]], axis=0), i * tm)
    pooled = [x.astype(BF16) for x in pooled]
    mixed = jnp.concatenate([_dot(pooled[g], pw[g]) for g in range(len(POOL_WINDOWS))], axis=-1)
    pool_out = (mixed * scale_ref[...]).astype(BF16)
    o_ret = oret_ref[...]
    if saved is None:
        a = _dot(o_ret, wru_ref[...])
        b = _dot(pool_out, wpu_ref[...])
    else:
        a, b = saved[0][...].astype(F32), saved[1][...].astype(F32)
    z = gates_ref[...].astype(F32)
    g0 = _sig(z[:, :D_MODEL] + bias_ref[0:1, :])
    g1 = _sig(z[:, D_MODEL:] + bias_ref[1:2, :])
    merged = (g0 * a + g1 * b).astype(BF16)
    return pooled, mixed, pool_out, o_ret, a, b, g0, g1, merged


def _mix_tail_forward(p, gates, o_ret, h1, bias, scale, pw8, wru, wpu, wo, ride=None):
    tm, nt = TOKEN_TILE, SEQ // TOKEN_TILE

    def body(p_ref, halo_ref, gates_ref, oret_ref, bias_ref, scale_ref, pw_hbm, wru_ref, wpu_ref, wo_ref, h1_ref,
             h2_ref, a_ref, b_ref, pw, sem):
        i = pl.program_id(0)

        @pl.when(i == 0)
        def _():
            _load_pool_weight(pw_hbm, pw, sem)

        out = _mix_tail_compute(i, tm, p_ref, halo_ref, gates_ref, oret_ref, bias_ref, scale_ref, pw, wru_ref, wpu_ref)
        a_ref[...] = out[4].astype(BF16)
        b_ref[...] = out[5].astype(BF16)
        h2_ref[...] = h1_ref[...] + _dot(out[-1], wo_ref[...])

    est = 3 * 2 * 2 * D_MODEL * D_MODEL + 2 * tm * D_MODEL * (4 + 4 + 2 + 4 + 4) + 16 * tm * D_MODEL * 4
    return _call(
        body, name="mix_tail_fwd", grid=(nt,),
        in_specs=_mix_tail_specs(tm) + [_row_spec(tm, D_MODEL)],
        out_specs=[_row_spec(tm, D_MODEL)] * 3,
        out_shape=[jax.ShapeDtypeStruct((SEQ, D_MODEL), F32)] + [jax.ShapeDtypeStruct((SEQ, D_MODEL), BF16)] * 2,
        scratch_shapes=[pltpu.VMEM((len(POOL_WINDOWS), POOL_GROUP_DIM, POOL_GROUP_DIM), BF16), pltpu.SemaphoreType.DMA((N_DEV,))],
        vmem_bytes=est, args=[p, p, gates, o_ret, bias, scale, pw8, wru, wpu, wo, h1], ride=ride)


def _mix_tail_backward(dh2, p, gates, o_ret, ret, qkvg, a_saved, b_saved, bias, scale, pw8, wru, wpu, wo, ride=None):
    tm, nt = TOKEN_TILE, SEQ // TOKEN_TILE
    n_groups = len(POOL_WINDOWS)
    rows_per_dev = POOL_GROUP_DIM // N_DEV

    def body(p_ref, halo_ref, gates_ref, oret_ref, bias_ref, scale_ref, pw_hbm, wru_ref, wpu_ref, wo_ref,
             dh2_ref, ret_ref, gr_ref, a_ref, b_ref,
             dret_ref, dgr_ref, dgates_ref, dpooled_ref, dwo_ref, dwru_ref, dwpu_ref, dpw_ref, dbias_ref, dscale_ref,
             pw, sem, acc_wo, acc_wru, acc_wpu, acc_pw, send_sq, recv_sq, send_pw, recv_pw, send_sems, recv_sems):
        i = pl.program_id(0)

        @pl.when(i == 0)
        def _():
            _load_pool_weight(pw_hbm, pw, sem)
            for ref in (acc_wo, acc_wru, acc_wpu, acc_pw, dbias_ref, dscale_ref):
                ref[...] = jnp.zeros_like(ref)

        pooled, mixed, pool_out, o_ret, a, b, g0, g1, merged = _mix_tail_compute(
            i, tm, p_ref, halo_ref, gates_ref, oret_ref, bias_ref, scale_ref, pw, wru_ref, wpu_ref, saved=(a_ref, b_ref))
        dh2 = dh2_ref[...].astype(BF16)
        dm = _dot_nt(dh2, wo_ref[...])
        acc_wo[...] += _dot_tn(merged, dh2)
        da = (dm * g0).astype(BF16)
        db = (dm * g1).astype(BF16)
        dz0 = dm * a * g0 * (1.0 - g0)
        dz1 = dm * b * g1 * (1.0 - g1)
        dbias_ref[0:1, :] += jnp.sum(dz0, axis=0, keepdims=True)
        dbias_ref[1:2, :] += jnp.sum(dz1, axis=0, keepdims=True)
        dgates_ref[:, pl.ds(0, D_MODEL)] = dz0.astype(BF16)
        dgates_ref[:, pl.ds(D_MODEL, D_MODEL)] = dz1.astype(BF16)
        acc_wru[...] += _dot_tn(o_ret, da)
        acc_wpu[...] += _dot_tn(pool_out, db)
        d_oret = _dot_nt(da, wru_ref[...])
        d_pool_out = _dot_nt(db, wpu_ref[...])
        dscale_ref[...] += jnp.sum(d_pool_out * mixed, axis=0, keepdims=True)
        dmixed = (d_pool_out * scale_ref[...]).astype(BF16)
        for g in range(n_groups):
            dmg = dmixed[:, g * POOL_GROUP_DIM:(g + 1) * POOL_GROUP_DIM]
            acc_pw[g] += _dot_tn(pooled[g], dmg)
            dpooled_ref[:, pl.ds(g * POOL_GROUP_DIM, POOL_GROUP_DIM)] = _dot_nt(dmg, pw[g])
        gr = gr_ref[...].astype(F32)
        s = _sig(gr)
        silu = gr * s
        for hd in range(HEADS):
            cols = slice(hd * HEAD_DIM, (hd + 1) * HEAD_DIM)
            r_h = ret_ref[:, cols]
            rr = lax.rsqrt(jnp.mean(r_h * r_h, axis=-1, keepdims=True) + NORM_EPS)
            rhat = r_h * rr
            do_h = d_oret[:, cols]
            dgr_ref[:, cols] = (do_h * rhat * (s[:, cols] * (1.0 + gr[:, cols] * (1.0 - s[:, cols])))).astype(BF16)
            dret_ref[:, cols] = _rms_bwd(do_h * silu[:, cols], rhat, rr).astype(BF16)

        @pl.when(i == nt - 1)
        def _():
            c = lax.axis_index("c")
            rows = D_MODEL // N_DEV
            squares = ((acc_wo, dwo_ref), (acc_wru, dwru_ref), (acc_wpu, dwpu_ref))
            for q in range(n_chips):
                own = pl.multiple_of((2 * q + c) * rows, rows)
                other = pl.multiple_of((2 * q + 1 - c) * rows, rows)
                for t, (acc, out) in enumerate(squares):
                    out[q] = acc[pl.ds(own, rows), :].astype(BF16)
                    send_sq[t, q] = acc[pl.ds(other, rows), :].astype(BF16)
                own_pw = pl.multiple_of((2 * q + c) * rows_per_dev, rows_per_dev)
                other_pw = pl.multiple_of((2 * q + 1 - c) * rows_per_dev, rows_per_dev)
                dpw_ref[q] = acc_pw[:, pl.ds(own_pw, rows_per_dev), :].astype(BF16)
                send_pw[q] = acc_pw[:, pl.ds(other_pw, rows_per_dev), :].astype(BF16)
            pushes = [_to_sibling(send_sq, recv_sq, send_sems.at[0], recv_sems.at[0]),
                      _to_sibling(send_pw, recv_pw, send_sems.at[1], recv_sems.at[1])]
            for cp in pushes:
                cp.start()
            for cp in pushes:
                cp.wait_recv()
            for t, (acc, out) in enumerate(squares):
                out[...] = (out[...].astype(F32) + recv_sq[t].astype(F32)).astype(BF16)
            dpw_ref[...] = (dpw_ref[...].astype(F32) + recv_pw[...].astype(F32)).astype(BF16)
            for cp in pushes:
                cp.wait_send()

    n_chips = N_DEV // 2
    sq = (n_chips, D_MODEL // N_DEV, D_MODEL)
    pw_shape = (n_chips, n_groups, rows_per_dev, POOL_GROUP_DIM)
    est = (3 * 2 * 2 * D_MODEL * D_MODEL + 3 * 4 * D_MODEL * D_MODEL + 3 * 2 * 2 * D_MODEL * D_MODEL
           + 2 * tm * D_MODEL * (4 + 4 + 2 + 4 + 4 + 2 + 2 + 2 + 4 + 4) + 24 * tm * D_MODEL * 4)
    return _call(
        body, name="mix_tail_bwd", grid=(nt,),
        in_specs=_mix_tail_specs(tm) + [_row_spec(tm, D_MODEL), _row_spec(tm, D_MODEL), _row_spec(tm, D_MODEL, 3),
                                        _row_spec(tm, D_MODEL), _row_spec(tm, D_MODEL)],
        out_specs=[_row_spec(tm, D_MODEL), _row_spec(tm, D_MODEL), _row_spec(tm, 2 * D_MODEL), _row_spec(tm, D_MODEL),
                   _full_spec(sq), _full_spec(sq), _full_spec(sq), _full_spec(pw_shape),
                   _full_spec((2, D_MODEL)), _full_spec((1, D_MODEL))],
        out_shape=[jax.ShapeDtypeStruct((SEQ, D_MODEL), BF16), jax.ShapeDtypeStruct((SEQ, D_MODEL), BF16),
                   jax.ShapeDtypeStruct((SEQ, 2 * D_MODEL), BF16), jax.ShapeDtypeStruct((SEQ, D_MODEL), F32),
                   jax.ShapeDtypeStruct(sq, BF16), jax.ShapeDtypeStruct(sq, BF16), jax.ShapeDtypeStruct(sq, BF16),
                   jax.ShapeDtypeStruct(pw_shape, BF16),
                   jax.ShapeDtypeStruct((2, D_MODEL), F32), jax.ShapeDtypeStruct((1, D_MODEL), F32)],
        scratch_shapes=[pltpu.VMEM((n_groups, POOL_GROUP_DIM, POOL_GROUP_DIM), BF16), pltpu.SemaphoreType.DMA((N_DEV,)),
                        pltpu.VMEM((D_MODEL, D_MODEL), F32), pltpu.VMEM((D_MODEL, D_MODEL), F32),
                        pltpu.VMEM((D_MODEL, D_MODEL), F32), pltpu.VMEM((n_groups, POOL_GROUP_DIM, POOL_GROUP_DIM), F32),
                        pltpu.VMEM((3,) + sq, BF16), pltpu.VMEM((3,) + sq, BF16), pltpu.VMEM(pw_shape, BF16),
                        pltpu.VMEM(pw_shape, BF16), pltpu.SemaphoreType.DMA((2,)), pltpu.SemaphoreType.DMA((2,))],
        vmem_bytes=est, args=[p, p, gates, o_ret, bias, scale, pw8, wru, wpu, wo, dh2, ret, qkvg, a_saved, b_saved], ride=ride)


def _mix_proj_backward(dq, dk, dv, dgr, dpooled, dgates, cos, sin, h1, gain, dh2, wmix8, ride=None):
    tm, nt = TOKEN_TILE, SEQ // TOKEN_TILE
    halo_blocks = tm // HALO
    last_halo = SEQ // HALO - 1
    k_scale = HEAD_DIM ** -0.5

    def body(dq_ref, dk_ref, dv_ref, dgr_ref, dpool_ref, dhalo_ref, dgates_ref, cos_ref, sin_ref, h1_ref, g_ref,
             dh2_ref, wmix_hbm, dh1_ref, dproj_ref, dg_ref, wmix, sem):
        i = pl.program_id(0)

        @pl.when(i == 0)
        def _():
            _load_mix_weight(wmix_hbm, wmix, sem)
            dg_ref[...] = jnp.zeros_like(dg_ref)

        cos_t, sin_t = cos_ref[...], sin_ref[...]
        for seg, ref, scale in ((0, dq_ref, 1.0), (1, dk_ref, k_scale)):
            for hd in range(HEADS):
                lo = hd * HEAD_DIM
                d1, d2 = ref[:, lo:lo + ROT_HALF], ref[:, lo + ROT_HALF:lo + HEAD_DIM]
                dproj_ref[:, pl.ds(seg * D_MODEL + lo, ROT_HALF)] = ((d1 * cos_t + d2 * sin_t) * scale).astype(BF16)
                dproj_ref[:, pl.ds(seg * D_MODEL + lo + ROT_HALF, ROT_HALF)] = ((d2 * cos_t - d1 * sin_t) * scale).astype(BF16)
        dproj_ref[:, pl.ds(2 * D_MODEL, D_MODEL)] = dv_ref[...].astype(BF16)
        dproj_ref[:, pl.ds(3 * D_MODEL, D_MODEL)] = dgr_ref[...]
        dp = _pooled_transpose(jnp.concatenate([dpool_ref[...], dhalo_ref[...]], axis=0), i * tm)
        for g in range(len(POOL_WINDOWS)):
            dproj_ref[:, pl.ds(4 * D_MODEL + g * POOL_GROUP_DIM, POOL_GROUP_DIM)] = dp[g].astype(BF16)
        dproj_ref[:, pl.ds(5 * D_MODEL, 2 * D_MODEL)] = dgates_ref[...]
        du = jnp.zeros((tm, D_MODEL), F32)
        for seg in range(N_SEG):
            cols = pl.ds(seg * D_MODEL, D_MODEL)
            du = du + _dot_nt(dproj_ref[:, cols], wmix[:, cols])
        g = g_ref[...]
        _, xhat, r = _rms(h1_ref[...], g)
        dg_ref[...] += jnp.sum(du * xhat, axis=0, keepdims=True)
        dh1_ref[...] = dh2_ref[...] + _rms_bwd(du * g, xhat, r)

    est = 2 * D_MODEL * N_SEG * D_MODEL + 2 * tm * D_MODEL * (3 * 4 + 2 + 4 + 4 + 4 + 4 + 4 + 14) + 12 * tm * D_MODEL * 4
    return _call(
        body, name="mix_proj_bwd", grid=(nt,),
        in_specs=[_row_spec(tm, D_MODEL), _row_spec(tm, D_MODEL), _row_spec(tm, D_MODEL), _row_spec(tm, D_MODEL),
                  _row_spec(tm, D_MODEL),
                  pl.BlockSpec((HALO, D_MODEL), lambda i: (jnp.minimum((i + 1) * halo_blocks, last_halo), 0)),
                  _row_spec(tm, 2 * D_MODEL), _row_spec(tm, ROT_HALF), _row_spec(tm, ROT_HALF),
                  _row_spec(tm, D_MODEL), _full_spec((1, D_MODEL)), _row_spec(tm, D_MODEL), ANY],
        out_specs=[_row_spec(tm, D_MODEL), _row_spec(tm, N_SEG * D_MODEL), _full_spec((1, D_MODEL))],
        out_shape=[jax.ShapeDtypeStruct((SEQ, D_MODEL), F32), jax.ShapeDtypeStruct((SEQ, N_SEG * D_MODEL), BF16),
                   jax.ShapeDtypeStruct((1, D_MODEL), F32)],
        scratch_shapes=[pltpu.VMEM((D_MODEL, N_SEG * D_MODEL), BF16), pltpu.SemaphoreType.DMA((N_DEV,))],
        vmem_bytes=est, args=[dq, dk, dv, dgr, dpooled, dpooled, dgates, cos, sin, h1, gain, dh2, wmix8], ride=ride)


def _adamw(w, parts, m, v, name, after=None):
    rows, cols = w.shape
    n_lists = len(parts)
    tr = max([t for t in range(16, 257, 16) if rows % t == 0], default=rows)
    c1 = 1.0 - ADAM_B1 ** ADAM_STEP
    c2 = 1.0 - ADAM_B2 ** ADAM_STEP

    def body(*refs):
        w_ref, m_ref, v_ref = refs[:3]
        part_refs = refs[3:3 + n_lists]
        g_out, d_out, m_out, v_out = refs[-4:]
        g = None
        for p_ref in part_refs:
            for k in range(p_ref.shape[0]):
                term = p_ref[k].astype(F32)
                g = term if g is None else g + term
        m_new = ADAM_B1 * m_ref[...] + (1.0 - ADAM_B1) * g
        v_new = ADAM_B2 * v_ref[...] + (1.0 - ADAM_B2) * (g * g)
        g_out[...] = g
        m_out[...] = m_new
        v_out[...] = v_new
        d_out[...] = -ADAM_LR * ((m_new / c1) / (jnp.sqrt(v_new / c2) + ADAM_EPS) + ADAM_WD * w_ref[...])

    spec = pl.BlockSpec((tr, cols), lambda i: (i, 0))
    out = jax.ShapeDtypeStruct((rows, cols), F32)
    part_specs = [pl.BlockSpec((p.shape[0], tr, cols), lambda i: (0, i, 0)) for p in parts]
    part_bytes = sum(p.shape[0] * p.dtype.itemsize for p in parts)
    extra = [] if after is None else [after]
    return pl.pallas_call(
        body, name=name, grid=(rows // tr,),
        in_specs=[spec, spec, spec] + part_specs + [ANY] * len(extra),
        out_specs=[spec] * 4, out_shape=[out] * 4,
        compiler_params=_params(2 * tr * cols * (7 * 4 + part_bytes) + 8 * tr * cols * 4, 1),
    )(_in_hbm(w), _in_hbm(m), _in_hbm(v), *[_in_hbm(p) for p in parts], *extra)


def _mix_w_in_grad(u, dproj, ride=None):
    return _weight_grad(
        u, dproj, N_DEV,
        lambda tt: pl.BlockSpec((tt, D_MODEL), lambda b, t: (t, 0)),
        lambda tt: pl.BlockSpec((tt, MIX_SHARD), lambda b, t: (t, b)),
        D_MODEL, MIX_SHARD, name="w_in_grad", ride=ride)


def kernel(x, norm_ffn1, ffn1_w_in, ffn1_w_out, norm_mix, w_in, gate_bias, pool_w, pool_scale, w_ret_up, w_pool_up, w_out, norm_ffn2, ffn2_w_in, ffn2_w_out, norm_final, loss_target, m_norm_ffn1, m_ffn1_w_in, m_ffn1_w_out, m_norm_mix, m_w_in, m_gate_bias, m_pool_w, m_pool_scale, m_w_ret_up, m_w_pool_up, m_w_out, m_norm_ffn2, m_ffn2_w_in, m_ffn2_w_out, m_norm_final, v_norm_ffn1, v_ffn1_w_in, v_ffn1_w_out, v_norm_mix, v_w_in, v_gate_bias, v_pool_w, v_pool_scale, v_w_ret_up, v_w_pool_up, v_w_out, v_norm_ffn2, v_ffn2_w_in, v_ffn2_w_out, v_norm_final):
    assert x.shape == (1, SEQ, D_MODEL) and ffn1_w_in.shape == (1, D_MODEL, FF_SHARD) and w_in.shape == (1, D_MODEL, MIX_SHARD)
    x2, target = x[0], loss_target[0]

    cos, sin = _rotary_tables()
    tables = _retention_tables()
    bf = lambda w: w[0].astype(BF16)
    bf_t = lambda w: jnp.swapaxes(w[0], 0, 1).astype(BF16)
    square = lambda w: w.reshape(D_MODEL, D_MODEL)

    win1, wout1, bias8 = _alone(_RelayGather([bf_t(ffn1_w_in), bf(ffn1_w_out), gate_bias[0]]), "ffn1_weights_all_gather")
    wout1 = wout1.reshape(N_FF_GROUPS, FF_SHARD, D_MODEL)
    bias = bias8.transpose(1, 0, 2).reshape(2, D_MODEL)

    (h1, gu1), (wmix8,) = _ffn_forward(x2, norm_ffn1, win1, wout1, "ffn1_fwd", ride=_GatherRide([bf(w_in)]))
    (u, qkvg, p, gates), (win2,) = _mix_proj_forward(h1, norm_mix, wmix8, cos, sin, ride=_GatherRide([bf_t(ffn2_w_in)]))
    (ret, o_ret), (pw8, wru, wpu, wo) = _retention_forward(
        qkvg, tables, ride=_GatherRide([bf(pool_w), bf(w_ret_up), bf(w_pool_up), bf(w_out)]))
    wru, wpu, wo = square(wru), square(wpu), square(wo)
    (h2, a_saved, b_saved), (wout2,) = _mix_tail_forward(p, gates, o_ret, h1, bias, pool_scale, pw8, wru, wpu, wo,
                                        ride=_GatherRide([bf(ffn2_w_out)]))
    wout2 = wout2.reshape(N_FF_GROUPS, FF_SHARD, D_MODEL)
    (dh3, gu2, loss_part, d_norm_final), _ = _ffn_forward(h2, norm_ffn2, win2, wout2, "ffn2_fwd_loss",
                                                          head=(target, norm_final.reshape(1, D_MODEL)))

    dh2, dgu2, act2, xn2, df2, d_norm_ffn2 = _ffn_backward(dh3, h2, norm_ffn2, gu2, win2, wout2, "ffn2_bwd")
    d_wout2, _ = _ffn_w_out_grad(act2, df2, 2)
    d_win2, (r_wout2,) = _ffn_w_in_grad(xn2, dgu2, 2, ride=_ScatterRide([d_wout2]))
    (dret, dgr, dgates, dpooled, d_wo, d_wru, d_wpu, d_pw, d_bias, d_scale), (r_win2,) = _mix_tail_backward(
        dh2, p, gates, o_ret, ret, qkvg, a_saved, b_saved, bias, pool_scale, pw8, wru, wpu, wo, ride=_ScatterRide([d_win2]))
    (dq,), _ = _retention_backward_q(qkvg, dret, tables)
    (dk, dv), (r_pw, r_wru, r_wpu, r_wo) = _retention_backward_kv(
        qkvg, dret, tables, ride=_ScatterRide([d_pw, d_wru, d_wpu, d_wo]))
    (dh1, dproj, d_norm_mix), _ = _mix_proj_backward(dq, dk, dv, dgr, dpooled, dgates, cos, sin, h1, norm_mix, dh2, wmix8)
    d_wmix, _ = _mix_w_in_grad(u, dproj)
    wmix_state, wmix_started = _scatter_start(d_wmix, "w_in_grad_exchange_start")
    grad_x, dgu1, act1, xn1, df1, d_norm_ffn1 = _ffn_backward(dh1, x2, norm_ffn1, gu1, win1, wout1, "ffn1_bwd", after=wmix_started)
    d_win1, _ = _ffn_w_in_grad(xn1, dgu1, 1)
    win1_state, win1_started = _scatter_start(d_win1, "ffn1_w_in_grad_exchange_start")
    d_wout1, _ = _ffn_w_out_grad(act1, df1, 1, after=win1_started)
    wout1_state, started = _scatter_start(d_wout1, "ffn1_w_out_grad_exchange_start")
    zero_row = jnp.zeros((1, D_MODEL), F32)
    small = _all_reduce_rows(jnp.concatenate(
        [d_norm_ffn1, d_norm_mix, d_scale, d_norm_ffn2, d_norm_final, d_bias, jnp.tile(loss_part, (1, D_MODEL // 128))],
        axis=0))
    loss = small[7, 0]

    results = {}

    def update(nm, w, parts, m, v, after):
        if nm in ("ffn1_w_in", "ffn2_w_in"):
            flat, back = (lambda a: jnp.swapaxes(a[0], 0, 1)), (lambda o: jnp.swapaxes(o, 0, 1)[None])
        else:
            flat, back = (lambda a: a.reshape(-1, w.shape[-1])), (lambda o: o.reshape(w.shape))
        parts = [p.reshape(p.shape[:1] + flat(w).shape) for p in parts]
        outs = _adamw(flat(w), parts, flat(m), flat(v), name=f"adamw_{nm}", after=after)
        results[nm] = [back(o) for o in outs]
        return outs[0]

    done = update("w_in", w_in, _scatter_wait(wmix_state, started, "w_in_grad_exchange_wait"), m_w_in, v_w_in, None)
    for nm, w, parts, m, v in (
            ("ffn2_w_in", ffn2_w_in, r_win2, m_ffn2_w_in, v_ffn2_w_in),
            ("ffn2_w_out", ffn2_w_out, r_wout2, m_ffn2_w_out, v_ffn2_w_out), ("w_ret_up", w_ret_up, r_wru, m_w_ret_up, v_w_ret_up),
            ("w_pool_up", w_pool_up, r_wpu, m_w_pool_up, v_w_pool_up), ("w_out", w_out, r_wo, m_w_out, v_w_out),
            ("pool_w", pool_w, r_pw, m_pool_w, v_pool_w)):
        done = update(nm, w, [parts], m, v, done)
    done = update("ffn1_w_in", ffn1_w_in, _scatter_wait(win1_state, done, "ffn1_w_in_grad_exchange_wait"),
                  m_ffn1_w_in, v_ffn1_w_in, None)
    update("ffn1_w_out", ffn1_w_out, _scatter_wait(wout1_state, done, "ffn1_w_out_grad_exchange_wait"),
           m_ffn1_w_out, v_ffn1_w_out, None)

    my_id = _linear_id(*_my_position())
    bias_cols = gate_bias.shape[-1]
    pad = lambda a: jnp.pad(a[0], ((0, 0), (0, D_MODEL - bias_cols)))
    pack = lambda a, b, c, d, e, gb: jnp.concatenate([a, b, c, d, e.reshape(1, D_MODEL), pad(gb), zero_row], axis=0)
    d_bias_mine = lax.dynamic_slice_in_dim(small[5:7], my_id * bias_cols, bias_cols, axis=1)
    g_small = jnp.concatenate([small[0:5], jnp.pad(d_bias_mine, ((0, 0), (0, D_MODEL - bias_cols))), zero_row], axis=0)
    s_outs = _adamw(pack(norm_ffn1, norm_mix, pool_scale, norm_ffn2, norm_final, gate_bias), [g_small[None]],
                    pack(m_norm_ffn1, m_norm_mix, m_pool_scale, m_norm_ffn2, m_norm_final, m_gate_bias),
                    pack(v_norm_ffn1, v_norm_mix, v_pool_scale, v_norm_ffn2, v_norm_final, v_gate_bias), name="adamw_small")
    for row, nm in enumerate(["norm_ffn1", "norm_mix", "pool_scale", "norm_ffn2"]):
        results[nm] = [o[row:row + 1] for o in s_outs]
    results["norm_final"] = [o[4] for o in s_outs]
    results["gate_bias"] = [o[5:7, :bias_cols][None] for o in s_outs]

    order = ["norm_ffn1", "ffn1_w_in", "ffn1_w_out", "norm_mix", "w_in", "gate_bias", "pool_w", "pool_scale",
             "w_ret_up", "w_pool_up", "w_out", "norm_ffn2", "ffn2_w_in", "ffn2_w_out", "norm_final"]
    return (loss, grad_x[None], *[results[nm][0] for nm in order], *[results[nm][1] for nm in order],
            *[results[nm][2] for nm in order], *[results[nm][3] for nm in order])
```

```python
import functools

import numpy as np
import jax
import jax.numpy as jnp
from jax import lax
from jax.experimental import pallas as pl
from jax.experimental.pallas import tpu as pltpu

F32 = jnp.float32
BF16 = jnp.bfloat16

N_DEV = 8
D_MODEL = 1024
SEQ = 4096
D_FF = 2816
FF_SHARD = 2 * D_FF // N_DEV
N_FF_GROUPS = N_DEV // 2
HEADS = 4
HEAD_DIM = 256
ROT_HALF = HEAD_DIM // 2
CHUNK = 64
RET_BLOCK = 256
POOL_WINDOWS = (2, 4, 8, 16)
POOL_GROUP_DIM = 256
HALO = 16
MIX_SHARD = 7 * D_MODEL // N_DEV
N_SEG = 7
ROPE_BASE = 10000.0
NORM_EPS = 1e-6
FFN_RES_WEIGHT = 0.5
ADAM_LR, ADAM_B1, ADAM_B2, ADAM_EPS, ADAM_WD, ADAM_STEP = 0.001, 0.9, 0.999, 1e-08, 0.01, 10

TOKEN_TILE = 256
WIDE_TILE = 512
VMEM_CAP_V7X = 64 * 1024 * 1024
MESH = pl.DeviceIdType.MESH
ANY = pl.BlockSpec(memory_space=pl.ANY)


def _vmem_limit(estimate_bytes):
    return int(min(estimate_bytes * 5 // 4 + (6 << 20), VMEM_CAP_V7X - (4 << 20)))


def _params(estimate_bytes, n_grid):
    return pltpu.CompilerParams(dimension_semantics=("arbitrary",) * n_grid,
                                vmem_limit_bytes=_vmem_limit(estimate_bytes))


def _dot(a, b):
    return jnp.dot(a, b, preferred_element_type=F32)


def _dot_nt(a, b):
    return lax.dot_general(a, b, (((1,), (1,)), ((), ())), preferred_element_type=F32)


def _dot_tn(a, b):
    return lax.dot_general(a, b, (((0,), (0,)), ((), ())), preferred_element_type=F32)


def _sig(x):
    return 1.0 / (1.0 + jnp.exp(-x))


def _rms(x, g):
    r = lax.rsqrt(jnp.mean(x * x, axis=-1, keepdims=True) + NORM_EPS)
    xhat = x * r
    return xhat * g, xhat, r


def _rms_bwd(dyg, xhat, r):
    return r * (dyg - xhat * jnp.mean(dyg * xhat, axis=-1, keepdims=True))


def _row_spec(tile, width, col=0):
    return pl.BlockSpec((tile, width), lambda i, c=col: (i, c))


def _full_spec(shape):
    return pl.BlockSpec(shape, lambda *_: (0,) * len(shape))


def _rotary_tables():
    inv_freq = (np.float32(ROPE_BASE) ** (-np.arange(ROT_HALF, dtype=np.float32) / np.float32(ROT_HALF))).astype(np.float32)
    ang = (np.arange(SEQ, dtype=np.float32)[:, None] * inv_freq[None, :]).astype(np.float32)
    return jnp.asarray(np.cos(ang.astype(np.float64)), F32), jnp.asarray(np.sin(ang.astype(np.float64)), F32)


def _retention_tables():
    log_gamma = np.log(1.0 - 2.0 ** (-5.0 - np.arange(HEADS, dtype=np.float64)))
    n = np.arange(RET_BLOCK)
    diff = (n[:, None] - n[None, :]).astype(np.float64)
    same = (n[:, None] // CHUNK) == (n[None, :] // CHUNK)
    earlier = (n[None, :] // CHUNK) < (n[:, None] // CHUNK)
    expo = np.where(same, np.abs(diff), diff)
    mask = np.where(same | earlier, np.exp(log_gamma[:, None, None] * expo[None]), 0.0)
    qdec = np.exp(log_gamma[:, None] * (n[None, :] + 1.0))[:, :, None]
    kdec = np.exp(log_gamma[:, None] * (RET_BLOCK - 1.0 - n[None, :]))[:, :, None]
    cdec = np.exp(log_gamma * RET_BLOCK)[:, None, None]
    return (jnp.asarray(mask, F32), jnp.asarray(qdec, F32), jnp.asarray(kdec, F32), jnp.asarray(cdec, F32))


def _my_position():
    return lax.axis_index("x"), lax.axis_index("y"), lax.axis_index("c")


def _linear_id(px, py, pc):
    return 4 * px + 2 * py + pc


def _when(pred, fn):
    if isinstance(pred, bool):
        if pred:
            fn()
    else:
        pl.when(pred)(fn)


class _GatherRide:
    def __init__(self, shards):
        self.args = list(shards)
        n = self.n = len(shards)
        self.out_shape = [pltpu.HBM((N_DEV,) + s.shape, s.dtype) for s in shards]
        self.scratch = [pltpu.SemaphoreType.DMA((n, 7)), pltpu.SemaphoreType.DMA((n, 7)), pltpu.SemaphoreType.DMA((n,))]

    def _plan(self, src, out, sems):
        send_sems, recv_sems, local_sem = sems
        x, y, c = _my_position()
        me, sibling = (x, y, c), (x, y, 1 - c)
        chips = [(1 - x, y), (x, 1 - y), (1 - x, 1 - y)]

        def copy(t, k, block, to, from_src=False):
            rows = out[t].at[_linear_id(*block)]
            return pltpu.make_async_remote_copy(
                src_ref=src[t] if from_src else rows, dst_ref=rows,
                send_sem=send_sems.at[t, k], recv_sem=recv_sems.at[t, k],
                device_id=to, device_id_type=MESH)

        def relay(t):
            return copy(t, 3, (x ^ (1 - c), y ^ c, c), (x ^ c, y ^ (1 - c), c))

        local = [pltpu.make_async_copy(src[t], out[t].at[_linear_id(*me)], local_sem.at[t]) for t in range(self.n)]
        return copy, relay, local, me, sibling, chips, c

    def begin(self, first, src, out, sems):
        copy, relay, local, me, sibling, chips, c = self._plan(src, out, sems)

        def start():
            for cp in local:
                cp.start()
            for t in range(self.n):
                copy(t, 0, me, sibling, from_src=True).start()
                for j in range(2):
                    copy(t, 1 + j, me, (*chips[j], c), from_src=True).start()

        _when(first, start)

    def finish(self, mid, last, src, out, sems):
        copy, relay, local, me, sibling, chips, c = self._plan(src, out, sems)

        def pass_on():
            for t in range(self.n):
                for j in range(2):
                    copy(t, 1 + j, (*chips[j], c), me).wait_recv()
                relay(t).start()
                for j in range(2):
                    copy(t, 4 + j, (*chips[j], c), sibling).start()

        def drain():
            for t in range(self.n):
                copy(t, 3, (*chips[2], c), me).wait_recv()
                copy(t, 6, (*chips[2], c), sibling).start()
            for t in range(self.n):
                copy(t, 0, sibling, me).wait_recv()
                for j in range(3):
                    copy(t, 4 + j, (*chips[j], 1 - c), me).wait_recv()
            for t in range(self.n):
                copy(t, 0, me, sibling, from_src=True).wait_send()
                for j in range(2):
                    copy(t, 1 + j, me, (*chips[j], c), from_src=True).wait_send()
                relay(t).wait_send()
                for j in range(3):
                    copy(t, 4 + j, (*chips[j], c), sibling).wait_send()
            for cp in local:
                cp.wait()

        _when(mid, pass_on)
        _when(last, drain)


class _ScatterRide:
    def __init__(self, chip_sums):
        self.args = list(chip_sums)
        n = self.n = len(chip_sums)
        self.out_shape = [pltpu.HBM(p.shape, p.dtype) for p in chip_sums]
        self.scratch = [pltpu.SemaphoreType.DMA((n, 3)), pltpu.SemaphoreType.DMA((n, 3)), pltpu.SemaphoreType.DMA((n,))]

    def _plan(self, src, out, sems):
        send_sems, recv_sems, local_sem = sems
        x, y, c = _my_position()

        def peer(k):
            return (x ^ (k >> 1), y ^ (k & 1))

        copies = [pltpu.make_async_remote_copy(
            src_ref=src[t].at[2 * peer(k)[0] + peer(k)[1]], dst_ref=out[t].at[k],
            send_sem=send_sems.at[t, k - 1], recv_sem=recv_sems.at[t, k - 1],
            device_id=(*peer(k), c), device_id_type=MESH) for t in range(self.n) for k in range(1, N_DEV // 2)]
        local = [pltpu.make_async_copy(src[t].at[2 * x + y], out[t].at[0], local_sem.at[t]) for t in range(self.n)]
        return copies, local

    def begin(self, first, src, out, sems):
        copies, local = self._plan(src, out, sems)

        def start():
            for cp in local + copies:
                cp.start()

        _when(first, start)

    def finish(self, mid, last, src, out, sems):
        copies, local = self._plan(src, out, sems)

        def drain():
            for cp in copies:
                cp.wait_recv()
            for cp in copies:
                cp.wait_send()
            for cp in local:
                cp.wait()

        _when(last, drain)


def _in_hbm(a):
    return pltpu.with_memory_space_constraint(a, pltpu.HBM)


def _call(body, *, name, grid, in_specs, out_specs, out_shape, scratch_shapes, vmem_bytes, args, ride=None, after=None):
    n_in, n_out, n_s = len(in_specs), len(out_specs), len(scratch_shapes)
    params = _params(vmem_bytes, len(grid))
    args = [_in_hbm(a) for a in args]
    out_shape = [pltpu.HBM(s.shape, s.dtype) for s in out_shape]
    if ride is None:
        if after is not None:
            def ordered_body(*refs):
                body(*refs[:n_in], *refs[n_in + 1:])
            outs = pl.pallas_call(ordered_body, name=name, grid=grid, in_specs=list(in_specs) + [ANY], out_specs=out_specs,
                                  out_shape=out_shape, scratch_shapes=scratch_shapes, compiler_params=params)(*args, after)
            return list(outs), []
        outs = pl.pallas_call(body, name=name, grid=grid, in_specs=in_specs, out_specs=out_specs, out_shape=out_shape,
                              scratch_shapes=scratch_shapes, compiler_params=params)(*args)
        return list(outs), []
    total = int(np.prod(grid))

    def riding_body(*refs):
        a = n_in
        b = a + ride.n
        c = b + n_out
        d = c + ride.n
        e = d + n_s
        step = pl.program_id(0)
        for axis in range(1, len(grid)):
            step = step * grid[axis] + pl.program_id(axis)
        ride.begin(step == 0, refs[a:b], refs[c:d], refs[e:])
        body(*refs[:a], *refs[b:c], *refs[d:e])
        ride.finish(step == total // 2, step == total - 1, refs[a:b], refs[c:d], refs[e:])

    outs = pl.pallas_call(
        riding_body, name=name, grid=grid, in_specs=list(in_specs) + [ANY] * ride.n,
        out_specs=list(out_specs) + [ANY] * ride.n, out_shape=list(out_shape) + ride.out_shape,
        scratch_shapes=list(scratch_shapes) + ride.scratch, compiler_params=params)(*args, *[_in_hbm(a) for a in ride.args])
    return list(outs[:n_out]), list(outs[n_out:])


def _alone(ride, name):
    def body(*refs):
        src, out, sems = refs[:ride.n], refs[ride.n:2 * ride.n], refs[2 * ride.n:]
        ride.begin(True, src, out, sems)
        ride.finish(True, True, src, out, sems)

    return list(pl.pallas_call(body, name=name, out_shape=ride.out_shape, in_specs=[ANY] * ride.n,
                               out_specs=[ANY] * ride.n, scratch_shapes=ride.scratch)(*[_in_hbm(a) for a in ride.args]))


def _scatter_copies(src, land, send_sems, recv_sems):
    x, y, c = _my_position()
    copies = []
    for k in range(1, N_DEV // 2):
        px, py = x ^ (k >> 1), y ^ (k & 1)
        copies.append(pltpu.make_async_remote_copy(
            src_ref=src.at[2 * px + py], dst_ref=land.at[k - 1], send_sem=send_sems.at[k - 1], recv_sem=recv_sems.at[k - 1],
            device_id=(px, py, c), device_id_type=MESH))
    return copies


def _scatter_start(chip_sums, name):
    n_peers = N_DEV // 2 - 1
    land_shape = (n_peers,) + chip_sums.shape[1:]
    hbm = pl.BlockSpec(memory_space=pltpu.HBM)
    sem = pl.BlockSpec(memory_space=pltpu.SEMAPHORE)

    def body(src_ref, land_ref, send_sems, recv_sems, src_thru, land_thru, token):
        for cp in _scatter_copies(src_ref, land_ref, send_sems, recv_sems):
            cp.start()
        token[...] = jnp.zeros_like(token)

    send_sems, recv_sems, src_thru, land_thru, token = pl.pallas_call(
        body, name=name,
        out_shape=(pltpu.SemaphoreType.DMA((n_peers,)), pltpu.SemaphoreType.DMA((n_peers,)),
                   pltpu.HBM(chip_sums.shape, chip_sums.dtype), pltpu.HBM(land_shape, chip_sums.dtype),
                   jax.ShapeDtypeStruct((8, 128), F32)),
        in_specs=(hbm, hbm), out_specs=(sem, sem, hbm, hbm, pl.BlockSpec(memory_space=pltpu.VMEM)),
        input_output_aliases={0: 2, 1: 3},
        compiler_params=pltpu.CompilerParams(has_side_effects=pltpu.SideEffectType.DATAFLOW_SIDE_EFFECTING),
    )(_in_hbm(chip_sums), _in_hbm(lax.empty(land_shape, chip_sums.dtype)))
    return (send_sems, recv_sems, src_thru, land_thru), token


def _scatter_wait(state, after, name):
    send_sems, recv_sems, src_thru, land_thru = state
    hbm = pl.BlockSpec(memory_space=pltpu.HBM)
    sem = pl.BlockSpec(memory_space=pltpu.SEMAPHORE)

    def body(src_ref, land_ref, send_sems, recv_sems, after_ref, src_out, land_out):
        for cp in _scatter_copies(src_ref, land_ref, send_sems, recv_sems):
            cp.wait_send()
            cp.wait_recv()

    src_done, land_done = pl.pallas_call(
        body, name=name,
        out_shape=(pltpu.HBM(src_thru.shape, src_thru.dtype), pltpu.HBM(land_thru.shape, land_thru.dtype)),
        in_specs=(hbm, hbm, sem, sem, ANY), out_specs=(hbm, hbm), input_output_aliases={0: 0, 1: 1},
        compiler_params=pltpu.CompilerParams(has_side_effects=pltpu.SideEffectType.DATAFLOW_SIDE_EFFECTING),
    )(src_thru, land_thru, send_sems, recv_sems, after)
    x, y, _ = _my_position()
    return lax.dynamic_slice_in_dim(src_done, 2 * x + y, 1, axis=0), land_done


def _all_reduce_rows(block):
    rows, width = block.shape

    def body(x_ref, sum_ref, gathered, send_sems, recv_sems, local_sem):
        x, y, c = _my_position()
        me, sibling = (x, y, c), (x, y, 1 - c)
        chips = [(1 - x, y), (x, 1 - y), (1 - x, 1 - y)]

        def slot(px, py, pc):
            return gathered.at[_linear_id(px, py, pc)]

        def copy(k, block_of, to, from_src=False):
            return pltpu.make_async_remote_copy(
                src_ref=x_ref if from_src else slot(*block_of), dst_ref=slot(*block_of),
                send_sem=send_sems.at[k], recv_sem=recv_sems.at[k], device_id=to, device_id_type=MESH)

        mine = pltpu.make_async_copy(x_ref, slot(*me), local_sem)
        mine.start()
        first = [copy(0, me, sibling, from_src=True)]
        first += [copy(1 + j, me, (*chip, c), from_src=True) for j, chip in enumerate(chips)]
        for cp in first:
            cp.start()
        passed = [copy(4 + j, (*chip, c), sibling) for j, chip in enumerate(chips)]
        for j, chip in enumerate(chips):
            copy(1 + j, (*chip, c), me).wait_recv()
            passed[j].start()
        copy(0, sibling, me).wait_recv()
        for j, chip in enumerate(chips):
            copy(4 + j, (*chip, 1 - c), me).wait_recv()
        for cp in first + passed:
            cp.wait_send()
        mine.wait()
        total = gathered[0]
        for d in range(1, N_DEV):
            total = total + gathered[d]
        sum_ref[...] = total

    return pl.pallas_call(
        body, name="small_grads_all_reduce",
        out_shape=jax.ShapeDtypeStruct((rows, width), F32),
        in_specs=[pl.BlockSpec(memory_space=pltpu.VMEM)],
        out_specs=pl.BlockSpec(memory_space=pltpu.VMEM),
        scratch_shapes=[pltpu.VMEM((N_DEV, rows, width), F32),
                        pltpu.SemaphoreType.DMA((7,)), pltpu.SemaphoreType.DMA((7,)), pltpu.SemaphoreType.DMA],
    )(block)


def _load_ffn_weights(win_hbm, wout_hbm, win, wout, sem):
    a = pltpu.make_async_copy(win_hbm, win, sem.at[0])
    b = pltpu.make_async_copy(wout_hbm, wout, sem.at[1])
    a.start()
    b.start()
    a.wait()
    b.wait()


def _ffn_forward(h_in, gain, win8, wout, name, head=None, ride=None):
    tm, nt = WIDE_TILE, SEQ // WIDE_TILE

    def body(*refs):
        if head is None:
            x_ref, g_ref, win_hbm, wout_hbm, out_ref, gu_ref, win, wout, sem = refs
        else:
            x_ref, g_ref, win_hbm, wout_hbm, tgt_ref, gf_ref, out_ref, gu_ref, loss_ref, dgf_ref, win, wout, sem = refs
        i = pl.program_id(0)

        @pl.when(i == 0)
        def _():
            _load_ffn_weights(win_hbm, wout_hbm, win, wout, sem)
            if head is not None:
                loss_ref[...] = jnp.zeros_like(loss_ref)
                dgf_ref[...] = jnp.zeros_like(dgf_ref)

        x = x_ref[...]
        xn, _, _ = _rms(x, g_ref[...])
        xb = xn.astype(BF16)
        acc = jnp.zeros((tm, D_MODEL), F32)
        for j in range(N_FF_GROUPS):
            gate = _dot_nt(xb, win[j])
            up = _dot_nt(xb, win[j + N_FF_GROUPS])
            gu_ref[j] = gate.astype(BF16)
            gu_ref[j + N_FF_GROUPS] = up.astype(BF16)
            act = gate * _sig(gate) * up
            acc = acc + _dot(act.astype(BF16), wout[j])
        h = x + FFN_RES_WEIGHT * acc
        if head is None:
            out_ref[...] = h
        else:
            gf = gf_ref[...]
            y, hhat, r = _rms(h, gf)
            err = y - tgt_ref[...]
            loss_ref[...] += jnp.full(loss_ref.shape, 0.5 / D_MODEL * jnp.sum(err * err), F32)
            dy = err * (1.0 / D_MODEL)
            dgf_ref[...] += jnp.sum(dy * hhat, axis=0, keepdims=True)
            out_ref[...] = _rms_bwd(dy * gf, hhat, r)

    weights = 2 * D_MODEL * 2 * D_FF + 2 * D_FF * D_MODEL
    tiles = 2 * (2 * 4 * tm * D_MODEL + 2 * tm * 2 * D_FF) + (2 * 4 * tm * D_MODEL if head else 0)
    in_specs = [_row_spec(tm, D_MODEL), _full_spec((1, D_MODEL)), ANY, ANY]
    out_shape = [jax.ShapeDtypeStruct((SEQ, D_MODEL), F32), jax.ShapeDtypeStruct((N_DEV, SEQ, FF_SHARD), BF16)]
    out_specs = [_row_spec(tm, D_MODEL), pl.BlockSpec((N_DEV, tm, FF_SHARD), lambda i: (0, i, 0))]
    args = [h_in, gain, win8, wout]
    if head is not None:
        in_specs += [_row_spec(tm, D_MODEL), _full_spec((1, D_MODEL))]
        out_shape += [jax.ShapeDtypeStruct((1, 128), F32), jax.ShapeDtypeStruct((1, D_MODEL), F32)]
        out_specs += [_full_spec((1, 128)), _full_spec((1, D_MODEL))]
        args += list(head)
    return _call(
        body, name=name, grid=(nt,), in_specs=in_specs, out_specs=out_specs, out_shape=out_shape,
        scratch_shapes=[pltpu.VMEM((N_DEV, FF_SHARD, D_MODEL), BF16), pltpu.VMEM((N_FF_GROUPS, FF_SHARD, D_MODEL), BF16),
                        pltpu.SemaphoreType.DMA((2,))],
        vmem_bytes=weights + tiles + 16 * tm * FF_SHARD * 4, args=args, ride=ride)


def _ffn_backward(dh_out, h_in, gain, gu, win8, wout, name, after=None):
    tm, nt = TOKEN_TILE, SEQ // TOKEN_TILE

    def body(dh_ref, x_ref, g_ref, gu_ref, win_hbm, wout_hbm,
             dhin_ref, dgu_ref, act_ref, xn_ref, df_ref, dg_ref, win, wout, sem):
        i = pl.program_id(0)

        @pl.when(i == 0)
        def _():
            _load_ffn_weights(win_hbm, wout_hbm, win, wout, sem)
            dg_ref[...] = jnp.zeros_like(dg_ref)

        dh = dh_ref[...]
        g = g_ref[...]
        xn, xhat, r = _rms(x_ref[...], g)
        df = (FFN_RES_WEIGHT * dh).astype(BF16)
        dxn = jnp.zeros((tm, D_MODEL), F32)
        for j in range(N_FF_GROUPS):
            gate = gu_ref[j].astype(F32)
            up = gu_ref[j + N_FF_GROUPS].astype(F32)
            dact = _dot_nt(df, wout[j])
            s = _sig(gate)
            silu = gate * s
            dgate = (dact * up * (s * (1.0 + gate * (1.0 - s)))).astype(BF16)
            dup = (dact * silu).astype(BF16)
            act_ref[j] = (silu * up).astype(BF16)
            dgu_ref[j] = dgate
            dgu_ref[j + N_FF_GROUPS] = dup
            dxn = dxn + _dot(dgate, win[j]) + _dot(dup, win[j + N_FF_GROUPS])
        dg_ref[...] += jnp.sum(dxn * xhat, axis=0, keepdims=True)
        dhin_ref[...] = dh + _rms_bwd(dxn * g, xhat, r)
        xn_ref[...] = xn.astype(BF16)
        df_ref[...] = df

    weights = 2 * D_MODEL * 2 * D_FF + 2 * D_FF * D_MODEL
    tiles = 2 * (3 * 4 * tm * D_MODEL + 2 * tm * (2 * 2 * D_FF + D_FF) + 2 * 2 * tm * D_MODEL)
    gu_spec = pl.BlockSpec((N_DEV, tm, FF_SHARD), lambda i: (0, i, 0))
    return _call(
        body, name=name, grid=(nt,),
        in_specs=[_row_spec(tm, D_MODEL), _row_spec(tm, D_MODEL), _full_spec((1, D_MODEL)), gu_spec, ANY, ANY],
        out_specs=[_row_spec(tm, D_MODEL), gu_spec, pl.BlockSpec((N_FF_GROUPS, tm, FF_SHARD), lambda i: (0, i, 0)),
                   _row_spec(tm, D_MODEL), _row_spec(tm, D_MODEL), _full_spec((1, D_MODEL))],
        out_shape=[jax.ShapeDtypeStruct((SEQ, D_MODEL), F32), jax.ShapeDtypeStruct((N_DEV, SEQ, FF_SHARD), BF16),
                   jax.ShapeDtypeStruct((N_FF_GROUPS, SEQ, FF_SHARD), BF16), jax.ShapeDtypeStruct((SEQ, D_MODEL), BF16),
                   jax.ShapeDtypeStruct((SEQ, D_MODEL), BF16), jax.ShapeDtypeStruct((1, D_MODEL), F32)],
        scratch_shapes=[pltpu.VMEM((N_DEV, FF_SHARD, D_MODEL), BF16), pltpu.VMEM((N_FF_GROUPS, FF_SHARD, D_MODEL), BF16),
                        pltpu.SemaphoreType.DMA((2,))],
        vmem_bytes=weights + tiles + 20 * tm * FF_SHARD * 4, args=[dh_out, h_in, gain, gu, win8, wout], after=after)[0]


def _to_sibling(src, dst, send_sem, recv_sem):
    x, y, c = _my_position()
    return pltpu.make_async_remote_copy(src_ref=src, dst_ref=dst, send_sem=send_sem, recv_sem=recv_sem,
                                        device_id=(x, y, 1 - c), device_id_type=MESH)


def _weight_grad(x, g, n_out, x_spec, g_spec, k_dim, n_dim, name, halves=False, tt=2048, ride=None, after=None):
    nt = SEQ // tt
    n_chips = N_DEV // 2
    rows = k_dim // 2 if halves else k_dim

    def body(x_ref, g_ref, out_ref, acc, sendbuf, recvbuf, send_sems, recv_sems):
        b, t = pl.program_id(0), pl.program_id(1)
        c = lax.axis_index("c")

        def push(q):
            return _to_sibling(sendbuf.at[q], recvbuf.at[q], send_sems.at[q], recv_sems.at[q])

        @pl.when(t == 0)
        def _():
            acc[...] = jnp.zeros_like(acc)

        acc[...] += _dot_tn(x_ref[...], g_ref[...])

        @pl.when(t == nt - 1)
        def _():
            if halves:
                for mine, other in ((0, 1), (1, 0)):
                    @pl.when(c == mine)
                    def _():
                        out_ref[b] = acc[pl.ds(mine * rows, rows), :].astype(BF16)
                        sendbuf[b] = acc[pl.ds(other * rows, rows), :].astype(BF16)
                push(b).start()
            else:
                q = b // 2

                @pl.when(b % 2 == c)
                def _():
                    out_ref[q] = acc[...].astype(BF16)

                @pl.when(b % 2 != c)
                def _():
                    sendbuf[q] = acc[...].astype(BF16)
                    push(q).start()

        @pl.when((b == n_out - 1) & (t == nt - 1))
        def _():
            for q in range(n_chips):
                push(q).wait_recv()
                out_ref[q] = (out_ref[q].astype(F32) + recvbuf[q].astype(F32)).astype(BF16)
            for q in range(n_chips):
                push(q).wait_send()

    piece = (n_chips, rows, n_dim)
    outs, ride_outs = _call(
        body, name=name, grid=(n_out, nt), in_specs=[x_spec(tt), g_spec(tt)],
        out_specs=[pl.BlockSpec(piece, lambda b, t: (0, 0, 0))],
        out_shape=[jax.ShapeDtypeStruct(piece, BF16)],
        scratch_shapes=[pltpu.VMEM((k_dim, n_dim), F32), pltpu.VMEM(piece, BF16), pltpu.VMEM(piece, BF16),
                        pltpu.SemaphoreType.DMA((n_chips,)), pltpu.SemaphoreType.DMA((n_chips,))],
        vmem_bytes=2 * 2 * tt * (k_dim + n_dim) + 8 * k_dim * n_dim + 4 * 2 * n_chips * rows * n_dim, args=[x, g], ride=ride,
        after=after)
    return outs[0], ride_outs


def _ffn_w_out_grad(act, df, tag, ride=None, after=None):
    return _weight_grad(
        act, df, N_FF_GROUPS,
        lambda tt: pl.BlockSpec((None, tt, FF_SHARD), lambda b, t: (b, t, 0)),
        lambda tt: pl.BlockSpec((tt, D_MODEL), lambda b, t: (t, 0)),
        FF_SHARD, D_MODEL, name=f"ffn{tag}_w_out_grad", halves=True, ride=ride, after=after)


def _ffn_w_in_grad(xn, dgu, tag, ride=None):
    return _weight_grad(
        dgu, xn, N_DEV,
        lambda tt: pl.BlockSpec((None, tt, FF_SHARD), lambda b, t: (b, t, 0)),
        lambda tt: pl.BlockSpec((tt, D_MODEL), lambda b, t: (t, 0)),
        FF_SHARD, D_MODEL, name=f"ffn{tag}_w_in_grad", ride=ride)


def _load_mix_weight(wmix_hbm, wmix, sem):
    copies = [pltpu.make_async_copy(wmix_hbm.at[d], wmix.at[:, pl.ds(d * MIX_SHARD, MIX_SHARD)], sem.at[d])
              for d in range(N_DEV)]
    for cp in copies:
        cp.start()
    for cp in copies:
        cp.wait()


def _load_pool_weight(pw_hbm, pw, sem):
    rows = POOL_GROUP_DIM // N_DEV
    copies = [pltpu.make_async_copy(pw_hbm.at[d], pw.at[:, pl.ds(d * rows, rows), :], sem.at[d]) for d in range(N_DEV)]
    for cp in copies:
        cp.start()
    for cp in copies:
        cp.wait()


def _rotate(x1, x2, cos, sin):
    return x1 * cos - x2 * sin, x1 * sin + x2 * cos


def _mix_proj_forward(h1, gain, wmix8, cos, sin, ride=None):
    tm, nt = WIDE_TILE, SEQ // WIDE_TILE
    k_scale = HEAD_DIM ** -0.5

    def body(h_ref, g_ref, wmix_hbm, cos_ref, sin_ref, u_ref, qkvg_ref, p_ref, gates_ref, wmix, sem):
        @pl.when(pl.program_id(0) == 0)
        def _():
            _load_mix_weight(wmix_hbm, wmix, sem)

        u = _rms(h_ref[...], g_ref[...])[0].astype(BF16)
        u_ref[...] = u
        cos_t, sin_t = cos_ref[...], sin_ref[...]
        for seg in range(N_SEG):
            pr = _dot(u, wmix[:, pl.ds(seg * D_MODEL, D_MODEL)])
            if seg < 2:
                scale = 1.0 if seg == 0 else k_scale
                for hd in range(HEADS):
                    lo = hd * HEAD_DIM
                    o1, o2 = _rotate(pr[:, lo:lo + ROT_HALF], pr[:, lo + ROT_HALF:lo + HEAD_DIM], cos_t, sin_t)
                    qkvg_ref[:, pl.ds(seg * D_MODEL + lo, ROT_HALF)] = (o1 * scale).astype(BF16)
                    qkvg_ref[:, pl.ds(seg * D_MODEL + lo + ROT_HALF, ROT_HALF)] = (o2 * scale).astype(BF16)
            elif seg < 4:
                qkvg_ref[:, pl.ds(seg * D_MODEL, D_MODEL)] = pr.astype(BF16)
            elif seg == 4:
                p_ref[...] = pr
            else:
                gates_ref[:, pl.ds((seg - 5) * D_MODEL, D_MODEL)] = pr.astype(BF16)

    est = 2 * D_MODEL * N_SEG * D_MODEL + 2 * tm * (4 * D_MODEL + 2 * D_MODEL + 2 * 4 * D_MODEL + 4 * D_MODEL + 2 * 2 * D_MODEL)
    return _call(
        body, name="mix_proj_fwd", grid=(nt,),
        in_specs=[_row_spec(tm, D_MODEL), _full_spec((1, D_MODEL)), ANY, _row_spec(tm, ROT_HALF), _row_spec(tm, ROT_HALF)],
        out_specs=[_row_spec(tm, D_MODEL), _row_spec(tm, 4 * D_MODEL), _row_spec(tm, D_MODEL), _row_spec(tm, 2 * D_MODEL)],
        out_shape=[jax.ShapeDtypeStruct((SEQ, D_MODEL), BF16), jax.ShapeDtypeStruct((SEQ, 4 * D_MODEL), BF16),
                   jax.ShapeDtypeStruct((SEQ, D_MODEL), F32), jax.ShapeDtypeStruct((SEQ, 2 * D_MODEL), BF16)],
        scratch_shapes=[pltpu.VMEM((D_MODEL, N_SEG * D_MODEL), BF16), pltpu.SemaphoreType.DMA((N_DEV,))],
        vmem_bytes=est + 8 * tm * D_MODEL * 4, args=[h1, gain, wmix8, cos, sin], ride=ride)


def _seg_block_spec(seg, reverse=False):
    nb = SEQ // RET_BLOCK
    if reverse:
        return pl.BlockSpec((RET_BLOCK, D_MODEL), lambda i, s=seg: (nb - 1 - i, s))
    return pl.BlockSpec((RET_BLOCK, D_MODEL), lambda i, s=seg: (i, s))


def _table_specs():
    return [_full_spec((HEADS, RET_BLOCK, RET_BLOCK)), _full_spec((HEADS, RET_BLOCK, 1)),
            _full_spec((HEADS, RET_BLOCK, 1)), _full_spec((HEADS, 1, 1))]


def _head_cols(h):
    return pl.ds(h * HEAD_DIM, HEAD_DIM)


def _retention_forward(qkvg, tables, ride=None):
    nb = SEQ // RET_BLOCK

    def body(q_ref, k_ref, v_ref, gr_ref, mask_ref, qdec_ref, kdec_ref, cdec_ref, ret_ref, o_ref, state):
        @pl.when(pl.program_id(0) == 0)
        def _():
            state[...] = jnp.zeros_like(state)

        for h in range(HEADS):
            cols = _head_cols(h)
            q, k, v = q_ref[:, cols], k_ref[:, cols], v_ref[:, cols]
            scores = _dot_nt(q, k) * mask_ref[h]
            inner = _dot(scores.astype(BF16), v)
            cross = _dot((q.astype(F32) * qdec_ref[h]).astype(BF16), state[h].astype(BF16))
            ret = inner + cross
            state[h] = state[h] * cdec_ref[h] + _dot_tn((k.astype(F32) * kdec_ref[h]).astype(BF16), v)
            ret_ref[:, cols] = ret
            retn = ret * lax.rsqrt(jnp.mean(ret * ret, axis=-1, keepdims=True) + NORM_EPS)
            gr = gr_ref[:, cols].astype(F32)
            o_ref[:, cols] = (retn * (gr * _sig(gr))).astype(BF16)

    return _call(
        body, name="retention_fwd", grid=(nb,),
        in_specs=[_seg_block_spec(0), _seg_block_spec(1), _seg_block_spec(2), _seg_block_spec(3)] + _table_specs(),
        out_specs=[_row_spec(RET_BLOCK, D_MODEL)] * 2,
        out_shape=[jax.ShapeDtypeStruct((SEQ, D_MODEL), F32), jax.ShapeDtypeStruct((SEQ, D_MODEL), BF16)],
        scratch_shapes=[pltpu.VMEM((HEADS, HEAD_DIM, HEAD_DIM), F32)],
        vmem_bytes=24 * RET_BLOCK * D_MODEL * 4, args=[qkvg, qkvg, qkvg, qkvg, *tables], ride=ride)


def _retention_backward_q(qkvg, dret, tables, ride=None):
    nb = SEQ // RET_BLOCK

    def body(k_ref, v_ref, do_ref, mask_ref, qdec_ref, kdec_ref, cdec_ref, dq_ref, state):
        @pl.when(pl.program_id(0) == 0)
        def _():
            state[...] = jnp.zeros_like(state)

        for h in range(HEADS):
            cols = _head_cols(h)
            k, v, do = k_ref[:, cols], v_ref[:, cols], do_ref[:, cols]
            dscores = _dot_nt(do, v) * mask_ref[h]
            dq_ref[:, cols] = _dot(dscores.astype(BF16), k) + _dot_nt(do, state[h].astype(BF16)) * qdec_ref[h]
            state[h] = state[h] * cdec_ref[h] + _dot_tn((k.astype(F32) * kdec_ref[h]).astype(BF16), v)

    return _call(
        body, name="retention_bwd_q", grid=(nb,),
        in_specs=[_seg_block_spec(1), _seg_block_spec(2), _row_spec(RET_BLOCK, D_MODEL)] + _table_specs(),
        out_specs=[_row_spec(RET_BLOCK, D_MODEL)],
        out_shape=[jax.ShapeDtypeStruct((SEQ, D_MODEL), F32)],
        scratch_shapes=[pltpu.VMEM((HEADS, HEAD_DIM, HEAD_DIM), F32)],
        vmem_bytes=24 * RET_BLOCK * D_MODEL * 4, args=[qkvg, qkvg, dret, *tables], ride=ride)


def _retention_backward_kv(qkvg, dret, tables, ride=None):
    nb = SEQ // RET_BLOCK

    def body(q_ref, k_ref, v_ref, do_ref, mask_ref, qdec_ref, kdec_ref, cdec_ref, dk_ref, dv_ref, gstate):
        @pl.when(pl.program_id(0) == 0)
        def _():
            gstate[...] = jnp.zeros_like(gstate)

        for h in range(HEADS):
            cols = _head_cols(h)
            q, k, v, do = q_ref[:, cols], k_ref[:, cols], v_ref[:, cols], do_ref[:, cols]
            mask = mask_ref[h]
            scores = (_dot_nt(q, k) * mask).astype(BF16)
            dscores = (_dot_nt(do, v) * mask).astype(BF16)
            gs = gstate[h].astype(BF16)
            dk_ref[:, cols] = _dot_tn(dscores, q) + _dot_nt(v, gs) * kdec_ref[h]
            dv_ref[:, cols] = _dot_tn(scores, do) + _dot((k.astype(F32) * kdec_ref[h]).astype(BF16), gs)
            gstate[h] = gstate[h] * cdec_ref[h] + _dot_tn((q.astype(F32) * qdec_ref[h]).astype(BF16), do)

    rev = lambda: pl.BlockSpec((RET_BLOCK, D_MODEL), lambda i: (nb - 1 - i, 0))
    return _call(
        body, name="retention_bwd_kv", grid=(nb,),
        in_specs=[_seg_block_spec(0, True), _seg_block_spec(1, True), _seg_block_spec(2, True), rev()] + _table_specs(),
        out_specs=[rev(), rev()],
        out_shape=[jax.ShapeDtypeStruct((SEQ, D_MODEL), F32)] * 2,
        scratch_shapes=[pltpu.VMEM((HEADS, HEAD_DIM, HEAD_DIM), F32)],
        vmem_bytes=32 * RET_BLOCK * D_MODEL * 4, args=[qkvg, qkvg, qkvg, dret, *tables], ride=ride)


def _pooled(p_ext, first_row):
    rows = p_ext.shape[0]
    t = first_row + lax.broadcasted_iota(jnp.int32, (rows - HALO, 1), 0)
    outs = []
    for g, w in enumerate(POOL_WINDOWS):
        e = p_ext[:, g * POOL_GROUP_DIM:(g + 1) * POOL_GROUP_DIM]
        s, span = e, 1
        while span < w:
            s = s + pltpu.roll(s, span, 0)
            span *= 2
        count = jnp.minimum(t + 1, w).astype(F32)
        outs.append(s[HALO:] / count - e[HALO:])
    return outs


def _pooled_transpose(d_ext, first_row):
    rows = d_ext.shape[0]
    t = first_row + lax.broadcasted_iota(jnp.int32, (rows, 1), 0)
    outs = []
    for g, w in enumerate(POOL_WINDOWS):
        d = d_ext[:, g * POOL_GROUP_DIM:(g + 1) * POOL_GROUP_DIM]
        e = jnp.where(t < SEQ, d / jnp.minimum(t + 1, w).astype(F32), 0.0)
        s, span = e, 1
        while span < w:
            s = s + pltpu.roll(s, rows - span, 0)
            span *= 2
        outs.append(s[:rows - HALO] - d[:rows - HALO])
    return outs


def _mix_tail_specs(tm):
    halo_blocks = tm // HALO
    return [
        _row_spec(tm, D_MODEL),
        pl.BlockSpec((HALO, D_MODEL), lambda i: (jnp.maximum(i * halo_blocks - 1, 0), 0)),
        _row_spec(tm, 2 * D_MODEL),
        _row_spec(tm, D_MODEL),
        _full_spec((2, D_MODEL)), _full_spec((1, D_MODEL)), ANY,
        _full_spec((D_MODEL, D_MODEL)), _full_spec((D_MODEL, D_MODEL)), _full_spec((D_MODEL, D_MODEL)),
    ]


def _mix_tail_compute(i, tm, p_ref, halo_ref, gates_ref, oret_ref, bias_ref, scale_ref, pw, wru_ref, wpu_ref, saved=None):
    halo = jnp.where(i > 0, halo_ref[...], 0.0)
    pooled = _pooled(jnp.concatenate([halo, p_ref[...]], axis=0), i * tm)
    pooled = [x.astype(BF16) for x in pooled]
    mixed = jnp.concatenate([_dot(pooled[g], pw[g]) for g in range(len(POOL_WINDOWS))], axis=-1)
    pool_out = (mixed * scale_ref[...]).astype(BF16)
    o_ret = oret_ref[...]
    if saved is None:
        a = _dot(o_ret, wru_ref[...])
        b = _dot(pool_out, wpu_ref[...])
    else:
        a, b = saved[0][...].astype(F32), saved[1][...].astype(F32)
    z = gates_ref[...].astype(F32)
    g0 = _sig(z[:, :D_MODEL] + bias_ref[0:1, :])
    g1 = _sig(z[:, D_MODEL:] + bias_ref[1:2, :])
    merged = (g0 * a + g1 * b).astype(BF16)
    return pooled, mixed, pool_out, o_ret, a, b, g0, g1, merged


def _mix_tail_forward(p, gates, o_ret, h1, bias, scale, pw8, wru, wpu, wo, ride=None):
    tm, nt = TOKEN_TILE, SEQ // TOKEN_TILE

    def body(p_ref, halo_ref, gates_ref, oret_ref, bias_ref, scale_ref, pw_hbm, wru_ref, wpu_ref, wo_ref, h1_ref,
             h2_ref, a_ref, b_ref, pw, sem):
        i = pl.program_id(0)

        @pl.when(i == 0)
        def _():
            _load_pool_weight(pw_hbm, pw, sem)

        out = _mix_tail_compute(i, tm, p_ref, halo_ref, gates_ref, oret_ref, bias_ref, scale_ref, pw, wru_ref, wpu_ref)
        a_ref[...] = out[4].astype(BF16)
        b_ref[...] = out[5].astype(BF16)
        h2_ref[...] = h1_ref[...] + _dot(out[-1], wo_ref[...])

    est = 3 * 2 * 2 * D_MODEL * D_MODEL + 2 * tm * D_MODEL * (4 + 4 + 2 + 4 + 4) + 16 * tm * D_MODEL * 4
    return _call(
        body, name="mix_tail_fwd", grid=(nt,),
        in_specs=_mix_tail_specs(tm) + [_row_spec(tm, D_MODEL)],
        out_specs=[_row_spec(tm, D_MODEL)] * 3,
        out_shape=[jax.ShapeDtypeStruct((SEQ, D_MODEL), F32)] + [jax.ShapeDtypeStruct((SEQ, D_MODEL), BF16)] * 2,
        scratch_shapes=[pltpu.VMEM((len(POOL_WINDOWS), POOL_GROUP_DIM, POOL_GROUP_DIM), BF16), pltpu.SemaphoreType.DMA((N_DEV,))],
        vmem_bytes=est, args=[p, p, gates, o_ret, bias, scale, pw8, wru, wpu, wo, h1], ride=ride)


def _mix_tail_backward(dh2, p, gates, o_ret, ret, qkvg, a_saved, b_saved, bias, scale, pw8, wru, wpu, wo, ride=None):
    tm, nt = TOKEN_TILE, SEQ // TOKEN_TILE
    n_groups = len(POOL_WINDOWS)
    rows_per_dev = POOL_GROUP_DIM // N_DEV

    def body(p_ref, halo_ref, gates_ref, oret_ref, bias_ref, scale_ref, pw_hbm, wru_ref, wpu_ref, wo_ref,
             dh2_ref, ret_ref, gr_ref, a_ref, b_ref,
             dret_ref, dgr_ref, dgates_ref, dpooled_ref, dwo_ref, dwru_ref, dwpu_ref, dpw_ref, dbias_ref, dscale_ref,
             pw, sem, acc_wo, acc_wru, acc_wpu, acc_pw, send_sq, recv_sq, send_pw, recv_pw, send_sems, recv_sems):
        i = pl.program_id(0)

        @pl.when(i == 0)
        def _():
            _load_pool_weight(pw_hbm, pw, sem)
            for ref in (acc_wo, acc_wru, acc_wpu, acc_pw, dbias_ref, dscale_ref):
                ref[...] = jnp.zeros_like(ref)

        pooled, mixed, pool_out, o_ret, a, b, g0, g1, merged = _mix_tail_compute(
            i, tm, p_ref, halo_ref, gates_ref, oret_ref, bias_ref, scale_ref, pw, wru_ref, wpu_ref, saved=(a_ref, b_ref))
        dh2 = dh2_ref[...].astype(BF16)
        dm = _dot_nt(dh2, wo_ref[...])
        acc_wo[...] += _dot_tn(merged, dh2)
        da = (dm * g0).astype(BF16)
        db = (dm * g1).astype(BF16)
        dz0 = dm * a * g0 * (1.0 - g0)
        dz1 = dm * b * g1 * (1.0 - g1)
        dbias_ref[0:1, :] += jnp.sum(dz0, axis=0, keepdims=True)
        dbias_ref[1:2, :] += jnp.sum(dz1, axis=0, keepdims=True)
        dgates_ref[:, pl.ds(0, D_MODEL)] = dz0.astype(BF16)
        dgates_ref[:, pl.ds(D_MODEL, D_MODEL)] = dz1.astype(BF16)
        acc_wru[...] += _dot_tn(o_ret, da)
        acc_wpu[...] += _dot_tn(pool_out, db)
        d_oret = _dot_nt(da, wru_ref[...])
        d_pool_out = _dot_nt(db, wpu_ref[...])
        dscale_ref[...] += jnp.sum(d_pool_out * mixed, axis=0, keepdims=True)
        dmixed = (d_pool_out * scale_ref[...]).astype(BF16)
        for g in range(n_groups):
            dmg = dmixed[:, g * POOL_GROUP_DIM:(g + 1) * POOL_GROUP_DIM]
            acc_pw[g] += _dot_tn(pooled[g], dmg)
            dpooled_ref[:, pl.ds(g * POOL_GROUP_DIM, POOL_GROUP_DIM)] = _dot_nt(dmg, pw[g])
        gr = gr_ref[...].astype(F32)
        s = _sig(gr)
        silu = gr * s
        for hd in range(HEADS):
            cols = slice(hd * HEAD_DIM, (hd + 1) * HEAD_DIM)
            r_h = ret_ref[:, cols]
            rr = lax.rsqrt(jnp.mean(r_h * r_h, axis=-1, keepdims=True) + NORM_EPS)
            rhat = r_h * rr
            do_h = d_oret[:, cols]
            dgr_ref[:, cols] = (do_h * rhat * (s[:, cols] * (1.0 + gr[:, cols] * (1.0 - s[:, cols])))).astype(BF16)
            dret_ref[:, cols] = _rms_bwd(do_h * silu[:, cols], rhat, rr).astype(BF16)

        @pl.when(i == nt - 1)
        def _():
            c = lax.axis_index("c")
            rows = D_MODEL // N_DEV
            squares = ((acc_wo, dwo_ref), (acc_wru, dwru_ref), (acc_wpu, dwpu_ref))
            for q in range(n_chips):
                own = pl.multiple_of((2 * q + c) * rows, rows)
                other = pl.multiple_of((2 * q + 1 - c) * rows, rows)
                for t, (acc, out) in enumerate(squares):
                    out[q] = acc[pl.ds(own, rows), :].astype(BF16)
                    send_sq[t, q] = acc[pl.ds(other, rows), :].astype(BF16)
                own_pw = pl.multiple_of((2 * q + c) * rows_per_dev, rows_per_dev)
                other_pw = pl.multiple_of((2 * q + 1 - c) * rows_per_dev, rows_per_dev)
                dpw_ref[q] = acc_pw[:, pl.ds(own_pw, rows_per_dev), :].astype(BF16)
                send_pw[q] = acc_pw[:, pl.ds(other_pw, rows_per_dev), :].astype(BF16)
            pushes = [_to_sibling(send_sq, recv_sq, send_sems.at[0], recv_sems.at[0]),
                      _to_sibling(send_pw, recv_pw, send_sems.at[1], recv_sems.at[1])]
            for cp in pushes:
                cp.start()
            for cp in pushes:
                cp.wait_recv()
            for t, (acc, out) in enumerate(squares):
                out[...] = (out[...].astype(F32) + recv_sq[t].astype(F32)).astype(BF16)
            dpw_ref[...] = (dpw_ref[...].astype(F32) + recv_pw[...].astype(F32)).astype(BF16)
            for cp in pushes:
                cp.wait_send()

    n_chips = N_DEV // 2
    sq = (n_chips, D_MODEL // N_DEV, D_MODEL)
    pw_shape = (n_chips, n_groups, rows_per_dev, POOL_GROUP_DIM)
    est = (3 * 2 * 2 * D_MODEL * D_MODEL + 3 * 4 * D_MODEL * D_MODEL + 3 * 2 * 2 * D_MODEL * D_MODEL
           + 2 * tm * D_MODEL * (4 + 4 + 2 + 4 + 4 + 2 + 2 + 2 + 4 + 4) + 24 * tm * D_MODEL * 4)
    return _call(
        body, name="mix_tail_bwd", grid=(nt,),
        in_specs=_mix_tail_specs(tm) + [_row_spec(tm, D_MODEL), _row_spec(tm, D_MODEL), _row_spec(tm, D_MODEL, 3),
                                        _row_spec(tm, D_MODEL), _row_spec(tm, D_MODEL)],
        out_specs=[_row_spec(tm, D_MODEL), _row_spec(tm, D_MODEL), _row_spec(tm, 2 * D_MODEL), _row_spec(tm, D_MODEL),
                   _full_spec(sq), _full_spec(sq), _full_spec(sq), _full_spec(pw_shape),
                   _full_spec((2, D_MODEL)), _full_spec((1, D_MODEL))],
        out_shape=[jax.ShapeDtypeStruct((SEQ, D_MODEL), BF16), jax.ShapeDtypeStruct((SEQ, D_MODEL), BF16),
                   jax.ShapeDtypeStruct((SEQ, 2 * D_MODEL), BF16), jax.ShapeDtypeStruct((SEQ, D_MODEL), F32),
                   jax.ShapeDtypeStruct(sq, BF16), jax.ShapeDtypeStruct(sq, BF16), jax.ShapeDtypeStruct(sq, BF16),
                   jax.ShapeDtypeStruct(pw_shape, BF16),
                   jax.ShapeDtypeStruct((2, D_MODEL), F32), jax.ShapeDtypeStruct((1, D_MODEL), F32)],
        scratch_shapes=[pltpu.VMEM((n_groups, POOL_GROUP_DIM, POOL_GROUP_DIM), BF16), pltpu.SemaphoreType.DMA((N_DEV,)),
                        pltpu.VMEM((D_MODEL, D_MODEL), F32), pltpu.VMEM((D_MODEL, D_MODEL), F32),
                        pltpu.VMEM((D_MODEL, D_MODEL), F32), pltpu.VMEM((n_groups, POOL_GROUP_DIM, POOL_GROUP_DIM), F32),
                        pltpu.VMEM((3,) + sq, BF16), pltpu.VMEM((3,) + sq, BF16), pltpu.VMEM(pw_shape, BF16),
                        pltpu.VMEM(pw_shape, BF16), pltpu.SemaphoreType.DMA((2,)), pltpu.SemaphoreType.DMA((2,))],
        vmem_bytes=est, args=[p, p, gates, o_ret, bias, scale, pw8, wru, wpu, wo, dh2, ret, qkvg, a_saved, b_saved], ride=ride)


def _mix_proj_backward(dq, dk, dv, dgr, dpooled, dgates, cos, sin, h1, gain, dh2, wmix8, ride=None):
    tm, nt = TOKEN_TILE, SEQ // TOKEN_TILE
    halo_blocks = tm // HALO
    last_halo = SEQ // HALO - 1
    k_scale = HEAD_DIM ** -0.5

    def body(dq_ref, dk_ref, dv_ref, dgr_ref, dpool_ref, dhalo_ref, dgates_ref, cos_ref, sin_ref, h1_ref, g_ref,
             dh2_ref, wmix_hbm, dh1_ref, dproj_ref, dg_ref, wmix, sem):
        i = pl.program_id(0)

        @pl.when(i == 0)
        def _():
            _load_mix_weight(wmix_hbm, wmix, sem)
            dg_ref[...] = jnp.zeros_like(dg_ref)

        cos_t, sin_t = cos_ref[...], sin_ref[...]
        for seg, ref, scale in ((0, dq_ref, 1.0), (1, dk_ref, k_scale)):
            for hd in range(HEADS):
                lo = hd * HEAD_DIM
                d1, d2 = ref[:, lo:lo + ROT_HALF], ref[:, lo + ROT_HALF:lo + HEAD_DIM]
                dproj_ref[:, pl.ds(seg * D_MODEL + lo, ROT_HALF)] = ((d1 * cos_t + d2 * sin_t) * scale).astype(BF16)
                dproj_ref[:, pl.ds(seg * D_MODEL + lo + ROT_HALF, ROT_HALF)] = ((d2 * cos_t - d1 * sin_t) * scale).astype(BF16)
        dproj_ref[:, pl.ds(2 * D_MODEL, D_MODEL)] = dv_ref[...].astype(BF16)
        dproj_ref[:, pl.ds(3 * D_MODEL, D_MODEL)] = dgr_ref[...]
        dp = _pooled_transpose(jnp.concatenate([dpool_ref[...], dhalo_ref[...]], axis=0), i * tm)
        for g in range(len(POOL_WINDOWS)):
            dproj_ref[:, pl.ds(4 * D_MODEL + g * POOL_GROUP_DIM, POOL_GROUP_DIM)] = dp[g].astype(BF16)
        dproj_ref[:, pl.ds(5 * D_MODEL, 2 * D_MODEL)] = dgates_ref[...]
        du = jnp.zeros((tm, D_MODEL), F32)
        for seg in range(N_SEG):
            cols = pl.ds(seg * D_MODEL, D_MODEL)
            du = du + _dot_nt(dproj_ref[:, cols], wmix[:, cols])
        g = g_ref[...]
        _, xhat, r = _rms(h1_ref[...], g)
        dg_ref[...] += jnp.sum(du * xhat, axis=0, keepdims=True)
        dh1_ref[...] = dh2_ref[...] + _rms_bwd(du * g, xhat, r)

    est = 2 * D_MODEL * N_SEG * D_MODEL + 2 * tm * D_MODEL * (3 * 4 + 2 + 4 + 4 + 4 + 4 + 4 + 14) + 12 * tm * D_MODEL * 4
    return _call(
        body, name="mix_proj_bwd", grid=(nt,),
        in_specs=[_row_spec(tm, D_MODEL), _row_spec(tm, D_MODEL), _row_spec(tm, D_MODEL), _row_spec(tm, D_MODEL),
                  _row_spec(tm, D_MODEL),
                  pl.BlockSpec((HALO, D_MODEL), lambda i: (jnp.minimum((i + 1) * halo_blocks, last_halo), 0)),
                  _row_spec(tm, 2 * D_MODEL), _row_spec(tm, ROT_HALF), _row_spec(tm, ROT_HALF),
                  _row_spec(tm, D_MODEL), _full_spec((1, D_MODEL)), _row_spec(tm, D_MODEL), ANY],
        out_specs=[_row_spec(tm, D_MODEL), _row_spec(tm, N_SEG * D_MODEL), _full_spec((1, D_MODEL))],
        out_shape=[jax.ShapeDtypeStruct((SEQ, D_MODEL), F32), jax.ShapeDtypeStruct((SEQ, N_SEG * D_MODEL), BF16),
                   jax.ShapeDtypeStruct((1, D_MODEL), F32)],
        scratch_shapes=[pltpu.VMEM((D_MODEL, N_SEG * D_MODEL), BF16), pltpu.SemaphoreType.DMA((N_DEV,))],
        vmem_bytes=est, args=[dq, dk, dv, dgr, dpooled, dpooled, dgates, cos, sin, h1, gain, dh2, wmix8], ride=ride)


def _adamw(w, parts, m, v, name, after=None):
    rows, cols = w.shape
    n_lists = len(parts)
    tr = max([t for t in range(16, 257, 16) if rows % t == 0], default=rows)
    c1 = 1.0 - ADAM_B1 ** ADAM_STEP
    c2 = 1.0 - ADAM_B2 ** ADAM_STEP

    def body(*refs):
        w_ref, m_ref, v_ref = refs[:3]
        part_refs = refs[3:3 + n_lists]
        g_out, d_out, m_out, v_out = refs[-4:]
        g = None
        for p_ref in part_refs:
            for k in range(p_ref.shape[0]):
                term = p_ref[k].astype(F32)
                g = term if g is None else g + term
        m_new = ADAM_B1 * m_ref[...] + (1.0 - ADAM_B1) * g
        v_new = ADAM_B2 * v_ref[...] + (1.0 - ADAM_B2) * (g * g)
        g_out[...] = g
        m_out[...] = m_new
        v_out[...] = v_new
        d_out[...] = -ADAM_LR * ((m_new / c1) / (jnp.sqrt(v_new / c2) + ADAM_EPS) + ADAM_WD * w_ref[...])

    spec = pl.BlockSpec((tr, cols), lambda i: (i, 0))
    out = jax.ShapeDtypeStruct((rows, cols), F32)
    part_specs = [pl.BlockSpec((p.shape[0], tr, cols), lambda i: (0, i, 0)) for p in parts]
    part_bytes = sum(p.shape[0] * p.dtype.itemsize for p in parts)
    extra = [] if after is None else [after]
    return pl.pallas_call(
        body, name=name, grid=(rows // tr,),
        in_specs=[spec, spec, spec] + part_specs + [ANY] * len(extra),
        out_specs=[spec] * 4, out_shape=[out] * 4,
        compiler_params=_params(2 * tr * cols * (7 * 4 + part_bytes) + 8 * tr * cols * 4, 1),
    )(_in_hbm(w), _in_hbm(m), _in_hbm(v), *[_in_hbm(p) for p in parts], *extra)


def _mix_w_in_grad(u, dproj, ride=None):
    return _weight_grad(
        u, dproj, N_DEV,
        lambda tt: pl.BlockSpec((tt, D_MODEL), lambda b, t: (t, 0)),
        lambda tt: pl.BlockSpec((tt, MIX_SHARD), lambda b, t: (t, b)),
        D_MODEL, MIX_SHARD, name="w_in_grad", ride=ride)


def kernel(x, norm_ffn1, ffn1_w_in, ffn1_w_out, norm_mix, w_in, gate_bias, pool_w, pool_scale, w_ret_up, w_pool_up, w_out, norm_ffn2, ffn2_w_in, ffn2_w_out, norm_final, loss_target, m_norm_ffn1, m_ffn1_w_in, m_ffn1_w_out, m_norm_mix, m_w_in, m_gate_bias, m_pool_w, m_pool_scale, m_w_ret_up, m_w_pool_up, m_w_out, m_norm_ffn2, m_ffn2_w_in, m_ffn2_w_out, m_norm_final, v_norm_ffn1, v_ffn1_w_in, v_ffn1_w_out, v_norm_mix, v_w_in, v_gate_bias, v_pool_w, v_pool_scale, v_w_ret_up, v_w_pool_up, v_w_out, v_norm_ffn2, v_ffn2_w_in, v_ffn2_w_out, v_norm_final):
    assert x.shape == (1, SEQ, D_MODEL) and ffn1_w_in.shape == (1, D_MODEL, FF_SHARD) and w_in.shape == (1, D_MODEL, MIX_SHARD)
    x2, target = x[0], loss_target[0]

    cos, sin = _rotary_tables()
    tables = _retention_tables()
    bf = lambda w: w[0].astype(BF16)
    bf_t = lambda w: jnp.swapaxes(w[0], 0, 1).astype(BF16)
    square = lambda w: w.reshape(D_MODEL, D_MODEL)

    win1, wout1, bias8 = _alone(_GatherRide([bf_t(ffn1_w_in), bf(ffn1_w_out), gate_bias[0]]), "ffn1_weights_all_gather")
    wout1 = wout1.reshape(N_FF_GROUPS, FF_SHARD, D_MODEL)
    bias = bias8.transpose(1, 0, 2).reshape(2, D_MODEL)

    (h1, gu1), (wmix8,) = _ffn_forward(x2, norm_ffn1, win1, wout1, "ffn1_fwd", ride=_GatherRide([bf(w_in)]))
    (u, qkvg, p, gates), (win2,) = _mix_proj_forward(h1, norm_mix, wmix8, cos, sin, ride=_GatherRide([bf_t(ffn2_w_in)]))
    (ret, o_ret), (pw8, wru, wpu, wo) = _retention_forward(
        qkvg, tables, ride=_GatherRide([bf(pool_w), bf(w_ret_up), bf(w_pool_up), bf(w_out)]))
    wru, wpu, wo = square(wru), square(wpu), square(wo)
    (h2, a_saved, b_saved), (wout2,) = _mix_tail_forward(p, gates, o_ret, h1, bias, pool_scale, pw8, wru, wpu, wo,
                                        ride=_GatherRide([bf(ffn2_w_out)]))
    wout2 = wout2.reshape(N_FF_GROUPS, FF_SHARD, D_MODEL)
    (dh3, gu2, loss_part, d_norm_final), _ = _ffn_forward(h2, norm_ffn2, win2, wout2, "ffn2_fwd_loss",
                                                          head=(target, norm_final.reshape(1, D_MODEL)))

    dh2, dgu2, act2, xn2, df2, d_norm_ffn2 = _ffn_backward(dh3, h2, norm_ffn2, gu2, win2, wout2, "ffn2_bwd")
    d_wout2, _ = _ffn_w_out_grad(act2, df2, 2)
    d_win2, (r_wout2,) = _ffn_w_in_grad(xn2, dgu2, 2, ride=_ScatterRide([d_wout2]))
    (dret, dgr, dgates, dpooled, d_wo, d_wru, d_wpu, d_pw, d_bias, d_scale), (r_win2,) = _mix_tail_backward(
        dh2, p, gates, o_ret, ret, qkvg, a_saved, b_saved, bias, pool_scale, pw8, wru, wpu, wo, ride=_ScatterRide([d_win2]))
    (dq,), _ = _retention_backward_q(qkvg, dret, tables)
    (dk, dv), (r_pw, r_wru, r_wpu, r_wo) = _retention_backward_kv(
        qkvg, dret, tables, ride=_ScatterRide([d_pw, d_wru, d_wpu, d_wo]))
    (dh1, dproj, d_norm_mix), _ = _mix_proj_backward(dq, dk, dv, dgr, dpooled, dgates, cos, sin, h1, norm_mix, dh2, wmix8)
    d_wmix, _ = _mix_w_in_grad(u, dproj)
    wmix_state, wmix_started = _scatter_start(d_wmix, "w_in_grad_exchange_start")
    grad_x, dgu1, act1, xn1, df1, d_norm_ffn1 = _ffn_backward(dh1, x2, norm_ffn1, gu1, win1, wout1, "ffn1_bwd", after=wmix_started)
    d_win1, _ = _ffn_w_in_grad(xn1, dgu1, 1)
    win1_state, win1_started = _scatter_start(d_win1, "ffn1_w_in_grad_exchange_start")
    d_wout1, _ = _ffn_w_out_grad(act1, df1, 1, after=win1_started)
    wout1_state, started = _scatter_start(d_wout1, "ffn1_w_out_grad_exchange_start")
    zero_row = jnp.zeros((1, D_MODEL), F32)
    small = _all_reduce_rows(jnp.concatenate(
        [d_norm_ffn1, d_norm_mix, d_scale, d_norm_ffn2, d_norm_final, d_bias, jnp.tile(loss_part, (1, D_MODEL // 128))],
        axis=0))
    loss = small[7, 0]

    results = {}

    def update(nm, w, parts, m, v, after):
        if nm in ("ffn1_w_in", "ffn2_w_in"):
            flat, back = (lambda a: jnp.swapaxes(a[0], 0, 1)), (lambda o: jnp.swapaxes(o, 0, 1)[None])
        else:
            flat, back = (lambda a: a.reshape(-1, w.shape[-1])), (lambda o: o.reshape(w.shape))
        parts = [p.reshape(p.shape[:1] + flat(w).shape) for p in parts]
        outs = _adamw(flat(w), parts, flat(m), flat(v), name=f"adamw_{nm}", after=after)
        results[nm] = [back(o) for o in outs]
        return outs[0]

    done = update("w_in", w_in, _scatter_wait(wmix_state, started, "w_in_grad_exchange_wait"), m_w_in, v_w_in, None)
    for nm, w, parts, m, v in (
            ("ffn2_w_in", ffn2_w_in, r_win2, m_ffn2_w_in, v_ffn2_w_in),
            ("ffn2_w_out", ffn2_w_out, r_wout2, m_ffn2_w_out, v_ffn2_w_out), ("w_ret_up", w_ret_up, r_wru, m_w_ret_up, v_w_ret_up),
            ("w_pool_up", w_pool_up, r_wpu, m_w_pool_up, v_w_pool_up), ("w_out", w_out, r_wo, m_w_out, v_w_out),
            ("pool_w", pool_w, r_pw, m_pool_w, v_pool_w)):
        done = update(nm, w, [parts], m, v, done)
    done = update("ffn1_w_in", ffn1_w_in, _scatter_wait(win1_state, done, "ffn1_w_in_grad_exchange_wait"),
                  m_ffn1_w_in, v_ffn1_w_in, None)
    update("ffn1_w_out", ffn1_w_out, _scatter_wait(wout1_state, done, "ffn1_w_out_grad_exchange_wait"),
           m_ffn1_w_out, v_ffn1_w_out, None)

    my_id = _linear_id(*_my_position())
    bias_cols = gate_bias.shape[-1]
    pad = lambda a: jnp.pad(a[0], ((0, 0), (0, D_MODEL - bias_cols)))
    pack = lambda a, b, c, d, e, gb: jnp.concatenate([a, b, c, d, e.reshape(1, D_MODEL), pad(gb), zero_row], axis=0)
    d_bias_mine = lax.dynamic_slice_in_dim(small[5:7], my_id * bias_cols, bias_cols, axis=1)
    g_small = jnp.concatenate([small[0:5], jnp.pad(d_bias_mine, ((0, 0), (0, D_MODEL - bias_cols))), zero_row], axis=0)
    s_outs = _adamw(pack(norm_ffn1, norm_mix, pool_scale, norm_ffn2, norm_final, gate_bias), [g_small[None]],
                    pack(m_norm_ffn1, m_norm_mix, m_pool_scale, m_norm_ffn2, m_norm_final, m_gate_bias),
                    pack(v_norm_ffn1, v_norm_mix, v_pool_scale, v_norm_ffn2, v_norm_final, v_gate_bias), name="adamw_small")
    for row, nm in enumerate(["norm_ffn1", "norm_mix", "pool_scale", "norm_ffn2"]):
        results[nm] = [o[row:row + 1] for o in s_outs]
    results["norm_final"] = [o[4] for o in s_outs]
    results["gate_bias"] = [o[5:7, :bias_cols][None] for o in s_outs]

    order = ["norm_ffn1", "ffn1_w_in", "ffn1_w_out", "norm_mix", "w_in", "gate_bias", "pool_w", "pool_scale",
             "w_ret_up", "w_pool_up", "w_out", "norm_ffn2", "ffn2_w_in", "ffn2_w_out", "norm_final"]
    return (loss, grad_x[None], *[results[nm][0] for nm in order], *[results[nm][1] for nm in order],
            *[results[nm][2] for nm in order], *[results[nm][3] for nm in order])
```

```python
import functools

import numpy as np
import jax
import jax.numpy as jnp
from jax import lax
from jax.experimental import pallas as pl
from jax.experimental.pallas import tpu as pltpu

F32 = jnp.float32
BF16 = jnp.bfloat16

N_DEV = 8
D_MODEL = 1024
SEQ = 4096
D_FF = 2816
FF_SHARD = 2 * D_FF // N_DEV
N_FF_GROUPS = N_DEV // 2
HEADS = 4
HEAD_DIM = 256
ROT_HALF = HEAD_DIM // 2
CHUNK = 64
RET_BLOCK = 256
POOL_WINDOWS = (2, 4, 8, 16)
POOL_GROUP_DIM = 256
HALO = 16
MIX_SHARD = 7 * D_MODEL // N_DEV
N_SEG = 7
ROPE_BASE = 10000.0
NORM_EPS = 1e-6
FFN_RES_WEIGHT = 0.5
ADAM_LR, ADAM_B1, ADAM_B2, ADAM_EPS, ADAM_WD, ADAM_STEP = 0.001, 0.9, 0.999, 1e-08, 0.01, 10

TOKEN_TILE = 256
WIDE_TILE = 512
VMEM_CAP_V7X = 64 * 1024 * 1024
MESH = pl.DeviceIdType.MESH
ANY = pl.BlockSpec(memory_space=pl.ANY)


def _vmem_limit(estimate_bytes):
    return int(min(estimate_bytes * 5 // 4 + (6 << 20), VMEM_CAP_V7X - (4 << 20)))


def _params(estimate_bytes, n_grid):
    return pltpu.CompilerParams(dimension_semantics=("arbitrary",) * n_grid,
                                vmem_limit_bytes=_vmem_limit(estimate_bytes))


def _dot(a, b):
    return jnp.dot(a, b, preferred_element_type=F32)


def _dot_nt(a, b):
    return lax.dot_general(a, b, (((1,), (1,)), ((), ())), preferred_element_type=F32)


def _dot_tn(a, b):
    return lax.dot_general(a, b, (((0,), (0,)), ((), ())), preferred_element_type=F32)


def _sig(x):
    return 1.0 / (1.0 + jnp.exp(-x))


def _rms(x, g):
    r = lax.rsqrt(jnp.mean(x * x, axis=-1, keepdims=True) + NORM_EPS)
    xhat = x * r
    return xhat * g, xhat, r


def _rms_bwd(dyg, xhat, r):
    return r * (dyg - xhat * jnp.mean(dyg * xhat, axis=-1, keepdims=True))


def _row_spec(tile, width, col=0):
    return pl.BlockSpec((tile, width), lambda i, c=col: (i, c))


def _full_spec(shape):
    return pl.BlockSpec(shape, lambda *_: (0,) * len(shape))


def _rotary_tables():
    inv_freq = (np.float32(ROPE_BASE) ** (-np.arange(ROT_HALF, dtype=np.float32) / np.float32(ROT_HALF))).astype(np.float32)
    ang = (np.arange(SEQ, dtype=np.float32)[:, None] * inv_freq[None, :]).astype(np.float32)
    return jnp.asarray(np.cos(ang.astype(np.float64)), F32), jnp.asarray(np.sin(ang.astype(np.float64)), F32)


def _retention_tables():
    log_gamma = np.log(1.0 - 2.0 ** (-5.0 - np.arange(HEADS, dtype=np.float64)))
    n = np.arange(RET_BLOCK)
    diff = (n[:, None] - n[None, :]).astype(np.float64)
    same = (n[:, None] // CHUNK) == (n[None, :] // CHUNK)
    earlier = (n[None, :] // CHUNK) < (n[:, None] // CHUNK)
    expo = np.where(same, np.abs(diff), diff)
    mask = np.where(same | earlier, np.exp(log_gamma[:, None, None] * expo[None]), 0.0)
    qdec = np.exp(log_gamma[:, None] * (n[None, :] + 1.0))[:, :, None]
    kdec = np.exp(log_gamma[:, None] * (RET_BLOCK - 1.0 - n[None, :]))[:, :, None]
    cdec = np.exp(log_gamma * RET_BLOCK)[:, None, None]
    return (jnp.asarray(mask, F32), jnp.asarray(qdec, F32), jnp.asarray(kdec, F32), jnp.asarray(cdec, F32))


def _my_position():
    return lax.axis_index("x"), lax.axis_index("y"), lax.axis_index("c")


def _linear_id(px, py, pc):
    return 4 * px + 2 * py + pc


def _when(pred, fn):
    if isinstance(pred, bool):
        if pred:
            fn()
    else:
        pl.when(pred)(fn)


class _GatherRide:
    def __init__(self, shards):
        self.args = list(shards)
        n = self.n = len(shards)
        self.out_shape = [pltpu.HBM((N_DEV,) + s.shape, s.dtype) for s in shards]
        self.scratch = [pltpu.SemaphoreType.DMA((n, 7)), pltpu.SemaphoreType.DMA((n, 7)), pltpu.SemaphoreType.DMA((n,))]

    def _plan(self, src, out, sems):
        send_sems, recv_sems, local_sem = sems
        x, y, c = _my_position()
        me, sibling = (x, y, c), (x, y, 1 - c)
        chips = [(1 - x, y), (x, 1 - y), (1 - x, 1 - y)]

        def copy(t, k, block, to, from_src=False):
            rows = out[t].at[_linear_id(*block)]
            return pltpu.make_async_remote_copy(
                src_ref=src[t] if from_src else rows, dst_ref=rows,
                send_sem=send_sems.at[t, k], recv_sem=recv_sems.at[t, k],
                device_id=to, device_id_type=MESH)

        def relay(t):
            return copy(t, 3, (x ^ (1 - c), y ^ c, c), (x ^ c, y ^ (1 - c), c))

        local = [pltpu.make_async_copy(src[t], out[t].at[_linear_id(*me)], local_sem.at[t]) for t in range(self.n)]
        return copy, relay, local, me, sibling, chips, c

    def begin(self, first, src, out, sems):
        copy, relay, local, me, sibling, chips, c = self._plan(src, out, sems)

        def start():
            for cp in local:
                cp.start()
            for t in range(self.n):
                copy(t, 0, me, sibling, from_src=True).start()
                for j in range(2):
                    copy(t, 1 + j, me, (*chips[j], c), from_src=True).start()

        _when(first, start)

    def finish(self, mid, late, last, src, out, sems):
        copy, relay, local, me, sibling, chips, c = self._plan(src, out, sems)

        def pass_on():
            for t in range(self.n):
                for j in range(2):
                    copy(t, 1 + j, (*chips[j], c), me).wait_recv()
                relay(t).start()
                for j in range(2):
                    copy(t, 4 + j, (*chips[j], c), sibling).start()

        def pass_on_relayed():
            for t in range(self.n):
                copy(t, 3, (*chips[2], c), me).wait_recv()
                copy(t, 6, (*chips[2], c), sibling).start()

        def drain():
            for t in range(self.n):
                copy(t, 0, sibling, me).wait_recv()
                for j in range(3):
                    copy(t, 4 + j, (*chips[j], 1 - c), me).wait_recv()
            for t in range(self.n):
                copy(t, 0, me, sibling, from_src=True).wait_send()
                for j in range(2):
                    copy(t, 1 + j, me, (*chips[j], c), from_src=True).wait_send()
                relay(t).wait_send()
                for j in range(3):
                    copy(t, 4 + j, (*chips[j], c), sibling).wait_send()
            for cp in local:
                cp.wait()

        _when(mid, pass_on)
        _when(late, pass_on_relayed)
        _when(last, drain)


class _ScatterRide:
    def __init__(self, chip_sums):
        self.args = list(chip_sums)
        n = self.n = len(chip_sums)
        self.out_shape = [pltpu.HBM(p.shape, p.dtype) for p in chip_sums]
        self.scratch = [pltpu.SemaphoreType.DMA((n, 3)), pltpu.SemaphoreType.DMA((n, 3)), pltpu.SemaphoreType.DMA((n,))]

    def _plan(self, src, out, sems):
        send_sems, recv_sems, local_sem = sems
        x, y, c = _my_position()

        def peer(k):
            return (x ^ (k >> 1), y ^ (k & 1))

        copies = [pltpu.make_async_remote_copy(
            src_ref=src[t].at[2 * peer(k)[0] + peer(k)[1]], dst_ref=out[t].at[k],
            send_sem=send_sems.at[t, k - 1], recv_sem=recv_sems.at[t, k - 1],
            device_id=(*peer(k), c), device_id_type=MESH) for t in range(self.n) for k in range(1, N_DEV // 2)]
        local = [pltpu.make_async_copy(src[t].at[2 * x + y], out[t].at[0], local_sem.at[t]) for t in range(self.n)]
        return copies, local

    def begin(self, first, src, out, sems):
        copies, local = self._plan(src, out, sems)

        def start():
            for cp in local + copies:
                cp.start()

        _when(first, start)

    def finish(self, mid, late, last, src, out, sems):
        copies, local = self._plan(src, out, sems)

        def drain():
            for cp in copies:
                cp.wait_recv()
            for cp in copies:
                cp.wait_send()
            for cp in local:
                cp.wait()

        _when(last, drain)


def _in_hbm(a):
    return pltpu.with_memory_space_constraint(a, pltpu.HBM)


def _call(body, *, name, grid, in_specs, out_specs, out_shape, scratch_shapes, vmem_bytes, args, ride=None, after=None):
    n_in, n_out, n_s = len(in_specs), len(out_specs), len(scratch_shapes)
    params = _params(vmem_bytes, len(grid))
    args = [_in_hbm(a) for a in args]
    out_shape = [pltpu.HBM(s.shape, s.dtype) for s in out_shape]
    if ride is None:
        if after is not None:
            def ordered_body(*refs):
                body(*refs[:n_in], *refs[n_in + 1:])
            outs = pl.pallas_call(ordered_body, name=name, grid=grid, in_specs=list(in_specs) + [ANY], out_specs=out_specs,
                                  out_shape=out_shape, scratch_shapes=scratch_shapes, compiler_params=params)(*args, after)
            return list(outs), []
        outs = pl.pallas_call(body, name=name, grid=grid, in_specs=in_specs, out_specs=out_specs, out_shape=out_shape,
                              scratch_shapes=scratch_shapes, compiler_params=params)(*args)
        return list(outs), []
    total = int(np.prod(grid))

    def riding_body(*refs):
        a = n_in
        b = a + ride.n
        c = b + n_out
        d = c + ride.n
        e = d + n_s
        step = pl.program_id(0)
        for axis in range(1, len(grid)):
            step = step * grid[axis] + pl.program_id(axis)
        ride.begin(step == 0, refs[a:b], refs[c:d], refs[e:])
        body(*refs[:a], *refs[b:c], *refs[d:e])
        ride.finish(step == total // 2, step == (3 * total) // 4, step == total - 1, refs[a:b], refs[c:d], refs[e:])

    outs = pl.pallas_call(
        riding_body, name=name, grid=grid, in_specs=list(in_specs) + [ANY] * ride.n,
        out_specs=list(out_specs) + [ANY] * ride.n, out_shape=list(out_shape) + ride.out_shape,
        scratch_shapes=list(scratch_shapes) + ride.scratch, compiler_params=params)(*args, *[_in_hbm(a) for a in ride.args])
    return list(outs[:n_out]), list(outs[n_out:])


def _alone(ride, name):
    def body(*refs):
        src, out, sems = refs[:ride.n], refs[ride.n:2 * ride.n], refs[2 * ride.n:]
        ride.begin(True, src, out, sems)
        ride.finish(True, True, True, src, out, sems)

    return list(pl.pallas_call(body, name=name, out_shape=ride.out_shape, in_specs=[ANY] * ride.n,
                               out_specs=[ANY] * ride.n, scratch_shapes=ride.scratch)(*[_in_hbm(a) for a in ride.args]))


def _scatter_copies(src, land, send_sems, recv_sems):
    x, y, c = _my_position()
    copies = []
    for k in range(1, N_DEV // 2):
        px, py = x ^ (k >> 1), y ^ (k & 1)
        copies.append(pltpu.make_async_remote_copy(
            src_ref=src.at[2 * px + py], dst_ref=land.at[k - 1], send_sem=send_sems.at[k - 1], recv_sem=recv_sems.at[k - 1],
            device_id=(px, py, c), device_id_type=MESH))
    return copies


def _scatter_start(chip_sums, name):
    n_peers = N_DEV // 2 - 1
    land_shape = (n_peers,) + chip_sums.shape[1:]
    hbm = pl.BlockSpec(memory_space=pltpu.HBM)
    sem = pl.BlockSpec(memory_space=pltpu.SEMAPHORE)

    def body(src_ref, land_ref, send_sems, recv_sems, src_thru, land_thru, token):
        for cp in _scatter_copies(src_ref, land_ref, send_sems, recv_sems):
            cp.start()
        token[...] = jnp.zeros_like(token)

    send_sems, recv_sems, src_thru, land_thru, token = pl.pallas_call(
        body, name=name,
        out_shape=(pltpu.SemaphoreType.DMA((n_peers,)), pltpu.SemaphoreType.DMA((n_peers,)),
                   pltpu.HBM(chip_sums.shape, chip_sums.dtype), pltpu.HBM(land_shape, chip_sums.dtype),
                   jax.ShapeDtypeStruct((8, 128), F32)),
        in_specs=(hbm, hbm), out_specs=(sem, sem, hbm, hbm, pl.BlockSpec(memory_space=pltpu.VMEM)),
        input_output_aliases={0: 2, 1: 3},
        compiler_params=pltpu.CompilerParams(has_side_effects=pltpu.SideEffectType.DATAFLOW_SIDE_EFFECTING),
    )(_in_hbm(chip_sums), _in_hbm(lax.empty(land_shape, chip_sums.dtype)))
    return (send_sems, recv_sems, src_thru, land_thru), token


def _scatter_wait(state, after, name):
    send_sems, recv_sems, src_thru, land_thru = state
    hbm = pl.BlockSpec(memory_space=pltpu.HBM)
    sem = pl.BlockSpec(memory_space=pltpu.SEMAPHORE)

    def body(src_ref, land_ref, send_sems, recv_sems, after_ref, src_out, land_out):
        for cp in _scatter_copies(src_ref, land_ref, send_sems, recv_sems):
            cp.wait_send()
            cp.wait_recv()

    src_done, land_done = pl.pallas_call(
        body, name=name,
        out_shape=(pltpu.HBM(src_thru.shape, src_thru.dtype), pltpu.HBM(land_thru.shape, land_thru.dtype)),
        in_specs=(hbm, hbm, sem, sem, ANY), out_specs=(hbm, hbm), input_output_aliases={0: 0, 1: 1},
        compiler_params=pltpu.CompilerParams(has_side_effects=pltpu.SideEffectType.DATAFLOW_SIDE_EFFECTING),
    )(src_thru, land_thru, send_sems, recv_sems, after)
    x, y, _ = _my_position()
    return lax.dynamic_slice_in_dim(src_done, 2 * x + y, 1, axis=0), land_done


def _all_reduce_rows(block):
    rows, width = block.shape

    def body(x_ref, sum_ref, gathered, send_sems, recv_sems, local_sem):
        x, y, c = _my_position()
        me, sibling = (x, y, c), (x, y, 1 - c)
        chips = [(1 - x, y), (x, 1 - y), (1 - x, 1 - y)]

        def slot(px, py, pc):
            return gathered.at[_linear_id(px, py, pc)]

        def copy(k, block_of, to, from_src=False):
            return pltpu.make_async_remote_copy(
                src_ref=x_ref if from_src else slot(*block_of), dst_ref=slot(*block_of),
                send_sem=send_sems.at[k], recv_sem=recv_sems.at[k], device_id=to, device_id_type=MESH)

        mine = pltpu.make_async_copy(x_ref, slot(*me), local_sem)
        mine.start()
        first = [copy(0, me, sibling, from_src=True)]
        first += [copy(1 + j, me, (*chip, c), from_src=True) for j, chip in enumerate(chips)]
        for cp in first:
            cp.start()
        passed = [copy(4 + j, (*chip, c), sibling) for j, chip in enumerate(chips)]
        for j, chip in enumerate(chips):
            copy(1 + j, (*chip, c), me).wait_recv()
            passed[j].start()
        copy(0, sibling, me).wait_recv()
        for j, chip in enumerate(chips):
            copy(4 + j, (*chip, 1 - c), me).wait_recv()
        for cp in first + passed:
            cp.wait_send()
        mine.wait()
        total = gathered[0]
        for d in range(1, N_DEV):
            total = total + gathered[d]
        sum_ref[...] = total

    return pl.pallas_call(
        body, name="small_grads_all_reduce",
        out_shape=jax.ShapeDtypeStruct((rows, width), F32),
        in_specs=[pl.BlockSpec(memory_space=pltpu.VMEM)],
        out_specs=pl.BlockSpec(memory_space=pltpu.VMEM),
        scratch_shapes=[pltpu.VMEM((N_DEV, rows, width), F32),
                        pltpu.SemaphoreType.DMA((7,)), pltpu.SemaphoreType.DMA((7,)), pltpu.SemaphoreType.DMA],
    )(block)


def _load_ffn_weights(win_hbm, wout_hbm, win, wout, sem):
    a = pltpu.make_async_copy(win_hbm, win, sem.at[0])
    b = pltpu.make_async_copy(wout_hbm, wout, sem.at[1])
    a.start()
    b.start()
    a.wait()
    b.wait()


def _ffn_forward(h_in, gain, win8, wout, name, head=None, ride=None):
    tm, nt = WIDE_TILE, SEQ // WIDE_TILE

    def body(*refs):
        if head is None:
            x_ref, g_ref, win_hbm, wout_hbm, out_ref, gu_ref, win, wout, sem = refs
        else:
            x_ref, g_ref, win_hbm, wout_hbm, tgt_ref, gf_ref, out_ref, gu_ref, loss_ref, dgf_ref, win, wout, sem = refs
        i = pl.program_id(0)

        @pl.when(i == 0)
        def _():
            _load_ffn_weights(win_hbm, wout_hbm, win, wout, sem)
            if head is not None:
                loss_ref[...] = jnp.zeros_like(loss_ref)
                dgf_ref[...] = jnp.zeros_like(dgf_ref)

        x = x_ref[...]
        xn, _, _ = _rms(x, g_ref[...])
        xb = xn.astype(BF16)
        acc = jnp.zeros((tm, D_MODEL), F32)
        for j in range(N_FF_GROUPS):
            gate = _dot_nt(xb, win[j])
            up = _dot_nt(xb, win[j + N_FF_GROUPS])
            gu_ref[j] = gate.astype(BF16)
            gu_ref[j + N_FF_GROUPS] = up.astype(BF16)
            act = gate * _sig(gate) * up
            acc = acc + _dot(act.astype(BF16), wout[j])
        h = x + FFN_RES_WEIGHT * acc
        if head is None:
            out_ref[...] = h
        else:
            gf = gf_ref[...]
            y, hhat, r = _rms(h, gf)
            err = y - tgt_ref[...]
            loss_ref[...] += jnp.full(loss_ref.shape, 0.5 / D_MODEL * jnp.sum(err * err), F32)
            dy = err * (1.0 / D_MODEL)
            dgf_ref[...] += jnp.sum(dy * hhat, axis=0, keepdims=True)
            out_ref[...] = _rms_bwd(dy * gf, hhat, r)

    weights = 2 * D_MODEL * 2 * D_FF + 2 * D_FF * D_MODEL
    tiles = 2 * (2 * 4 * tm * D_MODEL + 2 * tm * 2 * D_FF) + (2 * 4 * tm * D_MODEL if head else 0)
    in_specs = [_row_spec(tm, D_MODEL), _full_spec((1, D_MODEL)), ANY, ANY]
    out_shape = [jax.ShapeDtypeStruct((SEQ, D_MODEL), F32), jax.ShapeDtypeStruct((N_DEV, SEQ, FF_SHARD), BF16)]
    out_specs = [_row_spec(tm, D_MODEL), pl.BlockSpec((N_DEV, tm, FF_SHARD), lambda i: (0, i, 0))]
    args = [h_in, gain, win8, wout]
    if head is not None:
        in_specs += [_row_spec(tm, D_MODEL), _full_spec((1, D_MODEL))]
        out_shape += [jax.ShapeDtypeStruct((1, 128), F32), jax.ShapeDtypeStruct((1, D_MODEL), F32)]
        out_specs += [_full_spec((1, 128)), _full_spec((1, D_MODEL))]
        args += list(head)
    return _call(
        body, name=name, grid=(nt,), in_specs=in_specs, out_specs=out_specs, out_shape=out_shape,
        scratch_shapes=[pltpu.VMEM((N_DEV, FF_SHARD, D_MODEL), BF16), pltpu.VMEM((N_FF_GROUPS, FF_SHARD, D_MODEL), BF16),
                        pltpu.SemaphoreType.DMA((2,))],
        vmem_bytes=weights + tiles + 16 * tm * FF_SHARD * 4, args=args, ride=ride)


def _ffn_backward(dh_out, h_in, gain, gu, win8, wout, name, after=None):
    tm, nt = TOKEN_TILE, SEQ // TOKEN_TILE

    def body(dh_ref, x_ref, g_ref, gu_ref, win_hbm, wout_hbm,
             dhin_ref, dgu_ref, act_ref, xn_ref, df_ref, dg_ref, win, wout, sem):
        i = pl.program_id(0)

        @pl.when(i == 0)
        def _():
            _load_ffn_weights(win_hbm, wout_hbm, win, wout, sem)
            dg_ref[...] = jnp.zeros_like(dg_ref)

        dh = dh_ref[...]
        g = g_ref[...]
        xn, xhat, r = _rms(x_ref[...], g)
        df = (FFN_RES_WEIGHT * dh).astype(BF16)
        dxn = jnp.zeros((tm, D_MODEL), F32)
        for j in range(N_FF_GROUPS):
            gate = gu_ref[j].astype(F32)
            up = gu_ref[j + N_FF_GROUPS].astype(F32)
            dact = _dot_nt(df, wout[j])
            s = _sig(gate)
            silu = gate * s
            dgate = (dact * up * (s * (1.0 + gate * (1.0 - s)))).astype(BF16)
            dup = (dact * silu).astype(BF16)
            act_ref[j] = (silu * up).astype(BF16)
            dgu_ref[j] = dgate
            dgu_ref[j + N_FF_GROUPS] = dup
            dxn = dxn + _dot(dgate, win[j]) + _dot(dup, win[j + N_FF_GROUPS])
        dg_ref[...] += jnp.sum(dxn * xhat, axis=0, keepdims=True)
        dhin_ref[...] = dh + _rms_bwd(dxn * g, xhat, r)
        xn_ref[...] = xn.astype(BF16)
        df_ref[...] = df

    weights = 2 * D_MODEL * 2 * D_FF + 2 * D_FF * D_MODEL
    tiles = 2 * (3 * 4 * tm * D_MODEL + 2 * tm * (2 * 2 * D_FF + D_FF) + 2 * 2 * tm * D_MODEL)
    gu_spec = pl.BlockSpec((N_DEV, tm, FF_SHARD), lambda i: (0, i, 0))
    return _call(
        body, name=name, grid=(nt,),
        in_specs=[_row_spec(tm, D_MODEL), _row_spec(tm, D_MODEL), _full_spec((1, D_MODEL)), gu_spec, ANY, ANY],
        out_specs=[_row_spec(tm, D_MODEL), gu_spec, pl.BlockSpec((N_FF_GROUPS, tm, FF_SHARD), lambda i: (0, i, 0)),
                   _row_spec(tm, D_MODEL), _row_spec(tm, D_MODEL), _full_spec((1, D_MODEL))],
        out_shape=[jax.ShapeDtypeStruct((SEQ, D_MODEL), F32), jax.ShapeDtypeStruct((N_DEV, SEQ, FF_SHARD), BF16),
                   jax.ShapeDtypeStruct((N_FF_GROUPS, SEQ, FF_SHARD), BF16), jax.ShapeDtypeStruct((SEQ, D_MODEL), BF16),
                   jax.ShapeDtypeStruct((SEQ, D_MODEL), BF16), jax.ShapeDtypeStruct((1, D_MODEL), F32)],
        scratch_shapes=[pltpu.VMEM((N_DEV, FF_SHARD, D_MODEL), BF16), pltpu.VMEM((N_FF_GROUPS, FF_SHARD, D_MODEL), BF16),
                        pltpu.SemaphoreType.DMA((2,))],
        vmem_bytes=weights + tiles + 20 * tm * FF_SHARD * 4, args=[dh_out, h_in, gain, gu, win8, wout], after=after)[0]


def _to_sibling(src, dst, send_sem, recv_sem):
    x, y, c = _my_position()
    return pltpu.make_async_remote_copy(src_ref=src, dst_ref=dst, send_sem=send_sem, recv_sem=recv_sem,
                                        device_id=(x, y, 1 - c), device_id_type=MESH)


def _weight_grad(x, g, n_out, x_spec, g_spec, k_dim, n_dim, name, halves=False, tt=2048, ride=None, after=None):
    nt = SEQ // tt
    n_chips = N_DEV // 2
    rows = k_dim // 2 if halves else k_dim

    def body(x_ref, g_ref, out_ref, acc, sendbuf, recvbuf, send_sems, recv_sems):
        b, t = pl.program_id(0), pl.program_id(1)
        c = lax.axis_index("c")

        def push(q):
            return _to_sibling(sendbuf.at[q], recvbuf.at[q], send_sems.at[q], recv_sems.at[q])

        @pl.when(t == 0)
        def _():
            acc[...] = jnp.zeros_like(acc)

        acc[...] += _dot_tn(x_ref[...], g_ref[...])

        @pl.when(t == nt - 1)
        def _():
            if halves:
                for mine, other in ((0, 1), (1, 0)):
                    @pl.when(c == mine)
                    def _():
                        out_ref[b] = acc[pl.ds(mine * rows, rows), :].astype(BF16)
                        sendbuf[b] = acc[pl.ds(other * rows, rows), :].astype(BF16)
                push(b).start()
            else:
                q = b // 2

                @pl.when(b % 2 == c)
                def _():
                    out_ref[q] = acc[...].astype(BF16)

                @pl.when(b % 2 != c)
                def _():
                    sendbuf[q] = acc[...].astype(BF16)
                    push(q).start()

        @pl.when((b == n_out - 1) & (t == nt - 1))
        def _():
            for q in range(n_chips):
                push(q).wait_recv()
                out_ref[q] = (out_ref[q].astype(F32) + recvbuf[q].astype(F32)).astype(BF16)
            for q in range(n_chips):
                push(q).wait_send()

    piece = (n_chips, rows, n_dim)
    outs, ride_outs = _call(
        body, name=name, grid=(n_out, nt), in_specs=[x_spec(tt), g_spec(tt)],
        out_specs=[pl.BlockSpec(piece, lambda b, t: (0, 0, 0))],
        out_shape=[jax.ShapeDtypeStruct(piece, BF16)],
        scratch_shapes=[pltpu.VMEM((k_dim, n_dim), F32), pltpu.VMEM(piece, BF16), pltpu.VMEM(piece, BF16),
                        pltpu.SemaphoreType.DMA((n_chips,)), pltpu.SemaphoreType.DMA((n_chips,))],
        vmem_bytes=2 * 2 * tt * (k_dim + n_dim) + 8 * k_dim * n_dim + 4 * 2 * n_chips * rows * n_dim, args=[x, g], ride=ride,
        after=after)
    return outs[0], ride_outs


def _ffn_w_out_grad(act, df, tag, ride=None, after=None):
    return _weight_grad(
        act, df, N_FF_GROUPS,
        lambda tt: pl.BlockSpec((None, tt, FF_SHARD), lambda b, t: (b, t, 0)),
        lambda tt: pl.BlockSpec((tt, D_MODEL), lambda b, t: (t, 0)),
        FF_SHARD, D_MODEL, name=f"ffn{tag}_w_out_grad", halves=True, ride=ride, after=after)


def _ffn_w_in_grad(xn, dgu, tag, ride=None):
    return _weight_grad(
        dgu, xn, N_DEV,
        lambda tt: pl.BlockSpec((None, tt, FF_SHARD), lambda b, t: (b, t, 0)),
        lambda tt: pl.BlockSpec((tt, D_MODEL), lambda b, t: (t, 0)),
        FF_SHARD, D_MODEL, name=f"ffn{tag}_w_in_grad", ride=ride)


def _load_mix_weight(wmix_hbm, wmix, sem):
    copies = [pltpu.make_async_copy(wmix_hbm.at[d], wmix.at[:, pl.ds(d * MIX_SHARD, MIX_SHARD)], sem.at[d])
              for d in range(N_DEV)]
    for cp in copies:
        cp.start()
    for cp in copies:
        cp.wait()


def _load_pool_weight(pw_hbm, pw, sem):
    rows = POOL_GROUP_DIM // N_DEV
    copies = [pltpu.make_async_copy(pw_hbm.at[d], pw.at[:, pl.ds(d * rows, rows), :], sem.at[d]) for d in range(N_DEV)]
    for cp in copies:
        cp.start()
    for cp in copies:
        cp.wait()


def _rotate(x1, x2, cos, sin):
    return x1 * cos - x2 * sin, x1 * sin + x2 * cos


def _mix_proj_forward(h1, gain, wmix8, cos, sin, ride=None):
    tm, nt = WIDE_TILE, SEQ // WIDE_TILE
    k_scale = HEAD_DIM ** -0.5

    def body(h_ref, g_ref, wmix_hbm, cos_ref, sin_ref, u_ref, qkvg_ref, p_ref, gates_ref, wmix, sem):
        @pl.when(pl.program_id(0) == 0)
        def _():
            _load_mix_weight(wmix_hbm, wmix, sem)

        u = _rms(h_ref[...], g_ref[...])[0].astype(BF16)
        u_ref[...] = u
        cos_t, sin_t = cos_ref[...], sin_ref[...]
        for seg in range(N_SEG):
            pr = _dot(u, wmix[:, pl.ds(seg * D_MODEL, D_MODEL)])
            if seg < 2:
                scale = 1.0 if seg == 0 else k_scale
                for hd in range(HEADS):
                    lo = hd * HEAD_DIM
                    o1, o2 = _rotate(pr[:, lo:lo + ROT_HALF], pr[:, lo + ROT_HALF:lo + HEAD_DIM], cos_t, sin_t)
                    qkvg_ref[:, pl.ds(seg * D_MODEL + lo, ROT_HALF)] = (o1 * scale).astype(BF16)
                    qkvg_ref[:, pl.ds(seg * D_MODEL + lo + ROT_HALF, ROT_HALF)] = (o2 * scale).astype(BF16)
            elif seg < 4:
                qkvg_ref[:, pl.ds(seg * D_MODEL, D_MODEL)] = pr.astype(BF16)
            elif seg == 4:
                p_ref[...] = pr
            else:
                gates_ref[:, pl.ds((seg - 5) * D_MODEL, D_MODEL)] = pr.astype(BF16)

    est = 2 * D_MODEL * N_SEG * D_MODEL + 2 * tm * (4 * D_MODEL + 2 * D_MODEL + 2 * 4 * D_MODEL + 4 * D_MODEL + 2 * 2 * D_MODEL)
    return _call(
        body, name="mix_proj_fwd", grid=(nt,),
        in_specs=[_row_spec(tm, D_MODEL), _full_spec((1, D_MODEL)), ANY, _row_spec(tm, ROT_HALF), _row_spec(tm, ROT_HALF)],
        out_specs=[_row_spec(tm, D_MODEL), _row_spec(tm, 4 * D_MODEL), _row_spec(tm, D_MODEL), _row_spec(tm, 2 * D_MODEL)],
        out_shape=[jax.ShapeDtypeStruct((SEQ, D_MODEL), BF16), jax.ShapeDtypeStruct((SEQ, 4 * D_MODEL), BF16),
                   jax.ShapeDtypeStruct((SEQ, D_MODEL), F32), jax.ShapeDtypeStruct((SEQ, 2 * D_MODEL), BF16)],
        scratch_shapes=[pltpu.VMEM((D_MODEL, N_SEG * D_MODEL), BF16), pltpu.SemaphoreType.DMA((N_DEV,))],
        vmem_bytes=est + 8 * tm * D_MODEL * 4, args=[h1, gain, wmix8, cos, sin], ride=ride)


def _seg_block_spec(seg, reverse=False):
    nb = SEQ // RET_BLOCK
    if reverse:
        return pl.BlockSpec((RET_BLOCK, D_MODEL), lambda i, s=seg: (nb - 1 - i, s))
    return pl.BlockSpec((RET_BLOCK, D_MODEL), lambda i, s=seg: (i, s))


def _table_specs():
    return [_full_spec((HEADS, RET_BLOCK, RET_BLOCK)), _full_spec((HEADS, RET_BLOCK, 1)),
            _full_spec((HEADS, RET_BLOCK, 1)), _full_spec((HEADS, 1, 1))]


def _head_cols(h):
    return pl.ds(h * HEAD_DIM, HEAD_DIM)


def _retention_forward(qkvg, tables, ride=None):
    nb = SEQ // RET_BLOCK

    def body(q_ref, k_ref, v_ref, gr_ref, mask_ref, qdec_ref, kdec_ref, cdec_ref, ret_ref, o_ref, state):
        @pl.when(pl.program_id(0) == 0)
        def _():
            state[...] = jnp.zeros_like(state)

        for h in range(HEADS):
            cols = _head_cols(h)
            q, k, v = q_ref[:, cols], k_ref[:, cols], v_ref[:, cols]
            scores = _dot_nt(q, k) * mask_ref[h]
            inner = _dot(scores.astype(BF16), v)
            cross = _dot((q.astype(F32) * qdec_ref[h]).astype(BF16), state[h].astype(BF16))
            ret = inner + cross
            state[h] = state[h] * cdec_ref[h] + _dot_tn((k.astype(F32) * kdec_ref[h]).astype(BF16), v)
            ret_ref[:, cols] = ret
            retn = ret * lax.rsqrt(jnp.mean(ret * ret, axis=-1, keepdims=True) + NORM_EPS)
            gr = gr_ref[:, cols].astype(F32)
            o_ref[:, cols] = (retn * (gr * _sig(gr))).astype(BF16)

    return _call(
        body, name="retention_fwd", grid=(nb,),
        in_specs=[_seg_block_spec(0), _seg_block_spec(1), _seg_block_spec(2), _seg_block_spec(3)] + _table_specs(),
        out_specs=[_row_spec(RET_BLOCK, D_MODEL)] * 2,
        out_shape=[jax.ShapeDtypeStruct((SEQ, D_MODEL), F32), jax.ShapeDtypeStruct((SEQ, D_MODEL), BF16)],
        scratch_shapes=[pltpu.VMEM((HEADS, HEAD_DIM, HEAD_DIM), F32)],
        vmem_bytes=24 * RET_BLOCK * D_MODEL * 4, args=[qkvg, qkvg, qkvg, qkvg, *tables], ride=ride)


def _retention_backward_q(qkvg, dret, tables, ride=None):
    nb = SEQ // RET_BLOCK

    def body(k_ref, v_ref, do_ref, mask_ref, qdec_ref, kdec_ref, cdec_ref, dq_ref, state):
        @pl.when(pl.program_id(0) == 0)
        def _():
            state[...] = jnp.zeros_like(state)

        for h in range(HEADS):
            cols = _head_cols(h)
            k, v, do = k_ref[:, cols], v_ref[:, cols], do_ref[:, cols]
            dscores = _dot_nt(do, v) * mask_ref[h]
            dq_ref[:, cols] = _dot(dscores.astype(BF16), k) + _dot_nt(do, state[h].astype(BF16)) * qdec_ref[h]
            state[h] = state[h] * cdec_ref[h] + _dot_tn((k.astype(F32) * kdec_ref[h]).astype(BF16), v)

    return _call(
        body, name="retention_bwd_q", grid=(nb,),
        in_specs=[_seg_block_spec(1), _seg_block_spec(2), _row_spec(RET_BLOCK, D_MODEL)] + _table_specs(),
        out_specs=[_row_spec(RET_BLOCK, D_MODEL)],
        out_shape=[jax.ShapeDtypeStruct((SEQ, D_MODEL), F32)],
        scratch_shapes=[pltpu.VMEM((HEADS, HEAD_DIM, HEAD_DIM), F32)],
        vmem_bytes=24 * RET_BLOCK * D_MODEL * 4, args=[qkvg, qkvg, dret, *tables], ride=ride)


def _retention_backward_kv(qkvg, dret, tables, ride=None):
    nb = SEQ // RET_BLOCK

    def body(q_ref, k_ref, v_ref, do_ref, mask_ref, qdec_ref, kdec_ref, cdec_ref, dk_ref, dv_ref, gstate):
        @pl.when(pl.program_id(0) == 0)
        def _():
            gstate[...] = jnp.zeros_like(gstate)

        for h in range(HEADS):
            cols = _head_cols(h)
            q, k, v, do = q_ref[:, cols], k_ref[:, cols], v_ref[:, cols], do_ref[:, cols]
            mask = mask_ref[h]
            scores = (_dot_nt(q, k) * mask).astype(BF16)
            dscores = (_dot_nt(do, v) * mask).astype(BF16)
            gs = gstate[h].astype(BF16)
            dk_ref[:, cols] = _dot_tn(dscores, q) + _dot_nt(v, gs) * kdec_ref[h]
            dv_ref[:, cols] = _dot_tn(scores, do) + _dot((k.astype(F32) * kdec_ref[h]).astype(BF16), gs)
            gstate[h] = gstate[h] * cdec_ref[h] + _dot_tn((q.astype(F32) * qdec_ref[h]).astype(BF16), do)

    rev = lambda: pl.BlockSpec((RET_BLOCK, D_MODEL), lambda i: (nb - 1 - i, 0))
    return _call(
        body, name="retention_bwd_kv", grid=(nb,),
        in_specs=[_seg_block_spec(0, True), _seg_block_spec(1, True), _seg_block_spec(2, True), rev()] + _table_specs(),
        out_specs=[rev(), rev()],
        out_shape=[jax.ShapeDtypeStruct((SEQ, D_MODEL), F32)] * 2,
        scratch_shapes=[pltpu.VMEM((HEADS, HEAD_DIM, HEAD_DIM), F32)],
        vmem_bytes=32 * RET_BLOCK * D_MODEL * 4, args=[qkvg, qkvg, qkvg, dret, *tables], ride=ride)


def _pooled(p_ext, first_row):
    rows = p_ext.shape[0]
    t = first_row + lax.broadcasted_iota(jnp.int32, (rows - HALO, 1), 0)
    outs = []
    for g, w in enumerate(POOL_WINDOWS):
        e = p_ext[:, g * POOL_GROUP_DIM:(g + 1) * POOL_GROUP_DIM]
        s, span = e, 1
        while span < w:
            s = s + pltpu.roll(s, span, 0)
            span *= 2
        count = jnp.minimum(t + 1, w).astype(F32)
        outs.append(s[HALO:] / count - e[HALO:])
    return outs


def _pooled_transpose(d_ext, first_row):
    rows = d_ext.shape[0]
    t = first_row + lax.broadcasted_iota(jnp.int32, (rows, 1), 0)
    outs = []
    for g, w in enumerate(POOL_WINDOWS):
        d = d_ext[:, g * POOL_GROUP_DIM:(g + 1) * POOL_GROUP_DIM]
        e = jnp.where(t < SEQ, d / jnp.minimum(t + 1, w).astype(F32), 0.0)
        s, span = e, 1
        while span < w:
            s = s + pltpu.roll(s, rows - span, 0)
            span *= 2
        outs.append(s[:rows - HALO] - d[:rows - HALO])
    return outs


def _mix_tail_specs(tm):
    halo_blocks = tm // HALO
    return [
        _row_spec(tm, D_MODEL),
        pl.BlockSpec((HALO, D_MODEL), lambda i: (jnp.maximum(i * halo_blocks - 1, 0), 0)),
        _row_spec(tm, 2 * D_MODEL),
        _row_spec(tm, D_MODEL),
        _full_spec((2, D_MODEL)), _full_spec((1, D_MODEL)), ANY,
        _full_spec((D_MODEL, D_MODEL)), _full_spec((D_MODEL, D_MODEL)), _full_spec((D_MODEL, D_MODEL)),
    ]


def _mix_tail_compute(i, tm, p_ref, halo_ref, gates_ref, oret_ref, bias_ref, scale_ref, pw, wru_ref, wpu_ref, saved=None):
    halo = jnp.where(i > 0, halo_ref[...], 0.0)
    pooled = _pooled(jnp.concatenate([halo, p_ref[...]], axis=0), i * tm)
    pooled = [x.astype(BF16) for x in pooled]
    mixed = jnp.concatenate([_dot(pooled[g], pw[g]) for g in range(len(POOL_WINDOWS))], axis=-1)
    pool_out = (mixed * scale_ref[...]).astype(BF16)
    o_ret = oret_ref[...]
    if saved is None:
        a = _dot(o_ret, wru_ref[...])
        b = _dot(pool_out, wpu_ref[...])
    else:
        a, b = saved[0][...].astype(F32), saved[1][...].astype(F32)
    z = gates_ref[...].astype(F32)
    g0 = _sig(z[:, :D_MODEL] + bias_ref[0:1, :])
    g1 = _sig(z[:, D_MODEL:] + bias_ref[1:2, :])
    merged = (g0 * a + g1 * b).astype(BF16)
    return pooled, mixed, pool_out, o_ret, a, b, g0, g1, merged


def _mix_tail_forward(p, gates, o_ret, h1, bias, scale, pw8, wru, wpu, wo, ride=None):
    tm, nt = TOKEN_TILE, SEQ // TOKEN_TILE

    def body(p_ref, halo_ref, gates_ref, oret_ref, bias_ref, scale_ref, pw_hbm, wru_ref, wpu_ref, wo_ref, h1_ref,
             h2_ref, a_ref, b_ref, pw, sem):
        i = pl.program_id(0)

        @pl.when(i == 0)
        def _():
            _load_pool_weight(pw_hbm, pw, sem)

        out = _mix_tail_compute(i, tm, p_ref, halo_ref, gates_ref, oret_ref, bias_ref, scale_ref, pw, wru_ref, wpu_ref)
        a_ref[...] = out[4].astype(BF16)
        b_ref[...] = out[5].astype(BF16)
        h2_ref[...] = h1_ref[...] + _dot(out[-1], wo_ref[...])

    est = 3 * 2 * 2 * D_MODEL * D_MODEL + 2 * tm * D_MODEL * (4 + 4 + 2 + 4 + 4) + 16 * tm * D_MODEL * 4
    return _call(
        body, name="mix_tail_fwd", grid=(nt,),
        in_specs=_mix_tail_specs(tm) + [_row_spec(tm, D_MODEL)],
        out_specs=[_row_spec(tm, D_MODEL)] * 3,
        out_shape=[jax.ShapeDtypeStruct((SEQ, D_MODEL), F32)] + [jax.ShapeDtypeStruct((SEQ, D_MODEL), BF16)] * 2,
        scratch_shapes=[pltpu.VMEM((len(POOL_WINDOWS), POOL_GROUP_DIM, POOL_GROUP_DIM), BF16), pltpu.SemaphoreType.DMA((N_DEV,))],
        vmem_bytes=est, args=[p, p, gates, o_ret, bias, scale, pw8, wru, wpu, wo, h1], ride=ride)


def _mix_tail_backward(dh2, p, gates, o_ret, ret, qkvg, a_saved, b_saved, bias, scale, pw8, wru, wpu, wo, ride=None):
    tm, nt = TOKEN_TILE, SEQ // TOKEN_TILE
    n_groups = len(POOL_WINDOWS)
    rows_per_dev = POOL_GROUP_DIM // N_DEV

    def body(p_ref, halo_ref, gates_ref, oret_ref, bias_ref, scale_ref, pw_hbm, wru_ref, wpu_ref, wo_ref,
             dh2_ref, ret_ref, gr_ref, a_ref, b_ref,
             dret_ref, dgr_ref, dgates_ref, dpooled_ref, dwo_ref, dwru_ref, dwpu_ref, dpw_ref, dbias_ref, dscale_ref,
             pw, sem, acc_wo, acc_wru, acc_wpu, acc_pw, send_sq, recv_sq, send_pw, recv_pw, send_sems, recv_sems):
        i = pl.program_id(0)

        @pl.when(i == 0)
        def _():
            _load_pool_weight(pw_hbm, pw, sem)
            for ref in (acc_wo, acc_wru, acc_wpu, acc_pw, dbias_ref, dscale_ref):
                ref[...] = jnp.zeros_like(ref)

        pooled, mixed, pool_out, o_ret, a, b, g0, g1, merged = _mix_tail_compute(
            i, tm, p_ref, halo_ref, gates_ref, oret_ref, bias_ref, scale_ref, pw, wru_ref, wpu_ref, saved=(a_ref, b_ref))
        dh2 = dh2_ref[...].astype(BF16)
        dm = _dot_nt(dh2, wo_ref[...])
        acc_wo[...] += _dot_tn(merged, dh2)
        da = (dm * g0).astype(BF16)
        db = (dm * g1).astype(BF16)
        dz0 = dm * a * g0 * (1.0 - g0)
        dz1 = dm * b * g1 * (1.0 - g1)
        dbias_ref[0:1, :] += jnp.sum(dz0, axis=0, keepdims=True)
        dbias_ref[1:2, :] += jnp.sum(dz1, axis=0, keepdims=True)
        dgates_ref[:, pl.ds(0, D_MODEL)] = dz0.astype(BF16)
        dgates_ref[:, pl.ds(D_MODEL, D_MODEL)] = dz1.astype(BF16)
        acc_wru[...] += _dot_tn(o_ret, da)
        acc_wpu[...] += _dot_tn(pool_out, db)
        d_oret = _dot_nt(da, wru_ref[...])
        d_pool_out = _dot_nt(db, wpu_ref[...])
        dscale_ref[...] += jnp.sum(d_pool_out * mixed, axis=0, keepdims=True)
        dmixed = (d_pool_out * scale_ref[...]).astype(BF16)
        for g in range(n_groups):
            dmg = dmixed[:, g * POOL_GROUP_DIM:(g + 1) * POOL_GROUP_DIM]
            acc_pw[g] += _dot_tn(pooled[g], dmg)
            dpooled_ref[:, pl.ds(g * POOL_GROUP_DIM, POOL_GROUP_DIM)] = _dot_nt(dmg, pw[g])
        gr = gr_ref[...].astype(F32)
        s = _sig(gr)
        silu = gr * s
        for hd in range(HEADS):
            cols = slice(hd * HEAD_DIM, (hd + 1) * HEAD_DIM)
            r_h = ret_ref[:, cols]
            rr = lax.rsqrt(jnp.mean(r_h * r_h, axis=-1, keepdims=True) + NORM_EPS)
            rhat = r_h * rr
            do_h = d_oret[:, cols]
            dgr_ref[:, cols] = (do_h * rhat * (s[:, cols] * (1.0 + gr[:, cols] * (1.0 - s[:, cols])))).astype(BF16)
            dret_ref[:, cols] = _rms_bwd(do_h * silu[:, cols], rhat, rr).astype(BF16)

        @pl.when(i == nt - 1)
        def _():
            c = lax.axis_index("c")
            rows = D_MODEL // N_DEV
            squares = ((acc_wo, dwo_ref), (acc_wru, dwru_ref), (acc_wpu, dwpu_ref))
            for q in range(n_chips):
                own = pl.multiple_of((2 * q + c) * rows, rows)
                other = pl.multiple_of((2 * q + 1 - c) * rows, rows)
                for t, (acc, out) in enumerate(squares):
                    out[q] = acc[pl.ds(own, rows), :].astype(BF16)
                    send_sq[t, q] = acc[pl.ds(other, rows), :].astype(BF16)
                own_pw = pl.multiple_of((2 * q + c) * rows_per_dev, rows_per_dev)
                other_pw = pl.multiple_of((2 * q + 1 - c) * rows_per_dev, rows_per_dev)
                dpw_ref[q] = acc_pw[:, pl.ds(own_pw, rows_per_dev), :].astype(BF16)
                send_pw[q] = acc_pw[:, pl.ds(other_pw, rows_per_dev), :].astype(BF16)
            pushes = [_to_sibling(send_sq, recv_sq, send_sems.at[0], recv_sems.at[0]),
                      _to_sibling(send_pw, recv_pw, send_sems.at[1], recv_sems.at[1])]
            for cp in pushes:
                cp.start()
            for cp in pushes:
                cp.wait_recv()
            for t, (acc, out) in enumerate(squares):
                out[...] = (out[...].astype(F32) + recv_sq[t].astype(F32)).astype(BF16)
            dpw_ref[...] = (dpw_ref[...].astype(F32) + recv_pw[...].astype(F32)).astype(BF16)
            for cp in pushes:
                cp.wait_send()

    n_chips = N_DEV // 2
    sq = (n_chips, D_MODEL // N_DEV, D_MODEL)
    pw_shape = (n_chips, n_groups, rows_per_dev, POOL_GROUP_DIM)
    est = (3 * 2 * 2 * D_MODEL * D_MODEL + 3 * 4 * D_MODEL * D_MODEL + 3 * 2 * 2 * D_MODEL * D_MODEL
           + 2 * tm * D_MODEL * (4 + 4 + 2 + 4 + 4 + 2 + 2 + 2 + 4 + 4) + 24 * tm * D_MODEL * 4)
    return _call(
        body, name="mix_tail_bwd", grid=(nt,),
        in_specs=_mix_tail_specs(tm) + [_row_spec(tm, D_MODEL), _row_spec(tm, D_MODEL), _row_spec(tm, D_MODEL, 3),
                                        _row_spec(tm, D_MODEL), _row_spec(tm, D_MODEL)],
        out_specs=[_row_spec(tm, D_MODEL), _row_spec(tm, D_MODEL), _row_spec(tm, 2 * D_MODEL), _row_spec(tm, D_MODEL),
                   _full_spec(sq), _full_spec(sq), _full_spec(sq), _full_spec(pw_shape),
                   _full_spec((2, D_MODEL)), _full_spec((1, D_MODEL))],
        out_shape=[jax.ShapeDtypeStruct((SEQ, D_MODEL), BF16), jax.ShapeDtypeStruct((SEQ, D_MODEL), BF16),
                   jax.ShapeDtypeStruct((SEQ, 2 * D_MODEL), BF16), jax.ShapeDtypeStruct((SEQ, D_MODEL), F32),
                   jax.ShapeDtypeStruct(sq, BF16), jax.ShapeDtypeStruct(sq, BF16), jax.ShapeDtypeStruct(sq, BF16),
                   jax.ShapeDtypeStruct(pw_shape, BF16),
                   jax.ShapeDtypeStruct((2, D_MODEL), F32), jax.ShapeDtypeStruct((1, D_MODEL), F32)],
        scratch_shapes=[pltpu.VMEM((n_groups, POOL_GROUP_DIM, POOL_GROUP_DIM), BF16), pltpu.SemaphoreType.DMA((N_DEV,)),
                        pltpu.VMEM((D_MODEL, D_MODEL), F32), pltpu.VMEM((D_MODEL, D_MODEL), F32),
                        pltpu.VMEM((D_MODEL, D_MODEL), F32), pltpu.VMEM((n_groups, POOL_GROUP_DIM, POOL_GROUP_DIM), F32),
                        pltpu.VMEM((3,) + sq, BF16), pltpu.VMEM((3,) + sq, BF16), pltpu.VMEM(pw_shape, BF16),
                        pltpu.VMEM(pw_shape, BF16), pltpu.SemaphoreType.DMA((2,)), pltpu.SemaphoreType.DMA((2,))],
        vmem_bytes=est, args=[p, p, gates, o_ret, bias, scale, pw8, wru, wpu, wo, dh2, ret, qkvg, a_saved, b_saved], ride=ride)


def _mix_proj_backward(dq, dk, dv, dgr, dpooled, dgates, cos, sin, h1, gain, dh2, wmix8, ride=None):
    tm, nt = TOKEN_TILE, SEQ // TOKEN_TILE
    halo_blocks = tm // HALO
    last_halo = SEQ // HALO - 1
    k_scale = HEAD_DIM ** -0.5

    def body(dq_ref, dk_ref, dv_ref, dgr_ref, dpool_ref, dhalo_ref, dgates_ref, cos_ref, sin_ref, h1_ref, g_ref,
             dh2_ref, wmix_hbm, dh1_ref, dproj_ref, dg_ref, wmix, sem):
        i = pl.program_id(0)

        @pl.when(i == 0)
        def _():
            _load_mix_weight(wmix_hbm, wmix, sem)
            dg_ref[...] = jnp.zeros_like(dg_ref)

        cos_t, sin_t = cos_ref[...], sin_ref[...]
        for seg, ref, scale in ((0, dq_ref, 1.0), (1, dk_ref, k_scale)):
            for hd in range(HEADS):
                lo = hd * HEAD_DIM
                d1, d2 = ref[:, lo:lo + ROT_HALF], ref[:, lo + ROT_HALF:lo + HEAD_DIM]
                dproj_ref[:, pl.ds(seg * D_MODEL + lo, ROT_HALF)] = ((d1 * cos_t + d2 * sin_t) * scale).astype(BF16)
                dproj_ref[:, pl.ds(seg * D_MODEL + lo + ROT_HALF, ROT_HALF)] = ((d2 * cos_t - d1 * sin_t) * scale).astype(BF16)
        dproj_ref[:, pl.ds(2 * D_MODEL, D_MODEL)] = dv_ref[...].astype(BF16)
        dproj_ref[:, pl.ds(3 * D_MODEL, D_MODEL)] = dgr_ref[...]
        dp = _pooled_transpose(jnp.concatenate([dpool_ref[...], dhalo_ref[...]], axis=0), i * tm)
        for g in range(len(POOL_WINDOWS)):
            dproj_ref[:, pl.ds(4 * D_MODEL + g * POOL_GROUP_DIM, POOL_GROUP_DIM)] = dp[g].astype(BF16)
        dproj_ref[:, pl.ds(5 * D_MODEL, 2 * D_MODEL)] = dgates_ref[...]
        du = jnp.zeros((tm, D_MODEL), F32)
        for seg in range(N_SEG):
            cols = pl.ds(seg * D_MODEL, D_MODEL)
            du = du + _dot_nt(dproj_ref[:, cols], wmix[:, cols])
        g = g_ref[...]
        _, xhat, r = _rms(h1_ref[...], g)
        dg_ref[...] += jnp.sum(du * xhat, axis=0, keepdims=True)
        dh1_ref[...] = dh2_ref[...] + _rms_bwd(du * g, xhat, r)

    est = 2 * D_MODEL * N_SEG * D_MODEL + 2 * tm * D_MODEL * (3 * 4 + 2 + 4 + 4 + 4 + 4 + 4 + 14) + 12 * tm * D_MODEL * 4
    return _call(
        body, name="mix_proj_bwd", grid=(nt,),
        in_specs=[_row_spec(tm, D_MODEL), _row_spec(tm, D_MODEL), _row_spec(tm, D_MODEL), _row_spec(tm, D_MODEL),
                  _row_spec(tm, D_MODEL),
                  pl.BlockSpec((HALO, D_MODEL), lambda i: (jnp.minimum((i + 1) * halo_blocks, last_halo), 0)),
                  _row_spec(tm, 2 * D_MODEL), _row_spec(tm, ROT_HALF), _row_spec(tm, ROT_HALF),
                  _row_spec(tm, D_MODEL), _full_spec((1, D_MODEL)), _row_spec(tm, D_MODEL), ANY],
        out_specs=[_row_spec(tm, D_MODEL), _row_spec(tm, N_SEG * D_MODEL), _full_spec((1, D_MODEL))],
        out_shape=[jax.ShapeDtypeStruct((SEQ, D_MODEL), F32), jax.ShapeDtypeStruct((SEQ, N_SEG * D_MODEL), BF16),
                   jax.ShapeDtypeStruct((1, D_MODEL), F32)],
        scratch_shapes=[pltpu.VMEM((D_MODEL, N_SEG * D_MODEL), BF16), pltpu.SemaphoreType.DMA((N_DEV,))],
        vmem_bytes=est, args=[dq, dk, dv, dgr, dpooled, dpooled, dgates, cos, sin, h1, gain, dh2, wmix8], ride=ride)


def _adamw(w, parts, m, v, name, after=None):
    rows, cols = w.shape
    n_lists = len(parts)
    tr = max([t for t in range(16, 257, 16) if rows % t == 0], default=rows)
    c1 = 1.0 - ADAM_B1 ** ADAM_STEP
    c2 = 1.0 - ADAM_B2 ** ADAM_STEP

    def body(*refs):
        w_ref, m_ref, v_ref = refs[:3]
        part_refs = refs[3:3 + n_lists]
        g_out, d_out, m_out, v_out = refs[-4:]
        g = None
        for p_ref in part_refs:
            for k in range(p_ref.shape[0]):
                term = p_ref[k].astype(F32)
                g = term if g is None else g + term
        m_new = ADAM_B1 * m_ref[...] + (1.0 - ADAM_B1) * g
        v_new = ADAM_B2 * v_ref[...] + (1.0 - ADAM_B2) * (g * g)
        g_out[...] = g
        m_out[...] = m_new
        v_out[...] = v_new
        d_out[...] = -ADAM_LR * ((m_new / c1) / (jnp.sqrt(v_new / c2) + ADAM_EPS) + ADAM_WD * w_ref[...])

    spec = pl.BlockSpec((tr, cols), lambda i: (i, 0))
    out = jax.ShapeDtypeStruct((rows, cols), F32)
    part_specs = [pl.BlockSpec((p.shape[0], tr, cols), lambda i: (0, i, 0)) for p in parts]
    part_bytes = sum(p.shape[0] * p.dtype.itemsize for p in parts)
    extra = [] if after is None else [after]
    return pl.pallas_call(
        body, name=name, grid=(rows // tr,),
        in_specs=[spec, spec, spec] + part_specs + [ANY] * len(extra),
        out_specs=[spec] * 4, out_shape=[out] * 4,
        compiler_params=_params(2 * tr * cols * (7 * 4 + part_bytes) + 8 * tr * cols * 4, 1),
    )(_in_hbm(w), _in_hbm(m), _in_hbm(v), *[_in_hbm(p) for p in parts], *extra)


def _mix_w_in_grad(u, dproj, ride=None):
    return _weight_grad(
        u, dproj, N_DEV,
        lambda tt: pl.BlockSpec((tt, D_MODEL), lambda b, t: (t, 0)),
        lambda tt: pl.BlockSpec((tt, MIX_SHARD), lambda b, t: (t, b)),
        D_MODEL, MIX_SHARD, name="w_in_grad", ride=ride)


def kernel(x, norm_ffn1, ffn1_w_in, ffn1_w_out, norm_mix, w_in, gate_bias, pool_w, pool_scale, w_ret_up, w_pool_up, w_out, norm_ffn2, ffn2_w_in, ffn2_w_out, norm_final, loss_target, m_norm_ffn1, m_ffn1_w_in, m_ffn1_w_out, m_norm_mix, m_w_in, m_gate_bias, m_pool_w, m_pool_scale, m_w_ret_up, m_w_pool_up, m_w_out, m_norm_ffn2, m_ffn2_w_in, m_ffn2_w_out, m_norm_final, v_norm_ffn1, v_ffn1_w_in, v_ffn1_w_out, v_norm_mix, v_w_in, v_gate_bias, v_pool_w, v_pool_scale, v_w_ret_up, v_w_pool_up, v_w_out, v_norm_ffn2, v_ffn2_w_in, v_ffn2_w_out, v_norm_final):
    assert x.shape == (1, SEQ, D_MODEL) and ffn1_w_in.shape == (1, D_MODEL, FF_SHARD) and w_in.shape == (1, D_MODEL, MIX_SHARD)
    x2, target = x[0], loss_target[0]

    cos, sin = _rotary_tables()
    tables = _retention_tables()
    bf = lambda w: w[0].astype(BF16)
    bf_t = lambda w: jnp.swapaxes(w[0], 0, 1).astype(BF16)
    square = lambda w: w.reshape(D_MODEL, D_MODEL)

    win1, wout1, bias8 = _alone(_GatherRide([bf_t(ffn1_w_in), bf(ffn1_w_out), gate_bias[0]]), "ffn1_weights_all_gather")
    wout1 = wout1.reshape(N_FF_GROUPS, FF_SHARD, D_MODEL)
    bias = bias8.transpose(1, 0, 2).reshape(2, D_MODEL)

    (h1, gu1), (wmix8,) = _ffn_forward(x2, norm_ffn1, win1, wout1, "ffn1_fwd", ride=_GatherRide([bf(w_in)]))
    (u, qkvg, p, gates), (win2,) = _mix_proj_forward(h1, norm_mix, wmix8, cos, sin, ride=_GatherRide([bf_t(ffn2_w_in)]))
    (ret, o_ret), (pw8, wru, wpu, wo) = _retention_forward(
        qkvg, tables, ride=_GatherRide([bf(pool_w), bf(w_ret_up), bf(w_pool_up), bf(w_out)]))
    wru, wpu, wo = square(wru), square(wpu), square(wo)
    (h2, a_saved, b_saved), (wout2,) = _mix_tail_forward(p, gates, o_ret, h1, bias, pool_scale, pw8, wru, wpu, wo,
                                        ride=_GatherRide([bf(ffn2_w_out)]))
    wout2 = wout2.reshape(N_FF_GROUPS, FF_SHARD, D_MODEL)
    (dh3, gu2, loss_part, d_norm_final), _ = _ffn_forward(h2, norm_ffn2, win2, wout2, "ffn2_fwd_loss",
                                                          head=(target, norm_final.reshape(1, D_MODEL)))

    dh2, dgu2, act2, xn2, df2, d_norm_ffn2 = _ffn_backward(dh3, h2, norm_ffn2, gu2, win2, wout2, "ffn2_bwd")
    d_wout2, _ = _ffn_w_out_grad(act2, df2, 2)
    d_win2, (r_wout2,) = _ffn_w_in_grad(xn2, dgu2, 2, ride=_ScatterRide([d_wout2]))
    (dret, dgr, dgates, dpooled, d_wo, d_wru, d_wpu, d_pw, d_bias, d_scale), (r_win2,) = _mix_tail_backward(
        dh2, p, gates, o_ret, ret, qkvg, a_saved, b_saved, bias, pool_scale, pw8, wru, wpu, wo, ride=_ScatterRide([d_win2]))
    (dq,), _ = _retention_backward_q(qkvg, dret, tables)
    (dk, dv), (r_pw, r_wru, r_wpu, r_wo) = _retention_backward_kv(
        qkvg, dret, tables, ride=_ScatterRide([d_pw, d_wru, d_wpu, d_wo]))
    (dh1, dproj, d_norm_mix), _ = _mix_proj_backward(dq, dk, dv, dgr, dpooled, dgates, cos, sin, h1, norm_mix, dh2, wmix8)
    d_wmix, _ = _mix_w_in_grad(u, dproj)
    wmix_state, wmix_started = _scatter_start(d_wmix, "w_in_grad_exchange_start")
    grad_x, dgu1, act1, xn1, df1, d_norm_ffn1 = _ffn_backward(dh1, x2, norm_ffn1, gu1, win1, wout1, "ffn1_bwd", after=wmix_started)
    d_win1, _ = _ffn_w_in_grad(xn1, dgu1, 1)
    win1_state, win1_started = _scatter_start(d_win1, "ffn1_w_in_grad_exchange_start")
    d_wout1, _ = _ffn_w_out_grad(act1, df1, 1, after=win1_started)
    wout1_state, started = _scatter_start(d_wout1, "ffn1_w_out_grad_exchange_start")
    zero_row = jnp.zeros((1, D_MODEL), F32)
    small = _all_reduce_rows(jnp.concatenate(
        [d_norm_ffn1, d_norm_mix, d_scale, d_norm_ffn2, d_norm_final, d_bias, jnp.tile(loss_part, (1, D_MODEL // 128))],
        axis=0))
    loss = small[7, 0]

    results = {}

    def update(nm, w, parts, m, v, after):
        if nm in ("ffn1_w_in", "ffn2_w_in"):
            flat, back = (lambda a: jnp.swapaxes(a[0], 0, 1)), (lambda o: jnp.swapaxes(o, 0, 1)[None])
        else:
            flat, back = (lambda a: a.reshape(-1, w.shape[-1])), (lambda o: o.reshape(w.shape))
        parts = [p.reshape(p.shape[:1] + flat(w).shape) for p in parts]
        outs = _adamw(flat(w), parts, flat(m), flat(v), name=f"adamw_{nm}", after=after)
        results[nm] = [back(o) for o in outs]
        return outs[0]

    done = update("w_in", w_in, _scatter_wait(wmix_state, started, "w_in_grad_exchange_wait"), m_w_in, v_w_in, None)
    for nm, w, parts, m, v in (
            ("ffn2_w_in", ffn2_w_in, r_win2, m_ffn2_w_in, v_ffn2_w_in),
            ("ffn2_w_out", ffn2_w_out, r_wout2, m_ffn2_w_out, v_ffn2_w_out), ("w_ret_up", w_ret_up, r_wru, m_w_ret_up, v_w_ret_up),
            ("w_pool_up", w_pool_up, r_wpu, m_w_pool_up, v_w_pool_up), ("w_out", w_out, r_wo, m_w_out, v_w_out),
            ("pool_w", pool_w, r_pw, m_pool_w, v_pool_w)):
        done = update(nm, w, [parts], m, v, done)
    done = update("ffn1_w_in", ffn1_w_in, _scatter_wait(win1_state, done, "ffn1_w_in_grad_exchange_wait"),
                  m_ffn1_w_in, v_ffn1_w_in, None)
    update("ffn1_w_out", ffn1_w_out, _scatter_wait(wout1_state, done, "ffn1_w_out_grad_exchange_wait"),
           m_ffn1_w_out, v_ffn1_w_out, None)

    my_id = _linear_id(*_my_position())
    bias_cols = gate_bias.shape[-1]
    pad = lambda a: jnp.pad(a[0], ((0, 0), (0, D_MODEL - bias_cols)))
    pack = lambda a, b, c, d, e, gb: jnp.concatenate([a, b, c, d, e.reshape(1, D_MODEL), pad(gb), zero_row], axis=0)
    d_bias_mine = lax.dynamic_slice_in_dim(small[5:7], my_id * bias_cols, bias_cols, axis=1)
    g_small = jnp.concatenate([small[0:5], jnp.pad(d_bias_mine, ((0, 0), (0, D_MODEL - bias_cols))), zero_row], axis=0)
    s_outs = _adamw(pack(norm_ffn1, norm_mix, pool_scale, norm_ffn2, norm_final, gate_bias), [g_small[None]],
                    pack(m_norm_ffn1, m_norm_mix, m_pool_scale, m_norm_ffn2, m_norm_final, m_gate_bias),
                    pack(v_norm_ffn1, v_norm_mix, v_pool_scale, v_norm_ffn2, v_norm_final, v_gate_bias), name="adamw_small")
    for row, nm in enumerate(["norm_ffn1", "norm_mix", "pool_scale", "norm_ffn2"]):
        results[nm] = [o[row:row + 1] for o in s_outs]
    results["norm_final"] = [o[4] for o in s_outs]
    results["gate_bias"] = [o[5:7, :bias_cols][None] for o in s_outs]

    order = ["norm_ffn1", "ffn1_w_in", "ffn1_w_out", "norm_mix", "w_in", "gate_bias", "pool_w", "pool_scale",
             "w_ret_up", "w_pool_up", "w_out", "norm_ffn2", "ffn2_w_in", "ffn2_w_out", "norm_final"]
    return (loss, grad_x[None], *[results[nm][0] for nm in order], *[results[nm][1] for nm in order],
            *[results[nm][2] for nm in order], *[results[nm][3] for nm in order])
```

```python
import numpy as np
import jax
import jax.numpy as jnp
from jax import lax
from jax.experimental import pallas as pl
from jax.experimental.pallas import tpu as pltpu

F32 = jnp.float32
BF16 = jnp.bfloat16

N_DEV = 8
D_MODEL = 1024
SEQ = 4096
D_FF = 2816
FF_SHARD = 2 * D_FF // N_DEV
N_FF_GROUPS = N_DEV // 2
HEADS = 4
HEAD_DIM = 256
ROT_HALF = HEAD_DIM // 2
CHUNK = 64
RET_BLOCK = 256
POOL_WINDOWS = (2, 4, 8, 16)
POOL_GROUP_DIM = 256
HALO = 16
MIX_SHARD = 7 * D_MODEL // N_DEV
N_SEG = 7
ROPE_BASE = 10000.0
NORM_EPS = 1e-6
FFN_RES_WEIGHT = 0.5
ADAM_LR, ADAM_B1, ADAM_B2, ADAM_EPS, ADAM_WD, ADAM_STEP = 0.001, 0.9, 0.999, 1e-08, 0.01, 10

TOKEN_TILE = 256
WIDE_TILE = 512
VMEM_CAP_V7X = 64 * 1024 * 1024
MESH = pl.DeviceIdType.MESH
ANY = pl.BlockSpec(memory_space=pl.ANY)


def _vmem_limit(estimate_bytes):
    return int(min(estimate_bytes * 5 // 4 + (6 << 20), VMEM_CAP_V7X - (4 << 20)))


def _params(estimate_bytes, n_grid):
    return pltpu.CompilerParams(dimension_semantics=("arbitrary",) * n_grid,
                                vmem_limit_bytes=_vmem_limit(estimate_bytes))


def _dot(a, b):
    return jnp.dot(a, b, preferred_element_type=F32)


def _dot_nt(a, b):
    return lax.dot_general(a, b, (((1,), (1,)), ((), ())), preferred_element_type=F32)


def _dot_tn(a, b):
    return lax.dot_general(a, b, (((0,), (0,)), ((), ())), preferred_element_type=F32)


def _sig(x):
    return 1.0 / (1.0 + jnp.exp(-x))


def _rms(x, g):
    r = lax.rsqrt(jnp.mean(x * x, axis=-1, keepdims=True) + NORM_EPS)
    xhat = x * r
    return xhat * g, xhat, r


def _rms_bwd(dyg, xhat, r):
    return r * (dyg - xhat * jnp.mean(dyg * xhat, axis=-1, keepdims=True))


def _row_spec(tile, width, col=0):
    return pl.BlockSpec((tile, width), lambda i, c=col: (i, c))


def _full_spec(shape):
    return pl.BlockSpec(shape, lambda *_: (0,) * len(shape))


def _rotary_tables():
    inv_freq = (np.float32(ROPE_BASE) ** (-np.arange(ROT_HALF, dtype=np.float32) / np.float32(ROT_HALF))).astype(np.float32)
    ang = (np.arange(SEQ, dtype=np.float32)[:, None] * inv_freq[None, :]).astype(np.float32)
    return jnp.asarray(np.cos(ang.astype(np.float64)), F32), jnp.asarray(np.sin(ang.astype(np.float64)), F32)


def _retention_tables():
    log_gamma = np.log(1.0 - 2.0 ** (-5.0 - np.arange(HEADS, dtype=np.float64)))
    n = np.arange(RET_BLOCK)
    diff = (n[:, None] - n[None, :]).astype(np.float64)
    same = (n[:, None] // CHUNK) == (n[None, :] // CHUNK)
    earlier = (n[None, :] // CHUNK) < (n[:, None] // CHUNK)
    expo = np.where(same, np.abs(diff), diff)
    mask = np.where(same | earlier, np.exp(log_gamma[:, None, None] * expo[None]), 0.0)
    qdec = np.exp(log_gamma[:, None] * (n[None, :] + 1.0))[:, :, None]
    kdec = np.exp(log_gamma[:, None] * (RET_BLOCK - 1.0 - n[None, :]))[:, :, None]
    cdec = np.exp(log_gamma * RET_BLOCK)[:, None, None]
    return (jnp.asarray(mask, F32), jnp.asarray(qdec, F32), jnp.asarray(kdec, F32), jnp.asarray(cdec, F32))


def _my_position():
    return lax.axis_index("x"), lax.axis_index("y"), lax.axis_index("c")


def _linear_id(px, py, pc):
    return 4 * px + 2 * py + pc


def _when(pred, fn):
    if isinstance(pred, bool):
        if pred:
            fn()
    else:
        pl.when(pred)(fn)


class _GatherRide:
    def __init__(self, shards):
        self.args = list(shards)
        n = self.n = len(shards)
        self.out_shape = [pltpu.HBM((N_DEV,) + s.shape, s.dtype) for s in shards]
        self.scratch = [pltpu.SemaphoreType.DMA((n, 7)), pltpu.SemaphoreType.DMA((n, 7)), pltpu.SemaphoreType.DMA((n,))]

    def _plan(self, src, out, sems):
        send_sems, recv_sems, local_sem = sems
        x, y, c = _my_position()
        me, sibling = (x, y, c), (x, y, 1 - c)
        chips = [(1 - x, y), (x, 1 - y), (1 - x, 1 - y)]

        def copy(t, k, block, to, from_src=False):
            rows = out[t].at[_linear_id(*block)]
            return pltpu.make_async_remote_copy(
                src_ref=src[t] if from_src else rows, dst_ref=rows,
                send_sem=send_sems.at[t, k], recv_sem=recv_sems.at[t, k],
                device_id=to, device_id_type=MESH)

        def relay(t):
            return copy(t, 3, (x ^ (1 - c), y ^ c, c), (x ^ c, y ^ (1 - c), c))

        local = [pltpu.make_async_copy(src[t], out[t].at[_linear_id(*me)], local_sem.at[t]) for t in range(self.n)]
        return copy, relay, local, me, sibling, chips, c

    def begin(self, first, src, out, sems):
        copy, relay, local, me, sibling, chips, c = self._plan(src, out, sems)

        def start():
            for cp in local:
                cp.start()
            for t in range(self.n):
                copy(t, 0, me, sibling, from_src=True).start()
                for j in range(2):
                    copy(t, 1 + j, me, (*chips[j], c), from_src=True).start()

        _when(first, start)

    def finish(self, mid, late, last, src, out, sems):
        copy, relay, local, me, sibling, chips, c = self._plan(src, out, sems)

        def pass_on():
            for t in range(self.n):
                for j in range(2):
                    copy(t, 1 + j, (*chips[j], c), me).wait_recv()
                relay(t).start()
                for j in range(2):
                    copy(t, 4 + j, (*chips[j], c), sibling).start()

        def pass_on_relayed():
            for t in range(self.n):
                copy(t, 3, (*chips[2], c), me).wait_recv()
                copy(t, 6, (*chips[2], c), sibling).start()

        def drain():
            for t in range(self.n):
                copy(t, 0, sibling, me).wait_recv()
                for j in range(3):
                    copy(t, 4 + j, (*chips[j], 1 - c), me).wait_recv()
            for t in range(self.n):
                copy(t, 0, me, sibling, from_src=True).wait_send()
                for j in range(2):
                    copy(t, 1 + j, me, (*chips[j], c), from_src=True).wait_send()
                relay(t).wait_send()
                for j in range(3):
                    copy(t, 4 + j, (*chips[j], c), sibling).wait_send()
            for cp in local:
                cp.wait()

        _when(mid, pass_on)
        _when(late, pass_on_relayed)
        _when(last, drain)


class _ScatterRide:
    def __init__(self, chip_sums):
        self.args = list(chip_sums)
        n = self.n = len(chip_sums)
        self.out_shape = [pltpu.HBM(p.shape, p.dtype) for p in chip_sums]
        self.scratch = [pltpu.SemaphoreType.DMA((n, 3)), pltpu.SemaphoreType.DMA((n, 3)), pltpu.SemaphoreType.DMA((n,))]

    def _plan(self, src, out, sems):
        send_sems, recv_sems, local_sem = sems
        x, y, c = _my_position()

        def peer(k):
            return (x ^ (k >> 1), y ^ (k & 1))

        copies = [pltpu.make_async_remote_copy(
            src_ref=src[t].at[2 * peer(k)[0] + peer(k)[1]], dst_ref=out[t].at[k],
            send_sem=send_sems.at[t, k - 1], recv_sem=recv_sems.at[t, k - 1],
            device_id=(*peer(k), c), device_id_type=MESH) for t in range(self.n) for k in range(1, N_DEV // 2)]
        local = [pltpu.make_async_copy(src[t].at[2 * x + y], out[t].at[0], local_sem.at[t]) for t in range(self.n)]
        return copies, local

    def begin(self, first, src, out, sems):
        copies, local = self._plan(src, out, sems)

        def start():
            for cp in local + copies:
                cp.start()

        _when(first, start)

    def finish(self, mid, late, last, src, out, sems):
        copies, local = self._plan(src, out, sems)

        def drain():
            for cp in copies:
                cp.wait_recv()
            for cp in copies:
                cp.wait_send()
            for cp in local:
                cp.wait()

        _when(last, drain)


def _in_hbm(a):
    return pltpu.with_memory_space_constraint(a, pltpu.HBM)


def _call(body, *, name, grid, in_specs, out_specs, out_shape, scratch_shapes, vmem_bytes, args, ride=None, after=None):
    n_in, n_out, n_s = len(in_specs), len(out_specs), len(scratch_shapes)
    params = _params(vmem_bytes, len(grid))
    args = [_in_hbm(a) for a in args]
    out_shape = [pltpu.HBM(s.shape, s.dtype) for s in out_shape]
    if ride is None:
        if after is not None:
            def ordered_body(*refs):
                body(*refs[:n_in], *refs[n_in + 1:])
            outs = pl.pallas_call(ordered_body, name=name, grid=grid, in_specs=list(in_specs) + [ANY], out_specs=out_specs,
                                  out_shape=out_shape, scratch_shapes=scratch_shapes, compiler_params=params)(*args, after)
            return list(outs), []
        outs = pl.pallas_call(body, name=name, grid=grid, in_specs=in_specs, out_specs=out_specs, out_shape=out_shape,
                              scratch_shapes=scratch_shapes, compiler_params=params)(*args)
        return list(outs), []
    total = int(np.prod(grid))

    def riding_body(*refs):
        a = n_in
        b = a + ride.n
        c = b + n_out
        d = c + ride.n
        e = d + n_s
        step = pl.program_id(0)
        for axis in range(1, len(grid)):
            step = step * grid[axis] + pl.program_id(axis)
        ride.begin(step == 0, refs[a:b], refs[c:d], refs[e:])
        body(*refs[:a], *refs[b:c], *refs[d:e])
        ride.finish(step == total // 2, step == (3 * total) // 4, step == total - 1, refs[a:b], refs[c:d], refs[e:])

    outs = pl.pallas_call(
        riding_body, name=name, grid=grid, in_specs=list(in_specs) + [ANY] * ride.n,
        out_specs=list(out_specs) + [ANY] * ride.n, out_shape=list(out_shape) + ride.out_shape,
        scratch_shapes=list(scratch_shapes) + ride.scratch, compiler_params=params)(*args, *[_in_hbm(a) for a in ride.args])
    return list(outs[:n_out]), list(outs[n_out:])


def _alone(ride, name):
    def body(*refs):
        src, out, sems = refs[:ride.n], refs[ride.n:2 * ride.n], refs[2 * ride.n:]
        ride.begin(True, src, out, sems)
        ride.finish(True, True, True, src, out, sems)

    return list(pl.pallas_call(body, name=name, out_shape=ride.out_shape, in_specs=[ANY] * ride.n,
                               out_specs=[ANY] * ride.n, scratch_shapes=ride.scratch)(*[_in_hbm(a) for a in ride.args]))


def _scatter_copies(src, land, send_sems, recv_sems):
    x, y, c = _my_position()
    copies = []
    for k in range(1, N_DEV // 2):
        px, py = x ^ (k >> 1), y ^ (k & 1)
        copies.append(pltpu.make_async_remote_copy(
            src_ref=src.at[2 * px + py], dst_ref=land.at[k - 1], send_sem=send_sems.at[k - 1], recv_sem=recv_sems.at[k - 1],
            device_id=(px, py, c), device_id_type=MESH))
    return copies


def _scatter_start(chip_sums, name):
    n_peers = N_DEV // 2 - 1
    land_shape = (n_peers,) + chip_sums.shape[1:]
    hbm = pl.BlockSpec(memory_space=pltpu.HBM)
    sem = pl.BlockSpec(memory_space=pltpu.SEMAPHORE)

    def body(src_ref, land_ref, send_sems, recv_sems, src_thru, land_thru, token):
        for cp in _scatter_copies(src_ref, land_ref, send_sems, recv_sems):
            cp.start()
        token[...] = jnp.zeros_like(token)

    send_sems, recv_sems, src_thru, land_thru, token = pl.pallas_call(
        body, name=name,
        out_shape=(pltpu.SemaphoreType.DMA((n_peers,)), pltpu.SemaphoreType.DMA((n_peers,)),
                   pltpu.HBM(chip_sums.shape, chip_sums.dtype), pltpu.HBM(land_shape, chip_sums.dtype),
                   jax.ShapeDtypeStruct((8, 128), F32)),
        in_specs=(hbm, hbm), out_specs=(sem, sem, hbm, hbm, pl.BlockSpec(memory_space=pltpu.VMEM)),
        input_output_aliases={0: 2, 1: 3},
        compiler_params=pltpu.CompilerParams(has_side_effects=pltpu.SideEffectType.DATAFLOW_SIDE_EFFECTING),
    )(_in_hbm(chip_sums), _in_hbm(lax.empty(land_shape, chip_sums.dtype)))
    return (send_sems, recv_sems, src_thru, land_thru), token


def _scatter_wait(state, after, name):
    send_sems, recv_sems, src_thru, land_thru = state
    hbm = pl.BlockSpec(memory_space=pltpu.HBM)
    sem = pl.BlockSpec(memory_space=pltpu.SEMAPHORE)

    def body(src_ref, land_ref, send_sems, recv_sems, after_ref, src_out, land_out):
        for cp in _scatter_copies(src_ref, land_ref, send_sems, recv_sems):
            cp.wait_send()
            cp.wait_recv()

    src_done, land_done = pl.pallas_call(
        body, name=name,
        out_shape=(pltpu.HBM(src_thru.shape, src_thru.dtype), pltpu.HBM(land_thru.shape, land_thru.dtype)),
        in_specs=(hbm, hbm, sem, sem, ANY), out_specs=(hbm, hbm), input_output_aliases={0: 0, 1: 1},
        compiler_params=pltpu.CompilerParams(has_side_effects=pltpu.SideEffectType.DATAFLOW_SIDE_EFFECTING),
    )(src_thru, land_thru, send_sems, recv_sems, after)
    x, y, _ = _my_position()
    return lax.dynamic_slice_in_dim(src_done, 2 * x + y, 1, axis=0), land_done


def _load_ffn_weights(win_hbm, wout_hbm, win, wout, sem):
    a = pltpu.make_async_copy(win_hbm, win, sem.at[0])
    b = pltpu.make_async_copy(wout_hbm, wout, sem.at[1])
    a.start()
    b.start()
    a.wait()
    b.wait()


def _ffn_forward(h_in, gain, win8, wout, name, head=None, ride=None):
    tm, nt = WIDE_TILE, SEQ // WIDE_TILE

    def body(*refs):
        if head is None:
            x_ref, g_ref, win_hbm, wout_hbm, out_ref, gu_ref, win, wout, sem = refs
        else:
            x_ref, g_ref, win_hbm, wout_hbm, tgt_ref, gf_ref, out_ref, gu_ref, loss_ref, dgf_ref, win, wout, sem = refs
        i = pl.program_id(0)

        @pl.when(i == 0)
        def _():
            _load_ffn_weights(win_hbm, wout_hbm, win, wout, sem)
            if head is not None:
                loss_ref[...] = jnp.zeros_like(loss_ref)
                dgf_ref[...] = jnp.zeros_like(dgf_ref)

        x = x_ref[...]
        xn, _, _ = _rms(x, g_ref[...])
        xb = xn.astype(BF16)
        acc = jnp.zeros((tm, D_MODEL), F32)
        for j in range(N_FF_GROUPS):
            gate = _dot_nt(xb, win[j])
            up = _dot_nt(xb, win[j + N_FF_GROUPS])
            gu_ref[j] = gate.astype(BF16)
            gu_ref[j + N_FF_GROUPS] = up.astype(BF16)
            act = gate * _sig(gate) * up
            acc = acc + _dot(act.astype(BF16), wout[j])
        h = x + FFN_RES_WEIGHT * acc
        if head is None:
            out_ref[...] = h
        else:
            gf = gf_ref[...]
            y, hhat, r = _rms(h, gf)
            err = y - tgt_ref[...]
            loss_ref[...] += jnp.full(loss_ref.shape, 0.5 / D_MODEL * jnp.sum(err * err), F32)
            dy = err * (1.0 / D_MODEL)
            dgf_ref[...] += jnp.sum(dy * hhat, axis=0, keepdims=True)
            out_ref[...] = _rms_bwd(dy * gf, hhat, r)

    weights = 2 * D_MODEL * 2 * D_FF + 2 * D_FF * D_MODEL
    tiles = 2 * (2 * 4 * tm * D_MODEL + 2 * tm * 2 * D_FF) + (2 * 4 * tm * D_MODEL if head else 0)
    in_specs = [_row_spec(tm, D_MODEL), _full_spec((1, D_MODEL)), ANY, ANY]
    out_shape = [jax.ShapeDtypeStruct((SEQ, D_MODEL), F32), jax.ShapeDtypeStruct((N_DEV, SEQ, FF_SHARD), BF16)]
    out_specs = [_row_spec(tm, D_MODEL), pl.BlockSpec((N_DEV, tm, FF_SHARD), lambda i: (0, i, 0))]
    args = [h_in, gain, win8, wout]
    if head is not None:
        in_specs += [_row_spec(tm, D_MODEL), _full_spec((1, D_MODEL))]
        out_shape += [jax.ShapeDtypeStruct((1, 128), F32), jax.ShapeDtypeStruct((1, D_MODEL), F32)]
        out_specs += [_full_spec((1, 128)), _full_spec((1, D_MODEL))]
        args += list(head)
    return _call(
        body, name=name, grid=(nt,), in_specs=in_specs, out_specs=out_specs, out_shape=out_shape,
        scratch_shapes=[pltpu.VMEM((N_DEV, FF_SHARD, D_MODEL), BF16), pltpu.VMEM((N_FF_GROUPS, FF_SHARD, D_MODEL), BF16),
                        pltpu.SemaphoreType.DMA((2,))],
        vmem_bytes=weights + tiles + 16 * tm * FF_SHARD * 4, args=args, ride=ride)


def _ffn_backward(dh_out, h_in, gain, gu, win8, wout, name, after=None):
    tm, nt = TOKEN_TILE, SEQ // TOKEN_TILE

    def body(dh_ref, x_ref, g_ref, gu_ref, win_hbm, wout_hbm,
             dhin_ref, dgu_ref, act_ref, xn_ref, df_ref, dg_ref, win, wout, sem):
        i = pl.program_id(0)

        @pl.when(i == 0)
        def _():
            _load_ffn_weights(win_hbm, wout_hbm, win, wout, sem)
            dg_ref[...] = jnp.zeros_like(dg_ref)

        dh = dh_ref[...]
        g = g_ref[...]
        xn, xhat, r = _rms(x_ref[...], g)
        df = (FFN_RES_WEIGHT * dh).astype(BF16)
        dxn = jnp.zeros((tm, D_MODEL), F32)
        for j in range(N_FF_GROUPS):
            gate = gu_ref[j].astype(F32)
            up = gu_ref[j + N_FF_GROUPS].astype(F32)
            dact = _dot_nt(df, wout[j])
            s = _sig(gate)
            silu = gate * s
            dgate = (dact * up * (s * (1.0 + gate * (1.0 - s)))).astype(BF16)
            dup = (dact * silu).astype(BF16)
            act_ref[j] = (silu * up).astype(BF16)
            dgu_ref[j] = dgate
            dgu_ref[j + N_FF_GROUPS] = dup
            dxn = dxn + _dot(dgate, win[j]) + _dot(dup, win[j + N_FF_GROUPS])
        dg_ref[...] += jnp.sum(dxn * xhat, axis=0, keepdims=True)
        dhin_ref[...] = dh + _rms_bwd(dxn * g, xhat, r)
        xn_ref[...] = xn.astype(BF16)
        df_ref[...] = df

    weights = 2 * D_MODEL * 2 * D_FF + 2 * D_FF * D_MODEL
    tiles = 2 * (3 * 4 * tm * D_MODEL + 2 * tm * (2 * 2 * D_FF + D_FF) + 2 * 2 * tm * D_MODEL)
    gu_spec = pl.BlockSpec((N_DEV, tm, FF_SHARD), lambda i: (0, i, 0))
    return _call(
        body, name=name, grid=(nt,),
        in_specs=[_row_spec(tm, D_MODEL), _row_spec(tm, D_MODEL), _full_spec((1, D_MODEL)), gu_spec, ANY, ANY],
        out_specs=[_row_spec(tm, D_MODEL), gu_spec, pl.BlockSpec((N_FF_GROUPS, tm, FF_SHARD), lambda i: (0, i, 0)),
                   _row_spec(tm, D_MODEL), _row_spec(tm, D_MODEL), _full_spec((1, D_MODEL))],
        out_shape=[jax.ShapeDtypeStruct((SEQ, D_MODEL), F32), jax.ShapeDtypeStruct((N_DEV, SEQ, FF_SHARD), BF16),
                   jax.ShapeDtypeStruct((N_FF_GROUPS, SEQ, FF_SHARD), BF16), jax.ShapeDtypeStruct((SEQ, D_MODEL), BF16),
                   jax.ShapeDtypeStruct((SEQ, D_MODEL), BF16), jax.ShapeDtypeStruct((1, D_MODEL), F32)],
        scratch_shapes=[pltpu.VMEM((N_DEV, FF_SHARD, D_MODEL), BF16), pltpu.VMEM((N_FF_GROUPS, FF_SHARD, D_MODEL), BF16),
                        pltpu.SemaphoreType.DMA((2,))],
        vmem_bytes=weights + tiles + 20 * tm * FF_SHARD * 4, args=[dh_out, h_in, gain, gu, win8, wout], after=after)[0]


def _to_sibling(src, dst, send_sem, recv_sem):
    x, y, c = _my_position()
    return pltpu.make_async_remote_copy(src_ref=src, dst_ref=dst, send_sem=send_sem, recv_sem=recv_sem,
                                        device_id=(x, y, 1 - c), device_id_type=MESH)


def _weight_grad(x, g, n_out, x_spec, g_spec, k_dim, n_dim, name, halves=False, tt=2048, ride=None, after=None):
    nt = SEQ // tt
    n_chips = N_DEV // 2
    rows = k_dim // 2 if halves else k_dim

    def body(x_ref, g_ref, out_ref, acc, sendbuf, recvbuf, send_sems, recv_sems):
        b, t = pl.program_id(0), pl.program_id(1)
        c = lax.axis_index("c")

        def push(q):
            return _to_sibling(sendbuf.at[q], recvbuf.at[q], send_sems.at[q], recv_sems.at[q])

        @pl.when(t == 0)
        def _():
            acc[...] = jnp.zeros_like(acc)

        acc[...] += _dot_tn(x_ref[...], g_ref[...])

        @pl.when(t == nt - 1)
        def _():
            if halves:
                for mine, other in ((0, 1), (1, 0)):
                    @pl.when(c == mine)
                    def _():
                        out_ref[b] = acc[pl.ds(mine * rows, rows), :].astype(BF16)
                        sendbuf[b] = acc[pl.ds(other * rows, rows), :].astype(BF16)
                push(b).start()
            else:
                q = b // 2

                @pl.when(b % 2 == c)
                def _():
                    out_ref[q] = acc[...].astype(BF16)

                @pl.when(b % 2 != c)
                def _():
                    sendbuf[q] = acc[...].astype(BF16)
                    push(q).start()

        @pl.when((b == n_out - 1) & (t == nt - 1))
        def _():
            for q in range(n_chips):
                push(q).wait_recv()
                out_ref[q] = (out_ref[q].astype(F32) + recvbuf[q].astype(F32)).astype(BF16)
            for q in range(n_chips):
                push(q).wait_send()

    piece = (n_chips, rows, n_dim)
    outs, ride_outs = _call(
        body, name=name, grid=(n_out, nt), in_specs=[x_spec(tt), g_spec(tt)],
        out_specs=[pl.BlockSpec(piece, lambda b, t: (0, 0, 0))],
        out_shape=[jax.ShapeDtypeStruct(piece, BF16)],
        scratch_shapes=[pltpu.VMEM((k_dim, n_dim), F32), pltpu.VMEM(piece, BF16), pltpu.VMEM(piece, BF16),
                        pltpu.SemaphoreType.DMA((n_chips,)), pltpu.SemaphoreType.DMA((n_chips,))],
        vmem_bytes=2 * 2 * tt * (k_dim + n_dim) + 8 * k_dim * n_dim + 4 * 2 * n_chips * rows * n_dim, args=[x, g], ride=ride,
        after=after)
    return outs[0], ride_outs


def _ffn_w_out_grad(act, df, tag, ride=None, after=None):
    return _weight_grad(
        act, df, N_FF_GROUPS,
        lambda tt: pl.BlockSpec((None, tt, FF_SHARD), lambda b, t: (b, t, 0)),
        lambda tt: pl.BlockSpec((tt, D_MODEL), lambda b, t: (t, 0)),
        FF_SHARD, D_MODEL, name=f"ffn{tag}_w_out_grad", halves=True, ride=ride, after=after)


def _ffn_w_in_grad(xn, dgu, tag, ride=None):
    return _weight_grad(
        dgu, xn, N_DEV,
        lambda tt: pl.BlockSpec((None, tt, FF_SHARD), lambda b, t: (b, t, 0)),
        lambda tt: pl.BlockSpec((tt, D_MODEL), lambda b, t: (t, 0)),
        FF_SHARD, D_MODEL, name=f"ffn{tag}_w_in_grad", ride=ride)


def _load_mix_weight(wmix_hbm, wmix, sem):
    copies = [pltpu.make_async_copy(wmix_hbm.at[d], wmix.at[:, pl.ds(d * MIX_SHARD, MIX_SHARD)], sem.at[d])
              for d in range(N_DEV)]
    for cp in copies:
        cp.start()
    for cp in copies:
        cp.wait()


def _load_pool_weight(pw_hbm, pw, sem):
    rows = POOL_GROUP_DIM // N_DEV
    copies = [pltpu.make_async_copy(pw_hbm.at[d], pw.at[:, pl.ds(d * rows, rows), :], sem.at[d]) for d in range(N_DEV)]
    for cp in copies:
        cp.start()
    for cp in copies:
        cp.wait()


def _rotate(x1, x2, cos, sin):
    return x1 * cos - x2 * sin, x1 * sin + x2 * cos


def _mix_proj_forward(h1, gain, wmix8, cos, sin, ride=None):
    tm, nt = WIDE_TILE, SEQ // WIDE_TILE
    k_scale = HEAD_DIM ** -0.5

    def body(h_ref, g_ref, wmix_hbm, cos_ref, sin_ref, u_ref, qkvg_ref, p_ref, gates_ref, wmix, sem):
        @pl.when(pl.program_id(0) == 0)
        def _():
            _load_mix_weight(wmix_hbm, wmix, sem)

        u = _rms(h_ref[...], g_ref[...])[0].astype(BF16)
        u_ref[...] = u
        cos_t, sin_t = cos_ref[...], sin_ref[...]
        for seg in range(N_SEG):
            pr = _dot(u, wmix[:, pl.ds(seg * D_MODEL, D_MODEL)])
            if seg < 2:
                scale = 1.0 if seg == 0 else k_scale
                for hd in range(HEADS):
                    lo = hd * HEAD_DIM
                    o1, o2 = _rotate(pr[:, lo:lo + ROT_HALF], pr[:, lo + ROT_HALF:lo + HEAD_DIM], cos_t, sin_t)
                    qkvg_ref[:, pl.ds(seg * D_MODEL + lo, ROT_HALF)] = (o1 * scale).astype(BF16)
                    qkvg_ref[:, pl.ds(seg * D_MODEL + lo + ROT_HALF, ROT_HALF)] = (o2 * scale).astype(BF16)
            elif seg < 4:
                qkvg_ref[:, pl.ds(seg * D_MODEL, D_MODEL)] = pr.astype(BF16)
            elif seg == 4:
                p_ref[...] = pr
            else:
                gates_ref[:, pl.ds((seg - 5) * D_MODEL, D_MODEL)] = pr.astype(BF16)

    est = 2 * D_MODEL * N_SEG * D_MODEL + 2 * tm * (4 * D_MODEL + 2 * D_MODEL + 2 * 4 * D_MODEL + 4 * D_MODEL + 2 * 2 * D_MODEL)
    return _call(
        body, name="mix_proj_fwd", grid=(nt,),
        in_specs=[_row_spec(tm, D_MODEL), _full_spec((1, D_MODEL)), ANY, _row_spec(tm, ROT_HALF), _row_spec(tm, ROT_HALF)],
        out_specs=[_row_spec(tm, D_MODEL), _row_spec(tm, 4 * D_MODEL), _row_spec(tm, D_MODEL), _row_spec(tm, 2 * D_MODEL)],
        out_shape=[jax.ShapeDtypeStruct((SEQ, D_MODEL), BF16), jax.ShapeDtypeStruct((SEQ, 4 * D_MODEL), BF16),
                   jax.ShapeDtypeStruct((SEQ, D_MODEL), F32), jax.ShapeDtypeStruct((SEQ, 2 * D_MODEL), BF16)],
        scratch_shapes=[pltpu.VMEM((D_MODEL, N_SEG * D_MODEL), BF16), pltpu.SemaphoreType.DMA((N_DEV,))],
        vmem_bytes=est + 8 * tm * D_MODEL * 4, args=[h1, gain, wmix8, cos, sin], ride=ride)


def _seg_block_spec(seg, reverse=False):
    nb = SEQ // RET_BLOCK
    if reverse:
        return pl.BlockSpec((RET_BLOCK, D_MODEL), lambda i, s=seg: (nb - 1 - i, s))
    return pl.BlockSpec((RET_BLOCK, D_MODEL), lambda i, s=seg: (i, s))


def _table_specs():
    return [_full_spec((HEADS, RET_BLOCK, RET_BLOCK)), _full_spec((HEADS, RET_BLOCK, 1)),
            _full_spec((HEADS, RET_BLOCK, 1)), _full_spec((HEADS, 1, 1))]


def _head_cols(h):
    return pl.ds(h * HEAD_DIM, HEAD_DIM)


def _retention_forward(qkvg, tables, ride=None):
    nb = SEQ // RET_BLOCK

    def body(q_ref, k_ref, v_ref, gr_ref, mask_ref, qdec_ref, kdec_ref, cdec_ref, ret_ref, o_ref, state):
        @pl.when(pl.program_id(0) == 0)
        def _():
            state[...] = jnp.zeros_like(state)

        for h in range(HEADS):
            cols = _head_cols(h)
            q, k, v = q_ref[:, cols], k_ref[:, cols], v_ref[:, cols]
            scores = _dot_nt(q, k) * mask_ref[h]
            inner = _dot(scores.astype(BF16), v)
            cross = _dot((q.astype(F32) * qdec_ref[h]).astype(BF16), state[h].astype(BF16))
            ret = inner + cross
            state[h] = state[h] * cdec_ref[h] + _dot_tn((k.astype(F32) * kdec_ref[h]).astype(BF16), v)
            ret_ref[:, cols] = ret
            retn = ret * lax.rsqrt(jnp.mean(ret * ret, axis=-1, keepdims=True) + NORM_EPS)
            gr = gr_ref[:, cols].astype(F32)
            o_ref[:, cols] = (retn * (gr * _sig(gr))).astype(BF16)

    return _call(
        body, name="retention_fwd", grid=(nb,),
        in_specs=[_seg_block_spec(0), _seg_block_spec(1), _seg_block_spec(2), _seg_block_spec(3)] + _table_specs(),
        out_specs=[_row_spec(RET_BLOCK, D_MODEL)] * 2,
        out_shape=[jax.ShapeDtypeStruct((SEQ, D_MODEL), F32), jax.ShapeDtypeStruct((SEQ, D_MODEL), BF16)],
        scratch_shapes=[pltpu.VMEM((HEADS, HEAD_DIM, HEAD_DIM), F32)],
        vmem_bytes=24 * RET_BLOCK * D_MODEL * 4, args=[qkvg, qkvg, qkvg, qkvg, *tables], ride=ride)


def _retention_backward_q(qkvg, dret, tables, ride=None):
    nb = SEQ // RET_BLOCK

    def body(k_ref, v_ref, do_ref, mask_ref, qdec_ref, kdec_ref, cdec_ref, dq_ref, state):
        @pl.when(pl.program_id(0) == 0)
        def _():
            state[...] = jnp.zeros_like(state)

        for h in range(HEADS):
            cols = _head_cols(h)
            k, v, do = k_ref[:, cols], v_ref[:, cols], do_ref[:, cols]
            dscores = _dot_nt(do, v) * mask_ref[h]
            dq_ref[:, cols] = _dot(dscores.astype(BF16), k) + _dot_nt(do, state[h].astype(BF16)) * qdec_ref[h]
            state[h] = state[h] * cdec_ref[h] + _dot_tn((k.astype(F32) * kdec_ref[h]).astype(BF16), v)

    return _call(
        body, name="retention_bwd_q", grid=(nb,),
        in_specs=[_seg_block_spec(1), _seg_block_spec(2), _row_spec(RET_BLOCK, D_MODEL)] + _table_specs(),
        out_specs=[_row_spec(RET_BLOCK, D_MODEL)],
        out_shape=[jax.ShapeDtypeStruct((SEQ, D_MODEL), F32)],
        scratch_shapes=[pltpu.VMEM((HEADS, HEAD_DIM, HEAD_DIM), F32)],
        vmem_bytes=24 * RET_BLOCK * D_MODEL * 4, args=[qkvg, qkvg, dret, *tables], ride=ride)


def _retention_backward_kv(qkvg, dret, tables, ride=None):
    nb = SEQ // RET_BLOCK

    def body(q_ref, k_ref, v_ref, do_ref, mask_ref, qdec_ref, kdec_ref, cdec_ref, dk_ref, dv_ref, gstate):
        @pl.when(pl.program_id(0) == 0)
        def _():
            gstate[...] = jnp.zeros_like(gstate)

        for h in range(HEADS):
            cols = _head_cols(h)
            q, k, v, do = q_ref[:, cols], k_ref[:, cols], v_ref[:, cols], do_ref[:, cols]
            mask = mask_ref[h]
            scores = (_dot_nt(q, k) * mask).astype(BF16)
            dscores = (_dot_nt(do, v) * mask).astype(BF16)
            gs = gstate[h].astype(BF16)
            dk_ref[:, cols] = _dot_tn(dscores, q) + _dot_nt(v, gs) * kdec_ref[h]
            dv_ref[:, cols] = _dot_tn(scores, do) + _dot((k.astype(F32) * kdec_ref[h]).astype(BF16), gs)
            gstate[h] = gstate[h] * cdec_ref[h] + _dot_tn((q.astype(F32) * qdec_ref[h]).astype(BF16), do)

    rev = lambda: pl.BlockSpec((RET_BLOCK, D_MODEL), lambda i: (nb - 1 - i, 0))
    return _call(
        body, name="retention_bwd_kv", grid=(nb,),
        in_specs=[_seg_block_spec(0, True), _seg_block_spec(1, True), _seg_block_spec(2, True), rev()] + _table_specs(),
        out_specs=[rev(), rev()],
        out_shape=[jax.ShapeDtypeStruct((SEQ, D_MODEL), F32)] * 2,
        scratch_shapes=[pltpu.VMEM((HEADS, HEAD_DIM, HEAD_DIM), F32)],
        vmem_bytes=32 * RET_BLOCK * D_MODEL * 4, args=[qkvg, qkvg, qkvg, dret, *tables], ride=ride)


def _pooled(p_ext, first_row):
    rows = p_ext.shape[0]
    t = first_row + lax.broadcasted_iota(jnp.int32, (rows - HALO, 1), 0)
    outs = []
    for g, w in enumerate(POOL_WINDOWS):
        e = p_ext[:, g * POOL_GROUP_DIM:(g + 1) * POOL_GROUP_DIM]
        s, span = e, 1
        while span < w:
            s = s + pltpu.roll(s, span, 0)
            span *= 2
        count = jnp.minimum(t + 1, w).astype(F32)
        outs.append(s[HALO:] / count - e[HALO:])
    return outs


def _pooled_transpose(d_ext, first_row):
    rows = d_ext.shape[0]
    t = first_row + lax.broadcasted_iota(jnp.int32, (rows, 1), 0)
    outs = []
    for g, w in enumerate(POOL_WINDOWS):
        d = d_ext[:, g * POOL_GROUP_DIM:(g + 1) * POOL_GROUP_DIM]
        e = jnp.where(t < SEQ, d / jnp.minimum(t + 1, w).astype(F32), 0.0)
        s, span = e, 1
        while span < w:
            s = s + pltpu.roll(s, rows - span, 0)
            span *= 2
        outs.append(s[:rows - HALO] - d[:rows - HALO])
    return outs


def _mix_tail_specs(tm):
    halo_blocks = tm // HALO
    return [
        _row_spec(tm, D_MODEL),
        pl.BlockSpec((HALO, D_MODEL), lambda i: (jnp.maximum(i * halo_blocks - 1, 0), 0)),
        _row_spec(tm, 2 * D_MODEL),
        _row_spec(tm, D_MODEL),
        _full_spec((2, D_MODEL)), _full_spec((1, D_MODEL)), ANY,
        _full_spec((D_MODEL, D_MODEL)), _full_spec((D_MODEL, D_MODEL)), _full_spec((D_MODEL, D_MODEL)),
    ]


def _mix_tail_compute(i, tm, p_ref, halo_ref, gates_ref, oret_ref, bias_ref, scale_ref, pw, wru_ref, wpu_ref, saved=None):
    halo = jnp.where(i > 0, halo_ref[...], 0.0)
    pooled = _pooled(jnp.concatenate([halo, p_ref[...]], axis=0), i * tm)
    pooled = [x.astype(BF16) for x in pooled]
    mixed = jnp.concatenate([_dot(pooled[g], pw[g]) for g in range(len(POOL_WINDOWS))], axis=-1)
    pool_out = (mixed * scale_ref[...]).astype(BF16)
    o_ret = oret_ref[...]
    if saved is None:
        a = _dot(o_ret, wru_ref[...])
        b = _dot(pool_out, wpu_ref[...])
    else:
        a, b = saved[0][...].astype(F32), saved[1][...].astype(F32)
    z = gates_ref[...].astype(F32)
    g0 = _sig(z[:, :D_MODEL] + bias_ref[0:1, :])
    g1 = _sig(z[:, D_MODEL:] + bias_ref[1:2, :])
    merged = (g0 * a + g1 * b).astype(BF16)
    return pooled, mixed, pool_out, o_ret, a, b, g0, g1, merged


def _mix_tail_forward(p, gates, o_ret, h1, bias, scale, pw8, wru, wpu, wo, ride=None):
    tm, nt = TOKEN_TILE, SEQ // TOKEN_TILE

    def body(p_ref, halo_ref, gates_ref, oret_ref, bias_ref, scale_ref, pw_hbm, wru_ref, wpu_ref, wo_ref, h1_ref,
             h2_ref, a_ref, b_ref, pw, sem):
        i = pl.program_id(0)

        @pl.when(i == 0)
        def _():
            _load_pool_weight(pw_hbm, pw, sem)

        out = _mix_tail_compute(i, tm, p_ref, halo_ref, gates_ref, oret_ref, bias_ref, scale_ref, pw, wru_ref, wpu_ref)
        a_ref[...] = out[4].astype(BF16)
        b_ref[...] = out[5].astype(BF16)
        h2_ref[...] = h1_ref[...] + _dot(out[-1], wo_ref[...])

    est = 3 * 2 * 2 * D_MODEL * D_MODEL + 2 * tm * D_MODEL * (4 + 4 + 2 + 4 + 4) + 16 * tm * D_MODEL * 4
    return _call(
        body, name="mix_tail_fwd", grid=(nt,),
        in_specs=_mix_tail_specs(tm) + [_row_spec(tm, D_MODEL)],
        out_specs=[_row_spec(tm, D_MODEL)] * 3,
        out_shape=[jax.ShapeDtypeStruct((SEQ, D_MODEL), F32)] + [jax.ShapeDtypeStruct((SEQ, D_MODEL), BF16)] * 2,
        scratch_shapes=[pltpu.VMEM((len(POOL_WINDOWS), POOL_GROUP_DIM, POOL_GROUP_DIM), BF16), pltpu.SemaphoreType.DMA((N_DEV,))],
        vmem_bytes=est, args=[p, p, gates, o_ret, bias, scale, pw8, wru, wpu, wo, h1], ride=ride)


def _mix_tail_backward(dh2, p, gates, o_ret, ret, qkvg, a_saved, b_saved, bias, scale, pw8, wru, wpu, wo, ride=None):
    tm, nt = TOKEN_TILE, SEQ // TOKEN_TILE
    n_groups = len(POOL_WINDOWS)
    rows_per_dev = POOL_GROUP_DIM // N_DEV

    def body(p_ref, halo_ref, gates_ref, oret_ref, bias_ref, scale_ref, pw_hbm, wru_ref, wpu_ref, wo_ref,
             dh2_ref, ret_ref, gr_ref, a_ref, b_ref,
             dret_ref, dgr_ref, dgates_ref, dpooled_ref, dwo_ref, dwru_ref, dwpu_ref, dpw_ref, dbias_ref, dscale_ref,
             pw, sem, acc_wo, acc_wru, acc_wpu, acc_pw, send_sq, recv_sq, send_pw, recv_pw, send_sems, recv_sems):
        i = pl.program_id(0)

        @pl.when(i == 0)
        def _():
            _load_pool_weight(pw_hbm, pw, sem)
            for ref in (acc_wo, acc_wru, acc_wpu, acc_pw, dbias_ref, dscale_ref):
                ref[...] = jnp.zeros_like(ref)

        pooled, mixed, pool_out, o_ret, a, b, g0, g1, merged = _mix_tail_compute(
            i, tm, p_ref, halo_ref, gates_ref, oret_ref, bias_ref, scale_ref, pw, wru_ref, wpu_ref, saved=(a_ref, b_ref))
        dh2 = dh2_ref[...].astype(BF16)
        dm = _dot_nt(dh2, wo_ref[...])
        acc_wo[...] += _dot_tn(merged, dh2)
        da = (dm * g0).astype(BF16)
        db = (dm * g1).astype(BF16)
        dz0 = dm * a * g0 * (1.0 - g0)
        dz1 = dm * b * g1 * (1.0 - g1)
        dbias_ref[0:1, :] += jnp.sum(dz0, axis=0, keepdims=True)
        dbias_ref[1:2, :] += jnp.sum(dz1, axis=0, keepdims=True)
        dgates_ref[:, pl.ds(0, D_MODEL)] = dz0.astype(BF16)
        dgates_ref[:, pl.ds(D_MODEL, D_MODEL)] = dz1.astype(BF16)
        acc_wru[...] += _dot_tn(o_ret, da)
        acc_wpu[...] += _dot_tn(pool_out, db)
        d_oret = _dot_nt(da, wru_ref[...])
        d_pool_out = _dot_nt(db, wpu_ref[...])
        dscale_ref[...] += jnp.sum(d_pool_out * mixed, axis=0, keepdims=True)
        dmixed = (d_pool_out * scale_ref[...]).astype(BF16)
        for g in range(n_groups):
            dmg = dmixed[:, g * POOL_GROUP_DIM:(g + 1) * POOL_GROUP_DIM]
            acc_pw[g] += _dot_tn(pooled[g], dmg)
            dpooled_ref[:, pl.ds(g * POOL_GROUP_DIM, POOL_GROUP_DIM)] = _dot_nt(dmg, pw[g])
        gr = gr_ref[...].astype(F32)
        s = _sig(gr)
        silu = gr * s
        for hd in range(HEADS):
            cols = slice(hd * HEAD_DIM, (hd + 1) * HEAD_DIM)
            r_h = ret_ref[:, cols]
            rr = lax.rsqrt(jnp.mean(r_h * r_h, axis=-1, keepdims=True) + NORM_EPS)
            rhat = r_h * rr
            do_h = d_oret[:, cols]
            dgr_ref[:, cols] = (do_h * rhat * (s[:, cols] * (1.0 + gr[:, cols] * (1.0 - s[:, cols])))).astype(BF16)
            dret_ref[:, cols] = _rms_bwd(do_h * silu[:, cols], rhat, rr).astype(BF16)

        @pl.when(i == nt - 1)
        def _():
            c = lax.axis_index("c")
            rows = D_MODEL // N_DEV
            squares = ((acc_wo, dwo_ref), (acc_wru, dwru_ref), (acc_wpu, dwpu_ref))
            for q in range(n_chips):
                own = pl.multiple_of((2 * q + c) * rows, rows)
                other = pl.multiple_of((2 * q + 1 - c) * rows, rows)
                for t, (acc, out) in enumerate(squares):
                    out[q] = acc[pl.ds(own, rows), :].astype(BF16)
                    send_sq[t, q] = acc[pl.ds(other, rows), :].astype(BF16)
                own_pw = pl.multiple_of((2 * q + c) * rows_per_dev, rows_per_dev)
                other_pw = pl.multiple_of((2 * q + 1 - c) * rows_per_dev, rows_per_dev)
                dpw_ref[q] = acc_pw[:, pl.ds(own_pw, rows_per_dev), :].astype(BF16)
                send_pw[q] = acc_pw[:, pl.ds(other_pw, rows_per_dev), :].astype(BF16)
            pushes = [_to_sibling(send_sq, recv_sq, send_sems.at[0], recv_sems.at[0]),
                      _to_sibling(send_pw, recv_pw, send_sems.at[1], recv_sems.at[1])]
            for cp in pushes:
                cp.start()
            for cp in pushes:
                cp.wait_recv()
            for t, (acc, out) in enumerate(squares):
                out[...] = (out[...].astype(F32) + recv_sq[t].astype(F32)).astype(BF16)
            dpw_ref[...] = (dpw_ref[...].astype(F32) + recv_pw[...].astype(F32)).astype(BF16)
            for cp in pushes:
                cp.wait_send()

    n_chips = N_DEV // 2
    sq = (n_chips, D_MODEL // N_DEV, D_MODEL)
    pw_shape = (n_chips, n_groups, rows_per_dev, POOL_GROUP_DIM)
    est = (3 * 2 * 2 * D_MODEL * D_MODEL + 3 * 4 * D_MODEL * D_MODEL + 3 * 2 * 2 * D_MODEL * D_MODEL
           + 2 * tm * D_MODEL * (4 + 4 + 2 + 4 + 4 + 2 + 2 + 2 + 4 + 4) + 24 * tm * D_MODEL * 4)
    return _call(
        body, name="mix_tail_bwd", grid=(nt,),
        in_specs=_mix_tail_specs(tm) + [_row_spec(tm, D_MODEL), _row_spec(tm, D_MODEL), _row_spec(tm, D_MODEL, 3),
                                        _row_spec(tm, D_MODEL), _row_spec(tm, D_MODEL)],
        out_specs=[_row_spec(tm, D_MODEL), _row_spec(tm, D_MODEL), _row_spec(tm, 2 * D_MODEL), _row_spec(tm, D_MODEL),
                   _full_spec(sq), _full_spec(sq), _full_spec(sq), _full_spec(pw_shape),
                   _full_spec((2, D_MODEL)), _full_spec((1, D_MODEL))],
        out_shape=[jax.ShapeDtypeStruct((SEQ, D_MODEL), BF16), jax.ShapeDtypeStruct((SEQ, D_MODEL), BF16),
                   jax.ShapeDtypeStruct((SEQ, 2 * D_MODEL), BF16), jax.ShapeDtypeStruct((SEQ, D_MODEL), F32),
                   jax.ShapeDtypeStruct(sq, BF16), jax.ShapeDtypeStruct(sq, BF16), jax.ShapeDtypeStruct(sq, BF16),
                   jax.ShapeDtypeStruct(pw_shape, BF16),
                   jax.ShapeDtypeStruct((2, D_MODEL), F32), jax.ShapeDtypeStruct((1, D_MODEL), F32)],
        scratch_shapes=[pltpu.VMEM((n_groups, POOL_GROUP_DIM, POOL_GROUP_DIM), BF16), pltpu.SemaphoreType.DMA((N_DEV,)),
                        pltpu.VMEM((D_MODEL, D_MODEL), F32), pltpu.VMEM((D_MODEL, D_MODEL), F32),
                        pltpu.VMEM((D_MODEL, D_MODEL), F32), pltpu.VMEM((n_groups, POOL_GROUP_DIM, POOL_GROUP_DIM), F32),
                        pltpu.VMEM((3,) + sq, BF16), pltpu.VMEM((3,) + sq, BF16), pltpu.VMEM(pw_shape, BF16),
                        pltpu.VMEM(pw_shape, BF16), pltpu.SemaphoreType.DMA((2,)), pltpu.SemaphoreType.DMA((2,))],
        vmem_bytes=est, args=[p, p, gates, o_ret, bias, scale, pw8, wru, wpu, wo, dh2, ret, qkvg, a_saved, b_saved], ride=ride)


def _mix_proj_backward(dq, dk, dv, dgr, dpooled, dgates, cos, sin, h1, gain, dh2, wmix8, ride=None):
    tm, nt = TOKEN_TILE, SEQ // TOKEN_TILE
    halo_blocks = tm // HALO
    last_halo = SEQ // HALO - 1
    k_scale = HEAD_DIM ** -0.5

    def body(dq_ref, dk_ref, dv_ref, dgr_ref, dpool_ref, dhalo_ref, dgates_ref, cos_ref, sin_ref, h1_ref, g_ref,
             dh2_ref, wmix_hbm, dh1_ref, dproj_ref, dg_ref, wmix, sem):
        i = pl.program_id(0)

        @pl.when(i == 0)
        def _():
            _load_mix_weight(wmix_hbm, wmix, sem)
            dg_ref[...] = jnp.zeros_like(dg_ref)

        cos_t, sin_t = cos_ref[...], sin_ref[...]
        for seg, ref, scale in ((0, dq_ref, 1.0), (1, dk_ref, k_scale)):
            for hd in range(HEADS):
                lo = hd * HEAD_DIM
                d1, d2 = ref[:, lo:lo + ROT_HALF], ref[:, lo + ROT_HALF:lo + HEAD_DIM]
                dproj_ref[:, pl.ds(seg * D_MODEL + lo, ROT_HALF)] = ((d1 * cos_t + d2 * sin_t) * scale).astype(BF16)
                dproj_ref[:, pl.ds(seg * D_MODEL + lo + ROT_HALF, ROT_HALF)] = ((d2 * cos_t - d1 * sin_t) * scale).astype(BF16)
        dproj_ref[:, pl.ds(2 * D_MODEL, D_MODEL)] = dv_ref[...].astype(BF16)
        dproj_ref[:, pl.ds(3 * D_MODEL, D_MODEL)] = dgr_ref[...]
        dp = _pooled_transpose(jnp.concatenate([dpool_ref[...], dhalo_ref[...]], axis=0), i * tm)
        for g in range(len(POOL_WINDOWS)):
            dproj_ref[:, pl.ds(4 * D_MODEL + g * POOL_GROUP_DIM, POOL_GROUP_DIM)] = dp[g].astype(BF16)
        dproj_ref[:, pl.ds(5 * D_MODEL, 2 * D_MODEL)] = dgates_ref[...]
        du = jnp.zeros((tm, D_MODEL), F32)
        for seg in range(N_SEG):
            cols = pl.ds(seg * D_MODEL, D_MODEL)
            du = du + _dot_nt(dproj_ref[:, cols], wmix[:, cols])
        g = g_ref[...]
        _, xhat, r = _rms(h1_ref[...], g)
        dg_ref[...] += jnp.sum(du * xhat, axis=0, keepdims=True)
        dh1_ref[...] = dh2_ref[...] + _rms_bwd(du * g, xhat, r)

    est = 2 * D_MODEL * N_SEG * D_MODEL + 2 * tm * D_MODEL * (3 * 4 + 2 + 4 + 4 + 4 + 4 + 4 + 14) + 12 * tm * D_MODEL * 4
    return _call(
        body, name="mix_proj_bwd", grid=(nt,),
        in_specs=[_row_spec(tm, D_MODEL), _row_spec(tm, D_MODEL), _row_spec(tm, D_MODEL), _row_spec(tm, D_MODEL),
                  _row_spec(tm, D_MODEL),
                  pl.BlockSpec((HALO, D_MODEL), lambda i: (jnp.minimum((i + 1) * halo_blocks, last_halo), 0)),
                  _row_spec(tm, 2 * D_MODEL), _row_spec(tm, ROT_HALF), _row_spec(tm, ROT_HALF),
                  _row_spec(tm, D_MODEL), _full_spec((1, D_MODEL)), _row_spec(tm, D_MODEL), ANY],
        out_specs=[_row_spec(tm, D_MODEL), _row_spec(tm, N_SEG * D_MODEL), _full_spec((1, D_MODEL))],
        out_shape=[jax.ShapeDtypeStruct((SEQ, D_MODEL), F32), jax.ShapeDtypeStruct((SEQ, N_SEG * D_MODEL), BF16),
                   jax.ShapeDtypeStruct((1, D_MODEL), F32)],
        scratch_shapes=[pltpu.VMEM((D_MODEL, N_SEG * D_MODEL), BF16), pltpu.SemaphoreType.DMA((N_DEV,))],
        vmem_bytes=est, args=[dq, dk, dv, dgr, dpooled, dpooled, dgates, cos, sin, h1, gain, dh2, wmix8], ride=ride)


def _adamw(w, parts, m, v, name, after=None):
    rows, cols = w.shape
    n_lists = len(parts)
    tr = max([t for t in range(16, 257, 16) if rows % t == 0], default=rows)
    c1 = 1.0 - ADAM_B1 ** ADAM_STEP
    c2 = 1.0 - ADAM_B2 ** ADAM_STEP

    def body(*refs):
        w_ref, m_ref, v_ref = refs[:3]
        part_refs = refs[3:3 + n_lists]
        g_out, d_out, m_out, v_out = refs[-4:]
        g = None
        for p_ref in part_refs:
            for k in range(p_ref.shape[0]):
                term = p_ref[k].astype(F32)
                g = term if g is None else g + term
        m_new = ADAM_B1 * m_ref[...] + (1.0 - ADAM_B1) * g
        v_new = ADAM_B2 * v_ref[...] + (1.0 - ADAM_B2) * (g * g)
        g_out[...] = g
        m_out[...] = m_new
        v_out[...] = v_new
        d_out[...] = -ADAM_LR * ((m_new / c1) / (jnp.sqrt(v_new / c2) + ADAM_EPS) + ADAM_WD * w_ref[...])

    spec = pl.BlockSpec((tr, cols), lambda i: (i, 0))
    out = jax.ShapeDtypeStruct((rows, cols), F32)
    part_specs = [pl.BlockSpec((p.shape[0], tr, cols), lambda i: (0, i, 0)) for p in parts]
    part_bytes = sum(p.shape[0] * p.dtype.itemsize for p in parts)
    extra = [] if after is None else [after]
    return pl.pallas_call(
        body, name=name, grid=(rows // tr,),
        in_specs=[spec, spec, spec] + part_specs + [ANY] * len(extra),
        out_specs=[spec] * 4, out_shape=[out] * 4,
        compiler_params=_params(2 * tr * cols * (7 * 4 + part_bytes) + 8 * tr * cols * 4, 1),
    )(_in_hbm(w), _in_hbm(m), _in_hbm(v), *[_in_hbm(p) for p in parts], *extra)


def _mix_w_in_grad(u, dproj, ride=None):
    return _weight_grad(
        u, dproj, N_DEV,
        lambda tt: pl.BlockSpec((tt, D_MODEL), lambda b, t: (t, 0)),
        lambda tt: pl.BlockSpec((tt, MIX_SHARD), lambda b, t: (t, b)),
        D_MODEL, MIX_SHARD, name="w_in_grad", ride=ride)


def kernel(x, norm_ffn1, ffn1_w_in, ffn1_w_out, norm_mix, w_in, gate_bias, pool_w, pool_scale, w_ret_up, w_pool_up, w_out, norm_ffn2, ffn2_w_in, ffn2_w_out, norm_final, loss_target, m_norm_ffn1, m_ffn1_w_in, m_ffn1_w_out, m_norm_mix, m_w_in, m_gate_bias, m_pool_w, m_pool_scale, m_w_ret_up, m_w_pool_up, m_w_out, m_norm_ffn2, m_ffn2_w_in, m_ffn2_w_out, m_norm_final, v_norm_ffn1, v_ffn1_w_in, v_ffn1_w_out, v_norm_mix, v_w_in, v_gate_bias, v_pool_w, v_pool_scale, v_w_ret_up, v_w_pool_up, v_w_out, v_norm_ffn2, v_ffn2_w_in, v_ffn2_w_out, v_norm_final):
    assert x.shape == (1, SEQ, D_MODEL) and ffn1_w_in.shape == (1, D_MODEL, FF_SHARD) and w_in.shape == (1, D_MODEL, MIX_SHARD)
    x2, target = x[0], loss_target[0]

    cos, sin = _rotary_tables()
    tables = _retention_tables()
    bf = lambda w: w[0].astype(BF16)
    bf_t = lambda w: jnp.swapaxes(w[0], 0, 1).astype(BF16)
    square = lambda w: w.reshape(D_MODEL, D_MODEL)

    win1, wout1, bias8 = _alone(_GatherRide([bf_t(ffn1_w_in), bf(ffn1_w_out), gate_bias[0]]), "ffn1_weights_all_gather")
    wout1 = wout1.reshape(N_FF_GROUPS, FF_SHARD, D_MODEL)
    bias = bias8.transpose(1, 0, 2).reshape(2, D_MODEL)

    (h1, gu1), (wmix8,) = _ffn_forward(x2, norm_ffn1, win1, wout1, "ffn1_fwd", ride=_GatherRide([bf(w_in)]))
    (u, qkvg, p, gates), (win2,) = _mix_proj_forward(h1, norm_mix, wmix8, cos, sin, ride=_GatherRide([bf_t(ffn2_w_in)]))
    (ret, o_ret), (pw8, wru, wpu, wo) = _retention_forward(
        qkvg, tables, ride=_GatherRide([bf(pool_w), bf(w_ret_up), bf(w_pool_up), bf(w_out)]))
    wru, wpu, wo = square(wru), square(wpu), square(wo)
    (h2, a_saved, b_saved), (wout2,) = _mix_tail_forward(p, gates, o_ret, h1, bias, pool_scale, pw8, wru, wpu, wo,
                                        ride=_GatherRide([bf(ffn2_w_out)]))
    wout2 = wout2.reshape(N_FF_GROUPS, FF_SHARD, D_MODEL)
    (dh3, gu2, loss_part, d_norm_final), _ = _ffn_forward(h2, norm_ffn2, win2, wout2, "ffn2_fwd_loss",
                                                          head=(target, norm_final.reshape(1, D_MODEL)))

    dh2, dgu2, act2, xn2, df2, d_norm_ffn2 = _ffn_backward(dh3, h2, norm_ffn2, gu2, win2, wout2, "ffn2_bwd")
    d_wout2, _ = _ffn_w_out_grad(act2, df2, 2)
    d_win2, (r_wout2,) = _ffn_w_in_grad(xn2, dgu2, 2, ride=_ScatterRide([d_wout2]))
    (dret, dgr, dgates, dpooled, d_wo, d_wru, d_wpu, d_pw, d_bias, d_scale), (r_win2,) = _mix_tail_backward(
        dh2, p, gates, o_ret, ret, qkvg, a_saved, b_saved, bias, pool_scale, pw8, wru, wpu, wo, ride=_ScatterRide([d_win2]))
    (dq,), _ = _retention_backward_q(qkvg, dret, tables)
    (dk, dv), (r_pw, r_wru, r_wpu, r_wo) = _retention_backward_kv(
        qkvg, dret, tables, ride=_ScatterRide([d_pw, d_wru, d_wpu, d_wo]))
    (dh1, dproj, d_norm_mix), _ = _mix_proj_backward(dq, dk, dv, dgr, dpooled, dgates, cos, sin, h1, norm_mix, dh2, wmix8)
    d_wmix, _ = _mix_w_in_grad(u, dproj)
    wmix_state, wmix_started = _scatter_start(d_wmix, "w_in_grad_exchange_start")
    grad_x, dgu1, act1, xn1, df1, d_norm_ffn1 = _ffn_backward(dh1, x2, norm_ffn1, gu1, win1, wout1, "ffn1_bwd", after=wmix_started)
    small_rows = jnp.concatenate(
        [d_norm_ffn1, d_norm_mix, d_scale, d_norm_ffn2, d_norm_final, d_bias, jnp.tile(loss_part, (1, D_MODEL // 128))],
        axis=0)
    d_win1, (small_all,) = _ffn_w_in_grad(xn1, dgu1, 1, ride=_GatherRide([small_rows]))
    win1_state, win1_started = _scatter_start(d_win1, "ffn1_w_in_grad_exchange_start")
    d_wout1, _ = _ffn_w_out_grad(act1, df1, 1, after=win1_started)
    wout1_state, started = _scatter_start(d_wout1, "ffn1_w_out_grad_exchange_start")
    zero_row = jnp.zeros((1, D_MODEL), F32)

    results = {}

    def update(nm, w, parts, m, v, after):
        if nm in ("ffn1_w_in", "ffn2_w_in"):
            flat, back = (lambda a: jnp.swapaxes(a[0], 0, 1)), (lambda o: jnp.swapaxes(o, 0, 1)[None])
        else:
            flat, back = (lambda a: a.reshape(-1, w.shape[-1])), (lambda o: o.reshape(w.shape))
        parts = [p.reshape(p.shape[:1] + flat(w).shape) for p in parts]
        outs = _adamw(flat(w), parts, flat(m), flat(v), name=f"adamw_{nm}", after=after)
        results[nm] = [back(o) for o in outs]
        return outs[0]

    done = update("w_in", w_in, _scatter_wait(wmix_state, started, "w_in_grad_exchange_wait"), m_w_in, v_w_in, None)
    for nm, w, parts, m, v in (
            ("ffn2_w_in", ffn2_w_in, r_win2, m_ffn2_w_in, v_ffn2_w_in),
            ("ffn2_w_out", ffn2_w_out, r_wout2, m_ffn2_w_out, v_ffn2_w_out), ("w_ret_up", w_ret_up, r_wru, m_w_ret_up, v_w_ret_up),
            ("w_pool_up", w_pool_up, r_wpu, m_w_pool_up, v_w_pool_up), ("w_out", w_out, r_wo, m_w_out, v_w_out),
            ("pool_w", pool_w, r_pw, m_pool_w, v_pool_w)):
        done = update(nm, w, [parts], m, v, done)
    done = update("ffn1_w_in", ffn1_w_in, _scatter_wait(win1_state, done, "ffn1_w_in_grad_exchange_wait"),
                  m_ffn1_w_in, v_ffn1_w_in, None)
    update("ffn1_w_out", ffn1_w_out, _scatter_wait(wout1_state, done, "ffn1_w_out_grad_exchange_wait"),
           m_ffn1_w_out, v_ffn1_w_out, None)

    my_id = _linear_id(*_my_position())
    bias_cols = gate_bias.shape[-1]
    pad = lambda a: jnp.pad(a[0], ((0, 0), (0, D_MODEL - bias_cols)))
    pack = lambda a, b, c, d, e, gb: jnp.concatenate([a, b, c, d, e.reshape(1, D_MODEL), pad(gb), zero_row], axis=0)
    d_bias_mine = lax.dynamic_slice_in_dim(small_all[:, 5:7], my_id * bias_cols, bias_cols, axis=2)
    g_small = jnp.concatenate([small_all[:, 0:5], jnp.pad(d_bias_mine, ((0, 0), (0, 0), (0, D_MODEL - bias_cols))),
                               small_all[:, 7:8]], axis=1)
    s_outs = _adamw(pack(norm_ffn1, norm_mix, pool_scale, norm_ffn2, norm_final, gate_bias), [g_small],
                    pack(m_norm_ffn1, m_norm_mix, m_pool_scale, m_norm_ffn2, m_norm_final, m_gate_bias),
                    pack(v_norm_ffn1, v_norm_mix, v_pool_scale, v_norm_ffn2, v_norm_final, v_gate_bias), name="adamw_small")
    loss = s_outs[0][7, 0]
    for row, nm in enumerate(["norm_ffn1", "norm_mix", "pool_scale", "norm_ffn2"]):
        results[nm] = [o[row:row + 1] for o in s_outs]
    results["norm_final"] = [o[4] for o in s_outs]
    results["gate_bias"] = [o[5:7, :bias_cols][None] for o in s_outs]

    order = ["norm_ffn1", "ffn1_w_in", "ffn1_w_out", "norm_mix", "w_in", "gate_bias", "pool_w", "pool_scale",
             "w_ret_up", "w_pool_up", "w_out", "norm_ffn2", "ffn2_w_in", "ffn2_w_out", "norm_final"]
    return (loss, grad_x[None], *[results[nm][0] for nm in order], *[results[nm][1] for nm in order],
            *[results[nm][2] for nm in order], *[results[nm][3] for nm in order])
```

```python
import numpy as np
import jax
import jax.numpy as jnp
from jax import lax
from jax.experimental import pallas as pl
from jax.experimental.pallas import tpu as pltpu

F32 = jnp.float32
BF16 = jnp.bfloat16

N_DEV = 8
D_MODEL = 1024
SEQ = 4096
D_FF = 2816
FF_SHARD = 2 * D_FF // N_DEV
N_FF_GROUPS = N_DEV // 2
HEADS = 4
HEAD_DIM = 256
ROT_HALF = HEAD_DIM // 2
CHUNK = 64
RET_BLOCK = 256
POOL_WINDOWS = (2, 4, 8, 16)
POOL_GROUP_DIM = 256
HALO = 16
MIX_SHARD = 7 * D_MODEL // N_DEV
N_SEG = 7
ROPE_BASE = 10000.0
NORM_EPS = 1e-6
FFN_RES_WEIGHT = 0.5
ADAM_LR, ADAM_B1, ADAM_B2, ADAM_EPS, ADAM_WD, ADAM_STEP = 0.001, 0.9, 0.999, 1e-08, 0.01, 10

TOKEN_TILE = 256
WIDE_TILE = 512
VMEM_CAP_V7X = 64 * 1024 * 1024
MESH = pl.DeviceIdType.MESH
ANY = pl.BlockSpec(memory_space=pl.ANY)


def _vmem_limit(estimate_bytes):
    return int(min(estimate_bytes * 5 // 4 + (6 << 20), VMEM_CAP_V7X - (4 << 20)))


def _params(estimate_bytes, n_grid):
    return pltpu.CompilerParams(dimension_semantics=("arbitrary",) * n_grid,
                                vmem_limit_bytes=_vmem_limit(estimate_bytes))


def _dot(a, b):
    return jnp.dot(a, b, preferred_element_type=F32)


def _dot_nt(a, b):
    return lax.dot_general(a, b, (((1,), (1,)), ((), ())), preferred_element_type=F32)


def _dot_tn(a, b):
    return lax.dot_general(a, b, (((0,), (0,)), ((), ())), preferred_element_type=F32)


def _sig(x):
    return 1.0 / (1.0 + jnp.exp(-x))


def _rms(x, g):
    r = lax.rsqrt(jnp.mean(x * x, axis=-1, keepdims=True) + NORM_EPS)
    xhat = x * r
    return xhat * g, xhat, r


def _rms_bwd(dyg, xhat, r):
    return r * (dyg - xhat * jnp.mean(dyg * xhat, axis=-1, keepdims=True))


def _row_spec(tile, width, col=0):
    return pl.BlockSpec((tile, width), lambda i, c=col: (i, c))


def _full_spec(shape):
    return pl.BlockSpec(shape, lambda *_: (0,) * len(shape))


def _rotary_tables():
    inv_freq = (np.float32(ROPE_BASE) ** (-np.arange(ROT_HALF, dtype=np.float32) / np.float32(ROT_HALF))).astype(np.float32)
    ang = (np.arange(SEQ, dtype=np.float32)[:, None] * inv_freq[None, :]).astype(np.float32)
    return jnp.asarray(np.cos(ang.astype(np.float64)), F32), jnp.asarray(np.sin(ang.astype(np.float64)), F32)


def _retention_tables():
    log_gamma = np.log(1.0 - 2.0 ** (-5.0 - np.arange(HEADS, dtype=np.float64)))
    n = np.arange(RET_BLOCK)
    diff = (n[:, None] - n[None, :]).astype(np.float64)
    same = (n[:, None] // CHUNK) == (n[None, :] // CHUNK)
    earlier = (n[None, :] // CHUNK) < (n[:, None] // CHUNK)
    expo = np.where(same, np.abs(diff), diff)
    mask = np.where(same | earlier, np.exp(log_gamma[:, None, None] * expo[None]), 0.0)
    qdec = np.exp(log_gamma[:, None] * (n[None, :] + 1.0))[:, :, None]
    kdec = np.exp(log_gamma[:, None] * (RET_BLOCK - 1.0 - n[None, :]))[:, :, None]
    cdec = np.exp(log_gamma * RET_BLOCK)[:, None, None]
    return (jnp.asarray(mask, F32), jnp.asarray(qdec, F32), jnp.asarray(kdec, F32), jnp.asarray(cdec, F32))


def _my_position():
    return lax.axis_index("x"), lax.axis_index("y"), lax.axis_index("c")


def _linear_id(px, py, pc):
    return 4 * px + 2 * py + pc


def _when(pred, fn):
    if isinstance(pred, bool):
        if pred:
            fn()
    else:
        pl.when(pred)(fn)


class _GatherRide:
    def __init__(self, shards):
        self.args = list(shards)
        n = self.n = len(shards)
        self.out_shape = [pltpu.HBM((N_DEV,) + s.shape, s.dtype) for s in shards]
        self.scratch = [pltpu.SemaphoreType.DMA((n, 7)), pltpu.SemaphoreType.DMA((n, 7)), pltpu.SemaphoreType.DMA((n,))]

    def _plan(self, src, out, sems):
        send_sems, recv_sems, local_sem = sems
        x, y, c = _my_position()
        me, sibling = (x, y, c), (x, y, 1 - c)
        chips = [(1 - x, y), (x, 1 - y), (1 - x, 1 - y)]

        def copy(t, k, block, to, from_src=False):
            rows = out[t].at[_linear_id(*block)]
            return pltpu.make_async_remote_copy(
                src_ref=src[t] if from_src else rows, dst_ref=rows,
                send_sem=send_sems.at[t, k], recv_sem=recv_sems.at[t, k],
                device_id=to, device_id_type=MESH)

        def relay(t):
            return copy(t, 3, (x ^ (1 - c), y ^ c, c), (x ^ c, y ^ (1 - c), c))

        local = [pltpu.make_async_copy(src[t], out[t].at[_linear_id(*me)], local_sem.at[t]) for t in range(self.n)]
        return copy, relay, local, me, sibling, chips, c

    def begin(self, first, src, out, sems):
        copy, relay, local, me, sibling, chips, c = self._plan(src, out, sems)

        def start():
            for cp in local:
                cp.start()
            for t in range(self.n):
                copy(t, 0, me, sibling, from_src=True).start()
                for j in range(2):
                    copy(t, 1 + j, me, (*chips[j], c), from_src=True).start()

        _when(first, start)

    def finish(self, mid, late, last, src, out, sems):
        copy, relay, local, me, sibling, chips, c = self._plan(src, out, sems)

        def pass_on():
            for t in range(self.n):
                for j in range(2):
                    copy(t, 1 + j, (*chips[j], c), me).wait_recv()
                relay(t).start()
                for j in range(2):
                    copy(t, 4 + j, (*chips[j], c), sibling).start()

        def pass_on_relayed():
            for t in range(self.n):
                copy(t, 3, (*chips[2], c), me).wait_recv()
                copy(t, 6, (*chips[2], c), sibling).start()

        def drain():
            for t in range(self.n):
                copy(t, 0, sibling, me).wait_recv()
                for j in range(3):
                    copy(t, 4 + j, (*chips[j], 1 - c), me).wait_recv()
            for t in range(self.n):
                copy(t, 0, me, sibling, from_src=True).wait_send()
                for j in range(2):
                    copy(t, 1 + j, me, (*chips[j], c), from_src=True).wait_send()
                relay(t).wait_send()
                for j in range(3):
                    copy(t, 4 + j, (*chips[j], c), sibling).wait_send()
            for cp in local:
                cp.wait()

        _when(mid, pass_on)
        _when(late, pass_on_relayed)
        _when(last, drain)


class _ScatterRide:
    def __init__(self, chip_sums):
        self.args = list(chip_sums)
        n = self.n = len(chip_sums)
        self.out_shape = [pltpu.HBM(p.shape, p.dtype) for p in chip_sums]
        self.scratch = [pltpu.SemaphoreType.DMA((n, 3)), pltpu.SemaphoreType.DMA((n, 3)), pltpu.SemaphoreType.DMA((n,))]

    def _plan(self, src, out, sems):
        send_sems, recv_sems, local_sem = sems
        x, y, c = _my_position()

        def peer(k):
            return (x ^ (k >> 1), y ^ (k & 1))

        copies = [pltpu.make_async_remote_copy(
            src_ref=src[t].at[2 * peer(k)[0] + peer(k)[1]], dst_ref=out[t].at[k],
            send_sem=send_sems.at[t, k - 1], recv_sem=recv_sems.at[t, k - 1],
            device_id=(*peer(k), c), device_id_type=MESH) for t in range(self.n) for k in range(1, N_DEV // 2)]
        local = [pltpu.make_async_copy(src[t].at[2 * x + y], out[t].at[0], local_sem.at[t]) for t in range(self.n)]
        return copies, local

    def begin(self, first, src, out, sems):
        copies, local = self._plan(src, out, sems)

        def start():
            for cp in local + copies:
                cp.start()

        _when(first, start)

    def finish(self, mid, late, last, src, out, sems):
        copies, local = self._plan(src, out, sems)

        def drain():
            for cp in copies:
                cp.wait_recv()
            for cp in copies:
                cp.wait_send()
            for cp in local:
                cp.wait()

        _when(last, drain)


def _in_hbm(a):
    return pltpu.with_memory_space_constraint(a, pltpu.HBM)


def _call(body, *, name, grid, in_specs, out_specs, out_shape, scratch_shapes, vmem_bytes, args, ride=None, after=None):
    n_in, n_out, n_s = len(in_specs), len(out_specs), len(scratch_shapes)
    params = _params(vmem_bytes, len(grid))
    args = [_in_hbm(a) for a in args]
    out_shape = [pltpu.HBM(s.shape, s.dtype) for s in out_shape]
    if ride is None:
        if after is not None:
            def ordered_body(*refs):
                body(*refs[:n_in], *refs[n_in + 1:])
            outs = pl.pallas_call(ordered_body, name=name, grid=grid, in_specs=list(in_specs) + [ANY], out_specs=out_specs,
                                  out_shape=out_shape, scratch_shapes=scratch_shapes, compiler_params=params)(*args, after)
            return list(outs), []
        outs = pl.pallas_call(body, name=name, grid=grid, in_specs=in_specs, out_specs=out_specs, out_shape=out_shape,
                              scratch_shapes=scratch_shapes, compiler_params=params)(*args)
        return list(outs), []
    total = int(np.prod(grid))

    def riding_body(*refs):
        a = n_in
        b = a + ride.n
        c = b + n_out
        d = c + ride.n
        e = d + n_s
        step = pl.program_id(0)
        for axis in range(1, len(grid)):
            step = step * grid[axis] + pl.program_id(axis)
        ride.begin(step == 0, refs[a:b], refs[c:d], refs[e:])
        body(*refs[:a], *refs[b:c], *refs[d:e])
        ride.finish(step == total // 2, step == (3 * total) // 4, step == total - 1, refs[a:b], refs[c:d], refs[e:])

    outs = pl.pallas_call(
        riding_body, name=name, grid=grid, in_specs=list(in_specs) + [ANY] * ride.n,
        out_specs=list(out_specs) + [ANY] * ride.n, out_shape=list(out_shape) + ride.out_shape,
        scratch_shapes=list(scratch_shapes) + ride.scratch, compiler_params=params)(*args, *[_in_hbm(a) for a in ride.args])
    return list(outs[:n_out]), list(outs[n_out:])


def _alone(ride, name):
    def body(*refs):
        src, out, sems = refs[:ride.n], refs[ride.n:2 * ride.n], refs[2 * ride.n:]
        ride.begin(True, src, out, sems)
        ride.finish(True, True, True, src, out, sems)

    return list(pl.pallas_call(body, name=name, out_shape=ride.out_shape, in_specs=[ANY] * ride.n,
                               out_specs=[ANY] * ride.n, scratch_shapes=ride.scratch)(*[_in_hbm(a) for a in ride.args]))


def _scatter_copies(src, land, send_sems, recv_sems):
    x, y, c = _my_position()
    copies = []
    for k in range(1, N_DEV // 2):
        px, py = x ^ (k >> 1), y ^ (k & 1)
        copies.append(pltpu.make_async_remote_copy(
            src_ref=src.at[2 * px + py], dst_ref=land.at[k - 1], send_sem=send_sems.at[k - 1], recv_sem=recv_sems.at[k - 1],
            device_id=(px, py, c), device_id_type=MESH))
    return copies


def _scatter_start(chip_sums, name):
    n_peers = N_DEV // 2 - 1
    land_shape = (n_peers,) + chip_sums.shape[1:]
    hbm = pl.BlockSpec(memory_space=pltpu.HBM)
    sem = pl.BlockSpec(memory_space=pltpu.SEMAPHORE)

    def body(src_ref, land_ref, send_sems, recv_sems, src_thru, land_thru, token):
        for cp in _scatter_copies(src_ref, land_ref, send_sems, recv_sems):
            cp.start()
        token[...] = jnp.zeros_like(token)

    send_sems, recv_sems, src_thru, land_thru, token = pl.pallas_call(
        body, name=name,
        out_shape=(pltpu.SemaphoreType.DMA((n_peers,)), pltpu.SemaphoreType.DMA((n_peers,)),
                   pltpu.HBM(chip_sums.shape, chip_sums.dtype), pltpu.HBM(land_shape, chip_sums.dtype),
                   jax.ShapeDtypeStruct((8, 128), F32)),
        in_specs=(hbm, hbm), out_specs=(sem, sem, hbm, hbm, pl.BlockSpec(memory_space=pltpu.VMEM)),
        input_output_aliases={0: 2, 1: 3},
        compiler_params=pltpu.CompilerParams(has_side_effects=pltpu.SideEffectType.DATAFLOW_SIDE_EFFECTING),
    )(_in_hbm(chip_sums), _in_hbm(lax.empty(land_shape, chip_sums.dtype)))
    return (send_sems, recv_sems, src_thru, land_thru), token


def _scatter_wait(state, after, name):
    send_sems, recv_sems, src_thru, land_thru = state
    hbm = pl.BlockSpec(memory_space=pltpu.HBM)
    sem = pl.BlockSpec(memory_space=pltpu.SEMAPHORE)

    def body(src_ref, land_ref, send_sems, recv_sems, after_ref, src_out, land_out):
        for cp in _scatter_copies(src_ref, land_ref, send_sems, recv_sems):
            cp.wait_send()
            cp.wait_recv()

    src_done, land_done = pl.pallas_call(
        body, name=name,
        out_shape=(pltpu.HBM(src_thru.shape, src_thru.dtype), pltpu.HBM(land_thru.shape, land_thru.dtype)),
        in_specs=(hbm, hbm, sem, sem, ANY), out_specs=(hbm, hbm), input_output_aliases={0: 0, 1: 1},
        compiler_params=pltpu.CompilerParams(has_side_effects=pltpu.SideEffectType.DATAFLOW_SIDE_EFFECTING),
    )(src_thru, land_thru, send_sems, recv_sems, after)
    x, y, _ = _my_position()
    return lax.dynamic_slice_in_dim(src_done, 2 * x + y, 1, axis=0), land_done


def _load_ffn_weights(win_hbm, wout_hbm, win, wout, sem):
    a = pltpu.make_async_copy(win_hbm, win, sem.at[0])
    b = pltpu.make_async_copy(wout_hbm, wout, sem.at[1])
    a.start()
    b.start()
    a.wait()
    b.wait()


def _ffn_forward(h_in, gain, win8, wout, name, head=None, ride=None):
    tm, nt = WIDE_TILE, SEQ // WIDE_TILE

    def body(*refs):
        if head is None:
            x_ref, g_ref, win_hbm, wout_hbm, out_ref, gu_ref, win, wout, sem = refs
        else:
            x_ref, g_ref, win_hbm, wout_hbm, tgt_ref, gf_ref, out_ref, gu_ref, loss_ref, dgf_ref, win, wout, sem = refs
        i = pl.program_id(0)

        @pl.when(i == 0)
        def _():
            _load_ffn_weights(win_hbm, wout_hbm, win, wout, sem)
            if head is not None:
                loss_ref[...] = jnp.zeros_like(loss_ref)
                dgf_ref[...] = jnp.zeros_like(dgf_ref)

        x = x_ref[...]
        xn, _, _ = _rms(x, g_ref[...])
        xb = xn.astype(BF16)
        acc = jnp.zeros((tm, D_MODEL), F32)
        for j in range(N_FF_GROUPS):
            gate = _dot_nt(xb, win[j])
            up = _dot_nt(xb, win[j + N_FF_GROUPS])
            gu_ref[j] = gate.astype(BF16)
            gu_ref[j + N_FF_GROUPS] = up.astype(BF16)
            act = gate * _sig(gate) * up
            acc = acc + _dot(act.astype(BF16), wout[j])
        h = x + FFN_RES_WEIGHT * acc
        if head is None:
            out_ref[...] = h
        else:
            gf = gf_ref[...]
            y, hhat, r = _rms(h, gf)
            err = y - tgt_ref[...]
            loss_ref[...] += jnp.full(loss_ref.shape, 0.5 / D_MODEL * jnp.sum(err * err), F32)
            dy = err * (1.0 / D_MODEL)
            dgf_ref[...] += jnp.sum(dy * hhat, axis=0, keepdims=True)
            out_ref[...] = _rms_bwd(dy * gf, hhat, r)

    weights = 2 * D_MODEL * 2 * D_FF + 2 * D_FF * D_MODEL
    tiles = 2 * (2 * 4 * tm * D_MODEL + 2 * tm * 2 * D_FF) + (2 * 4 * tm * D_MODEL if head else 0)
    in_specs = [_row_spec(tm, D_MODEL), _full_spec((1, D_MODEL)), ANY, ANY]
    out_shape = [jax.ShapeDtypeStruct((SEQ, D_MODEL), F32), jax.ShapeDtypeStruct((N_DEV, SEQ, FF_SHARD), BF16)]
    out_specs = [_row_spec(tm, D_MODEL), pl.BlockSpec((N_DEV, tm, FF_SHARD), lambda i: (0, i, 0))]
    args = [h_in, gain, win8, wout]
    if head is not None:
        in_specs += [_row_spec(tm, D_MODEL), _full_spec((1, D_MODEL))]
        out_shape += [jax.ShapeDtypeStruct((1, 128), F32), jax.ShapeDtypeStruct((1, D_MODEL), F32)]
        out_specs += [_full_spec((1, 128)), _full_spec((1, D_MODEL))]
        args += list(head)
    return _call(
        body, name=name, grid=(nt,), in_specs=in_specs, out_specs=out_specs, out_shape=out_shape,
        scratch_shapes=[pltpu.VMEM((N_DEV, FF_SHARD, D_MODEL), BF16), pltpu.VMEM((N_FF_GROUPS, FF_SHARD, D_MODEL), BF16),
                        pltpu.SemaphoreType.DMA((2,))],
        vmem_bytes=weights + tiles + 16 * tm * FF_SHARD * 4, args=args, ride=ride)


def _ffn_backward(dh_out, h_in, gain, gu, win8, wout, name, after=None):
    tm, nt = TOKEN_TILE, SEQ // TOKEN_TILE

    def body(dh_ref, x_ref, g_ref, gu_ref, win_hbm, wout_hbm,
             dhin_ref, dgu_ref, act_ref, xn_ref, df_ref, dg_ref, win, wout, sem):
        i = pl.program_id(0)

        @pl.when(i == 0)
        def _():
            _load_ffn_weights(win_hbm, wout_hbm, win, wout, sem)
            dg_ref[...] = jnp.zeros_like(dg_ref)

        dh = dh_ref[...]
        g = g_ref[...]
        xn, xhat, r = _rms(x_ref[...], g)
        df = (FFN_RES_WEIGHT * dh).astype(BF16)
        dxn = jnp.zeros((tm, D_MODEL), F32)
        for j in range(N_FF_GROUPS):
            gate = gu_ref[j].astype(F32)
            up = gu_ref[j + N_FF_GROUPS].astype(F32)
            dact = _dot_nt(df, wout[j])
            s = _sig(gate)
            silu = gate * s
            dgate = (dact * up * (s * (1.0 + gate * (1.0 - s)))).astype(BF16)
            dup = (dact * silu).astype(BF16)
            act_ref[j] = (silu * up).astype(BF16)
            dgu_ref[j] = dgate
            dgu_ref[j + N_FF_GROUPS] = dup
            dxn = dxn + _dot(dgate, win[j]) + _dot(dup, win[j + N_FF_GROUPS])
        dg_ref[...] += jnp.sum(dxn * xhat, axis=0, keepdims=True)
        dhin_ref[...] = dh + _rms_bwd(dxn * g, xhat, r)
        xn_ref[...] = xn.astype(BF16)
        df_ref[...] = df

    weights = 2 * D_MODEL * 2 * D_FF + 2 * D_FF * D_MODEL
    tiles = 2 * (3 * 4 * tm * D_MODEL + 2 * tm * (2 * 2 * D_FF + D_FF) + 2 * 2 * tm * D_MODEL)
    gu_spec = pl.BlockSpec((N_DEV, tm, FF_SHARD), lambda i: (0, i, 0))
    return _call(
        body, name=name, grid=(nt,),
        in_specs=[_row_spec(tm, D_MODEL), _row_spec(tm, D_MODEL), _full_spec((1, D_MODEL)), gu_spec, ANY, ANY],
        out_specs=[_row_spec(tm, D_MODEL), gu_spec, pl.BlockSpec((N_FF_GROUPS, tm, FF_SHARD), lambda i: (0, i, 0)),
                   _row_spec(tm, D_MODEL), _row_spec(tm, D_MODEL), _full_spec((1, D_MODEL))],
        out_shape=[jax.ShapeDtypeStruct((SEQ, D_MODEL), F32), jax.ShapeDtypeStruct((N_DEV, SEQ, FF_SHARD), BF16),
                   jax.ShapeDtypeStruct((N_FF_GROUPS, SEQ, FF_SHARD), BF16), jax.ShapeDtypeStruct((SEQ, D_MODEL), BF16),
                   jax.ShapeDtypeStruct((SEQ, D_MODEL), BF16), jax.ShapeDtypeStruct((1, D_MODEL), F32)],
        scratch_shapes=[pltpu.VMEM((N_DEV, FF_SHARD, D_MODEL), BF16), pltpu.VMEM((N_FF_GROUPS, FF_SHARD, D_MODEL), BF16),
                        pltpu.SemaphoreType.DMA((2,))],
        vmem_bytes=weights + tiles + 20 * tm * FF_SHARD * 4, args=[dh_out, h_in, gain, gu, win8, wout], after=after)[0]


def _to_sibling(src, dst, send_sem, recv_sem):
    x, y, c = _my_position()
    return pltpu.make_async_remote_copy(src_ref=src, dst_ref=dst, send_sem=send_sem, recv_sem=recv_sem,
                                        device_id=(x, y, 1 - c), device_id_type=MESH)


def _weight_grad(x, g, n_out, x_spec, g_spec, k_dim, n_dim, name, halves=False, tt=2048, ride=None, after=None):
    nt = SEQ // tt
    n_chips = N_DEV // 2
    rows = k_dim // 2 if halves else k_dim

    def body(x_ref, g_ref, out_ref, acc, sendbuf, recvbuf, send_sems, recv_sems):
        b, t = pl.program_id(0), pl.program_id(1)
        c = lax.axis_index("c")

        def push(q):
            return _to_sibling(sendbuf.at[q], recvbuf.at[q], send_sems.at[q], recv_sems.at[q])

        @pl.when(t == 0)
        def _():
            acc[...] = jnp.zeros_like(acc)

        acc[...] += _dot_tn(x_ref[...], g_ref[...])

        @pl.when(t == nt - 1)
        def _():
            if halves:
                for mine, other in ((0, 1), (1, 0)):
                    @pl.when(c == mine)
                    def _():
                        out_ref[b] = acc[pl.ds(mine * rows, rows), :].astype(BF16)
                        sendbuf[b] = acc[pl.ds(other * rows, rows), :].astype(BF16)
                push(b).start()
            else:
                q = b // 2

                @pl.when(b % 2 == c)
                def _():
                    out_ref[q] = acc[...].astype(BF16)

                @pl.when(b % 2 != c)
                def _():
                    sendbuf[q] = acc[...].astype(BF16)
                    push(q).start()

        @pl.when((b == n_out - 1) & (t == nt - 1))
        def _():
            for q in range(n_chips):
                push(q).wait_recv()
                out_ref[q] = (out_ref[q].astype(F32) + recvbuf[q].astype(F32)).astype(BF16)
            for q in range(n_chips):
                push(q).wait_send()

    piece = (n_chips, rows, n_dim)
    outs, ride_outs = _call(
        body, name=name, grid=(n_out, nt), in_specs=[x_spec(tt), g_spec(tt)],
        out_specs=[pl.BlockSpec(piece, lambda b, t: (0, 0, 0))],
        out_shape=[jax.ShapeDtypeStruct(piece, BF16)],
        scratch_shapes=[pltpu.VMEM((k_dim, n_dim), F32), pltpu.VMEM(piece, BF16), pltpu.VMEM(piece, BF16),
                        pltpu.SemaphoreType.DMA((n_chips,)), pltpu.SemaphoreType.DMA((n_chips,))],
        vmem_bytes=2 * 2 * tt * (k_dim + n_dim) + 8 * k_dim * n_dim + 4 * 2 * n_chips * rows * n_dim, args=[x, g], ride=ride,
        after=after)
    return outs[0], ride_outs


def _ffn_w_out_grad(act, df, tag, ride=None, after=None):
    return _weight_grad(
        act, df, N_FF_GROUPS,
        lambda tt: pl.BlockSpec((None, tt, FF_SHARD), lambda b, t: (b, t, 0)),
        lambda tt: pl.BlockSpec((tt, D_MODEL), lambda b, t: (t, 0)),
        FF_SHARD, D_MODEL, name=f"ffn{tag}_w_out_grad", halves=True, ride=ride, after=after)


def _ffn_w_in_grad(xn, dgu, tag, ride=None):
    return _weight_grad(
        dgu, xn, N_DEV,
        lambda tt: pl.BlockSpec((None, tt, FF_SHARD), lambda b, t: (b, t, 0)),
        lambda tt: pl.BlockSpec((tt, D_MODEL), lambda b, t: (t, 0)),
        FF_SHARD, D_MODEL, name=f"ffn{tag}_w_in_grad", ride=ride)


def _load_mix_weight(wmix_hbm, wmix, sem):
    copies = [pltpu.make_async_copy(wmix_hbm.at[d], wmix.at[:, pl.ds(d * MIX_SHARD, MIX_SHARD)], sem.at[d])
              for d in range(N_DEV)]
    for cp in copies:
        cp.start()
    for cp in copies:
        cp.wait()


def _load_pool_weight(pw_hbm, pw, sem):
    rows = POOL_GROUP_DIM // N_DEV
    copies = [pltpu.make_async_copy(pw_hbm.at[d], pw.at[:, pl.ds(d * rows, rows), :], sem.at[d]) for d in range(N_DEV)]
    for cp in copies:
        cp.start()
    for cp in copies:
        cp.wait()


def _rotate(x1, x2, cos, sin):
    return x1 * cos - x2 * sin, x1 * sin + x2 * cos


def _mix_proj_forward(h1, gain, wmix8, cos, sin, ride=None):
    tm, nt = WIDE_TILE, SEQ // WIDE_TILE
    k_scale = HEAD_DIM ** -0.5

    def body(h_ref, g_ref, wmix_hbm, cos_ref, sin_ref, u_ref, qkvg_ref, p_ref, gates_ref, wmix, sem):
        @pl.when(pl.program_id(0) == 0)
        def _():
            _load_mix_weight(wmix_hbm, wmix, sem)

        u = _rms(h_ref[...], g_ref[...])[0].astype(BF16)
        u_ref[...] = u
        cos_t, sin_t = cos_ref[...], sin_ref[...]
        for seg in range(N_SEG):
            pr = _dot(u, wmix[:, pl.ds(seg * D_MODEL, D_MODEL)])
            if seg < 2:
                scale = 1.0 if seg == 0 else k_scale
                for hd in range(HEADS):
                    lo = hd * HEAD_DIM
                    o1, o2 = _rotate(pr[:, lo:lo + ROT_HALF], pr[:, lo + ROT_HALF:lo + HEAD_DIM], cos_t, sin_t)
                    qkvg_ref[:, pl.ds(seg * D_MODEL + lo, ROT_HALF)] = (o1 * scale).astype(BF16)
                    qkvg_ref[:, pl.ds(seg * D_MODEL + lo + ROT_HALF, ROT_HALF)] = (o2 * scale).astype(BF16)
            elif seg < 4:
                qkvg_ref[:, pl.ds(seg * D_MODEL, D_MODEL)] = pr.astype(BF16)
            elif seg == 4:
                p_ref[...] = pr
            else:
                gates_ref[:, pl.ds((seg - 5) * D_MODEL, D_MODEL)] = pr.astype(BF16)

    est = 2 * D_MODEL * N_SEG * D_MODEL + 2 * tm * (4 * D_MODEL + 2 * D_MODEL + 2 * 4 * D_MODEL + 4 * D_MODEL + 2 * 2 * D_MODEL)
    return _call(
        body, name="mix_proj_fwd", grid=(nt,),
        in_specs=[_row_spec(tm, D_MODEL), _full_spec((1, D_MODEL)), ANY, _row_spec(tm, ROT_HALF), _row_spec(tm, ROT_HALF)],
        out_specs=[_row_spec(tm, D_MODEL), _row_spec(tm, 4 * D_MODEL), _row_spec(tm, D_MODEL), _row_spec(tm, 2 * D_MODEL)],
        out_shape=[jax.ShapeDtypeStruct((SEQ, D_MODEL), BF16), jax.ShapeDtypeStruct((SEQ, 4 * D_MODEL), BF16),
                   jax.ShapeDtypeStruct((SEQ, D_MODEL), F32), jax.ShapeDtypeStruct((SEQ, 2 * D_MODEL), BF16)],
        scratch_shapes=[pltpu.VMEM((D_MODEL, N_SEG * D_MODEL), BF16), pltpu.SemaphoreType.DMA((N_DEV,))],
        vmem_bytes=est + 8 * tm * D_MODEL * 4, args=[h1, gain, wmix8, cos, sin], ride=ride)


def _seg_block_spec(seg, reverse=False):
    nb = SEQ // RET_BLOCK
    if reverse:
        return pl.BlockSpec((RET_BLOCK, D_MODEL), lambda i, s=seg: (nb - 1 - i, s))
    return pl.BlockSpec((RET_BLOCK, D_MODEL), lambda i, s=seg: (i, s))


def _table_specs():
    return [_full_spec((HEADS, RET_BLOCK, RET_BLOCK)), _full_spec((HEADS, RET_BLOCK, 1)),
            _full_spec((HEADS, RET_BLOCK, 1)), _full_spec((HEADS, 1, 1))]


def _head_cols(h):
    return pl.ds(h * HEAD_DIM, HEAD_DIM)


def _retention_forward(qkvg, tables, ride=None):
    nb = SEQ // RET_BLOCK

    def body(q_ref, k_ref, v_ref, gr_ref, mask_ref, qdec_ref, kdec_ref, cdec_ref, ret_ref, o_ref, state):
        @pl.when(pl.program_id(0) == 0)
        def _():
            state[...] = jnp.zeros_like(state)

        for h in range(HEADS):
            cols = _head_cols(h)
            q, k, v = q_ref[:, cols], k_ref[:, cols], v_ref[:, cols]
            scores = _dot_nt(q, k) * mask_ref[h]
            inner = _dot(scores.astype(BF16), v)
            cross = _dot((q.astype(F32) * qdec_ref[h]).astype(BF16), state[h].astype(BF16))
            ret = inner + cross
            state[h] = state[h] * cdec_ref[h] + _dot_tn((k.astype(F32) * kdec_ref[h]).astype(BF16), v)
            ret_ref[:, cols] = ret
            retn = ret * lax.rsqrt(jnp.mean(ret * ret, axis=-1, keepdims=True) + NORM_EPS)
            gr = gr_ref[:, cols].astype(F32)
            o_ref[:, cols] = (retn * (gr * _sig(gr))).astype(BF16)

    return _call(
        body, name="retention_fwd", grid=(nb,),
        in_specs=[_seg_block_spec(0), _seg_block_spec(1), _seg_block_spec(2), _seg_block_spec(3)] + _table_specs(),
        out_specs=[_row_spec(RET_BLOCK, D_MODEL)] * 2,
        out_shape=[jax.ShapeDtypeStruct((SEQ, D_MODEL), F32), jax.ShapeDtypeStruct((SEQ, D_MODEL), BF16)],
        scratch_shapes=[pltpu.VMEM((HEADS, HEAD_DIM, HEAD_DIM), F32)],
        vmem_bytes=24 * RET_BLOCK * D_MODEL * 4, args=[qkvg, qkvg, qkvg, qkvg, *tables], ride=ride)


def _retention_backward_q(qkvg, d_oret, ret, tables, ride=None):
    nb = SEQ // RET_BLOCK

    def body(k_ref, v_ref, gr_ref, doret_ref, ret_ref, mask_ref, qdec_ref, kdec_ref, cdec_ref,
             dq_ref, dret_ref, dgr_ref, state):
        @pl.when(pl.program_id(0) == 0)
        def _():
            state[...] = jnp.zeros_like(state)

        for h in range(HEADS):
            cols = _head_cols(h)
            r_h = ret_ref[:, cols]
            rr = lax.rsqrt(jnp.mean(r_h * r_h, axis=-1, keepdims=True) + NORM_EPS)
            rhat = r_h * rr
            gr = gr_ref[:, cols].astype(F32)
            s = _sig(gr)
            d_o = doret_ref[:, cols].astype(F32)
            dgr_ref[:, cols] = (d_o * rhat * (s * (1.0 + gr * (1.0 - s)))).astype(BF16)
            do = _rms_bwd(d_o * (gr * s), rhat, rr).astype(BF16)
            dret_ref[:, cols] = do
            k, v = k_ref[:, cols], v_ref[:, cols]
            dscores = _dot_nt(do, v) * mask_ref[h]
            dq_ref[:, cols] = _dot(dscores.astype(BF16), k) + _dot_nt(do, state[h].astype(BF16)) * qdec_ref[h]
            state[h] = state[h] * cdec_ref[h] + _dot_tn((k.astype(F32) * kdec_ref[h]).astype(BF16), v)

    row = _row_spec(RET_BLOCK, D_MODEL)
    return _call(
        body, name="retention_bwd_q", grid=(nb,),
        in_specs=[_seg_block_spec(1), _seg_block_spec(2), _seg_block_spec(3), row, row] + _table_specs(),
        out_specs=[row, row, row],
        out_shape=[jax.ShapeDtypeStruct((SEQ, D_MODEL), F32), jax.ShapeDtypeStruct((SEQ, D_MODEL), BF16),
                   jax.ShapeDtypeStruct((SEQ, D_MODEL), BF16)],
        scratch_shapes=[pltpu.VMEM((HEADS, HEAD_DIM, HEAD_DIM), F32)],
        vmem_bytes=40 * RET_BLOCK * D_MODEL * 4, args=[qkvg, qkvg, qkvg, d_oret, ret, *tables], ride=ride)


def _retention_backward_kv(qkvg, dret, tables, ride=None):
    nb = SEQ // RET_BLOCK

    def body(q_ref, k_ref, v_ref, do_ref, mask_ref, qdec_ref, kdec_ref, cdec_ref, dk_ref, dv_ref, gstate):
        @pl.when(pl.program_id(0) == 0)
        def _():
            gstate[...] = jnp.zeros_like(gstate)

        for h in range(HEADS):
            cols = _head_cols(h)
            q, k, v, do = q_ref[:, cols], k_ref[:, cols], v_ref[:, cols], do_ref[:, cols]
            mask = mask_ref[h]
            scores = (_dot_nt(q, k) * mask).astype(BF16)
            dscores = (_dot_nt(do, v) * mask).astype(BF16)
            gs = gstate[h].astype(BF16)
            dk_ref[:, cols] = _dot_tn(dscores, q) + _dot_nt(v, gs) * kdec_ref[h]
            dv_ref[:, cols] = _dot_tn(scores, do) + _dot((k.astype(F32) * kdec_ref[h]).astype(BF16), gs)
            gstate[h] = gstate[h] * cdec_ref[h] + _dot_tn((q.astype(F32) * qdec_ref[h]).astype(BF16), do)

    rev = lambda: pl.BlockSpec((RET_BLOCK, D_MODEL), lambda i: (nb - 1 - i, 0))
    return _call(
        body, name="retention_bwd_kv", grid=(nb,),
        in_specs=[_seg_block_spec(0, True), _seg_block_spec(1, True), _seg_block_spec(2, True), rev()] + _table_specs(),
        out_specs=[rev(), rev()],
        out_shape=[jax.ShapeDtypeStruct((SEQ, D_MODEL), F32)] * 2,
        scratch_shapes=[pltpu.VMEM((HEADS, HEAD_DIM, HEAD_DIM), F32)],
        vmem_bytes=32 * RET_BLOCK * D_MODEL * 4, args=[qkvg, qkvg, qkvg, dret, *tables], ride=ride)


def _pooled(p_ext, first_row):
    rows = p_ext.shape[0]
    t = first_row + lax.broadcasted_iota(jnp.int32, (rows - HALO, 1), 0)
    outs = []
    for g, w in enumerate(POOL_WINDOWS):
        e = p_ext[:, g * POOL_GROUP_DIM:(g + 1) * POOL_GROUP_DIM]
        s, span = e, 1
        while span < w:
            s = s + pltpu.roll(s, span, 0)
            span *= 2
        count = jnp.minimum(t + 1, w).astype(F32)
        outs.append(s[HALO:] / count - e[HALO:])
    return outs


def _pooled_transpose(d_ext, first_row):
    rows = d_ext.shape[0]
    t = first_row + lax.broadcasted_iota(jnp.int32, (rows, 1), 0)
    outs = []
    for g, w in enumerate(POOL_WINDOWS):
        d = d_ext[:, g * POOL_GROUP_DIM:(g + 1) * POOL_GROUP_DIM]
        e = jnp.where(t < SEQ, d / jnp.minimum(t + 1, w).astype(F32), 0.0)
        s, span = e, 1
        while span < w:
            s = s + pltpu.roll(s, rows - span, 0)
            span *= 2
        outs.append(s[:rows - HALO] - d[:rows - HALO])
    return outs


def _mix_tail_specs(tm):
    halo_blocks = tm // HALO
    return [
        _row_spec(tm, D_MODEL),
        pl.BlockSpec((HALO, D_MODEL), lambda i: (jnp.maximum(i * halo_blocks - 1, 0), 0)),
        _row_spec(tm, 2 * D_MODEL),
        _row_spec(tm, D_MODEL),
        _full_spec((2, D_MODEL)), _full_spec((1, D_MODEL)), ANY,
        _full_spec((D_MODEL, D_MODEL)), _full_spec((D_MODEL, D_MODEL)), _full_spec((D_MODEL, D_MODEL)),
    ]


def _mix_tail_compute(i, tm, p_ref, halo_ref, gates_ref, oret_ref, bias_ref, scale_ref, pw, wru_ref, wpu_ref, saved=None):
    halo = jnp.where(i > 0, halo_ref[...], 0.0)
    pooled = _pooled(jnp.concatenate([halo, p_ref[...]], axis=0), i * tm)
    pooled = [x.astype(BF16) for x in pooled]
    mixed = jnp.concatenate([_dot(pooled[g], pw[g]) for g in range(len(POOL_WINDOWS))], axis=-1)
    pool_out = (mixed * scale_ref[...]).astype(BF16)
    o_ret = oret_ref[...]
    if saved is None:
        a = _dot(o_ret, wru_ref[...])
        b = _dot(pool_out, wpu_ref[...])
    else:
        a, b = saved[0][...].astype(F32), saved[1][...].astype(F32)
    z = gates_ref[...].astype(F32)
    g0 = _sig(z[:, :D_MODEL] + bias_ref[0:1, :])
    g1 = _sig(z[:, D_MODEL:] + bias_ref[1:2, :])
    merged = (g0 * a + g1 * b).astype(BF16)
    return pooled, mixed, pool_out, o_ret, a, b, g0, g1, merged


def _mix_tail_forward(p, gates, o_ret, h1, bias, scale, pw8, wru, wpu, wo, ride=None):
    tm, nt = TOKEN_TILE, SEQ // TOKEN_TILE

    def body(p_ref, halo_ref, gates_ref, oret_ref, bias_ref, scale_ref, pw_hbm, wru_ref, wpu_ref, wo_ref, h1_ref,
             h2_ref, a_ref, b_ref, pw, sem):
        i = pl.program_id(0)

        @pl.when(i == 0)
        def _():
            _load_pool_weight(pw_hbm, pw, sem)

        out = _mix_tail_compute(i, tm, p_ref, halo_ref, gates_ref, oret_ref, bias_ref, scale_ref, pw, wru_ref, wpu_ref)
        a_ref[...] = out[4].astype(BF16)
        b_ref[...] = out[5].astype(BF16)
        h2_ref[...] = h1_ref[...] + _dot(out[-1], wo_ref[...])

    est = 3 * 2 * 2 * D_MODEL * D_MODEL + 2 * tm * D_MODEL * (4 + 4 + 2 + 4 + 4) + 16 * tm * D_MODEL * 4
    return _call(
        body, name="mix_tail_fwd", grid=(nt,),
        in_specs=_mix_tail_specs(tm) + [_row_spec(tm, D_MODEL)],
        out_specs=[_row_spec(tm, D_MODEL)] * 3,
        out_shape=[jax.ShapeDtypeStruct((SEQ, D_MODEL), F32)] + [jax.ShapeDtypeStruct((SEQ, D_MODEL), BF16)] * 2,
        scratch_shapes=[pltpu.VMEM((len(POOL_WINDOWS), POOL_GROUP_DIM, POOL_GROUP_DIM), BF16), pltpu.SemaphoreType.DMA((N_DEV,))],
        vmem_bytes=est, args=[p, p, gates, o_ret, bias, scale, pw8, wru, wpu, wo, h1], ride=ride)


def _mix_tail_backward(dh2, p, gates, o_ret, a_saved, b_saved, bias, scale, pw8, wru, wpu, wo, ride=None):
    tm, nt = TOKEN_TILE, SEQ // TOKEN_TILE
    n_groups = len(POOL_WINDOWS)
    rows_per_dev = POOL_GROUP_DIM // N_DEV

    def body(p_ref, halo_ref, gates_ref, oret_ref, bias_ref, scale_ref, pw_hbm, wru_ref, wpu_ref, wo_ref,
             dh2_ref, a_ref, b_ref,
             doret_ref, dgates_ref, dpooled_ref, dwo_ref, dwru_ref, dwpu_ref, dpw_ref, dbias_ref, dscale_ref,
             pw, sem, acc_wo, acc_wru, acc_wpu, acc_pw, send_sq, recv_sq, send_pw, recv_pw, send_sems, recv_sems):
        i = pl.program_id(0)

        @pl.when(i == 0)
        def _():
            _load_pool_weight(pw_hbm, pw, sem)
            for ref in (acc_wo, acc_wru, acc_wpu, acc_pw, dbias_ref, dscale_ref):
                ref[...] = jnp.zeros_like(ref)

        pooled, mixed, pool_out, o_ret, a, b, g0, g1, merged = _mix_tail_compute(
            i, tm, p_ref, halo_ref, gates_ref, oret_ref, bias_ref, scale_ref, pw, wru_ref, wpu_ref, saved=(a_ref, b_ref))
        dh2 = dh2_ref[...].astype(BF16)
        dm = _dot_nt(dh2, wo_ref[...])
        acc_wo[...] += _dot_tn(merged, dh2)
        da = (dm * g0).astype(BF16)
        db = (dm * g1).astype(BF16)
        dz0 = dm * a * g0 * (1.0 - g0)
        dz1 = dm * b * g1 * (1.0 - g1)
        dbias_ref[0:1, :] += jnp.sum(dz0, axis=0, keepdims=True)
        dbias_ref[1:2, :] += jnp.sum(dz1, axis=0, keepdims=True)
        dgates_ref[:, pl.ds(0, D_MODEL)] = dz0.astype(BF16)
        dgates_ref[:, pl.ds(D_MODEL, D_MODEL)] = dz1.astype(BF16)
        acc_wru[...] += _dot_tn(o_ret, da)
        acc_wpu[...] += _dot_tn(pool_out, db)
        doret_ref[...] = _dot_nt(da, wru_ref[...]).astype(BF16)
        d_pool_out = _dot_nt(db, wpu_ref[...])
        dscale_ref[...] += jnp.sum(d_pool_out * mixed, axis=0, keepdims=True)
        dmixed = (d_pool_out * scale_ref[...]).astype(BF16)
        for g in range(n_groups):
            dmg = dmixed[:, g * POOL_GROUP_DIM:(g + 1) * POOL_GROUP_DIM]
            acc_pw[g] += _dot_tn(pooled[g], dmg)
            dpooled_ref[:, pl.ds(g * POOL_GROUP_DIM, POOL_GROUP_DIM)] = _dot_nt(dmg, pw[g])

        @pl.when(i == nt - 1)
        def _():
            c = lax.axis_index("c")
            rows = D_MODEL // N_DEV
            squares = ((acc_wo, dwo_ref), (acc_wru, dwru_ref), (acc_wpu, dwpu_ref))
            for q in range(n_chips):
                own = pl.multiple_of((2 * q + c) * rows, rows)
                other = pl.multiple_of((2 * q + 1 - c) * rows, rows)
                for t, (acc, out) in enumerate(squares):
                    out[q] = acc[pl.ds(own, rows), :].astype(BF16)
                    send_sq[t, q] = acc[pl.ds(other, rows), :].astype(BF16)
                own_pw = pl.multiple_of((2 * q + c) * rows_per_dev, rows_per_dev)
                other_pw = pl.multiple_of((2 * q + 1 - c) * rows_per_dev, rows_per_dev)
                dpw_ref[q] = acc_pw[:, pl.ds(own_pw, rows_per_dev), :].astype(BF16)
                send_pw[q] = acc_pw[:, pl.ds(other_pw, rows_per_dev), :].astype(BF16)
            pushes = [_to_sibling(send_sq, recv_sq, send_sems.at[0], recv_sems.at[0]),
                      _to_sibling(send_pw, recv_pw, send_sems.at[1], recv_sems.at[1])]
            for cp in pushes:
                cp.start()
            for cp in pushes:
                cp.wait_recv()
            for t, (acc, out) in enumerate(squares):
                out[...] = (out[...].astype(F32) + recv_sq[t].astype(F32)).astype(BF16)
            dpw_ref[...] = (dpw_ref[...].astype(F32) + recv_pw[...].astype(F32)).astype(BF16)
            for cp in pushes:
                cp.wait_send()

    n_chips = N_DEV // 2
    sq = (n_chips, D_MODEL // N_DEV, D_MODEL)
    pw_shape = (n_chips, n_groups, rows_per_dev, POOL_GROUP_DIM)
    est = (3 * 2 * 2 * D_MODEL * D_MODEL + 3 * 4 * D_MODEL * D_MODEL + 3 * 2 * 2 * D_MODEL * D_MODEL
           + 2 * tm * D_MODEL * (4 + 4 + 2 + 4 + 4 + 2 + 2 + 2 + 4 + 4) + 24 * tm * D_MODEL * 4)
    return _call(
        body, name="mix_tail_bwd", grid=(nt,),
        in_specs=_mix_tail_specs(tm) + [_row_spec(tm, D_MODEL), _row_spec(tm, D_MODEL), _row_spec(tm, D_MODEL)],
        out_specs=[_row_spec(tm, D_MODEL), _row_spec(tm, 2 * D_MODEL), _row_spec(tm, D_MODEL),
                   _full_spec(sq), _full_spec(sq), _full_spec(sq), _full_spec(pw_shape),
                   _full_spec((2, D_MODEL)), _full_spec((1, D_MODEL))],
        out_shape=[jax.ShapeDtypeStruct((SEQ, D_MODEL), BF16),
                   jax.ShapeDtypeStruct((SEQ, 2 * D_MODEL), BF16), jax.ShapeDtypeStruct((SEQ, D_MODEL), F32),
                   jax.ShapeDtypeStruct(sq, BF16), jax.ShapeDtypeStruct(sq, BF16), jax.ShapeDtypeStruct(sq, BF16),
                   jax.ShapeDtypeStruct(pw_shape, BF16),
                   jax.ShapeDtypeStruct((2, D_MODEL), F32), jax.ShapeDtypeStruct((1, D_MODEL), F32)],
        scratch_shapes=[pltpu.VMEM((n_groups, POOL_GROUP_DIM, POOL_GROUP_DIM), BF16), pltpu.SemaphoreType.DMA((N_DEV,)),
                        pltpu.VMEM((D_MODEL, D_MODEL), F32), pltpu.VMEM((D_MODEL, D_MODEL), F32),
                        pltpu.VMEM((D_MODEL, D_MODEL), F32), pltpu.VMEM((n_groups, POOL_GROUP_DIM, POOL_GROUP_DIM), F32),
                        pltpu.VMEM((3,) + sq, BF16), pltpu.VMEM((3,) + sq, BF16), pltpu.VMEM(pw_shape, BF16),
                        pltpu.VMEM(pw_shape, BF16), pltpu.SemaphoreType.DMA((2,)), pltpu.SemaphoreType.DMA((2,))],
        vmem_bytes=est, args=[p, p, gates, o_ret, bias, scale, pw8, wru, wpu, wo, dh2, a_saved, b_saved], ride=ride)


def _mix_proj_backward(dq, dk, dv, dgr, dpooled, dgates, cos, sin, h1, gain, dh2, wmix8, ride=None):
    tm, nt = TOKEN_TILE, SEQ // TOKEN_TILE
    halo_blocks = tm // HALO
    last_halo = SEQ // HALO - 1
    k_scale = HEAD_DIM ** -0.5

    def body(dq_ref, dk_ref, dv_ref, dgr_ref, dpool_ref, dhalo_ref, dgates_ref, cos_ref, sin_ref, h1_ref, g_ref,
             dh2_ref, wmix_hbm, dh1_ref, dproj_ref, dg_ref, wmix, sem):
        i = pl.program_id(0)

        @pl.when(i == 0)
        def _():
            _load_mix_weight(wmix_hbm, wmix, sem)
            dg_ref[...] = jnp.zeros_like(dg_ref)

        cos_t, sin_t = cos_ref[...], sin_ref[...]
        for seg, ref, scale in ((0, dq_ref, 1.0), (1, dk_ref, k_scale)):
            for hd in range(HEADS):
                lo = hd * HEAD_DIM
                d1, d2 = ref[:, lo:lo + ROT_HALF], ref[:, lo + ROT_HALF:lo + HEAD_DIM]
                dproj_ref[:, pl.ds(seg * D_MODEL + lo, ROT_HALF)] = ((d1 * cos_t + d2 * sin_t) * scale).astype(BF16)
                dproj_ref[:, pl.ds(seg * D_MODEL + lo + ROT_HALF, ROT_HALF)] = ((d2 * cos_t - d1 * sin_t) * scale).astype(BF16)
        dproj_ref[:, pl.ds(2 * D_MODEL, D_MODEL)] = dv_ref[...].astype(BF16)
        dproj_ref[:, pl.ds(3 * D_MODEL, D_MODEL)] = dgr_ref[...]
        dp = _pooled_transpose(jnp.concatenate([dpool_ref[...], dhalo_ref[...]], axis=0), i * tm)
        for g in range(len(POOL_WINDOWS)):
            dproj_ref[:, pl.ds(4 * D_MODEL + g * POOL_GROUP_DIM, POOL_GROUP_DIM)] = dp[g].astype(BF16)
        dproj_ref[:, pl.ds(5 * D_MODEL, 2 * D_MODEL)] = dgates_ref[...]
        du = jnp.zeros((tm, D_MODEL), F32)
        for seg in range(N_SEG):
            cols = pl.ds(seg * D_MODEL, D_MODEL)
            du = du + _dot_nt(dproj_ref[:, cols], wmix[:, cols])
        g = g_ref[...]
        _, xhat, r = _rms(h1_ref[...], g)
        dg_ref[...] += jnp.sum(du * xhat, axis=0, keepdims=True)
        dh1_ref[...] = dh2_ref[...] + _rms_bwd(du * g, xhat, r)

    est = 2 * D_MODEL * N_SEG * D_MODEL + 2 * tm * D_MODEL * (3 * 4 + 2 + 4 + 4 + 4 + 4 + 4 + 14) + 12 * tm * D_MODEL * 4
    return _call(
        body, name="mix_proj_bwd", grid=(nt,),
        in_specs=[_row_spec(tm, D_MODEL), _row_spec(tm, D_MODEL), _row_spec(tm, D_MODEL), _row_spec(tm, D_MODEL),
                  _row_spec(tm, D_MODEL),
                  pl.BlockSpec((HALO, D_MODEL), lambda i: (jnp.minimum((i + 1) * halo_blocks, last_halo), 0)),
                  _row_spec(tm, 2 * D_MODEL), _row_spec(tm, ROT_HALF), _row_spec(tm, ROT_HALF),
                  _row_spec(tm, D_MODEL), _full_spec((1, D_MODEL)), _row_spec(tm, D_MODEL), ANY],
        out_specs=[_row_spec(tm, D_MODEL), _row_spec(tm, N_SEG * D_MODEL), _full_spec((1, D_MODEL))],
        out_shape=[jax.ShapeDtypeStruct((SEQ, D_MODEL), F32), jax.ShapeDtypeStruct((SEQ, N_SEG * D_MODEL), BF16),
                   jax.ShapeDtypeStruct((1, D_MODEL), F32)],
        scratch_shapes=[pltpu.VMEM((D_MODEL, N_SEG * D_MODEL), BF16), pltpu.SemaphoreType.DMA((N_DEV,))],
        vmem_bytes=est, args=[dq, dk, dv, dgr, dpooled, dpooled, dgates, cos, sin, h1, gain, dh2, wmix8], ride=ride)


def _adamw(w, parts, m, v, name, after=None):
    rows, cols = w.shape
    n_lists = len(parts)
    tr = max([t for t in range(16, 257, 16) if rows % t == 0], default=rows)
    c1 = 1.0 - ADAM_B1 ** ADAM_STEP
    c2 = 1.0 - ADAM_B2 ** ADAM_STEP

    def body(*refs):
        w_ref, m_ref, v_ref = refs[:3]
        part_refs = refs[3:3 + n_lists]
        g_out, d_out, m_out, v_out = refs[-4:]
        g = None
        for p_ref in part_refs:
            for k in range(p_ref.shape[0]):
                term = p_ref[k].astype(F32)
                g = term if g is None else g + term
        m_new = ADAM_B1 * m_ref[...] + (1.0 - ADAM_B1) * g
        v_new = ADAM_B2 * v_ref[...] + (1.0 - ADAM_B2) * (g * g)
        g_out[...] = g
        m_out[...] = m_new
        v_out[...] = v_new
        d_out[...] = -ADAM_LR * ((m_new / c1) / (jnp.sqrt(v_new / c2) + ADAM_EPS) + ADAM_WD * w_ref[...])

    spec = pl.BlockSpec((tr, cols), lambda i: (i, 0))
    out = jax.ShapeDtypeStruct((rows, cols), F32)
    part_specs = [pl.BlockSpec((p.shape[0], tr, cols), lambda i: (0, i, 0)) for p in parts]
    part_bytes = sum(p.shape[0] * p.dtype.itemsize for p in parts)
    extra = [] if after is None else [after]
    return pl.pallas_call(
        body, name=name, grid=(rows // tr,),
        in_specs=[spec, spec, spec] + part_specs + [ANY] * len(extra),
        out_specs=[spec] * 4, out_shape=[out] * 4,
        compiler_params=_params(2 * tr * cols * (7 * 4 + part_bytes) + 8 * tr * cols * 4, 1),
    )(_in_hbm(w), _in_hbm(m), _in_hbm(v), *[_in_hbm(p) for p in parts], *extra)


def _mix_w_in_grad(u, dproj, ride=None):
    return _weight_grad(
        u, dproj, N_DEV,
        lambda tt: pl.BlockSpec((tt, D_MODEL), lambda b, t: (t, 0)),
        lambda tt: pl.BlockSpec((tt, MIX_SHARD), lambda b, t: (t, b)),
        D_MODEL, MIX_SHARD, name="w_in_grad", ride=ride)


def kernel(x, norm_ffn1, ffn1_w_in, ffn1_w_out, norm_mix, w_in, gate_bias, pool_w, pool_scale, w_ret_up, w_pool_up, w_out, norm_ffn2, ffn2_w_in, ffn2_w_out, norm_final, loss_target, m_norm_ffn1, m_ffn1_w_in, m_ffn1_w_out, m_norm_mix, m_w_in, m_gate_bias, m_pool_w, m_pool_scale, m_w_ret_up, m_w_pool_up, m_w_out, m_norm_ffn2, m_ffn2_w_in, m_ffn2_w_out, m_norm_final, v_norm_ffn1, v_ffn1_w_in, v_ffn1_w_out, v_norm_mix, v_w_in, v_gate_bias, v_pool_w, v_pool_scale, v_w_ret_up, v_w_pool_up, v_w_out, v_norm_ffn2, v_ffn2_w_in, v_ffn2_w_out, v_norm_final):
    assert x.shape == (1, SEQ, D_MODEL) and ffn1_w_in.shape == (1, D_MODEL, FF_SHARD) and w_in.shape == (1, D_MODEL, MIX_SHARD)
    x2, target = x[0], loss_target[0]

    cos, sin = _rotary_tables()
    tables = _retention_tables()
    bf = lambda w: w[0].astype(BF16)
    bf_t = lambda w: jnp.swapaxes(w[0], 0, 1).astype(BF16)
    square = lambda w: w.reshape(D_MODEL, D_MODEL)

    win1, wout1, bias8 = _alone(_GatherRide([bf_t(ffn1_w_in), bf(ffn1_w_out), gate_bias[0]]), "ffn1_weights_all_gather")
    wout1 = wout1.reshape(N_FF_GROUPS, FF_SHARD, D_MODEL)
    bias = bias8.transpose(1, 0, 2).reshape(2, D_MODEL)

    (h1, gu1), (wmix8,) = _ffn_forward(x2, norm_ffn1, win1, wout1, "ffn1_fwd", ride=_GatherRide([bf(w_in)]))
    (u, qkvg, p, gates), (win2,) = _mix_proj_forward(h1, norm_mix, wmix8, cos, sin, ride=_GatherRide([bf_t(ffn2_w_in)]))
    (ret, o_ret), (pw8, wru, wpu, wo) = _retention_forward(
        qkvg, tables, ride=_GatherRide([bf(pool_w), bf(w_ret_up), bf(w_pool_up), bf(w_out)]))
    wru, wpu, wo = square(wru), square(wpu), square(wo)
    (h2, a_saved, b_saved), (wout2,) = _mix_tail_forward(p, gates, o_ret, h1, bias, pool_scale, pw8, wru, wpu, wo,
                                        ride=_GatherRide([bf(ffn2_w_out)]))
    wout2 = wout2.reshape(N_FF_GROUPS, FF_SHARD, D_MODEL)
    (dh3, gu2, loss_part, d_norm_final), _ = _ffn_forward(h2, norm_ffn2, win2, wout2, "ffn2_fwd_loss",
                                                          head=(target, norm_final.reshape(1, D_MODEL)))

    dh2, dgu2, act2, xn2, df2, d_norm_ffn2 = _ffn_backward(dh3, h2, norm_ffn2, gu2, win2, wout2, "ffn2_bwd")
    d_wout2, _ = _ffn_w_out_grad(act2, df2, 2)
    d_win2, (r_wout2,) = _ffn_w_in_grad(xn2, dgu2, 2, ride=_ScatterRide([d_wout2]))
    (d_oret, dgates, dpooled, d_wo, d_wru, d_wpu, d_pw, d_bias, d_scale), (r_win2,) = _mix_tail_backward(
        dh2, p, gates, o_ret, a_saved, b_saved, bias, pool_scale, pw8, wru, wpu, wo, ride=_ScatterRide([d_win2]))
    (dq, dret, dgr), _ = _retention_backward_q(qkvg, d_oret, ret, tables)
    (dk, dv), (r_pw, r_wru, r_wpu, r_wo) = _retention_backward_kv(
        qkvg, dret, tables, ride=_ScatterRide([d_pw, d_wru, d_wpu, d_wo]))
    (dh1, dproj, d_norm_mix), _ = _mix_proj_backward(dq, dk, dv, dgr, dpooled, dgates, cos, sin, h1, norm_mix, dh2, wmix8)
    d_wmix, _ = _mix_w_in_grad(u, dproj)
    wmix_state, wmix_started = _scatter_start(d_wmix, "w_in_grad_exchange_start")
    grad_x, dgu1, act1, xn1, df1, d_norm_ffn1 = _ffn_backward(dh1, x2, norm_ffn1, gu1, win1, wout1, "ffn1_bwd", after=wmix_started)
    small_rows = jnp.concatenate(
        [d_norm_ffn1, d_norm_mix, d_scale, d_norm_ffn2, d_norm_final, d_bias, jnp.tile(loss_part, (1, D_MODEL // 128))],
        axis=0)
    d_win1, (small_all,) = _ffn_w_in_grad(xn1, dgu1, 1, ride=_GatherRide([small_rows]))
    win1_state, win1_started = _scatter_start(d_win1, "ffn1_w_in_grad_exchange_start")
    d_wout1, _ = _ffn_w_out_grad(act1, df1, 1, after=win1_started)
    wout1_state, started = _scatter_start(d_wout1, "ffn1_w_out_grad_exchange_start")
    zero_row = jnp.zeros((1, D_MODEL), F32)

    results = {}

    def update(nm, w, parts, m, v, after):
        if nm in ("ffn1_w_in", "ffn2_w_in"):
            flat, back = (lambda a: jnp.swapaxes(a[0], 0, 1)), (lambda o: jnp.swapaxes(o, 0, 1)[None])
        else:
            flat, back = (lambda a: a.reshape(-1, w.shape[-1])), (lambda o: o.reshape(w.shape))
        parts = [p.reshape(p.shape[:1] + flat(w).shape) for p in parts]
        outs = _adamw(flat(w), parts, flat(m), flat(v), name=f"adamw_{nm}", after=after)
        results[nm] = [back(o) for o in outs]
        return outs[0]

    done = update("w_in", w_in, _scatter_wait(wmix_state, started, "w_in_grad_exchange_wait"), m_w_in, v_w_in, None)
    for nm, w, parts, m, v in (
            ("ffn2_w_in", ffn2_w_in, r_win2, m_ffn2_w_in, v_ffn2_w_in),
            ("ffn2_w_out", ffn2_w_out, r_wout2, m_ffn2_w_out, v_ffn2_w_out), ("w_ret_up", w_ret_up, r_wru, m_w_ret_up, v_w_ret_up),
            ("w_pool_up", w_pool_up, r_wpu, m_w_pool_up, v_w_pool_up), ("w_out", w_out, r_wo, m_w_out, v_w_out),
            ("pool_w", pool_w, r_pw, m_pool_w, v_pool_w)):
        done = update(nm, w, [parts], m, v, done)
    done = update("ffn1_w_in", ffn1_w_in, _scatter_wait(win1_state, done, "ffn1_w_in_grad_exchange_wait"),
                  m_ffn1_w_in, v_ffn1_w_in, None)
    update("ffn1_w_out", ffn1_w_out, _scatter_wait(wout1_state, done, "ffn1_w_out_grad_exchange_wait"),
           m_ffn1_w_out, v_ffn1_w_out, None)

    my_id = _linear_id(*_my_position())
    bias_cols = gate_bias.shape[-1]
    pad = lambda a: jnp.pad(a[0], ((0, 0), (0, D_MODEL - bias_cols)))
    pack = lambda a, b, c, d, e, gb: jnp.concatenate([a, b, c, d, e.reshape(1, D_MODEL), pad(gb), zero_row], axis=0)
    d_bias_mine = lax.dynamic_slice_in_dim(small_all[:, 5:7], my_id * bias_cols, bias_cols, axis=2)
    g_small = jnp.concatenate([small_all[:, 0:5], jnp.pad(d_bias_mine, ((0, 0), (0, 0), (0, D_MODEL - bias_cols))),
                               small_all[:, 7:8]], axis=1)
    s_outs = _adamw(pack(norm_ffn1, norm_mix, pool_scale, norm_ffn2, norm_final, gate_bias), [g_small],
                    pack(m_norm_ffn1, m_norm_mix, m_pool_scale, m_norm_ffn2, m_norm_final, m_gate_bias),
                    pack(v_norm_ffn1, v_norm_mix, v_pool_scale, v_norm_ffn2, v_norm_final, v_gate_bias), name="adamw_small")
    loss = s_outs[0][7, 0]
    for row, nm in enumerate(["norm_ffn1", "norm_mix", "pool_scale", "norm_ffn2"]):
        results[nm] = [o[row:row + 1] for o in s_outs]
    results["norm_final"] = [o[4] for o in s_outs]
    results["gate_bias"] = [o[5:7, :bias_cols][None] for o in s_outs]

    order = ["norm_ffn1", "ffn1_w_in", "ffn1_w_out", "norm_mix", "w_in", "gate_bias", "pool_w", "pool_scale",
             "w_ret_up", "w_pool_up", "w_out", "norm_ffn2", "ffn2_w_in", "ffn2_w_out", "norm_final"]
    return (loss, grad_x[None], *[results[nm][0] for nm in order], *[results[nm][1] for nm in order],
            *[results[nm][2] for nm in order], *[results[nm][3] for nm in order])
```

```python
import numpy as np
import jax
import jax.numpy as jnp
from jax import lax
from jax.experimental import pallas as pl
from jax.experimental.pallas import tpu as pltpu

F32 = jnp.float32
BF16 = jnp.bfloat16

N_DEV = 8
D_MODEL = 1024
SEQ = 4096
D_FF = 2816
FF_SHARD = 2 * D_FF // N_DEV
N_FF_GROUPS = N_DEV // 2
HEADS = 4
HEAD_DIM = 256
ROT_HALF = HEAD_DIM // 2
CHUNK = 64
RET_BLOCK = 256
POOL_WINDOWS = (2, 4, 8, 16)
POOL_GROUP_DIM = 256
HALO = 16
MIX_SHARD = 7 * D_MODEL // N_DEV
N_SEG = 7
ROPE_BASE = 10000.0
NORM_EPS = 1e-6
FFN_RES_WEIGHT = 0.5
ADAM_LR, ADAM_B1, ADAM_B2, ADAM_EPS, ADAM_WD, ADAM_STEP = 0.001, 0.9, 0.999, 1e-08, 0.01, 10

TOKEN_TILE = 256
WIDE_TILE = 512
VMEM_CAP_V7X = 64 * 1024 * 1024
MESH = pl.DeviceIdType.MESH
ANY = pl.BlockSpec(memory_space=pl.ANY)


def _vmem_limit(estimate_bytes):
    return int(min(estimate_bytes * 5 // 4 + (6 << 20), VMEM_CAP_V7X - (4 << 20)))


def _params(estimate_bytes, n_grid):
    return pltpu.CompilerParams(dimension_semantics=("arbitrary",) * n_grid,
                                vmem_limit_bytes=_vmem_limit(estimate_bytes))


def _dot(a, b):
    return jnp.dot(a, b, preferred_element_type=F32)


def _dot_nt(a, b):
    return lax.dot_general(a, b, (((1,), (1,)), ((), ())), preferred_element_type=F32)


def _dot_tn(a, b):
    return lax.dot_general(a, b, (((0,), (0,)), ((), ())), preferred_element_type=F32)


def _sig(x):
    return 1.0 / (1.0 + jnp.exp(-x))


def _rms(x, g):
    r = lax.rsqrt(jnp.mean(x * x, axis=-1, keepdims=True) + NORM_EPS)
    xhat = x * r
    return xhat * g, xhat, r


def _rms_bwd(dyg, xhat, r):
    return r * (dyg - xhat * jnp.mean(dyg * xhat, axis=-1, keepdims=True))


def _row_spec(tile, width, col=0):
    return pl.BlockSpec((tile, width), lambda i, c=col: (i, c))


def _full_spec(shape):
    return pl.BlockSpec(shape, lambda *_: (0,) * len(shape))


def _rotary_tables():
    inv_freq = (np.float32(ROPE_BASE) ** (-np.arange(ROT_HALF, dtype=np.float32) / np.float32(ROT_HALF))).astype(np.float32)
    ang = (np.arange(SEQ, dtype=np.float32)[:, None] * inv_freq[None, :]).astype(np.float32)
    return jnp.asarray(np.cos(ang.astype(np.float64)), F32), jnp.asarray(np.sin(ang.astype(np.float64)), F32)


def _retention_tables():
    log_gamma = np.log(1.0 - 2.0 ** (-5.0 - np.arange(HEADS, dtype=np.float64)))
    n = np.arange(RET_BLOCK)
    diff = (n[:, None] - n[None, :]).astype(np.float64)
    same = (n[:, None] // CHUNK) == (n[None, :] // CHUNK)
    earlier = (n[None, :] // CHUNK) < (n[:, None] // CHUNK)
    expo = np.where(same, np.abs(diff), diff)
    mask = np.where(same | earlier, np.exp(log_gamma[:, None, None] * expo[None]), 0.0)
    qdec = np.exp(log_gamma[:, None] * (n[None, :] + 1.0))[:, :, None]
    kdec = np.exp(log_gamma[:, None] * (RET_BLOCK - 1.0 - n[None, :]))[:, :, None]
    cdec = np.exp(log_gamma * RET_BLOCK)[:, None, None]
    return (jnp.asarray(mask, F32), jnp.asarray(qdec, F32), jnp.asarray(kdec, F32), jnp.asarray(cdec, F32))


def _my_position():
    return lax.axis_index("x"), lax.axis_index("y"), lax.axis_index("c")


def _linear_id(px, py, pc):
    return 4 * px + 2 * py + pc


def _when(pred, fn):
    if isinstance(pred, bool):
        if pred:
            fn()
    else:
        pl.when(pred)(fn)


class _GatherRide:
    def __init__(self, shards):
        self.args = list(shards)
        n = self.n = len(shards)
        self.out_shape = [pltpu.HBM((N_DEV,) + s.shape, s.dtype) for s in shards]
        self.pieces = []
        for t, s in enumerate(shards):
            half = s.shape[0] // 2
            self.pieces += [(t, 0, half), (t, half, half)] if s.shape[0] >= 256 and half % 16 == 0 else [(t, 0, s.shape[0])]
        p = len(self.pieces)
        self.scratch = [pltpu.SemaphoreType.DMA((p, 7)), pltpu.SemaphoreType.DMA((p, 7)), pltpu.SemaphoreType.DMA((n,))]

    def _plan(self, src, out, sems):
        send_sems, recv_sems, local_sem = sems
        x, y, c = _my_position()
        me, sibling = (x, y, c), (x, y, 1 - c)
        chips = [(1 - x, y), (x, 1 - y), (1 - x, 1 - y)]

        def copy(i, k, block, to, from_src=False):
            t, first, count = self.pieces[i]
            rows = out[t].at[_linear_id(*block), pl.ds(first, count)]
            return pltpu.make_async_remote_copy(
                src_ref=src[t].at[pl.ds(first, count)] if from_src else rows, dst_ref=rows,
                send_sem=send_sems.at[i, k], recv_sem=recv_sems.at[i, k],
                device_id=to, device_id_type=MESH)

        def relay(t):
            return copy(t, 3, (x ^ (1 - c), y ^ c, c), (x ^ c, y ^ (1 - c), c))

        local = [pltpu.make_async_copy(src[t], out[t].at[_linear_id(*me)], local_sem.at[t]) for t in range(self.n)]
        return copy, relay, local, me, sibling, chips, c

    def begin(self, first, src, out, sems):
        copy, relay, local, me, sibling, chips, c = self._plan(src, out, sems)

        def start():
            for cp in local:
                cp.start()
            for t in range(len(self.pieces)):
                copy(t, 0, me, sibling, from_src=True).start()
                for j in range(2):
                    copy(t, 1 + j, me, (*chips[j], c), from_src=True).start()

        _when(first, start)

    def finish(self, mid, late, last, src, out, sems):
        copy, relay, local, me, sibling, chips, c = self._plan(src, out, sems)

        def pass_on():
            for t in range(len(self.pieces)):
                for j in range(2):
                    copy(t, 1 + j, (*chips[j], c), me).wait_recv()
                relay(t).start()
                for j in range(2):
                    copy(t, 4 + j, (*chips[j], c), sibling).start()

        def pass_on_relayed():
            for t in range(len(self.pieces)):
                copy(t, 3, (*chips[2], c), me).wait_recv()
                copy(t, 6, (*chips[2], c), sibling).start()

        def drain():
            for t in range(len(self.pieces)):
                copy(t, 0, sibling, me).wait_recv()
                for j in range(3):
                    copy(t, 4 + j, (*chips[j], 1 - c), me).wait_recv()
            for t in range(len(self.pieces)):
                copy(t, 0, me, sibling, from_src=True).wait_send()
                for j in range(2):
                    copy(t, 1 + j, me, (*chips[j], c), from_src=True).wait_send()
                relay(t).wait_send()
                for j in range(3):
                    copy(t, 4 + j, (*chips[j], c), sibling).wait_send()
            for cp in local:
                cp.wait()

        _when(mid, pass_on)
        _when(late, pass_on_relayed)
        _when(last, drain)


class _ScatterRide:
    def __init__(self, chip_sums):
        self.args = list(chip_sums)
        n = self.n = len(chip_sums)
        self.out_shape = [pltpu.HBM(p.shape, p.dtype) for p in chip_sums]
        self.scratch = [pltpu.SemaphoreType.DMA((n, 3)), pltpu.SemaphoreType.DMA((n, 3)), pltpu.SemaphoreType.DMA((n,))]

    def _plan(self, src, out, sems):
        send_sems, recv_sems, local_sem = sems
        x, y, c = _my_position()

        def peer(k):
            return (x ^ (k >> 1), y ^ (k & 1))

        copies = [pltpu.make_async_remote_copy(
            src_ref=src[t].at[2 * peer(k)[0] + peer(k)[1]], dst_ref=out[t].at[k],
            send_sem=send_sems.at[t, k - 1], recv_sem=recv_sems.at[t, k - 1],
            device_id=(*peer(k), c), device_id_type=MESH) for t in range(self.n) for k in range(1, N_DEV // 2)]
        local = [pltpu.make_async_copy(src[t].at[2 * x + y], out[t].at[0], local_sem.at[t]) for t in range(self.n)]
        return copies, local

    def begin(self, first, src, out, sems):
        copies, local = self._plan(src, out, sems)

        def start():
            for cp in local + copies:
                cp.start()

        _when(first, start)

    def finish(self, mid, late, last, src, out, sems):
        copies, local = self._plan(src, out, sems)

        def drain():
            for cp in copies:
                cp.wait_recv()
            for cp in copies:
                cp.wait_send()
            for cp in local:
                cp.wait()

        _when(last, drain)


def _in_hbm(a):
    return pltpu.with_memory_space_constraint(a, pltpu.HBM)


def _call(body, *, name, grid, in_specs, out_specs, out_shape, scratch_shapes, vmem_bytes, args, ride=None, after=None):
    n_in, n_out, n_s = len(in_specs), len(out_specs), len(scratch_shapes)
    params = _params(vmem_bytes, len(grid))
    args = [_in_hbm(a) for a in args]
    out_shape = [pltpu.HBM(s.shape, s.dtype) for s in out_shape]
    if ride is None:
        if after is not None:
            def ordered_body(*refs):
                body(*refs[:n_in], *refs[n_in + 1:])
            outs = pl.pallas_call(ordered_body, name=name, grid=grid, in_specs=list(in_specs) + [ANY], out_specs=out_specs,
                                  out_shape=out_shape, scratch_shapes=scratch_shapes, compiler_params=params)(*args, after)
            return list(outs), []
        outs = pl.pallas_call(body, name=name, grid=grid, in_specs=in_specs, out_specs=out_specs, out_shape=out_shape,
                              scratch_shapes=scratch_shapes, compiler_params=params)(*args)
        return list(outs), []
    total = int(np.prod(grid))

    def riding_body(*refs):
        a = n_in
        b = a + ride.n
        c = b + n_out
        d = c + ride.n
        e = d + n_s
        step = pl.program_id(0)
        for axis in range(1, len(grid)):
            step = step * grid[axis] + pl.program_id(axis)
        ride.begin(step == 0, refs[a:b], refs[c:d], refs[e:])
        body(*refs[:a], *refs[b:c], *refs[d:e])
        ride.finish(step == total // 2, step == (3 * total) // 4, step == total - 1, refs[a:b], refs[c:d], refs[e:])

    outs = pl.pallas_call(
        riding_body, name=name, grid=grid, in_specs=list(in_specs) + [ANY] * ride.n,
        out_specs=list(out_specs) + [ANY] * ride.n, out_shape=list(out_shape) + ride.out_shape,
        scratch_shapes=list(scratch_shapes) + ride.scratch, compiler_params=params)(*args, *[_in_hbm(a) for a in ride.args])
    return list(outs[:n_out]), list(outs[n_out:])


def _alone(ride, name):
    def body(*refs):
        src, out, sems = refs[:ride.n], refs[ride.n:2 * ride.n], refs[2 * ride.n:]
        ride.begin(True, src, out, sems)
        ride.finish(True, True, True, src, out, sems)

    return list(pl.pallas_call(body, name=name, out_shape=ride.out_shape, in_specs=[ANY] * ride.n,
                               out_specs=[ANY] * ride.n, scratch_shapes=ride.scratch)(*[_in_hbm(a) for a in ride.args]))


def _scatter_copies(src, land, send_sems, recv_sems):
    x, y, c = _my_position()
    copies = []
    for k in range(1, N_DEV // 2):
        px, py = x ^ (k >> 1), y ^ (k & 1)
        copies.append(pltpu.make_async_remote_copy(
            src_ref=src.at[2 * px + py], dst_ref=land.at[k - 1], send_sem=send_sems.at[k - 1], recv_sem=recv_sems.at[k - 1],
            device_id=(px, py, c), device_id_type=MESH))
    return copies


def _scatter_start(chip_sums, name):
    n_peers = N_DEV // 2 - 1
    land_shape = (n_peers,) + chip_sums.shape[1:]
    hbm = pl.BlockSpec(memory_space=pltpu.HBM)
    sem = pl.BlockSpec(memory_space=pltpu.SEMAPHORE)

    def body(src_ref, land_ref, send_sems, recv_sems, src_thru, land_thru, token):
        for cp in _scatter_copies(src_ref, land_ref, send_sems, recv_sems):
            cp.start()
        token[...] = jnp.zeros_like(token)

    send_sems, recv_sems, src_thru, land_thru, token = pl.pallas_call(
        body, name=name,
        out_shape=(pltpu.SemaphoreType.DMA((n_peers,)), pltpu.SemaphoreType.DMA((n_peers,)),
                   pltpu.HBM(chip_sums.shape, chip_sums.dtype), pltpu.HBM(land_shape, chip_sums.dtype),
                   jax.ShapeDtypeStruct((8, 128), F32)),
        in_specs=(hbm, hbm), out_specs=(sem, sem, hbm, hbm, pl.BlockSpec(memory_space=pltpu.VMEM)),
        input_output_aliases={0: 2, 1: 3},
        compiler_params=pltpu.CompilerParams(has_side_effects=pltpu.SideEffectType.DATAFLOW_SIDE_EFFECTING),
    )(_in_hbm(chip_sums), _in_hbm(lax.empty(land_shape, chip_sums.dtype)))
    return (send_sems, recv_sems, src_thru, land_thru), token


def _scatter_wait(state, after, name):
    send_sems, recv_sems, src_thru, land_thru = state
    hbm = pl.BlockSpec(memory_space=pltpu.HBM)
    sem = pl.BlockSpec(memory_space=pltpu.SEMAPHORE)

    def body(src_ref, land_ref, send_sems, recv_sems, after_ref, src_out, land_out):
        for cp in _scatter_copies(src_ref, land_ref, send_sems, recv_sems):
            cp.wait_send()
            cp.wait_recv()

    src_done, land_done = pl.pallas_call(
        body, name=name,
        out_shape=(pltpu.HBM(src_thru.shape, src_thru.dtype), pltpu.HBM(land_thru.shape, land_thru.dtype)),
        in_specs=(hbm, hbm, sem, sem, ANY), out_specs=(hbm, hbm), input_output_aliases={0: 0, 1: 1},
        compiler_params=pltpu.CompilerParams(has_side_effects=pltpu.SideEffectType.DATAFLOW_SIDE_EFFECTING),
    )(src_thru, land_thru, send_sems, recv_sems, after)
    x, y, _ = _my_position()
    return lax.dynamic_slice_in_dim(src_done, 2 * x + y, 1, axis=0), land_done


def _load_ffn_weights(win_hbm, wout_hbm, win, wout, sem):
    a = pltpu.make_async_copy(win_hbm, win, sem.at[0])
    b = pltpu.make_async_copy(wout_hbm, wout, sem.at[1])
    a.start()
    b.start()
    a.wait()
    b.wait()


def _ffn_forward(h_in, gain, win8, wout, name, head=None, ride=None):
    tm, nt = WIDE_TILE, SEQ // WIDE_TILE

    def body(*refs):
        if head is None:
            x_ref, g_ref, win_hbm, wout_hbm, out_ref, gu_ref, win, wout, sem = refs
        else:
            x_ref, g_ref, win_hbm, wout_hbm, tgt_ref, gf_ref, out_ref, gu_ref, loss_ref, dgf_ref, win, wout, sem = refs
        i = pl.program_id(0)

        @pl.when(i == 0)
        def _():
            _load_ffn_weights(win_hbm, wout_hbm, win, wout, sem)
            if head is not None:
                loss_ref[...] = jnp.zeros_like(loss_ref)
                dgf_ref[...] = jnp.zeros_like(dgf_ref)

        x = x_ref[...]
        xn, _, _ = _rms(x, g_ref[...])
        xb = xn.astype(BF16)
        acc = jnp.zeros((tm, D_MODEL), F32)
        for j in range(N_FF_GROUPS):
            gate = _dot_nt(xb, win[j])
            up = _dot_nt(xb, win[j + N_FF_GROUPS])
            gu_ref[j] = gate.astype(BF16)
            gu_ref[j + N_FF_GROUPS] = up.astype(BF16)
            act = gate * _sig(gate) * up
            acc = acc + _dot(act.astype(BF16), wout[j])
        h = x + FFN_RES_WEIGHT * acc
        if head is None:
            out_ref[...] = h
        else:
            gf = gf_ref[...]
            y, hhat, r = _rms(h, gf)
            err = y - tgt_ref[...]
            loss_ref[...] += jnp.full(loss_ref.shape, 0.5 / D_MODEL * jnp.sum(err * err), F32)
            dy = err * (1.0 / D_MODEL)
            dgf_ref[...] += jnp.sum(dy * hhat, axis=0, keepdims=True)
            out_ref[...] = _rms_bwd(dy * gf, hhat, r)

    weights = 2 * D_MODEL * 2 * D_FF + 2 * D_FF * D_MODEL
    tiles = 2 * (2 * 4 * tm * D_MODEL + 2 * tm * 2 * D_FF) + (2 * 4 * tm * D_MODEL if head else 0)
    in_specs = [_row_spec(tm, D_MODEL), _full_spec((1, D_MODEL)), ANY, ANY]
    out_shape = [jax.ShapeDtypeStruct((SEQ, D_MODEL), F32), jax.ShapeDtypeStruct((N_DEV, SEQ, FF_SHARD), BF16)]
    out_specs = [_row_spec(tm, D_MODEL), pl.BlockSpec((N_DEV, tm, FF_SHARD), lambda i: (0, i, 0))]
    args = [h_in, gain, win8, wout]
    if head is not None:
        in_specs += [_row_spec(tm, D_MODEL), _full_spec((1, D_MODEL))]
        out_shape += [jax.ShapeDtypeStruct((1, 128), F32), jax.ShapeDtypeStruct((1, D_MODEL), F32)]
        out_specs += [_full_spec((1, 128)), _full_spec((1, D_MODEL))]
        args += list(head)
    return _call(
        body, name=name, grid=(nt,), in_specs=in_specs, out_specs=out_specs, out_shape=out_shape,
        scratch_shapes=[pltpu.VMEM((N_DEV, FF_SHARD, D_MODEL), BF16), pltpu.VMEM((N_FF_GROUPS, FF_SHARD, D_MODEL), BF16),
                        pltpu.SemaphoreType.DMA((2,))],
        vmem_bytes=weights + tiles + 16 * tm * FF_SHARD * 4, args=args, ride=ride)


def _ffn_backward(dh_out, h_in, gain, gu, win8, wout, name, after=None):
    tm, nt = TOKEN_TILE, SEQ // TOKEN_TILE

    def body(dh_ref, x_ref, g_ref, gu_ref, win_hbm, wout_hbm,
             dhin_ref, dgu_ref, act_ref, xn_ref, df_ref, dg_ref, win, wout, sem):
        i = pl.program_id(0)

        @pl.when(i == 0)
        def _():
            _load_ffn_weights(win_hbm, wout_hbm, win, wout, sem)
            dg_ref[...] = jnp.zeros_like(dg_ref)

        dh = dh_ref[...]
        g = g_ref[...]
        xn, xhat, r = _rms(x_ref[...], g)
        df = (FFN_RES_WEIGHT * dh).astype(BF16)
        dxn = jnp.zeros((tm, D_MODEL), F32)
        for j in range(N_FF_GROUPS):
            gate = gu_ref[j].astype(F32)
            up = gu_ref[j + N_FF_GROUPS].astype(F32)
            dact = _dot_nt(df, wout[j])
            s = _sig(gate)
            silu = gate * s
            dgate = (dact * up * (s * (1.0 + gate * (1.0 - s)))).astype(BF16)
            dup = (dact * silu).astype(BF16)
            act_ref[j] = (silu * up).astype(BF16)
            dgu_ref[j] = dgate
            dgu_ref[j + N_FF_GROUPS] = dup
            dxn = dxn + _dot(dgate, win[j]) + _dot(dup, win[j + N_FF_GROUPS])
        dg_ref[...] += jnp.sum(dxn * xhat, axis=0, keepdims=True)
        dhin_ref[...] = dh + _rms_bwd(dxn * g, xhat, r)
        xn_ref[...] = xn.astype(BF16)
        df_ref[...] = df

    weights = 2 * D_MODEL * 2 * D_FF + 2 * D_FF * D_MODEL
    tiles = 2 * (3 * 4 * tm * D_MODEL + 2 * tm * (2 * 2 * D_FF + D_FF) + 2 * 2 * tm * D_MODEL)
    gu_spec = pl.BlockSpec((N_DEV, tm, FF_SHARD), lambda i: (0, i, 0))
    return _call(
        body, name=name, grid=(nt,),
        in_specs=[_row_spec(tm, D_MODEL), _row_spec(tm, D_MODEL), _full_spec((1, D_MODEL)), gu_spec, ANY, ANY],
        out_specs=[_row_spec(tm, D_MODEL), gu_spec, pl.BlockSpec((N_FF_GROUPS, tm, FF_SHARD), lambda i: (0, i, 0)),
                   _row_spec(tm, D_MODEL), _row_spec(tm, D_MODEL), _full_spec((1, D_MODEL))],
        out_shape=[jax.ShapeDtypeStruct((SEQ, D_MODEL), F32), jax.ShapeDtypeStruct((N_DEV, SEQ, FF_SHARD), BF16),
                   jax.ShapeDtypeStruct((N_FF_GROUPS, SEQ, FF_SHARD), BF16), jax.ShapeDtypeStruct((SEQ, D_MODEL), BF16),
                   jax.ShapeDtypeStruct((SEQ, D_MODEL), BF16), jax.ShapeDtypeStruct((1, D_MODEL), F32)],
        scratch_shapes=[pltpu.VMEM((N_DEV, FF_SHARD, D_MODEL), BF16), pltpu.VMEM((N_FF_GROUPS, FF_SHARD, D_MODEL), BF16),
                        pltpu.SemaphoreType.DMA((2,))],
        vmem_bytes=weights + tiles + 20 * tm * FF_SHARD * 4, args=[dh_out, h_in, gain, gu, win8, wout], after=after)[0]


def _to_sibling(src, dst, send_sem, recv_sem):
    x, y, c = _my_position()
    return pltpu.make_async_remote_copy(src_ref=src, dst_ref=dst, send_sem=send_sem, recv_sem=recv_sem,
                                        device_id=(x, y, 1 - c), device_id_type=MESH)


def _weight_grad(x, g, n_out, x_spec, g_spec, k_dim, n_dim, name, halves=False, tt=2048, ride=None, after=None):
    nt = SEQ // tt
    n_chips = N_DEV // 2
    rows = k_dim // 2 if halves else k_dim

    def body(x_ref, g_ref, out_ref, acc, sendbuf, recvbuf, send_sems, recv_sems):
        b, t = pl.program_id(0), pl.program_id(1)
        c = lax.axis_index("c")

        def push(q):
            return _to_sibling(sendbuf.at[q], recvbuf.at[q], send_sems.at[q], recv_sems.at[q])

        @pl.when(t == 0)
        def _():
            acc[...] = jnp.zeros_like(acc)

        acc[...] += _dot_tn(x_ref[...], g_ref[...])

        @pl.when(t == nt - 1)
        def _():
            if halves:
                for mine, other in ((0, 1), (1, 0)):
                    @pl.when(c == mine)
                    def _():
                        out_ref[b] = acc[pl.ds(mine * rows, rows), :].astype(BF16)
                        sendbuf[b] = acc[pl.ds(other * rows, rows), :].astype(BF16)
                push(b).start()
            else:
                q = b // 2

                @pl.when(b % 2 == c)
                def _():
                    out_ref[q] = acc[...].astype(BF16)

                @pl.when(b % 2 != c)
                def _():
                    sendbuf[q] = acc[...].astype(BF16)
                    push(q).start()

        @pl.when((b == n_out - 1) & (t == nt - 1))
        def _():
            for q in range(n_chips):
                push(q).wait_recv()
                out_ref[q] = (out_ref[q].astype(F32) + recvbuf[q].astype(F32)).astype(BF16)
            for q in range(n_chips):
                push(q).wait_send()

    piece = (n_chips, rows, n_dim)
    outs, ride_outs = _call(
        body, name=name, grid=(n_out, nt), in_specs=[x_spec(tt), g_spec(tt)],
        out_specs=[pl.BlockSpec(piece, lambda b, t: (0, 0, 0))],
        out_shape=[jax.ShapeDtypeStruct(piece, BF16)],
        scratch_shapes=[pltpu.VMEM((k_dim, n_dim), F32), pltpu.VMEM(piece, BF16), pltpu.VMEM(piece, BF16),
                        pltpu.SemaphoreType.DMA((n_chips,)), pltpu.SemaphoreType.DMA((n_chips,))],
        vmem_bytes=2 * 2 * tt * (k_dim + n_dim) + 8 * k_dim * n_dim + 4 * 2 * n_chips * rows * n_dim, args=[x, g], ride=ride,
        after=after)
    return outs[0], ride_outs


def _ffn_w_out_grad(act, df, tag, ride=None, after=None):
    return _weight_grad(
        act, df, N_FF_GROUPS,
        lambda tt: pl.BlockSpec((None, tt, FF_SHARD), lambda b, t: (b, t, 0)),
        lambda tt: pl.BlockSpec((tt, D_MODEL), lambda b, t: (t, 0)),
        FF_SHARD, D_MODEL, name=f"ffn{tag}_w_out_grad", halves=True, ride=ride, after=after)


def _ffn_w_in_grad(xn, dgu, tag, ride=None):
    return _weight_grad(
        dgu, xn, N_DEV,
        lambda tt: pl.BlockSpec((None, tt, FF_SHARD), lambda b, t: (b, t, 0)),
        lambda tt: pl.BlockSpec((tt, D_MODEL), lambda b, t: (t, 0)),
        FF_SHARD, D_MODEL, name=f"ffn{tag}_w_in_grad", ride=ride)


def _load_mix_weight(wmix_hbm, wmix, sem):
    copies = [pltpu.make_async_copy(wmix_hbm.at[d], wmix.at[:, pl.ds(d * MIX_SHARD, MIX_SHARD)], sem.at[d])
              for d in range(N_DEV)]
    for cp in copies:
        cp.start()
    for cp in copies:
        cp.wait()


def _load_pool_weight(pw_hbm, pw, sem):
    rows = POOL_GROUP_DIM // N_DEV
    copies = [pltpu.make_async_copy(pw_hbm.at[d], pw.at[:, pl.ds(d * rows, rows), :], sem.at[d]) for d in range(N_DEV)]
    for cp in copies:
        cp.start()
    for cp in copies:
        cp.wait()


def _rotate(x1, x2, cos, sin):
    return x1 * cos - x2 * sin, x1 * sin + x2 * cos


def _mix_proj_forward(h1, gain, wmix8, cos, sin, ride=None):
    tm, nt = WIDE_TILE, SEQ // WIDE_TILE
    k_scale = HEAD_DIM ** -0.5

    def body(h_ref, g_ref, wmix_hbm, cos_ref, sin_ref, u_ref, qkvg_ref, p_ref, gates_ref, wmix, sem):
        @pl.when(pl.program_id(0) == 0)
        def _():
            _load_mix_weight(wmix_hbm, wmix, sem)

        u = _rms(h_ref[...], g_ref[...])[0].astype(BF16)
        u_ref[...] = u
        cos_t, sin_t = cos_ref[...], sin_ref[...]
        for seg in range(N_SEG):
            pr = _dot(u, wmix[:, pl.ds(seg * D_MODEL, D_MODEL)])
            if seg < 2:
                scale = 1.0 if seg == 0 else k_scale
                for hd in range(HEADS):
                    lo = hd * HEAD_DIM
                    o1, o2 = _rotate(pr[:, lo:lo + ROT_HALF], pr[:, lo + ROT_HALF:lo + HEAD_DIM], cos_t, sin_t)
                    qkvg_ref[:, pl.ds(seg * D_MODEL + lo, ROT_HALF)] = (o1 * scale).astype(BF16)
                    qkvg_ref[:, pl.ds(seg * D_MODEL + lo + ROT_HALF, ROT_HALF)] = (o2 * scale).astype(BF16)
            elif seg < 4:
                qkvg_ref[:, pl.ds(seg * D_MODEL, D_MODEL)] = pr.astype(BF16)
            elif seg == 4:
                p_ref[...] = pr
            else:
                gates_ref[:, pl.ds((seg - 5) * D_MODEL, D_MODEL)] = pr.astype(BF16)

    est = 2 * D_MODEL * N_SEG * D_MODEL + 2 * tm * (4 * D_MODEL + 2 * D_MODEL + 2 * 4 * D_MODEL + 4 * D_MODEL + 2 * 2 * D_MODEL)
    return _call(
        body, name="mix_proj_fwd", grid=(nt,),
        in_specs=[_row_spec(tm, D_MODEL), _full_spec((1, D_MODEL)), ANY, _row_spec(tm, ROT_HALF), _row_spec(tm, ROT_HALF)],
        out_specs=[_row_spec(tm, D_MODEL), _row_spec(tm, 4 * D_MODEL), _row_spec(tm, D_MODEL), _row_spec(tm, 2 * D_MODEL)],
        out_shape=[jax.ShapeDtypeStruct((SEQ, D_MODEL), BF16), jax.ShapeDtypeStruct((SEQ, 4 * D_MODEL), BF16),
                   jax.ShapeDtypeStruct((SEQ, D_MODEL), F32), jax.ShapeDtypeStruct((SEQ, 2 * D_MODEL), BF16)],
        scratch_shapes=[pltpu.VMEM((D_MODEL, N_SEG * D_MODEL), BF16), pltpu.SemaphoreType.DMA((N_DEV,))],
        vmem_bytes=est + 8 * tm * D_MODEL * 4, args=[h1, gain, wmix8, cos, sin], ride=ride)


def _seg_block_spec(seg, reverse=False):
    nb = SEQ // RET_BLOCK
    if reverse:
        return pl.BlockSpec((RET_BLOCK, D_MODEL), lambda i, s=seg: (nb - 1 - i, s))
    return pl.BlockSpec((RET_BLOCK, D_MODEL), lambda i, s=seg: (i, s))


def _table_specs():
    return [_full_spec((HEADS, RET_BLOCK, RET_BLOCK)), _full_spec((HEADS, RET_BLOCK, 1)),
            _full_spec((HEADS, RET_BLOCK, 1)), _full_spec((HEADS, 1, 1))]


def _head_cols(h):
    return pl.ds(h * HEAD_DIM, HEAD_DIM)


def _retention_forward(qkvg, tables, ride=None):
    nb = SEQ // RET_BLOCK

    def body(q_ref, k_ref, v_ref, gr_ref, mask_ref, qdec_ref, kdec_ref, cdec_ref, ret_ref, o_ref, state):
        @pl.when(pl.program_id(0) == 0)
        def _():
            state[...] = jnp.zeros_like(state)

        for h in range(HEADS):
            cols = _head_cols(h)
            q, k, v = q_ref[:, cols], k_ref[:, cols], v_ref[:, cols]
            scores = _dot_nt(q, k) * mask_ref[h]
            inner = _dot(scores.astype(BF16), v)
            cross = _dot((q.astype(F32) * qdec_ref[h]).astype(BF16), state[h].astype(BF16))
            ret = inner + cross
            state[h] = state[h] * cdec_ref[h] + _dot_tn((k.astype(F32) * kdec_ref[h]).astype(BF16), v)
            ret_ref[:, cols] = ret
            retn = ret * lax.rsqrt(jnp.mean(ret * ret, axis=-1, keepdims=True) + NORM_EPS)
            gr = gr_ref[:, cols].astype(F32)
            o_ref[:, cols] = (retn * (gr * _sig(gr))).astype(BF16)

    return _call(
        body, name="retention_fwd", grid=(nb,),
        in_specs=[_seg_block_spec(0), _seg_block_spec(1), _seg_block_spec(2), _seg_block_spec(3)] + _table_specs(),
        out_specs=[_row_spec(RET_BLOCK, D_MODEL)] * 2,
        out_shape=[jax.ShapeDtypeStruct((SEQ, D_MODEL), F32), jax.ShapeDtypeStruct((SEQ, D_MODEL), BF16)],
        scratch_shapes=[pltpu.VMEM((HEADS, HEAD_DIM, HEAD_DIM), F32)],
        vmem_bytes=24 * RET_BLOCK * D_MODEL * 4, args=[qkvg, qkvg, qkvg, qkvg, *tables], ride=ride)


def _retention_backward_q(qkvg, dret, tables, ride=None):
    nb = SEQ // RET_BLOCK

    def body(k_ref, v_ref, do_ref, mask_ref, qdec_ref, kdec_ref, cdec_ref, dq_ref, state):
        @pl.when(pl.program_id(0) == 0)
        def _():
            state[...] = jnp.zeros_like(state)

        for h in range(HEADS):
            cols = _head_cols(h)
            k, v, do = k_ref[:, cols], v_ref[:, cols], do_ref[:, cols]
            dscores = _dot_nt(do, v) * mask_ref[h]
            dq_ref[:, cols] = _dot(dscores.astype(BF16), k) + _dot_nt(do, state[h].astype(BF16)) * qdec_ref[h]
            state[h] = state[h] * cdec_ref[h] + _dot_tn((k.astype(F32) * kdec_ref[h]).astype(BF16), v)

    return _call(
        body, name="retention_bwd_q", grid=(nb,),
        in_specs=[_seg_block_spec(1), _seg_block_spec(2), _row_spec(RET_BLOCK, D_MODEL)] + _table_specs(),
        out_specs=[_row_spec(RET_BLOCK, D_MODEL)],
        out_shape=[jax.ShapeDtypeStruct((SEQ, D_MODEL), F32)],
        scratch_shapes=[pltpu.VMEM((HEADS, HEAD_DIM, HEAD_DIM), F32)],
        vmem_bytes=24 * RET_BLOCK * D_MODEL * 4, args=[qkvg, qkvg, dret, *tables], ride=ride)


def _retention_backward_kv(qkvg, dret, tables, ride=None):
    nb = SEQ // RET_BLOCK

    def body(q_ref, k_ref, v_ref, do_ref, mask_ref, qdec_ref, kdec_ref, cdec_ref, dk_ref, dv_ref, gstate):
        @pl.when(pl.program_id(0) == 0)
        def _():
            gstate[...] = jnp.zeros_like(gstate)

        for h in range(HEADS):
            cols = _head_cols(h)
            q, k, v, do = q_ref[:, cols], k_ref[:, cols], v_ref[:, cols], do_ref[:, cols]
            mask = mask_ref[h]
            scores = (_dot_nt(q, k) * mask).astype(BF16)
            dscores = (_dot_nt(do, v) * mask).astype(BF16)
            gs = gstate[h].astype(BF16)
            dk_ref[:, cols] = _dot_tn(dscores, q) + _dot_nt(v, gs) * kdec_ref[h]
            dv_ref[:, cols] = _dot_tn(scores, do) + _dot((k.astype(F32) * kdec_ref[h]).astype(BF16), gs)
            gstate[h] = gstate[h] * cdec_ref[h] + _dot_tn((q.astype(F32) * qdec_ref[h]).astype(BF16), do)

    rev = lambda: pl.BlockSpec((RET_BLOCK, D_MODEL), lambda i: (nb - 1 - i, 0))
    return _call(
        body, name="retention_bwd_kv", grid=(nb,),
        in_specs=[_seg_block_spec(0, True), _seg_block_spec(1, True), _seg_block_spec(2, True), rev()] + _table_specs(),
        out_specs=[rev(), rev()],
        out_shape=[jax.ShapeDtypeStruct((SEQ, D_MODEL), F32)] * 2,
        scratch_shapes=[pltpu.VMEM((HEADS, HEAD_DIM, HEAD_DIM), F32)],
        vmem_bytes=32 * RET_BLOCK * D_MODEL * 4, args=[qkvg, qkvg, qkvg, dret, *tables], ride=ride)


def _pooled(p_ext, first_row):
    rows = p_ext.shape[0]
    t = first_row + lax.broadcasted_iota(jnp.int32, (rows - HALO, 1), 0)
    outs = []
    for g, w in enumerate(POOL_WINDOWS):
        e = p_ext[:, g * POOL_GROUP_DIM:(g + 1) * POOL_GROUP_DIM]
        s, span = e, 1
        while span < w:
            s = s + pltpu.roll(s, span, 0)
            span *= 2
        count = jnp.minimum(t + 1, w).astype(F32)
        outs.append(s[HALO:] / count - e[HALO:])
    return outs


def _pooled_transpose(d_ext, first_row):
    rows = d_ext.shape[0]
    t = first_row + lax.broadcasted_iota(jnp.int32, (rows, 1), 0)
    outs = []
    for g, w in enumerate(POOL_WINDOWS):
        d = d_ext[:, g * POOL_GROUP_DIM:(g + 1) * POOL_GROUP_DIM]
        e = jnp.where(t < SEQ, d / jnp.minimum(t + 1, w).astype(F32), 0.0)
        s, span = e, 1
        while span < w:
            s = s + pltpu.roll(s, rows - span, 0)
            span *= 2
        outs.append(s[:rows - HALO] - d[:rows - HALO])
    return outs


def _mix_tail_specs(tm):
    halo_blocks = tm // HALO
    return [
        _row_spec(tm, D_MODEL),
        pl.BlockSpec((HALO, D_MODEL), lambda i: (jnp.maximum(i * halo_blocks - 1, 0), 0)),
        _row_spec(tm, 2 * D_MODEL),
        _row_spec(tm, D_MODEL),
        _full_spec((2, D_MODEL)), _full_spec((1, D_MODEL)), ANY,
        _full_spec((D_MODEL, D_MODEL)), _full_spec((D_MODEL, D_MODEL)), _full_spec((D_MODEL, D_MODEL)),
    ]


def _mix_tail_compute(i, tm, p_ref, halo_ref, gates_ref, oret_ref, bias_ref, scale_ref, pw, wru_ref, wpu_ref, saved=None):
    halo = jnp.where(i > 0, halo_ref[...], 0.0)
    pooled = _pooled(jnp.concatenate([halo, p_ref[...]], axis=0), i * tm)
    pooled = [x.astype(BF16) for x in pooled]
    mixed = jnp.concatenate([_dot(pooled[g], pw[g]) for g in range(len(POOL_WINDOWS))], axis=-1)
    pool_out = (mixed * scale_ref[...]).astype(BF16)
    o_ret = oret_ref[...]
    if saved is None:
        a = _dot(o_ret, wru_ref[...])
        b = _dot(pool_out, wpu_ref[...])
    else:
        a, b = saved[0][...].astype(F32), saved[1][...].astype(F32)
    z = gates_ref[...].astype(F32)
    g0 = _sig(z[:, :D_MODEL] + bias_ref[0:1, :])
    g1 = _sig(z[:, D_MODEL:] + bias_ref[1:2, :])
    merged = (g0 * a + g1 * b).astype(BF16)
    return pooled, mixed, pool_out, o_ret, a, b, g0, g1, merged


def _mix_tail_forward(p, gates, o_ret, h1, bias, scale, pw8, wru, wpu, wo, ride=None):
    tm, nt = TOKEN_TILE, SEQ // TOKEN_TILE

    def body(p_ref, halo_ref, gates_ref, oret_ref, bias_ref, scale_ref, pw_hbm, wru_ref, wpu_ref, wo_ref, h1_ref,
             h2_ref, a_ref, b_ref, pw, sem):
        i = pl.program_id(0)

        @pl.when(i == 0)
        def _():
            _load_pool_weight(pw_hbm, pw, sem)

        out = _mix_tail_compute(i, tm, p_ref, halo_ref, gates_ref, oret_ref, bias_ref, scale_ref, pw, wru_ref, wpu_ref)
        a_ref[...] = out[4].astype(BF16)
        b_ref[...] = out[5].astype(BF16)
        h2_ref[...] = h1_ref[...] + _dot(out[-1], wo_ref[...])

    est = 3 * 2 * 2 * D_MODEL * D_MODEL + 2 * tm * D_MODEL * (4 + 4 + 2 + 4 + 4) + 16 * tm * D_MODEL * 4
    return _call(
        body, name="mix_tail_fwd", grid=(nt,),
        in_specs=_mix_tail_specs(tm) + [_row_spec(tm, D_MODEL)],
        out_specs=[_row_spec(tm, D_MODEL)] * 3,
        out_shape=[jax.ShapeDtypeStruct((SEQ, D_MODEL), F32)] + [jax.ShapeDtypeStruct((SEQ, D_MODEL), BF16)] * 2,
        scratch_shapes=[pltpu.VMEM((len(POOL_WINDOWS), POOL_GROUP_DIM, POOL_GROUP_DIM), BF16), pltpu.SemaphoreType.DMA((N_DEV,))],
        vmem_bytes=est, args=[p, p, gates, o_ret, bias, scale, pw8, wru, wpu, wo, h1], ride=ride)


def _mix_tail_backward(dh2, p, gates, o_ret, ret, qkvg, a_saved, b_saved, bias, scale, pw8, wru, wpu, wo, ride=None):
    tm, nt = TOKEN_TILE, SEQ // TOKEN_TILE
    n_groups = len(POOL_WINDOWS)
    rows_per_dev = POOL_GROUP_DIM // N_DEV

    def body(p_ref, halo_ref, gates_ref, oret_ref, bias_ref, scale_ref, pw_hbm, wru_ref, wpu_ref, wo_ref,
             dh2_ref, ret_ref, gr_ref, a_ref, b_ref,
             dret_ref, dgr_ref, dgates_ref, dpooled_ref, dwo_ref, dwru_ref, dwpu_ref, dpw_ref, dbias_ref, dscale_ref,
             pw, sem, acc_wo, acc_wru, acc_wpu, acc_pw, send_sq, recv_sq, send_pw, recv_pw, send_sems, recv_sems):
        i = pl.program_id(0)

        @pl.when(i == 0)
        def _():
            _load_pool_weight(pw_hbm, pw, sem)
            for ref in (acc_wo, acc_wru, acc_wpu, acc_pw, dbias_ref, dscale_ref):
                ref[...] = jnp.zeros_like(ref)

        pooled, mixed, pool_out, o_ret, a, b, g0, g1, merged = _mix_tail_compute(
            i, tm, p_ref, halo_ref, gates_ref, oret_ref, bias_ref, scale_ref, pw, wru_ref, wpu_ref, saved=(a_ref, b_ref))
        dh2 = dh2_ref[...].astype(BF16)
        dm = _dot_nt(dh2, wo_ref[...])
        acc_wo[...] += _dot_tn(merged, dh2)
        da = (dm * g0).astype(BF16)
        db = (dm * g1).astype(BF16)
        dz0 = dm * a * g0 * (1.0 - g0)
        dz1 = dm * b * g1 * (1.0 - g1)
        dbias_ref[0:1, :] += jnp.sum(dz0, axis=0, keepdims=True)
        dbias_ref[1:2, :] += jnp.sum(dz1, axis=0, keepdims=True)
        dgates_ref[:, pl.ds(0, D_MODEL)] = dz0.astype(BF16)
        dgates_ref[:, pl.ds(D_MODEL, D_MODEL)] = dz1.astype(BF16)
        acc_wru[...] += _dot_tn(o_ret, da)
        acc_wpu[...] += _dot_tn(pool_out, db)
        d_oret = _dot_nt(da, wru_ref[...])
        d_pool_out = _dot_nt(db, wpu_ref[...])
        dscale_ref[...] += jnp.sum(d_pool_out * mixed, axis=0, keepdims=True)
        dmixed = (d_pool_out * scale_ref[...]).astype(BF16)
        for g in range(n_groups):
            dmg = dmixed[:, g * POOL_GROUP_DIM:(g + 1) * POOL_GROUP_DIM]
            acc_pw[g] += _dot_tn(pooled[g], dmg)
            dpooled_ref[:, pl.ds(g * POOL_GROUP_DIM, POOL_GROUP_DIM)] = _dot_nt(dmg, pw[g])
        gr = gr_ref[...].astype(F32)
        s = _sig(gr)
        silu = gr * s
        for hd in range(HEADS):
            cols = slice(hd * HEAD_DIM, (hd + 1) * HEAD_DIM)
            r_h = ret_ref[:, cols]
            rr = lax.rsqrt(jnp.mean(r_h * r_h, axis=-1, keepdims=True) + NORM_EPS)
            rhat = r_h * rr
            do_h = d_oret[:, cols]
            dgr_ref[:, cols] = (do_h * rhat * (s[:, cols] * (1.0 + gr[:, cols] * (1.0 - s[:, cols])))).astype(BF16)
            dret_ref[:, cols] = _rms_bwd(do_h * silu[:, cols], rhat, rr).astype(BF16)

        @pl.when(i == nt - 1)
        def _():
            c = lax.axis_index("c")
            rows = D_MODEL // N_DEV
            squares = ((acc_wo, dwo_ref), (acc_wru, dwru_ref), (acc_wpu, dwpu_ref))
            for q in range(n_chips):
                own = pl.multiple_of((2 * q + c) * rows, rows)
                other = pl.multiple_of((2 * q + 1 - c) * rows, rows)
                for t, (acc, out) in enumerate(squares):
                    out[q] = acc[pl.ds(own, rows), :].astype(BF16)
                    send_sq[t, q] = acc[pl.ds(other, rows), :].astype(BF16)
                own_pw = pl.multiple_of((2 * q + c) * rows_per_dev, rows_per_dev)
                other_pw = pl.multiple_of((2 * q + 1 - c) * rows_per_dev, rows_per_dev)
                dpw_ref[q] = acc_pw[:, pl.ds(own_pw, rows_per_dev), :].astype(BF16)
                send_pw[q] = acc_pw[:, pl.ds(other_pw, rows_per_dev), :].astype(BF16)
            pushes = [_to_sibling(send_sq, recv_sq, send_sems.at[0], recv_sems.at[0]),
                      _to_sibling(send_pw, recv_pw, send_sems.at[1], recv_sems.at[1])]
            for cp in pushes:
                cp.start()
            for cp in pushes:
                cp.wait_recv()
            for t, (acc, out) in enumerate(squares):
                out[...] = (out[...].astype(F32) + recv_sq[t].astype(F32)).astype(BF16)
            dpw_ref[...] = (dpw_ref[...].astype(F32) + recv_pw[...].astype(F32)).astype(BF16)
            for cp in pushes:
                cp.wait_send()

    n_chips = N_DEV // 2
    sq = (n_chips, D_MODEL // N_DEV, D_MODEL)
    pw_shape = (n_chips, n_groups, rows_per_dev, POOL_GROUP_DIM)
    est = (3 * 2 * 2 * D_MODEL * D_MODEL + 3 * 4 * D_MODEL * D_MODEL + 3 * 2 * 2 * D_MODEL * D_MODEL
           + 2 * tm * D_MODEL * (4 + 4 + 2 + 4 + 4 + 2 + 2 + 2 + 4 + 4) + 24 * tm * D_MODEL * 4)
    return _call(
        body, name="mix_tail_bwd", grid=(nt,),
        in_specs=_mix_tail_specs(tm) + [_row_spec(tm, D_MODEL), _row_spec(tm, D_MODEL), _row_spec(tm, D_MODEL, 3),
                                        _row_spec(tm, D_MODEL), _row_spec(tm, D_MODEL)],
        out_specs=[_row_spec(tm, D_MODEL), _row_spec(tm, D_MODEL), _row_spec(tm, 2 * D_MODEL), _row_spec(tm, D_MODEL),
                   _full_spec(sq), _full_spec(sq), _full_spec(sq), _full_spec(pw_shape),
                   _full_spec((2, D_MODEL)), _full_spec((1, D_MODEL))],
        out_shape=[jax.ShapeDtypeStruct((SEQ, D_MODEL), BF16), jax.ShapeDtypeStruct((SEQ, D_MODEL), BF16),
                   jax.ShapeDtypeStruct((SEQ, 2 * D_MODEL), BF16), jax.ShapeDtypeStruct((SEQ, D_MODEL), F32),
                   jax.ShapeDtypeStruct(sq, BF16), jax.ShapeDtypeStruct(sq, BF16), jax.ShapeDtypeStruct(sq, BF16),
                   jax.ShapeDtypeStruct(pw_shape, BF16),
                   jax.ShapeDtypeStruct((2, D_MODEL), F32), jax.ShapeDtypeStruct((1, D_MODEL), F32)],
        scratch_shapes=[pltpu.VMEM((n_groups, POOL_GROUP_DIM, POOL_GROUP_DIM), BF16), pltpu.SemaphoreType.DMA((N_DEV,)),
                        pltpu.VMEM((D_MODEL, D_MODEL), F32), pltpu.VMEM((D_MODEL, D_MODEL), F32),
                        pltpu.VMEM((D_MODEL, D_MODEL), F32), pltpu.VMEM((n_groups, POOL_GROUP_DIM, POOL_GROUP_DIM), F32),
                        pltpu.VMEM((3,) + sq, BF16), pltpu.VMEM((3,) + sq, BF16), pltpu.VMEM(pw_shape, BF16),
                        pltpu.VMEM(pw_shape, BF16), pltpu.SemaphoreType.DMA((2,)), pltpu.SemaphoreType.DMA((2,))],
        vmem_bytes=est, args=[p, p, gates, o_ret, bias, scale, pw8, wru, wpu, wo, dh2, ret, qkvg, a_saved, b_saved], ride=ride)


def _mix_proj_backward(dq, dk, dv, dgr, dpooled, dgates, cos, sin, h1, gain, dh2, wmix8, ride=None):
    tm, nt = TOKEN_TILE, SEQ // TOKEN_TILE
    halo_blocks = tm // HALO
    last_halo = SEQ // HALO - 1
    k_scale = HEAD_DIM ** -0.5

    def body(dq_ref, dk_ref, dv_ref, dgr_ref, dpool_ref, dhalo_ref, dgates_ref, cos_ref, sin_ref, h1_ref, g_ref,
             dh2_ref, wmix_hbm, dh1_ref, dproj_ref, dg_ref, wmix, sem):
        i = pl.program_id(0)

        @pl.when(i == 0)
        def _():
            _load_mix_weight(wmix_hbm, wmix, sem)
            dg_ref[...] = jnp.zeros_like(dg_ref)

        cos_t, sin_t = cos_ref[...], sin_ref[...]
        for seg, ref, scale in ((0, dq_ref, 1.0), (1, dk_ref, k_scale)):
            for hd in range(HEADS):
                lo = hd * HEAD_DIM
                d1, d2 = ref[:, lo:lo + ROT_HALF], ref[:, lo + ROT_HALF:lo + HEAD_DIM]
                dproj_ref[:, pl.ds(seg * D_MODEL + lo, ROT_HALF)] = ((d1 * cos_t + d2 * sin_t) * scale).astype(BF16)
                dproj_ref[:, pl.ds(seg * D_MODEL + lo + ROT_HALF, ROT_HALF)] = ((d2 * cos_t - d1 * sin_t) * scale).astype(BF16)
        dproj_ref[:, pl.ds(2 * D_MODEL, D_MODEL)] = dv_ref[...].astype(BF16)
        dproj_ref[:, pl.ds(3 * D_MODEL, D_MODEL)] = dgr_ref[...]
        dp = _pooled_transpose(jnp.concatenate([dpool_ref[...], dhalo_ref[...]], axis=0), i * tm)
        for g in range(len(POOL_WINDOWS)):
            dproj_ref[:, pl.ds(4 * D_MODEL + g * POOL_GROUP_DIM, POOL_GROUP_DIM)] = dp[g].astype(BF16)
        dproj_ref[:, pl.ds(5 * D_MODEL, 2 * D_MODEL)] = dgates_ref[...]
        du = jnp.zeros((tm, D_MODEL), F32)
        for seg in range(N_SEG):
            cols = pl.ds(seg * D_MODEL, D_MODEL)
            du = du + _dot_nt(dproj_ref[:, cols], wmix[:, cols])
        g = g_ref[...]
        _, xhat, r = _rms(h1_ref[...], g)
        dg_ref[...] += jnp.sum(du * xhat, axis=0, keepdims=True)
        dh1_ref[...] = dh2_ref[...] + _rms_bwd(du * g, xhat, r)

    est = 2 * D_MODEL * N_SEG * D_MODEL + 2 * tm * D_MODEL * (3 * 4 + 2 + 4 + 4 + 4 + 4 + 4 + 14) + 12 * tm * D_MODEL * 4
    return _call(
        body, name="mix_proj_bwd", grid=(nt,),
        in_specs=[_row_spec(tm, D_MODEL), _row_spec(tm, D_MODEL), _row_spec(tm, D_MODEL), _row_spec(tm, D_MODEL),
                  _row_spec(tm, D_MODEL),
                  pl.BlockSpec((HALO, D_MODEL), lambda i: (jnp.minimum((i + 1) * halo_blocks, last_halo), 0)),
                  _row_spec(tm, 2 * D_MODEL), _row_spec(tm, ROT_HALF), _row_spec(tm, ROT_HALF),
                  _row_spec(tm, D_MODEL), _full_spec((1, D_MODEL)), _row_spec(tm, D_MODEL), ANY],
        out_specs=[_row_spec(tm, D_MODEL), _row_spec(tm, N_SEG * D_MODEL), _full_spec((1, D_MODEL))],
        out_shape=[jax.ShapeDtypeStruct((SEQ, D_MODEL), F32), jax.ShapeDtypeStruct((SEQ, N_SEG * D_MODEL), BF16),
                   jax.ShapeDtypeStruct((1, D_MODEL), F32)],
        scratch_shapes=[pltpu.VMEM((D_MODEL, N_SEG * D_MODEL), BF16), pltpu.SemaphoreType.DMA((N_DEV,))],
        vmem_bytes=est, args=[dq, dk, dv, dgr, dpooled, dpooled, dgates, cos, sin, h1, gain, dh2, wmix8], ride=ride)


def _adamw(w, parts, m, v, name, after=None):
    rows, cols = w.shape
    n_lists = len(parts)
    tr = max([t for t in range(16, 257, 16) if rows % t == 0], default=rows)
    c1 = 1.0 - ADAM_B1 ** ADAM_STEP
    c2 = 1.0 - ADAM_B2 ** ADAM_STEP

    def body(*refs):
        w_ref, m_ref, v_ref = refs[:3]
        part_refs = refs[3:3 + n_lists]
        g_out, d_out, m_out, v_out = refs[-4:]
        g = None
        for p_ref in part_refs:
            for k in range(p_ref.shape[0]):
                term = p_ref[k].astype(F32)
                g = term if g is None else g + term
        m_new = ADAM_B1 * m_ref[...] + (1.0 - ADAM_B1) * g
        v_new = ADAM_B2 * v_ref[...] + (1.0 - ADAM_B2) * (g * g)
        g_out[...] = g
        m_out[...] = m_new
        v_out[...] = v_new
        d_out[...] = -ADAM_LR * ((m_new / c1) / (jnp.sqrt(v_new / c2) + ADAM_EPS) + ADAM_WD * w_ref[...])

    spec = pl.BlockSpec((tr, cols), lambda i: (i, 0))
    out = jax.ShapeDtypeStruct((rows, cols), F32)
    part_specs = [pl.BlockSpec((p.shape[0], tr, cols), lambda i: (0, i, 0)) for p in parts]
    part_bytes = sum(p.shape[0] * p.dtype.itemsize for p in parts)
    extra = [] if after is None else [after]
    return pl.pallas_call(
        body, name=name, grid=(rows // tr,),
        in_specs=[spec, spec, spec] + part_specs + [ANY] * len(extra),
        out_specs=[spec] * 4, out_shape=[out] * 4,
        compiler_params=_params(2 * tr * cols * (7 * 4 + part_bytes) + 8 * tr * cols * 4, 1),
    )(_in_hbm(w), _in_hbm(m), _in_hbm(v), *[_in_hbm(p) for p in parts], *extra)


def _mix_w_in_grad(u, dproj, ride=None):
    return _weight_grad(
        u, dproj, N_DEV,
        lambda tt: pl.BlockSpec((tt, D_MODEL), lambda b, t: (t, 0)),
        lambda tt: pl.BlockSpec((tt, MIX_SHARD), lambda b, t: (t, b)),
        D_MODEL, MIX_SHARD, name="w_in_grad", ride=ride)


def kernel(x, norm_ffn1, ffn1_w_in, ffn1_w_out, norm_mix, w_in, gate_bias, pool_w, pool_scale, w_ret_up, w_pool_up, w_out, norm_ffn2, ffn2_w_in, ffn2_w_out, norm_final, loss_target, m_norm_ffn1, m_ffn1_w_in, m_ffn1_w_out, m_norm_mix, m_w_in, m_gate_bias, m_pool_w, m_pool_scale, m_w_ret_up, m_w_pool_up, m_w_out, m_norm_ffn2, m_ffn2_w_in, m_ffn2_w_out, m_norm_final, v_norm_ffn1, v_ffn1_w_in, v_ffn1_w_out, v_norm_mix, v_w_in, v_gate_bias, v_pool_w, v_pool_scale, v_w_ret_up, v_w_pool_up, v_w_out, v_norm_ffn2, v_ffn2_w_in, v_ffn2_w_out, v_norm_final):
    assert x.shape == (1, SEQ, D_MODEL) and ffn1_w_in.shape == (1, D_MODEL, FF_SHARD) and w_in.shape == (1, D_MODEL, MIX_SHARD)
    x2, target = x[0], loss_target[0]

    cos, sin = _rotary_tables()
    tables = _retention_tables()
    bf = lambda w: w[0].astype(BF16)
    bf_t = lambda w: jnp.swapaxes(w[0], 0, 1).astype(BF16)
    square = lambda w: w.reshape(D_MODEL, D_MODEL)

    win1, wout1, bias8 = _alone(_GatherRide([bf_t(ffn1_w_in), bf(ffn1_w_out), gate_bias[0]]), "ffn1_weights_all_gather")
    wout1 = wout1.reshape(N_FF_GROUPS, FF_SHARD, D_MODEL)
    bias = bias8.transpose(1, 0, 2).reshape(2, D_MODEL)

    (h1, gu1), (wmix8,) = _ffn_forward(x2, norm_ffn1, win1, wout1, "ffn1_fwd", ride=_GatherRide([bf(w_in)]))
    (u, qkvg, p, gates), (win2,) = _mix_proj_forward(h1, norm_mix, wmix8, cos, sin, ride=_GatherRide([bf_t(ffn2_w_in)]))
    (ret, o_ret), (pw8, wru, wpu, wo) = _retention_forward(
        qkvg, tables, ride=_GatherRide([bf(pool_w), bf(w_ret_up), bf(w_pool_up), bf(w_out)]))
    wru, wpu, wo = square(wru), square(wpu), square(wo)
    (h2, a_saved, b_saved), (wout2,) = _mix_tail_forward(p, gates, o_ret, h1, bias, pool_scale, pw8, wru, wpu, wo,
                                        ride=_GatherRide([bf(ffn2_w_out)]))
    wout2 = wout2.reshape(N_FF_GROUPS, FF_SHARD, D_MODEL)
    (dh3, gu2, loss_part, d_norm_final), _ = _ffn_forward(h2, norm_ffn2, win2, wout2, "ffn2_fwd_loss",
                                                          head=(target, norm_final.reshape(1, D_MODEL)))

    dh2, dgu2, act2, xn2, df2, d_norm_ffn2 = _ffn_backward(dh3, h2, norm_ffn2, gu2, win2, wout2, "ffn2_bwd")
    d_wout2, _ = _ffn_w_out_grad(act2, df2, 2)
    d_win2, (r_wout2,) = _ffn_w_in_grad(xn2, dgu2, 2, ride=_ScatterRide([d_wout2]))
    (dret, dgr, dgates, dpooled, d_wo, d_wru, d_wpu, d_pw, d_bias, d_scale), (r_win2,) = _mix_tail_backward(
        dh2, p, gates, o_ret, ret, qkvg, a_saved, b_saved, bias, pool_scale, pw8, wru, wpu, wo, ride=_ScatterRide([d_win2]))
    (dq,), _ = _retention_backward_q(qkvg, dret, tables)
    (dk, dv), (r_pw, r_wru, r_wpu, r_wo) = _retention_backward_kv(
        qkvg, dret, tables, ride=_ScatterRide([d_pw, d_wru, d_wpu, d_wo]))
    (dh1, dproj, d_norm_mix), _ = _mix_proj_backward(dq, dk, dv, dgr, dpooled, dgates, cos, sin, h1, norm_mix, dh2, wmix8)
    d_wmix, _ = _mix_w_in_grad(u, dproj)
    wmix_state, wmix_started = _scatter_start(d_wmix, "w_in_grad_exchange_start")
    grad_x, dgu1, act1, xn1, df1, d_norm_ffn1 = _ffn_backward(dh1, x2, norm_ffn1, gu1, win1, wout1, "ffn1_bwd", after=wmix_started)
    small_rows = jnp.concatenate(
        [d_norm_ffn1, d_norm_mix, d_scale, d_norm_ffn2, d_norm_final, d_bias, jnp.tile(loss_part, (1, D_MODEL // 128))],
        axis=0)
    d_win1, (small_all,) = _ffn_w_in_grad(xn1, dgu1, 1, ride=_GatherRide([small_rows]))
    win1_state, win1_started = _scatter_start(d_win1, "ffn1_w_in_grad_exchange_start")
    d_wout1, _ = _ffn_w_out_grad(act1, df1, 1, after=win1_started)
    wout1_state, started = _scatter_start(d_wout1, "ffn1_w_out_grad_exchange_start")
    zero_row = jnp.zeros((1, D_MODEL), F32)

    results = {}

    def update(nm, w, parts, m, v, after):
        if nm in ("ffn1_w_in", "ffn2_w_in"):
            flat, back = (lambda a: jnp.swapaxes(a[0], 0, 1)), (lambda o: jnp.swapaxes(o, 0, 1)[None])
        else:
            flat, back = (lambda a: a.reshape(-1, w.shape[-1])), (lambda o: o.reshape(w.shape))
        parts = [p.reshape(p.shape[:1] + flat(w).shape) for p in parts]
        outs = _adamw(flat(w), parts, flat(m), flat(v), name=f"adamw_{nm}", after=after)
        results[nm] = [back(o) for o in outs]
        return outs[0]

    done = update("w_in", w_in, _scatter_wait(wmix_state, started, "w_in_grad_exchange_wait"), m_w_in, v_w_in, None)
    for nm, w, parts, m, v in (
            ("ffn2_w_in", ffn2_w_in, r_win2, m_ffn2_w_in, v_ffn2_w_in),
            ("ffn2_w_out", ffn2_w_out, r_wout2, m_ffn2_w_out, v_ffn2_w_out), ("w_ret_up", w_ret_up, r_wru, m_w_ret_up, v_w_ret_up),
            ("w_pool_up", w_pool_up, r_wpu, m_w_pool_up, v_w_pool_up), ("w_out", w_out, r_wo, m_w_out, v_w_out),
            ("pool_w", pool_w, r_pw, m_pool_w, v_pool_w)):
        done = update(nm, w, [parts], m, v, done)
    done = update("ffn1_w_in", ffn1_w_in, _scatter_wait(win1_state, done, "ffn1_w_in_grad_exchange_wait"),
                  m_ffn1_w_in, v_ffn1_w_in, None)
    update("ffn1_w_out", ffn1_w_out, _scatter_wait(wout1_state, done, "ffn1_w_out_grad_exchange_wait"),
           m_ffn1_w_out, v_ffn1_w_out, None)

    my_id = _linear_id(*_my_position())
    bias_cols = gate_bias.shape[-1]
    pad = lambda a: jnp.pad(a[0], ((0, 0), (0, D_MODEL - bias_cols)))
    pack = lambda a, b, c, d, e, gb: jnp.concatenate([a, b, c, d, e.reshape(1, D_MODEL), pad(gb), zero_row], axis=0)
    d_bias_mine = lax.dynamic_slice_in_dim(small_all[:, 5:7], my_id * bias_cols, bias_cols, axis=2)
    g_small = jnp.concatenate([small_all[:, 0:5], jnp.pad(d_bias_mine, ((0, 0), (0, 0), (0, D_MODEL - bias_cols))),
                               small_all[:, 7:8]], axis=1)
    s_outs = _adamw(pack(norm_ffn1, norm_mix, pool_scale, norm_ffn2, norm_final, gate_bias), [g_small],
                    pack(m_norm_ffn1, m_norm_mix, m_pool_scale, m_norm_ffn2, m_norm_final, m_gate_bias),
                    pack(v_norm_ffn1, v_norm_mix, v_pool_scale, v_norm_ffn2, v_norm_final, v_gate_bias), name="adamw_small")
    loss = s_outs[0][7, 0]
    for row, nm in enumerate(["norm_ffn1", "norm_mix", "pool_scale", "norm_ffn2"]):
        results[nm] = [o[row:row + 1] for o in s_outs]
    results["norm_final"] = [o[4] for o in s_outs]
    results["gate_bias"] = [o[5:7, :bias_cols][None] for o in s_outs]

    order = ["norm_ffn1", "ffn1_w_in", "ffn1_w_out", "norm_mix", "w_in", "gate_bias", "pool_w", "pool_scale",
             "w_ret_up", "w_pool_up", "w_out", "norm_ffn2", "ffn2_w_in", "ffn2_w_out", "norm_final"]
    return (loss, grad_x[None], *[results[nm][0] for nm in order], *[results[nm][1] for nm in order],
            *[results[nm][2] for nm in order], *[results[nm][3] for nm in order])
```

```python
import numpy as np
import jax
import jax.numpy as jnp
from jax import lax
from jax.experimental import pallas as pl
from jax.experimental.pallas import tpu as pltpu

F32 = jnp.float32
BF16 = jnp.bfloat16

N_DEV = 8
D_MODEL = 1024
SEQ = 4096
D_FF = 2816
FF_SHARD = 2 * D_FF // N_DEV
N_FF_GROUPS = N_DEV // 2
HEADS = 4
HEAD_DIM = 256
ROT_HALF = HEAD_DIM // 2
CHUNK = 64
RET_BLOCK = 256
POOL_WINDOWS = (2, 4, 8, 16)
POOL_GROUP_DIM = 256
HALO = 16
MIX_SHARD = 7 * D_MODEL // N_DEV
N_SEG = 7
ROPE_BASE = 10000.0
NORM_EPS = 1e-6
FFN_RES_WEIGHT = 0.5
ADAM_LR, ADAM_B1, ADAM_B2, ADAM_EPS, ADAM_WD, ADAM_STEP = 0.001, 0.9, 0.999, 1e-08, 0.01, 10

TOKEN_TILE = 256
WIDE_TILE = 512
VMEM_CAP_V7X = 64 * 1024 * 1024
MESH = pl.DeviceIdType.MESH
ANY = pl.BlockSpec(memory_space=pl.ANY)


def _vmem_limit(estimate_bytes):
    return int(min(estimate_bytes * 5 // 4 + (6 << 20), VMEM_CAP_V7X - (4 << 20)))


def _params(estimate_bytes, n_grid):
    return pltpu.CompilerParams(dimension_semantics=("arbitrary",) * n_grid,
                                vmem_limit_bytes=_vmem_limit(estimate_bytes))


def _dot(a, b):
    return jnp.dot(a, b, preferred_element_type=F32)


def _dot_nt(a, b):
    return lax.dot_general(a, b, (((1,), (1,)), ((), ())), preferred_element_type=F32)


def _dot_tn(a, b):
    return lax.dot_general(a, b, (((0,), (0,)), ((), ())), preferred_element_type=F32)


def _sig(x):
    return 1.0 / (1.0 + jnp.exp(-x))


def _rms(x, g):
    r = lax.rsqrt(jnp.mean(x * x, axis=-1, keepdims=True) + NORM_EPS)
    xhat = x * r
    return xhat * g, xhat, r


def _rms_bwd(dyg, xhat, r):
    return r * (dyg - xhat * jnp.mean(dyg * xhat, axis=-1, keepdims=True))


def _row_spec(tile, width, col=0):
    return pl.BlockSpec((tile, width), lambda i, c=col: (i, c))


def _full_spec(shape):
    return pl.BlockSpec(shape, lambda *_: (0,) * len(shape))


def _rotary_tables():
    inv_freq = (np.float32(ROPE_BASE) ** (-np.arange(ROT_HALF, dtype=np.float32) / np.float32(ROT_HALF))).astype(np.float32)
    ang = (np.arange(SEQ, dtype=np.float32)[:, None] * inv_freq[None, :]).astype(np.float32)
    return jnp.asarray(np.cos(ang.astype(np.float64)), F32), jnp.asarray(np.sin(ang.astype(np.float64)), F32)


def _retention_tables():
    log_gamma = np.log(1.0 - 2.0 ** (-5.0 - np.arange(HEADS, dtype=np.float64)))
    n = np.arange(RET_BLOCK)
    diff = (n[:, None] - n[None, :]).astype(np.float64)
    same = (n[:, None] // CHUNK) == (n[None, :] // CHUNK)
    earlier = (n[None, :] // CHUNK) < (n[:, None] // CHUNK)
    expo = np.where(same, np.abs(diff), diff)
    mask = np.where(same | earlier, np.exp(log_gamma[:, None, None] * expo[None]), 0.0)
    qdec = np.exp(log_gamma[:, None] * (n[None, :] + 1.0))[:, :, None]
    kdec = np.exp(log_gamma[:, None] * (RET_BLOCK - 1.0 - n[None, :]))[:, :, None]
    cdec = np.exp(log_gamma * RET_BLOCK)[:, None, None]
    return (jnp.asarray(mask, F32), jnp.asarray(qdec, F32), jnp.asarray(kdec, F32), jnp.asarray(cdec, F32))


def _my_position():
    return lax.axis_index("x"), lax.axis_index("y"), lax.axis_index("c")


def _linear_id(px, py, pc):
    return 4 * px + 2 * py + pc


def _when(pred, fn):
    if isinstance(pred, bool):
        if pred:
            fn()
    else:
        pl.when(pred)(fn)


class _GatherRide:
    def __init__(self, shards):
        self.args = list(shards)
        n = self.n = len(shards)
        self.out_shape = [pltpu.HBM((N_DEV,) + s.shape, s.dtype) for s in shards]
        self.scratch = [pltpu.SemaphoreType.DMA((n, 7)), pltpu.SemaphoreType.DMA((n, 7)), pltpu.SemaphoreType.DMA((n,))]

    def _plan(self, src, out, sems):
        send_sems, recv_sems, local_sem = sems
        x, y, c = _my_position()
        me, sibling = (x, y, c), (x, y, 1 - c)
        chips = [(1 - x, y), (x, 1 - y), (1 - x, 1 - y)]

        def copy(t, k, block, to, from_src=False):
            rows = out[t].at[_linear_id(*block)]
            return pltpu.make_async_remote_copy(
                src_ref=src[t] if from_src else rows, dst_ref=rows,
                send_sem=send_sems.at[t, k], recv_sem=recv_sems.at[t, k],
                device_id=to, device_id_type=MESH)

        def relay(t):
            return copy(t, 3, (x ^ (1 - c), y ^ c, c), (x ^ c, y ^ (1 - c), c))

        local = [pltpu.make_async_copy(src[t], out[t].at[_linear_id(*me)], local_sem.at[t]) for t in range(self.n)]
        return copy, relay, local, me, sibling, chips, c

    def begin(self, first, src, out, sems):
        copy, relay, local, me, sibling, chips, c = self._plan(src, out, sems)

        def start():
            for cp in local:
                cp.start()
            for t in range(self.n):
                copy(t, 0, me, sibling, from_src=True).start()
                for j in range(2):
                    copy(t, 1 + j, me, (*chips[j], c), from_src=True).start()

        _when(first, start)

    def finish(self, mid, late, last, src, out, sems):
        copy, relay, local, me, sibling, chips, c = self._plan(src, out, sems)

        def pass_on():
            for t in range(self.n):
                for j in range(2):
                    copy(t, 1 + j, (*chips[j], c), me).wait_recv()
                relay(t).start()
                for j in range(2):
                    copy(t, 4 + j, (*chips[j], c), sibling).start()

        def pass_on_relayed():
            for t in range(self.n):
                copy(t, 3, (*chips[2], c), me).wait_recv()
                copy(t, 6, (*chips[2], c), sibling).start()

        def drain():
            for t in range(self.n):
                copy(t, 0, sibling, me).wait_recv()
                for j in range(3):
                    copy(t, 4 + j, (*chips[j], 1 - c), me).wait_recv()
            for t in range(self.n):
                copy(t, 0, me, sibling, from_src=True).wait_send()
                for j in range(2):
                    copy(t, 1 + j, me, (*chips[j], c), from_src=True).wait_send()
                relay(t).wait_send()
                for j in range(3):
                    copy(t, 4 + j, (*chips[j], c), sibling).wait_send()
            for cp in local:
                cp.wait()

        _when(mid, pass_on)
        _when(late, pass_on_relayed)
        _when(last, drain)


class _ScatterRide:
    def __init__(self, chip_sums):
        self.args = list(chip_sums)
        n = self.n = len(chip_sums)
        self.out_shape = [pltpu.HBM(p.shape, p.dtype) for p in chip_sums]
        self.scratch = [pltpu.SemaphoreType.DMA((n, 3)), pltpu.SemaphoreType.DMA((n, 3)), pltpu.SemaphoreType.DMA((n,))]

    def _plan(self, src, out, sems):
        send_sems, recv_sems, local_sem = sems
        x, y, c = _my_position()

        def peer(k):
            return (x ^ (k >> 1), y ^ (k & 1))

        copies = [pltpu.make_async_remote_copy(
            src_ref=src[t].at[2 * peer(k)[0] + peer(k)[1]], dst_ref=out[t].at[k],
            send_sem=send_sems.at[t, k - 1], recv_sem=recv_sems.at[t, k - 1],
            device_id=(*peer(k), c), device_id_type=MESH) for t in range(self.n) for k in range(1, N_DEV // 2)]
        local = [pltpu.make_async_copy(src[t].at[2 * x + y], out[t].at[0], local_sem.at[t]) for t in range(self.n)]
        return copies, local

    def begin(self, first, src, out, sems):
        copies, local = self._plan(src, out, sems)

        def start():
            for cp in local + copies:
                cp.start()

        _when(first, start)

    def finish(self, mid, late, last, src, out, sems):
        copies, local = self._plan(src, out, sems)

        def drain():
            for cp in copies:
                cp.wait_recv()
            for cp in copies:
                cp.wait_send()
            for cp in local:
                cp.wait()

        _when(last, drain)


def _in_hbm(a):
    return pltpu.with_memory_space_constraint(a, pltpu.HBM)


def _call(body, *, name, grid, in_specs, out_specs, out_shape, scratch_shapes, vmem_bytes, args, ride=None, after=None):
    n_in, n_out, n_s = len(in_specs), len(out_specs), len(scratch_shapes)
    params = _params(vmem_bytes, len(grid))
    args = [_in_hbm(a) for a in args]
    out_shape = [pltpu.HBM(s.shape, s.dtype) for s in out_shape]
    if ride is None:
        if after is not None:
            def ordered_body(*refs):
                body(*refs[:n_in], *refs[n_in + 1:])
            outs = pl.pallas_call(ordered_body, name=name, grid=grid, in_specs=list(in_specs) + [ANY], out_specs=out_specs,
                                  out_shape=out_shape, scratch_shapes=scratch_shapes, compiler_params=params)(*args, after)
            return list(outs), []
        outs = pl.pallas_call(body, name=name, grid=grid, in_specs=in_specs, out_specs=out_specs, out_shape=out_shape,
                              scratch_shapes=scratch_shapes, compiler_params=params)(*args)
        return list(outs), []
    total = int(np.prod(grid))

    def riding_body(*refs):
        a = n_in
        b = a + ride.n
        c = b + n_out
        d = c + ride.n
        e = d + n_s
        step = pl.program_id(0)
        for axis in range(1, len(grid)):
            step = step * grid[axis] + pl.program_id(axis)
        ride.begin(step == 0, refs[a:b], refs[c:d], refs[e:])
        body(*refs[:a], *refs[b:c], *refs[d:e])
        ride.finish(step == total // 2, step == (3 * total) // 4, step == total - 1, refs[a:b], refs[c:d], refs[e:])

    outs = pl.pallas_call(
        riding_body, name=name, grid=grid, in_specs=list(in_specs) + [ANY] * ride.n,
        out_specs=list(out_specs) + [ANY] * ride.n, out_shape=list(out_shape) + ride.out_shape,
        scratch_shapes=list(scratch_shapes) + ride.scratch, compiler_params=params)(*args, *[_in_hbm(a) for a in ride.args])
    return list(outs[:n_out]), list(outs[n_out:])


def _alone(ride, name):
    def body(*refs):
        src, out, sems = refs[:ride.n], refs[ride.n:2 * ride.n], refs[2 * ride.n:]
        ride.begin(True, src, out, sems)
        ride.finish(True, True, True, src, out, sems)

    return list(pl.pallas_call(body, name=name, out_shape=ride.out_shape, in_specs=[ANY] * ride.n,
                               out_specs=[ANY] * ride.n, scratch_shapes=ride.scratch)(*[_in_hbm(a) for a in ride.args]))


def _scatter_copies(src, land, send_sems, recv_sems):
    x, y, c = _my_position()
    copies = []
    for k in range(1, N_DEV // 2):
        px, py = x ^ (k >> 1), y ^ (k & 1)
        copies.append(pltpu.make_async_remote_copy(
            src_ref=src.at[2 * px + py], dst_ref=land.at[k - 1], send_sem=send_sems.at[k - 1], recv_sem=recv_sems.at[k - 1],
            device_id=(px, py, c), device_id_type=MESH))
    return copies


def _scatter_start(chip_sums, name):
    n_peers = N_DEV // 2 - 1
    land_shape = (n_peers,) + chip_sums.shape[1:]
    hbm = pl.BlockSpec(memory_space=pltpu.HBM)
    sem = pl.BlockSpec(memory_space=pltpu.SEMAPHORE)

    def body(src_ref, land_ref, send_sems, recv_sems, src_thru, land_thru, token):
        for cp in _scatter_copies(src_ref, land_ref, send_sems, recv_sems):
            cp.start()
        token[...] = jnp.zeros_like(token)

    send_sems, recv_sems, src_thru, land_thru, token = pl.pallas_call(
        body, name=name,
        out_shape=(pltpu.SemaphoreType.DMA((n_peers,)), pltpu.SemaphoreType.DMA((n_peers,)),
                   pltpu.HBM(chip_sums.shape, chip_sums.dtype), pltpu.HBM(land_shape, chip_sums.dtype),
                   jax.ShapeDtypeStruct((8, 128), F32)),
        in_specs=(hbm, hbm), out_specs=(sem, sem, hbm, hbm, pl.BlockSpec(memory_space=pltpu.VMEM)),
        input_output_aliases={0: 2, 1: 3},
        compiler_params=pltpu.CompilerParams(has_side_effects=pltpu.SideEffectType.DATAFLOW_SIDE_EFFECTING),
    )(_in_hbm(chip_sums), _in_hbm(lax.empty(land_shape, chip_sums.dtype)))
    return (send_sems, recv_sems, src_thru, land_thru), token


def _scatter_wait(state, after, name):
    send_sems, recv_sems, src_thru, land_thru = state
    hbm = pl.BlockSpec(memory_space=pltpu.HBM)
    sem = pl.BlockSpec(memory_space=pltpu.SEMAPHORE)

    def body(src_ref, land_ref, send_sems, recv_sems, after_ref, src_out, land_out):
        for cp in _scatter_copies(src_ref, land_ref, send_sems, recv_sems):
            cp.wait_send()
            cp.wait_recv()

    src_done, land_done = pl.pallas_call(
        body, name=name,
        out_shape=(pltpu.HBM(src_thru.shape, src_thru.dtype), pltpu.HBM(land_thru.shape, land_thru.dtype)),
        in_specs=(hbm, hbm, sem, sem, ANY), out_specs=(hbm, hbm), input_output_aliases={0: 0, 1: 1},
        compiler_params=pltpu.CompilerParams(has_side_effects=pltpu.SideEffectType.DATAFLOW_SIDE_EFFECTING),
    )(src_thru, land_thru, send_sems, recv_sems, after)
    x, y, _ = _my_position()
    return lax.dynamic_slice_in_dim(src_done, 2 * x + y, 1, axis=0), land_done


def _load_ffn_weights(win_hbm, wout_hbm, win, wout, sem):
    a = pltpu.make_async_copy(win_hbm, win, sem.at[0])
    b = pltpu.make_async_copy(wout_hbm, wout, sem.at[1])
    a.start()
    b.start()
    a.wait()
    b.wait()


def _ffn_forward(h_in, gain, win8, wout, name, head=None, ride=None):
    tm, nt = WIDE_TILE, SEQ // WIDE_TILE

    def body(*refs):
        if head is None:
            x_ref, g_ref, win_hbm, wout_hbm, out_ref, gu_ref, win, wout, sem = refs
        else:
            x_ref, g_ref, win_hbm, wout_hbm, tgt_ref, gf_ref, out_ref, gu_ref, loss_ref, dgf_ref, win, wout, sem = refs
        i = pl.program_id(0)

        @pl.when(i == 0)
        def _():
            _load_ffn_weights(win_hbm, wout_hbm, win, wout, sem)
            if head is not None:
                loss_ref[...] = jnp.zeros_like(loss_ref)
                dgf_ref[...] = jnp.zeros_like(dgf_ref)

        x = x_ref[...]
        xn, _, _ = _rms(x, g_ref[...])
        xb = xn.astype(BF16)
        acc = jnp.zeros((tm, D_MODEL), F32)
        for j in range(N_FF_GROUPS):
            gate = _dot_nt(xb, win[j])
            up = _dot_nt(xb, win[j + N_FF_GROUPS])
            gu_ref[j] = gate.astype(BF16)
            gu_ref[j + N_FF_GROUPS] = up.astype(BF16)
            act = gate * _sig(gate) * up
            acc = acc + _dot(act.astype(BF16), wout[j])
        h = x + FFN_RES_WEIGHT * acc
        if head is None:
            out_ref[...] = h
        else:
            gf = gf_ref[...]
            y, hhat, r = _rms(h, gf)
            err = y - tgt_ref[...]
            loss_ref[...] += jnp.full(loss_ref.shape, 0.5 / D_MODEL * jnp.sum(err * err), F32)
            dy = err * (1.0 / D_MODEL)
            dgf_ref[...] += jnp.sum(dy * hhat, axis=0, keepdims=True)
            out_ref[...] = _rms_bwd(dy * gf, hhat, r)

    weights = 2 * D_MODEL * 2 * D_FF + 2 * D_FF * D_MODEL
    tiles = 2 * (2 * 4 * tm * D_MODEL + 2 * tm * 2 * D_FF) + (2 * 4 * tm * D_MODEL if head else 0)
    in_specs = [_row_spec(tm, D_MODEL), _full_spec((1, D_MODEL)), ANY, ANY]
    out_shape = [jax.ShapeDtypeStruct((SEQ, D_MODEL), F32), jax.ShapeDtypeStruct((N_DEV, SEQ, FF_SHARD), BF16)]
    out_specs = [_row_spec(tm, D_MODEL), pl.BlockSpec((N_DEV, tm, FF_SHARD), lambda i: (0, i, 0))]
    args = [h_in, gain, win8, wout]
    if head is not None:
        in_specs += [_row_spec(tm, D_MODEL), _full_spec((1, D_MODEL))]
        out_shape += [jax.ShapeDtypeStruct((1, 128), F32), jax.ShapeDtypeStruct((1, D_MODEL), F32)]
        out_specs += [_full_spec((1, 128)), _full_spec((1, D_MODEL))]
        args += list(head)
    return _call(
        body, name=name, grid=(nt,), in_specs=in_specs, out_specs=out_specs, out_shape=out_shape,
        scratch_shapes=[pltpu.VMEM((N_DEV, FF_SHARD, D_MODEL), BF16), pltpu.VMEM((N_FF_GROUPS, FF_SHARD, D_MODEL), BF16),
                        pltpu.SemaphoreType.DMA((2,))],
        vmem_bytes=weights + tiles + 16 * tm * FF_SHARD * 4, args=args, ride=ride)


def _ffn_backward(dh_out, h_in, gain, gu, win8, wout, name, after=None):
    tm, nt = TOKEN_TILE, SEQ // TOKEN_TILE

    def body(dh_ref, x_ref, g_ref, gu_ref, win_hbm, wout_hbm,
             dhin_ref, dgu_ref, act_ref, xn_ref, df_ref, dg_ref, win, wout, sem):
        i = pl.program_id(0)

        @pl.when(i == 0)
        def _():
            _load_ffn_weights(win_hbm, wout_hbm, win, wout, sem)
            dg_ref[...] = jnp.zeros_like(dg_ref)

        dh = dh_ref[...]
        g = g_ref[...]
        xn, xhat, r = _rms(x_ref[...], g)
        df = (FFN_RES_WEIGHT * dh).astype(BF16)
        dxn = jnp.zeros((tm, D_MODEL), F32)
        for j in range(N_FF_GROUPS):
            gate = gu_ref[j].astype(F32)
            up = gu_ref[j + N_FF_GROUPS].astype(F32)
            dact = _dot_nt(df, wout[j])
            s = _sig(gate)
            silu = gate * s
            dgate = (dact * up * (s * (1.0 + gate * (1.0 - s)))).astype(BF16)
            dup = (dact * silu).astype(BF16)
            act_ref[j] = (silu * up).astype(BF16)
            dgu_ref[j] = dgate
            dgu_ref[j + N_FF_GROUPS] = dup
            dxn = dxn + _dot(dgate, win[j]) + _dot(dup, win[j + N_FF_GROUPS])
        dg_ref[...] += jnp.sum(dxn * xhat, axis=0, keepdims=True)
        dhin_ref[...] = dh + _rms_bwd(dxn * g, xhat, r)
        xn_ref[...] = xn.astype(BF16)
        df_ref[...] = df

    weights = 2 * D_MODEL * 2 * D_FF + 2 * D_FF * D_MODEL
    tiles = 2 * (3 * 4 * tm * D_MODEL + 2 * tm * (2 * 2 * D_FF + D_FF) + 2 * 2 * tm * D_MODEL)
    gu_spec = pl.BlockSpec((N_DEV, tm, FF_SHARD), lambda i: (0, i, 0))
    return _call(
        body, name=name, grid=(nt,),
        in_specs=[_row_spec(tm, D_MODEL), _row_spec(tm, D_MODEL), _full_spec((1, D_MODEL)), gu_spec, ANY, ANY],
        out_specs=[_row_spec(tm, D_MODEL), gu_spec, pl.BlockSpec((N_FF_GROUPS, tm, FF_SHARD), lambda i: (0, i, 0)),
                   _row_spec(tm, D_MODEL), _row_spec(tm, D_MODEL), _full_spec((1, D_MODEL))],
        out_shape=[jax.ShapeDtypeStruct((SEQ, D_MODEL), F32), jax.ShapeDtypeStruct((N_DEV, SEQ, FF_SHARD), BF16),
                   jax.ShapeDtypeStruct((N_FF_GROUPS, SEQ, FF_SHARD), BF16), jax.ShapeDtypeStruct((SEQ, D_MODEL), BF16),
                   jax.ShapeDtypeStruct((SEQ, D_MODEL), BF16), jax.ShapeDtypeStruct((1, D_MODEL), F32)],
        scratch_shapes=[pltpu.VMEM((N_DEV, FF_SHARD, D_MODEL), BF16), pltpu.VMEM((N_FF_GROUPS, FF_SHARD, D_MODEL), BF16),
                        pltpu.SemaphoreType.DMA((2,))],
        vmem_bytes=weights + tiles + 20 * tm * FF_SHARD * 4, args=[dh_out, h_in, gain, gu, win8, wout], after=after)[0]


def _to_sibling(src, dst, send_sem, recv_sem):
    x, y, c = _my_position()
    return pltpu.make_async_remote_copy(src_ref=src, dst_ref=dst, send_sem=send_sem, recv_sem=recv_sem,
                                        device_id=(x, y, 1 - c), device_id_type=MESH)


def _weight_grad(x, g, n_out, x_spec, g_spec, k_dim, n_dim, name, halves=False, tt=2048, ride=None, after=None):
    nt = SEQ // tt
    n_chips = N_DEV // 2
    rows = k_dim // 2 if halves else k_dim

    def body(x_ref, g_ref, out_ref, acc, sendbuf, recvbuf, send_sems, recv_sems):
        b, t = pl.program_id(0), pl.program_id(1)
        c = lax.axis_index("c")

        def push(q):
            return _to_sibling(sendbuf.at[q], recvbuf.at[q], send_sems.at[q], recv_sems.at[q])

        @pl.when(t == 0)
        def _():
            acc[...] = jnp.zeros_like(acc)

        acc[...] += _dot_tn(x_ref[...], g_ref[...])

        @pl.when(t == nt - 1)
        def _():
            if halves:
                for mine, other in ((0, 1), (1, 0)):
                    @pl.when(c == mine)
                    def _():
                        out_ref[b] = acc[pl.ds(mine * rows, rows), :].astype(BF16)
                        sendbuf[b] = acc[pl.ds(other * rows, rows), :].astype(BF16)
                push(b).start()
            else:
                q = b // 2

                @pl.when(b % 2 == c)
                def _():
                    out_ref[q] = acc[...].astype(BF16)

                @pl.when(b % 2 != c)
                def _():
                    sendbuf[q] = acc[...].astype(BF16)
                    push(q).start()

        @pl.when((b == n_out - 1) & (t == nt - 1))
        def _():
            for q in range(n_chips):
                push(q).wait_recv()
                out_ref[q] = (out_ref[q].astype(F32) + recvbuf[q].astype(F32)).astype(BF16)
            for q in range(n_chips):
                push(q).wait_send()

    piece = (n_chips, rows, n_dim)
    outs, ride_outs = _call(
        body, name=name, grid=(n_out, nt), in_specs=[x_spec(tt), g_spec(tt)],
        out_specs=[pl.BlockSpec(piece, lambda b, t: (0, 0, 0))],
        out_shape=[jax.ShapeDtypeStruct(piece, BF16)],
        scratch_shapes=[pltpu.VMEM((k_dim, n_dim), F32), pltpu.VMEM(piece, BF16), pltpu.VMEM(piece, BF16),
                        pltpu.SemaphoreType.DMA((n_chips,)), pltpu.SemaphoreType.DMA((n_chips,))],
        vmem_bytes=2 * 2 * tt * (k_dim + n_dim) + 8 * k_dim * n_dim + 4 * 2 * n_chips * rows * n_dim, args=[x, g], ride=ride,
        after=after)
    return outs[0], ride_outs


def _ffn_w_out_grad(act, df, tag, ride=None, after=None):
    return _weight_grad(
        act, df, N_FF_GROUPS,
        lambda tt: pl.BlockSpec((None, tt, FF_SHARD), lambda b, t: (b, t, 0)),
        lambda tt: pl.BlockSpec((tt, D_MODEL), lambda b, t: (t, 0)),
        FF_SHARD, D_MODEL, name=f"ffn{tag}_w_out_grad", halves=True, ride=ride, after=after)


def _ffn_w_in_grad(xn, dgu, tag, ride=None):
    return _weight_grad(
        dgu, xn, N_DEV,
        lambda tt: pl.BlockSpec((None, tt, FF_SHARD), lambda b, t: (b, t, 0)),
        lambda tt: pl.BlockSpec((tt, D_MODEL), lambda b, t: (t, 0)),
        FF_SHARD, D_MODEL, name=f"ffn{tag}_w_in_grad", ride=ride)


def _load_mix_weight(wmix_hbm, wmix, sem):
    copies = [pltpu.make_async_copy(wmix_hbm.at[d], wmix.at[:, pl.ds(d * MIX_SHARD, MIX_SHARD)], sem.at[d])
              for d in range(N_DEV)]
    for cp in copies:
        cp.start()
    for cp in copies:
        cp.wait()


def _load_pool_weight(pw_hbm, pw, sem):
    rows = POOL_GROUP_DIM // N_DEV
    copies = [pltpu.make_async_copy(pw_hbm.at[d], pw.at[:, pl.ds(d * rows, rows), :], sem.at[d]) for d in range(N_DEV)]
    for cp in copies:
        cp.start()
    for cp in copies:
        cp.wait()


def _rotate(x1, x2, cos, sin):
    return x1 * cos - x2 * sin, x1 * sin + x2 * cos


def _mix_proj_forward(h1, gain, wmix8, cos, sin, ride=None):
    tm, nt = WIDE_TILE, SEQ // WIDE_TILE
    k_scale = HEAD_DIM ** -0.5

    def body(h_ref, g_ref, wmix_hbm, cos_ref, sin_ref, u_ref, qkvg_ref, p_ref, gates_ref, wmix, sem):
        @pl.when(pl.program_id(0) == 0)
        def _():
            _load_mix_weight(wmix_hbm, wmix, sem)

        u = _rms(h_ref[...], g_ref[...])[0].astype(BF16)
        u_ref[...] = u
        cos_t, sin_t = cos_ref[...], sin_ref[...]
        for seg in range(N_SEG):
            pr = _dot(u, wmix[:, pl.ds(seg * D_MODEL, D_MODEL)])
            if seg < 2:
                scale = 1.0 if seg == 0 else k_scale
                for hd in range(HEADS):
                    lo = hd * HEAD_DIM
                    o1, o2 = _rotate(pr[:, lo:lo + ROT_HALF], pr[:, lo + ROT_HALF:lo + HEAD_DIM], cos_t, sin_t)
                    qkvg_ref[:, pl.ds(seg * D_MODEL + lo, ROT_HALF)] = (o1 * scale).astype(BF16)
                    qkvg_ref[:, pl.ds(seg * D_MODEL + lo + ROT_HALF, ROT_HALF)] = (o2 * scale).astype(BF16)
            elif seg < 4:
                qkvg_ref[:, pl.ds(seg * D_MODEL, D_MODEL)] = pr.astype(BF16)
            elif seg == 4:
                p_ref[...] = pr
            else:
                gates_ref[:, pl.ds((seg - 5) * D_MODEL, D_MODEL)] = pr.astype(BF16)

    est = 2 * D_MODEL * N_SEG * D_MODEL + 2 * tm * (4 * D_MODEL + 2 * D_MODEL + 2 * 4 * D_MODEL + 4 * D_MODEL + 2 * 2 * D_MODEL)
    return _call(
        body, name="mix_proj_fwd", grid=(nt,),
        in_specs=[_row_spec(tm, D_MODEL), _full_spec((1, D_MODEL)), ANY, _row_spec(tm, ROT_HALF), _row_spec(tm, ROT_HALF)],
        out_specs=[_row_spec(tm, D_MODEL), _row_spec(tm, 4 * D_MODEL), _row_spec(tm, D_MODEL), _row_spec(tm, 2 * D_MODEL)],
        out_shape=[jax.ShapeDtypeStruct((SEQ, D_MODEL), BF16), jax.ShapeDtypeStruct((SEQ, 4 * D_MODEL), BF16),
                   jax.ShapeDtypeStruct((SEQ, D_MODEL), F32), jax.ShapeDtypeStruct((SEQ, 2 * D_MODEL), BF16)],
        scratch_shapes=[pltpu.VMEM((D_MODEL, N_SEG * D_MODEL), BF16), pltpu.SemaphoreType.DMA((N_DEV,))],
        vmem_bytes=est + 8 * tm * D_MODEL * 4, args=[h1, gain, wmix8, cos, sin], ride=ride)


def _seg_block_spec(seg, reverse=False):
    nb = SEQ // RET_BLOCK
    if reverse:
        return pl.BlockSpec((RET_BLOCK, D_MODEL), lambda i, s=seg: (nb - 1 - i, s))
    return pl.BlockSpec((RET_BLOCK, D_MODEL), lambda i, s=seg: (i, s))


def _table_specs():
    return [_full_spec((HEADS, RET_BLOCK, RET_BLOCK)), _full_spec((HEADS, RET_BLOCK, 1)),
            _full_spec((HEADS, RET_BLOCK, 1)), _full_spec((HEADS, 1, 1))]


def _head_cols(h):
    return pl.ds(h * HEAD_DIM, HEAD_DIM)


def _retention_forward(qkvg, tables, ride=None):
    nb = SEQ // RET_BLOCK

    def body(q_ref, k_ref, v_ref, gr_ref, mask_ref, qdec_ref, kdec_ref, cdec_ref, ret_ref, o_ref, state):
        @pl.when(pl.program_id(0) == 0)
        def _():
            state[...] = jnp.zeros_like(state)

        for h in range(HEADS):
            cols = _head_cols(h)
            q, k, v = q_ref[:, cols], k_ref[:, cols], v_ref[:, cols]
            scores = _dot_nt(q, k) * mask_ref[h]
            inner = _dot(scores.astype(BF16), v)
            cross = _dot((q.astype(F32) * qdec_ref[h]).astype(BF16), state[h].astype(BF16))
            ret = inner + cross
            state[h] = state[h] * cdec_ref[h] + _dot_tn((k.astype(F32) * kdec_ref[h]).astype(BF16), v)
            ret_ref[:, cols] = ret
            retn = ret * lax.rsqrt(jnp.mean(ret * ret, axis=-1, keepdims=True) + NORM_EPS)
            gr = gr_ref[:, cols].astype(F32)
            o_ref[:, cols] = (retn * (gr * _sig(gr))).astype(BF16)

    return _call(
        body, name="retention_fwd", grid=(nb,),
        in_specs=[_seg_block_spec(0), _seg_block_spec(1), _seg_block_spec(2), _seg_block_spec(3)] + _table_specs(),
        out_specs=[_row_spec(RET_BLOCK, D_MODEL)] * 2,
        out_shape=[jax.ShapeDtypeStruct((SEQ, D_MODEL), F32), jax.ShapeDtypeStruct((SEQ, D_MODEL), BF16)],
        scratch_shapes=[pltpu.VMEM((HEADS, HEAD_DIM, HEAD_DIM), F32)],
        vmem_bytes=24 * RET_BLOCK * D_MODEL * 4, args=[qkvg, qkvg, qkvg, qkvg, *tables], ride=ride)


def _retention_backward_q(qkvg, dret, tables, ride=None):
    nb = SEQ // RET_BLOCK

    def body(k_ref, v_ref, do_ref, mask_ref, qdec_ref, kdec_ref, cdec_ref, dq_ref, state):
        @pl.when(pl.program_id(0) == 0)
        def _():
            state[...] = jnp.zeros_like(state)

        for h in range(HEADS):
            cols = _head_cols(h)
            k, v, do = k_ref[:, cols], v_ref[:, cols], do_ref[:, cols]
            dscores = _dot_nt(do, v) * mask_ref[h]
            dq_ref[:, cols] = _dot(dscores.astype(BF16), k) + _dot_nt(do, state[h].astype(BF16)) * qdec_ref[h]
            state[h] = state[h] * cdec_ref[h] + _dot_tn((k.astype(F32) * kdec_ref[h]).astype(BF16), v)

    return _call(
        body, name="retention_bwd_q", grid=(nb,),
        in_specs=[_seg_block_spec(1), _seg_block_spec(2), _row_spec(RET_BLOCK, D_MODEL)] + _table_specs(),
        out_specs=[_row_spec(RET_BLOCK, D_MODEL)],
        out_shape=[jax.ShapeDtypeStruct((SEQ, D_MODEL), F32)],
        scratch_shapes=[pltpu.VMEM((HEADS, HEAD_DIM, HEAD_DIM), F32)],
        vmem_bytes=24 * RET_BLOCK * D_MODEL * 4, args=[qkvg, qkvg, dret, *tables], ride=ride)


def _retention_backward_kv(qkvg, dret, tables, ride=None):
    nb = SEQ // RET_BLOCK

    def body(q_ref, k_ref, v_ref, do_ref, mask_ref, qdec_ref, kdec_ref, cdec_ref, dk_ref, dv_ref, gstate):
        @pl.when(pl.program_id(0) == 0)
        def _():
            gstate[...] = jnp.zeros_like(gstate)

        for h in range(HEADS):
            cols = _head_cols(h)
            q, k, v, do = q_ref[:, cols], k_ref[:, cols], v_ref[:, cols], do_ref[:, cols]
            mask = mask_ref[h]
            scores = (_dot_nt(q, k) * mask).astype(BF16)
            dscores = (_dot_nt(do, v) * mask).astype(BF16)
            gs = gstate[h].astype(BF16)
            dk_ref[:, cols] = _dot_tn(dscores, q) + _dot_nt(v, gs) * kdec_ref[h]
            dv_ref[:, cols] = _dot_tn(scores, do) + _dot((k.astype(F32) * kdec_ref[h]).astype(BF16), gs)
            gstate[h] = gstate[h] * cdec_ref[h] + _dot_tn((q.astype(F32) * qdec_ref[h]).astype(BF16), do)

    rev = lambda: pl.BlockSpec((RET_BLOCK, D_MODEL), lambda i: (nb - 1 - i, 0))
    return _call(
        body, name="retention_bwd_kv", grid=(nb,),
        in_specs=[_seg_block_spec(0, True), _seg_block_spec(1, True), _seg_block_spec(2, True), rev()] + _table_specs(),
        out_specs=[rev(), rev()],
        out_shape=[jax.ShapeDtypeStruct((SEQ, D_MODEL), F32)] * 2,
        scratch_shapes=[pltpu.VMEM((HEADS, HEAD_DIM, HEAD_DIM), F32)],
        vmem_bytes=32 * RET_BLOCK * D_MODEL * 4, args=[qkvg, qkvg, qkvg, dret, *tables], ride=ride)


def _pooled(p_ext, first_row):
    rows = p_ext.shape[0]
    t = first_row + lax.broadcasted_iota(jnp.int32, (rows - HALO, 1), 0)
    outs = []
    for g, w in enumerate(POOL_WINDOWS):
        e = p_ext[:, g * POOL_GROUP_DIM:(g + 1) * POOL_GROUP_DIM]
        s, span = e, 1
        while span < w:
            s = s + pltpu.roll(s, span, 0)
            span *= 2
        count = jnp.minimum(t + 1, w).astype(F32)
        outs.append(s[HALO:] / count - e[HALO:])
    return outs


def _pooled_transpose(d_ext, first_row):
    rows = d_ext.shape[0]
    t = first_row + lax.broadcasted_iota(jnp.int32, (rows, 1), 0)
    outs = []
    for g, w in enumerate(POOL_WINDOWS):
        d = d_ext[:, g * POOL_GROUP_DIM:(g + 1) * POOL_GROUP_DIM]
        e = jnp.where(t < SEQ, d / jnp.minimum(t + 1, w).astype(F32), 0.0)
        s, span = e, 1
        while span < w:
            s = s + pltpu.roll(s, rows - span, 0)
            span *= 2
        outs.append(s[:rows - HALO] - d[:rows - HALO])
    return outs


def _mix_tail_specs(tm):
    halo_blocks = tm // HALO
    return [
        _row_spec(tm, D_MODEL),
        pl.BlockSpec((HALO, D_MODEL), lambda i: (jnp.maximum(i * halo_blocks - 1, 0), 0)),
        _row_spec(tm, 2 * D_MODEL),
        _row_spec(tm, D_MODEL),
        _full_spec((2, D_MODEL)), _full_spec((1, D_MODEL)), ANY,
        _full_spec((D_MODEL, D_MODEL)), _full_spec((D_MODEL, D_MODEL)), _full_spec((D_MODEL, D_MODEL)),
    ]


def _mix_tail_compute(i, tm, p_ref, halo_ref, gates_ref, oret_ref, bias_ref, scale_ref, pw, wru_ref, wpu_ref, saved=None):
    halo = jnp.where(i > 0, halo_ref[...], 0.0)
    pooled = _pooled(jnp.concatenate([halo, p_ref[...]], axis=0), i * tm)
    pooled = [x.astype(BF16) for x in pooled]
    mixed = jnp.concatenate([_dot(pooled[g], pw[g]) for g in range(len(POOL_WINDOWS))], axis=-1)
    pool_out = (mixed * scale_ref[...]).astype(BF16)
    o_ret = oret_ref[...]
    if saved is None:
        a = _dot(o_ret, wru_ref[...])
        b = _dot(pool_out, wpu_ref[...])
    else:
        a, b = saved[0][...].astype(F32), saved[1][...].astype(F32)
    z = gates_ref[...].astype(F32)
    g0 = _sig(z[:, :D_MODEL] + bias_ref[0:1, :])
    g1 = _sig(z[:, D_MODEL:] + bias_ref[1:2, :])
    merged = (g0 * a + g1 * b).astype(BF16)
    return pooled, mixed, pool_out, o_ret, a, b, g0, g1, merged


def _mix_tail_forward(p, gates, o_ret, h1, bias, scale, pw8, wru, wpu, wo, ride=None):
    tm, nt = TOKEN_TILE, SEQ // TOKEN_TILE

    def body(p_ref, halo_ref, gates_ref, oret_ref, bias_ref, scale_ref, pw_hbm, wru_ref, wpu_ref, wo_ref, h1_ref,
             h2_ref, a_ref, b_ref, pw, sem):
        i = pl.program_id(0)

        @pl.when(i == 0)
        def _():
            _load_pool_weight(pw_hbm, pw, sem)

        out = _mix_tail_compute(i, tm, p_ref, halo_ref, gates_ref, oret_ref, bias_ref, scale_ref, pw, wru_ref, wpu_ref)
        a_ref[...] = out[4].astype(BF16)
        b_ref[...] = out[5].astype(BF16)
        h2_ref[...] = h1_ref[...] + _dot(out[-1], wo_ref[...])

    est = 3 * 2 * 2 * D_MODEL * D_MODEL + 2 * tm * D_MODEL * (4 + 4 + 2 + 4 + 4) + 16 * tm * D_MODEL * 4
    return _call(
        body, name="mix_tail_fwd", grid=(nt,),
        in_specs=_mix_tail_specs(tm) + [_row_spec(tm, D_MODEL)],
        out_specs=[_row_spec(tm, D_MODEL)] * 3,
        out_shape=[jax.ShapeDtypeStruct((SEQ, D_MODEL), F32)] + [jax.ShapeDtypeStruct((SEQ, D_MODEL), BF16)] * 2,
        scratch_shapes=[pltpu.VMEM((len(POOL_WINDOWS), POOL_GROUP_DIM, POOL_GROUP_DIM), BF16), pltpu.SemaphoreType.DMA((N_DEV,))],
        vmem_bytes=est, args=[p, p, gates, o_ret, bias, scale, pw8, wru, wpu, wo, h1], ride=ride)


def _mix_tail_backward(dh2, p, gates, o_ret, ret, qkvg, a_saved, b_saved, bias, scale, pw8, wru, wpu, wo, ride=None):
    tm, nt = TOKEN_TILE, SEQ // TOKEN_TILE
    n_groups = len(POOL_WINDOWS)
    rows_per_dev = POOL_GROUP_DIM // N_DEV

    def body(p_ref, halo_ref, gates_ref, oret_ref, bias_ref, scale_ref, pw_hbm, wru_ref, wpu_ref, wo_ref,
             dh2_ref, ret_ref, gr_ref, a_ref, b_ref,
             dret_ref, dgr_ref, dgates_ref, dpooled_ref, dwo_ref, dwru_ref, dwpu_ref, dpw_ref, dbias_ref, dscale_ref,
             pw, sem, acc_wo, acc_wru, acc_wpu, acc_pw, send_sq, recv_sq, send_pw, recv_pw, send_sems, recv_sems):
        i = pl.program_id(0)

        @pl.when(i == 0)
        def _():
            _load_pool_weight(pw_hbm, pw, sem)
            for ref in (acc_wo, acc_wru, acc_wpu, acc_pw, dbias_ref, dscale_ref):
                ref[...] = jnp.zeros_like(ref)

        pooled, mixed, pool_out, o_ret, a, b, g0, g1, merged = _mix_tail_compute(
            i, tm, p_ref, halo_ref, gates_ref, oret_ref, bias_ref, scale_ref, pw, wru_ref, wpu_ref, saved=(a_ref, b_ref))
        dh2 = dh2_ref[...].astype(BF16)
        dm = _dot_nt(dh2, wo_ref[...])
        acc_wo[...] += _dot_tn(merged, dh2)
        da = (dm * g0).astype(BF16)
        db = (dm * g1).astype(BF16)
        dz0 = dm * a * g0 * (1.0 - g0)
        dz1 = dm * b * g1 * (1.0 - g1)
        dbias_ref[0:1, :] += jnp.sum(dz0, axis=0, keepdims=True)
        dbias_ref[1:2, :] += jnp.sum(dz1, axis=0, keepdims=True)
        dgates_ref[:, pl.ds(0, D_MODEL)] = dz0.astype(BF16)
        dgates_ref[:, pl.ds(D_MODEL, D_MODEL)] = dz1.astype(BF16)
        acc_wru[...] += _dot_tn(o_ret, da)
        acc_wpu[...] += _dot_tn(pool_out, db)
        d_oret = _dot_nt(da, wru_ref[...])
        d_pool_out = _dot_nt(db, wpu_ref[...])
        dscale_ref[...] += jnp.sum(d_pool_out * mixed, axis=0, keepdims=True)
        dmixed = (d_pool_out * scale_ref[...]).astype(BF16)
        for g in range(n_groups):
            dmg = dmixed[:, g * POOL_GROUP_DIM:(g + 1) * POOL_GROUP_DIM]
            acc_pw[g] += _dot_tn(pooled[g], dmg)
            dpooled_ref[:, pl.ds(g * POOL_GROUP_DIM, POOL_GROUP_DIM)] = _dot_nt(dmg, pw[g])
        gr = gr_ref[...].astype(F32)
        s = _sig(gr)
        silu = gr * s
        for hd in range(HEADS):
            cols = slice(hd * HEAD_DIM, (hd + 1) * HEAD_DIM)
            r_h = ret_ref[:, cols]
            rr = lax.rsqrt(jnp.mean(r_h * r_h, axis=-1, keepdims=True) + NORM_EPS)
            rhat = r_h * rr
            do_h = d_oret[:, cols]
            dgr_ref[:, cols] = (do_h * rhat * (s[:, cols] * (1.0 + gr[:, cols] * (1.0 - s[:, cols])))).astype(BF16)
            dret_ref[:, cols] = _rms_bwd(do_h * silu[:, cols], rhat, rr).astype(BF16)

        @pl.when(i == nt - 1)
        def _():
            c = lax.axis_index("c")
            rows = D_MODEL // N_DEV
            squares = ((acc_wo, dwo_ref), (acc_wru, dwru_ref), (acc_wpu, dwpu_ref))
            for q in range(n_chips):
                own = pl.multiple_of((2 * q + c) * rows, rows)
                other = pl.multiple_of((2 * q + 1 - c) * rows, rows)
                for t, (acc, out) in enumerate(squares):
                    out[q] = acc[pl.ds(own, rows), :].astype(BF16)
                    send_sq[t, q] = acc[pl.ds(other, rows), :].astype(BF16)
                own_pw = pl.multiple_of((2 * q + c) * rows_per_dev, rows_per_dev)
                other_pw = pl.multiple_of((2 * q + 1 - c) * rows_per_dev, rows_per_dev)
                dpw_ref[q] = acc_pw[:, pl.ds(own_pw, rows_per_dev), :].astype(BF16)
                send_pw[q] = acc_pw[:, pl.ds(other_pw, rows_per_dev), :].astype(BF16)
            pushes = [_to_sibling(send_sq, recv_sq, send_sems.at[0], recv_sems.at[0]),
                      _to_sibling(send_pw, recv_pw, send_sems.at[1], recv_sems.at[1])]
            for cp in pushes:
                cp.start()
            for cp in pushes:
                cp.wait_recv()
            for t, (acc, out) in enumerate(squares):
                out[...] = (out[...].astype(F32) + recv_sq[t].astype(F32)).astype(BF16)
            dpw_ref[...] = (dpw_ref[...].astype(F32) + recv_pw[...].astype(F32)).astype(BF16)
            for cp in pushes:
                cp.wait_send()

    n_chips = N_DEV // 2
    sq = (n_chips, D_MODEL // N_DEV, D_MODEL)
    pw_shape = (n_chips, n_groups, rows_per_dev, POOL_GROUP_DIM)
    est = (3 * 2 * 2 * D_MODEL * D_MODEL + 3 * 4 * D_MODEL * D_MODEL + 3 * 2 * 2 * D_MODEL * D_MODEL
           + 2 * tm * D_MODEL * (4 + 4 + 2 + 4 + 4 + 2 + 2 + 2 + 4 + 4) + 24 * tm * D_MODEL * 4)
    return _call(
        body, name="mix_tail_bwd", grid=(nt,),
        in_specs=_mix_tail_specs(tm) + [_row_spec(tm, D_MODEL), _row_spec(tm, D_MODEL), _row_spec(tm, D_MODEL, 3),
                                        _row_spec(tm, D_MODEL), _row_spec(tm, D_MODEL)],
        out_specs=[_row_spec(tm, D_MODEL), _row_spec(tm, D_MODEL), _row_spec(tm, 2 * D_MODEL), _row_spec(tm, D_MODEL),
                   _full_spec(sq), _full_spec(sq), _full_spec(sq), _full_spec(pw_shape),
                   _full_spec((2, D_MODEL)), _full_spec((1, D_MODEL))],
        out_shape=[jax.ShapeDtypeStruct((SEQ, D_MODEL), BF16), jax.ShapeDtypeStruct((SEQ, D_MODEL), BF16),
                   jax.ShapeDtypeStruct((SEQ, 2 * D_MODEL), BF16), jax.ShapeDtypeStruct((SEQ, D_MODEL), F32),
                   jax.ShapeDtypeStruct(sq, BF16), jax.ShapeDtypeStruct(sq, BF16), jax.ShapeDtypeStruct(sq, BF16),
                   jax.ShapeDtypeStruct(pw_shape, BF16),
                   jax.ShapeDtypeStruct((2, D_MODEL), F32), jax.ShapeDtypeStruct((1, D_MODEL), F32)],
        scratch_shapes=[pltpu.VMEM((n_groups, POOL_GROUP_DIM, POOL_GROUP_DIM), BF16), pltpu.SemaphoreType.DMA((N_DEV,)),
                        pltpu.VMEM((D_MODEL, D_MODEL), F32), pltpu.VMEM((D_MODEL, D_MODEL), F32),
                        pltpu.VMEM((D_MODEL, D_MODEL), F32), pltpu.VMEM((n_groups, POOL_GROUP_DIM, POOL_GROUP_DIM), F32),
                        pltpu.VMEM((3,) + sq, BF16), pltpu.VMEM((3,) + sq, BF16), pltpu.VMEM(pw_shape, BF16),
                        pltpu.VMEM(pw_shape, BF16), pltpu.SemaphoreType.DMA((2,)), pltpu.SemaphoreType.DMA((2,))],
        vmem_bytes=est, args=[p, p, gates, o_ret, bias, scale, pw8, wru, wpu, wo, dh2, ret, qkvg, a_saved, b_saved], ride=ride)


def _mix_proj_backward(dq, dk, dv, dgr, dpooled, dgates, cos, sin, h1, gain, dh2, wmix8, ride=None):
    tm, nt = TOKEN_TILE, SEQ // TOKEN_TILE
    halo_blocks = tm // HALO
    last_halo = SEQ // HALO - 1
    k_scale = HEAD_DIM ** -0.5

    def body(dq_ref, dk_ref, dv_ref, dgr_ref, dpool_ref, dhalo_ref, dgates_ref, cos_ref, sin_ref, h1_ref, g_ref,
             dh2_ref, wmix_hbm, dh1_ref, dproj_ref, dg_ref, wmix, sem):
        i = pl.program_id(0)

        @pl.when(i == 0)
        def _():
            _load_mix_weight(wmix_hbm, wmix, sem)
            dg_ref[...] = jnp.zeros_like(dg_ref)

        cos_t, sin_t = cos_ref[...], sin_ref[...]
        for seg, ref, scale in ((0, dq_ref, 1.0), (1, dk_ref, k_scale)):
            for hd in range(HEADS):
                lo = hd * HEAD_DIM
                d1, d2 = ref[:, lo:lo + ROT_HALF], ref[:, lo + ROT_HALF:lo + HEAD_DIM]
                dproj_ref[:, pl.ds(seg * D_MODEL + lo, ROT_HALF)] = ((d1 * cos_t + d2 * sin_t) * scale).astype(BF16)
                dproj_ref[:, pl.ds(seg * D_MODEL + lo + ROT_HALF, ROT_HALF)] = ((d2 * cos_t - d1 * sin_t) * scale).astype(BF16)
        dproj_ref[:, pl.ds(2 * D_MODEL, D_MODEL)] = dv_ref[...].astype(BF16)
        dproj_ref[:, pl.ds(3 * D_MODEL, D_MODEL)] = dgr_ref[...]
        dp = _pooled_transpose(jnp.concatenate([dpool_ref[...], dhalo_ref[...]], axis=0), i * tm)
        for g in range(len(POOL_WINDOWS)):
            dproj_ref[:, pl.ds(4 * D_MODEL + g * POOL_GROUP_DIM, POOL_GROUP_DIM)] = dp[g].astype(BF16)
        dproj_ref[:, pl.ds(5 * D_MODEL, 2 * D_MODEL)] = dgates_ref[...]
        du = jnp.zeros((tm, D_MODEL), F32)
        for seg in range(N_SEG):
            cols = pl.ds(seg * D_MODEL, D_MODEL)
            du = du + _dot_nt(dproj_ref[:, cols], wmix[:, cols])
        g = g_ref[...]
        _, xhat, r = _rms(h1_ref[...], g)
        dg_ref[...] += jnp.sum(du * xhat, axis=0, keepdims=True)
        dh1_ref[...] = dh2_ref[...] + _rms_bwd(du * g, xhat, r)

    est = 2 * D_MODEL * N_SEG * D_MODEL + 2 * tm * D_MODEL * (3 * 4 + 2 + 4 + 4 + 4 + 4 + 4 + 14) + 12 * tm * D_MODEL * 4
    return _call(
        body, name="mix_proj_bwd", grid=(nt,),
        in_specs=[_row_spec(tm, D_MODEL), _row_spec(tm, D_MODEL), _row_spec(tm, D_MODEL), _row_spec(tm, D_MODEL),
                  _row_spec(tm, D_MODEL),
                  pl.BlockSpec((HALO, D_MODEL), lambda i: (jnp.minimum((i + 1) * halo_blocks, last_halo), 0)),
                  _row_spec(tm, 2 * D_MODEL), _row_spec(tm, ROT_HALF), _row_spec(tm, ROT_HALF),
                  _row_spec(tm, D_MODEL), _full_spec((1, D_MODEL)), _row_spec(tm, D_MODEL), ANY],
        out_specs=[_row_spec(tm, D_MODEL), _row_spec(tm, N_SEG * D_MODEL), _full_spec((1, D_MODEL))],
        out_shape=[jax.ShapeDtypeStruct((SEQ, D_MODEL), F32), jax.ShapeDtypeStruct((SEQ, N_SEG * D_MODEL), BF16),
                   jax.ShapeDtypeStruct((1, D_MODEL), F32)],
        scratch_shapes=[pltpu.VMEM((D_MODEL, N_SEG * D_MODEL), BF16), pltpu.SemaphoreType.DMA((N_DEV,))],
        vmem_bytes=est, args=[dq, dk, dv, dgr, dpooled, dpooled, dgates, cos, sin, h1, gain, dh2, wmix8], ride=ride)


def _adamw(w, parts, m, v, name, after=None):
    rows, cols = w.shape
    n_lists = len(parts)
    tr = max([t for t in range(16, 257, 16) if rows % t == 0], default=rows)
    c1 = 1.0 - ADAM_B1 ** ADAM_STEP
    c2 = 1.0 - ADAM_B2 ** ADAM_STEP

    def body(*refs):
        w_ref, m_ref, v_ref = refs[:3]
        part_refs = refs[3:3 + n_lists]
        g_out, d_out, m_out, v_out = refs[-4:]
        g = None
        for p_ref in part_refs:
            for k in range(p_ref.shape[0]):
                term = p_ref[k].astype(F32)
                g = term if g is None else g + term
        m_new = ADAM_B1 * m_ref[...] + (1.0 - ADAM_B1) * g
        v_new = ADAM_B2 * v_ref[...] + (1.0 - ADAM_B2) * (g * g)
        g_out[...] = g
        m_out[...] = m_new
        v_out[...] = v_new
        d_out[...] = -ADAM_LR * ((m_new / c1) / (jnp.sqrt(v_new / c2) + ADAM_EPS) + ADAM_WD * w_ref[...])

    spec = pl.BlockSpec((tr, cols), lambda i: (i, 0))
    out = jax.ShapeDtypeStruct((rows, cols), F32)
    part_specs = [pl.BlockSpec((p.shape[0], tr, cols), lambda i: (0, i, 0)) for p in parts]
    part_bytes = sum(p.shape[0] * p.dtype.itemsize for p in parts)
    extra = [] if after is None else [after]
    return pl.pallas_call(
        body, name=name, grid=(rows // tr,),
        in_specs=[spec, spec, spec] + part_specs + [ANY] * len(extra),
        out_specs=[spec] * 4, out_shape=[out] * 4,
        compiler_params=_params(2 * tr * cols * (7 * 4 + part_bytes) + 8 * tr * cols * 4, 1),
    )(_in_hbm(w), _in_hbm(m), _in_hbm(v), *[_in_hbm(p) for p in parts], *extra)


def _mix_w_in_grad(u, dproj, ride=None):
    return _weight_grad(
        u, dproj, N_DEV,
        lambda tt: pl.BlockSpec((tt, D_MODEL), lambda b, t: (t, 0)),
        lambda tt: pl.BlockSpec((tt, MIX_SHARD), lambda b, t: (t, b)),
        D_MODEL, MIX_SHARD, name="w_in_grad", ride=ride)


def kernel(x, norm_ffn1, ffn1_w_in, ffn1_w_out, norm_mix, w_in, gate_bias, pool_w, pool_scale, w_ret_up, w_pool_up, w_out, norm_ffn2, ffn2_w_in, ffn2_w_out, norm_final, loss_target, m_norm_ffn1, m_ffn1_w_in, m_ffn1_w_out, m_norm_mix, m_w_in, m_gate_bias, m_pool_w, m_pool_scale, m_w_ret_up, m_w_pool_up, m_w_out, m_norm_ffn2, m_ffn2_w_in, m_ffn2_w_out, m_norm_final, v_norm_ffn1, v_ffn1_w_in, v_ffn1_w_out, v_norm_mix, v_w_in, v_gate_bias, v_pool_w, v_pool_scale, v_w_ret_up, v_w_pool_up, v_w_out, v_norm_ffn2, v_ffn2_w_in, v_ffn2_w_out, v_norm_final):
    assert x.shape == (1, SEQ, D_MODEL) and ffn1_w_in.shape == (1, D_MODEL, FF_SHARD) and w_in.shape == (1, D_MODEL, MIX_SHARD)
    x2, target = x[0], loss_target[0]

    cos, sin = _rotary_tables()
    tables = _retention_tables()
    bf = lambda w: w[0].astype(BF16)
    bf_t = lambda w: jnp.swapaxes(w[0], 0, 1).astype(BF16)
    square = lambda w: w.reshape(D_MODEL, D_MODEL)

    win1, wout1, bias8 = _alone(_GatherRide([bf_t(ffn1_w_in), bf(ffn1_w_out), gate_bias[0]]), "ffn1_weights_all_gather")
    wout1 = wout1.reshape(N_FF_GROUPS, FF_SHARD, D_MODEL)
    bias = bias8.transpose(1, 0, 2).reshape(2, D_MODEL)

    (h1, gu1), (wmix8,) = _ffn_forward(x2, norm_ffn1, win1, wout1, "ffn1_fwd", ride=_GatherRide([bf(w_in)]))
    (u, qkvg, p, gates), (win2,) = _mix_proj_forward(h1, norm_mix, wmix8, cos, sin, ride=_GatherRide([bf_t(ffn2_w_in)]))
    (ret, o_ret), (pw8, wru, wpu, wo) = _retention_forward(
        qkvg, tables, ride=_GatherRide([bf(pool_w), bf(w_ret_up), bf(w_pool_up), bf(w_out)]))
    wru, wpu, wo = square(wru), square(wpu), square(wo)
    (h2, a_saved, b_saved), (wout2,) = _mix_tail_forward(p, gates, o_ret, h1, bias, pool_scale, pw8, wru, wpu, wo,
                                        ride=_GatherRide([bf(ffn2_w_out)]))
    wout2 = wout2.reshape(N_FF_GROUPS, FF_SHARD, D_MODEL)
    (dh3, gu2, loss_part, d_norm_final), _ = _ffn_forward(h2, norm_ffn2, win2, wout2, "ffn2_fwd_loss",
                                                          head=(target, norm_final.reshape(1, D_MODEL)))

    dh2, dgu2, act2, xn2, df2, d_norm_ffn2 = _ffn_backward(dh3, h2, norm_ffn2, gu2, win2, wout2, "ffn2_bwd")
    d_wout2, _ = _ffn_w_out_grad(act2, df2, 2)
    d_win2, (r_wout2,) = _ffn_w_in_grad(xn2, dgu2, 2, ride=_ScatterRide([d_wout2]))
    (dret, dgr, dgates, dpooled, d_wo, d_wru, d_wpu, d_pw, d_bias, d_scale), (r_win2,) = _mix_tail_backward(
        dh2, p, gates, o_ret, ret, qkvg, a_saved, b_saved, bias, pool_scale, pw8, wru, wpu, wo, ride=_ScatterRide([d_win2]))
    (dq,), _ = _retention_backward_q(qkvg, dret, tables)
    (dk, dv), _ = _retention_backward_kv(qkvg, dret, tables)
    (dh1, dproj, d_norm_mix), (r_pw, r_wru, r_wpu, r_wo) = _mix_proj_backward(
        dq, dk, dv, dgr, dpooled, dgates, cos, sin, h1, norm_mix, dh2, wmix8, ride=_ScatterRide([d_pw, d_wru, d_wpu, d_wo]))
    d_wmix, _ = _mix_w_in_grad(u, dproj)
    wmix_state, wmix_started = _scatter_start(d_wmix, "w_in_grad_exchange_start")
    grad_x, dgu1, act1, xn1, df1, d_norm_ffn1 = _ffn_backward(dh1, x2, norm_ffn1, gu1, win1, wout1, "ffn1_bwd", after=wmix_started)
    small_rows = jnp.concatenate(
        [d_norm_ffn1, d_norm_mix, d_scale, d_norm_ffn2, d_norm_final, d_bias, jnp.tile(loss_part, (1, D_MODEL // 128))],
        axis=0)
    d_win1, (small_all,) = _ffn_w_in_grad(xn1, dgu1, 1, ride=_GatherRide([small_rows]))
    win1_state, win1_started = _scatter_start(d_win1, "ffn1_w_in_grad_exchange_start")
    d_wout1, _ = _ffn_w_out_grad(act1, df1, 1, after=win1_started)
    wout1_state, started = _scatter_start(d_wout1, "ffn1_w_out_grad_exchange_start")
    zero_row = jnp.zeros((1, D_MODEL), F32)

    results = {}

    def update(nm, w, parts, m, v, after):
        if nm in ("ffn1_w_in", "ffn2_w_in"):
            flat, back = (lambda a: jnp.swapaxes(a[0], 0, 1)), (lambda o: jnp.swapaxes(o, 0, 1)[None])
        else:
            flat, back = (lambda a: a.reshape(-1, w.shape[-1])), (lambda o: o.reshape(w.shape))
        parts = [p.reshape(p.shape[:1] + flat(w).shape) for p in parts]
        outs = _adamw(flat(w), parts, flat(m), flat(v), name=f"adamw_{nm}", after=after)
        results[nm] = [back(o) for o in outs]
        return outs[0]

    done = update("w_in", w_in, _scatter_wait(wmix_state, started, "w_in_grad_exchange_wait"), m_w_in, v_w_in, None)
    for nm, w, parts, m, v in (
            ("ffn2_w_in", ffn2_w_in, r_win2, m_ffn2_w_in, v_ffn2_w_in),
            ("ffn2_w_out", ffn2_w_out, r_wout2, m_ffn2_w_out, v_ffn2_w_out), ("w_ret_up", w_ret_up, r_wru, m_w_ret_up, v_w_ret_up),
            ("w_pool_up", w_pool_up, r_wpu, m_w_pool_up, v_w_pool_up), ("w_out", w_out, r_wo, m_w_out, v_w_out),
            ("pool_w", pool_w, r_pw, m_pool_w, v_pool_w)):
        done = update(nm, w, [parts], m, v, done)
    done = update("ffn1_w_in", ffn1_w_in, _scatter_wait(win1_state, done, "ffn1_w_in_grad_exchange_wait"),
                  m_ffn1_w_in, v_ffn1_w_in, None)
    update("ffn1_w_out", ffn1_w_out, _scatter_wait(wout1_state, done, "ffn1_w_out_grad_exchange_wait"),
           m_ffn1_w_out, v_ffn1_w_out, None)

    my_id = _linear_id(*_my_position())
    bias_cols = gate_bias.shape[-1]
    pad = lambda a: jnp.pad(a[0], ((0, 0), (0, D_MODEL - bias_cols)))
    pack = lambda a, b, c, d, e, gb: jnp.concatenate([a, b, c, d, e.reshape(1, D_MODEL), pad(gb), zero_row], axis=0)
    d_bias_mine = lax.dynamic_slice_in_dim(small_all[:, 5:7], my_id * bias_cols, bias_cols, axis=2)
    g_small = jnp.concatenate([small_all[:, 0:5], jnp.pad(d_bias_mine, ((0, 0), (0, 0), (0, D_MODEL - bias_cols))),
                               small_all[:, 7:8]], axis=1)
    s_outs = _adamw(pack(norm_ffn1, norm_mix, pool_scale, norm_ffn2, norm_final, gate_bias), [g_small],
                    pack(m_norm_ffn1, m_norm_mix, m_pool_scale, m_norm_ffn2, m_norm_final, m_gate_bias),
                    pack(v_norm_ffn1, v_norm_mix, v_pool_scale, v_norm_ffn2, v_norm_final, v_gate_bias), name="adamw_small")
    loss = s_outs[0][7, 0]
    for row, nm in enumerate(["norm_ffn1", "norm_mix", "pool_scale", "norm_ffn2"]):
        results[nm] = [o[row:row + 1] for o in s_outs]
    results["norm_final"] = [o[4] for o in s_outs]
    results["gate_bias"] = [o[5:7, :bias_cols][None] for o in s_outs]

    order = ["norm_ffn1", "ffn1_w_in", "ffn1_w_out", "norm_mix", "w_in", "gate_bias", "pool_w", "pool_scale",
             "w_ret_up", "w_pool_up", "w_out", "norm_ffn2", "ffn2_w_in", "ffn2_w_out", "norm_final"]
    return (loss, grad_x[None], *[results[nm][0] for nm in order], *[results[nm][1] for nm in order],
            *[results[nm][2] for nm in order], *[results[nm][3] for nm in order])
```

```python
import numpy as np
import jax
import jax.numpy as jnp
from jax import lax
from jax.experimental import pallas as pl
from jax.experimental.pallas import tpu as pltpu

F32 = jnp.float32
BF16 = jnp.bfloat16

N_DEV = 8
D_MODEL = 1024
SEQ = 4096
D_FF = 2816
FF_SHARD = 2 * D_FF // N_DEV
N_FF_GROUPS = N_DEV // 2
HEADS = 4
HEAD_DIM = 256
ROT_HALF = HEAD_DIM // 2
CHUNK = 64
RET_BLOCK = 256
POOL_WINDOWS = (2, 4, 8, 16)
POOL_GROUP_DIM = 256
HALO = 16
MIX_SHARD = 7 * D_MODEL // N_DEV
N_SEG = 7
ROPE_BASE = 10000.0
NORM_EPS = 1e-6
FFN_RES_WEIGHT = 0.5
ADAM_LR, ADAM_B1, ADAM_B2, ADAM_EPS, ADAM_WD, ADAM_STEP = 0.001, 0.9, 0.999, 1e-08, 0.01, 10

TOKEN_TILE = 256
WIDE_TILE = 512
VMEM_CAP_V7X = 64 * 1024 * 1024
MESH = pl.DeviceIdType.MESH
ANY = pl.BlockSpec(memory_space=pl.ANY)


def _vmem_limit(estimate_bytes):
    return int(min(estimate_bytes * 5 // 4 + (6 << 20), VMEM_CAP_V7X - (4 << 20)))


def _params(estimate_bytes, n_grid):
    return pltpu.CompilerParams(dimension_semantics=("arbitrary",) * n_grid,
                                vmem_limit_bytes=_vmem_limit(estimate_bytes))


def _dot(a, b):
    return jnp.dot(a, b, preferred_element_type=F32)


def _dot_nt(a, b):
    return lax.dot_general(a, b, (((1,), (1,)), ((), ())), preferred_element_type=F32)


def _dot_tn(a, b):
    return lax.dot_general(a, b, (((0,), (0,)), ((), ())), preferred_element_type=F32)


def _sig(x):
    return 1.0 / (1.0 + jnp.exp(-x))


def _rms(x, g):
    r = lax.rsqrt(jnp.mean(x * x, axis=-1, keepdims=True) + NORM_EPS)
    xhat = x * r
    return xhat * g, xhat, r


def _rms_bwd(dyg, xhat, r):
    return r * (dyg - xhat * jnp.mean(dyg * xhat, axis=-1, keepdims=True))


def _row_spec(tile, width, col=0):
    return pl.BlockSpec((tile, width), lambda i, c=col: (i, c))


def _full_spec(shape):
    return pl.BlockSpec(shape, lambda *_: (0,) * len(shape))


def _rotary_tables():
    inv_freq = (np.float32(ROPE_BASE) ** (-np.arange(ROT_HALF, dtype=np.float32) / np.float32(ROT_HALF))).astype(np.float32)
    ang = (np.arange(SEQ, dtype=np.float32)[:, None] * inv_freq[None, :]).astype(np.float32)
    return jnp.asarray(np.cos(ang.astype(np.float64)), F32), jnp.asarray(np.sin(ang.astype(np.float64)), F32)


def _retention_tables():
    log_gamma = np.log(1.0 - 2.0 ** (-5.0 - np.arange(HEADS, dtype=np.float64)))
    n = np.arange(RET_BLOCK)
    diff = (n[:, None] - n[None, :]).astype(np.float64)
    same = (n[:, None] // CHUNK) == (n[None, :] // CHUNK)
    earlier = (n[None, :] // CHUNK) < (n[:, None] // CHUNK)
    expo = np.where(same, np.abs(diff), diff)
    mask = np.where(same | earlier, np.exp(log_gamma[:, None, None] * expo[None]), 0.0)
    qdec = np.exp(log_gamma[:, None] * (n[None, :] + 1.0))[:, :, None]
    kdec = np.exp(log_gamma[:, None] * (RET_BLOCK - 1.0 - n[None, :]))[:, :, None]
    cdec = np.exp(log_gamma * RET_BLOCK)[:, None, None]
    return (jnp.asarray(mask, F32), jnp.asarray(qdec, F32), jnp.asarray(kdec, F32), jnp.asarray(cdec, F32))


def _my_position():
    return lax.axis_index("x"), lax.axis_index("y"), lax.axis_index("c")


def _linear_id(px, py, pc):
    return 4 * px + 2 * py + pc


def _when(pred, fn):
    if isinstance(pred, bool):
        if pred:
            fn()
    else:
        pl.when(pred)(fn)


class _GatherRide:
    def __init__(self, shards):
        self.args = list(shards)
        n = self.n = len(shards)
        self.out_shape = [pltpu.HBM((N_DEV,) + s.shape, s.dtype) for s in shards]
        self.scratch = [pltpu.SemaphoreType.DMA((n, 7)), pltpu.SemaphoreType.DMA((n, 7)), pltpu.SemaphoreType.DMA((n,))]

    def _plan(self, src, out, sems):
        send_sems, recv_sems, local_sem = sems
        x, y, c = _my_position()
        me, sibling = (x, y, c), (x, y, 1 - c)
        chips = [(1 - x, y), (x, 1 - y), (1 - x, 1 - y)]

        def copy(t, k, block, to, from_src=False):
            rows = out[t].at[_linear_id(*block)]
            return pltpu.make_async_remote_copy(
                src_ref=src[t] if from_src else rows, dst_ref=rows,
                send_sem=send_sems.at[t, k], recv_sem=recv_sems.at[t, k],
                device_id=to, device_id_type=MESH)

        def relay(t):
            return copy(t, 3, (x ^ (1 - c), y ^ c, c), (x ^ c, y ^ (1 - c), c))

        local = [pltpu.make_async_copy(src[t], out[t].at[_linear_id(*me)], local_sem.at[t]) for t in range(self.n)]
        return copy, relay, local, me, sibling, chips, c

    def begin(self, first, src, out, sems):
        copy, relay, local, me, sibling, chips, c = self._plan(src, out, sems)

        def start():
            for cp in local:
                cp.start()
            for t in range(self.n):
                copy(t, 0, me, sibling, from_src=True).start()
                for j in range(2):
                    copy(t, 1 + j, me, (*chips[j], c), from_src=True).start()

        _when(first, start)

    def finish(self, mid, late, last, src, out, sems):
        copy, relay, local, me, sibling, chips, c = self._plan(src, out, sems)

        def pass_on():
            for t in range(self.n):
                for j in range(2):
                    copy(t, 1 + j, (*chips[j], c), me).wait_recv()
                relay(t).start()
                for j in range(2):
                    copy(t, 4 + j, (*chips[j], c), sibling).start()

        def pass_on_relayed():
            for t in range(self.n):
                copy(t, 3, (*chips[2], c), me).wait_recv()
                copy(t, 6, (*chips[2], c), sibling).start()

        def drain():
            for t in range(self.n):
                copy(t, 0, sibling, me).wait_recv()
                for j in range(3):
                    copy(t, 4 + j, (*chips[j], 1 - c), me).wait_recv()
            for t in range(self.n):
                copy(t, 0, me, sibling, from_src=True).wait_send()
                for j in range(2):
                    copy(t, 1 + j, me, (*chips[j], c), from_src=True).wait_send()
                relay(t).wait_send()
                for j in range(3):
                    copy(t, 4 + j, (*chips[j], c), sibling).wait_send()
            for cp in local:
                cp.wait()

        _when(mid, pass_on)
        _when(late, pass_on_relayed)
        _when(last, drain)


class _ScatterRide:
    def __init__(self, chip_sums):
        self.args = list(chip_sums)
        n = self.n = len(chip_sums)
        self.out_shape = [pltpu.HBM(p.shape, p.dtype) for p in chip_sums]
        self.scratch = [pltpu.SemaphoreType.DMA((n, 3)), pltpu.SemaphoreType.DMA((n, 3)), pltpu.SemaphoreType.DMA((n,))]

    def _plan(self, src, out, sems):
        send_sems, recv_sems, local_sem = sems
        x, y, c = _my_position()

        def peer(k):
            return (x ^ (k >> 1), y ^ (k & 1))

        copies = [pltpu.make_async_remote_copy(
            src_ref=src[t].at[2 * peer(k)[0] + peer(k)[1]], dst_ref=out[t].at[k],
            send_sem=send_sems.at[t, k - 1], recv_sem=recv_sems.at[t, k - 1],
            device_id=(*peer(k), c), device_id_type=MESH) for t in range(self.n) for k in range(1, N_DEV // 2)]
        local = [pltpu.make_async_copy(src[t].at[2 * x + y], out[t].at[0], local_sem.at[t]) for t in range(self.n)]
        return copies, local

    def begin(self, first, src, out, sems):
        copies, local = self._plan(src, out, sems)

        def start():
            for cp in local + copies:
                cp.start()

        _when(first, start)

    def finish(self, mid, late, last, src, out, sems):
        copies, local = self._plan(src, out, sems)

        def drain():
            for cp in copies:
                cp.wait_recv()
            for cp in copies:
                cp.wait_send()
            for cp in local:
                cp.wait()

        _when(last, drain)


def _in_hbm(a):
    return pltpu.with_memory_space_constraint(a, pltpu.HBM)


def _call(body, *, name, grid, in_specs, out_specs, out_shape, scratch_shapes, vmem_bytes, args, ride=None, after=None):
    n_in, n_out, n_s = len(in_specs), len(out_specs), len(scratch_shapes)
    params = _params(vmem_bytes, len(grid))
    args = [_in_hbm(a) for a in args]
    out_shape = [pltpu.HBM(s.shape, s.dtype) for s in out_shape]
    if ride is None:
        if after is not None:
            def ordered_body(*refs):
                body(*refs[:n_in], *refs[n_in + 1:])
            outs = pl.pallas_call(ordered_body, name=name, grid=grid, in_specs=list(in_specs) + [ANY], out_specs=out_specs,
                                  out_shape=out_shape, scratch_shapes=scratch_shapes, compiler_params=params)(*args, after)
            return list(outs), []
        outs = pl.pallas_call(body, name=name, grid=grid, in_specs=in_specs, out_specs=out_specs, out_shape=out_shape,
                              scratch_shapes=scratch_shapes, compiler_params=params)(*args)
        return list(outs), []
    total = int(np.prod(grid))

    def riding_body(*refs):
        a = n_in
        b = a + ride.n
        c = b + n_out
        d = c + ride.n
        e = d + n_s
        step = pl.program_id(0)
        for axis in range(1, len(grid)):
            step = step * grid[axis] + pl.program_id(axis)
        ride.begin(step == 0, refs[a:b], refs[c:d], refs[e:])
        body(*refs[:a], *refs[b:c], *refs[d:e])
        ride.finish(step == total // 2, step == (3 * total) // 4, step == total - 1, refs[a:b], refs[c:d], refs[e:])

    outs = pl.pallas_call(
        riding_body, name=name, grid=grid, in_specs=list(in_specs) + [ANY] * ride.n,
        out_specs=list(out_specs) + [ANY] * ride.n, out_shape=list(out_shape) + ride.out_shape,
        scratch_shapes=list(scratch_shapes) + ride.scratch, compiler_params=params)(*args, *[_in_hbm(a) for a in ride.args])
    return list(outs[:n_out]), list(outs[n_out:])


def _alone(ride, name):
    def body(*refs):
        src, out, sems = refs[:ride.n], refs[ride.n:2 * ride.n], refs[2 * ride.n:]
        ride.begin(True, src, out, sems)
        ride.finish(True, True, True, src, out, sems)

    return list(pl.pallas_call(body, name=name, out_shape=ride.out_shape, in_specs=[ANY] * ride.n,
                               out_specs=[ANY] * ride.n, scratch_shapes=ride.scratch)(*[_in_hbm(a) for a in ride.args]))


def _scatter_copies(src, land, send_sems, recv_sems):
    x, y, c = _my_position()
    copies = []
    for k in range(1, N_DEV // 2):
        px, py = x ^ (k >> 1), y ^ (k & 1)
        copies.append(pltpu.make_async_remote_copy(
            src_ref=src.at[2 * px + py], dst_ref=land.at[k - 1], send_sem=send_sems.at[k - 1], recv_sem=recv_sems.at[k - 1],
            device_id=(px, py, c), device_id_type=MESH))
    return copies


def _scatter_start(chip_sums, name):
    n_peers = N_DEV // 2 - 1
    land_shape = (n_peers,) + chip_sums.shape[1:]
    hbm = pl.BlockSpec(memory_space=pltpu.HBM)
    sem = pl.BlockSpec(memory_space=pltpu.SEMAPHORE)

    def body(src_ref, land_ref, send_sems, recv_sems, src_thru, land_thru, token):
        for cp in _scatter_copies(src_ref, land_ref, send_sems, recv_sems):
            cp.start()
        token[...] = jnp.zeros_like(token)

    send_sems, recv_sems, src_thru, land_thru, token = pl.pallas_call(
        body, name=name,
        out_shape=(pltpu.SemaphoreType.DMA((n_peers,)), pltpu.SemaphoreType.DMA((n_peers,)),
                   pltpu.HBM(chip_sums.shape, chip_sums.dtype), pltpu.HBM(land_shape, chip_sums.dtype),
                   jax.ShapeDtypeStruct((8, 128), F32)),
        in_specs=(hbm, hbm), out_specs=(sem, sem, hbm, hbm, pl.BlockSpec(memory_space=pltpu.VMEM)),
        input_output_aliases={0: 2, 1: 3},
        compiler_params=pltpu.CompilerParams(has_side_effects=pltpu.SideEffectType.DATAFLOW_SIDE_EFFECTING),
    )(_in_hbm(chip_sums), _in_hbm(lax.empty(land_shape, chip_sums.dtype)))
    return (send_sems, recv_sems, src_thru, land_thru), token


def _scatter_wait(state, after, name):
    send_sems, recv_sems, src_thru, land_thru = state
    hbm = pl.BlockSpec(memory_space=pltpu.HBM)
    sem = pl.BlockSpec(memory_space=pltpu.SEMAPHORE)

    def body(src_ref, land_ref, send_sems, recv_sems, after_ref, src_out, land_out):
        for cp in _scatter_copies(src_ref, land_ref, send_sems, recv_sems):
            cp.wait_send()
            cp.wait_recv()

    src_done, land_done = pl.pallas_call(
        body, name=name,
        out_shape=(pltpu.HBM(src_thru.shape, src_thru.dtype), pltpu.HBM(land_thru.shape, land_thru.dtype)),
        in_specs=(hbm, hbm, sem, sem, ANY), out_specs=(hbm, hbm), input_output_aliases={0: 0, 1: 1},
        compiler_params=pltpu.CompilerParams(has_side_effects=pltpu.SideEffectType.DATAFLOW_SIDE_EFFECTING),
    )(src_thru, land_thru, send_sems, recv_sems, after)
    x, y, _ = _my_position()
    return lax.dynamic_slice_in_dim(src_done, 2 * x + y, 1, axis=0), land_done


def _load_ffn_weights(win_hbm, wout_hbm, win, wout, sem):
    a = pltpu.make_async_copy(win_hbm, win, sem.at[0])
    b = pltpu.make_async_copy(wout_hbm, wout, sem.at[1])
    a.start()
    b.start()
    a.wait()
    b.wait()


def _ffn_forward(h_in, gain, win8, wout, name, head=None, ride=None):
    tm, nt = WIDE_TILE, SEQ // WIDE_TILE

    def body(*refs):
        if head is None:
            x_ref, g_ref, win_hbm, wout_hbm, out_ref, gu_ref, win, wout, sem = refs
        else:
            x_ref, g_ref, win_hbm, wout_hbm, tgt_ref, gf_ref, out_ref, gu_ref, loss_ref, dgf_ref, win, wout, sem = refs
        i = pl.program_id(0)

        @pl.when(i == 0)
        def _():
            _load_ffn_weights(win_hbm, wout_hbm, win, wout, sem)
            if head is not None:
                loss_ref[...] = jnp.zeros_like(loss_ref)
                dgf_ref[...] = jnp.zeros_like(dgf_ref)

        x = x_ref[...]
        xn, _, _ = _rms(x, g_ref[...])
        xb = xn.astype(BF16)
        acc = jnp.zeros((tm, D_MODEL), F32)
        for j in range(N_FF_GROUPS):
            gate = _dot_nt(xb, win[j])
            up = _dot_nt(xb, win[j + N_FF_GROUPS])
            gu_ref[j] = gate.astype(BF16)
            gu_ref[j + N_FF_GROUPS] = up.astype(BF16)
            act = gate * _sig(gate) * up
            acc = acc + _dot(act.astype(BF16), wout[j])
        h = x + FFN_RES_WEIGHT * acc
        if head is None:
            out_ref[...] = h
        else:
            gf = gf_ref[...]
            y, hhat, r = _rms(h, gf)
            err = y - tgt_ref[...]
            loss_ref[...] += jnp.full(loss_ref.shape, 0.5 / D_MODEL * jnp.sum(err * err), F32)
            dy = err * (1.0 / D_MODEL)
            dgf_ref[...] += jnp.sum(dy * hhat, axis=0, keepdims=True)
            out_ref[...] = _rms_bwd(dy * gf, hhat, r)

    weights = 2 * D_MODEL * 2 * D_FF + 2 * D_FF * D_MODEL
    tiles = 2 * (2 * 4 * tm * D_MODEL + 2 * tm * 2 * D_FF) + (2 * 4 * tm * D_MODEL if head else 0)
    in_specs = [_row_spec(tm, D_MODEL), _full_spec((1, D_MODEL)), ANY, ANY]
    out_shape = [jax.ShapeDtypeStruct((SEQ, D_MODEL), F32), jax.ShapeDtypeStruct((N_DEV, SEQ, FF_SHARD), BF16)]
    out_specs = [_row_spec(tm, D_MODEL), pl.BlockSpec((N_DEV, tm, FF_SHARD), lambda i: (0, i, 0))]
    args = [h_in, gain, win8, wout]
    if head is not None:
        in_specs += [_row_spec(tm, D_MODEL), _full_spec((1, D_MODEL))]
        out_shape += [jax.ShapeDtypeStruct((1, 128), F32), jax.ShapeDtypeStruct((1, D_MODEL), F32)]
        out_specs += [_full_spec((1, 128)), _full_spec((1, D_MODEL))]
        args += list(head)
    return _call(
        body, name=name, grid=(nt,), in_specs=in_specs, out_specs=out_specs, out_shape=out_shape,
        scratch_shapes=[pltpu.VMEM((N_DEV, FF_SHARD, D_MODEL), BF16), pltpu.VMEM((N_FF_GROUPS, FF_SHARD, D_MODEL), BF16),
                        pltpu.SemaphoreType.DMA((2,))],
        vmem_bytes=weights + tiles + 16 * tm * FF_SHARD * 4, args=args, ride=ride)


def _ffn_backward(dh_out, h_in, gain, gu, win8, wout, name, after=None):
    tm, nt = TOKEN_TILE, SEQ // TOKEN_TILE

    def body(dh_ref, x_ref, g_ref, gu_ref, win_hbm, wout_hbm,
             dhin_ref, dgu_ref, act_ref, xn_ref, df_ref, dg_ref, win, wout, sem):
        i = pl.program_id(0)

        @pl.when(i == 0)
        def _():
            _load_ffn_weights(win_hbm, wout_hbm, win, wout, sem)
            dg_ref[...] = jnp.zeros_like(dg_ref)

        dh = dh_ref[...]
        g = g_ref[...]
        xn, xhat, r = _rms(x_ref[...], g)
        df = (FFN_RES_WEIGHT * dh).astype(BF16)
        dxn = jnp.zeros((tm, D_MODEL), F32)
        for j in range(N_FF_GROUPS):
            gate = gu_ref[j].astype(F32)
            up = gu_ref[j + N_FF_GROUPS].astype(F32)
            dact = _dot_nt(df, wout[j])
            s = _sig(gate)
            silu = gate * s
            dgate = (dact * up * (s * (1.0 + gate * (1.0 - s)))).astype(BF16)
            dup = (dact * silu).astype(BF16)
            act_ref[j] = (silu * up).astype(BF16)
            dgu_ref[j] = dgate
            dgu_ref[j + N_FF_GROUPS] = dup
            dxn = dxn + _dot(dgate, win[j]) + _dot(dup, win[j + N_FF_GROUPS])
        dg_ref[...] += jnp.sum(dxn * xhat, axis=0, keepdims=True)
        dhin_ref[...] = dh + _rms_bwd(dxn * g, xhat, r)
        xn_ref[...] = xn.astype(BF16)
        df_ref[...] = df

    weights = 2 * D_MODEL * 2 * D_FF + 2 * D_FF * D_MODEL
    tiles = 2 * (3 * 4 * tm * D_MODEL + 2 * tm * (2 * 2 * D_FF + D_FF) + 2 * 2 * tm * D_MODEL)
    gu_spec = pl.BlockSpec((N_DEV, tm, FF_SHARD), lambda i: (0, i, 0))
    return _call(
        body, name=name, grid=(nt,),
        in_specs=[_row_spec(tm, D_MODEL), _row_spec(tm, D_MODEL), _full_spec((1, D_MODEL)), gu_spec, ANY, ANY],
        out_specs=[_row_spec(tm, D_MODEL), gu_spec, pl.BlockSpec((N_FF_GROUPS, tm, FF_SHARD), lambda i: (0, i, 0)),
                   _row_spec(tm, D_MODEL), _row_spec(tm, D_MODEL), _full_spec((1, D_MODEL))],
        out_shape=[jax.ShapeDtypeStruct((SEQ, D_MODEL), F32), jax.ShapeDtypeStruct((N_DEV, SEQ, FF_SHARD), BF16),
                   jax.ShapeDtypeStruct((N_FF_GROUPS, SEQ, FF_SHARD), BF16), jax.ShapeDtypeStruct((SEQ, D_MODEL), BF16),
                   jax.ShapeDtypeStruct((SEQ, D_MODEL), BF16), jax.ShapeDtypeStruct((1, D_MODEL), F32)],
        scratch_shapes=[pltpu.VMEM((N_DEV, FF_SHARD, D_MODEL), BF16), pltpu.VMEM((N_FF_GROUPS, FF_SHARD, D_MODEL), BF16),
                        pltpu.SemaphoreType.DMA((2,))],
        vmem_bytes=weights + tiles + 20 * tm * FF_SHARD * 4, args=[dh_out, h_in, gain, gu, win8, wout], after=after)[0]


def _to_sibling(src, dst, send_sem, recv_sem):
    x, y, c = _my_position()
    return pltpu.make_async_remote_copy(src_ref=src, dst_ref=dst, send_sem=send_sem, recv_sem=recv_sem,
                                        device_id=(x, y, 1 - c), device_id_type=MESH)


def _weight_grad(x, g, n_out, x_spec, g_spec, k_dim, n_dim, name, halves=False, tt=2048, ride=None, after=None):
    nt = SEQ // tt
    n_chips = N_DEV // 2
    rows = k_dim // 2 if halves else k_dim

    def body(x_ref, g_ref, out_ref, acc, sendbuf, recvbuf, send_sems, recv_sems):
        b, t = pl.program_id(0), pl.program_id(1)
        c = lax.axis_index("c")

        def push(q):
            return _to_sibling(sendbuf.at[q], recvbuf.at[q], send_sems.at[q], recv_sems.at[q])

        @pl.when(t == 0)
        def _():
            acc[...] = jnp.zeros_like(acc)

        acc[...] += _dot_tn(x_ref[...], g_ref[...])

        @pl.when(t == nt - 1)
        def _():
            if halves:
                for mine, other in ((0, 1), (1, 0)):
                    @pl.when(c == mine)
                    def _():
                        out_ref[b] = acc[pl.ds(mine * rows, rows), :].astype(BF16)
                        sendbuf[b] = acc[pl.ds(other * rows, rows), :].astype(BF16)
                push(b).start()
            else:
                q = b // 2

                @pl.when(b % 2 == c)
                def _():
                    out_ref[q] = acc[...].astype(BF16)

                @pl.when(b % 2 != c)
                def _():
                    sendbuf[q] = acc[...].astype(BF16)
                    push(q).start()

        @pl.when((b == n_out - 1) & (t == nt - 1))
        def _():
            for q in range(n_chips):
                push(q).wait_recv()
                out_ref[q] = (out_ref[q].astype(F32) + recvbuf[q].astype(F32)).astype(BF16)
            for q in range(n_chips):
                push(q).wait_send()

    piece = (n_chips, rows, n_dim)
    outs, ride_outs = _call(
        body, name=name, grid=(n_out, nt), in_specs=[x_spec(tt), g_spec(tt)],
        out_specs=[pl.BlockSpec(piece, lambda b, t: (0, 0, 0))],
        out_shape=[jax.ShapeDtypeStruct(piece, BF16)],
        scratch_shapes=[pltpu.VMEM((k_dim, n_dim), F32), pltpu.VMEM(piece, BF16), pltpu.VMEM(piece, BF16),
                        pltpu.SemaphoreType.DMA((n_chips,)), pltpu.SemaphoreType.DMA((n_chips,))],
        vmem_bytes=2 * 2 * tt * (k_dim + n_dim) + 8 * k_dim * n_dim + 4 * 2 * n_chips * rows * n_dim, args=[x, g], ride=ride,
        after=after)
    return outs[0], ride_outs


def _ffn_w_out_grad(act, df, tag, ride=None, after=None):
    return _weight_grad(
        act, df, N_FF_GROUPS,
        lambda tt: pl.BlockSpec((None, tt, FF_SHARD), lambda b, t: (b, t, 0)),
        lambda tt: pl.BlockSpec((tt, D_MODEL), lambda b, t: (t, 0)),
        FF_SHARD, D_MODEL, name=f"ffn{tag}_w_out_grad", halves=True, ride=ride, after=after)


def _ffn_w_in_grad(xn, dgu, tag, ride=None):
    return _weight_grad(
        dgu, xn, N_DEV,
        lambda tt: pl.BlockSpec((None, tt, FF_SHARD), lambda b, t: (b, t, 0)),
        lambda tt: pl.BlockSpec((tt, D_MODEL), lambda b, t: (t, 0)),
        FF_SHARD, D_MODEL, name=f"ffn{tag}_w_in_grad", ride=ride)


def _load_mix_weight(wmix_hbm, wmix, sem):
    copies = [pltpu.make_async_copy(wmix_hbm.at[d], wmix.at[:, pl.ds(d * MIX_SHARD, MIX_SHARD)], sem.at[d])
              for d in range(N_DEV)]
    for cp in copies:
        cp.start()
    for cp in copies:
        cp.wait()


def _load_pool_weight(pw_hbm, pw, sem):
    rows = POOL_GROUP_DIM // N_DEV
    copies = [pltpu.make_async_copy(pw_hbm.at[d], pw.at[:, pl.ds(d * rows, rows), :], sem.at[d]) for d in range(N_DEV)]
    for cp in copies:
        cp.start()
    for cp in copies:
        cp.wait()


def _rotate(x1, x2, cos, sin):
    return x1 * cos - x2 * sin, x1 * sin + x2 * cos


def _mix_proj_forward(h1, gain, wmix8, cos, sin, ride=None):
    tm, nt = WIDE_TILE, SEQ // WIDE_TILE
    k_scale = HEAD_DIM ** -0.5

    def body(h_ref, g_ref, wmix_hbm, cos_ref, sin_ref, u_ref, qkvg_ref, p_ref, gates_ref, wmix, sem):
        @pl.when(pl.program_id(0) == 0)
        def _():
            _load_mix_weight(wmix_hbm, wmix, sem)

        u = _rms(h_ref[...], g_ref[...])[0].astype(BF16)
        u_ref[...] = u
        cos_t, sin_t = cos_ref[...], sin_ref[...]
        for seg in range(N_SEG):
            pr = _dot(u, wmix[:, pl.ds(seg * D_MODEL, D_MODEL)])
            if seg < 2:
                scale = 1.0 if seg == 0 else k_scale
                for hd in range(HEADS):
                    lo = hd * HEAD_DIM
                    o1, o2 = _rotate(pr[:, lo:lo + ROT_HALF], pr[:, lo + ROT_HALF:lo + HEAD_DIM], cos_t, sin_t)
                    qkvg_ref[:, pl.ds(seg * D_MODEL + lo, ROT_HALF)] = (o1 * scale).astype(BF16)
                    qkvg_ref[:, pl.ds(seg * D_MODEL + lo + ROT_HALF, ROT_HALF)] = (o2 * scale).astype(BF16)
            elif seg < 4:
                qkvg_ref[:, pl.ds(seg * D_MODEL, D_MODEL)] = pr.astype(BF16)
            elif seg == 4:
                p_ref[...] = pr
            else:
                gates_ref[:, pl.ds((seg - 5) * D_MODEL, D_MODEL)] = pr.astype(BF16)

    est = 2 * D_MODEL * N_SEG * D_MODEL + 2 * tm * (4 * D_MODEL + 2 * D_MODEL + 2 * 4 * D_MODEL + 4 * D_MODEL + 2 * 2 * D_MODEL)
    return _call(
        body, name="mix_proj_fwd", grid=(nt,),
        in_specs=[_row_spec(tm, D_MODEL), _full_spec((1, D_MODEL)), ANY, _row_spec(tm, ROT_HALF), _row_spec(tm, ROT_HALF)],
        out_specs=[_row_spec(tm, D_MODEL), _row_spec(tm, 4 * D_MODEL), _row_spec(tm, D_MODEL), _row_spec(tm, 2 * D_MODEL)],
        out_shape=[jax.ShapeDtypeStruct((SEQ, D_MODEL), BF16), jax.ShapeDtypeStruct((SEQ, 4 * D_MODEL), BF16),
                   jax.ShapeDtypeStruct((SEQ, D_MODEL), F32), jax.ShapeDtypeStruct((SEQ, 2 * D_MODEL), BF16)],
        scratch_shapes=[pltpu.VMEM((D_MODEL, N_SEG * D_MODEL), BF16), pltpu.SemaphoreType.DMA((N_DEV,))],
        vmem_bytes=est + 8 * tm * D_MODEL * 4, args=[h1, gain, wmix8, cos, sin], ride=ride)


def _seg_block_spec(seg, reverse=False):
    nb = SEQ // RET_BLOCK
    if reverse:
        return pl.BlockSpec((RET_BLOCK, D_MODEL), lambda i, s=seg: (nb - 1 - i, s))
    return pl.BlockSpec((RET_BLOCK, D_MODEL), lambda i, s=seg: (i, s))


def _table_specs():
    return [_full_spec((HEADS, RET_BLOCK, RET_BLOCK)), _full_spec((HEADS, RET_BLOCK, 1)),
            _full_spec((HEADS, RET_BLOCK, 1)), _full_spec((HEADS, 1, 1))]


def _head_cols(h):
    return pl.ds(h * HEAD_DIM, HEAD_DIM)


def _retention_forward(qkvg, tables, ride=None):
    nb = SEQ // RET_BLOCK

    def body(q_ref, k_ref, v_ref, gr_ref, mask_ref, qdec_ref, kdec_ref, cdec_ref, ret_ref, o_ref, state):
        @pl.when(pl.program_id(0) == 0)
        def _():
            state[...] = jnp.zeros_like(state)

        for h in range(HEADS):
            cols = _head_cols(h)
            q, k, v = q_ref[:, cols], k_ref[:, cols], v_ref[:, cols]
            scores = _dot_nt(q, k) * mask_ref[h]
            inner = _dot(scores.astype(BF16), v)
            cross = _dot((q.astype(F32) * qdec_ref[h]).astype(BF16), state[h].astype(BF16))
            ret = inner + cross
            state[h] = state[h] * cdec_ref[h] + _dot_tn((k.astype(F32) * kdec_ref[h]).astype(BF16), v)
            ret_ref[:, cols] = ret
            retn = ret * lax.rsqrt(jnp.mean(ret * ret, axis=-1, keepdims=True) + NORM_EPS)
            gr = gr_ref[:, cols].astype(F32)
            o_ref[:, cols] = (retn * (gr * _sig(gr))).astype(BF16)

    return _call(
        body, name="retention_fwd", grid=(nb,),
        in_specs=[_seg_block_spec(0), _seg_block_spec(1), _seg_block_spec(2), _seg_block_spec(3)] + _table_specs(),
        out_specs=[_row_spec(RET_BLOCK, D_MODEL)] * 2,
        out_shape=[jax.ShapeDtypeStruct((SEQ, D_MODEL), F32), jax.ShapeDtypeStruct((SEQ, D_MODEL), BF16)],
        scratch_shapes=[pltpu.VMEM((HEADS, HEAD_DIM, HEAD_DIM), F32)],
        vmem_bytes=24 * RET_BLOCK * D_MODEL * 4, args=[qkvg, qkvg, qkvg, qkvg, *tables], ride=ride)


def _retention_backward_q(qkvg, dret, tables, ride=None):
    nb = SEQ // RET_BLOCK

    def body(k_ref, v_ref, do_ref, mask_ref, qdec_ref, kdec_ref, cdec_ref, dq_ref, state):
        @pl.when(pl.program_id(0) == 0)
        def _():
            state[...] = jnp.zeros_like(state)

        for h in range(HEADS):
            cols = _head_cols(h)
            k, v, do = k_ref[:, cols], v_ref[:, cols], do_ref[:, cols]
            dscores = _dot_nt(do, v) * mask_ref[h]
            dq_ref[:, cols] = _dot(dscores.astype(BF16), k) + _dot_nt(do, state[h].astype(BF16)) * qdec_ref[h]
            state[h] = state[h] * cdec_ref[h] + _dot_tn((k.astype(F32) * kdec_ref[h]).astype(BF16), v)

    return _call(
        body, name="retention_bwd_q", grid=(nb,),
        in_specs=[_seg_block_spec(1), _seg_block_spec(2), _row_spec(RET_BLOCK, D_MODEL)] + _table_specs(),
        out_specs=[_row_spec(RET_BLOCK, D_MODEL)],
        out_shape=[jax.ShapeDtypeStruct((SEQ, D_MODEL), F32)],
        scratch_shapes=[pltpu.VMEM((HEADS, HEAD_DIM, HEAD_DIM), F32)],
        vmem_bytes=24 * RET_BLOCK * D_MODEL * 4, args=[qkvg, qkvg, dret, *tables], ride=ride)


def _retention_backward_kv(qkvg, dret, tables, ride=None):
    nb = SEQ // RET_BLOCK

    def body(q_ref, k_ref, v_ref, do_ref, mask_ref, qdec_ref, kdec_ref, cdec_ref, dk_ref, dv_ref, gstate):
        @pl.when(pl.program_id(0) == 0)
        def _():
            gstate[...] = jnp.zeros_like(gstate)

        for h in range(HEADS):
            cols = _head_cols(h)
            q, k, v, do = q_ref[:, cols], k_ref[:, cols], v_ref[:, cols], do_ref[:, cols]
            mask = mask_ref[h]
            scores = (_dot_nt(q, k) * mask).astype(BF16)
            dscores = (_dot_nt(do, v) * mask).astype(BF16)
            gs = gstate[h].astype(BF16)
            dk_ref[:, cols] = _dot_tn(dscores, q) + _dot_nt(v, gs) * kdec_ref[h]
            dv_ref[:, cols] = _dot_tn(scores, do) + _dot((k.astype(F32) * kdec_ref[h]).astype(BF16), gs)
            gstate[h] = gstate[h] * cdec_ref[h] + _dot_tn((q.astype(F32) * qdec_ref[h]).astype(BF16), do)

    rev = lambda: pl.BlockSpec((RET_BLOCK, D_MODEL), lambda i: (nb - 1 - i, 0))
    return _call(
        body, name="retention_bwd_kv", grid=(nb,),
        in_specs=[_seg_block_spec(0, True), _seg_block_spec(1, True), _seg_block_spec(2, True), rev()] + _table_specs(),
        out_specs=[rev(), rev()],
        out_shape=[jax.ShapeDtypeStruct((SEQ, D_MODEL), F32)] * 2,
        scratch_shapes=[pltpu.VMEM((HEADS, HEAD_DIM, HEAD_DIM), F32)],
        vmem_bytes=32 * RET_BLOCK * D_MODEL * 4, args=[qkvg, qkvg, qkvg, dret, *tables], ride=ride)


def _pooled(p_ext, first_row):
    rows = p_ext.shape[0]
    t = first_row + lax.broadcasted_iota(jnp.int32, (rows - HALO, 1), 0)
    outs = []
    for g, w in enumerate(POOL_WINDOWS):
        e = p_ext[:, g * POOL_GROUP_DIM:(g + 1) * POOL_GROUP_DIM]
        s, span = e, 1
        while span < w:
            s = s + pltpu.roll(s, span, 0)
            span *= 2
        count = jnp.minimum(t + 1, w).astype(F32)
        outs.append(s[HALO:] / count - e[HALO:])
    return outs


def _pooled_transpose(d_ext, first_row):
    rows = d_ext.shape[0]
    t = first_row + lax.broadcasted_iota(jnp.int32, (rows, 1), 0)
    outs = []
    for g, w in enumerate(POOL_WINDOWS):
        d = d_ext[:, g * POOL_GROUP_DIM:(g + 1) * POOL_GROUP_DIM]
        e = jnp.where(t < SEQ, d / jnp.minimum(t + 1, w).astype(F32), 0.0)
        s, span = e, 1
        while span < w:
            s = s + pltpu.roll(s, rows - span, 0)
            span *= 2
        outs.append(s[:rows - HALO] - d[:rows - HALO])
    return outs


def _mix_tail_specs(tm):
    halo_blocks = tm // HALO
    return [
        _row_spec(tm, D_MODEL),
        pl.BlockSpec((HALO, D_MODEL), lambda i: (jnp.maximum(i * halo_blocks - 1, 0), 0)),
        _row_spec(tm, 2 * D_MODEL),
        _row_spec(tm, D_MODEL),
        _full_spec((2, D_MODEL)), _full_spec((1, D_MODEL)), ANY,
        _full_spec((D_MODEL, D_MODEL)), _full_spec((D_MODEL, D_MODEL)), _full_spec((D_MODEL, D_MODEL)),
    ]


def _mix_tail_compute(i, tm, p_ref, halo_ref, gates_ref, oret_ref, bias_ref, scale_ref, pw, wru_ref, wpu_ref, saved=None):
    halo = jnp.where(i > 0, halo_ref[...], 0.0)
    pooled = _pooled(jnp.concatenate([halo, p_ref[...]], axis=0), i * tm)
    pooled = [x.astype(BF16) for x in pooled]
    mixed = jnp.concatenate([_dot(pooled[g], pw[g]) for g in range(len(POOL_WINDOWS))], axis=-1)
    pool_out = (mixed * scale_ref[...]).astype(BF16)
    o_ret = oret_ref[...]
    if saved is None:
        a = _dot(o_ret, wru_ref[...])
        b = _dot(pool_out, wpu_ref[...])
    else:
        a, b = saved[0][...].astype(F32), saved[1][...].astype(F32)
    z = gates_ref[...].astype(F32)
    g0 = _sig(z[:, :D_MODEL] + bias_ref[0:1, :])
    g1 = _sig(z[:, D_MODEL:] + bias_ref[1:2, :])
    merged = (g0 * a + g1 * b).astype(BF16)
    return pooled, mixed, pool_out, o_ret, a, b, g0, g1, merged


def _mix_tail_forward(p, gates, o_ret, h1, bias, scale, pw8, wru, wpu, wo, ride=None):
    tm, nt = TOKEN_TILE, SEQ // TOKEN_TILE

    def body(p_ref, halo_ref, gates_ref, oret_ref, bias_ref, scale_ref, pw_hbm, wru_ref, wpu_ref, wo_ref, h1_ref,
             h2_ref, a_ref, b_ref, pw, sem):
        i = pl.program_id(0)

        @pl.when(i == 0)
        def _():
            _load_pool_weight(pw_hbm, pw, sem)

        out = _mix_tail_compute(i, tm, p_ref, halo_ref, gates_ref, oret_ref, bias_ref, scale_ref, pw, wru_ref, wpu_ref)
        a_ref[...] = out[4].astype(BF16)
        b_ref[...] = out[5].astype(BF16)
        h2_ref[...] = h1_ref[...] + _dot(out[-1], wo_ref[...])

    est = 3 * 2 * 2 * D_MODEL * D_MODEL + 2 * tm * D_MODEL * (4 + 4 + 2 + 4 + 4) + 16 * tm * D_MODEL * 4
    return _call(
        body, name="mix_tail_fwd", grid=(nt,),
        in_specs=_mix_tail_specs(tm) + [_row_spec(tm, D_MODEL)],
        out_specs=[_row_spec(tm, D_MODEL)] * 3,
        out_shape=[jax.ShapeDtypeStruct((SEQ, D_MODEL), F32)] + [jax.ShapeDtypeStruct((SEQ, D_MODEL), BF16)] * 2,
        scratch_shapes=[pltpu.VMEM((len(POOL_WINDOWS), POOL_GROUP_DIM, POOL_GROUP_DIM), BF16), pltpu.SemaphoreType.DMA((N_DEV,))],
        vmem_bytes=est, args=[p, p, gates, o_ret, bias, scale, pw8, wru, wpu, wo, h1], ride=ride)


def _mix_tail_backward(dh2, p, gates, o_ret, ret, qkvg, a_saved, b_saved, bias, scale, pw8, wru, wpu, wo, ride=None):
    tm, nt = TOKEN_TILE, SEQ // TOKEN_TILE
    n_groups = len(POOL_WINDOWS)
    rows_per_dev = POOL_GROUP_DIM // N_DEV

    def body(p_ref, halo_ref, gates_ref, oret_ref, bias_ref, scale_ref, pw_hbm, wru_ref, wpu_ref, wo_ref,
             dh2_ref, ret_ref, gr_ref, a_ref, b_ref,
             dret_ref, dgr_ref, dgates_ref, dpooled_ref, dwo_ref, dwru_ref, dwpu_ref, dpw_ref, dbias_ref, dscale_ref,
             pw, sem, acc_wo, acc_wru, acc_wpu, acc_pw, send_sq, recv_sq, send_pw, recv_pw, send_sems, recv_sems):
        i = pl.program_id(0)

        @pl.when(i == 0)
        def _():
            _load_pool_weight(pw_hbm, pw, sem)
            for ref in (acc_wo, acc_wru, acc_wpu, acc_pw, dbias_ref, dscale_ref):
                ref[...] = jnp.zeros_like(ref)

        pooled, mixed, pool_out, o_ret, a, b, g0, g1, merged = _mix_tail_compute(
            i, tm, p_ref, halo_ref, gates_ref, oret_ref, bias_ref, scale_ref, pw, wru_ref, wpu_ref, saved=(a_ref, b_ref))
        dh2 = dh2_ref[...].astype(BF16)
        dm = _dot_nt(dh2, wo_ref[...])
        acc_wo[...] += _dot_tn(merged, dh2)
        da = (dm * g0).astype(BF16)
        db = (dm * g1).astype(BF16)
        dz0 = dm * a * g0 * (1.0 - g0)
        dz1 = dm * b * g1 * (1.0 - g1)
        dbias_ref[0:1, :] += jnp.sum(dz0, axis=0, keepdims=True)
        dbias_ref[1:2, :] += jnp.sum(dz1, axis=0, keepdims=True)
        dgates_ref[:, pl.ds(0, D_MODEL)] = dz0.astype(BF16)
        dgates_ref[:, pl.ds(D_MODEL, D_MODEL)] = dz1.astype(BF16)
        acc_wru[...] += _dot_tn(o_ret, da)
        acc_wpu[...] += _dot_tn(pool_out, db)
        d_oret = _dot_nt(da, wru_ref[...])
        d_pool_out = _dot_nt(db, wpu_ref[...])
        dscale_ref[...] += jnp.sum(d_pool_out * mixed, axis=0, keepdims=True)
        dmixed = (d_pool_out * scale_ref[...]).astype(BF16)
        for g in range(n_groups):
            dmg = dmixed[:, g * POOL_GROUP_DIM:(g + 1) * POOL_GROUP_DIM]
            acc_pw[g] += _dot_tn(pooled[g], dmg)
            dpooled_ref[:, pl.ds(g * POOL_GROUP_DIM, POOL_GROUP_DIM)] = _dot_nt(dmg, pw[g])
        gr = gr_ref[...].astype(F32)
        s = _sig(gr)
        silu = gr * s
        for hd in range(HEADS):
            cols = slice(hd * HEAD_DIM, (hd + 1) * HEAD_DIM)
            r_h = ret_ref[:, cols]
            rr = lax.rsqrt(jnp.mean(r_h * r_h, axis=-1, keepdims=True) + NORM_EPS)
            rhat = r_h * rr
            do_h = d_oret[:, cols]
            dgr_ref[:, cols] = (do_h * rhat * (s[:, cols] * (1.0 + gr[:, cols] * (1.0 - s[:, cols])))).astype(BF16)
            dret_ref[:, cols] = _rms_bwd(do_h * silu[:, cols], rhat, rr).astype(BF16)

        @pl.when(i == nt - 1)
        def _():
            c = lax.axis_index("c")
            rows = D_MODEL // N_DEV
            squares = ((acc_wo, dwo_ref), (acc_wru, dwru_ref), (acc_wpu, dwpu_ref))
            for q in range(n_chips):
                own = pl.multiple_of((2 * q + c) * rows, rows)
                other = pl.multiple_of((2 * q + 1 - c) * rows, rows)
                for t, (acc, out) in enumerate(squares):
                    out[q] = acc[pl.ds(own, rows), :].astype(BF16)
                    send_sq[t, q] = acc[pl.ds(other, rows), :].astype(BF16)
                own_pw = pl.multiple_of((2 * q + c) * rows_per_dev, rows_per_dev)
                other_pw = pl.multiple_of((2 * q + 1 - c) * rows_per_dev, rows_per_dev)
                dpw_ref[q] = acc_pw[:, pl.ds(own_pw, rows_per_dev), :].astype(BF16)
                send_pw[q] = acc_pw[:, pl.ds(other_pw, rows_per_dev), :].astype(BF16)
            pushes = [_to_sibling(send_sq, recv_sq, send_sems.at[0], recv_sems.at[0]),
                      _to_sibling(send_pw, recv_pw, send_sems.at[1], recv_sems.at[1])]
            for cp in pushes:
                cp.start()
            for cp in pushes:
                cp.wait_recv()
            for t, (acc, out) in enumerate(squares):
                out[...] = (out[...].astype(F32) + recv_sq[t].astype(F32)).astype(BF16)
            dpw_ref[...] = (dpw_ref[...].astype(F32) + recv_pw[...].astype(F32)).astype(BF16)
            for cp in pushes:
                cp.wait_send()

    n_chips = N_DEV // 2
    sq = (n_chips, D_MODEL // N_DEV, D_MODEL)
    pw_shape = (n_chips, n_groups, rows_per_dev, POOL_GROUP_DIM)
    est = (3 * 2 * 2 * D_MODEL * D_MODEL + 3 * 4 * D_MODEL * D_MODEL + 3 * 2 * 2 * D_MODEL * D_MODEL
           + 2 * tm * D_MODEL * (4 + 4 + 2 + 4 + 4 + 2 + 2 + 2 + 4 + 4) + 24 * tm * D_MODEL * 4)
    return _call(
        body, name="mix_tail_bwd", grid=(nt,),
        in_specs=_mix_tail_specs(tm) + [_row_spec(tm, D_MODEL), _row_spec(tm, D_MODEL), _row_spec(tm, D_MODEL, 3),
                                        _row_spec(tm, D_MODEL), _row_spec(tm, D_MODEL)],
        out_specs=[_row_spec(tm, D_MODEL), _row_spec(tm, D_MODEL), _row_spec(tm, 2 * D_MODEL), _row_spec(tm, D_MODEL),
                   _full_spec(sq), _full_spec(sq), _full_spec(sq), _full_spec(pw_shape),
                   _full_spec((2, D_MODEL)), _full_spec((1, D_MODEL))],
        out_shape=[jax.ShapeDtypeStruct((SEQ, D_MODEL), BF16), jax.ShapeDtypeStruct((SEQ, D_MODEL), BF16),
                   jax.ShapeDtypeStruct((SEQ, 2 * D_MODEL), BF16), jax.ShapeDtypeStruct((SEQ, D_MODEL), F32),
                   jax.ShapeDtypeStruct(sq, BF16), jax.ShapeDtypeStruct(sq, BF16), jax.ShapeDtypeStruct(sq, BF16),
                   jax.ShapeDtypeStruct(pw_shape, BF16),
                   jax.ShapeDtypeStruct((2, D_MODEL), F32), jax.ShapeDtypeStruct((1, D_MODEL), F32)],
        scratch_shapes=[pltpu.VMEM((n_groups, POOL_GROUP_DIM, POOL_GROUP_DIM), BF16), pltpu.SemaphoreType.DMA((N_DEV,)),
                        pltpu.VMEM((D_MODEL, D_MODEL), F32), pltpu.VMEM((D_MODEL, D_MODEL), F32),
                        pltpu.VMEM((D_MODEL, D_MODEL), F32), pltpu.VMEM((n_groups, POOL_GROUP_DIM, POOL_GROUP_DIM), F32),
                        pltpu.VMEM((3,) + sq, BF16), pltpu.VMEM((3,) + sq, BF16), pltpu.VMEM(pw_shape, BF16),
                        pltpu.VMEM(pw_shape, BF16), pltpu.SemaphoreType.DMA((2,)), pltpu.SemaphoreType.DMA((2,))],
        vmem_bytes=est, args=[p, p, gates, o_ret, bias, scale, pw8, wru, wpu, wo, dh2, ret, qkvg, a_saved, b_saved], ride=ride)


def _mix_proj_backward(dq, dk, dv, dgr, dpooled, dgates, cos, sin, h1, gain, dh2, wmix8, ride=None):
    tm, nt = TOKEN_TILE, SEQ // TOKEN_TILE
    halo_blocks = tm // HALO
    last_halo = SEQ // HALO - 1
    k_scale = HEAD_DIM ** -0.5

    def body(dq_ref, dk_ref, dv_ref, dgr_ref, dpool_ref, dhalo_ref, dgates_ref, cos_ref, sin_ref, h1_ref, g_ref,
             dh2_ref, wmix_hbm, dh1_ref, dproj_ref, dg_ref, wmix, sem):
        i = pl.program_id(0)

        @pl.when(i == 0)
        def _():
            _load_mix_weight(wmix_hbm, wmix, sem)
            dg_ref[...] = jnp.zeros_like(dg_ref)

        cos_t, sin_t = cos_ref[...], sin_ref[...]
        for seg, ref, scale in ((0, dq_ref, 1.0), (1, dk_ref, k_scale)):
            for hd in range(HEADS):
                lo = hd * HEAD_DIM
                d1, d2 = ref[:, lo:lo + ROT_HALF], ref[:, lo + ROT_HALF:lo + HEAD_DIM]
                dproj_ref[:, pl.ds(seg * D_MODEL + lo, ROT_HALF)] = ((d1 * cos_t + d2 * sin_t) * scale).astype(BF16)
                dproj_ref[:, pl.ds(seg * D_MODEL + lo + ROT_HALF, ROT_HALF)] = ((d2 * cos_t - d1 * sin_t) * scale).astype(BF16)
        dproj_ref[:, pl.ds(2 * D_MODEL, D_MODEL)] = dv_ref[...].astype(BF16)
        dproj_ref[:, pl.ds(3 * D_MODEL, D_MODEL)] = dgr_ref[...]
        dp = _pooled_transpose(jnp.concatenate([dpool_ref[...], dhalo_ref[...]], axis=0), i * tm)
        for g in range(len(POOL_WINDOWS)):
            dproj_ref[:, pl.ds(4 * D_MODEL + g * POOL_GROUP_DIM, POOL_GROUP_DIM)] = dp[g].astype(BF16)
        dproj_ref[:, pl.ds(5 * D_MODEL, 2 * D_MODEL)] = dgates_ref[...]
        du = jnp.zeros((tm, D_MODEL), F32)
        for seg in range(N_SEG):
            cols = pl.ds(seg * D_MODEL, D_MODEL)
            du = du + _dot_nt(dproj_ref[:, cols], wmix[:, cols])
        g = g_ref[...]
        _, xhat, r = _rms(h1_ref[...], g)
        dg_ref[...] += jnp.sum(du * xhat, axis=0, keepdims=True)
        dh1_ref[...] = dh2_ref[...] + _rms_bwd(du * g, xhat, r)

    est = 2 * D_MODEL * N_SEG * D_MODEL + 2 * tm * D_MODEL * (3 * 4 + 2 + 4 + 4 + 4 + 4 + 4 + 14) + 12 * tm * D_MODEL * 4
    return _call(
        body, name="mix_proj_bwd", grid=(nt,),
        in_specs=[_row_spec(tm, D_MODEL), _row_spec(tm, D_MODEL), _row_spec(tm, D_MODEL), _row_spec(tm, D_MODEL),
                  _row_spec(tm, D_MODEL),
                  pl.BlockSpec((HALO, D_MODEL), lambda i: (jnp.minimum((i + 1) * halo_blocks, last_halo), 0)),
                  _row_spec(tm, 2 * D_MODEL), _row_spec(tm, ROT_HALF), _row_spec(tm, ROT_HALF),
                  _row_spec(tm, D_MODEL), _full_spec((1, D_MODEL)), _row_spec(tm, D_MODEL), ANY],
        out_specs=[_row_spec(tm, D_MODEL), _row_spec(tm, N_SEG * D_MODEL), _full_spec((1, D_MODEL))],
        out_shape=[jax.ShapeDtypeStruct((SEQ, D_MODEL), F32), jax.ShapeDtypeStruct((SEQ, N_SEG * D_MODEL), BF16),
                   jax.ShapeDtypeStruct((1, D_MODEL), F32)],
        scratch_shapes=[pltpu.VMEM((D_MODEL, N_SEG * D_MODEL), BF16), pltpu.SemaphoreType.DMA((N_DEV,))],
        vmem_bytes=est, args=[dq, dk, dv, dgr, dpooled, dpooled, dgates, cos, sin, h1, gain, dh2, wmix8], ride=ride)


def _adamw(w, parts, m, v, name, after=None):
    rows, cols = w.shape
    n_lists = len(parts)
    tr = max([t for t in range(16, 257, 16) if rows % t == 0], default=rows)
    c1 = 1.0 - ADAM_B1 ** ADAM_STEP
    c2 = 1.0 - ADAM_B2 ** ADAM_STEP

    def body(*refs):
        w_ref, m_ref, v_ref = refs[:3]
        part_refs = refs[3:3 + n_lists]
        g_out, d_out, m_out, v_out = refs[-4:]
        g = None
        for p_ref in part_refs:
            for k in range(p_ref.shape[0]):
                term = p_ref[k].astype(F32)
                g = term if g is None else g + term
        m_new = ADAM_B1 * m_ref[...] + (1.0 - ADAM_B1) * g
        v_new = ADAM_B2 * v_ref[...] + (1.0 - ADAM_B2) * (g * g)
        g_out[...] = g
        m_out[...] = m_new
        v_out[...] = v_new
        d_out[...] = -ADAM_LR * ((m_new / c1) / (jnp.sqrt(v_new / c2) + ADAM_EPS) + ADAM_WD * w_ref[...])

    spec = pl.BlockSpec((tr, cols), lambda i: (i, 0))
    out = jax.ShapeDtypeStruct((rows, cols), F32)
    part_specs = [pl.BlockSpec((p.shape[0], tr, cols), lambda i: (0, i, 0)) for p in parts]
    part_bytes = sum(p.shape[0] * p.dtype.itemsize for p in parts)
    extra = [] if after is None else [after]
    return pl.pallas_call(
        body, name=name, grid=(rows // tr,),
        in_specs=[spec, spec, spec] + part_specs + [ANY] * len(extra),
        out_specs=[spec] * 4, out_shape=[out] * 4,
        compiler_params=_params(2 * tr * cols * (7 * 4 + part_bytes) + 8 * tr * cols * 4, 1),
    )(_in_hbm(w), _in_hbm(m), _in_hbm(v), *[_in_hbm(p) for p in parts], *extra)


def _mix_w_in_grad(u, dproj, ride=None):
    return _weight_grad(
        u, dproj, N_DEV,
        lambda tt: pl.BlockSpec((tt, D_MODEL), lambda b, t: (t, 0)),
        lambda tt: pl.BlockSpec((tt, MIX_SHARD), lambda b, t: (t, b)),
        D_MODEL, MIX_SHARD, name="w_in_grad", ride=ride)


def kernel(x, norm_ffn1, ffn1_w_in, ffn1_w_out, norm_mix, w_in, gate_bias, pool_w, pool_scale, w_ret_up, w_pool_up, w_out, norm_ffn2, ffn2_w_in, ffn2_w_out, norm_final, loss_target, m_norm_ffn1, m_ffn1_w_in, m_ffn1_w_out, m_norm_mix, m_w_in, m_gate_bias, m_pool_w, m_pool_scale, m_w_ret_up, m_w_pool_up, m_w_out, m_norm_ffn2, m_ffn2_w_in, m_ffn2_w_out, m_norm_final, v_norm_ffn1, v_ffn1_w_in, v_ffn1_w_out, v_norm_mix, v_w_in, v_gate_bias, v_pool_w, v_pool_scale, v_w_ret_up, v_w_pool_up, v_w_out, v_norm_ffn2, v_ffn2_w_in, v_ffn2_w_out, v_norm_final):
    assert x.shape == (1, SEQ, D_MODEL) and ffn1_w_in.shape == (1, D_MODEL, FF_SHARD) and w_in.shape == (1, D_MODEL, MIX_SHARD)
    x2, target = x[0], loss_target[0]

    cos, sin = _rotary_tables()
    tables = _retention_tables()
    bf = lambda w: w[0].astype(BF16)
    bf_t = lambda w: jnp.swapaxes(w[0], 0, 1).astype(BF16)
    square = lambda w: w.reshape(D_MODEL, D_MODEL)

    win1, wout1, bias8 = _alone(_GatherRide([bf_t(ffn1_w_in), bf(ffn1_w_out), gate_bias[0]]), "ffn1_weights_all_gather")
    wout1 = wout1.reshape(N_FF_GROUPS, FF_SHARD, D_MODEL)
    bias = bias8.transpose(1, 0, 2).reshape(2, D_MODEL)

    (h1, gu1), (wmix8,) = _ffn_forward(x2, norm_ffn1, win1, wout1, "ffn1_fwd", ride=_GatherRide([bf(w_in)]))
    (u, qkvg, p, gates), (win2, pw8, wru, wpu, wo) = _mix_proj_forward(
        h1, norm_mix, wmix8, cos, sin,
        ride=_GatherRide([bf_t(ffn2_w_in), bf(pool_w), bf(w_ret_up), bf(w_pool_up), bf(w_out)]))
    (ret, o_ret), (wout2,) = _retention_forward(qkvg, tables, ride=_GatherRide([bf(ffn2_w_out)]))
    wru, wpu, wo = square(wru), square(wpu), square(wo)
    (h2, a_saved, b_saved), _ = _mix_tail_forward(p, gates, o_ret, h1, bias, pool_scale, pw8, wru, wpu, wo)
    wout2 = wout2.reshape(N_FF_GROUPS, FF_SHARD, D_MODEL)
    (dh3, gu2, loss_part, d_norm_final), _ = _ffn_forward(h2, norm_ffn2, win2, wout2, "ffn2_fwd_loss",
                                                          head=(target, norm_final.reshape(1, D_MODEL)))

    dh2, dgu2, act2, xn2, df2, d_norm_ffn2 = _ffn_backward(dh3, h2, norm_ffn2, gu2, win2, wout2, "ffn2_bwd")
    d_wout2, _ = _ffn_w_out_grad(act2, df2, 2)
    d_win2, (r_wout2,) = _ffn_w_in_grad(xn2, dgu2, 2, ride=_ScatterRide([d_wout2]))
    (dret, dgr, dgates, dpooled, d_wo, d_wru, d_wpu, d_pw, d_bias, d_scale), (r_win2,) = _mix_tail_backward(
        dh2, p, gates, o_ret, ret, qkvg, a_saved, b_saved, bias, pool_scale, pw8, wru, wpu, wo, ride=_ScatterRide([d_win2]))
    (dq,), _ = _retention_backward_q(qkvg, dret, tables)
    (dk, dv), _ = _retention_backward_kv(qkvg, dret, tables)
    (dh1, dproj, d_norm_mix), _ = _mix_proj_backward(dq, dk, dv, dgr, dpooled, dgates, cos, sin, h1, norm_mix, dh2, wmix8)
    d_wmix, (r_pw, r_wru, r_wpu, r_wo) = _mix_w_in_grad(u, dproj, ride=_ScatterRide([d_pw, d_wru, d_wpu, d_wo]))
    wmix_state, wmix_started = _scatter_start(d_wmix, "w_in_grad_exchange_start")
    grad_x, dgu1, act1, xn1, df1, d_norm_ffn1 = _ffn_backward(dh1, x2, norm_ffn1, gu1, win1, wout1, "ffn1_bwd", after=wmix_started)
    small_rows = jnp.concatenate(
        [d_norm_ffn1, d_norm_mix, d_scale, d_norm_ffn2, d_norm_final, d_bias, jnp.tile(loss_part, (1, D_MODEL // 128))],
        axis=0)
    d_win1, (small_all,) = _ffn_w_in_grad(xn1, dgu1, 1, ride=_GatherRide([small_rows]))
    win1_state, win1_started = _scatter_start(d_win1, "ffn1_w_in_grad_exchange_start")
    d_wout1, _ = _ffn_w_out_grad(act1, df1, 1, after=win1_started)
    wout1_state, started = _scatter_start(d_wout1, "ffn1_w_out_grad_exchange_start")
    zero_row = jnp.zeros((1, D_MODEL), F32)

    results = {}

    def update(nm, w, parts, m, v, after):
        if nm in ("ffn1_w_in", "ffn2_w_in"):
            flat, back = (lambda a: jnp.swapaxes(a[0], 0, 1)), (lambda o: jnp.swapaxes(o, 0, 1)[None])
        else:
            flat, back = (lambda a: a.reshape(-1, w.shape[-1])), (lambda o: o.reshape(w.shape))
        parts = [p.reshape(p.shape[:1] + flat(w).shape) for p in parts]
        outs = _adamw(flat(w), parts, flat(m), flat(v), name=f"adamw_{nm}", after=after)
        results[nm] = [back(o) for o in outs]
        return outs[0]

    done = update("w_in", w_in, _scatter_wait(wmix_state, started, "w_in_grad_exchange_wait"), m_w_in, v_w_in, None)
    for nm, w, parts, m, v in (
            ("ffn2_w_in", ffn2_w_in, r_win2, m_ffn2_w_in, v_ffn2_w_in),
            ("ffn2_w_out", ffn2_w_out, r_wout2, m_ffn2_w_out, v_ffn2_w_out), ("w_ret_up", w_ret_up, r_wru, m_w_ret_up, v_w_ret_up),
            ("w_pool_up", w_pool_up, r_wpu, m_w_pool_up, v_w_pool_up), ("w_out", w_out, r_wo, m_w_out, v_w_out),
            ("pool_w", pool_w, r_pw, m_pool_w, v_pool_w)):
        done = update(nm, w, [parts], m, v, done)
    done = update("ffn1_w_in", ffn1_w_in, _scatter_wait(win1_state, done, "ffn1_w_in_grad_exchange_wait"),
                  m_ffn1_w_in, v_ffn1_w_in, None)
    update("ffn1_w_out", ffn1_w_out, _scatter_wait(wout1_state, done, "ffn1_w_out_grad_exchange_wait"),
           m_ffn1_w_out, v_ffn1_w_out, None)

    my_id = _linear_id(*_my_position())
    bias_cols = gate_bias.shape[-1]
    pad = lambda a: jnp.pad(a[0], ((0, 0), (0, D_MODEL - bias_cols)))
    pack = lambda a, b, c, d, e, gb: jnp.concatenate([a, b, c, d, e.reshape(1, D_MODEL), pad(gb), zero_row], axis=0)
    d_bias_mine = lax.dynamic_slice_in_dim(small_all[:, 5:7], my_id * bias_cols, bias_cols, axis=2)
    g_small = jnp.concatenate([small_all[:, 0:5], jnp.pad(d_bias_mine, ((0, 0), (0, 0), (0, D_MODEL - bias_cols))),
                               small_all[:, 7:8]], axis=1)
    s_outs = _adamw(pack(norm_ffn1, norm_mix, pool_scale, norm_ffn2, norm_final, gate_bias), [g_small],
                    pack(m_norm_ffn1, m_norm_mix, m_pool_scale, m_norm_ffn2, m_norm_final, m_gate_bias),
                    pack(v_norm_ffn1, v_norm_mix, v_pool_scale, v_norm_ffn2, v_norm_final, v_gate_bias), name="adamw_small")
    loss = s_outs[0][7, 0]
    for row, nm in enumerate(["norm_ffn1", "norm_mix", "pool_scale", "norm_ffn2"]):
        results[nm] = [o[row:row + 1] for o in s_outs]
    results["norm_final"] = [o[4] for o in s_outs]
    results["gate_bias"] = [o[5:7, :bias_cols][None] for o in s_outs]

    order = ["norm_ffn1", "ffn1_w_in", "ffn1_w_out", "norm_mix", "w_in", "gate_bias", "pool_w", "pool_scale",
             "w_ret_up", "w_pool_up", "w_out", "norm_ffn2", "ffn2_w_in", "ffn2_w_out", "norm_final"]
    return (loss, grad_x[None], *[results[nm][0] for nm in order], *[results[nm][1] for nm in order],
            *[results[nm][2] for nm in order], *[results[nm][3] for nm in order])
```

```python
import numpy as np
import jax
import jax.numpy as jnp
from jax import lax
from jax.experimental import pallas as pl
from jax.experimental.pallas import tpu as pltpu

F32 = jnp.float32
BF16 = jnp.bfloat16

N_DEV = 8
D_MODEL = 1024
SEQ = 4096
D_FF = 2816
FF_SHARD = 2 * D_FF // N_DEV
N_FF_GROUPS = N_DEV // 2
HEADS = 4
HEAD_DIM = 256
ROT_HALF = HEAD_DIM // 2
CHUNK = 64
RET_BLOCK = 256
POOL_WINDOWS = (2, 4, 8, 16)
POOL_GROUP_DIM = 256
HALO = 16
MIX_SHARD = 7 * D_MODEL // N_DEV
N_SEG = 7
ROPE_BASE = 10000.0
NORM_EPS = 1e-6
FFN_RES_WEIGHT = 0.5
ADAM_LR, ADAM_B1, ADAM_B2, ADAM_EPS, ADAM_WD, ADAM_STEP = 0.001, 0.9, 0.999, 1e-08, 0.01, 10

TOKEN_TILE = 256
WIDE_TILE = 512
VMEM_CAP_V7X = 64 * 1024 * 1024
MESH = pl.DeviceIdType.MESH
ANY = pl.BlockSpec(memory_space=pl.ANY)


def _vmem_limit(estimate_bytes):
    return int(min(estimate_bytes * 5 // 4 + (6 << 20), VMEM_CAP_V7X - (4 << 20)))


def _params(estimate_bytes, n_grid):
    return pltpu.CompilerParams(dimension_semantics=("arbitrary",) * n_grid,
                                vmem_limit_bytes=_vmem_limit(estimate_bytes))


def _dot(a, b):
    return jnp.dot(a, b, preferred_element_type=F32)


def _dot_nt(a, b):
    return lax.dot_general(a, b, (((1,), (1,)), ((), ())), preferred_element_type=F32)


def _dot_tn(a, b):
    return lax.dot_general(a, b, (((0,), (0,)), ((), ())), preferred_element_type=F32)


def _sig(x):
    return 1.0 / (1.0 + jnp.exp(-x))


def _rms(x, g):
    r = lax.rsqrt(jnp.mean(x * x, axis=-1, keepdims=True) + NORM_EPS)
    xhat = x * r
    return xhat * g, xhat, r


def _rms_bwd(dyg, xhat, r):
    return r * (dyg - xhat * jnp.mean(dyg * xhat, axis=-1, keepdims=True))


def _row_spec(tile, width, col=0):
    return pl.BlockSpec((tile, width), lambda i, c=col: (i, c))


def _full_spec(shape):
    return pl.BlockSpec(shape, lambda *_: (0,) * len(shape))


def _rotary_tables():
    inv_freq = (np.float32(ROPE_BASE) ** (-np.arange(ROT_HALF, dtype=np.float32) / np.float32(ROT_HALF))).astype(np.float32)
    ang = (np.arange(SEQ, dtype=np.float32)[:, None] * inv_freq[None, :]).astype(np.float32)
    return jnp.asarray(np.cos(ang.astype(np.float64)), F32), jnp.asarray(np.sin(ang.astype(np.float64)), F32)


def _retention_tables():
    log_gamma = np.log(1.0 - 2.0 ** (-5.0 - np.arange(HEADS, dtype=np.float64)))
    n = np.arange(RET_BLOCK)
    diff = (n[:, None] - n[None, :]).astype(np.float64)
    same = (n[:, None] // CHUNK) == (n[None, :] // CHUNK)
    earlier = (n[None, :] // CHUNK) < (n[:, None] // CHUNK)
    expo = np.where(same, np.abs(diff), diff)
    mask = np.where(same | earlier, np.exp(log_gamma[:, None, None] * expo[None]), 0.0)
    qdec = np.exp(log_gamma[:, None] * (n[None, :] + 1.0))[:, :, None]
    kdec = np.exp(log_gamma[:, None] * (RET_BLOCK - 1.0 - n[None, :]))[:, :, None]
    cdec = np.exp(log_gamma * RET_BLOCK)[:, None, None]
    return (jnp.asarray(mask, F32), jnp.asarray(qdec, F32), jnp.asarray(kdec, F32), jnp.asarray(cdec, F32))


def _my_position():
    return lax.axis_index("x"), lax.axis_index("y"), lax.axis_index("c")


def _linear_id(px, py, pc):
    return 4 * px + 2 * py + pc


def _when(pred, fn):
    if isinstance(pred, bool):
        if pred:
            fn()
    else:
        pl.when(pred)(fn)


class _GatherRide:
    def __init__(self, shards):
        self.args = list(shards)
        n = self.n = len(shards)
        self.out_shape = [pltpu.HBM((N_DEV,) + s.shape, s.dtype) for s in shards]
        self.scratch = [pltpu.SemaphoreType.DMA((n, 7)), pltpu.SemaphoreType.DMA((n, 7)), pltpu.SemaphoreType.DMA((n,))]

    def _plan(self, src, out, sems):
        send_sems, recv_sems, local_sem = sems
        x, y, c = _my_position()
        me, sibling = (x, y, c), (x, y, 1 - c)
        chips = [(1 - x, y), (x, 1 - y), (1 - x, 1 - y)]

        def copy(t, k, block, to, from_src=False):
            rows = out[t].at[_linear_id(*block)]
            return pltpu.make_async_remote_copy(
                src_ref=src[t] if from_src else rows, dst_ref=rows,
                send_sem=send_sems.at[t, k], recv_sem=recv_sems.at[t, k],
                device_id=to, device_id_type=MESH)

        def relay(t):
            return copy(t, 3, (x ^ (1 - c), y ^ c, c), (x ^ c, y ^ (1 - c), c))

        local = [pltpu.make_async_copy(src[t], out[t].at[_linear_id(*me)], local_sem.at[t]) for t in range(self.n)]
        return copy, relay, local, me, sibling, chips, c

    def begin(self, first, src, out, sems):
        copy, relay, local, me, sibling, chips, c = self._plan(src, out, sems)

        def start():
            for cp in local:
                cp.start()
            for t in range(self.n):
                copy(t, 0, me, sibling, from_src=True).start()
                for j in range(2):
                    copy(t, 1 + j, me, (*chips[j], c), from_src=True).start()

        _when(first, start)

    def finish(self, mid, late, last, src, out, sems):
        copy, relay, local, me, sibling, chips, c = self._plan(src, out, sems)

        def pass_on():
            for t in range(self.n):
                for j in range(2):
                    copy(t, 1 + j, (*chips[j], c), me).wait_recv()
                relay(t).start()
                for j in range(2):
                    copy(t, 4 + j, (*chips[j], c), sibling).start()

        def pass_on_relayed():
            for t in range(self.n):
                copy(t, 3, (*chips[2], c), me).wait_recv()
                copy(t, 6, (*chips[2], c), sibling).start()

        def drain():
            for t in range(self.n):
                copy(t, 0, sibling, me).wait_recv()
                for j in range(3):
                    copy(t, 4 + j, (*chips[j], 1 - c), me).wait_recv()
            for t in range(self.n):
                copy(t, 0, me, sibling, from_src=True).wait_send()
                for j in range(2):
                    copy(t, 1 + j, me, (*chips[j], c), from_src=True).wait_send()
                relay(t).wait_send()
                for j in range(3):
                    copy(t, 4 + j, (*chips[j], c), sibling).wait_send()
            for cp in local:
                cp.wait()

        _when(mid, pass_on)
        _when(late, pass_on_relayed)
        _when(last, drain)


class _ScatterRide:
    def __init__(self, chip_sums):
        self.args = list(chip_sums)
        n = self.n = len(chip_sums)
        self.out_shape = [pltpu.HBM(p.shape, p.dtype) for p in chip_sums]
        self.scratch = [pltpu.SemaphoreType.DMA((n, 3)), pltpu.SemaphoreType.DMA((n, 3)), pltpu.SemaphoreType.DMA((n,))]

    def _plan(self, src, out, sems):
        send_sems, recv_sems, local_sem = sems
        x, y, c = _my_position()

        def peer(k):
            return (x ^ (k >> 1), y ^ (k & 1))

        copies = [pltpu.make_async_remote_copy(
            src_ref=src[t].at[2 * peer(k)[0] + peer(k)[1]], dst_ref=out[t].at[k],
            send_sem=send_sems.at[t, k - 1], recv_sem=recv_sems.at[t, k - 1],
            device_id=(*peer(k), c), device_id_type=MESH) for t in range(self.n) for k in range(1, N_DEV // 2)]
        local = [pltpu.make_async_copy(src[t].at[2 * x + y], out[t].at[0], local_sem.at[t]) for t in range(self.n)]
        return copies, local

    def begin(self, first, src, out, sems):
        copies, local = self._plan(src, out, sems)

        def start():
            for cp in local + copies:
                cp.start()

        _when(first, start)

    def finish(self, mid, late, last, src, out, sems):
        copies, local = self._plan(src, out, sems)

        def drain():
            for cp in copies:
                cp.wait_recv()
            for cp in copies:
                cp.wait_send()
            for cp in local:
                cp.wait()

        _when(last, drain)


def _in_hbm(a):
    return pltpu.with_memory_space_constraint(a, pltpu.HBM)


def _call(body, *, name, grid, in_specs, out_specs, out_shape, scratch_shapes, vmem_bytes, args, ride=None, after=None):
    n_in, n_out, n_s = len(in_specs), len(out_specs), len(scratch_shapes)
    params = _params(vmem_bytes, len(grid))
    args = [_in_hbm(a) for a in args]
    out_shape = [pltpu.HBM(s.shape, s.dtype) for s in out_shape]
    if ride is None:
        if after is not None:
            def ordered_body(*refs):
                body(*refs[:n_in], *refs[n_in + 1:])
            outs = pl.pallas_call(ordered_body, name=name, grid=grid, in_specs=list(in_specs) + [ANY], out_specs=out_specs,
                                  out_shape=out_shape, scratch_shapes=scratch_shapes, compiler_params=params)(*args, after)
            return list(outs), []
        outs = pl.pallas_call(body, name=name, grid=grid, in_specs=in_specs, out_specs=out_specs, out_shape=out_shape,
                              scratch_shapes=scratch_shapes, compiler_params=params)(*args)
        return list(outs), []
    total = int(np.prod(grid))

    def riding_body(*refs):
        a = n_in
        b = a + ride.n
        c = b + n_out
        d = c + ride.n
        e = d + n_s
        step = pl.program_id(0)
        for axis in range(1, len(grid)):
            step = step * grid[axis] + pl.program_id(axis)
        ride.begin(step == 0, refs[a:b], refs[c:d], refs[e:])
        body(*refs[:a], *refs[b:c], *refs[d:e])
        ride.finish(step == total // 2, step == (3 * total) // 4, step == total - 1, refs[a:b], refs[c:d], refs[e:])

    outs = pl.pallas_call(
        riding_body, name=name, grid=grid, in_specs=list(in_specs) + [ANY] * ride.n,
        out_specs=list(out_specs) + [ANY] * ride.n, out_shape=list(out_shape) + ride.out_shape,
        scratch_shapes=list(scratch_shapes) + ride.scratch, compiler_params=params)(*args, *[_in_hbm(a) for a in ride.args])
    return list(outs[:n_out]), list(outs[n_out:])


def _alone(ride, name):
    def body(*refs):
        src, out, sems = refs[:ride.n], refs[ride.n:2 * ride.n], refs[2 * ride.n:]
        ride.begin(True, src, out, sems)
        ride.finish(True, True, True, src, out, sems)

    return list(pl.pallas_call(body, name=name, out_shape=ride.out_shape, in_specs=[ANY] * ride.n,
                               out_specs=[ANY] * ride.n, scratch_shapes=ride.scratch)(*[_in_hbm(a) for a in ride.args]))


def _scatter_copies(src, land, send_sems, recv_sems):
    x, y, c = _my_position()
    copies = []
    for k in range(1, N_DEV // 2):
        px, py = x ^ (k >> 1), y ^ (k & 1)
        copies.append(pltpu.make_async_remote_copy(
            src_ref=src.at[2 * px + py], dst_ref=land.at[k - 1], send_sem=send_sems.at[k - 1], recv_sem=recv_sems.at[k - 1],
            device_id=(px, py, c), device_id_type=MESH))
    return copies


def _scatter_start(chip_sums, name):
    n_peers = N_DEV // 2 - 1
    land_shape = (n_peers,) + chip_sums.shape[1:]
    hbm = pl.BlockSpec(memory_space=pltpu.HBM)
    sem = pl.BlockSpec(memory_space=pltpu.SEMAPHORE)

    def body(src_ref, land_ref, send_sems, recv_sems, src_thru, land_thru, token):
        for cp in _scatter_copies(src_ref, land_ref, send_sems, recv_sems):
            cp.start()
        token[...] = jnp.zeros_like(token)

    send_sems, recv_sems, src_thru, land_thru, token = pl.pallas_call(
        body, name=name,
        out_shape=(pltpu.SemaphoreType.DMA((n_peers,)), pltpu.SemaphoreType.DMA((n_peers,)),
                   pltpu.HBM(chip_sums.shape, chip_sums.dtype), pltpu.HBM(land_shape, chip_sums.dtype),
                   jax.ShapeDtypeStruct((8, 128), F32)),
        in_specs=(hbm, hbm), out_specs=(sem, sem, hbm, hbm, pl.BlockSpec(memory_space=pltpu.VMEM)),
        input_output_aliases={0: 2, 1: 3},
        compiler_params=pltpu.CompilerParams(has_side_effects=pltpu.SideEffectType.DATAFLOW_SIDE_EFFECTING),
    )(_in_hbm(chip_sums), _in_hbm(lax.empty(land_shape, chip_sums.dtype)))
    return (send_sems, recv_sems, src_thru, land_thru), token


def _scatter_wait(state, after, name):
    send_sems, recv_sems, src_thru, land_thru = state
    hbm = pl.BlockSpec(memory_space=pltpu.HBM)
    sem = pl.BlockSpec(memory_space=pltpu.SEMAPHORE)

    def body(src_ref, land_ref, send_sems, recv_sems, after_ref, src_out, land_out):
        for cp in _scatter_copies(src_ref, land_ref, send_sems, recv_sems):
            cp.wait_send()
            cp.wait_recv()

    src_done, land_done = pl.pallas_call(
        body, name=name,
        out_shape=(pltpu.HBM(src_thru.shape, src_thru.dtype), pltpu.HBM(land_thru.shape, land_thru.dtype)),
        in_specs=(hbm, hbm, sem, sem, ANY), out_specs=(hbm, hbm), input_output_aliases={0: 0, 1: 1},
        compiler_params=pltpu.CompilerParams(has_side_effects=pltpu.SideEffectType.DATAFLOW_SIDE_EFFECTING),
    )(src_thru, land_thru, send_sems, recv_sems, after)
    x, y, _ = _my_position()
    return lax.dynamic_slice_in_dim(src_done, 2 * x + y, 1, axis=0), land_done


def _load_ffn_weights(win_hbm, wout_hbm, win, wout, sem):
    a = pltpu.make_async_copy(win_hbm, win, sem.at[0])
    b = pltpu.make_async_copy(wout_hbm, wout, sem.at[1])
    a.start()
    b.start()
    a.wait()
    b.wait()


def _ffn_forward(h_in, gain, win8, wout, name, head=None, ride=None):
    tm, nt = WIDE_TILE, SEQ // WIDE_TILE

    def body(*refs):
        if head is None:
            x_ref, g_ref, win_hbm, wout_hbm, out_ref, gu_ref, win, wout, sem = refs
        else:
            x_ref, g_ref, win_hbm, wout_hbm, tgt_ref, gf_ref, out_ref, gu_ref, loss_ref, dgf_ref, win, wout, sem = refs
        i = pl.program_id(0)

        @pl.when(i == 0)
        def _():
            _load_ffn_weights(win_hbm, wout_hbm, win, wout, sem)
            if head is not None:
                loss_ref[...] = jnp.zeros_like(loss_ref)
                dgf_ref[...] = jnp.zeros_like(dgf_ref)

        x = x_ref[...]
        xn, _, _ = _rms(x, g_ref[...])
        xb = xn.astype(BF16)
        acc = jnp.zeros((tm, D_MODEL), F32)
        for j in range(N_FF_GROUPS):
            gate = _dot_nt(xb, win[j])
            up = _dot_nt(xb, win[j + N_FF_GROUPS])
            gu_ref[j] = gate.astype(BF16)
            gu_ref[j + N_FF_GROUPS] = up.astype(BF16)
            act = gate * _sig(gate) * up
            acc = acc + _dot(act.astype(BF16), wout[j])
        h = x + FFN_RES_WEIGHT * acc
        if head is None:
            out_ref[...] = h
        else:
            gf = gf_ref[...]
            y, hhat, r = _rms(h, gf)
            err = y - tgt_ref[...]
            loss_ref[...] += jnp.full(loss_ref.shape, 0.5 / D_MODEL * jnp.sum(err * err), F32)
            dy = err * (1.0 / D_MODEL)
            dgf_ref[...] += jnp.sum(dy * hhat, axis=0, keepdims=True)
            out_ref[...] = _rms_bwd(dy * gf, hhat, r)

    weights = 2 * D_MODEL * 2 * D_FF + 2 * D_FF * D_MODEL
    tiles = 2 * (2 * 4 * tm * D_MODEL + 2 * tm * 2 * D_FF) + (2 * 4 * tm * D_MODEL if head else 0)
    in_specs = [_row_spec(tm, D_MODEL), _full_spec((1, D_MODEL)), ANY, ANY]
    out_shape = [jax.ShapeDtypeStruct((SEQ, D_MODEL), F32), jax.ShapeDtypeStruct((N_DEV, SEQ, FF_SHARD), BF16)]
    out_specs = [_row_spec(tm, D_MODEL), pl.BlockSpec((N_DEV, tm, FF_SHARD), lambda i: (0, i, 0))]
    args = [h_in, gain, win8, wout]
    if head is not None:
        in_specs += [_row_spec(tm, D_MODEL), _full_spec((1, D_MODEL))]
        out_shape += [jax.ShapeDtypeStruct((1, 128), F32), jax.ShapeDtypeStruct((1, D_MODEL), F32)]
        out_specs += [_full_spec((1, 128)), _full_spec((1, D_MODEL))]
        args += list(head)
    return _call(
        body, name=name, grid=(nt,), in_specs=in_specs, out_specs=out_specs, out_shape=out_shape,
        scratch_shapes=[pltpu.VMEM((N_DEV, FF_SHARD, D_MODEL), BF16), pltpu.VMEM((N_FF_GROUPS, FF_SHARD, D_MODEL), BF16),
                        pltpu.SemaphoreType.DMA((2,))],
        vmem_bytes=weights + tiles + 16 * tm * FF_SHARD * 4, args=args, ride=ride)


def _ffn_backward(dh_out, h_in, gain, gu, win8, wout, name, after=None):
    tm, nt = TOKEN_TILE, SEQ // TOKEN_TILE

    def body(dh_ref, x_ref, g_ref, gu_ref, win_hbm, wout_hbm,
             dhin_ref, dgu_ref, act_ref, xn_ref, df_ref, dg_ref, win, wout, sem):
        i = pl.program_id(0)

        @pl.when(i == 0)
        def _():
            _load_ffn_weights(win_hbm, wout_hbm, win, wout, sem)
            dg_ref[...] = jnp.zeros_like(dg_ref)

        dh = dh_ref[...]
        g = g_ref[...]
        xn, xhat, r = _rms(x_ref[...], g)
        df = (FFN_RES_WEIGHT * dh).astype(BF16)
        dxn = jnp.zeros((tm, D_MODEL), F32)
        for j in range(N_FF_GROUPS):
            gate = gu_ref[j].astype(F32)
            up = gu_ref[j + N_FF_GROUPS].astype(F32)
            dact = _dot_nt(df, wout[j])
            s = _sig(gate)
            silu = gate * s
            dgate = (dact * up * (s * (1.0 + gate * (1.0 - s)))).astype(BF16)
            dup = (dact * silu).astype(BF16)
            act_ref[j] = (silu * up).astype(BF16)
            dgu_ref[j] = dgate
            dgu_ref[j + N_FF_GROUPS] = dup
            dxn = dxn + _dot(dgate, win[j]) + _dot(dup, win[j + N_FF_GROUPS])
        dg_ref[...] += jnp.sum(dxn * xhat, axis=0, keepdims=True)
        dhin_ref[...] = dh + _rms_bwd(dxn * g, xhat, r)
        xn_ref[...] = xn.astype(BF16)
        df_ref[...] = df

    weights = 2 * D_MODEL * 2 * D_FF + 2 * D_FF * D_MODEL
    tiles = 2 * (3 * 4 * tm * D_MODEL + 2 * tm * (2 * 2 * D_FF + D_FF) + 2 * 2 * tm * D_MODEL)
    gu_spec = pl.BlockSpec((N_DEV, tm, FF_SHARD), lambda i: (0, i, 0))
    return _call(
        body, name=name, grid=(nt,),
        in_specs=[_row_spec(tm, D_MODEL), _row_spec(tm, D_MODEL), _full_spec((1, D_MODEL)), gu_spec, ANY, ANY],
        out_specs=[_row_spec(tm, D_MODEL), gu_spec, pl.BlockSpec((N_FF_GROUPS, tm, FF_SHARD), lambda i: (0, i, 0)),
                   _row_spec(tm, D_MODEL), _row_spec(tm, D_MODEL), _full_spec((1, D_MODEL))],
        out_shape=[jax.ShapeDtypeStruct((SEQ, D_MODEL), F32), jax.ShapeDtypeStruct((N_DEV, SEQ, FF_SHARD), BF16),
                   jax.ShapeDtypeStruct((N_FF_GROUPS, SEQ, FF_SHARD), BF16), jax.ShapeDtypeStruct((SEQ, D_MODEL), BF16),
                   jax.ShapeDtypeStruct((SEQ, D_MODEL), BF16), jax.ShapeDtypeStruct((1, D_MODEL), F32)],
        scratch_shapes=[pltpu.VMEM((N_DEV, FF_SHARD, D_MODEL), BF16), pltpu.VMEM((N_FF_GROUPS, FF_SHARD, D_MODEL), BF16),
                        pltpu.SemaphoreType.DMA((2,))],
        vmem_bytes=weights + tiles + 20 * tm * FF_SHARD * 4, args=[dh_out, h_in, gain, gu, win8, wout], after=after)[0]


def _to_sibling(src, dst, send_sem, recv_sem):
    x, y, c = _my_position()
    return pltpu.make_async_remote_copy(src_ref=src, dst_ref=dst, send_sem=send_sem, recv_sem=recv_sem,
                                        device_id=(x, y, 1 - c), device_id_type=MESH)


def _weight_grad(x, g, n_out, x_spec, g_spec, k_dim, n_dim, name, halves=False, tt=2048, ride=None, after=None):
    nt = SEQ // tt
    n_chips = N_DEV // 2
    rows = k_dim // 2 if halves else k_dim

    def body(x_ref, g_ref, out_ref, acc, sendbuf, recvbuf, send_sems, recv_sems):
        b, t = pl.program_id(0), pl.program_id(1)
        c = lax.axis_index("c")

        def push(q):
            return _to_sibling(sendbuf.at[q], recvbuf.at[q], send_sems.at[q], recv_sems.at[q])

        @pl.when(t == 0)
        def _():
            acc[...] = jnp.zeros_like(acc)

        acc[...] += _dot_tn(x_ref[...], g_ref[...])

        @pl.when(t == nt - 1)
        def _():
            if halves:
                for mine, other in ((0, 1), (1, 0)):
                    @pl.when(c == mine)
                    def _():
                        out_ref[b] = acc[pl.ds(mine * rows, rows), :].astype(BF16)
                        sendbuf[b] = acc[pl.ds(other * rows, rows), :].astype(BF16)
                push(b).start()
            else:
                q = b // 2

                @pl.when(b % 2 == c)
                def _():
                    out_ref[q] = acc[...].astype(BF16)

                @pl.when(b % 2 != c)
                def _():
                    sendbuf[q] = acc[...].astype(BF16)
                    push(q).start()

        @pl.when((b == n_out - 1) & (t == nt - 1))
        def _():
            for q in range(n_chips):
                push(q).wait_recv()
                out_ref[q] = (out_ref[q].astype(F32) + recvbuf[q].astype(F32)).astype(BF16)
            for q in range(n_chips):
                push(q).wait_send()

    piece = (n_chips, rows, n_dim)
    outs, ride_outs = _call(
        body, name=name, grid=(n_out, nt), in_specs=[x_spec(tt), g_spec(tt)],
        out_specs=[pl.BlockSpec(piece, lambda b, t: (0, 0, 0))],
        out_shape=[jax.ShapeDtypeStruct(piece, BF16)],
        scratch_shapes=[pltpu.VMEM((k_dim, n_dim), F32), pltpu.VMEM(piece, BF16), pltpu.VMEM(piece, BF16),
                        pltpu.SemaphoreType.DMA((n_chips,)), pltpu.SemaphoreType.DMA((n_chips,))],
        vmem_bytes=2 * 2 * tt * (k_dim + n_dim) + 8 * k_dim * n_dim + 4 * 2 * n_chips * rows * n_dim, args=[x, g], ride=ride,
        after=after)
    return outs[0], ride_outs


def _ffn_w_out_grad(act, df, tag, ride=None, after=None):
    return _weight_grad(
        act, df, N_FF_GROUPS,
        lambda tt: pl.BlockSpec((None, tt, FF_SHARD), lambda b, t: (b, t, 0)),
        lambda tt: pl.BlockSpec((tt, D_MODEL), lambda b, t: (t, 0)),
        FF_SHARD, D_MODEL, name=f"ffn{tag}_w_out_grad", halves=True, ride=ride, after=after)


def _ffn_w_in_grad(xn, dgu, tag, ride=None):
    return _weight_grad(
        dgu, xn, N_DEV,
        lambda tt: pl.BlockSpec((None, tt, FF_SHARD), lambda b, t: (b, t, 0)),
        lambda tt: pl.BlockSpec((tt, D_MODEL), lambda b, t: (t, 0)),
        FF_SHARD, D_MODEL, name=f"ffn{tag}_w_in_grad", ride=ride)


def _load_mix_weight(wmix_hbm, wmix, sem):
    copies = [pltpu.make_async_copy(wmix_hbm.at[d], wmix.at[:, pl.ds(d * MIX_SHARD, MIX_SHARD)], sem.at[d])
              for d in range(N_DEV)]
    for cp in copies:
        cp.start()
    for cp in copies:
        cp.wait()


def _load_pool_weight(pw_hbm, pw, sem):
    rows = POOL_GROUP_DIM // N_DEV
    copies = [pltpu.make_async_copy(pw_hbm.at[d], pw.at[:, pl.ds(d * rows, rows), :], sem.at[d]) for d in range(N_DEV)]
    for cp in copies:
        cp.start()
    for cp in copies:
        cp.wait()


def _rotate(x1, x2, cos, sin):
    return x1 * cos - x2 * sin, x1 * sin + x2 * cos


def _mix_proj_forward(h1, gain, wmix8, cos, sin, ride=None):
    tm, nt = WIDE_TILE, SEQ // WIDE_TILE
    k_scale = HEAD_DIM ** -0.5

    def body(h_ref, g_ref, wmix_hbm, cos_ref, sin_ref, u_ref, qkvg_ref, p_ref, gates_ref, wmix, sem):
        @pl.when(pl.program_id(0) == 0)
        def _():
            _load_mix_weight(wmix_hbm, wmix, sem)

        u = _rms(h_ref[...], g_ref[...])[0].astype(BF16)
        u_ref[...] = u
        cos_t, sin_t = cos_ref[...], sin_ref[...]
        for seg in range(N_SEG):
            pr = _dot(u, wmix[:, pl.ds(seg * D_MODEL, D_MODEL)])
            if seg < 2:
                scale = 1.0 if seg == 0 else k_scale
                for hd in range(HEADS):
                    lo = hd * HEAD_DIM
                    o1, o2 = _rotate(pr[:, lo:lo + ROT_HALF], pr[:, lo + ROT_HALF:lo + HEAD_DIM], cos_t, sin_t)
                    qkvg_ref[:, pl.ds(seg * D_MODEL + lo, ROT_HALF)] = (o1 * scale).astype(BF16)
                    qkvg_ref[:, pl.ds(seg * D_MODEL + lo + ROT_HALF, ROT_HALF)] = (o2 * scale).astype(BF16)
            elif seg < 4:
                qkvg_ref[:, pl.ds(seg * D_MODEL, D_MODEL)] = pr.astype(BF16)
            elif seg == 4:
                p_ref[...] = pr
            else:
                gates_ref[:, pl.ds((seg - 5) * D_MODEL, D_MODEL)] = pr.astype(BF16)

    est = 2 * D_MODEL * N_SEG * D_MODEL + 2 * tm * (4 * D_MODEL + 2 * D_MODEL + 2 * 4 * D_MODEL + 4 * D_MODEL + 2 * 2 * D_MODEL)
    return _call(
        body, name="mix_proj_fwd", grid=(nt,),
        in_specs=[_row_spec(tm, D_MODEL), _full_spec((1, D_MODEL)), ANY, _row_spec(tm, ROT_HALF), _row_spec(tm, ROT_HALF)],
        out_specs=[_row_spec(tm, D_MODEL), _row_spec(tm, 4 * D_MODEL), _row_spec(tm, D_MODEL), _row_spec(tm, 2 * D_MODEL)],
        out_shape=[jax.ShapeDtypeStruct((SEQ, D_MODEL), BF16), jax.ShapeDtypeStruct((SEQ, 4 * D_MODEL), BF16),
                   jax.ShapeDtypeStruct((SEQ, D_MODEL), F32), jax.ShapeDtypeStruct((SEQ, 2 * D_MODEL), BF16)],
        scratch_shapes=[pltpu.VMEM((D_MODEL, N_SEG * D_MODEL), BF16), pltpu.SemaphoreType.DMA((N_DEV,))],
        vmem_bytes=est + 8 * tm * D_MODEL * 4, args=[h1, gain, wmix8, cos, sin], ride=ride)


def _seg_block_spec(seg, reverse=False):
    nb = SEQ // RET_BLOCK
    if reverse:
        return pl.BlockSpec((RET_BLOCK, D_MODEL), lambda i, s=seg: (nb - 1 - i, s))
    return pl.BlockSpec((RET_BLOCK, D_MODEL), lambda i, s=seg: (i, s))


def _table_specs():
    return [_full_spec((HEADS, RET_BLOCK, RET_BLOCK)), _full_spec((HEADS, RET_BLOCK, 1)),
            _full_spec((HEADS, RET_BLOCK, 1)), _full_spec((HEADS, 1, 1))]


def _head_cols(h):
    return pl.ds(h * HEAD_DIM, HEAD_DIM)


def _retention_forward(qkvg, tables, ride=None):
    nb = SEQ // RET_BLOCK

    def body(q_ref, k_ref, v_ref, gr_ref, mask_ref, qdec_ref, kdec_ref, cdec_ref, ret_ref, o_ref, state):
        @pl.when(pl.program_id(0) == 0)
        def _():
            state[...] = jnp.zeros_like(state)

        for h in range(HEADS):
            cols = _head_cols(h)
            q, k, v = q_ref[:, cols], k_ref[:, cols], v_ref[:, cols]
            scores = _dot_nt(q, k) * mask_ref[h]
            inner = _dot(scores.astype(BF16), v)
            cross = _dot((q.astype(F32) * qdec_ref[h]).astype(BF16), state[h].astype(BF16))
            ret = inner + cross
            state[h] = state[h] * cdec_ref[h] + _dot_tn((k.astype(F32) * kdec_ref[h]).astype(BF16), v)
            ret_ref[:, cols] = ret
            retn = ret * lax.rsqrt(jnp.mean(ret * ret, axis=-1, keepdims=True) + NORM_EPS)
            gr = gr_ref[:, cols].astype(F32)
            o_ref[:, cols] = (retn * (gr * _sig(gr))).astype(BF16)

    return _call(
        body, name="retention_fwd", grid=(nb,),
        in_specs=[_seg_block_spec(0), _seg_block_spec(1), _seg_block_spec(2), _seg_block_spec(3)] + _table_specs(),
        out_specs=[_row_spec(RET_BLOCK, D_MODEL)] * 2,
        out_shape=[jax.ShapeDtypeStruct((SEQ, D_MODEL), F32), jax.ShapeDtypeStruct((SEQ, D_MODEL), BF16)],
        scratch_shapes=[pltpu.VMEM((HEADS, HEAD_DIM, HEAD_DIM), F32)],
        vmem_bytes=24 * RET_BLOCK * D_MODEL * 4, args=[qkvg, qkvg, qkvg, qkvg, *tables], ride=ride)


def _retention_backward_q(qkvg, dret, tables, ride=None):
    nb = SEQ // RET_BLOCK

    def body(k_ref, v_ref, do_ref, mask_ref, qdec_ref, kdec_ref, cdec_ref, dq_ref, state):
        @pl.when(pl.program_id(0) == 0)
        def _():
            state[...] = jnp.zeros_like(state)

        for h in range(HEADS):
            cols = _head_cols(h)
            k, v, do = k_ref[:, cols], v_ref[:, cols], do_ref[:, cols]
            dscores = _dot_nt(do, v) * mask_ref[h]
            dq_ref[:, cols] = _dot(dscores.astype(BF16), k) + _dot_nt(do, state[h].astype(BF16)) * qdec_ref[h]
            state[h] = state[h] * cdec_ref[h] + _dot_tn((k.astype(F32) * kdec_ref[h]).astype(BF16), v)

    return _call(
        body, name="retention_bwd_q", grid=(nb,),
        in_specs=[_seg_block_spec(1), _seg_block_spec(2), _row_spec(RET_BLOCK, D_MODEL)] + _table_specs(),
        out_specs=[_row_spec(RET_BLOCK, D_MODEL)],
        out_shape=[jax.ShapeDtypeStruct((SEQ, D_MODEL), F32)],
        scratch_shapes=[pltpu.VMEM((HEADS, HEAD_DIM, HEAD_DIM), F32)],
        vmem_bytes=24 * RET_BLOCK * D_MODEL * 4, args=[qkvg, qkvg, dret, *tables], ride=ride)


def _retention_backward_kv(qkvg, dret, tables, ride=None):
    nb = SEQ // RET_BLOCK

    def body(q_ref, k_ref, v_ref, do_ref, mask_ref, qdec_ref, kdec_ref, cdec_ref, dk_ref, dv_ref, gstate):
        @pl.when(pl.program_id(0) == 0)
        def _():
            gstate[...] = jnp.zeros_like(gstate)

        for h in range(HEADS):
            cols = _head_cols(h)
            q, k, v, do = q_ref[:, cols], k_ref[:, cols], v_ref[:, cols], do_ref[:, cols]
            mask = mask_ref[h]
            scores = (_dot_nt(q, k) * mask).astype(BF16)
            dscores = (_dot_nt(do, v) * mask).astype(BF16)
            gs = gstate[h].astype(BF16)
            dk_ref[:, cols] = _dot_tn(dscores, q) + _dot_nt(v, gs) * kdec_ref[h]
            dv_ref[:, cols] = _dot_tn(scores, do) + _dot((k.astype(F32) * kdec_ref[h]).astype(BF16), gs)
            gstate[h] = gstate[h] * cdec_ref[h] + _dot_tn((q.astype(F32) * qdec_ref[h]).astype(BF16), do)

    rev = lambda: pl.BlockSpec((RET_BLOCK, D_MODEL), lambda i: (nb - 1 - i, 0))
    return _call(
        body, name="retention_bwd_kv", grid=(nb,),
        in_specs=[_seg_block_spec(0, True), _seg_block_spec(1, True), _seg_block_spec(2, True), rev()] + _table_specs(),
        out_specs=[rev(), rev()],
        out_shape=[jax.ShapeDtypeStruct((SEQ, D_MODEL), F32)] * 2,
        scratch_shapes=[pltpu.VMEM((HEADS, HEAD_DIM, HEAD_DIM), F32)],
        vmem_bytes=32 * RET_BLOCK * D_MODEL * 4, args=[qkvg, qkvg, qkvg, dret, *tables], ride=ride)


def _pooled(p_ext, first_row):
    rows = p_ext.shape[0]
    t = first_row + lax.broadcasted_iota(jnp.int32, (rows - HALO, 1), 0)
    outs = []
    for g, w in enumerate(POOL_WINDOWS):
        e = p_ext[:, g * POOL_GROUP_DIM:(g + 1) * POOL_GROUP_DIM]
        s, span = e, 1
        while span < w:
            s = s + pltpu.roll(s, span, 0)
            span *= 2
        count = jnp.minimum(t + 1, w).astype(F32)
        outs.append(s[HALO:] / count - e[HALO:])
    return outs


def _pooled_transpose(d_ext, first_row):
    rows = d_ext.shape[0]
    t = first_row + lax.broadcasted_iota(jnp.int32, (rows, 1), 0)
    outs = []
    for g, w in enumerate(POOL_WINDOWS):
        d = d_ext[:, g * POOL_GROUP_DIM:(g + 1) * POOL_GROUP_DIM]
        e = jnp.where(t < SEQ, d / jnp.minimum(t + 1, w).astype(F32), 0.0)
        s, span = e, 1
        while span < w:
            s = s + pltpu.roll(s, rows - span, 0)
            span *= 2
        outs.append(s[:rows - HALO] - d[:rows - HALO])
    return outs


def _mix_tail_specs(tm):
    halo_blocks = tm // HALO
    return [
        _row_spec(tm, D_MODEL),
        pl.BlockSpec((HALO, D_MODEL), lambda i: (jnp.maximum(i * halo_blocks - 1, 0), 0)),
        _row_spec(tm, 2 * D_MODEL),
        _row_spec(tm, D_MODEL),
        _full_spec((2, D_MODEL)), _full_spec((1, D_MODEL)), ANY,
        _full_spec((D_MODEL, D_MODEL)), _full_spec((D_MODEL, D_MODEL)), _full_spec((D_MODEL, D_MODEL)),
    ]


def _mix_tail_compute(i, tm, p_ref, halo_ref, gates_ref, oret_ref, bias_ref, scale_ref, pw, wru_ref, wpu_ref, saved=None):
    halo = jnp.where(i > 0, halo_ref[...], 0.0)
    pooled = _pooled(jnp.concatenate([halo, p_ref[...]], axis=0), i * tm)
    pooled = [x.astype(BF16) for x in pooled]
    mixed = jnp.concatenate([_dot(pooled[g], pw[g]) for g in range(len(POOL_WINDOWS))], axis=-1)
    pool_out = (mixed * scale_ref[...]).astype(BF16)
    o_ret = oret_ref[...]
    if saved is None:
        a = _dot(o_ret, wru_ref[...])
        b = _dot(pool_out, wpu_ref[...])
    else:
        a, b = saved[0][...].astype(F32), saved[1][...].astype(F32)
    z = gates_ref[...].astype(F32)
    g0 = _sig(z[:, :D_MODEL] + bias_ref[0:1, :])
    g1 = _sig(z[:, D_MODEL:] + bias_ref[1:2, :])
    merged = (g0 * a + g1 * b).astype(BF16)
    return pooled, mixed, pool_out, o_ret, a, b, g0, g1, merged


def _mix_tail_forward(p, gates, o_ret, h1, bias, scale, pw8, wru, wpu, wo, ride=None):
    tm, nt = TOKEN_TILE, SEQ // TOKEN_TILE

    def body(p_ref, halo_ref, gates_ref, oret_ref, bias_ref, scale_ref, pw_hbm, wru_ref, wpu_ref, wo_ref, h1_ref,
             h2_ref, a_ref, b_ref, pw, sem):
        i = pl.program_id(0)

        @pl.when(i == 0)
        def _():
            _load_pool_weight(pw_hbm, pw, sem)

        out = _mix_tail_compute(i, tm, p_ref, halo_ref, gates_ref, oret_ref, bias_ref, scale_ref, pw, wru_ref, wpu_ref)
        a_ref[...] = out[4].astype(BF16)
        b_ref[...] = out[5].astype(BF16)
        h2_ref[...] = h1_ref[...] + _dot(out[-1], wo_ref[...])

    est = 3 * 2 * 2 * D_MODEL * D_MODEL + 2 * tm * D_MODEL * (4 + 4 + 2 + 4 + 4) + 16 * tm * D_MODEL * 4
    return _call(
        body, name="mix_tail_fwd", grid=(nt,),
        in_specs=_mix_tail_specs(tm) + [_row_spec(tm, D_MODEL)],
        out_specs=[_row_spec(tm, D_MODEL)] * 3,
        out_shape=[jax.ShapeDtypeStruct((SEQ, D_MODEL), F32)] + [jax.ShapeDtypeStruct((SEQ, D_MODEL), BF16)] * 2,
        scratch_shapes=[pltpu.VMEM((len(POOL_WINDOWS), POOL_GROUP_DIM, POOL_GROUP_DIM), BF16), pltpu.SemaphoreType.DMA((N_DEV,))],
        vmem_bytes=est, args=[p, p, gates, o_ret, bias, scale, pw8, wru, wpu, wo, h1], ride=ride)


def _mix_tail_backward(dh2, p, gates, o_ret, ret, qkvg, a_saved, b_saved, bias, scale, pw8, wru, wpu, wo, ride=None):
    tm, nt = TOKEN_TILE, SEQ // TOKEN_TILE
    n_groups = len(POOL_WINDOWS)
    rows_per_dev = POOL_GROUP_DIM // N_DEV

    def body(p_ref, halo_ref, gates_ref, oret_ref, bias_ref, scale_ref, pw_hbm, wru_ref, wpu_ref, wo_ref,
             dh2_ref, ret_ref, gr_ref, a_ref, b_ref,
             dret_ref, dgr_ref, dgates_ref, dpooled_ref, dwo_ref, dwru_ref, dwpu_ref, dpw_ref, dbias_ref, dscale_ref,
             pw, sem, acc_wo, acc_wru, acc_wpu, acc_pw, send_sq, recv_sq, send_pw, recv_pw, send_sems, recv_sems):
        i = pl.program_id(0)

        @pl.when(i == 0)
        def _():
            _load_pool_weight(pw_hbm, pw, sem)
            for ref in (acc_wo, acc_wru, acc_wpu, acc_pw, dbias_ref, dscale_ref):
                ref[...] = jnp.zeros_like(ref)

        pooled, mixed, pool_out, o_ret, a, b, g0, g1, merged = _mix_tail_compute(
            i, tm, p_ref, halo_ref, gates_ref, oret_ref, bias_ref, scale_ref, pw, wru_ref, wpu_ref, saved=(a_ref, b_ref))
        dh2 = dh2_ref[...].astype(BF16)
        dm = _dot_nt(dh2, wo_ref[...])
        acc_wo[...] += _dot_tn(merged, dh2)
        da = (dm * g0).astype(BF16)
        db = (dm * g1).astype(BF16)
        dz0 = dm * a * g0 * (1.0 - g0)
        dz1 = dm * b * g1 * (1.0 - g1)
        dbias_ref[0:1, :] += jnp.sum(dz0, axis=0, keepdims=True)
        dbias_ref[1:2, :] += jnp.sum(dz1, axis=0, keepdims=True)
        dgates_ref[:, pl.ds(0, D_MODEL)] = dz0.astype(BF16)
        dgates_ref[:, pl.ds(D_MODEL, D_MODEL)] = dz1.astype(BF16)
        acc_wru[...] += _dot_tn(o_ret, da)
        acc_wpu[...] += _dot_tn(pool_out, db)
        d_oret = _dot_nt(da, wru_ref[...])
        d_pool_out = _dot_nt(db, wpu_ref[...])
        dscale_ref[...] += jnp.sum(d_pool_out * mixed, axis=0, keepdims=True)
        dmixed = (d_pool_out * scale_ref[...]).astype(BF16)
        for g in range(n_groups):
            dmg = dmixed[:, g * POOL_GROUP_DIM:(g + 1) * POOL_GROUP_DIM]
            acc_pw[g] += _dot_tn(pooled[g], dmg)
            dpooled_ref[:, pl.ds(g * POOL_GROUP_DIM, POOL_GROUP_DIM)] = _dot_nt(dmg, pw[g])
        gr = gr_ref[...].astype(F32)
        s = _sig(gr)
        silu = gr * s
        for hd in range(HEADS):
            cols = slice(hd * HEAD_DIM, (hd + 1) * HEAD_DIM)
            r_h = ret_ref[:, cols]
            rr = lax.rsqrt(jnp.mean(r_h * r_h, axis=-1, keepdims=True) + NORM_EPS)
            rhat = r_h * rr
            do_h = d_oret[:, cols]
            dgr_ref[:, cols] = (do_h * rhat * (s[:, cols] * (1.0 + gr[:, cols] * (1.0 - s[:, cols])))).astype(BF16)
            dret_ref[:, cols] = _rms_bwd(do_h * silu[:, cols], rhat, rr).astype(BF16)

        @pl.when(i == nt - 1)
        def _():
            c = lax.axis_index("c")
            rows = D_MODEL // N_DEV
            squares = ((acc_wo, dwo_ref), (acc_wru, dwru_ref), (acc_wpu, dwpu_ref))
            for q in range(n_chips):
                own = pl.multiple_of((2 * q + c) * rows, rows)
                other = pl.multiple_of((2 * q + 1 - c) * rows, rows)
                for t, (acc, out) in enumerate(squares):
                    out[q] = acc[pl.ds(own, rows), :].astype(BF16)
                    send_sq[t, q] = acc[pl.ds(other, rows), :].astype(BF16)
                own_pw = pl.multiple_of((2 * q + c) * rows_per_dev, rows_per_dev)
                other_pw = pl.multiple_of((2 * q + 1 - c) * rows_per_dev, rows_per_dev)
                dpw_ref[q] = acc_pw[:, pl.ds(own_pw, rows_per_dev), :].astype(BF16)
                send_pw[q] = acc_pw[:, pl.ds(other_pw, rows_per_dev), :].astype(BF16)
            pushes = [_to_sibling(send_sq, recv_sq, send_sems.at[0], recv_sems.at[0]),
                      _to_sibling(send_pw, recv_pw, send_sems.at[1], recv_sems.at[1])]
            for cp in pushes:
                cp.start()
            for cp in pushes:
                cp.wait_recv()
            for t, (acc, out) in enumerate(squares):
                out[...] = (out[...].astype(F32) + recv_sq[t].astype(F32)).astype(BF16)
            dpw_ref[...] = (dpw_ref[...].astype(F32) + recv_pw[...].astype(F32)).astype(BF16)
            for cp in pushes:
                cp.wait_send()

    n_chips = N_DEV // 2
    sq = (n_chips, D_MODEL // N_DEV, D_MODEL)
    pw_shape = (n_chips, n_groups, rows_per_dev, POOL_GROUP_DIM)
    est = (3 * 2 * 2 * D_MODEL * D_MODEL + 3 * 4 * D_MODEL * D_MODEL + 3 * 2 * 2 * D_MODEL * D_MODEL
           + 2 * tm * D_MODEL * (4 + 4 + 2 + 4 + 4 + 2 + 2 + 2 + 4 + 4) + 24 * tm * D_MODEL * 4)
    return _call(
        body, name="mix_tail_bwd", grid=(nt,),
        in_specs=_mix_tail_specs(tm) + [_row_spec(tm, D_MODEL), _row_spec(tm, D_MODEL), _row_spec(tm, D_MODEL, 3),
                                        _row_spec(tm, D_MODEL), _row_spec(tm, D_MODEL)],
        out_specs=[_row_spec(tm, D_MODEL), _row_spec(tm, D_MODEL), _row_spec(tm, 2 * D_MODEL), _row_spec(tm, D_MODEL),
                   _full_spec(sq), _full_spec(sq), _full_spec(sq), _full_spec(pw_shape),
                   _full_spec((2, D_MODEL)), _full_spec((1, D_MODEL))],
        out_shape=[jax.ShapeDtypeStruct((SEQ, D_MODEL), BF16), jax.ShapeDtypeStruct((SEQ, D_MODEL), BF16),
                   jax.ShapeDtypeStruct((SEQ, 2 * D_MODEL), BF16), jax.ShapeDtypeStruct((SEQ, D_MODEL), F32),
                   jax.ShapeDtypeStruct(sq, BF16), jax.ShapeDtypeStruct(sq, BF16), jax.ShapeDtypeStruct(sq, BF16),
                   jax.ShapeDtypeStruct(pw_shape, BF16),
                   jax.ShapeDtypeStruct((2, D_MODEL), F32), jax.ShapeDtypeStruct((1, D_MODEL), F32)],
        scratch_shapes=[pltpu.VMEM((n_groups, POOL_GROUP_DIM, POOL_GROUP_DIM), BF16), pltpu.SemaphoreType.DMA((N_DEV,)),
                        pltpu.VMEM((D_MODEL, D_MODEL), F32), pltpu.VMEM((D_MODEL, D_MODEL), F32),
                        pltpu.VMEM((D_MODEL, D_MODEL), F32), pltpu.VMEM((n_groups, POOL_GROUP_DIM, POOL_GROUP_DIM), F32),
                        pltpu.VMEM((3,) + sq, BF16), pltpu.VMEM((3,) + sq, BF16), pltpu.VMEM(pw_shape, BF16),
                        pltpu.VMEM(pw_shape, BF16), pltpu.SemaphoreType.DMA((2,)), pltpu.SemaphoreType.DMA((2,))],
        vmem_bytes=est, args=[p, p, gates, o_ret, bias, scale, pw8, wru, wpu, wo, dh2, ret, qkvg, a_saved, b_saved], ride=ride)


def _mix_proj_backward(dq, dk, dv, dgr, dpooled, dgates, cos, sin, h1, gain, dh2, wmix8, ride=None):
    tm, nt = TOKEN_TILE, SEQ // TOKEN_TILE
    halo_blocks = tm // HALO
    last_halo = SEQ // HALO - 1
    k_scale = HEAD_DIM ** -0.5

    def body(dq_ref, dk_ref, dv_ref, dgr_ref, dpool_ref, dhalo_ref, dgates_ref, cos_ref, sin_ref, h1_ref, g_ref,
             dh2_ref, wmix_hbm, dh1_ref, dproj_ref, dg_ref, wmix, sem):
        i = pl.program_id(0)

        @pl.when(i == 0)
        def _():
            _load_mix_weight(wmix_hbm, wmix, sem)
            dg_ref[...] = jnp.zeros_like(dg_ref)

        cos_t, sin_t = cos_ref[...], sin_ref[...]
        for seg, ref, scale in ((0, dq_ref, 1.0), (1, dk_ref, k_scale)):
            for hd in range(HEADS):
                lo = hd * HEAD_DIM
                d1, d2 = ref[:, lo:lo + ROT_HALF], ref[:, lo + ROT_HALF:lo + HEAD_DIM]
                dproj_ref[:, pl.ds(seg * D_MODEL + lo, ROT_HALF)] = ((d1 * cos_t + d2 * sin_t) * scale).astype(BF16)
                dproj_ref[:, pl.ds(seg * D_MODEL + lo + ROT_HALF, ROT_HALF)] = ((d2 * cos_t - d1 * sin_t) * scale).astype(BF16)
        dproj_ref[:, pl.ds(2 * D_MODEL, D_MODEL)] = dv_ref[...].astype(BF16)
        dproj_ref[:, pl.ds(3 * D_MODEL, D_MODEL)] = dgr_ref[...]
        dp = _pooled_transpose(jnp.concatenate([dpool_ref[...], dhalo_ref[...]], axis=0), i * tm)
        for g in range(len(POOL_WINDOWS)):
            dproj_ref[:, pl.ds(4 * D_MODEL + g * POOL_GROUP_DIM, POOL_GROUP_DIM)] = dp[g].astype(BF16)
        dproj_ref[:, pl.ds(5 * D_MODEL, 2 * D_MODEL)] = dgates_ref[...]
        du = jnp.zeros((tm, D_MODEL), F32)
        for seg in range(N_SEG):
            cols = pl.ds(seg * D_MODEL, D_MODEL)
            du = du + _dot_nt(dproj_ref[:, cols], wmix[:, cols])
        g = g_ref[...]
        _, xhat, r = _rms(h1_ref[...], g)
        dg_ref[...] += jnp.sum(du * xhat, axis=0, keepdims=True)
        dh1_ref[...] = dh2_ref[...] + _rms_bwd(du * g, xhat, r)

    est = 2 * D_MODEL * N_SEG * D_MODEL + 2 * tm * D_MODEL * (3 * 4 + 2 + 4 + 4 + 4 + 4 + 4 + 14) + 12 * tm * D_MODEL * 4
    return _call(
        body, name="mix_proj_bwd", grid=(nt,),
        in_specs=[_row_spec(tm, D_MODEL), _row_spec(tm, D_MODEL), _row_spec(tm, D_MODEL), _row_spec(tm, D_MODEL),
                  _row_spec(tm, D_MODEL),
                  pl.BlockSpec((HALO, D_MODEL), lambda i: (jnp.minimum((i + 1) * halo_blocks, last_halo), 0)),
                  _row_spec(tm, 2 * D_MODEL), _row_spec(tm, ROT_HALF), _row_spec(tm, ROT_HALF),
                  _row_spec(tm, D_MODEL), _full_spec((1, D_MODEL)), _row_spec(tm, D_MODEL), ANY],
        out_specs=[_row_spec(tm, D_MODEL), _row_spec(tm, N_SEG * D_MODEL), _full_spec((1, D_MODEL))],
        out_shape=[jax.ShapeDtypeStruct((SEQ, D_MODEL), F32), jax.ShapeDtypeStruct((SEQ, N_SEG * D_MODEL), BF16),
                   jax.ShapeDtypeStruct((1, D_MODEL), F32)],
        scratch_shapes=[pltpu.VMEM((D_MODEL, N_SEG * D_MODEL), BF16), pltpu.SemaphoreType.DMA((N_DEV,))],
        vmem_bytes=est, args=[dq, dk, dv, dgr, dpooled, dpooled, dgates, cos, sin, h1, gain, dh2, wmix8], ride=ride)


def _adamw(w, parts, m, v, name, after=None):
    rows, cols = w.shape
    n_lists = len(parts)
    tr = max([t for t in range(16, 257, 16) if rows % t == 0], default=rows)
    c1 = 1.0 - ADAM_B1 ** ADAM_STEP
    c2 = 1.0 - ADAM_B2 ** ADAM_STEP

    def body(*refs):
        w_ref, m_ref, v_ref = refs[:3]
        part_refs = refs[3:3 + n_lists]
        g_out, d_out, m_out, v_out = refs[-4:]
        g = None
        for p_ref in part_refs:
            for k in range(p_ref.shape[0]):
                term = p_ref[k].astype(F32)
                g = term if g is None else g + term
        m_new = ADAM_B1 * m_ref[...] + (1.0 - ADAM_B1) * g
        v_new = ADAM_B2 * v_ref[...] + (1.0 - ADAM_B2) * (g * g)
        g_out[...] = g
        m_out[...] = m_new
        v_out[...] = v_new
        d_out[...] = -ADAM_LR * ((m_new / c1) / (jnp.sqrt(v_new / c2) + ADAM_EPS) + ADAM_WD * w_ref[...])

    spec = pl.BlockSpec((tr, cols), lambda i: (i, 0))
    out = jax.ShapeDtypeStruct((rows, cols), F32)
    part_specs = [pl.BlockSpec((p.shape[0], tr, cols), lambda i: (0, i, 0)) for p in parts]
    part_bytes = sum(p.shape[0] * p.dtype.itemsize for p in parts)
    extra = [] if after is None else [after]
    return pl.pallas_call(
        body, name=name, grid=(rows // tr,),
        in_specs=[spec, spec, spec] + part_specs + [ANY] * len(extra),
        out_specs=[spec] * 4, out_shape=[out] * 4,
        compiler_params=_params(2 * tr * cols * (7 * 4 + part_bytes) + 8 * tr * cols * 4, 1),
    )(_in_hbm(w), _in_hbm(m), _in_hbm(v), *[_in_hbm(p) for p in parts], *extra)


def _mix_w_in_grad(u, dproj, ride=None):
    return _weight_grad(
        u, dproj, N_DEV,
        lambda tt: pl.BlockSpec((tt, D_MODEL), lambda b, t: (t, 0)),
        lambda tt: pl.BlockSpec((tt, MIX_SHARD), lambda b, t: (t, b)),
        D_MODEL, MIX_SHARD, name="w_in_grad", ride=ride)


def kernel(x, norm_ffn1, ffn1_w_in, ffn1_w_out, norm_mix, w_in, gate_bias, pool_w, pool_scale, w_ret_up, w_pool_up, w_out, norm_ffn2, ffn2_w_in, ffn2_w_out, norm_final, loss_target, m_norm_ffn1, m_ffn1_w_in, m_ffn1_w_out, m_norm_mix, m_w_in, m_gate_bias, m_pool_w, m_pool_scale, m_w_ret_up, m_w_pool_up, m_w_out, m_norm_ffn2, m_ffn2_w_in, m_ffn2_w_out, m_norm_final, v_norm_ffn1, v_ffn1_w_in, v_ffn1_w_out, v_norm_mix, v_w_in, v_gate_bias, v_pool_w, v_pool_scale, v_w_ret_up, v_w_pool_up, v_w_out, v_norm_ffn2, v_ffn2_w_in, v_ffn2_w_out, v_norm_final):
    assert x.shape == (1, SEQ, D_MODEL) and ffn1_w_in.shape == (1, D_MODEL, FF_SHARD) and w_in.shape == (1, D_MODEL, MIX_SHARD)
    x2, target = x[0], loss_target[0]

    cos, sin = _rotary_tables()
    tables = _retention_tables()
    bf = lambda w: w[0].astype(BF16)
    bf_t = lambda w: jnp.swapaxes(w[0], 0, 1).astype(BF16)
    square = lambda w: w.reshape(D_MODEL, D_MODEL)

    win1, wout1, bias8 = _alone(_GatherRide([bf_t(ffn1_w_in), bf(ffn1_w_out), gate_bias[0]]), "ffn1_weights_all_gather")
    wout1 = wout1.reshape(N_FF_GROUPS, FF_SHARD, D_MODEL)
    bias = bias8.transpose(1, 0, 2).reshape(2, D_MODEL)

    (h1, gu1), (wmix8, wout2) = _ffn_forward(x2, norm_ffn1, win1, wout1, "ffn1_fwd",
                                             ride=_GatherRide([bf(w_in), bf(ffn2_w_out)]))
    (u, qkvg, p, gates), (win2, pw8, wru, wpu, wo) = _mix_proj_forward(
        h1, norm_mix, wmix8, cos, sin,
        ride=_GatherRide([bf_t(ffn2_w_in), bf(pool_w), bf(w_ret_up), bf(w_pool_up), bf(w_out)]))
    (ret, o_ret), _ = _retention_forward(qkvg, tables)
    wru, wpu, wo = square(wru), square(wpu), square(wo)
    (h2, a_saved, b_saved), _ = _mix_tail_forward(p, gates, o_ret, h1, bias, pool_scale, pw8, wru, wpu, wo)
    wout2 = wout2.reshape(N_FF_GROUPS, FF_SHARD, D_MODEL)
    (dh3, gu2, loss_part, d_norm_final), _ = _ffn_forward(h2, norm_ffn2, win2, wout2, "ffn2_fwd_loss",
                                                          head=(target, norm_final.reshape(1, D_MODEL)))

    dh2, dgu2, act2, xn2, df2, d_norm_ffn2 = _ffn_backward(dh3, h2, norm_ffn2, gu2, win2, wout2, "ffn2_bwd")
    d_wout2, _ = _ffn_w_out_grad(act2, df2, 2)
    d_win2, (r_wout2,) = _ffn_w_in_grad(xn2, dgu2, 2, ride=_ScatterRide([d_wout2]))
    (dret, dgr, dgates, dpooled, d_wo, d_wru, d_wpu, d_pw, d_bias, d_scale), (r_win2,) = _mix_tail_backward(
        dh2, p, gates, o_ret, ret, qkvg, a_saved, b_saved, bias, pool_scale, pw8, wru, wpu, wo, ride=_ScatterRide([d_win2]))
    (dq,), _ = _retention_backward_q(qkvg, dret, tables)
    (dk, dv), _ = _retention_backward_kv(qkvg, dret, tables)
    (dh1, dproj, d_norm_mix), _ = _mix_proj_backward(dq, dk, dv, dgr, dpooled, dgates, cos, sin, h1, norm_mix, dh2, wmix8)
    d_wmix, (r_pw, r_wru, r_wpu, r_wo) = _mix_w_in_grad(u, dproj, ride=_ScatterRide([d_pw, d_wru, d_wpu, d_wo]))
    wmix_state, wmix_started = _scatter_start(d_wmix, "w_in_grad_exchange_start")
    grad_x, dgu1, act1, xn1, df1, d_norm_ffn1 = _ffn_backward(dh1, x2, norm_ffn1, gu1, win1, wout1, "ffn1_bwd", after=wmix_started)
    small_rows = jnp.concatenate(
        [d_norm_ffn1, d_norm_mix, d_scale, d_norm_ffn2, d_norm_final, d_bias, jnp.tile(loss_part, (1, D_MODEL // 128))],
        axis=0)
    d_win1, (small_all,) = _ffn_w_in_grad(xn1, dgu1, 1, ride=_GatherRide([small_rows]))
    win1_state, win1_started = _scatter_start(d_win1, "ffn1_w_in_grad_exchange_start")
    d_wout1, _ = _ffn_w_out_grad(act1, df1, 1, after=win1_started)
    wout1_state, started = _scatter_start(d_wout1, "ffn1_w_out_grad_exchange_start")
    zero_row = jnp.zeros((1, D_MODEL), F32)

    results = {}

    def update(nm, w, parts, m, v, after):
        if nm in ("ffn1_w_in", "ffn2_w_in"):
            flat, back = (lambda a: jnp.swapaxes(a[0], 0, 1)), (lambda o: jnp.swapaxes(o, 0, 1)[None])
        else:
            flat, back = (lambda a: a.reshape(-1, w.shape[-1])), (lambda o: o.reshape(w.shape))
        parts = [p.reshape(p.shape[:1] + flat(w).shape) for p in parts]
        outs = _adamw(flat(w), parts, flat(m), flat(v), name=f"adamw_{nm}", after=after)
        results[nm] = [back(o) for o in outs]
        return outs[0]

    done = update("w_in", w_in, _scatter_wait(wmix_state, started, "w_in_grad_exchange_wait"), m_w_in, v_w_in, None)
    for nm, w, parts, m, v in (
            ("ffn2_w_in", ffn2_w_in, r_win2, m_ffn2_w_in, v_ffn2_w_in),
            ("ffn2_w_out", ffn2_w_out, r_wout2, m_ffn2_w_out, v_ffn2_w_out), ("w_ret_up", w_ret_up, r_wru, m_w_ret_up, v_w_ret_up),
            ("w_pool_up", w_pool_up, r_wpu, m_w_pool_up, v_w_pool_up), ("w_out", w_out, r_wo, m_w_out, v_w_out),
            ("pool_w", pool_w, r_pw, m_pool_w, v_pool_w)):
        done = update(nm, w, [parts], m, v, done)
    done = update("ffn1_w_in", ffn1_w_in, _scatter_wait(win1_state, done, "ffn1_w_in_grad_exchange_wait"),
                  m_ffn1_w_in, v_ffn1_w_in, None)
    update("ffn1_w_out", ffn1_w_out, _scatter_wait(wout1_state, done, "ffn1_w_out_grad_exchange_wait"),
           m_ffn1_w_out, v_ffn1_w_out, None)

    my_id = _linear_id(*_my_position())
    bias_cols = gate_bias.shape[-1]
    pad = lambda a: jnp.pad(a[0], ((0, 0), (0, D_MODEL - bias_cols)))
    pack = lambda a, b, c, d, e, gb: jnp.concatenate([a, b, c, d, e.reshape(1, D_MODEL), pad(gb), zero_row], axis=0)
    d_bias_mine = lax.dynamic_slice_in_dim(small_all[:, 5:7], my_id * bias_cols, bias_cols, axis=2)
    g_small = jnp.concatenate([small_all[:, 0:5], jnp.pad(d_bias_mine, ((0, 0), (0, 0), (0, D_MODEL - bias_cols))),
                               small_all[:, 7:8]], axis=1)
    s_outs = _adamw(pack(norm_ffn1, norm_mix, pool_scale, norm_ffn2, norm_final, gate_bias), [g_small],
                    pack(m_norm_ffn1, m_norm_mix, m_pool_scale, m_norm_ffn2, m_norm_final, m_gate_bias),
                    pack(v_norm_ffn1, v_norm_mix, v_pool_scale, v_norm_ffn2, v_norm_final, v_gate_bias), name="adamw_small")
    loss = s_outs[0][7, 0]
    for row, nm in enumerate(["norm_ffn1", "norm_mix", "pool_scale", "norm_ffn2"]):
        results[nm] = [o[row:row + 1] for o in s_outs]
    results["norm_final"] = [o[4] for o in s_outs]
    results["gate_bias"] = [o[5:7, :bias_cols][None] for o in s_outs]

    order = ["norm_ffn1", "ffn1_w_in", "ffn1_w_out", "norm_mix", "w_in", "gate_bias", "pool_w", "pool_scale",
             "w_ret_up", "w_pool_up", "w_out", "norm_ffn2", "ffn2_w_in", "ffn2_w_out", "norm_final"]
    return (loss, grad_x[None], *[results[nm][0] for nm in order], *[results[nm][1] for nm in order],
            *[results[nm][2] for nm in order], *[results[nm][3] for nm in order])
```
